```python
import jax, jax.numpy as jnp
from jax import lax
import numpy as np

D_MODEL = 1024
BATCH = 8
SEQ = 2048
DEPTH = 4

D_MIX = D_MODEL
SSD_HEAD_DIM = 64
SSD_WIDTH = D_MIX // 2
SSD_HEADS = SSD_WIDTH // SSD_HEAD_DIM
SSD_GROUPS = 2
SSD_HEADS_PER_GROUP = SSD_HEADS // SSD_GROUPS
D_STATE = 128
CONV_WIDTH = 4
CHUNK = 128
CONV_DIM = SSD_WIDTH + 2 * SSD_GROUPS * D_STATE
SB_HEAD_DIM = 64
SB_WIDTH = D_MIX // 4
SB_HEADS = SB_WIDTH // SB_HEAD_DIM
Q_BLOCK = 128
POOL_WINDOWS = (2, 4, 8, 16)
POOL_GROUPS = len(POOL_WINDOWS)
POOL_WIDTH = D_MIX - SSD_WIDTH - SB_WIDTH
POOL_GROUP_DIM = POOL_WIDTH // POOL_GROUPS
D_IN_PROJ = SSD_WIDTH + CONV_DIM + SSD_HEADS + 3 * SB_WIDTH + POOL_WIDTH
SPLIT_POINTS = (SSD_WIDTH,
                SSD_WIDTH + CONV_DIM,
                SSD_WIDTH + CONV_DIM + SSD_HEADS,
                SSD_WIDTH + CONV_DIM + SSD_HEADS + 3 * SB_WIDTH)
D_FF = -(-8 * D_MODEL // (3 * 256)) * 256
EPS = 1e-6

kernel_name = 'hybrid_ssd_stickbreak_pool_trunk'


def rmsnorm(x, w):
    xf = x.astype(jnp.float32)
    y = xf * lax.rsqrt(jnp.mean(xf * xf, axis=-1, keepdims=True) + EPS)
    return (y * w.astype(jnp.float32)).astype(x.dtype)


def causal_depthwise_conv(u, w, b):
    out = lax.conv_general_dilated(
        u, w[:, None, :].astype(u.dtype), window_strides=(1,),
        padding=[(CONV_WIDTH - 1, 0)],
        dimension_numbers=('NWC', 'WIO', 'NWC'),
        feature_group_count=u.shape[-1])
    return out + b.astype(u.dtype)


def ssd_mixer(z, xbc, dt_raw, conv_w, conv_b, dt_bias, a_log, d_skip, norm_w):
    f32 = jnp.float32
    bsz, seqlen, _ = xbc.shape
    nc = seqlen // CHUNK
    G, K, P, N, L = SSD_GROUPS, SSD_HEADS_PER_GROUP, SSD_HEAD_DIM, D_STATE, CHUNK
    xbc = jax.nn.silu(causal_depthwise_conv(xbc, conv_w, conv_b))
    xs, bm, cm = jnp.split(xbc, [SSD_WIDTH, SSD_WIDTH + SSD_GROUPS * D_STATE], axis=-1)
    dt = jax.nn.softplus(dt_raw.astype(f32) + dt_bias.astype(f32))
    a = -jnp.exp(a_log.astype(f32))
    xh = xs.astype(f32).reshape(bsz, nc, L, G, K, P)
    dtc = dt.reshape(bsz, nc, L, G, K)
    X = xh * dtc[..., None]
    Bc = bm.astype(f32).reshape(bsz, nc, L, G, N)
    Cc = cm.astype(f32).reshape(bsz, nc, L, G, N)
    dA = jnp.transpose(dtc * a.reshape(G, K), (0, 1, 3, 4, 2))
    acum = jnp.cumsum(dA, axis=-1)
    causal = jnp.tril(jnp.ones((L, L), dtype=bool))
    seg = jnp.where(causal, acum[..., :, None] - acum[..., None, :], -jnp.inf)
    decay_in = jnp.exp(seg)
    cb = jnp.einsum('bclgn,bcsgn->bcgls', Cc, Bc)
    y_diag = jnp.einsum('bcgls,bcgkls,bcsgkp->bclgkp', cb, decay_in, X)
    decay_to_end = jnp.exp(acum[..., -1:] - acum)
    chunk_states = jnp.einsum('bclgn,bcgkl,bclgkp->bcgkpn', Bc, decay_to_end, X)
    chunk_decay = jnp.exp(acum[..., -1])

    def step(state, inp):
        st, dec = inp
        return state * dec[..., None, None] + st, state

    init = jnp.zeros((bsz, G, K, P, N), f32)
    _, prev = lax.scan(step, init, (jnp.moveaxis(chunk_states, 1, 0), jnp.moveaxis(chunk_decay, 1, 0)))
    prev = jnp.moveaxis(prev, 0, 1)
    y_off = jnp.einsum('bclgn,bcgkpn,bcgkl->bclgkp', Cc, prev, jnp.exp(acum))
    y = y_diag + y_off + xh * d_skip.astype(f32).reshape(G, K)[:, :, None]
    y = y.reshape(bsz, seqlen, SSD_WIDTH) * jax.nn.silu(z.astype(f32))
    return rmsnorm(y, norm_w).astype(z.dtype)


def stick_breaking_attention(q, k, v):
    f32 = jnp.float32
    bsz, seqlen, _ = q.shape
    qh = q.astype(f32).reshape(bsz, seqlen, SB_HEADS, SB_HEAD_DIM) * (SB_HEAD_DIM ** -0.5)
    kh = k.astype(f32).reshape(bsz, seqlen, SB_HEADS, SB_HEAD_DIM)
    vh = v.astype(f32).reshape(bsz, seqlen, SB_HEADS, SB_HEAD_DIM)
    outs = []
    for start in range(0, seqlen, Q_BLOCK):
        end = start + Q_BLOCK
        logits = jnp.einsum('bthd,bshd->bhts', qh[:, start:end], kh[:, :end])
        before = jnp.arange(end)[None, :] < jnp.arange(start, end)[:, None]
        log_keep = jnp.where(before, jax.nn.log_sigmoid(-logits), 0.0)
        log_keep_after = lax.cumsum(log_keep, axis=3, reverse=True) - log_keep
        w = jnp.where(before, jnp.exp(jax.nn.log_sigmoid(logits) + log_keep_after), 0.0)
        outs.append(jnp.einsum('bhts,bshd->bthd', w, vh[:, :end]))
    o = jnp.concatenate(outs, axis=1)
    return o.reshape(bsz, seqlen, SB_WIDTH).astype(q.dtype)


def multiscale_pool(p, pool_w, pool_b, pool_scale):
    f32 = jnp.float32
    bsz, seqlen, _ = p.shape
    groups = p.astype(f32).reshape(bsz, seqlen, POOL_GROUPS, POOL_GROUP_DIM)
    csum = jnp.pad(jnp.cumsum(groups, axis=1), ((0, 0), (1, 0), (0, 0), (0, 0)))
    pos = jnp.arange(seqlen)
    pooled = []
    for gi, win in enumerate(POOL_WINDOWS):
        cg = csum[:, :, gi]
        lo = jnp.maximum(pos + 1 - win, 0)
        wsum = cg[:, 1:] - cg[:, lo]
        count = jnp.minimum(pos + 1, win).astype(f32)
        pooled.append(wsum / count[None, :, None] - groups[:, :, gi])
    pooled = jnp.stack(pooled, axis=2)
    mixed = jnp.einsum('bsgc,gcd->bsgd', pooled, pool_w.astype(f32)) + pool_b.astype(f32)
    return (mixed.reshape(bsz, seqlen, POOL_WIDTH) * pool_scale.astype(f32)).astype(p.dtype)


def _fwd_setup_inputs(seed: int = 0) -> dict:
    key = jax.random.key(seed)
    ks = jax.random.split(key, 20)
    f32 = jnp.float32
    nrm = lambda k, shape, s: jax.random.normal(k, shape, f32) * s
    dt0 = jnp.exp(jax.random.uniform(ks[5], (DEPTH, SSD_HEADS), f32, np.log(1e-3), np.log(1e-1)))
    return {
        'x': jax.random.normal(ks[0], (BATCH, SEQ, D_MODEL), f32),
        'norm1_w': 1.0 + nrm(ks[1], (DEPTH, D_MODEL), 0.02),
        'w_in': nrm(ks[2], (DEPTH, D_MODEL, D_IN_PROJ), D_MODEL ** -0.5),
        'conv_w': nrm(ks[3], (DEPTH, CONV_WIDTH, CONV_DIM), CONV_WIDTH ** -0.5),
        'conv_b': nrm(ks[4], (DEPTH, CONV_DIM), 0.02),
        'dt_bias': dt0 + jnp.log(-jnp.expm1(-dt0)),
        'a_log': jnp.log(jax.random.uniform(ks[6], (DEPTH, SSD_HEADS), f32, 1.0, 16.0)),
        'd_skip': 1.0 + nrm(ks[7], (DEPTH, SSD_HEADS), 0.1),
        'ssd_norm_w': 1.0 + nrm(ks[8], (DEPTH, SSD_WIDTH), 0.02),
        'pool_w': nrm(ks[9], (DEPTH, POOL_GROUPS, POOL_GROUP_DIM, POOL_GROUP_DIM), POOL_GROUP_DIM ** -0.5),
        'pool_b': nrm(ks[10], (DEPTH, POOL_GROUPS, POOL_GROUP_DIM), 0.02),
        'pool_scale': 1.0 + nrm(ks[11], (DEPTH, POOL_WIDTH), 0.1),
        'w_out': nrm(ks[12], (DEPTH, D_MIX, D_MODEL), D_MIX ** -0.5),
        'norm2_w': 1.0 + nrm(ks[13], (DEPTH, D_MODEL), 0.02),
        'w_gate': nrm(ks[14], (DEPTH, D_MODEL, D_FF), D_MODEL ** -0.5),
        'w_up': nrm(ks[15], (DEPTH, D_MODEL, D_FF), D_MODEL ** -0.5),
        'w_down': nrm(ks[16], (DEPTH, D_FF, D_MODEL), D_FF ** -0.5),
        'final_norm_w': 1.0 + nrm(ks[17], (D_MODEL,), 0.02),
    }


def _fwd_reference(x, norm1_w, w_in, conv_w, conv_b, dt_bias, a_log, d_skip, ssd_norm_w,
              pool_w, pool_b, pool_scale, w_out, norm2_w, w_gate, w_up, w_down, final_norm_w):
    for layer in range(DEPTH):
        h = rmsnorm(x, norm1_w[layer])
        proj = h @ w_in[layer]
        z, xbc, dt_raw, qkv, p = jnp.split(proj, list(SPLIT_POINTS), axis=-1)
        q, k, v = jnp.split(qkv, 3, axis=-1)
        y_ssd = ssd_mixer(z, xbc, dt_raw, conv_w[layer], conv_b[layer], dt_bias[layer],
                          a_log[layer], d_skip[layer], ssd_norm_w[layer])
        y_sb = stick_breaking_attention(q, k, v)
        y_pool = multiscale_pool(p, pool_w[layer], pool_b[layer], pool_scale[layer])
        y = jnp.concatenate([y_ssd, y_sb, y_pool], axis=-1)
        x = x + y @ w_out[layer]
        h = rmsnorm(x, norm2_w[layer])
        x = x + (jax.nn.silu(h @ w_gate[layer]) * (h @ w_up[layer])) @ w_down[layer]
    return rmsnorm(x, final_norm_w)


import jax as _jax
import jax.numpy as _jnp

TWIN_FORMAT = 'train_step'
FWD_PARAMS = ['x', 'norm1_w', 'w_in', 'conv_w', 'conv_b', 'dt_bias', 'a_log', 'd_skip', 'ssd_norm_w', 'pool_w', 'pool_b', 'pool_scale', 'w_out', 'norm2_w', 'w_gate', 'w_up', 'w_down', 'final_norm_w']
TWIN_WEIGHTS = ['norm1_w', 'w_in', 'conv_w', 'conv_b', 'dt_bias', 'a_log', 'd_skip', 'ssd_norm_w', 'pool_w', 'pool_b', 'pool_scale', 'w_out', 'norm2_w', 'w_gate', 'w_up', 'w_down', 'final_norm_w']
TWIN_DIFF_INPUT = 'x'
TWIN_INPUTS = ['x', 'norm1_w', 'w_in', 'conv_w', 'conv_b', 'dt_bias', 'a_log', 'd_skip', 'ssd_norm_w', 'pool_w', 'pool_b', 'pool_scale', 'w_out', 'norm2_w', 'w_gate', 'w_up', 'w_down', 'final_norm_w', 'loss_target', 'm_norm1_w', 'm_w_in', 'm_conv_w', 'm_conv_b', 'm_dt_bias', 'm_a_log', 'm_d_skip', 'm_ssd_norm_w', 'm_pool_w', 'm_pool_b', 'm_pool_scale', 'm_w_out', 'm_norm2_w', 'm_w_gate', 'm_w_up', 'm_w_down', 'm_final_norm_w', 'v_norm1_w', 'v_w_in', 'v_conv_w', 'v_conv_b', 'v_dt_bias', 'v_a_log', 'v_d_skip', 'v_ssd_norm_w', 'v_pool_w', 'v_pool_b', 'v_pool_scale', 'v_w_out', 'v_norm2_w', 'v_w_gate', 'v_w_up', 'v_w_down', 'v_final_norm_w']
TWIN_OUTPUTS = ['loss', 'grad_x', 'grad_norm1_w', 'grad_w_in', 'grad_conv_w', 'grad_conv_b', 'grad_dt_bias', 'grad_a_log', 'grad_d_skip', 'grad_ssd_norm_w', 'grad_pool_w', 'grad_pool_b', 'grad_pool_scale', 'grad_w_out', 'grad_norm2_w', 'grad_w_gate', 'grad_w_up', 'grad_w_down', 'grad_final_norm_w', 'delta_norm1_w', 'delta_w_in', 'delta_conv_w', 'delta_conv_b', 'delta_dt_bias', 'delta_a_log', 'delta_d_skip', 'delta_ssd_norm_w', 'delta_pool_w', 'delta_pool_b', 'delta_pool_scale', 'delta_w_out', 'delta_norm2_w', 'delta_w_gate', 'delta_w_up', 'delta_w_down', 'delta_final_norm_w', 'new_m_norm1_w', 'new_m_w_in', 'new_m_conv_w', 'new_m_conv_b', 'new_m_dt_bias', 'new_m_a_log', 'new_m_d_skip', 'new_m_ssd_norm_w', 'new_m_pool_w', 'new_m_pool_b', 'new_m_pool_scale', 'new_m_w_out', 'new_m_norm2_w', 'new_m_w_gate', 'new_m_w_up', 'new_m_w_down', 'new_m_final_norm_w', 'new_v_norm1_w', 'new_v_w_in', 'new_v_conv_w', 'new_v_conv_b', 'new_v_dt_bias', 'new_v_a_log', 'new_v_d_skip', 'new_v_ssd_norm_w', 'new_v_pool_w', 'new_v_pool_b', 'new_v_pool_scale', 'new_v_w_out', 'new_v_norm2_w', 'new_v_w_gate', 'new_v_w_up', 'new_v_w_down', 'new_v_final_norm_w']
TWIN_LEAF_KINDS = {'loss': 'loss', 'grad_x': 'grad_x', 'grad_norm1_w': 'grad_w', 'grad_w_in': 'grad_w', 'grad_conv_w': 'grad_w', 'grad_conv_b': 'grad_w', 'grad_dt_bias': 'grad_w', 'grad_a_log': 'grad_w', 'grad_d_skip': 'grad_w', 'grad_ssd_norm_w': 'grad_w', 'grad_pool_w': 'grad_w', 'grad_pool_b': 'grad_w', 'grad_pool_scale': 'grad_w', 'grad_w_out': 'grad_w', 'grad_norm2_w': 'grad_w', 'grad_w_gate': 'grad_w', 'grad_w_up': 'grad_w', 'grad_w_down': 'grad_w', 'grad_final_norm_w': 'grad_w', 'delta_norm1_w': 'delta_w', 'delta_w_in': 'delta_w', 'delta_conv_w': 'delta_w', 'delta_conv_b': 'delta_w', 'delta_dt_bias': 'delta_w', 'delta_a_log': 'delta_w', 'delta_d_skip': 'delta_w', 'delta_ssd_norm_w': 'delta_w', 'delta_pool_w': 'delta_w', 'delta_pool_b': 'delta_w', 'delta_pool_scale': 'delta_w', 'delta_w_out': 'delta_w', 'delta_norm2_w': 'delta_w', 'delta_w_gate': 'delta_w', 'delta_w_up': 'delta_w', 'delta_w_down': 'delta_w', 'delta_final_norm_w': 'delta_w', 'new_m_norm1_w': 'new_m', 'new_m_w_in': 'new_m', 'new_m_conv_w': 'new_m', 'new_m_conv_b': 'new_m', 'new_m_dt_bias': 'new_m', 'new_m_a_log': 'new_m', 'new_m_d_skip': 'new_m', 'new_m_ssd_norm_w': 'new_m', 'new_m_pool_w': 'new_m', 'new_m_pool_b': 'new_m', 'new_m_pool_scale': 'new_m', 'new_m_w_out': 'new_m', 'new_m_norm2_w': 'new_m', 'new_m_w_gate': 'new_m', 'new_m_w_up': 'new_m', 'new_m_w_down': 'new_m', 'new_m_final_norm_w': 'new_m', 'new_v_norm1_w': 'new_v', 'new_v_w_in': 'new_v', 'new_v_conv_w': 'new_v', 'new_v_conv_b': 'new_v', 'new_v_dt_bias': 'new_v', 'new_v_a_log': 'new_v', 'new_v_d_skip': 'new_v', 'new_v_ssd_norm_w': 'new_v', 'new_v_pool_w': 'new_v', 'new_v_pool_b': 'new_v', 'new_v_pool_scale': 'new_v', 'new_v_w_out': 'new_v', 'new_v_norm2_w': 'new_v', 'new_v_w_gate': 'new_v', 'new_v_w_up': 'new_v', 'new_v_w_down': 'new_v', 'new_v_final_norm_w': 'new_v'}


def _forward(args):
    return _fwd_reference(*[args[k] for k in FWD_PARAMS])


def _output_shape():
    out = _jax.eval_shape(lambda: _forward(_fwd_setup_inputs(0)))
    return out.shape, out.dtype

N_MICROBATCH = 1
ADAM_LR = 0.001
ADAM_B1 = 0.9
ADAM_B2 = 0.999
ADAM_EPS = 1e-08
ADAM_WD = 0.01
ADAM_STEP = 10
PER_EXAMPLE_BATCH_AXIS = {'x': 0, 'loss_target': 0}
SHARED_INPUTS = []
_WEIGHT_DTYPES = {'norm1_w': _jnp.float32, 'w_in': _jnp.float32, 'conv_w': _jnp.float32, 'conv_b': _jnp.float32, 'dt_bias': _jnp.float32, 'a_log': _jnp.float32, 'd_skip': _jnp.float32, 'ssd_norm_w': _jnp.float32, 'pool_w': _jnp.float32, 'pool_b': _jnp.float32, 'pool_scale': _jnp.float32, 'w_out': _jnp.float32, 'norm2_w': _jnp.float32, 'w_gate': _jnp.float32, 'w_up': _jnp.float32, 'w_down': _jnp.float32, 'final_norm_w': _jnp.float32}
MOMENT_SCALE = {'norm1_w': 1.229032e-01, 'w_in': 7.737799e-02, 'conv_w': 7.867305e-02, 'conv_b': 9.916019e-02, 'dt_bias': 3.704645e-01, 'a_log': 2.335207e-01, 'd_skip': 5.640710e-01, 'ssd_norm_w': 1.051511e-01, 'pool_w': 8.639267e-02, 'pool_b': 1.215226e-01, 'pool_scale': 8.723570e-02, 'w_out': 8.828778e-02, 'norm2_w': 7.341729e-02, 'w_gate': 3.209989e-02, 'w_up': 3.103556e-02, 'w_down': 5.146122e-02, 'final_norm_w': 1.603205e+01}


def _to_microbatches(a, axis):
    t = _jnp.moveaxis(a, axis, 0)
    t = t.reshape((N_MICROBATCH, t.shape[0] // N_MICROBATCH) + t.shape[1:])
    return _jnp.moveaxis(t, 1, axis + 1)


def setup_inputs(seed: int = 0) -> dict:
    inp = _fwd_setup_inputs(seed)
    key = _jax.random.fold_in(_jax.random.key(seed), 7919)
    shape, _ = _output_shape()
    out = dict(inp)
    out["loss_target"] = _jax.random.normal(_jax.random.fold_in(key, 0), shape, _jnp.float32)
    for i, name in enumerate(TWIN_WEIGHTS):
        w = inp[name].astype(_jnp.float32)
        if MOMENT_SCALE is None:
            s = _jnp.sqrt(_jnp.mean(_jnp.square(w)) + 1e-30)
        else:
            s = MOMENT_SCALE[name]
        km, kv = _jax.random.split(_jax.random.fold_in(key, i + 1))
        out[name] = w
        out["m_" + name] = s * _jax.random.normal(km, w.shape, _jnp.float32)
        out["v_" + name] = (s * s) * _jax.random.uniform(kv, w.shape, _jnp.float32, 0.5, 1.5)
    if N_MICROBATCH > 1:
        for name, axis in PER_EXAMPLE_BATCH_AXIS.items():
            out[name] = _to_microbatches(out[name], axis)
    return {'x': out['x'], 'norm1_w': out['norm1_w'], 'w_in': out['w_in'], 'conv_w': out['conv_w'], 'conv_b': out['conv_b'], 'dt_bias': out['dt_bias'], 'a_log': out['a_log'], 'd_skip': out['d_skip'], 'ssd_norm_w': out['ssd_norm_w'], 'pool_w': out['pool_w'], 'pool_b': out['pool_b'], 'pool_scale': out['pool_scale'], 'w_out': out['w_out'], 'norm2_w': out['norm2_w'], 'w_gate': out['w_gate'], 'w_up': out['w_up'], 'w_down': out['w_down'], 'final_norm_w': out['final_norm_w'], 'loss_target': out['loss_target'], 'm_norm1_w': out['m_norm1_w'], 'm_w_in': out['m_w_in'], 'm_conv_w': out['m_conv_w'], 'm_conv_b': out['m_conv_b'], 'm_dt_bias': out['m_dt_bias'], 'm_a_log': out['m_a_log'], 'm_d_skip': out['m_d_skip'], 'm_ssd_norm_w': out['m_ssd_norm_w'], 'm_pool_w': out['m_pool_w'], 'm_pool_b': out['m_pool_b'], 'm_pool_scale': out['m_pool_scale'], 'm_w_out': out['m_w_out'], 'm_norm2_w': out['m_norm2_w'], 'm_w_gate': out['m_w_gate'], 'm_w_up': out['m_w_up'], 'm_w_down': out['m_w_down'], 'm_final_norm_w': out['m_final_norm_w'], 'v_norm1_w': out['v_norm1_w'], 'v_w_in': out['v_w_in'], 'v_conv_w': out['v_conv_w'], 'v_conv_b': out['v_conv_b'], 'v_dt_bias': out['v_dt_bias'], 'v_a_log': out['v_a_log'], 'v_d_skip': out['v_d_skip'], 'v_ssd_norm_w': out['v_ssd_norm_w'], 'v_pool_w': out['v_pool_w'], 'v_pool_b': out['v_pool_b'], 'v_pool_scale': out['v_pool_scale'], 'v_w_out': out['v_w_out'], 'v_norm2_w': out['v_norm2_w'], 'v_w_gate': out['v_w_gate'], 'v_w_up': out['v_w_up'], 'v_w_down': out['v_w_down'], 'v_final_norm_w': out['v_final_norm_w']}


def _loss(weights, diff, rest, loss_target):
    with _jax.named_scope("forward"):
        args = {**rest, TWIN_DIFF_INPUT: diff, **{k: w.astype(_WEIGHT_DTYPES[k]) for k, w in weights.items()}}
        y = _forward(args)
    with _jax.named_scope("loss_head"):
        err = _jnp.square(y.astype(_jnp.float32) - loss_target)
        return 0.5 * _jnp.sum(_jnp.mean(err, axis=-1)) if err.ndim else 0.5 * err


def _adamw(w, g, m, v):
    m = ADAM_B1 * m + (1.0 - ADAM_B1) * g
    v = ADAM_B2 * v + (1.0 - ADAM_B2) * _jnp.square(g)
    m_hat = m / (1.0 - ADAM_B1 ** ADAM_STEP)
    v_hat = v / (1.0 - ADAM_B2 ** ADAM_STEP)
    delta = -ADAM_LR * (m_hat / (_jnp.sqrt(v_hat) + ADAM_EPS) + ADAM_WD * w)
    return delta, m, v


def reference(x, norm1_w, w_in, conv_w, conv_b, dt_bias, a_log, d_skip, ssd_norm_w, pool_w, pool_b, pool_scale, w_out, norm2_w, w_gate, w_up, w_down, final_norm_w, loss_target, m_norm1_w, m_w_in, m_conv_w, m_conv_b, m_dt_bias, m_a_log, m_d_skip, m_ssd_norm_w, m_pool_w, m_pool_b, m_pool_scale, m_w_out, m_norm2_w, m_w_gate, m_w_up, m_w_down, m_final_norm_w, v_norm1_w, v_w_in, v_conv_w, v_conv_b, v_dt_bias, v_a_log, v_d_skip, v_ssd_norm_w, v_pool_w, v_pool_b, v_pool_scale, v_w_out, v_norm2_w, v_w_gate, v_w_up, v_w_down, v_final_norm_w):
    given = dict(x=x, norm1_w=norm1_w, w_in=w_in, conv_w=conv_w, conv_b=conv_b, dt_bias=dt_bias, a_log=a_log, d_skip=d_skip, ssd_norm_w=ssd_norm_w, pool_w=pool_w, pool_b=pool_b, pool_scale=pool_scale, w_out=w_out, norm2_w=norm2_w, w_gate=w_gate, w_up=w_up, w_down=w_down, final_norm_w=final_norm_w, loss_target=loss_target, m_norm1_w=m_norm1_w, m_w_in=m_w_in, m_conv_w=m_conv_w, m_conv_b=m_conv_b, m_dt_bias=m_dt_bias, m_a_log=m_a_log, m_d_skip=m_d_skip, m_ssd_norm_w=m_ssd_norm_w, m_pool_w=m_pool_w, m_pool_b=m_pool_b, m_pool_scale=m_pool_scale, m_w_out=m_w_out, m_norm2_w=m_norm2_w, m_w_gate=m_w_gate, m_w_up=m_w_up, m_w_down=m_w_down, m_final_norm_w=m_final_norm_w, v_norm1_w=v_norm1_w, v_w_in=v_w_in, v_conv_w=v_conv_w, v_conv_b=v_conv_b, v_dt_bias=v_dt_bias, v_a_log=v_a_log, v_d_skip=v_d_skip, v_ssd_norm_w=v_ssd_norm_w, v_pool_w=v_pool_w, v_pool_b=v_pool_b, v_pool_scale=v_pool_scale, v_w_out=v_w_out, v_norm2_w=v_norm2_w, v_w_gate=v_w_gate, v_w_up=v_w_up, v_w_down=v_w_down, v_final_norm_w=v_final_norm_w)
    weights = {n: given[n] for n in TWIN_WEIGHTS}
    shared = {n: given[n] for n in SHARED_INPUTS}
    per_example = {n: given[n] for n in ['x']}
    grad_fn = _jax.value_and_grad(_loss, argnums=(0, 1))

    def one_microbatch(ex, loss_target):
        ex = dict(ex)
        diff = ex.pop(TWIN_DIFF_INPUT)
        return grad_fn(weights, diff, {**shared, **ex}, loss_target)

    if N_MICROBATCH == 1:
        loss, (grad_w, grad_x) = one_microbatch(per_example, given["loss_target"])
    else:
        def body(carry, xs):
            loss_sum, grad_sum = carry
            l_k, (gw_k, gx_k) = one_microbatch(xs[0], xs[1])
            with _jax.named_scope("update"):
                return (loss_sum + l_k, _jax.tree.map(_jnp.add, grad_sum, gw_k)), gx_k

        init = (_jnp.zeros((), _jnp.float32), _jax.tree.map(_jnp.zeros_like, weights))
        (loss, grad_w), grad_x = _jax.lax.scan(body, init, (per_example, given["loss_target"]))
    with _jax.named_scope("update"):
        delta_w, new_m, new_v = {}, {}, {}
        for n in TWIN_WEIGHTS:
            delta_w[n], new_m[n], new_v[n] = _adamw(weights[n], grad_w[n], given["m_" + n], given["v_" + n])
    return (loss, grad_x, *[grad_w[n] for n in TWIN_WEIGHTS], *[delta_w[n] for n in TWIN_WEIGHTS],
            *[new_m[n] for n in TWIN_WEIGHTS], *[new_v[n] for n in TWIN_WEIGHTS])
```

```python
import functools

import numpy as np
import jax
import jax.numpy as jnp
from jax import lax
from jax.experimental import pallas as pl
from jax.experimental.pallas import tpu as pltpu

F32 = jnp.float32
BF16 = jnp.bfloat16
MESH = pl.DeviceIdType.MESH

D_MODEL = 1024
DEPTH = 4
EPS = 1e-6
SSD_WIDTH = 512
SSD_HEADS = 8
HEAD_DIM = 64
D_STATE = 128
CHUNK = 128
CONV_WIDTH = 4
CONV_DIM = 1024
SB_WIDTH = 256
POOL_WIDTH = 256
POOL_WINDOWS = (2, 4, 8, 16)
D_FF = 2816
D_IN = 2568
N_CHIPS = 4
OFF_Z, OFF_XBC, OFF_QKV, OFF_P, OFF_DT = 0, 512, 1536, 2304, 2560
D_INP = 2688
ROWS_IN, ROWS_OUT, ROWS_FF = 642, 256, 704
FLAT_ROWS = 3072
ADAM_LR, ADAM_B1, ADAM_B2, ADAM_EPS, ADAM_WD, ADAM_STEP = 0.001, 0.9, 0.999, 1e-08, 0.01, 10
LANES = 128
VMEM_LIMIT = 56 * 1024 * 1024


def _params(sem=None):
    return pltpu.CompilerParams(dimension_semantics=sem, vmem_limit_bytes=VMEM_LIMIT)


def _tile(n, cap):
    best = None
    for t in range(LANES, min(n, cap) + 1, LANES):
        if n % t == 0:
            best = t
    assert best is not None, (n, cap)
    return best


def _nt(a, b):
    return lax.dot_general(a, b, (((1,), (1,)), ((), ())), preferred_element_type=F32)


def _tn(a, b):
    return lax.dot_general(a, b, (((0,), (0,)), ((), ())), preferred_element_type=F32)


def _nn(a, b):
    return jnp.dot(a, b, preferred_element_type=F32)


def _split_dot(a, b_exact, terms=3, dot=_nn):
    acc = None
    rest = a
    for _ in range(terms):
        hi = rest.astype(BF16)
        part = dot(hi, b_exact)
        acc = part if acc is None else acc + part
        rest = rest - hi.astype(F32)
    return acc


def _split_dot_left(a_exact, b, terms=3):
    acc = None
    rest = b
    for _ in range(terms):
        hi = rest.astype(BF16)
        part = _nn(a_exact, hi)
        acc = part if acc is None else acc + part
        rest = rest - hi.astype(F32)
    return acc


def _sigmoid(x):
    return 1.0 / (1.0 + jnp.exp(-x))


def _softplus(x):
    return jnp.maximum(x, 0.0) + jnp.log(1.0 + jnp.exp(-jnp.abs(x)))


def rms_matmul(x, nw, w, name):
    s, d = x.shape
    n = w.shape[1]
    tm, tn = _tile(s, 512), _tile(n, 896)

    def body(x_ref, nw_ref, w_ref, o_ref, h_ref):
        @pl.when(pl.program_id(1) == 0)
        def _():
            xv = x_ref[...]
            r = lax.rsqrt(jnp.mean(xv * xv, axis=-1, keepdims=True) + EPS)
            h_ref[...] = (xv * r * nw_ref[...]).astype(BF16)
        o_ref[...] = _nn(h_ref[...], w_ref[...])

    return pl.pallas_call(
        body, name=name, grid=(s // tm, n // tn),
        in_specs=[pl.BlockSpec((tm, d), lambda i, j: (i, 0)), pl.BlockSpec((1, d), lambda i, j: (0, 0)),
                  pl.BlockSpec((d, tn), lambda i, j: (0, j))],
        out_specs=pl.BlockSpec((tm, tn), lambda i, j: (i, j)),
        out_shape=jax.ShapeDtypeStruct((s, n), F32),
        scratch_shapes=[pltpu.VMEM((tm, d), BF16)],
        compiler_params=_params(("parallel", "arbitrary")),
    )(x, nw, w)


def matmul_residual(a, w, res, name):
    s, k = a.shape
    n = w.shape[1]
    tm, tn = _tile(s, 512), _tile(n, 512)

    def body(a_ref, w_ref, r_ref, o_ref):
        o_ref[...] = r_ref[...] + _nn(a_ref[...], w_ref[...])

    return pl.pallas_call(
        body, name=name, grid=(s // tm, n // tn),
        in_specs=[pl.BlockSpec((tm, k), lambda i, j: (i, 0)), pl.BlockSpec((k, tn), lambda i, j: (0, j)),
                  pl.BlockSpec((tm, tn), lambda i, j: (i, j))],
        out_specs=pl.BlockSpec((tm, tn), lambda i, j: (i, j)),
        out_shape=jax.ShapeDtypeStruct((s, n), F32),
        compiler_params=_params(("parallel", "parallel")),
    )(a, w, res)


def matmul_nt(a, w, name, out_dtype=F32):
    s, n = a.shape
    k = w.shape[0]
    tm, tk = _tile(s, 512), _tile(k, 512)

    def body(a_ref, w_ref, o_ref):
        o_ref[...] = _nt(a_ref[...], w_ref[...]).astype(out_dtype)

    return pl.pallas_call(
        body, name=name, grid=(s // tm, k // tk),
        in_specs=[pl.BlockSpec((tm, n), lambda i, j: (i, 0)), pl.BlockSpec((tk, n), lambda i, j: (j, 0))],
        out_specs=pl.BlockSpec((tm, tk), lambda i, j: (i, j)),
        out_shape=jax.ShapeDtypeStruct((s, k), out_dtype),
        compiler_params=_params(("parallel", "parallel")),
    )(a, w)


def matmul_tn(a, b, name, out_dtype=BF16):
    s, m = a.shape
    n = b.shape[1]
    tm, tn = _tile(m, 512), _tile(n, 512)

    def body(a_ref, b_ref, o_ref):
        o_ref[...] = _tn(a_ref[...], b_ref[...]).astype(out_dtype)

    return pl.pallas_call(
        body, name=name, grid=(m // tm, n // tn),
        in_specs=[pl.BlockSpec((s, tm), lambda i, j: (0, i)), pl.BlockSpec((s, tn), lambda i, j: (0, j))],
        out_specs=pl.BlockSpec((tm, tn), lambda i, j: (i, j)),
        out_shape=jax.ShapeDtypeStruct((m, n), out_dtype),
        compiler_params=_params(("parallel", "parallel")),
    )(a, b)


def ffn_forward(x1, nw, wg, wu, wd, name):
    s, d = x1.shape
    f = wg.shape[1]
    tm, tf = _tile(s, 1024), _tile(f, 256)

    def body(x_ref, nw_ref, wg_ref, wu_ref, wd_ref, o_ref, g_ref, u_ref, h_ref, acc_ref):
        j = pl.program_id(1)

        @pl.when(j == 0)
        def _():
            xv = x_ref[...]
            r = lax.rsqrt(jnp.mean(xv * xv, axis=-1, keepdims=True) + EPS)
            h_ref[...] = (xv * r * nw_ref[...]).astype(BF16)
            acc_ref[...] = xv

        h = h_ref[...]
        g = _nn(h, wg_ref[...])
        u = _nn(h, wu_ref[...])
        g_ref[...] = g.astype(BF16)
        u_ref[...] = u.astype(BF16)
        a = (g * _sigmoid(g) * u).astype(BF16)
        acc_ref[...] += _nn(a, wd_ref[...])

        @pl.when(j == pl.num_programs(1) - 1)
        def _():
            o_ref[...] = acc_ref[...]

    return pl.pallas_call(
        body, name=name, grid=(s // tm, f // tf),
        in_specs=[pl.BlockSpec((tm, d), lambda i, j: (i, 0)), pl.BlockSpec((1, d), lambda i, j: (0, 0)),
                  pl.BlockSpec((d, tf), lambda i, j: (0, j)), pl.BlockSpec((d, tf), lambda i, j: (0, j)),
                  pl.BlockSpec((tf, d), lambda i, j: (j, 0))],
        out_specs=[pl.BlockSpec((tm, d), lambda i, j: (i, 0)), pl.BlockSpec((tm, tf), lambda i, j: (i, j)),
                   pl.BlockSpec((tm, tf), lambda i, j: (i, j))],
        out_shape=[jax.ShapeDtypeStruct((s, d), F32), jax.ShapeDtypeStruct((s, f), BF16),
                   jax.ShapeDtypeStruct((s, f), BF16)],
        scratch_shapes=[pltpu.VMEM((tm, d), BF16), pltpu.VMEM((tm, d), F32)],
        compiler_params=_params(("parallel", "arbitrary")),
    )(x1, nw, wg, wu, wd)


def ffn_backward_act(dx2, g, u, wd, name):
    s, d = dx2.shape
    f = wd.shape[0]
    tm, tf = _tile(s, 512), _tile(f, 1408)

    def body(dx_ref, g_ref, u_ref, wd_ref, dg_ref, du_ref, a_ref):
        da = _nt(dx_ref[...], wd_ref[...])
        gv = g_ref[...].astype(F32)
        uv = u_ref[...].astype(F32)
        sg = _sigmoid(gv)
        silu = gv * sg
        dg_ref[...] = (da * uv * (sg * (1.0 + gv * (1.0 - sg)))).astype(BF16)
        du_ref[...] = (da * silu).astype(BF16)
        a_ref[...] = (silu * uv).astype(BF16)

    blk = pl.BlockSpec((tm, tf), lambda i, j: (i, j))
    return pl.pallas_call(
        body, name=name, grid=(s // tm, f // tf),
        in_specs=[pl.BlockSpec((tm, d), lambda i, j: (i, 0)), blk, blk, pl.BlockSpec((tf, d), lambda i, j: (j, 0))],
        out_specs=[blk, blk, blk],
        out_shape=[jax.ShapeDtypeStruct((s, f), BF16)] * 3,
        compiler_params=_params(("parallel", "parallel")),
    )(dx2, g, u, wd)


def rms_backward_nt(dzs, ws, x, nw, dres, name, tm):
    s, d = x.shape
    nz = len(dzs)

    def body(*refs):
        dz_refs, w_refs = refs[:nz], refs[nz:2 * nz]
        x_ref, nw_ref, dres_ref, dx_ref, dxb_ref, h_ref, dnw_ref = refs[2 * nz:]
        dh = _nt(dz_refs[0][...], w_refs[0][...])
        for k in range(1, nz):
            dh = dh + _nt(dz_refs[k][...], w_refs[k][...])
        xv = x_ref[...]
        r = lax.rsqrt(jnp.mean(xv * xv, axis=-1, keepdims=True) + EPS)
        xhat = xv * r
        nwv = nw_ref[...]
        h_ref[...] = (xhat * nwv).astype(BF16)

        @pl.when(pl.program_id(0) == 0)
        def _():
            dnw_ref[...] = jnp.zeros_like(dnw_ref)

        dnw_ref[...] += jnp.sum(dh * xhat, axis=0, keepdims=True)
        gdh = dh * nwv
        dx = dres_ref[...] + r * (gdh - xhat * jnp.mean(gdh * xhat, axis=-1, keepdims=True))
        dx_ref[...] = dx
        dxb_ref[...] = dx.astype(BF16)

    row = pl.BlockSpec((tm, d), lambda i: (i, 0))
    in_specs = [pl.BlockSpec((tm, dz.shape[1]), lambda i: (i, 0)) for dz in dzs]
    in_specs += [pl.BlockSpec(w.shape, lambda i: (0, 0)) for w in ws]
    in_specs += [row, pl.BlockSpec((1, d), lambda i: (0, 0)), row]
    return pl.pallas_call(
        body, name=name, grid=(s // tm,),
        in_specs=in_specs,
        out_specs=[row, row, row, pl.BlockSpec((1, d), lambda i: (0, 0))],
        out_shape=[jax.ShapeDtypeStruct((s, d), F32), jax.ShapeDtypeStruct((s, d), BF16),
                   jax.ShapeDtypeStruct((s, d), BF16), jax.ShapeDtypeStruct((1, d), F32)],
        compiler_params=_params(("arbitrary",)),
    )(*dzs, *ws, x, nw, dres)


def loss_head(x, nw, target, name):
    s, d = x.shape
    tm = _tile(s, 512)

    def body(x_ref, nw_ref, t_ref, loss_ref, dx_ref, dxb_ref, dnw_ref):
        xv = x_ref[...]
        r = lax.rsqrt(jnp.mean(xv * xv, axis=-1, keepdims=True) + EPS)
        xhat = xv * r
        nwv = nw_ref[...]
        err = xhat * nwv - t_ref[...]

        @pl.when(pl.program_id(0) == 0)
        def _():
            dnw_ref[...] = jnp.zeros_like(dnw_ref)
            loss_ref[...] = jnp.zeros_like(loss_ref)

        part = jnp.sum(jnp.sum(err * err, axis=-1, keepdims=True), axis=0, keepdims=True) * (0.5 / d)
        loss_ref[...] += jnp.broadcast_to(part, loss_ref.shape)
        dout = err * (1.0 / d)
        dnw_ref[...] += jnp.sum(dout * xhat, axis=0, keepdims=True)
        gdh = dout * nwv
        dx = r * (gdh - xhat * jnp.mean(gdh * xhat, axis=-1, keepdims=True))
        dx_ref[...] = dx
        dxb_ref[...] = dx.astype(BF16)

    row = pl.BlockSpec((tm, d), lambda i: (i, 0))
    return pl.pallas_call(
        body, name=name, grid=(s // tm,),
        in_specs=[row, pl.BlockSpec((1, d), lambda i: (0, 0)), row],
        out_specs=[pl.BlockSpec((1, LANES), lambda i: (0, 0)), row, row, pl.BlockSpec((1, d), lambda i: (0, 0))],
        out_shape=[jax.ShapeDtypeStruct((1, LANES), F32), jax.ShapeDtypeStruct((s, d), F32),
                   jax.ShapeDtypeStruct((s, d), BF16), jax.ShapeDtypeStruct((1, d), F32)],
        compiler_params=_params(("arbitrary",)),
    )(x, nw, target)


def _shift_down(x, k):
    row = lax.broadcasted_iota(jnp.int32, x.shape, 0)
    return jnp.where(row >= k, pltpu.roll(x, k, axis=0), 0.0)


def _shift_up(x, k):
    s = x.shape[0]
    row = lax.broadcasted_iota(jnp.int32, x.shape, 0)
    return jnp.where(row < s - k, pltpu.roll(x, s - k, axis=0), 0.0)


def conv_forward(proj, cw, cb, name):
    s = proj.shape[0]
    tn = 256
    off = OFF_XBC // tn

    def body(u_ref, w_ref, b_ref, o_ref):
        u = u_ref[...]
        pre = b_ref[...] + w_ref[CONV_WIDTH - 1:CONV_WIDTH, :] * u
        for i in range(CONV_WIDTH - 1):
            pre = pre + w_ref[i:i + 1, :] * _shift_down(u, CONV_WIDTH - 1 - i)
        o_ref[...] = pre * _sigmoid(pre)

    return pl.pallas_call(
        body, name=name, grid=(CONV_DIM // tn,),
        in_specs=[pl.BlockSpec((s, tn), lambda j: (0, off + j)), pl.BlockSpec((8, tn), lambda j: (0, j)),
                  pl.BlockSpec((1, tn), lambda j: (0, j))],
        out_specs=pl.BlockSpec((s, tn), lambda j: (0, j)),
        out_shape=jax.ShapeDtypeStruct((s, CONV_DIM), F32),
        compiler_params=_params(("parallel",)),
    )(proj, cw, cb)


def conv_backward(proj, dxc, cw, cb, name):
    s = proj.shape[0]
    tn = 256
    off = OFF_XBC // tn

    def body(u_ref, d_ref, w_ref, b_ref, du_ref, dw_ref, db_ref):
        u = u_ref[...]
        shifted = [_shift_down(u, CONV_WIDTH - 1 - i) for i in range(CONV_WIDTH - 1)] + [u]
        pre = b_ref[...] + w_ref[CONV_WIDTH - 1:CONV_WIDTH, :] * u
        for i in range(CONV_WIDTH - 1):
            pre = pre + w_ref[i:i + 1, :] * shifted[i]
        sg = _sigmoid(pre)
        dpre = d_ref[...] * (sg * (1.0 + pre * (1.0 - sg)))
        du = w_ref[CONV_WIDTH - 1:CONV_WIDTH, :] * dpre
        for i in range(CONV_WIDTH - 1):
            du = du + w_ref[i:i + 1, :] * _shift_up(dpre, CONV_WIDTH - 1 - i)
        du_ref[...] = du.astype(BF16)
        rows = [jnp.sum(dpre * shifted[i], axis=0, keepdims=True) for i in range(CONV_WIDTH)]
        rows.append(jnp.zeros((8 - CONV_WIDTH, tn), F32))
        dw_ref[...] = jnp.concatenate(rows, axis=0)
        db_ref[...] = jnp.sum(dpre, axis=0, keepdims=True)

    return pl.pallas_call(
        body, name=name, grid=(CONV_DIM // tn,),
        in_specs=[pl.BlockSpec((s, tn), lambda j: (0, off + j)), pl.BlockSpec((s, tn), lambda j: (0, j)),
                  pl.BlockSpec((8, tn), lambda j: (0, j)), pl.BlockSpec((1, tn), lambda j: (0, j))],
        out_specs=[pl.BlockSpec((s, tn), lambda j: (0, j)), pl.BlockSpec((8, tn), lambda j: (0, j)),
                   pl.BlockSpec((1, tn), lambda j: (0, j))],
        out_shape=[jax.ShapeDtypeStruct((s, CONV_DIM), BF16), jax.ShapeDtypeStruct((8, CONV_DIM), F32),
                   jax.ShapeDtypeStruct((1, CONV_DIM), F32)],
        compiler_params=_params(("parallel",)),
    )(proj, dxc, cw, cb)


def _pool_lane_window(shape):
    lane = lax.broadcasted_iota(jnp.int32, shape, 1)
    grp = lane // (POOL_WIDTH // len(POOL_WINDOWS))
    win = jnp.full(shape, POOL_WINDOWS[-1], jnp.int32)
    for gi in range(len(POOL_WINDOWS) - 2, -1, -1):
        win = jnp.where(grp == gi, POOL_WINDOWS[gi], win)
    return grp, win


def _pool_select(grp, sums):
    out = sums[-1]
    for gi in range(len(sums) - 2, -1, -1):
        out = jnp.where(grp == gi, sums[gi], out)
    return out


def _pool_pooled(p):
    grp, win = _pool_lane_window(p.shape)
    row = lax.broadcasted_iota(jnp.int32, p.shape, 0)
    inv_count = 1.0 / jnp.minimum(row + 1, win).astype(F32)
    sums, acc, k = [], p, 1
    for _ in POOL_WINDOWS:
        acc = acc + _shift_down(acc, k)
        sums.append(acc)
        k *= 2
    return _pool_select(grp, sums) * inv_count - p, grp, inv_count


def pool_forward(proj, wbd, pb, ps, name):
    s = proj.shape[0]

    def body(p_ref, w_ref, b_ref, s_ref, o_ref):
        pooled, _, _ = _pool_pooled(p_ref[...])
        mixed = _nn(pooled.astype(BF16), w_ref[...]) + b_ref[...]
        o_ref[...] = (mixed * s_ref[...]).astype(BF16)

    vec = pl.BlockSpec((1, POOL_WIDTH), lambda j: (0, 0))
    return pl.pallas_call(
        body, name=name, grid=(1,),
        in_specs=[pl.BlockSpec((s, POOL_WIDTH), lambda j: (0, OFF_P // POOL_WIDTH)),
                  pl.BlockSpec((POOL_WIDTH, POOL_WIDTH), lambda j: (0, 0)), vec, vec],
        out_specs=pl.BlockSpec((s, POOL_WIDTH), lambda j: (0, 0)),
        out_shape=jax.ShapeDtypeStruct((s, POOL_WIDTH), BF16),
        compiler_params=_params(("arbitrary",)),
    )(proj, wbd, pb, ps)


def pool_backward(proj, dyall, wbd, pb, ps, name):
    s = proj.shape[0]

    def body(p_ref, dy_ref, w_ref, b_ref, s_ref, dp_ref, dw_ref, db_ref, ds_ref):
        pooled, grp, inv_count = _pool_pooled(p_ref[...])
        pooled_b = pooled.astype(BF16)
        mixed = _nn(pooled_b, w_ref[...]) + b_ref[...]
        dy = dy_ref[...]
        ds_ref[...] = jnp.sum(dy * mixed, axis=0, keepdims=True)
        dmixed = dy * s_ref[...]
        db_ref[...] = jnp.sum(dmixed, axis=0, keepdims=True)
        dmixed_b = dmixed.astype(BF16)
        dw_ref[...] = _tn(pooled_b, dmixed_b)
        dpooled = _nt(dmixed_b, w_ref[...])
        sums, acc, k = [], dpooled * inv_count, 1
        for _ in POOL_WINDOWS:
            acc = acc + _shift_up(acc, k)
            sums.append(acc)
            k *= 2
        dp_ref[...] = (_pool_select(grp, sums) - dpooled).astype(BF16)

    vec = pl.BlockSpec((1, POOL_WIDTH), lambda j: (0, 0))
    mat = pl.BlockSpec((POOL_WIDTH, POOL_WIDTH), lambda j: (0, 0))
    return pl.pallas_call(
        body, name=name, grid=(1,),
        in_specs=[pl.BlockSpec((s, POOL_WIDTH), lambda j: (0, OFF_P // POOL_WIDTH)),
                  pl.BlockSpec((s, POOL_WIDTH), lambda j: (0, (SSD_WIDTH + SB_WIDTH) // POOL_WIDTH)), mat, vec, vec],
        out_specs=[pl.BlockSpec((s, POOL_WIDTH), lambda j: (0, 0)), mat, vec, vec],
        out_shape=[jax.ShapeDtypeStruct((s, POOL_WIDTH), BF16), jax.ShapeDtypeStruct((POOL_WIDTH, POOL_WIDTH), F32),
                   jax.ShapeDtypeStruct((1, POOL_WIDTH), F32), jax.ShapeDtypeStruct((1, POOL_WIDTH), F32)],
        compiler_params=_params(("arbitrary",)),
    )(proj, dyall, wbd, pb, ps)


N_PAIRS = SSD_HEADS // 2


def _iota2(shape, dim):
    return lax.broadcasted_iota(jnp.int32, shape, dim)


def _ssd_common(xc, dtraw, dtb, alog):
    c = CHUNK
    dt = _softplus(dtraw + dtb)
    a = -jnp.exp(alog)
    ltri = (_iota2((c, c), 0) >= _iota2((c, c), 1)).astype(BF16)
    acum = _split_dot_left(ltri, dt * a)
    expand = (_iota2((c, SSD_WIDTH), 1) // HEAD_DIM == _iota2((c, SSD_WIDTH), 0)).astype(BF16)
    expand_wide = (_iota2((c, SSD_HEADS * c), 1) // c == _iota2((c, SSD_HEADS * c), 0)).astype(BF16)
    acum_x = _split_dot(acum, expand)
    dt_x = _split_dot(dt, expand)
    alast_x = acum_x[c - 1:c, :]
    return dict(dt=dt, a=a, acum=acum, acum_x=acum_x, dt_x=dt_x, ea_x=jnp.exp(acum_x),
                dte_x=jnp.exp(alast_x - acum_x), eal_x=jnp.exp(alast_x),
                acol=_split_dot(acum, expand_wide), acum_t=acum.T,
                xs=xc[:, :SSD_WIDTH], causal=_iota2((c, c), 0) >= _iota2((c, c), 1),
                left=_iota2((c, c), 1) < HEAD_DIM)


def _ssd_group(xc, g):
    b = xc[:, SSD_WIDTH + D_STATE * g:SSD_WIDTH + D_STATE * (g + 1)]
    cm = xc[:, SSD_WIDTH + 2 * D_STATE + D_STATE * g:SSD_WIDTH + 2 * D_STATE + D_STATE * (g + 1)]
    return b, cm


def _ssd_decay(q, k, hh):
    col = q["acol"][:, CHUNK * hh:CHUNK * (hh + 1)]
    row = q["acum_t"][hh:hh + 1, :]
    return jnp.where(q["causal"], jnp.exp(jnp.minimum(col - row, 0.0)), 0.0)


def ssd_forward(proj, xc, dtb, alog, dskip_x, nw, name):
    s = xc.shape[0]
    nc = s // CHUNK

    def body(xc_ref, dt_ref, z_ref, dtb_ref, alog_ref, dsk_ref, nw_ref, y_ref, yc_ref, st_ref, state):
        @pl.when(pl.program_id(0) == 0)
        def _():
            state[...] = jnp.zeros_like(state)

        xcv = xc_ref[...]
        q = _ssd_common(xcv, dt_ref[...], dtb_ref[...], alog_ref[...])
        x = q["xs"] * q["dt_x"]
        xb = x.astype(BF16)
        xd = (x * q["dte_x"]).astype(BF16)
        pieces = []
        for g in range(2):
            bg, cg = _ssd_group(xcv, g)
            bgb, cgb = bg.astype(BF16), cg.astype(BF16)
            cb = _nt(cgb, bgb)
            bgt = bg.T.astype(BF16)
            for pr in (2 * g, 2 * g + 1):
                sl = slice(CHUNK * pr, CHUNK * (pr + 1))
                st = state[pr]
                st_ref[0, pr] = st
                yp = _nn(cgb, st.astype(BF16)) * q["ea_x"][:, sl]
                for k, hh in enumerate((2 * pr, 2 * pr + 1)):
                    w = (cb * _ssd_decay(q, k, hh)).astype(BF16)
                    mask = q["left"] if k == 0 else jnp.logical_not(q["left"])
                    yp = yp + _nn(w, jnp.where(mask, xb[:, sl], jnp.zeros_like(xb[:, sl])))
                state[pr] = st * q["eal_x"][:, sl] + _nn(bgt, xd[:, sl])
                pieces.append(yp)
        y = jnp.concatenate(pieces, axis=1) + q["xs"] * dsk_ref[...]
        yc_ref[...] = y
        zv = z_ref[...]
        yg = y * (zv * _sigmoid(zv))
        r = lax.rsqrt(jnp.mean(yg * yg, axis=-1, keepdims=True) + EPS)
        y_ref[...] = (yg * r * nw_ref[...]).astype(BF16)

    vec = lambda n: pl.BlockSpec((1, n), lambda c: (0, 0))
    return pl.pallas_call(
        body, name=name, grid=(nc,),
        in_specs=[pl.BlockSpec((CHUNK, CONV_DIM), lambda c: (c, 0)),
                  pl.BlockSpec((CHUNK, LANES), lambda c: (c, OFF_DT // LANES)),
                  pl.BlockSpec((CHUNK, SSD_WIDTH), lambda c: (c, 0)),
                  vec(LANES), vec(LANES), vec(SSD_WIDTH), vec(SSD_WIDTH)],
        out_specs=[pl.BlockSpec((CHUNK, SSD_WIDTH), lambda c: (c, 0)), pl.BlockSpec((CHUNK, SSD_WIDTH), lambda c: (c, 0)),
                   pl.BlockSpec((1, N_PAIRS, D_STATE, CHUNK), lambda c: (c, 0, 0, 0))],
        out_shape=[jax.ShapeDtypeStruct((s, SSD_WIDTH), BF16), jax.ShapeDtypeStruct((s, SSD_WIDTH), F32),
                   jax.ShapeDtypeStruct((nc, N_PAIRS, D_STATE, CHUNK), F32)],
        scratch_shapes=[pltpu.VMEM((N_PAIRS, D_STATE, CHUNK), F32)],
        compiler_params=_params(("arbitrary",)),
    )(xc, proj, proj, dtb, alog, dskip_x, nw)


def ssd_backward(proj, xc, ycore, dyall, states, dtb, alog, dskip_x, nw, name):
    s = xc.shape[0]
    nc = s // CHUNK
    c = CHUNK

    def body(xc_ref, dt_ref, z_ref, yc_ref, dy_ref, st_ref, dtb_ref, alog_ref, dsk_ref, nw_ref,
             dxc_ref, ddt_ref, dz_ref, dnw_ref, ddsk_ref, ddtb_ref, dalog_ref, dstate):
        @pl.when(pl.program_id(0) == 0)
        def _():
            dstate[...] = jnp.zeros_like(dstate)
            dnw_ref[...] = jnp.zeros_like(dnw_ref)
            ddsk_ref[...] = jnp.zeros_like(ddsk_ref)
            ddtb_ref[...] = jnp.zeros_like(ddtb_ref)
            dalog_ref[...] = jnp.zeros_like(dalog_ref)

        xcv = xc_ref[...]
        dtraw = dt_ref[...]
        q = _ssd_common(xcv, dtraw, dtb_ref[...], alog_ref[...])
        xs = q["xs"]
        x = xs * q["dt_x"]
        zv, yc, dy, nwv = z_ref[...], yc_ref[...], dy_ref[...], nw_ref[...]
        sgz = _sigmoid(zv)
        siluz = zv * sgz
        yg = yc * siluz
        r = lax.rsqrt(jnp.mean(yg * yg, axis=-1, keepdims=True) + EPS)
        dnw_ref[...] += jnp.sum(dy * yg * r, axis=0, keepdims=True)
        g1 = dy * nwv
        dyg = r * (g1 - yg * (r * r) * jnp.mean(g1 * yg, axis=-1, keepdims=True))
        dyv = dyg * siluz
        dz_ref[...] = (dyg * yc * (sgz * (1.0 + zv * (1.0 - sgz)))).astype(BF16)
        ddsk_ref[...] += jnp.sum(dyv * xs, axis=0, keepdims=True)
        dye = dyv * q["ea_x"]
        dx_parts, yoff_parts, u_parts, v_parts, e_parts = [], [], [], [], []
        db_parts, dc_parts = [], []
        for g in range(2):
            bg, cg = _ssd_group(xcv, g)
            bgb, cgb = bg.astype(BF16), cg.astype(BF16)
            cb = _nt(cgb, bgb)
            cgt = cg.T.astype(BF16)
            dgsum = jnp.zeros((c, c), F32)
            dbg = jnp.zeros((c, D_STATE), F32)
            dcg = jnp.zeros((c, D_STATE), F32)
            for pr in (2 * g, 2 * g + 1):
                sl = slice(c * pr, c * (pr + 1))
                st = st_ref[0, pr]
                dst = dstate[pr]
                stb, dstb = st.astype(BF16), dst.astype(BF16)
                xp = x[:, sl]
                xpb = xp.astype(BF16)
                dyp = dyv[:, sl]
                xdp = xp * q["dte_x"][:, sl]
                yoff_parts.append(_nn(cgb, stb) * q["ea_x"][:, sl])
                rr = _nn(bgb, dstb)
                dxp = rr * q["dte_x"][:, sl]
                u_parts.append(rr * xdp)
                v_parts.append(dst * st * q["eal_x"][:, sl])
                for k, hh in enumerate((2 * pr, 2 * pr + 1)):
                    decay = _ssd_decay(q, k, hh)
                    w = cb * decay
                    mask = q["left"] if k == 0 else jnp.logical_not(q["left"])
                    dym = jnp.where(mask, dyp, 0.0).astype(BF16)
                    dw = _nt(dym, xpb)
                    dgsum = dgsum + dw * decay
                    e_parts.append(dw * w)
                    dxp = dxp + _nn(w.T.astype(BF16), dym)
                dyeb = dye[:, sl].astype(BF16)
                dcg = dcg + _nt(dyeb, stb)
                dbg = dbg + _nt(xdp.astype(BF16), dstb)
                dstate[pr] = dst * q["eal_x"][:, sl] + _nn(cgt, dyeb)
                dx_parts.append(dxp)
            dcg = dcg + _nn(dgsum.astype(BF16), bgb)
            dbg = dbg + _nn(dgsum.T.astype(BF16), cgb)
            db_parts.append(dbg)
            dc_parts.append(dcg)
        dx = jnp.concatenate(dx_parts, axis=1)
        yoff = jnp.concatenate(yoff_parts, axis=1)
        u = jnp.concatenate(u_parts, axis=1)
        v = jnp.concatenate(v_parts, axis=1)
        reduce_heads = (_iota2((SSD_WIDTH, c), 0) // HEAD_DIM == _iota2((SSD_WIDTH, c), 1)).astype(BF16)
        to_head = (_iota2((SSD_HEADS * c, c), 0) // c == _iota2((SSD_HEADS * c, c), 1)).astype(BF16)
        da = _split_dot(dyv * yoff - u, reduce_heads, 2)
        da = da + _split_dot(jnp.concatenate(e_parts, axis=1), to_head, 2)
        da = da - _split_dot(jnp.concatenate(e_parts, axis=0), to_head, 2, dot=_tn)
        dalast = jnp.sum(_split_dot(u + v, reduce_heads, 2), axis=0, keepdims=True)
        da = da + jnp.where(_iota2((c, c), 0) == c - 1, dalast, 0.0)
        utri = (_iota2((c, c), 1) >= _iota2((c, c), 0)).astype(BF16)
        dda = _split_dot_left(utri, da)
        ddt = dda * q["a"] + _split_dot(dx * xs, reduce_heads, 2)
        dalog_ref[...] += jnp.sum(dda * q["dt"], axis=0, keepdims=True) * q["a"]
        ddtraw = jnp.where(_iota2((c, c), 1) < SSD_HEADS, ddt * _sigmoid(dtraw + dtb_ref[...]), 0.0)
        ddtb_ref[...] += jnp.sum(ddtraw, axis=0, keepdims=True)
        ddt_ref[...] = ddtraw.astype(BF16)
        dxs = dx * q["dt_x"] + dyv * dsk_ref[...]
        dxc_ref[...] = jnp.concatenate([dxs] + db_parts + dc_parts, axis=1)

    rev = lambda i: nc - 1 - i
    vec = lambda n: pl.BlockSpec((1, n), lambda i: (0, 0))
    wide = pl.BlockSpec((c, SSD_WIDTH), lambda i: (rev(i), 0))
    return pl.pallas_call(
        body, name=name, grid=(nc,),
        in_specs=[pl.BlockSpec((c, CONV_DIM), lambda i: (rev(i), 0)),
                  pl.BlockSpec((c, LANES), lambda i: (rev(i), OFF_DT // LANES)),
                  wide, wide, wide,
                  pl.BlockSpec((1, N_PAIRS, D_STATE, c), lambda i: (rev(i), 0, 0, 0)),
                  vec(LANES), vec(LANES), vec(SSD_WIDTH), vec(SSD_WIDTH)],
        out_specs=[pl.BlockSpec((c, CONV_DIM), lambda i: (rev(i), 0)), pl.BlockSpec((c, LANES), lambda i: (rev(i), 0)),
                   wide, vec(SSD_WIDTH), vec(SSD_WIDTH), vec(LANES), vec(LANES)],
        out_shape=[jax.ShapeDtypeStruct((s, CONV_DIM), F32), jax.ShapeDtypeStruct((s, LANES), BF16),
                   jax.ShapeDtypeStruct((s, SSD_WIDTH), BF16), jax.ShapeDtypeStruct((1, SSD_WIDTH), F32),
                   jax.ShapeDtypeStruct((1, SSD_WIDTH), F32), jax.ShapeDtypeStruct((1, LANES), F32),
                   jax.ShapeDtypeStruct((1, LANES), F32)],
        scratch_shapes=[pltpu.VMEM((N_PAIRS, D_STATE, c), F32)],
        compiler_params=_params(("arbitrary",)),
    )(xc, proj, proj, ycore, dyall, states, dtb, alog, dskip_x, nw)


SB_BLOCK = 256
SB_SCALE = HEAD_DIM ** -0.5


def _sb_masks(i, j):
    t = SB_BLOCK
    valid = (_iota2((t, t), 1) + j * t) < (_iota2((t, t), 0) + i * t)
    return valid


def _sb_weights(qm, kb, valid, run_lk, strict_after):
    z = _nt(qm, kb)
    ls = -_softplus(-z)
    lk = jnp.where(valid, ls - z, 0.0)
    after = _split_dot(lk, strict_after, 2) + run_lk
    w = jnp.where(valid, jnp.exp(ls + after), 0.0)
    return z, lk, w


def sb_forward(proj, name):
    s = proj.shape[0]
    t = SB_BLOCK
    nq = s // t
    qoff, koff, voff = OFF_QKV // LANES, (OFF_QKV + SB_WIDTH) // LANES, (OFF_QKV + 2 * SB_WIDTH) // LANES

    def body(q_ref, k_ref, v_ref, o_ref):
        i = pl.program_id(1)
        left = _iota2((t, LANES), 1) < HEAD_DIM
        qv = q_ref[...] * SB_SCALE
        zero = jnp.zeros_like(qv)
        qms = (jnp.where(left, qv, zero).astype(BF16), jnp.where(left, zero, qv).astype(BF16))
        strict_after = (_iota2((t, t), 0) > _iota2((t, t), 1)).astype(BF16)

        def step(jj, carry):
            o, runs = carry[0], carry[1:]
            j = i - jj
            rows = pl.ds(pl.multiple_of(j * t, t), t)
            kb = k_ref[rows, :].astype(BF16)
            vv = v_ref[rows, :]
            valid = _sb_masks(i, j)
            new_runs = []
            for k in range(2):
                _, lk, w = _sb_weights(qms[k], kb, valid, runs[k], strict_after)
                vm = jnp.where(left if k == 0 else jnp.logical_not(left), vv, 0.0).astype(BF16)
                o = o + _split_dot(w, vm, 2)
                new_runs.append(runs[k] + jnp.sum(lk, axis=1, keepdims=True))
            return (o, *new_runs)

        init = (jnp.zeros((t, LANES), F32), jnp.zeros((t, 1), F32), jnp.zeros((t, 1), F32))
        o_ref[...] = lax.fori_loop(0, i + 1, step, init)[0]

    return pl.pallas_call(
        body, name=name, grid=(2, nq),
        in_specs=[pl.BlockSpec((t, LANES), lambda p, i: (i, qoff + p)),
                  pl.BlockSpec((s, LANES), lambda p, i: (0, koff + p)),
                  pl.BlockSpec((s, LANES), lambda p, i: (0, voff + p))],
        out_specs=pl.BlockSpec((t, LANES), lambda p, i: (i, p)),
        out_shape=jax.ShapeDtypeStruct((s, SB_WIDTH), F32),
        compiler_params=_params(("parallel", "arbitrary")),
    )(proj, proj, proj)


def sb_backward(proj, o, dyall, name):
    s = proj.shape[0]
    t = SB_BLOCK
    nq = s // t
    qoff, koff, voff = OFF_QKV // LANES, (OFF_QKV + SB_WIDTH) // LANES, (OFF_QKV + 2 * SB_WIDTH) // LANES

    def body(q_ref, k_ref, v_ref, o_ref, do_ref, dq_ref, dk_ref, dv_ref, dk_acc, dv_acc):
        dk_acc[...] = jnp.zeros_like(dk_acc)
        dv_acc[...] = jnp.zeros_like(dv_acc)
        left = _iota2((t, LANES), 1) < HEAD_DIM
        lane_masks = (left, jnp.logical_not(left))
        strict_after = (_iota2((t, t), 0) > _iota2((t, t), 1)).astype(BF16)
        from_here = (_iota2((t, t), 0) >= _iota2((t, t), 1)).astype(BF16)

        def query_block(i, _):
            qrows = pl.ds(pl.multiple_of(i * t, t), t)
            qv = q_ref[qrows, :] * SB_SCALE
            dov = do_ref[qrows, :]
            zero = jnp.zeros_like(qv)
            qb = qv.astype(BF16)
            dob = dov.astype(BF16)
            prod = dob.astype(F32) * o_ref[qrows, :]
            qms = [jnp.where(m, qv, zero).astype(BF16) for m in lane_masks]
            doms = [jnp.where(m, dov, zero).astype(BF16) for m in lane_masks]
            deltas = [jnp.sum(jnp.where(m, prod, zero), axis=1, keepdims=True) for m in lane_masks]

            def step(jj, carry):
                dq = carry[0]
                run_lk, run_e = carry[1:3], carry[3:5]
                j = i - jj
                rows = pl.ds(pl.multiple_of(j * t, t), t)
                kb = k_ref[rows, :].astype(BF16)
                vb = v_ref[rows, :].astype(BF16)
                valid = _sb_masks(i, j)
                dkj = jnp.zeros((t, LANES), F32)
                dvj = jnp.zeros((t, LANES), F32)
                new_lk, new_e = [], []
                for k in range(2):
                    z, lk, w = _sb_weights(qms[k], kb, valid, run_lk[k], strict_after)
                    sg = _sigmoid(z)
                    e = _nt(doms[k], vb) * w
                    before = deltas[k] - _split_dot(e, from_here, 2) - run_e[k]
                    dz = jnp.where(valid, e * (1.0 - sg) - sg * before, 0.0).astype(BF16)
                    m = lane_masks[k]
                    dvj = dvj + jnp.where(m, _tn(w.astype(BF16), dob), 0.0)
                    dkj = dkj + jnp.where(m, _tn(dz, qb), 0.0)
                    dq = dq + jnp.where(m, _nn(dz, kb), 0.0)
                    new_lk.append(run_lk[k] + jnp.sum(lk, axis=1, keepdims=True))
                    new_e.append(run_e[k] + jnp.sum(e, axis=1, keepdims=True))
                dk_acc[rows, :] += dkj
                dv_acc[rows, :] += dvj
                return (dq, *new_lk, *new_e)

            col = jnp.zeros((t, 1), F32)
            dq = lax.fori_loop(0, i + 1, step, (jnp.zeros((t, LANES), F32), col, col, col, col))[0]
            dq_ref[qrows, :] = (dq * SB_SCALE).astype(BF16)
            return 0

        lax.fori_loop(0, nq, query_block, 0)
        dk_ref[...] = dk_acc[...].astype(BF16)
        dv_ref[...] = dv_acc[...].astype(BF16)

    col = lambda off: pl.BlockSpec((s, LANES), lambda p: (0, off + p))
    return pl.pallas_call(
        body, name=name, grid=(2,),
        in_specs=[col(qoff), col(koff), col(voff), col(0), col(SSD_WIDTH // LANES)],
        out_specs=[col(0), col(0), col(0)],
        out_shape=[jax.ShapeDtypeStruct((s, SB_WIDTH), BF16)] * 3,
        scratch_shapes=[pltpu.VMEM((s, LANES), F32), pltpu.VMEM((s, LANES), F32)],
        compiler_params=_params(("parallel",)),
    )(proj, proj, proj, o, dyall)


def adamw(w, g, m, v, name):
    b, r, c = w.shape
    tr = r
    for cand in (512, 256, 128, 64, 32, 16, 8):
        if r % cand == 0 and r > cand:
            tr = cand
            break

    def body(w_ref, g_ref, m_ref, v_ref, d_ref, nm_ref, nv_ref):
        gv = g_ref[...]
        nm = ADAM_B1 * m_ref[...] + (1.0 - ADAM_B1) * gv
        nv = ADAM_B2 * v_ref[...] + (1.0 - ADAM_B2) * (gv * gv)
        m_hat = nm / (1.0 - ADAM_B1 ** ADAM_STEP)
        v_hat = nv / (1.0 - ADAM_B2 ** ADAM_STEP)
        d_ref[...] = -ADAM_LR * (m_hat / (jnp.sqrt(v_hat) + ADAM_EPS) + ADAM_WD * w_ref[...])
        nm_ref[...] = nm
        nv_ref[...] = nv

    blk = pl.BlockSpec((1, tr, c), lambda i, j: (i, j, 0))
    return pl.pallas_call(
        body, name=name, grid=(b, r // tr),
        in_specs=[blk] * 4, out_specs=[blk] * 3,
        out_shape=[jax.ShapeDtypeStruct(w.shape, F32)] * 3,
        compiler_params=_params(("parallel", "parallel")),
    )(w, g, m, v)


def _position():
    return lax.axis_index("x"), lax.axis_index("y"), lax.axis_index("c")


def _flipped(pos, flip):
    return tuple((1 - p) if f else p for p, f in zip(pos, flip))


FLIP_C = (0, 0, 1)
CHIP_FLIPS = {1: (0, 1, 0), 2: (1, 0, 0), 3: (1, 1, 0)}


def remote_exchange(src, plan, out_shape, name):
    n = len(plan)

    def body(src_ref, dst_ref, send_sems, recv_sems):
        pos = _position()
        copies = []
        for k, (src_ix, dst_ix, flip) in enumerate(plan):
            src_at = src_ix(*pos)
            cp = pltpu.make_async_remote_copy(
                src_ref=src_ref.at[src_at] if src_at else src_ref, dst_ref=dst_ref.at[dst_ix] if dst_ix else dst_ref,
                send_sem=send_sems.at[k], recv_sem=recv_sems.at[k],
                device_id=_flipped(pos, flip), device_id_type=MESH)
            cp.start()
            copies.append(cp)
        for cp in copies:
            cp.wait()

    return pl.pallas_call(
        body, name=name,
        in_specs=[pl.BlockSpec(memory_space=pl.ANY)], out_specs=pl.BlockSpec(memory_space=pl.ANY),
        out_shape=out_shape,
        scratch_shapes=[pltpu.SemaphoreType.DMA((n,)), pltpu.SemaphoreType.DMA((n,))],
    )(src)


def small_allreduce(v, name):
    r, c = v.shape

    def body(v_ref, o_ref, buf, send_sems, recv_sems):
        pos = _position()
        me = 4 * pos[0] + 2 * pos[1] + pos[2]
        buf[0] = v_ref[...]
        copies = []
        for f in range(1, 8):
            flip = ((f >> 2) & 1, (f >> 1) & 1, f & 1)
            cp = pltpu.make_async_remote_copy(
                src_ref=v_ref, dst_ref=buf.at[f], send_sem=send_sems.at[f - 1], recv_sem=recv_sems.at[f - 1],
                device_id=_flipped(pos, flip), device_id_type=MESH)
            cp.start()
            copies.append(cp)
        for cp in copies:
            cp.wait()
        acc = buf[me]
        for d in range(1, 8):
            acc = acc + buf[lax.bitwise_xor(me, d)]
        o_ref[...] = acc

    return pl.pallas_call(
        body, name=name,
        in_specs=[pl.BlockSpec(memory_space=pltpu.VMEM)], out_specs=pl.BlockSpec(memory_space=pltpu.VMEM),
        out_shape=jax.ShapeDtypeStruct((r, c), F32),
        scratch_shapes=[pltpu.VMEM((8, r, c), F32), pltpu.SemaphoreType.DMA((7,)), pltpu.SemaphoreType.DMA((7,))],
    )(v)


def add_pairs(g, recv, half, name):
    _, _, r, c = g.shape
    tr = _tile(r, 512)

    def body(half_ref, g_ref, r_ref, o_ref):
        o_ref[...] = (g_ref[...].astype(F32) + r_ref[...].astype(F32)).astype(BF16)

    return pl.pallas_call(
        body, name=name,
        grid_spec=pltpu.PrefetchScalarGridSpec(
            num_scalar_prefetch=1, grid=(N_CHIPS, 2, r // tr),
            in_specs=[pl.BlockSpec((1, 1, tr, c), lambda j, l, i, h: (j, 2 * h[0] + l, i, 0)),
                      pl.BlockSpec((1, 1, tr, c), lambda j, l, i, h: (j, l, i, 0))],
            out_specs=pl.BlockSpec((1, 1, tr, c), lambda j, l, i, h: (j, l, i, 0))),
        out_shape=jax.ShapeDtypeStruct(recv.shape, BF16),
        compiler_params=_params(("parallel", "parallel", "parallel")),
    )(half, g, recv)


def add_chips(p, recv, chip, name):
    _, _, r, c = p.shape
    tr = _tile(r, 512)

    def body(chip_ref, p_ref, r_ref, o_ref):
        acc = p_ref[0, 0].astype(F32)
        for k in range(N_CHIPS - 1):
            acc = acc + r_ref[k, 0].astype(F32)
        o_ref[0] = acc

    return pl.pallas_call(
        body, name=name,
        grid_spec=pltpu.PrefetchScalarGridSpec(
            num_scalar_prefetch=1, grid=(2, r // tr),
            in_specs=[pl.BlockSpec((1, 1, tr, c), lambda l, i, ch: (ch[0], l, i, 0)),
                      pl.BlockSpec((N_CHIPS - 1, 1, tr, c), lambda l, i, ch: (0, l, i, 0))],
            out_specs=pl.BlockSpec((1, tr, c), lambda l, i, ch: (l, i, 0))),
        out_shape=jax.ShapeDtypeStruct((2, r, c), F32),
        compiler_params=_params(("parallel", "parallel")),
    )(chip, p, recv)


FLAT_SPLITS = np.cumsum([0, ROWS_IN, ROWS_OUT, ROWS_FF, ROWS_FF, ROWS_FF])
HALF_LAYERS = DEPTH // 2


def _pack_shard(w_in, w_out, w_gate, w_up, w_down, dtype):
    d = w_in.shape[0]
    parts = [w_in.reshape(d, ROWS_IN, D_MODEL), w_out, w_gate.reshape(d, ROWS_FF, D_MODEL),
             w_up.reshape(d, ROWS_FF, D_MODEL), w_down,
             jnp.zeros((d, FLAT_ROWS - int(FLAT_SPLITS[-1]), D_MODEL), w_in.dtype)]
    return jnp.concatenate([p.astype(dtype) for p in parts], axis=1)


def _unpack_shard(flat):
    d = flat.shape[0]
    a = FLAT_SPLITS
    return (flat[:, a[0]:a[1]].reshape(d, D_MODEL, ROWS_IN), flat[:, a[1]:a[2]],
            flat[:, a[2]:a[3]].reshape(d, D_MODEL, ROWS_FF), flat[:, a[3]:a[4]].reshape(d, D_MODEL, ROWS_FF),
            flat[:, a[4]:a[5]])


def _permute_in(w):
    pad = jnp.zeros(w.shape[:-1] + (D_INP - D_IN,), w.dtype)
    return jnp.concatenate([w[..., :1536], w[..., 1544:2568], w[..., 1536:1544], pad], axis=-1)


def _unpermute_in(w):
    return jnp.concatenate([w[..., :1536], w[..., OFF_DT:OFF_DT + SSD_HEADS], w[..., 1536:OFF_DT]], axis=-1)


SMALL_NAMES = ("norm1_w", "conv_w", "conv_b", "dt_bias", "a_log", "d_skip", "ssd_norm_w", "pool_w", "pool_b",
               "pool_scale", "norm2_w", "final_norm_w")
SMALL_ROWS = 104


def _pack_small(parts):
    flat = jnp.concatenate([p.reshape(-1) for p in parts])
    return jnp.pad(flat, (0, SMALL_ROWS * D_MODEL - flat.shape[0])).reshape(SMALL_ROWS, D_MODEL)


def _unpack_small(flat, shapes):
    flat = flat.reshape(-1)
    out, at = [], 0
    for shp in shapes:
        n = int(np.prod(shp))
        out.append(flat[at:at + n].reshape(shp))
        at += n
    return out


def kernel(x, norm1_w, w_in, conv_w, conv_b, dt_bias, a_log, d_skip, ssd_norm_w, pool_w, pool_b, pool_scale, w_out, norm2_w, w_gate, w_up, w_down, final_norm_w, loss_target, m_norm1_w, m_w_in, m_conv_w, m_conv_b, m_dt_bias, m_a_log, m_d_skip, m_ssd_norm_w, m_pool_w, m_pool_b, m_pool_scale, m_w_out, m_norm2_w, m_w_gate, m_w_up, m_w_down, m_final_norm_w, v_norm1_w, v_w_in, v_conv_w, v_conv_b, v_dt_bias, v_a_log, v_d_skip, v_ssd_norm_w, v_pool_w, v_pool_b, v_pool_scale, v_w_out, v_norm2_w, v_w_gate, v_w_up, v_w_down, v_final_norm_w):
    px, py, pc = _position()
    chip = 2 * px + py
    chip_arr = jnp.reshape(chip, (1,)).astype(jnp.int32)
    half_arr = jnp.reshape(pc, (1,)).astype(jnp.int32)
    r = FLAT_ROWS

    mine = _pack_shard(w_in, w_out, w_gate, w_up, w_down, BF16)
    over_ici = remote_exchange(
        mine, [(lambda x_, y_, c_: (pl.ds(HALF_LAYERS * c_, HALF_LAYERS),), (f - 1,), CHIP_FLIPS[f]) for f in (1, 2, 3)],
        jax.ShapeDtypeStruct((N_CHIPS - 1, HALF_LAYERS, r, D_MODEL), BF16), "gather_ici")
    over_d2d = remote_exchange(
        over_ici, [(lambda x_, y_, c_: (), (), FLIP_C)],
        jax.ShapeDtypeStruct(over_ici.shape, BF16), "gather_d2d")
    low = jnp.where(pc == 0, over_ici, over_d2d)
    high = jnp.where(pc == 0, over_d2d, over_ici)
    by_flip = jnp.concatenate([mine[None], jnp.concatenate([low, high], axis=1)], axis=0)
    by_chip = [lax.dynamic_index_in_dim(by_flip, lax.bitwise_xor(chip, j), 0, keepdims=False) for j in range(N_CHIPS)]
    parts = [_unpack_shard(s) for s in by_chip]
    w_in_f = _permute_in(jnp.concatenate([p[0] for p in parts], axis=2))
    w_out_f = jnp.concatenate([p[1] for p in parts], axis=1)
    w_gate_f = jnp.concatenate([p[2] for p in parts], axis=2)
    w_up_f = jnp.concatenate([p[3] for p in parts], axis=2)
    w_down_f = jnp.concatenate([p[4] for p in parts], axis=1)

    pad_heads = lambda v: jnp.pad(v, ((0, 0), (0, LANES - SSD_HEADS)))[:, None, :]
    dtb, alog = pad_heads(dt_bias), pad_heads(a_log)
    dskip_x = jnp.repeat(d_skip, HEAD_DIM, axis=1)[:, None, :]
    eye = jnp.eye(len(POOL_WINDOWS), dtype=F32)
    wbd = (pool_w[:, :, :, None, :] * eye[None, :, None, :, None]).reshape(DEPTH, POOL_WIDTH, POOL_WIDTH).astype(BF16)
    pool_b2 = pool_b.reshape(DEPTH, 1, POOL_WIDTH)
    cw_cols = lax.dynamic_update_slice(jnp.zeros((DEPTH, CONV_WIDTH, CONV_DIM), F32), conv_w,
                                       (0, 0, chip * (CONV_DIM // N_CHIPS)))
    cw_cols = jnp.where(pc == 0, cw_cols, 0.0)
    cw_rows = (DEPTH * CONV_WIDTH * CONV_DIM) // D_MODEL
    conv_w_f = small_allreduce(jnp.pad(cw_cols.reshape(cw_rows, D_MODEL), ((0, 8), (0, 0))), "gather_conv_w")
    conv_w_f = conv_w_f[:cw_rows].reshape(DEPTH, CONV_WIDTH, CONV_DIM)
    cw8 = jnp.pad(conv_w_f, ((0, 0), (0, 8 - CONV_WIDTH), (0, 0)))

    h = x[0]
    saved = []
    for l in range(DEPTH):
        proj = rms_matmul(h, norm1_w[l][None], w_in_f[l], "in_proj")
        xc = conv_forward(proj, cw8[l], conv_b[l][None], "conv_fwd")
        y_ssd, ycore, states = ssd_forward(proj, xc, dtb[l], alog[l], dskip_x[l], ssd_norm_w[l][None], "ssd_fwd")
        o_sb = sb_forward(proj, "sb_fwd")
        y_pool = pool_forward(proj, wbd[l], pool_b2[l], pool_scale[l][None], "pool_fwd")
        y_all = jnp.concatenate([y_ssd, o_sb.astype(BF16), y_pool], axis=1)
        x1 = matmul_residual(y_all, w_out_f[l], h, "out_proj")
        x2, g, u = ffn_forward(x1, norm2_w[l][None], w_gate_f[l], w_up_f[l], w_down_f[l], "ffn_fwd")
        saved.append((h, proj, xc, ycore, states, o_sb, y_all, x1, g, u))
        h = x2

    loss_part, dx, dxb, d_final = loss_head(h, final_norm_w[None], loss_target[0], "loss_head")
    loss = lax.psum(loss_part[0, 0], ("x", "y", "c"))

    small = {n: [None] * DEPTH for n in SMALL_NAMES if n != "final_norm_w"}
    big = [None] * DEPTH
    for l in reversed(range(DEPTH)):
        xin, proj, xc, ycore, states, o_sb, y_all, x1, g, u = saved[l]
        dg, du, act = ffn_backward_act(dxb, g, u, w_down_f[l], "ffn_bwd_act")
        dx1, dx1b, h2, dn2 = rms_backward_nt([dg, du], [w_gate_f[l], w_up_f[l]], x1, norm2_w[l][None], dx,
                                             "ffn_bwd_norm", 256)
        dw_down = matmul_tn(act, dxb, "dw_down")
        dw_gate = matmul_tn(h2, dg, "dw_gate")
        dw_up = matmul_tn(h2, du, "dw_up")
        dyall = matmul_nt(dx1b, w_out_f[l], "out_proj_bwd")
        dw_out = matmul_tn(y_all, dx1b, "dw_out")
        dp, dwbd, dpb, dps = pool_backward(proj, dyall, wbd[l], pool_b2[l], pool_scale[l][None], "pool_bwd")
        dq, dk, dv = sb_backward(proj, o_sb, dyall, "sb_bwd")
        dxc, ddt, dz, dsn, ddsk, ddtb, dalog = ssd_backward(proj, xc, ycore, dyall, states, dtb[l], alog[l],
                                                           dskip_x[l], ssd_norm_w[l][None], "ssd_bwd")
        dxbc, dcw, dcb = conv_backward(proj, dxc, cw8[l], conv_b[l][None], "conv_bwd")
        dproj = jnp.concatenate([dz, dxbc, dq, dk, dv, dp, ddt], axis=1)
        dx, dxb, h1, dn1 = rms_backward_nt([dproj], [w_in_f[l]], xin, norm1_w[l][None], dx1, "in_proj_bwd", 256)
        dw_in = _unpermute_in(matmul_tn(h1, dproj, "dw_in"))
        big[l] = (dw_in, dw_out, dw_gate, dw_up, dw_down)
        small["norm1_w"][l] = dn1[0]
        small["conv_w"][l] = dcw[:CONV_WIDTH]
        small["conv_b"][l] = dcb[0]
        small["dt_bias"][l] = ddtb[0, :SSD_HEADS]
        small["a_log"][l] = dalog[0, :SSD_HEADS]
        small["d_skip"][l] = ddsk.reshape(SSD_HEADS, HEAD_DIM).sum(axis=1)
        small["ssd_norm_w"][l] = dsn[0]
        small["pool_w"][l] = jnp.stack([dwbd[64 * k:64 * k + 64, 64 * k:64 * k + 64] for k in range(len(POOL_WINDOWS))])
        small["pool_b"][l] = dpb.reshape(len(POOL_WINDOWS), -1)
        small["pool_scale"][l] = dps[0]
        small["norm2_w"][l] = dn2[0]
    grad_x = dx[None]

    def for_chip(j):
        dw_in, dw_out, dw_gate, dw_up, dw_down = [jnp.stack([big[l][k] for l in range(DEPTH)]) for k in range(5)]
        return _pack_shard(dw_in[:, :, ROWS_IN * j:ROWS_IN * (j + 1)], dw_out[:, ROWS_OUT * j:ROWS_OUT * (j + 1)],
                           dw_gate[:, :, ROWS_FF * j:ROWS_FF * (j + 1)], dw_up[:, :, ROWS_FF * j:ROWS_FF * (j + 1)],
                           dw_down[:, ROWS_FF * j:ROWS_FF * (j + 1)], BF16)

    contrib = jnp.stack([for_chip(j) for j in range(N_CHIPS)])
    from_sibling = remote_exchange(
        contrib, [(lambda x_, y_, c_, j=j: (j, pl.ds(HALF_LAYERS * (1 - c_), HALF_LAYERS)), (j,), FLIP_C)
                  for j in range(N_CHIPS)],
        jax.ShapeDtypeStruct((N_CHIPS, HALF_LAYERS, r, D_MODEL), BF16), "reduce_d2d")
    chip_sum = add_pairs(contrib, from_sibling, half_arr, "reduce_add_pairs")
    from_chips = remote_exchange(
        chip_sum, [(lambda x_, y_, c_, f=f: (lax.bitwise_xor(2 * x_ + y_, f),), (f - 1,), CHIP_FLIPS[f]) for f in (1, 2, 3)],
        jax.ShapeDtypeStruct((N_CHIPS - 1, HALF_LAYERS, r, D_MODEL), BF16), "reduce_ici")
    my_layers = add_chips(chip_sum, from_chips, chip_arr, "reduce_add_chips")
    other_layers = remote_exchange(
        my_layers, [(lambda x_, y_, c_: (), (), FLIP_C)], jax.ShapeDtypeStruct(my_layers.shape, F32), "reduce_swap")
    g_flat = jnp.concatenate([jnp.where(pc == 0, my_layers, other_layers), jnp.where(pc == 0, other_layers, my_layers)], axis=0)
    g_big = _unpack_shard(g_flat)

    small_parts = [d_final if n == "final_norm_w" else jnp.stack(small[n]) for n in SMALL_NAMES]
    small_shapes = [p.shape for p in small_parts]
    g_small = dict(zip(SMALL_NAMES, _unpack_small(small_allreduce(_pack_small(small_parts), "reduce_small"), small_shapes)))
    g_small["final_norm_w"] = g_small["final_norm_w"].reshape(final_norm_w.shape)
    g_small["conv_w"] = lax.dynamic_slice_in_dim(g_small["conv_w"], chip * (CONV_DIM // N_CHIPS), CONV_DIM // N_CHIPS, axis=2)

    given = dict(norm1_w=(norm1_w, m_norm1_w, v_norm1_w), w_in=(w_in, m_w_in, v_w_in), conv_w=(conv_w, m_conv_w, v_conv_w),
                 conv_b=(conv_b, m_conv_b, v_conv_b), dt_bias=(dt_bias, m_dt_bias, v_dt_bias), a_log=(a_log, m_a_log, v_a_log),
                 d_skip=(d_skip, m_d_skip, v_d_skip), ssd_norm_w=(ssd_norm_w, m_ssd_norm_w, v_ssd_norm_w),
                 pool_w=(pool_w, m_pool_w, v_pool_w), pool_b=(pool_b, m_pool_b, v_pool_b),
                 pool_scale=(pool_scale, m_pool_scale, v_pool_scale), w_out=(w_out, m_w_out, v_w_out),
                 norm2_w=(norm2_w, m_norm2_w, v_norm2_w), w_gate=(w_gate, m_w_gate, v_w_gate), w_up=(w_up, m_w_up, v_w_up),
                 w_down=(w_down, m_w_down, v_w_down), final_norm_w=(final_norm_w, m_final_norm_w, v_final_norm_w))
    order = ("norm1_w", "w_in", "conv_w", "conv_b", "dt_bias", "a_log", "d_skip", "ssd_norm_w", "pool_w", "pool_b",
             "pool_scale", "w_out", "norm2_w", "w_gate", "w_up", "w_down", "final_norm_w")
    grads = dict(g_small)
    grads.update(zip(("w_in", "w_out", "w_gate", "w_up", "w_down"), g_big))
    results = {}
    for n in ("w_in", "w_out", "w_gate", "w_up", "w_down"):
        w, m, v = given[n]
        results[n] = adamw(w, grads[n], m, v, "adamw_" + n)
    small_shapes = [given[n][0].shape for n in SMALL_NAMES]
    packed = [_pack_small([given[n][k] for n in SMALL_NAMES])[None] for k in range(3)]
    packed_g = _pack_small([grads[n] for n in SMALL_NAMES])[None]
    small_out = adamw(packed[0], packed_g, packed[1], packed[2], "adamw_small")
    small_out = [_unpack_small(o[0], small_shapes) for o in small_out]
    for i, n in enumerate(SMALL_NAMES):
        results[n] = tuple(small_out[k][i] for k in range(3))

    return (loss, grad_x, *[grads[n] for n in order], *[results[n][0] for n in order],
            *[results[n][1] for n in order], *[results[n][2] for n in order])
```

```python
import numpy as np
import jax
import jax.numpy as jnp
from jax import lax
from jax.experimental import pallas as pl
from jax.experimental.pallas import tpu as pltpu

F32 = jnp.float32
BF16 = jnp.bfloat16
MESH = pl.DeviceIdType.MESH
ANY = pl.BlockSpec(memory_space=pl.ANY)

D_MODEL = 1024
DEPTH = 4
EPS = 1e-6
SSD_WIDTH = 512
SSD_HEADS = 8
HEAD_DIM = 64
D_STATE = 128
CHUNK = 128
CONV_WIDTH = 4
CONV_DIM = 1024
SB_WIDTH = 256
POOL_WIDTH = 256
POOL_WINDOWS = (2, 4, 8, 16)
D_FF = 2816
D_IN = 2568
N_CHIPS = 4
OFF_QKV, OFF_Z, OFF_DT, OFF_XBC, OFF_P = 0, 768, 1280, 1536, 2560
D_INP = 2816
ZDT = 768
SHARD_IN, SHARD_OUT, SHARD_FF = 672, 256, 704
COLS_IN = 642
ADAM_LR, ADAM_B1, ADAM_B2, ADAM_EPS, ADAM_WD, ADAM_STEP = 0.001, 0.9, 0.999, 1e-08, 0.01, 10
LANES = 128
VMEM_LIMIT = 56 * 1024 * 1024


def _params(sem=None):
    return pltpu.CompilerParams(dimension_semantics=sem, vmem_limit_bytes=VMEM_LIMIT)


def _tile(n, cap):
    best = None
    for t in range(LANES, min(n, cap) + 1, LANES):
        if n % t == 0:
            best = t
    assert best is not None, (n, cap)
    return best


def _nt(a, b):
    return lax.dot_general(a, b, (((1,), (1,)), ((), ())), preferred_element_type=F32)


def _tn(a, b):
    return lax.dot_general(a, b, (((0,), (0,)), ((), ())), preferred_element_type=F32)


def _nn(a, b):
    return jnp.dot(a, b, preferred_element_type=F32)


def _split_dot(a, b_exact, terms=3, dot=_nn):
    acc = None
    rest = a
    for _ in range(terms):
        hi = rest.astype(BF16)
        part = dot(hi, b_exact)
        acc = part if acc is None else acc + part
        rest = rest - hi.astype(F32)
    return acc


def _split_dot_left(a_exact, b, terms=3):
    acc = None
    rest = b
    for _ in range(terms):
        hi = rest.astype(BF16)
        part = _nn(a_exact, hi)
        acc = part if acc is None else acc + part
        rest = rest - hi.astype(F32)
    return acc


def _sigmoid(x):
    return 1.0 / (1.0 + jnp.exp(-x))


def _softplus(x):
    return jnp.maximum(x, 0.0) + jnp.log(1.0 + jnp.exp(-jnp.abs(x)))


def _iota2(shape, dim):
    return lax.broadcasted_iota(jnp.int32, shape, dim)


def rms_matmul(x, nw, wt, name):
    s, d = x.shape
    n = wt.shape[0]
    tm, tn = _tile(s, 512), _tile(n, 1408)

    def body(x_ref, nw_ref, w_ref, o_ref, h_ref):
        @pl.when(pl.program_id(1) == 0)
        def _():
            xv = x_ref[...]
            r = lax.rsqrt(jnp.mean(xv * xv, axis=-1, keepdims=True) + EPS)
            h_ref[...] = (xv * r * nw_ref[...]).astype(BF16)
        o_ref[...] = _nt(h_ref[...], w_ref[...])

    return pl.pallas_call(
        body, name=name, grid=(s // tm, n // tn),
        in_specs=[pl.BlockSpec((tm, d), lambda i, j: (i, 0)), pl.BlockSpec((1, d), lambda i, j: (0, 0)),
                  pl.BlockSpec((tn, d), lambda i, j: (j, 0))],
        out_specs=pl.BlockSpec((tm, tn), lambda i, j: (i, j)),
        out_shape=jax.ShapeDtypeStruct((s, n), F32),
        scratch_shapes=[pltpu.VMEM((tm, d), BF16)],
        compiler_params=_params(("parallel", "arbitrary")),
    )(x, nw, wt)


def matmul_residual(a, w, res, name):
    s, k = a.shape
    n = w.shape[1]
    tm, tn = _tile(s, 512), _tile(n, 512)

    def body(a_ref, w_ref, r_ref, o_ref):
        o_ref[...] = r_ref[...] + _nn(a_ref[...], w_ref[...])

    return pl.pallas_call(
        body, name=name, grid=(s // tm, n // tn),
        in_specs=[pl.BlockSpec((tm, k), lambda i, j: (i, 0)), pl.BlockSpec((k, tn), lambda i, j: (0, j)),
                  pl.BlockSpec((tm, tn), lambda i, j: (i, j))],
        out_specs=pl.BlockSpec((tm, tn), lambda i, j: (i, j)),
        out_shape=jax.ShapeDtypeStruct((s, n), F32),
        compiler_params=_params(("parallel", "parallel")),
    )(a, w, res)


def matmul_nt(a, w, name, out_dtype=F32):
    s, n = a.shape
    k = w.shape[0]
    tm, tk = _tile(s, 512), _tile(k, 512)

    def body(a_ref, w_ref, o_ref):
        o_ref[...] = _nt(a_ref[...], w_ref[...]).astype(out_dtype)

    return pl.pallas_call(
        body, name=name, grid=(s // tm, k // tk),
        in_specs=[pl.BlockSpec((tm, n), lambda i, j: (i, 0)), pl.BlockSpec((tk, n), lambda i, j: (j, 0))],
        out_specs=pl.BlockSpec((tm, tk), lambda i, j: (i, j)),
        out_shape=jax.ShapeDtypeStruct((s, k), out_dtype),
        compiler_params=_params(("parallel", "parallel")),
    )(a, w)


def matmul_tn(a, b, name, stack, layer):
    s, m = a.shape
    n = b.shape[1]
    tm, tn = _tile(m, 512), _tile(n, 512)

    def body(a_ref, b_ref, *rest):
        rest[-1][0] = _tn(a_ref[...], b_ref[...]).astype(BF16)

    ins = [a, b] if stack is None else [a, b, stack]
    return pl.pallas_call(
        body, name=name, grid=(m // tm, n // tn),
        in_specs=[pl.BlockSpec((s, tm), lambda i, j: (0, i)), pl.BlockSpec((s, tn), lambda i, j: (0, j))] + [ANY] * (len(ins) - 2),
        out_specs=pl.BlockSpec((1, tm, tn), lambda i, j: (layer, i, j)),
        out_shape=jax.ShapeDtypeStruct((DEPTH, m, n), BF16),
        input_output_aliases={} if stack is None else {2: 0},
        compiler_params=_params(("parallel", "parallel")),
    )(*ins)


def ffn_forward(x1, nw, wgt, wut, wd, name):
    s, d = x1.shape
    f = wgt.shape[0]
    tm, tf = _tile(s, 1024), _tile(f, 256)

    def body(x_ref, nw_ref, wg_ref, wu_ref, wd_ref, o_ref, g_ref, u_ref, h_ref, acc_ref):
        j = pl.program_id(1)

        @pl.when(j == 0)
        def _():
            xv = x_ref[...]
            r = lax.rsqrt(jnp.mean(xv * xv, axis=-1, keepdims=True) + EPS)
            h_ref[...] = (xv * r * nw_ref[...]).astype(BF16)
            acc_ref[...] = xv

        h = h_ref[...]
        g = _nt(h, wg_ref[...])
        u = _nt(h, wu_ref[...])
        g_ref[...] = g.astype(BF16)
        u_ref[...] = u.astype(BF16)
        a = (g * _sigmoid(g) * u).astype(BF16)
        acc_ref[...] += _nn(a, wd_ref[...])

        @pl.when(j == pl.num_programs(1) - 1)
        def _():
            o_ref[...] = acc_ref[...]

    wblk = pl.BlockSpec((tf, d), lambda i, j: (j, 0))
    return pl.pallas_call(
        body, name=name, grid=(s // tm, f // tf),
        in_specs=[pl.BlockSpec((tm, d), lambda i, j: (i, 0)), pl.BlockSpec((1, d), lambda i, j: (0, 0)), wblk, wblk, wblk],
        out_specs=[pl.BlockSpec((tm, d), lambda i, j: (i, 0)), pl.BlockSpec((tm, tf), lambda i, j: (i, j)),
                   pl.BlockSpec((tm, tf), lambda i, j: (i, j))],
        out_shape=[jax.ShapeDtypeStruct((s, d), F32), jax.ShapeDtypeStruct((s, f), BF16),
                   jax.ShapeDtypeStruct((s, f), BF16)],
        scratch_shapes=[pltpu.VMEM((tm, d), BF16), pltpu.VMEM((tm, d), F32)],
        compiler_params=_params(("parallel", "arbitrary")),
    )(x1, nw, wgt, wut, wd)


def ffn_backward_act(dx2, g, u, wd, name):
    s, d = dx2.shape
    f = wd.shape[0]
    tm, tf = _tile(s, 512), _tile(f, 1408)

    def body(dx_ref, g_ref, u_ref, wd_ref, dg_ref, du_ref, a_ref):
        da = _nt(dx_ref[...], wd_ref[...])
        gv = g_ref[...].astype(F32)
        uv = u_ref[...].astype(F32)
        sg = _sigmoid(gv)
        silu = gv * sg
        dg_ref[...] = (da * uv * (sg * (1.0 + gv * (1.0 - sg)))).astype(BF16)
        du_ref[...] = (da * silu).astype(BF16)
        a_ref[...] = (silu * uv).astype(BF16)

    blk = pl.BlockSpec((tm, tf), lambda i, j: (i, j))
    return pl.pallas_call(
        body, name=name, grid=(s // tm, f // tf),
        in_specs=[pl.BlockSpec((tm, d), lambda i, j: (i, 0)), blk, blk, pl.BlockSpec((tf, d), lambda i, j: (j, 0))],
        out_specs=[blk, blk, blk],
        out_shape=[jax.ShapeDtypeStruct((s, f), BF16)] * 3,
        compiler_params=_params(("parallel", "parallel")),
    )(dx2, g, u, wd)


def rms_backward(dzs, wts, x, nw, dres, name, tm):
    s, d = x.shape
    nz = len(dzs)

    def body(*refs):
        dz_refs, w_refs = refs[:nz], refs[nz:2 * nz]
        x_ref, nw_ref, dres_ref, dx_ref, dxb_ref, h_ref, dnw_ref = refs[2 * nz:]
        dh = _nn(dz_refs[0][...], w_refs[0][...])
        for k in range(1, nz):
            dh = dh + _nn(dz_refs[k][...], w_refs[k][...])
        xv = x_ref[...]
        r = lax.rsqrt(jnp.mean(xv * xv, axis=-1, keepdims=True) + EPS)
        xhat = xv * r
        nwv = nw_ref[...]
        h_ref[...] = (xhat * nwv).astype(BF16)

        @pl.when(pl.program_id(0) == 0)
        def _():
            dnw_ref[...] = jnp.zeros_like(dnw_ref)

        dnw_ref[...] += jnp.sum(dh * xhat, axis=0, keepdims=True)
        gdh = dh * nwv
        dx = dres_ref[...] + r * (gdh - xhat * jnp.mean(gdh * xhat, axis=-1, keepdims=True))
        dx_ref[...] = dx
        dxb_ref[...] = dx.astype(BF16)

    row = pl.BlockSpec((tm, d), lambda i: (i, 0))
    in_specs = [pl.BlockSpec((tm, dz.shape[1]), lambda i: (i, 0)) for dz in dzs]
    in_specs += [pl.BlockSpec(w.shape, lambda i: (0, 0)) for w in wts]
    in_specs += [row, pl.BlockSpec((1, d), lambda i: (0, 0)), row]
    return pl.pallas_call(
        body, name=name, grid=(s // tm,),
        in_specs=in_specs,
        out_specs=[row, row, row, pl.BlockSpec((1, d), lambda i: (0, 0))],
        out_shape=[jax.ShapeDtypeStruct((s, d), F32), jax.ShapeDtypeStruct((s, d), BF16),
                   jax.ShapeDtypeStruct((s, d), BF16), jax.ShapeDtypeStruct((1, d), F32)],
        compiler_params=_params(("arbitrary",)),
    )(*dzs, *wts, x, nw, dres)


def loss_head(x, nw, target, name):
    s, d = x.shape
    tm = _tile(s, 512)

    def body(x_ref, nw_ref, t_ref, loss_ref, dx_ref, dxb_ref, dnw_ref):
        xv = x_ref[...]
        r = lax.rsqrt(jnp.mean(xv * xv, axis=-1, keepdims=True) + EPS)
        xhat = xv * r
        nwv = nw_ref[...]
        err = xhat * nwv - t_ref[...]

        @pl.when(pl.program_id(0) == 0)
        def _():
            dnw_ref[...] = jnp.zeros_like(dnw_ref)
            loss_ref[...] = jnp.zeros_like(loss_ref)

        part = jnp.sum(jnp.sum(err * err, axis=-1, keepdims=True), axis=0, keepdims=True) * (0.5 / d)
        loss_ref[...] += jnp.broadcast_to(part, loss_ref.shape)
        dout = err * (1.0 / d)
        dnw_ref[...] += jnp.sum(dout * xhat, axis=0, keepdims=True)
        gdh = dout * nwv
        dx = r * (gdh - xhat * jnp.mean(gdh * xhat, axis=-1, keepdims=True))
        dx_ref[...] = dx
        dxb_ref[...] = dx.astype(BF16)

    row = pl.BlockSpec((tm, d), lambda i: (i, 0))
    return pl.pallas_call(
        body, name=name, grid=(s // tm,),
        in_specs=[row, pl.BlockSpec((1, d), lambda i: (0, 0)), row],
        out_specs=[pl.BlockSpec((1, LANES), lambda i: (0, 0)), row, row, pl.BlockSpec((1, d), lambda i: (0, 0))],
        out_shape=[jax.ShapeDtypeStruct((1, LANES), F32), jax.ShapeDtypeStruct((s, d), F32),
                   jax.ShapeDtypeStruct((s, d), BF16), jax.ShapeDtypeStruct((1, d), F32)],
        compiler_params=_params(("arbitrary",)),
    )(x, nw, target)


def _shift_down(x, k):
    return jnp.where(_iota2(x.shape, 0) >= k, pltpu.roll(x, k, axis=0), 0.0)


def _shift_up(x, k):
    s = x.shape[0]
    return jnp.where(_iota2(x.shape, 0) < s - k, pltpu.roll(x, s - k, axis=0), 0.0)


CONV_TILE = 256


def conv_forward(proj, cw, cb, name):
    s = proj.shape[0]
    tn = CONV_TILE
    off = OFF_XBC // tn

    def body(u_ref, w_ref, b_ref, o_ref):
        u = u_ref[...]
        pre = b_ref[...] + w_ref[CONV_WIDTH - 1:CONV_WIDTH, :] * u
        for i in range(CONV_WIDTH - 1):
            pre = pre + w_ref[i:i + 1, :] * _shift_down(u, CONV_WIDTH - 1 - i)
        o_ref[...] = pre * _sigmoid(pre)

    return pl.pallas_call(
        body, name=name, grid=(CONV_DIM // tn,),
        in_specs=[pl.BlockSpec((s, tn), lambda j: (0, off + j)), pl.BlockSpec((8, tn), lambda j: (0, j)),
                  pl.BlockSpec((1, tn), lambda j: (0, j))],
        out_specs=pl.BlockSpec((s, tn), lambda j: (0, j)),
        out_shape=jax.ShapeDtypeStruct((s, CONV_DIM), F32),
        compiler_params=_params(("parallel",)),
    )(proj, cw, cb)


def conv_backward(proj, dxc, cw, cb, dproj, name):
    s = proj.shape[0]
    tn = CONV_TILE
    off = OFF_XBC // tn

    def body(u_ref, d_ref, w_ref, b_ref, _, du_ref, dw_ref, db_ref):
        u = u_ref[...]
        shifted = [_shift_down(u, CONV_WIDTH - 1 - i) for i in range(CONV_WIDTH - 1)] + [u]
        pre = b_ref[...] + w_ref[CONV_WIDTH - 1:CONV_WIDTH, :] * u
        for i in range(CONV_WIDTH - 1):
            pre = pre + w_ref[i:i + 1, :] * shifted[i]
        sg = _sigmoid(pre)
        dpre = d_ref[...] * (sg * (1.0 + pre * (1.0 - sg)))
        du = w_ref[CONV_WIDTH - 1:CONV_WIDTH, :] * dpre
        for i in range(CONV_WIDTH - 1):
            du = du + w_ref[i:i + 1, :] * _shift_up(dpre, CONV_WIDTH - 1 - i)
        du_ref[...] = du.astype(BF16)
        rows = [jnp.sum(dpre * shifted[i], axis=0, keepdims=True) for i in range(CONV_WIDTH)]
        rows.append(jnp.zeros((8 - CONV_WIDTH, tn), F32))
        dw_ref[...] = jnp.concatenate(rows, axis=0)
        db_ref[...] = jnp.sum(dpre, axis=0, keepdims=True)

    return pl.pallas_call(
        body, name=name, grid=(CONV_DIM // tn,),
        in_specs=[pl.BlockSpec((s, tn), lambda j: (0, off + j)), pl.BlockSpec((s, tn), lambda j: (0, j)),
                  pl.BlockSpec((8, tn), lambda j: (0, j)), pl.BlockSpec((1, tn), lambda j: (0, j)), ANY],
        out_specs=[pl.BlockSpec((s, tn), lambda j: (0, off + j)), pl.BlockSpec((8, tn), lambda j: (0, j)),
                   pl.BlockSpec((1, tn), lambda j: (0, j))],
        out_shape=[jax.ShapeDtypeStruct(dproj.shape, BF16), jax.ShapeDtypeStruct((8, CONV_DIM), F32),
                   jax.ShapeDtypeStruct((1, CONV_DIM), F32)],
        input_output_aliases={4: 0},
        compiler_params=_params(("parallel",)),
    )(proj, dxc, cw, cb, dproj)


def _pool_lane_window(shape):
    grp = _iota2(shape, 1) // (POOL_WIDTH // len(POOL_WINDOWS))
    win = jnp.full(shape, POOL_WINDOWS[-1], jnp.int32)
    for gi in range(len(POOL_WINDOWS) - 2, -1, -1):
        win = jnp.where(grp == gi, POOL_WINDOWS[gi], win)
    return grp, win


def _pool_select(grp, sums):
    out = sums[-1]
    for gi in range(len(sums) - 2, -1, -1):
        out = jnp.where(grp == gi, sums[gi], out)
    return out


def _pool_pooled(p):
    grp, win = _pool_lane_window(p.shape)
    inv_count = 1.0 / jnp.minimum(_iota2(p.shape, 0) + 1, win).astype(F32)
    sums, acc, k = [], p, 1
    for _ in POOL_WINDOWS:
        acc = acc + _shift_down(acc, k)
        sums.append(acc)
        k *= 2
    return _pool_select(grp, sums) * inv_count - p, grp, inv_count


def pool_forward(proj, wbd, pb, ps, y_all, name):
    s = proj.shape[0]

    def body(p_ref, w_ref, b_ref, s_ref, _, o_ref):
        pooled, _, _ = _pool_pooled(p_ref[...])
        mixed = _nn(pooled.astype(BF16), w_ref[...]) + b_ref[...]
        o_ref[...] = (mixed * s_ref[...]).astype(BF16)

    vec = pl.BlockSpec((1, POOL_WIDTH), lambda j: (0, 0))
    return pl.pallas_call(
        body, name=name, grid=(1,),
        in_specs=[pl.BlockSpec((s, POOL_WIDTH), lambda j: (0, OFF_P // POOL_WIDTH)),
                  pl.BlockSpec((POOL_WIDTH, POOL_WIDTH), lambda j: (0, 0)), vec, vec, ANY],
        out_specs=pl.BlockSpec((s, POOL_WIDTH), lambda j: (0, (SSD_WIDTH + SB_WIDTH) // POOL_WIDTH)),
        out_shape=jax.ShapeDtypeStruct(y_all.shape, BF16),
        input_output_aliases={4: 0},
        compiler_params=_params(("arbitrary",)),
    )(proj, wbd, pb, ps, y_all)


def pool_backward(proj, dyall, wbd, pb, ps, dproj, name):
    s = proj.shape[0]

    def body(p_ref, dy_ref, w_ref, b_ref, s_ref, _, dp_ref, dw_ref, db_ref, ds_ref):
        pooled, grp, inv_count = _pool_pooled(p_ref[...])
        pooled_b = pooled.astype(BF16)
        mixed = _nn(pooled_b, w_ref[...]) + b_ref[...]
        dy = dy_ref[...]
        ds_ref[...] = jnp.sum(dy * mixed, axis=0, keepdims=True)
        dmixed = dy * s_ref[...]
        db_ref[...] = jnp.sum(dmixed, axis=0, keepdims=True)
        dmixed_b = dmixed.astype(BF16)
        dw_ref[...] = _tn(pooled_b, dmixed_b)
        dpooled = _nt(dmixed_b, w_ref[...])
        sums, acc, k = [], dpooled * inv_count, 1
        for _ in POOL_WINDOWS:
            acc = acc + _shift_up(acc, k)
            sums.append(acc)
            k *= 2
        dp_ref[...] = (_pool_select(grp, sums) - dpooled).astype(BF16)

    vec = pl.BlockSpec((1, POOL_WIDTH), lambda j: (0, 0))
    mat = pl.BlockSpec((POOL_WIDTH, POOL_WIDTH), lambda j: (0, 0))
    pcol = pl.BlockSpec((s, POOL_WIDTH), lambda j: (0, OFF_P // POOL_WIDTH))
    return pl.pallas_call(
        body, name=name, grid=(1,),
        in_specs=[pcol, pl.BlockSpec((s, POOL_WIDTH), lambda j: (0, (SSD_WIDTH + SB_WIDTH) // POOL_WIDTH)), mat, vec, vec, ANY],
        out_specs=[pcol, mat, vec, vec],
        out_shape=[jax.ShapeDtypeStruct(dproj.shape, BF16), jax.ShapeDtypeStruct((POOL_WIDTH, POOL_WIDTH), F32),
                   jax.ShapeDtypeStruct((1, POOL_WIDTH), F32), jax.ShapeDtypeStruct((1, POOL_WIDTH), F32)],
        input_output_aliases={5: 0},
        compiler_params=_params(("arbitrary",)),
    )(proj, dyall, wbd, pb, ps, dproj)


N_PAIRS = SSD_HEADS // 2


def _ssd_common(xc, dtraw, dtb, alog):
    c = CHUNK
    dt = _softplus(dtraw + dtb)
    a = -jnp.exp(alog)
    ltri = (_iota2((c, c), 0) >= _iota2((c, c), 1)).astype(BF16)
    acum = _split_dot_left(ltri, dt * a)
    expand = (_iota2((c, SSD_WIDTH), 1) // HEAD_DIM == _iota2((c, SSD_WIDTH), 0)).astype(BF16)
    expand_wide = (_iota2((c, SSD_HEADS * c), 1) // c == _iota2((c, SSD_HEADS * c), 0)).astype(BF16)
    acum_x = _split_dot(acum, expand)
    dt_x = _split_dot(dt, expand)
    alast_x = acum_x[c - 1:c, :]
    return dict(dt=dt, a=a, acum=acum, acum_x=acum_x, dt_x=dt_x, ea_x=jnp.exp(acum_x),
                dte_x=jnp.exp(alast_x - acum_x), eal_x=jnp.exp(alast_x),
                acol=_split_dot(acum, expand_wide), acum_t=acum.T,
                xs=xc[:, :SSD_WIDTH], causal=_iota2((c, c), 0) >= _iota2((c, c), 1),
                left=_iota2((c, c), 1) < HEAD_DIM)


def _ssd_group(xc, g):
    b = xc[:, SSD_WIDTH + D_STATE * g:SSD_WIDTH + D_STATE * (g + 1)]
    cm = xc[:, SSD_WIDTH + 2 * D_STATE + D_STATE * g:SSD_WIDTH + 2 * D_STATE + D_STATE * (g + 1)]
    return b, cm


def _ssd_decay(q, hh):
    col = q["acol"][:, CHUNK * hh:CHUNK * (hh + 1)]
    row = q["acum_t"][hh:hh + 1, :]
    return jnp.where(q["causal"], jnp.exp(jnp.minimum(col - row, 0.0)), 0.0)


def ssd_forward(proj, xc, dtb, alog, dskip_x, nw, name):
    s = xc.shape[0]
    nc = s // CHUNK

    def body(xc_ref, zdt_ref, dtb_ref, alog_ref, dsk_ref, nw_ref, y_ref, yc_ref, st_ref, state):
        @pl.when(pl.program_id(0) == 0)
        def _():
            state[...] = jnp.zeros_like(state)

        xcv = xc_ref[...]
        q = _ssd_common(xcv, zdt_ref[:, SSD_WIDTH:SSD_WIDTH + LANES], dtb_ref[...], alog_ref[...])
        x = q["xs"] * q["dt_x"]
        xb = x.astype(BF16)
        xd = (x * q["dte_x"]).astype(BF16)
        pieces = []
        for g in range(2):
            bg, cg = _ssd_group(xcv, g)
            bgb, cgb = bg.astype(BF16), cg.astype(BF16)
            cb = _nt(cgb, bgb)
            bgt = bg.T.astype(BF16)
            for pr in (2 * g, 2 * g + 1):
                sl = slice(CHUNK * pr, CHUNK * (pr + 1))
                st = state[pr]
                st_ref[0, pr] = st
                yp = _nn(cgb, st.astype(BF16)) * q["ea_x"][:, sl]
                for k, hh in enumerate((2 * pr, 2 * pr + 1)):
                    w = (cb * _ssd_decay(q, hh)).astype(BF16)
                    mask = q["left"] if k == 0 else jnp.logical_not(q["left"])
                    yp = yp + _nn(w, jnp.where(mask, xb[:, sl], jnp.zeros_like(xb[:, sl])))
                state[pr] = st * q["eal_x"][:, sl] + _nn(bgt, xd[:, sl])
                pieces.append(yp)
        y = jnp.concatenate(pieces, axis=1) + q["xs"] * dsk_ref[...]
        yc_ref[...] = y
        zv = zdt_ref[:, :SSD_WIDTH]
        yg = y * (zv * _sigmoid(zv))
        r = lax.rsqrt(jnp.mean(yg * yg, axis=-1, keepdims=True) + EPS)
        y_ref[...] = (yg * r * nw_ref[...]).astype(BF16)

    vec = lambda n: pl.BlockSpec((1, n), lambda c: (0, 0))
    return pl.pallas_call(
        body, name=name, grid=(nc,),
        in_specs=[pl.BlockSpec((CHUNK, CONV_DIM), lambda c: (c, 0)),
                  pl.BlockSpec((CHUNK, ZDT), lambda c: (c, OFF_Z // ZDT)),
                  vec(LANES), vec(LANES), vec(SSD_WIDTH), vec(SSD_WIDTH)],
        out_specs=[pl.BlockSpec((CHUNK, SSD_WIDTH), lambda c: (c, 0)), pl.BlockSpec((CHUNK, SSD_WIDTH), lambda c: (c, 0)),
                   pl.BlockSpec((1, N_PAIRS, D_STATE, CHUNK), lambda c: (c, 0, 0, 0))],
        out_shape=[jax.ShapeDtypeStruct((s, D_MODEL), BF16), jax.ShapeDtypeStruct((s, SSD_WIDTH), F32),
                   jax.ShapeDtypeStruct((nc, N_PAIRS, D_STATE, CHUNK), F32)],
        scratch_shapes=[pltpu.VMEM((N_PAIRS, D_STATE, CHUNK), F32)],
        compiler_params=_params(("arbitrary",)),
    )(xc, proj, dtb, alog, dskip_x, nw)


def ssd_backward(proj, xc, ycore, dyall, states, dtb, alog, dskip_x, nw, name):
    s = xc.shape[0]
    nc = s // CHUNK
    c = CHUNK

    def body(xc_ref, zdt_ref, yc_ref, dy_ref, st_ref, dtb_ref, alog_ref, dsk_ref, nw_ref,
             dxc_ref, dzdt_ref, dnw_ref, ddsk_ref, ddtb_ref, dalog_ref, dstate):
        @pl.when(pl.program_id(0) == 0)
        def _():
            dstate[...] = jnp.zeros_like(dstate)
            dnw_ref[...] = jnp.zeros_like(dnw_ref)
            ddsk_ref[...] = jnp.zeros_like(ddsk_ref)
            ddtb_ref[...] = jnp.zeros_like(ddtb_ref)
            dalog_ref[...] = jnp.zeros_like(dalog_ref)

        xcv = xc_ref[...]
        dtraw = zdt_ref[:, SSD_WIDTH:SSD_WIDTH + LANES]
        q = _ssd_common(xcv, dtraw, dtb_ref[...], alog_ref[...])
        xs = q["xs"]
        x = xs * q["dt_x"]
        zv, yc, dy, nwv = zdt_ref[:, :SSD_WIDTH], yc_ref[...], dy_ref[...], nw_ref[...]
        sgz = _sigmoid(zv)
        siluz = zv * sgz
        yg = yc * siluz
        r = lax.rsqrt(jnp.mean(yg * yg, axis=-1, keepdims=True) + EPS)
        dnw_ref[...] += jnp.sum(dy * yg * r, axis=0, keepdims=True)
        g1 = dy * nwv
        dyg = r * (g1 - yg * (r * r) * jnp.mean(g1 * yg, axis=-1, keepdims=True))
        dyv = dyg * siluz
        dz = (dyg * yc * (sgz * (1.0 + zv * (1.0 - sgz)))).astype(BF16)
        ddsk_ref[...] += jnp.sum(dyv * xs, axis=0, keepdims=True)
        dye = dyv * q["ea_x"]
        dx_parts, yoff_parts, u_parts, v_parts, e_parts = [], [], [], [], []
        db_parts, dc_parts = [], []
        for g in range(2):
            bg, cg = _ssd_group(xcv, g)
            bgb, cgb = bg.astype(BF16), cg.astype(BF16)
            cb = _nt(cgb, bgb)
            cgt = cg.T.astype(BF16)
            dgsum = jnp.zeros((c, c), F32)
            dbg = jnp.zeros((c, D_STATE), F32)
            dcg = jnp.zeros((c, D_STATE), F32)
            for pr in (2 * g, 2 * g + 1):
                sl = slice(c * pr, c * (pr + 1))
                st = st_ref[0, pr]
                dst = dstate[pr]
                stb, dstb = st.astype(BF16), dst.astype(BF16)
                xp = x[:, sl]
                xpb = xp.astype(BF16)
                dyp = dyv[:, sl]
                xdp = xp * q["dte_x"][:, sl]
                yoff_parts.append(_nn(cgb, stb) * q["ea_x"][:, sl])
                rr = _nn(bgb, dstb)
                dxp = rr * q["dte_x"][:, sl]
                u_parts.append(rr * xdp)
                v_parts.append(dst * st * q["eal_x"][:, sl])
                for k, hh in enumerate((2 * pr, 2 * pr + 1)):
                    decay = _ssd_decay(q, hh)
                    w = cb * decay
                    mask = q["left"] if k == 0 else jnp.logical_not(q["left"])
                    dym = jnp.where(mask, dyp, 0.0).astype(BF16)
                    dw = _nt(dym, xpb)
                    dgsum = dgsum + dw * decay
                    e_parts.append(dw * w)
                    dxp = dxp + _nn(w.T.astype(BF16), dym)
                dyeb = dye[:, sl].astype(BF16)
                dcg = dcg + _nt(dyeb, stb)
                dbg = dbg + _nt(xdp.astype(BF16), dstb)
                dstate[pr] = dst * q["eal_x"][:, sl] + _nn(cgt, dyeb)
                dx_parts.append(dxp)
            dcg = dcg + _nn(dgsum.astype(BF16), bgb)
            dbg = dbg + _nn(dgsum.T.astype(BF16), cgb)
            db_parts.append(dbg)
            dc_parts.append(dcg)
        dx = jnp.concatenate(dx_parts, axis=1)
        yoff = jnp.concatenate(yoff_parts, axis=1)
        u = jnp.concatenate(u_parts, axis=1)
        v = jnp.concatenate(v_parts, axis=1)
        reduce_heads = (_iota2((SSD_WIDTH, c), 0) // HEAD_DIM == _iota2((SSD_WIDTH, c), 1)).astype(BF16)
        to_head = (_iota2((SSD_HEADS * c, c), 0) // c == _iota2((SSD_HEADS * c, c), 1)).astype(BF16)
        da = _split_dot(dyv * yoff - u, reduce_heads, 2)
        da = da + _split_dot(jnp.concatenate(e_parts, axis=1), to_head, 2)
        da = da - _split_dot(jnp.concatenate(e_parts, axis=0), to_head, 2, dot=_tn)
        dalast = jnp.sum(_split_dot(u + v, reduce_heads, 2), axis=0, keepdims=True)
        da = da + jnp.where(_iota2((c, c), 0) == c - 1, dalast, 0.0)
        utri = (_iota2((c, c), 1) >= _iota2((c, c), 0)).astype(BF16)
        dda = _split_dot_left(utri, da)
        ddt = dda * q["a"] + _split_dot(dx * xs, reduce_heads, 2)
        dalog_ref[...] += jnp.sum(dda * q["dt"], axis=0, keepdims=True) * q["a"]
        ddtraw = jnp.where(_iota2((c, c), 1) < SSD_HEADS, ddt * _sigmoid(dtraw + dtb_ref[...]), 0.0)
        ddtb_ref[...] += jnp.sum(ddtraw, axis=0, keepdims=True)
        dzdt_ref[...] = jnp.concatenate([dz, ddtraw.astype(BF16), jnp.zeros((c, ZDT - SSD_WIDTH - LANES), BF16)], axis=1)
        dxs = dx * q["dt_x"] + dyv * dsk_ref[...]
        dxc_ref[...] = jnp.concatenate([dxs] + db_parts + dc_parts, axis=1)

    rev = lambda i: nc - 1 - i
    vec = lambda n: pl.BlockSpec((1, n), lambda i: (0, 0))
    wide = pl.BlockSpec((c, SSD_WIDTH), lambda i: (rev(i), 0))
    zdt = pl.BlockSpec((c, ZDT), lambda i: (rev(i), OFF_Z // ZDT))
    return pl.pallas_call(
        body, name=name, grid=(nc,),
        in_specs=[pl.BlockSpec((c, CONV_DIM), lambda i: (rev(i), 0)), zdt, wide, wide,
                  pl.BlockSpec((1, N_PAIRS, D_STATE, c), lambda i: (rev(i), 0, 0, 0)),
                  vec(LANES), vec(LANES), vec(SSD_WIDTH), vec(SSD_WIDTH)],
        out_specs=[pl.BlockSpec((c, CONV_DIM), lambda i: (rev(i), 0)), zdt,
                   vec(SSD_WIDTH), vec(SSD_WIDTH), vec(LANES), vec(LANES)],
        out_shape=[jax.ShapeDtypeStruct((s, CONV_DIM), F32), jax.ShapeDtypeStruct((s, D_INP), BF16),
                   jax.ShapeDtypeStruct((1, SSD_WIDTH), F32),
                   jax.ShapeDtypeStruct((1, SSD_WIDTH), F32), jax.ShapeDtypeStruct((1, LANES), F32),
                   jax.ShapeDtypeStruct((1, LANES), F32)],
        scratch_shapes=[pltpu.VMEM((N_PAIRS, D_STATE, c), F32)],
        compiler_params=_params(("arbitrary",)),
    )(xc, proj, ycore, dyall, states, dtb, alog, dskip_x, nw)


SB_BLOCK = 256
SB_SCALE = HEAD_DIM ** -0.5


def _sb_masks(i, j):
    t = SB_BLOCK
    return (_iota2((t, t), 1) + j * t) < (_iota2((t, t), 0) + i * t)


def _sb_weights(qm, kb, valid, run_lk, strict_after):
    z = _nt(qm, kb)
    ls = -_softplus(-z)
    lk = jnp.where(valid, ls - z, 0.0)
    after = _split_dot(lk, strict_after, 2) + run_lk
    w = jnp.where(valid, jnp.exp(ls + after), 0.0)
    return z, lk, w


def sb_forward(proj, y_all, name):
    s = proj.shape[0]
    t = SB_BLOCK
    nq = s // t

    def body(q_ref, k_ref, v_ref, _, y_ref, o_ref):
        i = pl.program_id(1)
        left = _iota2((t, LANES), 1) < HEAD_DIM
        qv = q_ref[...] * SB_SCALE
        zero = jnp.zeros_like(qv)
        qms = (jnp.where(left, qv, zero).astype(BF16), jnp.where(left, zero, qv).astype(BF16))
        strict_after = (_iota2((t, t), 0) > _iota2((t, t), 1)).astype(BF16)

        def step(jj, carry):
            o, runs = carry[0], carry[1:]
            j = i - jj
            rows = pl.ds(pl.multiple_of(j * t, t), t)
            kb = k_ref[rows, :].astype(BF16)
            vv = v_ref[rows, :]
            valid = _sb_masks(i, j)
            new_runs = []
            for k in range(2):
                _, lk, w = _sb_weights(qms[k], kb, valid, runs[k], strict_after)
                vm = jnp.where(left if k == 0 else jnp.logical_not(left), vv, 0.0).astype(BF16)
                o = o + _split_dot(w, vm, 2)
                new_runs.append(runs[k] + jnp.sum(lk, axis=1, keepdims=True))
            return (o, *new_runs)

        init = (jnp.zeros((t, LANES), F32), jnp.zeros((t, 1), F32), jnp.zeros((t, 1), F32))
        o = lax.fori_loop(0, i + 1, step, init)[0]
        o_ref[...] = o
        y_ref[...] = o.astype(BF16)

    return pl.pallas_call(
        body, name=name, grid=(2, nq),
        in_specs=[pl.BlockSpec((t, LANES), lambda p, i: (i, 3 * p)),
                  pl.BlockSpec((s, LANES), lambda p, i: (0, 3 * p + 1)),
                  pl.BlockSpec((s, LANES), lambda p, i: (0, 3 * p + 2)), ANY],
        out_specs=[pl.BlockSpec((t, LANES), lambda p, i: (i, SSD_WIDTH // LANES + p)),
                   pl.BlockSpec((t, LANES), lambda p, i: (i, p))],
        out_shape=[jax.ShapeDtypeStruct(y_all.shape, BF16), jax.ShapeDtypeStruct((s, SB_WIDTH), F32)],
        input_output_aliases={3: 0},
        compiler_params=_params(("parallel", "arbitrary")),
    )(proj, proj, proj, y_all)


def sb_backward(proj, o, dyall, dproj, name):
    s = proj.shape[0]
    t = SB_BLOCK
    nq = s // t

    def body(q_ref, k_ref, v_ref, o_ref, do_ref, _, dqkv_ref, dk_acc, dv_acc):
        dk_acc[...] = jnp.zeros_like(dk_acc)
        dv_acc[...] = jnp.zeros_like(dv_acc)
        left = _iota2((t, LANES), 1) < HEAD_DIM
        lane_masks = (left, jnp.logical_not(left))
        strict_after = (_iota2((t, t), 0) > _iota2((t, t), 1)).astype(BF16)
        from_here = (_iota2((t, t), 0) >= _iota2((t, t), 1)).astype(BF16)

        def query_block(i, _):
            qrows = pl.ds(pl.multiple_of(i * t, t), t)
            qv = q_ref[qrows, :] * SB_SCALE
            dov = do_ref[qrows, :]
            zero = jnp.zeros_like(qv)
            qb = qv.astype(BF16)
            dob = dov.astype(BF16)
            prod = dob.astype(F32) * o_ref[qrows, :]
            qms = [jnp.where(m, qv, zero).astype(BF16) for m in lane_masks]
            doms = [jnp.where(m, dov, zero).astype(BF16) for m in lane_masks]
            deltas = [jnp.sum(jnp.where(m, prod, zero), axis=1, keepdims=True) for m in lane_masks]

            def step(jj, carry):
                dq = carry[0]
                run_lk, run_e = carry[1:3], carry[3:5]
                j = i - jj
                rows = pl.ds(pl.multiple_of(j * t, t), t)
                kb = k_ref[rows, :].astype(BF16)
                vb = v_ref[rows, :].astype(BF16)
                valid = _sb_masks(i, j)
                dkj = jnp.zeros((t, LANES), F32)
                dvj = jnp.zeros((t, LANES), F32)
                new_lk, new_e = [], []
                for k in range(2):
                    z, lk, w = _sb_weights(qms[k], kb, valid, run_lk[k], strict_after)
                    sg = _sigmoid(z)
                    e = _nt(doms[k], vb) * w
                    before = deltas[k] - _split_dot(e, from_here, 2) - run_e[k]
                    dz = jnp.where(valid, e * (1.0 - sg) - sg * before, 0.0).astype(BF16)
                    m = lane_masks[k]
                    dvj = dvj + jnp.where(m, _tn(w.astype(BF16), dob), 0.0)
                    dkj = dkj + jnp.where(m, _tn(dz, qb), 0.0)
                    dq = dq + jnp.where(m, _nn(dz, kb), 0.0)
                    new_lk.append(run_lk[k] + jnp.sum(lk, axis=1, keepdims=True))
                    new_e.append(run_e[k] + jnp.sum(e, axis=1, keepdims=True))
                dk_acc[rows, :] += dkj
                dv_acc[rows, :] += dvj
                return (dq, *new_lk, *new_e)

            col = jnp.zeros((t, 1), F32)
            dq = lax.fori_loop(0, i + 1, step, (jnp.zeros((t, LANES), F32), col, col, col, col))[0]
            dqkv_ref[qrows, 0:LANES] = (dq * SB_SCALE).astype(BF16)
            return 0

        lax.fori_loop(0, nq, query_block, 0)
        dqkv_ref[:, LANES:2 * LANES] = dk_acc[...].astype(BF16)
        dqkv_ref[:, 2 * LANES:3 * LANES] = dv_acc[...].astype(BF16)

    col = lambda f: pl.BlockSpec((s, LANES), f)
    return pl.pallas_call(
        body, name=name, grid=(2,),
        in_specs=[col(lambda p: (0, 3 * p)), col(lambda p: (0, 3 * p + 1)), col(lambda p: (0, 3 * p + 2)),
                  col(lambda p: (0, p)), col(lambda p: (0, SSD_WIDTH // LANES + p)), ANY],
        out_specs=pl.BlockSpec((s, 3 * LANES), lambda p: (0, p)),
        out_shape=jax.ShapeDtypeStruct(dproj.shape, BF16),
        input_output_aliases={5: 0},
        scratch_shapes=[pltpu.VMEM((s, LANES), F32), pltpu.VMEM((s, LANES), F32)],
        compiler_params=_params(("parallel",)),
    )(proj, proj, proj, o, dyall, dproj)


def adamw(w, g, m, v, name):
    b, r, c = w.shape
    tr = r
    for cand in (512, 256, 128, 64, 32, 16, 8):
        if r % cand == 0 and r > cand:
            tr = cand
            break

    def body(w_ref, g_ref, m_ref, v_ref, d_ref, nm_ref, nv_ref):
        gv = g_ref[...]
        nm = ADAM_B1 * m_ref[...] + (1.0 - ADAM_B1) * gv
        nv = ADAM_B2 * v_ref[...] + (1.0 - ADAM_B2) * (gv * gv)
        m_hat = nm / (1.0 - ADAM_B1 ** ADAM_STEP)
        v_hat = nv / (1.0 - ADAM_B2 ** ADAM_STEP)
        d_ref[...] = -ADAM_LR * (m_hat / (jnp.sqrt(v_hat) + ADAM_EPS) + ADAM_WD * w_ref[...])
        nm_ref[...] = nm
        nv_ref[...] = nv

    blk = pl.BlockSpec((1, tr, c), lambda i, j: (i, j, 0))
    return pl.pallas_call(
        body, name=name, grid=(b, r // tr),
        in_specs=[blk] * 4, out_specs=[blk] * 3,
        out_shape=[jax.ShapeDtypeStruct(w.shape, F32)] * 3,
        compiler_params=_params(("parallel", "parallel")),
    )(w, g, m, v)


def _position():
    return lax.axis_index("x"), lax.axis_index("y"), lax.axis_index("c")


def _flipped(pos, flip):
    return tuple((1 - p) if f else p for p, f in zip(pos, flip))


FLIP_C = (0, 0, 1)
CHIP_FLIPS = {1: (0, 1, 0), 2: (1, 0, 0), 3: (1, 1, 0)}
SHARD_ROWS = (SHARD_IN, SHARD_OUT, SHARD_FF, SHARD_FF, SHARD_FF)


def _rows(start, size):
    return pl.ds(pl.multiple_of(start, 16), size)


def exchange(name, ins, out_shapes, aliases, plan):
    n_in = len(ins)

    def body(*refs):
        in_refs, out_refs = refs[:n_in], refs[n_in:n_in + len(out_shapes)]
        send_sems, recv_sems, local_sems = refs[n_in + len(out_shapes):]
        pos = _position()
        remote, local = plan(pos, in_refs, out_refs)
        copies = []
        for k, (src, dst) in enumerate(local):
            cp = pltpu.make_async_copy(src, dst, local_sems.at[k])
            cp.start()
            copies.append(cp)
        for k, (src, dst, flip) in enumerate(remote):
            cp = pltpu.make_async_remote_copy(src_ref=src, dst_ref=dst, send_sem=send_sems.at[k], recv_sem=recv_sems.at[k],
                                              device_id=_flipped(pos, flip), device_id_type=MESH)
            cp.start()
            copies.append(cp)
        for cp in copies:
            cp.wait()

    n_remote, n_local = plan.counts
    return pl.pallas_call(
        body, name=name,
        in_specs=[ANY] * n_in, out_specs=[ANY] * len(out_shapes), out_shape=out_shapes,
        input_output_aliases=aliases,
        scratch_shapes=[pltpu.SemaphoreType.DMA((n_remote,)), pltpu.SemaphoreType.DMA((n_remote,)),
                        pltpu.SemaphoreType.DMA((max(n_local, 1),))],
    )(*ins)


def _plan(counts):
    def wrap(fn):
        fn.counts = counts
        return fn
    return wrap


def gather_weights(shards):
    fulls = [jax.ShapeDtypeStruct((DEPTH, N_CHIPS * r, D_MODEL), BF16) for r in SHARD_ROWS]

    @_plan((15, 5))
    def over_ici(pos, ins, outs):
        chip, c = 2 * pos[0] + pos[1], pos[2]
        remote, local = [], []
        for src, dst, r in zip(ins, outs, SHARD_ROWS):
            h = r // 2
            local.append((src, dst.at[:, _rows(chip * r, r)]))
            for f in (1, 2, 3):
                remote.append((src.at[:, _rows(c * h, h)], dst.at[:, _rows(chip * r + c * h, h)], CHIP_FLIPS[f]))
        return remote, local

    @_plan((15, 0))
    def over_d2d(pos, ins, outs):
        chip, c = 2 * pos[0] + pos[1], pos[2]
        remote = []
        for src, dst, r in zip(ins, outs, SHARD_ROWS):
            h = r // 2
            for f in (1, 2, 3):
                at = _rows(lax.bitwise_xor(chip, f) * r + c * h, h)
                remote.append((src.at[:, at], dst.at[:, at], FLIP_C))
        return remote, []

    part = exchange("gather_ici", shards, fulls, {}, over_ici)
    return exchange("gather_d2d", part, fulls, {k: k for k in range(5)}, over_d2d)


def add_halves(d, recv, half, name):
    dep, nch, r, c = d.shape
    h = r // 2

    def body(half_ref, d_ref, r_ref, o_ref):
        o_ref[...] = (d_ref[...].astype(F32) + r_ref[...].astype(F32)).astype(BF16)

    return pl.pallas_call(
        body, name=name,
        grid_spec=pltpu.PrefetchScalarGridSpec(
            num_scalar_prefetch=1, grid=(dep, nch),
            in_specs=[pl.BlockSpec((1, 1, h, c), lambda l, j, hf: (l, j, hf[0], 0)),
                      pl.BlockSpec((1, 1, h, c), lambda l, j, hf: (l, j, 0, 0))],
            out_specs=pl.BlockSpec((1, 1, h, c), lambda l, j, hf: (l, j, 0, 0))),
        out_shape=jax.ShapeDtypeStruct(recv.shape, BF16),
        compiler_params=_params(("parallel", "parallel")),
    )(half, d, recv)


def add_chips(p, recv, chip, name):
    dep, _, r, c = p.shape

    def body(chip_ref, p_ref, r_ref, o_ref):
        acc = p_ref[0, 0].astype(F32)
        for k in range(N_CHIPS - 1):
            acc = acc + r_ref[0, k].astype(F32)
        o_ref[0] = acc

    return pl.pallas_call(
        body, name=name,
        grid_spec=pltpu.PrefetchScalarGridSpec(
            num_scalar_prefetch=1, grid=(dep,),
            in_specs=[pl.BlockSpec((1, 1, r, c), lambda l, ch: (l, ch[0], 0, 0)),
                      pl.BlockSpec((1, N_CHIPS - 1, r, c), lambda l, ch: (l, 0, 0, 0))],
            out_specs=pl.BlockSpec((1, r, c), lambda l, ch: (l, 0, 0))),
        out_shape=jax.ShapeDtypeStruct((dep, r, c), F32),
        compiler_params=_params(("parallel",)),
    )(chip, p, recv)


def reduce_gradients(stacks, half_arr, chip_arr):
    views = [s.reshape(DEPTH, N_CHIPS, r, D_MODEL) for s, r in zip(stacks, SHARD_ROWS)]

    @_plan((5, 0))
    def to_sibling(pos, ins, outs):
        c = pos[2]
        return [(src.at[:, :, _rows((1 - c) * (r // 2), r // 2)], dst, FLIP_C) for src, dst, r in zip(ins, outs, SHARD_ROWS)], []

    from_sibling = exchange("reduce_d2d", views,
                            [jax.ShapeDtypeStruct((DEPTH, N_CHIPS, r // 2, D_MODEL), BF16) for r in SHARD_ROWS], {}, to_sibling)
    chip_sums = [add_halves(d, rv, half_arr, "reduce_add_halves") for d, rv in zip(views, from_sibling)]

    @_plan((15, 0))
    def to_chips(pos, ins, outs):
        chip = 2 * pos[0] + pos[1]
        remote = []
        for src, dst in zip(ins, outs):
            for f in (1, 2, 3):
                remote.append((src.at[:, lax.bitwise_xor(chip, f)], dst.at[:, f - 1], CHIP_FLIPS[f]))
        return remote, []

    from_chips = exchange("reduce_ici", chip_sums,
                          [jax.ShapeDtypeStruct((DEPTH, N_CHIPS - 1, r // 2, D_MODEL), BF16) for r in SHARD_ROWS], {}, to_chips)
    mine = [add_chips(p, rv, chip_arr, "reduce_add_chips") for p, rv in zip(chip_sums, from_chips)]

    @_plan((5, 5))
    def swap(pos, ins, outs):
        c = pos[2]
        remote, local = [], []
        for src, dst, r in zip(ins, outs, SHARD_ROWS):
            at = _rows(c * (r // 2), r // 2)
            local.append((src, dst.at[:, at]))
            remote.append((src, dst.at[:, at], FLIP_C))
        return remote, local

    return exchange("reduce_swap", mine, [jax.ShapeDtypeStruct((DEPTH, r, D_MODEL), F32) for r in SHARD_ROWS], {}, swap)


def small_allreduce(v, name):
    r, c = v.shape

    def body(v_ref, o_ref, buf, send_sems, recv_sems):
        pos = _position()
        me = 4 * pos[0] + 2 * pos[1] + pos[2]
        buf[0] = v_ref[...]
        copies = []
        for f in range(1, 8):
            flip = ((f >> 2) & 1, (f >> 1) & 1, f & 1)
            cp = pltpu.make_async_remote_copy(
                src_ref=v_ref, dst_ref=buf.at[f], send_sem=send_sems.at[f - 1], recv_sem=recv_sems.at[f - 1],
                device_id=_flipped(pos, flip), device_id_type=MESH)
            cp.start()
            copies.append(cp)
        for cp in copies:
            cp.wait()
        acc = buf[me]
        for d in range(1, 8):
            acc = acc + buf[lax.bitwise_xor(me, d)]
        o_ref[...] = acc

    return pl.pallas_call(
        body, name=name,
        in_specs=[pl.BlockSpec(memory_space=pltpu.VMEM)], out_specs=pl.BlockSpec(memory_space=pltpu.VMEM),
        out_shape=jax.ShapeDtypeStruct((r, c), F32),
        scratch_shapes=[pltpu.VMEM((8, r, c), F32), pltpu.SemaphoreType.DMA((7,)), pltpu.SemaphoreType.DMA((7,))],
    )(v)


_IN_SEGMENTS = ((0, 1544, 128), (128, 1800, 128), (256, 2056, 128), (384, 1672, 128), (512, 1928, 128), (640, 2184, 128),
                (OFF_Z, 0, SSD_WIDTH), (OFF_DT, 1536, SSD_HEADS), (OFF_XBC, 512, CONV_DIM), (OFF_P, 2312, POOL_WIDTH))


def _in_column_map():
    m = np.full((D_INP,), -1, np.int64)
    for at, orig, n in _IN_SEGMENTS:
        cols = np.arange(orig, orig + n)
        m[at:at + n] = (cols // COLS_IN) * SHARD_IN + cols % COLS_IN
    return m


def _runs(idx):
    out, i = [], 0
    while i < len(idx):
        j = i + 1
        while j < len(idx) and ((idx[i] < 0 and idx[j] < 0) or (idx[i] >= 0 and idx[j] == idx[j - 1] + 1)):
            j += 1
        out.append((int(idx[i]), j - i))
        i = j
    return out


def _take_rows(a, idx):
    parts = []
    for first, n in _runs(idx):
        parts.append(jnp.zeros((a.shape[0], n, a.shape[2]), a.dtype) if first < 0 else a[:, first:first + n])
    return jnp.concatenate(parts, axis=1)


def _in_weight_layout(staged):
    return _take_rows(staged, _in_column_map())


def _in_gradient_layout(dwt):
    fwd = _in_column_map()
    inv = np.full((N_CHIPS * SHARD_IN,), -1, np.int64)
    inv[fwd[fwd >= 0]] = np.nonzero(fwd >= 0)[0]
    return _take_rows(dwt, inv)


SMALL_NAMES = ("norm1_w", "conv_w", "conv_b", "dt_bias", "a_log", "d_skip", "ssd_norm_w", "pool_w", "pool_b",
               "pool_scale", "norm2_w", "final_norm_w")
SMALL_ROWS = 104


def _pack_small(parts):
    flat = jnp.concatenate([p.reshape(-1) for p in parts])
    return jnp.pad(flat, (0, SMALL_ROWS * D_MODEL - flat.shape[0])).reshape(SMALL_ROWS, D_MODEL)


def _unpack_small(flat, shapes):
    flat = flat.reshape(-1)
    out, at = [], 0
    for shp in shapes:
        n = int(np.prod(shp))
        out.append(flat[at:at + n].reshape(shp))
        at += n
    return out


def kernel(x, norm1_w, w_in, conv_w, conv_b, dt_bias, a_log, d_skip, ssd_norm_w, pool_w, pool_b, pool_scale, w_out, norm2_w, w_gate, w_up, w_down, final_norm_w, loss_target, m_norm1_w, m_w_in, m_conv_w, m_conv_b, m_dt_bias, m_a_log, m_d_skip, m_ssd_norm_w, m_pool_w, m_pool_b, m_pool_scale, m_w_out, m_norm2_w, m_w_gate, m_w_up, m_w_down, m_final_norm_w, v_norm1_w, v_w_in, v_conv_w, v_conv_b, v_dt_bias, v_a_log, v_d_skip, v_ssd_norm_w, v_pool_w, v_pool_b, v_pool_scale, v_w_out, v_norm2_w, v_w_gate, v_w_up, v_w_down, v_final_norm_w):
    px, py, pc = _position()
    chip = 2 * px + py
    chip_arr = jnp.reshape(chip, (1,)).astype(jnp.int32)
    half_arr = jnp.reshape(pc, (1,)).astype(jnp.int32)

    w_in_t = jnp.pad(jnp.swapaxes(w_in, 1, 2).astype(BF16), ((0, 0), (0, SHARD_IN - COLS_IN), (0, 0)))
    shards = [w_in_t, w_out.astype(BF16), jnp.swapaxes(w_gate, 1, 2).astype(BF16),
              jnp.swapaxes(w_up, 1, 2).astype(BF16), w_down.astype(BF16)]
    w_in_st, w_out_f, w_gate_t, w_up_t, w_down_f = gather_weights(shards)
    w_in_f = _in_weight_layout(w_in_st)

    pad_heads = lambda v: jnp.pad(v, ((0, 0), (0, LANES - SSD_HEADS)))[:, None, :]
    dtb, alog = pad_heads(dt_bias), pad_heads(a_log)
    dskip_x = jnp.repeat(d_skip, HEAD_DIM, axis=1)[:, None, :]
    eye = jnp.eye(len(POOL_WINDOWS), dtype=F32)
    wbd = (pool_w[:, :, :, None, :] * eye[None, :, None, :, None]).reshape(DEPTH, POOL_WIDTH, POOL_WIDTH).astype(BF16)
    pool_b2 = pool_b.reshape(DEPTH, 1, POOL_WIDTH)
    cw_cols = lax.dynamic_update_slice(jnp.zeros((DEPTH, CONV_WIDTH, CONV_DIM), F32), conv_w,
                                       (0, 0, chip * (CONV_DIM // N_CHIPS)))
    cw_cols = jnp.where(pc == 0, cw_cols, 0.0)
    cw_rows = (DEPTH * CONV_WIDTH * CONV_DIM) // D_MODEL
    conv_w_f = small_allreduce(jnp.pad(cw_cols.reshape(cw_rows, D_MODEL), ((0, 8), (0, 0))), "gather_conv_w")
    conv_w_f = conv_w_f[:cw_rows].reshape(DEPTH, CONV_WIDTH, CONV_DIM)
    cw8 = jnp.pad(conv_w_f, ((0, 0), (0, 8 - CONV_WIDTH), (0, 0)))

    h = x[0]
    saved = []
    for l in range(DEPTH):
        proj = rms_matmul(h, norm1_w[l][None], w_in_f[l], "in_proj")
        xc = conv_forward(proj, cw8[l], conv_b[l][None], "conv_fwd")
        y_all, ycore, states = ssd_forward(proj, xc, dtb[l], alog[l], dskip_x[l], ssd_norm_w[l][None], "ssd_fwd")
        y_all, o_sb = sb_forward(proj, y_all, "sb_fwd")
        y_all = pool_forward(proj, wbd[l], pool_b2[l], pool_scale[l][None], y_all, "pool_fwd")
        x1 = matmul_residual(y_all, w_out_f[l], h, "out_proj")
        x2, g, u = ffn_forward(x1, norm2_w[l][None], w_gate_t[l], w_up_t[l], w_down_f[l], "ffn_fwd")
        saved.append((h, proj, xc, ycore, states, o_sb, y_all, x1, g, u))
        h = x2

    loss_part, dx, dxb, d_final = loss_head(h, final_norm_w[None], loss_target[0], "loss_head")
    loss = lax.psum(loss_part[0, 0], ("x", "y", "c"))

    small = {n: [None] * DEPTH for n in SMALL_NAMES if n != "final_norm_w"}
    stacks = [None] * 5
    for l in reversed(range(DEPTH)):
        xin, proj, xc, ycore, states, o_sb, y_all, x1, g, u = saved[l]
        dg, du, act = ffn_backward_act(dxb, g, u, w_down_f[l], "ffn_bwd_act")
        dx1, dx1b, h2, dn2 = rms_backward([dg, du], [w_gate_t[l], w_up_t[l]], x1, norm2_w[l][None], dx, "ffn_bwd_norm", 256)
        stacks[4] = matmul_tn(act, dxb, "dw_down", stacks[4], l)
        stacks[2] = matmul_tn(dg, h2, "dw_gate", stacks[2], l)
        stacks[3] = matmul_tn(du, h2, "dw_up", stacks[3], l)
        dyall = matmul_nt(dx1b, w_out_f[l], "out_proj_bwd")
        stacks[1] = matmul_tn(y_all, dx1b, "dw_out", stacks[1], l)
        dxc, dproj, dsn, ddsk, ddtb, dalog = ssd_backward(proj, xc, ycore, dyall, states, dtb[l], alog[l],
                                                          dskip_x[l], ssd_norm_w[l][None], "ssd_bwd")
        dproj, dcw, dcb = conv_backward(proj, dxc, cw8[l], conv_b[l][None], dproj, "conv_bwd")
        dproj = sb_backward(proj, o_sb, dyall, dproj, "sb_bwd")
        dproj, dwbd, dpb, dps = pool_backward(proj, dyall, wbd[l], pool_b2[l], pool_scale[l][None], dproj, "pool_bwd")
        dx, dxb, h1, dn1 = rms_backward([dproj], [w_in_f[l]], xin, norm1_w[l][None], dx1, "in_proj_bwd", 256)
        stacks[0] = matmul_tn(dproj, h1, "dw_in", stacks[0], l)
        small["norm1_w"][l] = dn1[0]
        small["conv_w"][l] = dcw[:CONV_WIDTH]
        small["conv_b"][l] = dcb[0]
        small["dt_bias"][l] = ddtb[0, :SSD_HEADS]
        small["a_log"][l] = dalog[0, :SSD_HEADS]
        small["d_skip"][l] = ddsk.reshape(SSD_HEADS, HEAD_DIM).sum(axis=1)
        small["ssd_norm_w"][l] = dsn[0]
        small["pool_w"][l] = jnp.stack([dwbd[64 * k:64 * k + 64, 64 * k:64 * k + 64] for k in range(len(POOL_WINDOWS))])
        small["pool_b"][l] = dpb.reshape(len(POOL_WINDOWS), -1)
        small["pool_scale"][l] = dps[0]
        small["norm2_w"][l] = dn2[0]
    grad_x = dx[None]

    stacks[0] = _in_gradient_layout(stacks[0])
    g_in_t, g_out, g_gate_t, g_up_t, g_down = reduce_gradients(stacks, half_arr, chip_arr)
    g_big = dict(w_in=jnp.swapaxes(g_in_t[:, :COLS_IN], 1, 2), w_out=g_out, w_gate=jnp.swapaxes(g_gate_t, 1, 2),
                 w_up=jnp.swapaxes(g_up_t, 1, 2), w_down=g_down)

    small_parts = [d_final if n == "final_norm_w" else jnp.stack(small[n]) for n in SMALL_NAMES]
    small_shapes = [p.shape for p in small_parts]
    g_small = dict(zip(SMALL_NAMES, _unpack_small(small_allreduce(_pack_small(small_parts), "reduce_small"), small_shapes)))
    g_small["final_norm_w"] = g_small["final_norm_w"].reshape(final_norm_w.shape)
    g_small["conv_w"] = lax.dynamic_slice_in_dim(g_small["conv_w"], chip * (CONV_DIM // N_CHIPS), CONV_DIM // N_CHIPS, axis=2)

    given = dict(norm1_w=(norm1_w, m_norm1_w, v_norm1_w), w_in=(w_in, m_w_in, v_w_in), conv_w=(conv_w, m_conv_w, v_conv_w),
                 conv_b=(conv_b, m_conv_b, v_conv_b), dt_bias=(dt_bias, m_dt_bias, v_dt_bias), a_log=(a_log, m_a_log, v_a_log),
                 d_skip=(d_skip, m_d_skip, v_d_skip), ssd_norm_w=(ssd_norm_w, m_ssd_norm_w, v_ssd_norm_w),
                 pool_w=(pool_w, m_pool_w, v_pool_w), pool_b=(pool_b, m_pool_b, v_pool_b),
                 pool_scale=(pool_scale, m_pool_scale, v_pool_scale), w_out=(w_out, m_w_out, v_w_out),
                 norm2_w=(norm2_w, m_norm2_w, v_norm2_w), w_gate=(w_gate, m_w_gate, v_w_gate), w_up=(w_up, m_w_up, v_w_up),
                 w_down=(w_down, m_w_down, v_w_down), final_norm_w=(final_norm_w, m_final_norm_w, v_final_norm_w))
    order = ("norm1_w", "w_in", "conv_w", "conv_b", "dt_bias", "a_log", "d_skip", "ssd_norm_w", "pool_w", "pool_b",
             "pool_scale", "w_out", "norm2_w", "w_gate", "w_up", "w_down", "final_norm_w")
    grads = dict(g_small)
    grads.update(g_big)
    results = {}
    for n in ("w_in", "w_out", "w_gate", "w_up", "w_down"):
        w, m, v = given[n]
        results[n] = adamw(w, grads[n], m, v, "adamw_" + n)
    small_shapes = [given[n][0].shape for n in SMALL_NAMES]
    packed = [_pack_small([given[n][k] for n in SMALL_NAMES])[None] for k in range(3)]
    packed_g = _pack_small([grads[n] for n in SMALL_NAMES])[None]
    small_out = adamw(packed[0], packed_g, packed[1], packed[2], "adamw_small")
    small_out = [_unpack_small(o[0], small_shapes) for o in small_out]
    for i, n in enumerate(SMALL_NAMES):
        results[n] = tuple(small_out[k][i] for k in range(3))

    return (loss, grad_x, *[grads[n] for n in order], *[results[n][0] for n in order],
            *[results[n][1] for n in order], *[results[n][2] for n in order])
```

```python
import numpy as np
import jax
import jax.numpy as jnp
from jax import lax
from jax.experimental import pallas as pl
from jax.experimental.pallas import tpu as pltpu

F32 = jnp.float32
BF16 = jnp.bfloat16
MESH = pl.DeviceIdType.MESH
ANY = pl.BlockSpec(memory_space=pl.ANY)

D_MODEL = 1024
DEPTH = 4
EPS = 1e-6
SSD_WIDTH = 512
SSD_HEADS = 8
HEAD_DIM = 64
D_STATE = 128
CHUNK = 128
CONV_WIDTH = 4
CONV_DIM = 1024
SB_WIDTH = 256
POOL_WIDTH = 256
POOL_WINDOWS = (2, 4, 8, 16)
D_FF = 2816
D_IN = 2568
N_CHIPS = 4
OFF_QKV, OFF_Z, OFF_DT, OFF_XBC, OFF_P = 0, 768, 1280, 1536, 2560
D_INP = 2816
ZDT = 768
SHARD_IN, SHARD_OUT, SHARD_FF = 672, 256, 704
COLS_IN = 642
ADAM_LR, ADAM_B1, ADAM_B2, ADAM_EPS, ADAM_WD, ADAM_STEP = 0.001, 0.9, 0.999, 1e-08, 0.01, 10
LANES = 128
VMEM_LIMIT = 56 * 1024 * 1024


def _params(sem=None):
    return pltpu.CompilerParams(dimension_semantics=sem, vmem_limit_bytes=VMEM_LIMIT)


def _tile(n, cap):
    best = None
    for t in range(LANES, min(n, cap) + 1, LANES):
        if n % t == 0:
            best = t
    assert best is not None, (n, cap)
    return best


def _nt(a, b):
    return lax.dot_general(a, b, (((1,), (1,)), ((), ())), preferred_element_type=F32)


def _tn(a, b):
    return lax.dot_general(a, b, (((0,), (0,)), ((), ())), preferred_element_type=F32)


def _nn(a, b):
    return jnp.dot(a, b, preferred_element_type=F32)


def _split_dot(a, b_exact, terms=3, dot=_nn):
    acc = None
    rest = a
    for _ in range(terms):
        hi = rest.astype(BF16)
        part = dot(hi, b_exact)
        acc = part if acc is None else acc + part
        rest = rest - hi.astype(F32)
    return acc


def _split_dot_left(a_exact, b, terms=3):
    acc = None
    rest = b
    for _ in range(terms):
        hi = rest.astype(BF16)
        part = _nn(a_exact, hi)
        acc = part if acc is None else acc + part
        rest = rest - hi.astype(F32)
    return acc


def _sigmoid(x):
    return 1.0 / (1.0 + jnp.exp(-x))


def _softplus(x):
    return jnp.maximum(x, 0.0) + jnp.log(1.0 + jnp.exp(-jnp.abs(x)))


def _iota2(shape, dim):
    return lax.broadcasted_iota(jnp.int32, shape, dim)


def rms_matmul(x, nw, wt, name):
    s, d = x.shape
    n = wt.shape[0]
    tm, tn = _tile(s, 512), _tile(n, 1408)

    def body(x_ref, nw_ref, w_ref, o_ref, h_ref):
        @pl.when(pl.program_id(1) == 0)
        def _():
            xv = x_ref[...]
            r = lax.rsqrt(jnp.mean(xv * xv, axis=-1, keepdims=True) + EPS)
            h_ref[...] = (xv * r * nw_ref[...]).astype(BF16)
        o_ref[...] = _nt(h_ref[...], w_ref[...])

    return pl.pallas_call(
        body, name=name, grid=(s // tm, n // tn),
        in_specs=[pl.BlockSpec((tm, d), lambda i, j: (i, 0)), pl.BlockSpec((1, d), lambda i, j: (0, 0)),
                  pl.BlockSpec((tn, d), lambda i, j: (j, 0))],
        out_specs=pl.BlockSpec((tm, tn), lambda i, j: (i, j)),
        out_shape=jax.ShapeDtypeStruct((s, n), F32),
        scratch_shapes=[pltpu.VMEM((tm, d), BF16)],
        compiler_params=_params(("parallel", "arbitrary")),
    )(x, nw, wt)


def matmul_residual(a, w, res, name):
    s, k = a.shape
    n = w.shape[1]
    tm, tn = _tile(s, 512), _tile(n, 512)

    def body(a_ref, w_ref, r_ref, o_ref):
        o_ref[...] = r_ref[...] + _nn(a_ref[...], w_ref[...])

    return pl.pallas_call(
        body, name=name, grid=(s // tm, n // tn),
        in_specs=[pl.BlockSpec((tm, k), lambda i, j: (i, 0)), pl.BlockSpec((k, tn), lambda i, j: (0, j)),
                  pl.BlockSpec((tm, tn), lambda i, j: (i, j))],
        out_specs=pl.BlockSpec((tm, tn), lambda i, j: (i, j)),
        out_shape=jax.ShapeDtypeStruct((s, n), F32),
        compiler_params=_params(("parallel", "parallel")),
    )(a, w, res)


def matmul_nt(a, w, name, out_dtype=F32):
    s, n = a.shape
    k = w.shape[0]
    tm, tk = _tile(s, 512), _tile(k, 512)

    def body(a_ref, w_ref, o_ref):
        o_ref[...] = _nt(a_ref[...], w_ref[...]).astype(out_dtype)

    return pl.pallas_call(
        body, name=name, grid=(s // tm, k // tk),
        in_specs=[pl.BlockSpec((tm, n), lambda i, j: (i, 0)), pl.BlockSpec((tk, n), lambda i, j: (j, 0))],
        out_specs=pl.BlockSpec((tm, tk), lambda i, j: (i, j)),
        out_shape=jax.ShapeDtypeStruct((s, k), out_dtype),
        compiler_params=_params(("parallel", "parallel")),
    )(a, w)


def matmul_tn(a, b, name, stack, layer):
    s, m = a.shape
    n = b.shape[1]
    tm, tn = _tile(m, 512), _tile(n, 512)

    def body(a_ref, b_ref, *rest):
        rest[-1][0] = _tn(a_ref[...], b_ref[...]).astype(BF16)

    ins = [a, b] if stack is None else [a, b, stack]
    return pl.pallas_call(
        body, name=name, grid=(m // tm, n // tn),
        in_specs=[pl.BlockSpec((s, tm), lambda i, j: (0, i)), pl.BlockSpec((s, tn), lambda i, j: (0, j))] + [ANY] * (len(ins) - 2),
        out_specs=pl.BlockSpec((1, tm, tn), lambda i, j: (layer, i, j)),
        out_shape=jax.ShapeDtypeStruct((DEPTH, m, n), BF16),
        input_output_aliases={} if stack is None else {2: 0},
        compiler_params=_params(("parallel", "parallel")),
    )(*ins)


def ffn_forward(x1, nw, wgt, wut, wd, name):
    s, d = x1.shape
    f = wgt.shape[0]
    tm, tf = _tile(s, 1024), _tile(f, 256)

    def body(x_ref, nw_ref, wg_ref, wu_ref, wd_ref, o_ref, g_ref, u_ref, h_ref, acc_ref):
        j = pl.program_id(1)

        @pl.when(j == 0)
        def _():
            xv = x_ref[...]
            r = lax.rsqrt(jnp.mean(xv * xv, axis=-1, keepdims=True) + EPS)
            h_ref[...] = (xv * r * nw_ref[...]).astype(BF16)
            acc_ref[...] = xv

        h = h_ref[...]
        g = _nt(h, wg_ref[...])
        u = _nt(h, wu_ref[...])
        g_ref[...] = g.astype(BF16)
        u_ref[...] = u.astype(BF16)
        a = (g * _sigmoid(g) * u).astype(BF16)
        acc_ref[...] += _nn(a, wd_ref[...])

        @pl.when(j == pl.num_programs(1) - 1)
        def _():
            o_ref[...] = acc_ref[...]

    wblk = pl.BlockSpec((tf, d), lambda i, j: (j, 0))
    return pl.pallas_call(
        body, name=name, grid=(s // tm, f // tf),
        in_specs=[pl.BlockSpec((tm, d), lambda i, j: (i, 0)), pl.BlockSpec((1, d), lambda i, j: (0, 0)), wblk, wblk, wblk],
        out_specs=[pl.BlockSpec((tm, d), lambda i, j: (i, 0)), pl.BlockSpec((tm, tf), lambda i, j: (i, j)),
                   pl.BlockSpec((tm, tf), lambda i, j: (i, j))],
        out_shape=[jax.ShapeDtypeStruct((s, d), F32), jax.ShapeDtypeStruct((s, f), BF16),
                   jax.ShapeDtypeStruct((s, f), BF16)],
        scratch_shapes=[pltpu.VMEM((tm, d), BF16), pltpu.VMEM((tm, d), F32)],
        compiler_params=_params(("parallel", "arbitrary")),
    )(x1, nw, wgt, wut, wd)


def ffn_backward_act(dx2, g, u, wd, name):
    s, d = dx2.shape
    f = wd.shape[0]
    tm, tf = _tile(s, 512), _tile(f, 1408)

    def body(dx_ref, g_ref, u_ref, wd_ref, dg_ref, du_ref, a_ref):
        da = _nt(dx_ref[...], wd_ref[...])
        gv = g_ref[...].astype(F32)
        uv = u_ref[...].astype(F32)
        sg = _sigmoid(gv)
        silu = gv * sg
        dg_ref[...] = (da * uv * (sg * (1.0 + gv * (1.0 - sg)))).astype(BF16)
        du_ref[...] = (da * silu).astype(BF16)
        a_ref[...] = (silu * uv).astype(BF16)

    blk = pl.BlockSpec((tm, tf), lambda i, j: (i, j))
    return pl.pallas_call(
        body, name=name, grid=(s // tm, f // tf),
        in_specs=[pl.BlockSpec((tm, d), lambda i, j: (i, 0)), blk, blk, pl.BlockSpec((tf, d), lambda i, j: (j, 0))],
        out_specs=[blk, blk, blk],
        out_shape=[jax.ShapeDtypeStruct((s, f), BF16)] * 3,
        compiler_params=_params(("parallel", "parallel")),
    )(dx2, g, u, wd)


def rms_backward(dzs, wts, x, nw, dres, name, tm):
    s, d = x.shape
    nz = len(dzs)

    def body(*refs):
        dz_refs, w_refs = refs[:nz], refs[nz:2 * nz]
        x_ref, nw_ref, dres_ref, dx_ref, dxb_ref, h_ref, dnw_ref = refs[2 * nz:]
        dh = _nn(dz_refs[0][...], w_refs[0][...])
        for k in range(1, nz):
            dh = dh + _nn(dz_refs[k][...], w_refs[k][...])
        xv = x_ref[...]
        r = lax.rsqrt(jnp.mean(xv * xv, axis=-1, keepdims=True) + EPS)
        xhat = xv * r
        nwv = nw_ref[...]
        h_ref[...] = (xhat * nwv).astype(BF16)

        @pl.when(pl.program_id(0) == 0)
        def _():
            dnw_ref[...] = jnp.zeros_like(dnw_ref)

        dnw_ref[...] += jnp.sum(dh * xhat, axis=0, keepdims=True)
        gdh = dh * nwv
        dx = dres_ref[...] + r * (gdh - xhat * jnp.mean(gdh * xhat, axis=-1, keepdims=True))
        dx_ref[...] = dx
        dxb_ref[...] = dx.astype(BF16)

    row = pl.BlockSpec((tm, d), lambda i: (i, 0))
    in_specs = [pl.BlockSpec((tm, dz.shape[1]), lambda i: (i, 0)) for dz in dzs]
    in_specs += [pl.BlockSpec(w.shape, lambda i: (0, 0)) for w in wts]
    in_specs += [row, pl.BlockSpec((1, d), lambda i: (0, 0)), row]
    return pl.pallas_call(
        body, name=name, grid=(s // tm,),
        in_specs=in_specs,
        out_specs=[row, row, row, pl.BlockSpec((1, d), lambda i: (0, 0))],
        out_shape=[jax.ShapeDtypeStruct((s, d), F32), jax.ShapeDtypeStruct((s, d), BF16),
                   jax.ShapeDtypeStruct((s, d), BF16), jax.ShapeDtypeStruct((1, d), F32)],
        compiler_params=_params(("arbitrary",)),
    )(*dzs, *wts, x, nw, dres)


def loss_head(x, nw, target, name):
    s, d = x.shape
    tm = _tile(s, 512)

    def body(x_ref, nw_ref, t_ref, loss_ref, dx_ref, dxb_ref, dnw_ref):
        xv = x_ref[...]
        r = lax.rsqrt(jnp.mean(xv * xv, axis=-1, keepdims=True) + EPS)
        xhat = xv * r
        nwv = nw_ref[...]
        err = xhat * nwv - t_ref[...]

        @pl.when(pl.program_id(0) == 0)
        def _():
            dnw_ref[...] = jnp.zeros_like(dnw_ref)
            loss_ref[...] = jnp.zeros_like(loss_ref)

        part = jnp.sum(jnp.sum(err * err, axis=-1, keepdims=True), axis=0, keepdims=True) * (0.5 / d)
        loss_ref[...] += jnp.broadcast_to(part, loss_ref.shape)
        dout = err * (1.0 / d)
        dnw_ref[...] += jnp.sum(dout * xhat, axis=0, keepdims=True)
        gdh = dout * nwv
        dx = r * (gdh - xhat * jnp.mean(gdh * xhat, axis=-1, keepdims=True))
        dx_ref[...] = dx
        dxb_ref[...] = dx.astype(BF16)

    row = pl.BlockSpec((tm, d), lambda i: (i, 0))
    return pl.pallas_call(
        body, name=name, grid=(s // tm,),
        in_specs=[row, pl.BlockSpec((1, d), lambda i: (0, 0)), row],
        out_specs=[pl.BlockSpec((1, LANES), lambda i: (0, 0)), row, row, pl.BlockSpec((1, d), lambda i: (0, 0))],
        out_shape=[jax.ShapeDtypeStruct((1, LANES), F32), jax.ShapeDtypeStruct((s, d), F32),
                   jax.ShapeDtypeStruct((s, d), BF16), jax.ShapeDtypeStruct((1, d), F32)],
        compiler_params=_params(("arbitrary",)),
    )(x, nw, target)


def _shift_down(x, k):
    return jnp.where(_iota2(x.shape, 0) >= k, pltpu.roll(x, k, axis=0), 0.0)


def _shift_up(x, k):
    s = x.shape[0]
    return jnp.where(_iota2(x.shape, 0) < s - k, pltpu.roll(x, s - k, axis=0), 0.0)


CONV_TILE = 256


def conv_forward(proj, cw, cb, name):
    s = proj.shape[0]
    tn = CONV_TILE
    off = OFF_XBC // tn

    def body(u_ref, w_ref, b_ref, o_ref):
        u = u_ref[...]
        pre = b_ref[...] + w_ref[CONV_WIDTH - 1:CONV_WIDTH, :] * u
        for i in range(CONV_WIDTH - 1):
            pre = pre + w_ref[i:i + 1, :] * _shift_down(u, CONV_WIDTH - 1 - i)
        o_ref[...] = pre * _sigmoid(pre)

    return pl.pallas_call(
        body, name=name, grid=(CONV_DIM // tn,),
        in_specs=[pl.BlockSpec((s, tn), lambda j: (0, off + j)), pl.BlockSpec((8, tn), lambda j: (0, j)),
                  pl.BlockSpec((1, tn), lambda j: (0, j))],
        out_specs=pl.BlockSpec((s, tn), lambda j: (0, j)),
        out_shape=jax.ShapeDtypeStruct((s, CONV_DIM), F32),
        compiler_params=_params(("parallel",)),
    )(proj, cw, cb)


def conv_backward(proj, dxc, cw, cb, dproj, name):
    s = proj.shape[0]
    tn = CONV_TILE
    off = OFF_XBC // tn

    def body(u_ref, d_ref, w_ref, b_ref, _, du_ref, dw_ref, db_ref):
        u = u_ref[...]
        shifted = [_shift_down(u, CONV_WIDTH - 1 - i) for i in range(CONV_WIDTH - 1)] + [u]
        pre = b_ref[...] + w_ref[CONV_WIDTH - 1:CONV_WIDTH, :] * u
        for i in range(CONV_WIDTH - 1):
            pre = pre + w_ref[i:i + 1, :] * shifted[i]
        sg = _sigmoid(pre)
        dpre = d_ref[...] * (sg * (1.0 + pre * (1.0 - sg)))
        du = w_ref[CONV_WIDTH - 1:CONV_WIDTH, :] * dpre
        for i in range(CONV_WIDTH - 1):
            du = du + w_ref[i:i + 1, :] * _shift_up(dpre, CONV_WIDTH - 1 - i)
        du_ref[...] = du.astype(BF16)
        rows = [jnp.sum(dpre * shifted[i], axis=0, keepdims=True) for i in range(CONV_WIDTH)]
        rows.append(jnp.zeros((8 - CONV_WIDTH, tn), F32))
        dw_ref[...] = jnp.concatenate(rows, axis=0)
        db_ref[...] = jnp.sum(dpre, axis=0, keepdims=True)

    return pl.pallas_call(
        body, name=name, grid=(CONV_DIM // tn,),
        in_specs=[pl.BlockSpec((s, tn), lambda j: (0, off + j)), pl.BlockSpec((s, tn), lambda j: (0, j)),
                  pl.BlockSpec((8, tn), lambda j: (0, j)), pl.BlockSpec((1, tn), lambda j: (0, j)), ANY],
        out_specs=[pl.BlockSpec((s, tn), lambda j: (0, off + j)), pl.BlockSpec((8, tn), lambda j: (0, j)),
                   pl.BlockSpec((1, tn), lambda j: (0, j))],
        out_shape=[jax.ShapeDtypeStruct(dproj.shape, BF16), jax.ShapeDtypeStruct((8, CONV_DIM), F32),
                   jax.ShapeDtypeStruct((1, CONV_DIM), F32)],
        input_output_aliases={4: 0},
        compiler_params=_params(("parallel",)),
    )(proj, dxc, cw, cb, dproj)


def _pool_lane_window(shape):
    grp = _iota2(shape, 1) // (POOL_WIDTH // len(POOL_WINDOWS))
    win = jnp.full(shape, POOL_WINDOWS[-1], jnp.int32)
    for gi in range(len(POOL_WINDOWS) - 2, -1, -1):
        win = jnp.where(grp == gi, POOL_WINDOWS[gi], win)
    return grp, win


def _pool_select(grp, sums):
    out = sums[-1]
    for gi in range(len(sums) - 2, -1, -1):
        out = jnp.where(grp == gi, sums[gi], out)
    return out


def _pool_pooled(p):
    grp, win = _pool_lane_window(p.shape)
    inv_count = 1.0 / jnp.minimum(_iota2(p.shape, 0) + 1, win).astype(F32)
    sums, acc, k = [], p, 1
    for _ in POOL_WINDOWS:
        acc = acc + _shift_down(acc, k)
        sums.append(acc)
        k *= 2
    return _pool_select(grp, sums) * inv_count - p, grp, inv_count


def pool_forward(proj, wbd, pb, ps, y_all, name):
    s = proj.shape[0]

    def body(p_ref, w_ref, b_ref, s_ref, _, o_ref):
        pooled, _, _ = _pool_pooled(p_ref[...])
        mixed = _nn(pooled.astype(BF16), w_ref[...]) + b_ref[...]
        o_ref[...] = (mixed * s_ref[...]).astype(BF16)

    vec = pl.BlockSpec((1, POOL_WIDTH), lambda j: (0, 0))
    return pl.pallas_call(
        body, name=name, grid=(1,),
        in_specs=[pl.BlockSpec((s, POOL_WIDTH), lambda j: (0, OFF_P // POOL_WIDTH)),
                  pl.BlockSpec((POOL_WIDTH, POOL_WIDTH), lambda j: (0, 0)), vec, vec, ANY],
        out_specs=pl.BlockSpec((s, POOL_WIDTH), lambda j: (0, (SSD_WIDTH + SB_WIDTH) // POOL_WIDTH)),
        out_shape=jax.ShapeDtypeStruct(y_all.shape, BF16),
        input_output_aliases={4: 0},
        compiler_params=_params(("arbitrary",)),
    )(proj, wbd, pb, ps, y_all)


def pool_backward(proj, dyall, wbd, pb, ps, dproj, name):
    s = proj.shape[0]

    def body(p_ref, dy_ref, w_ref, b_ref, s_ref, _, dp_ref, dw_ref, db_ref, ds_ref):
        pooled, grp, inv_count = _pool_pooled(p_ref[...])
        pooled_b = pooled.astype(BF16)
        mixed = _nn(pooled_b, w_ref[...]) + b_ref[...]
        dy = dy_ref[...]
        ds_ref[...] = jnp.sum(dy * mixed, axis=0, keepdims=True)
        dmixed = dy * s_ref[...]
        db_ref[...] = jnp.sum(dmixed, axis=0, keepdims=True)
        dmixed_b = dmixed.astype(BF16)
        dw_ref[...] = _tn(pooled_b, dmixed_b)
        dpooled = _nt(dmixed_b, w_ref[...])
        sums, acc, k = [], dpooled * inv_count, 1
        for _ in POOL_WINDOWS:
            acc = acc + _shift_up(acc, k)
            sums.append(acc)
            k *= 2
        dp_ref[...] = (_pool_select(grp, sums) - dpooled).astype(BF16)

    vec = pl.BlockSpec((1, POOL_WIDTH), lambda j: (0, 0))
    mat = pl.BlockSpec((POOL_WIDTH, POOL_WIDTH), lambda j: (0, 0))
    pcol = pl.BlockSpec((s, POOL_WIDTH), lambda j: (0, OFF_P // POOL_WIDTH))
    return pl.pallas_call(
        body, name=name, grid=(1,),
        in_specs=[pcol, pl.BlockSpec((s, POOL_WIDTH), lambda j: (0, (SSD_WIDTH + SB_WIDTH) // POOL_WIDTH)), mat, vec, vec, ANY],
        out_specs=[pcol, mat, vec, vec],
        out_shape=[jax.ShapeDtypeStruct(dproj.shape, BF16), jax.ShapeDtypeStruct((POOL_WIDTH, POOL_WIDTH), F32),
                   jax.ShapeDtypeStruct((1, POOL_WIDTH), F32), jax.ShapeDtypeStruct((1, POOL_WIDTH), F32)],
        input_output_aliases={5: 0},
        compiler_params=_params(("arbitrary",)),
    )(proj, dyall, wbd, pb, ps, dproj)


N_PAIRS = SSD_HEADS // 2


def _ssd_common(xc, dtraw, dtb, alog):
    c = CHUNK
    dt = _softplus(dtraw + dtb)
    a = -jnp.exp(alog)
    ltri = (_iota2((c, c), 0) >= _iota2((c, c), 1)).astype(BF16)
    acum = _split_dot_left(ltri, dt * a)
    expand = (_iota2((c, SSD_WIDTH), 1) // HEAD_DIM == _iota2((c, SSD_WIDTH), 0)).astype(BF16)
    expand_wide = (_iota2((c, SSD_HEADS * c), 1) // c == _iota2((c, SSD_HEADS * c), 0)).astype(BF16)
    acum_x = _split_dot(acum, expand)
    dt_x = _split_dot(dt, expand)
    alast_x = acum_x[c - 1:c, :]
    return dict(dt=dt, a=a, acum=acum, acum_x=acum_x, dt_x=dt_x, ea_x=jnp.exp(acum_x),
                dte_x=jnp.exp(alast_x - acum_x), eal_x=jnp.exp(alast_x),
                acol=_split_dot(acum, expand_wide), acum_t=acum.T,
                xs=xc[:, :SSD_WIDTH], causal=_iota2((c, c), 0) >= _iota2((c, c), 1),
                left=_iota2((c, c), 1) < HEAD_DIM)


def _ssd_group(xc, g):
    b = xc[:, SSD_WIDTH + D_STATE * g:SSD_WIDTH + D_STATE * (g + 1)]
    cm = xc[:, SSD_WIDTH + 2 * D_STATE + D_STATE * g:SSD_WIDTH + 2 * D_STATE + D_STATE * (g + 1)]
    return b, cm


def _ssd_decay(q, hh):
    col = q["acol"][:, CHUNK * hh:CHUNK * (hh + 1)]
    row = q["acum_t"][hh:hh + 1, :]
    return jnp.where(q["causal"], jnp.exp(jnp.minimum(col - row, 0.0)), 0.0)


def ssd_forward(proj, xc, dtb, alog, dskip_x, nw, name):
    s = xc.shape[0]
    nc = s // CHUNK

    def body(xc_ref, zdt_ref, dtb_ref, alog_ref, dsk_ref, nw_ref, y_ref, yc_ref, st_ref, state):
        @pl.when(pl.program_id(0) == 0)
        def _():
            state[...] = jnp.zeros_like(state)

        xcv = xc_ref[...]
        q = _ssd_common(xcv, zdt_ref[:, SSD_WIDTH:SSD_WIDTH + LANES], dtb_ref[...], alog_ref[...])
        x = q["xs"] * q["dt_x"]
        xb = x.astype(BF16)
        xd = (x * q["dte_x"]).astype(BF16)
        pieces = []
        for g in range(2):
            bg, cg = _ssd_group(xcv, g)
            bgb, cgb = bg.astype(BF16), cg.astype(BF16)
            cb = _nt(cgb, bgb)
            bgt = bg.T.astype(BF16)
            for pr in (2 * g, 2 * g + 1):
                sl = slice(CHUNK * pr, CHUNK * (pr + 1))
                st = state[pr]
                st_ref[0, pr] = st
                yp = _nn(cgb, st.astype(BF16)) * q["ea_x"][:, sl]
                for k, hh in enumerate((2 * pr, 2 * pr + 1)):
                    w = (cb * _ssd_decay(q, hh)).astype(BF16)
                    mask = q["left"] if k == 0 else jnp.logical_not(q["left"])
                    yp = yp + _nn(w, jnp.where(mask, xb[:, sl], jnp.zeros_like(xb[:, sl])))
                state[pr] = st * q["eal_x"][:, sl] + _nn(bgt, xd[:, sl])
                pieces.append(yp)
        y = jnp.concatenate(pieces, axis=1) + q["xs"] * dsk_ref[...]
        yc_ref[...] = y
        zv = zdt_ref[:, :SSD_WIDTH]
        yg = y * (zv * _sigmoid(zv))
        r = lax.rsqrt(jnp.mean(yg * yg, axis=-1, keepdims=True) + EPS)
        y_ref[...] = (yg * r * nw_ref[...]).astype(BF16)

    vec = lambda n: pl.BlockSpec((1, n), lambda c: (0, 0))
    return pl.pallas_call(
        body, name=name, grid=(nc,),
        in_specs=[pl.BlockSpec((CHUNK, CONV_DIM), lambda c: (c, 0)),
                  pl.BlockSpec((CHUNK, ZDT), lambda c: (c, OFF_Z // ZDT)),
                  vec(LANES), vec(LANES), vec(SSD_WIDTH), vec(SSD_WIDTH)],
        out_specs=[pl.BlockSpec((CHUNK, SSD_WIDTH), lambda c: (c, 0)), pl.BlockSpec((CHUNK, SSD_WIDTH), lambda c: (c, 0)),
                   pl.BlockSpec((1, N_PAIRS, D_STATE, CHUNK), lambda c: (c, 0, 0, 0))],
        out_shape=[jax.ShapeDtypeStruct((s, D_MODEL), BF16), jax.ShapeDtypeStruct((s, SSD_WIDTH), F32),
                   jax.ShapeDtypeStruct((nc, N_PAIRS, D_STATE, CHUNK), F32)],
        scratch_shapes=[pltpu.VMEM((N_PAIRS, D_STATE, CHUNK), F32)],
        compiler_params=_params(("arbitrary",)),
    )(xc, proj, dtb, alog, dskip_x, nw)


def ssd_backward(proj, xc, ycore, dyall, states, dtb, alog, dskip_x, nw, name):
    s = xc.shape[0]
    nc = s // CHUNK
    c = CHUNK

    def body(xc_ref, zdt_ref, yc_ref, dy_ref, st_ref, dtb_ref, alog_ref, dsk_ref, nw_ref,
             dxc_ref, dzdt_ref, dnw_ref, ddsk_ref, ddtb_ref, dalog_ref, dstate):
        @pl.when(pl.program_id(0) == 0)
        def _():
            dstate[...] = jnp.zeros_like(dstate)
            dnw_ref[...] = jnp.zeros_like(dnw_ref)
            ddsk_ref[...] = jnp.zeros_like(ddsk_ref)
            ddtb_ref[...] = jnp.zeros_like(ddtb_ref)
            dalog_ref[...] = jnp.zeros_like(dalog_ref)

        xcv = xc_ref[...]
        dtraw = zdt_ref[:, SSD_WIDTH:SSD_WIDTH + LANES]
        q = _ssd_common(xcv, dtraw, dtb_ref[...], alog_ref[...])
        xs = q["xs"]
        x = xs * q["dt_x"]
        zv, yc, dy, nwv = zdt_ref[:, :SSD_WIDTH], yc_ref[...], dy_ref[...], nw_ref[...]
        sgz = _sigmoid(zv)
        siluz = zv * sgz
        yg = yc * siluz
        r = lax.rsqrt(jnp.mean(yg * yg, axis=-1, keepdims=True) + EPS)
        dnw_ref[...] += jnp.sum(dy * yg * r, axis=0, keepdims=True)
        g1 = dy * nwv
        dyg = r * (g1 - yg * (r * r) * jnp.mean(g1 * yg, axis=-1, keepdims=True))
        dyv = dyg * siluz
        dz = (dyg * yc * (sgz * (1.0 + zv * (1.0 - sgz)))).astype(BF16)
        ddsk_ref[...] += jnp.sum(dyv * xs, axis=0, keepdims=True)
        dye = dyv * q["ea_x"]
        dx_parts, yoff_parts, u_parts, v_parts, e_parts = [], [], [], [], []
        db_parts, dc_parts = [], []
        for g in range(2):
            bg, cg = _ssd_group(xcv, g)
            bgb, cgb = bg.astype(BF16), cg.astype(BF16)
            cb = _nt(cgb, bgb)
            cgt = cg.T.astype(BF16)
            dgsum = jnp.zeros((c, c), F32)
            dbg = jnp.zeros((c, D_STATE), F32)
            dcg = jnp.zeros((c, D_STATE), F32)
            for pr in (2 * g, 2 * g + 1):
                sl = slice(c * pr, c * (pr + 1))
                st = st_ref[0, pr]
                dst = dstate[pr]
                stb, dstb = st.astype(BF16), dst.astype(BF16)
                xp = x[:, sl]
                xpb = xp.astype(BF16)
                dyp = dyv[:, sl]
                xdp = xp * q["dte_x"][:, sl]
                yoff_parts.append(_nn(cgb, stb) * q["ea_x"][:, sl])
                rr = _nn(bgb, dstb)
                dxp = rr * q["dte_x"][:, sl]
                u_parts.append(rr * xdp)
                v_parts.append(dst * st * q["eal_x"][:, sl])
                for k, hh in enumerate((2 * pr, 2 * pr + 1)):
                    decay = _ssd_decay(q, hh)
                    w = cb * decay
                    mask = q["left"] if k == 0 else jnp.logical_not(q["left"])
                    dym = jnp.where(mask, dyp, 0.0).astype(BF16)
                    dw = _nt(dym, xpb)
                    dgsum = dgsum + dw * decay
                    e_parts.append(dw * w)
                    dxp = dxp + _nn(w.T.astype(BF16), dym)
                dyeb = dye[:, sl].astype(BF16)
                dcg = dcg + _nt(dyeb, stb)
                dbg = dbg + _nt(xdp.astype(BF16), dstb)
                dstate[pr] = dst * q["eal_x"][:, sl] + _nn(cgt, dyeb)
                dx_parts.append(dxp)
            dcg = dcg + _nn(dgsum.astype(BF16), bgb)
            dbg = dbg + _nn(dgsum.T.astype(BF16), cgb)
            db_parts.append(dbg)
            dc_parts.append(dcg)
        dx = jnp.concatenate(dx_parts, axis=1)
        yoff = jnp.concatenate(yoff_parts, axis=1)
        u = jnp.concatenate(u_parts, axis=1)
        v = jnp.concatenate(v_parts, axis=1)
        reduce_heads = (_iota2((SSD_WIDTH, c), 0) // HEAD_DIM == _iota2((SSD_WIDTH, c), 1)).astype(BF16)
        to_head = (_iota2((SSD_HEADS * c, c), 0) // c == _iota2((SSD_HEADS * c, c), 1)).astype(BF16)
        da = _split_dot(dyv * yoff - u, reduce_heads, 2)
        da = da + _split_dot(jnp.concatenate(e_parts, axis=1), to_head, 2)
        da = da - _split_dot(jnp.concatenate(e_parts, axis=0), to_head, 2, dot=_tn)
        dalast = jnp.sum(_split_dot(u + v, reduce_heads, 2), axis=0, keepdims=True)
        da = da + jnp.where(_iota2((c, c), 0) == c - 1, dalast, 0.0)
        utri = (_iota2((c, c), 1) >= _iota2((c, c), 0)).astype(BF16)
        dda = _split_dot_left(utri, da)
        ddt = dda * q["a"] + _split_dot(dx * xs, reduce_heads, 2)
        dalog_ref[...] += jnp.sum(dda * q["dt"], axis=0, keepdims=True) * q["a"]
        ddtraw = jnp.where(_iota2((c, c), 1) < SSD_HEADS, ddt * _sigmoid(dtraw + dtb_ref[...]), 0.0)
        ddtb_ref[...] += jnp.sum(ddtraw, axis=0, keepdims=True)
        dzdt_ref[...] = jnp.concatenate([dz, ddtraw.astype(BF16), jnp.zeros((c, ZDT - SSD_WIDTH - LANES), BF16)], axis=1)
        dxs = dx * q["dt_x"] + dyv * dsk_ref[...]
        dxc_ref[...] = jnp.concatenate([dxs] + db_parts + dc_parts, axis=1)

    rev = lambda i: nc - 1 - i
    vec = lambda n: pl.BlockSpec((1, n), lambda i: (0, 0))
    wide = pl.BlockSpec((c, SSD_WIDTH), lambda i: (rev(i), 0))
    zdt = pl.BlockSpec((c, ZDT), lambda i: (rev(i), OFF_Z // ZDT))
    return pl.pallas_call(
        body, name=name, grid=(nc,),
        in_specs=[pl.BlockSpec((c, CONV_DIM), lambda i: (rev(i), 0)), zdt, wide, wide,
                  pl.BlockSpec((1, N_PAIRS, D_STATE, c), lambda i: (rev(i), 0, 0, 0)),
                  vec(LANES), vec(LANES), vec(SSD_WIDTH), vec(SSD_WIDTH)],
        out_specs=[pl.BlockSpec((c, CONV_DIM), lambda i: (rev(i), 0)), zdt,
                   vec(SSD_WIDTH), vec(SSD_WIDTH), vec(LANES), vec(LANES)],
        out_shape=[jax.ShapeDtypeStruct((s, CONV_DIM), F32), jax.ShapeDtypeStruct((s, D_INP), BF16),
                   jax.ShapeDtypeStruct((1, SSD_WIDTH), F32),
                   jax.ShapeDtypeStruct((1, SSD_WIDTH), F32), jax.ShapeDtypeStruct((1, LANES), F32),
                   jax.ShapeDtypeStruct((1, LANES), F32)],
        scratch_shapes=[pltpu.VMEM((N_PAIRS, D_STATE, c), F32)],
        compiler_params=_params(("arbitrary",)),
    )(xc, proj, ycore, dyall, states, dtb, alog, dskip_x, nw)


SB_BLOCK = 256
SB_SCALE = HEAD_DIM ** -0.5


def _sb_masks(i, j):
    t = SB_BLOCK
    return (_iota2((t, t), 1) + j * t) < (_iota2((t, t), 0) + i * t)


def _sb_weights(qm, kb, valid, run_lk, strict_after):
    z = _nt(qm, kb)
    ls = -_softplus(-z)
    lk = jnp.where(valid, ls - z, 0.0)
    after = _split_dot(lk, strict_after, 2) + run_lk
    w = jnp.where(valid, jnp.exp(ls + after), 0.0)
    return z, lk, w


def sb_forward(proj, y_all, name):
    s = proj.shape[0]
    t = SB_BLOCK
    nq = s // t

    def body(q_ref, k_ref, v_ref, _, y_ref, o_ref):
        i = pl.program_id(1)
        left = _iota2((t, LANES), 1) < HEAD_DIM
        qv = q_ref[...] * SB_SCALE
        zero = jnp.zeros_like(qv)
        qms = (jnp.where(left, qv, zero).astype(BF16), jnp.where(left, zero, qv).astype(BF16))
        strict_after = (_iota2((t, t), 0) > _iota2((t, t), 1)).astype(BF16)

        def step(jj, carry):
            o, runs = carry[0], carry[1:]
            j = i - jj
            rows = pl.ds(pl.multiple_of(j * t, t), t)
            kb = k_ref[rows, :].astype(BF16)
            vv = v_ref[rows, :]
            valid = _sb_masks(i, j)
            new_runs = []
            for k in range(2):
                _, lk, w = _sb_weights(qms[k], kb, valid, runs[k], strict_after)
                vm = jnp.where(left if k == 0 else jnp.logical_not(left), vv, 0.0).astype(BF16)
                o = o + _nn(w.astype(BF16), vm)
                new_runs.append(runs[k] + jnp.sum(lk, axis=1, keepdims=True))
            return (o, *new_runs)

        init = (jnp.zeros((t, LANES), F32), jnp.zeros((t, 1), F32), jnp.zeros((t, 1), F32))
        o = lax.fori_loop(0, i + 1, step, init)[0]
        o_ref[...] = o
        y_ref[...] = o.astype(BF16)

    return pl.pallas_call(
        body, name=name, grid=(2, nq),
        in_specs=[pl.BlockSpec((t, LANES), lambda p, i: (i, 3 * p)),
                  pl.BlockSpec((s, LANES), lambda p, i: (0, 3 * p + 1)),
                  pl.BlockSpec((s, LANES), lambda p, i: (0, 3 * p + 2)), ANY],
        out_specs=[pl.BlockSpec((t, LANES), lambda p, i: (i, SSD_WIDTH // LANES + p)),
                   pl.BlockSpec((t, LANES), lambda p, i: (i, p))],
        out_shape=[jax.ShapeDtypeStruct(y_all.shape, BF16), jax.ShapeDtypeStruct((s, SB_WIDTH), F32)],
        input_output_aliases={3: 0},
        compiler_params=_params(("parallel", "arbitrary")),
    )(proj, proj, proj, y_all)


def sb_backward(proj, o, dyall, dproj, name):
    s = proj.shape[0]
    t = SB_BLOCK
    nq = s // t

    def body(q_ref, k_ref, v_ref, o_ref, do_ref, _, dqkv_ref, dk_acc, dv_acc):
        dk_acc[...] = jnp.zeros_like(dk_acc)
        dv_acc[...] = jnp.zeros_like(dv_acc)
        left = _iota2((t, LANES), 1) < HEAD_DIM
        lane_masks = (left, jnp.logical_not(left))
        strict_after = (_iota2((t, t), 0) > _iota2((t, t), 1)).astype(BF16)
        from_here = (_iota2((t, t), 0) >= _iota2((t, t), 1)).astype(BF16)

        def query_block(i, _):
            qrows = pl.ds(pl.multiple_of(i * t, t), t)
            qv = q_ref[qrows, :] * SB_SCALE
            dov = do_ref[qrows, :]
            zero = jnp.zeros_like(qv)
            qb = qv.astype(BF16)
            dob = dov.astype(BF16)
            prod = dob.astype(F32) * o_ref[qrows, :]
            qms = [jnp.where(m, qv, zero).astype(BF16) for m in lane_masks]
            doms = [jnp.where(m, dov, zero).astype(BF16) for m in lane_masks]
            deltas = [jnp.sum(jnp.where(m, prod, zero), axis=1, keepdims=True) for m in lane_masks]

            def step(jj, carry):
                dq = carry[0]
                run_lk, run_e = carry[1:3], carry[3:5]
                j = i - jj
                rows = pl.ds(pl.multiple_of(j * t, t), t)
                kb = k_ref[rows, :].astype(BF16)
                vb = v_ref[rows, :].astype(BF16)
                valid = _sb_masks(i, j)
                dkj = jnp.zeros((t, LANES), F32)
                dvj = jnp.zeros((t, LANES), F32)
                new_lk, new_e = [], []
                for k in range(2):
                    z, lk, w = _sb_weights(qms[k], kb, valid, run_lk[k], strict_after)
                    sg = _sigmoid(z)
                    wb = w.astype(BF16)
                    e = _nt(doms[k], vb) * wb.astype(F32)
                    before = deltas[k] - _split_dot(e, from_here, 2) - run_e[k]
                    dz = jnp.where(valid, e * (1.0 - sg) - sg * before, 0.0).astype(BF16)
                    m = lane_masks[k]
                    dvj = dvj + jnp.where(m, _tn(wb, dob), 0.0)
                    dkj = dkj + jnp.where(m, _tn(dz, qb), 0.0)
                    dq = dq + jnp.where(m, _nn(dz, kb), 0.0)
                    new_lk.append(run_lk[k] + jnp.sum(lk, axis=1, keepdims=True))
                    new_e.append(run_e[k] + jnp.sum(e, axis=1, keepdims=True))
                dk_acc[rows, :] += dkj
                dv_acc[rows, :] += dvj
                return (dq, *new_lk, *new_e)

            col = jnp.zeros((t, 1), F32)
            dq = lax.fori_loop(0, i + 1, step, (jnp.zeros((t, LANES), F32), col, col, col, col))[0]
            dqkv_ref[qrows, 0:LANES] = (dq * SB_SCALE).astype(BF16)
            return 0

        lax.fori_loop(0, nq, query_block, 0)
        dqkv_ref[:, LANES:2 * LANES] = dk_acc[...].astype(BF16)
        dqkv_ref[:, 2 * LANES:3 * LANES] = dv_acc[...].astype(BF16)

    col = lambda f: pl.BlockSpec((s, LANES), f)
    return pl.pallas_call(
        body, name=name, grid=(2,),
        in_specs=[col(lambda p: (0, 3 * p)), col(lambda p: (0, 3 * p + 1)), col(lambda p: (0, 3 * p + 2)),
                  col(lambda p: (0, p)), col(lambda p: (0, SSD_WIDTH // LANES + p)), ANY],
        out_specs=pl.BlockSpec((s, 3 * LANES), lambda p: (0, p)),
        out_shape=jax.ShapeDtypeStruct(dproj.shape, BF16),
        input_output_aliases={5: 0},
        scratch_shapes=[pltpu.VMEM((s, LANES), F32), pltpu.VMEM((s, LANES), F32)],
        compiler_params=_params(("parallel",)),
    )(proj, proj, proj, o, dyall, dproj)


def adamw(w, g, m, v, name):
    b, r, c = w.shape
    tr = max([t for t in range(8, min(r, 512) + 1, 8) if r % t == 0], default=r)

    def body(w_ref, g_ref, m_ref, v_ref, d_ref, nm_ref, nv_ref):
        gv = g_ref[...]
        nm = ADAM_B1 * m_ref[...] + (1.0 - ADAM_B1) * gv
        nv = ADAM_B2 * v_ref[...] + (1.0 - ADAM_B2) * (gv * gv)
        m_hat = nm / (1.0 - ADAM_B1 ** ADAM_STEP)
        v_hat = nv / (1.0 - ADAM_B2 ** ADAM_STEP)
        d_ref[...] = -ADAM_LR * (m_hat / (jnp.sqrt(v_hat) + ADAM_EPS) + ADAM_WD * w_ref[...])
        nm_ref[...] = nm
        nv_ref[...] = nv

    blk = pl.BlockSpec((1, tr, c), lambda i, j: (i, j, 0))
    return pl.pallas_call(
        body, name=name, grid=(b, r // tr),
        in_specs=[blk] * 4, out_specs=[blk] * 3,
        out_shape=[jax.ShapeDtypeStruct(w.shape, F32)] * 3,
        compiler_params=_params(("parallel", "parallel")),
    )(w, g, m, v)


def _position():
    return lax.axis_index("x"), lax.axis_index("y"), lax.axis_index("c")


def _flipped(pos, flip):
    return tuple((1 - p) if f else p for p, f in zip(pos, flip))


FLIP_C = (0, 0, 1)
CHIP_FLIPS = {1: (0, 1, 0), 2: (1, 0, 0), 3: (1, 1, 0)}
SHARD_ROWS = (SHARD_IN, SHARD_OUT, SHARD_FF, SHARD_FF, SHARD_FF)


def _rows(start, size):
    return pl.ds(pl.multiple_of(start, 16), size)


def exchange(name, ins, out_shapes, aliases, plan):
    n_in = len(ins)

    def body(*refs):
        in_refs, out_refs = refs[:n_in], refs[n_in:n_in + len(out_shapes)]
        send_sems, recv_sems, local_sems = refs[n_in + len(out_shapes):]
        pos = _position()
        remote, local = plan(pos, in_refs, out_refs)
        copies = []
        for k, (src, dst) in enumerate(local):
            cp = pltpu.make_async_copy(src, dst, local_sems.at[k])
            cp.start()
            copies.append(cp)
        for k, (src, dst, flip) in enumerate(remote):
            cp = pltpu.make_async_remote_copy(src_ref=src, dst_ref=dst, send_sem=send_sems.at[k], recv_sem=recv_sems.at[k],
                                              device_id=_flipped(pos, flip), device_id_type=MESH)
            cp.start()
            copies.append(cp)
        for cp in copies:
            cp.wait()

    n_remote, n_local = plan.counts
    return pl.pallas_call(
        body, name=name,
        in_specs=[ANY] * n_in, out_specs=[ANY] * len(out_shapes), out_shape=out_shapes,
        input_output_aliases=aliases,
        scratch_shapes=[pltpu.SemaphoreType.DMA((n_remote,)), pltpu.SemaphoreType.DMA((n_remote,)),
                        pltpu.SemaphoreType.DMA((max(n_local, 1),))],
    )(*ins)


def _plan(counts):
    def wrap(fn):
        fn.counts = counts
        return fn
    return wrap


def gather_weights(shards):
    fulls = [jax.ShapeDtypeStruct((DEPTH, N_CHIPS * r, D_MODEL), BF16) for r in SHARD_ROWS]

    @_plan((15, 0))
    def over_ici(pos, ins, outs):
        chip, c = 2 * pos[0] + pos[1], pos[2]
        remote = []
        for src, dst, r in zip(ins, outs, SHARD_ROWS):
            h = r // 2
            for f in (1, 2, 3):
                remote.append((src.at[:, _rows(c * h, h)], dst.at[:, _rows(chip * r + c * h, h)], CHIP_FLIPS[f]))
        return remote, []

    @_plan((20, 0))
    def over_d2d(pos, ins, outs):
        chip, c = 2 * pos[0] + pos[1], pos[2]
        remote = []
        for src, own, dst, r in zip(ins[:5], ins[5:], outs, SHARD_ROWS):
            h = r // 2
            remote.append((own, dst.at[:, _rows(chip * r, r)], FLIP_C))
            for f in (1, 2, 3):
                at = _rows(lax.bitwise_xor(chip, f) * r + c * h, h)
                remote.append((src.at[:, at], dst.at[:, at], FLIP_C))
        return remote, []

    part = exchange("gather_ici", shards, fulls, {}, over_ici)
    return exchange("gather_d2d", list(part) + list(shards), fulls, {k: k for k in range(5)}, over_d2d)


def add_halves(d, recv, half, name):
    dep, nch, r, c = d.shape
    h = r // 2

    def body(half_ref, d_ref, r_ref, o_ref):
        o_ref[...] = (d_ref[...].astype(F32) + r_ref[...].astype(F32)).astype(BF16)

    return pl.pallas_call(
        body, name=name,
        grid_spec=pltpu.PrefetchScalarGridSpec(
            num_scalar_prefetch=1, grid=(dep, nch),
            in_specs=[pl.BlockSpec((1, 1, h, c), lambda l, j, hf: (l, j, hf[0], 0)),
                      pl.BlockSpec((1, 1, h, c), lambda l, j, hf: (l, j, 0, 0))],
            out_specs=pl.BlockSpec((1, 1, h, c), lambda l, j, hf: (l, j, 0, 0))),
        out_shape=jax.ShapeDtypeStruct(recv.shape, BF16),
        compiler_params=_params(("parallel", "parallel")),
    )(half, d, recv)


def add_chips(p, recv, chip, name):
    dep, _, r, c = p.shape

    def body(chip_ref, p_ref, r_ref, o_ref):
        acc = p_ref[0, 0].astype(F32)
        for k in range(N_CHIPS - 1):
            acc = acc + r_ref[0, k].astype(F32)
        o_ref[0] = acc

    return pl.pallas_call(
        body, name=name,
        grid_spec=pltpu.PrefetchScalarGridSpec(
            num_scalar_prefetch=1, grid=(dep,),
            in_specs=[pl.BlockSpec((1, 1, r, c), lambda l, ch: (l, ch[0], 0, 0)),
                      pl.BlockSpec((1, N_CHIPS - 1, r, c), lambda l, ch: (l, 0, 0, 0))],
            out_specs=pl.BlockSpec((1, r, c), lambda l, ch: (l, ch[1], 0))),
        out_shape=jax.ShapeDtypeStruct((dep, 2 * r, c), F32),
        compiler_params=_params(("parallel",)),
    )(chip, p, recv)


def reduce_gradients(stacks, half_arr, chip_arr):
    views = [s.reshape(DEPTH, N_CHIPS, r, D_MODEL) for s, r in zip(stacks, SHARD_ROWS)]

    @_plan((5, 0))
    def to_sibling(pos, ins, outs):
        c = pos[2]
        return [(src.at[:, :, _rows((1 - c) * (r // 2), r // 2)], dst, FLIP_C) for src, dst, r in zip(ins, outs, SHARD_ROWS)], []

    from_sibling = exchange("reduce_d2d", views,
                            [jax.ShapeDtypeStruct((DEPTH, N_CHIPS, r // 2, D_MODEL), BF16) for r in SHARD_ROWS], {}, to_sibling)
    chip_sums = [add_halves(d, rv, half_arr, "reduce_add_halves") for d, rv in zip(views, from_sibling)]

    @_plan((15, 0))
    def to_chips(pos, ins, outs):
        chip = 2 * pos[0] + pos[1]
        remote = []
        for src, dst in zip(ins, outs):
            for f in (1, 2, 3):
                remote.append((src.at[:, lax.bitwise_xor(chip, f)], dst.at[:, f - 1], CHIP_FLIPS[f]))
        return remote, []

    from_chips = exchange("reduce_ici", chip_sums,
                          [jax.ShapeDtypeStruct((DEPTH, N_CHIPS - 1, r // 2, D_MODEL), BF16) for r in SHARD_ROWS], {}, to_chips)
    chip_half = jnp.concatenate([chip_arr, half_arr])
    mine = [add_chips(p, rv, chip_half, "reduce_add_chips") for p, rv in zip(chip_sums, from_chips)]

    @_plan((5, 0))
    def swap(pos, ins, outs):
        c = pos[2]
        remote = []
        for src, dst, r in zip(ins, outs, SHARD_ROWS):
            at = _rows(c * (r // 2), r // 2)
            remote.append((src.at[:, at], dst.at[:, at], FLIP_C))
        return remote, []

    return exchange("reduce_swap", mine, [jax.ShapeDtypeStruct((DEPTH, r, D_MODEL), F32) for r in SHARD_ROWS],
                    {k: k for k in range(5)}, swap)


def small_allreduce(v, name):
    r, c = v.shape

    def body(v_ref, o_ref, buf, send_sems, recv_sems):
        pos = _position()
        me = 4 * pos[0] + 2 * pos[1] + pos[2]
        buf[0] = v_ref[...]
        copies = []
        for f in range(1, 8):
            flip = ((f >> 2) & 1, (f >> 1) & 1, f & 1)
            cp = pltpu.make_async_remote_copy(
                src_ref=v_ref, dst_ref=buf.at[f], send_sem=send_sems.at[f - 1], recv_sem=recv_sems.at[f - 1],
                device_id=_flipped(pos, flip), device_id_type=MESH)
            cp.start()
            copies.append(cp)
        for cp in copies:
            cp.wait()
        acc = buf[me]
        for d in range(1, 8):
            acc = acc + buf[lax.bitwise_xor(me, d)]
        o_ref[...] = acc

    return pl.pallas_call(
        body, name=name,
        in_specs=[pl.BlockSpec(memory_space=pltpu.VMEM)], out_specs=pl.BlockSpec(memory_space=pltpu.VMEM),
        out_shape=jax.ShapeDtypeStruct((r, c), F32),
        scratch_shapes=[pltpu.VMEM((8, r, c), F32), pltpu.SemaphoreType.DMA((7,)), pltpu.SemaphoreType.DMA((7,))],
    )(v)


_IN_SEGMENTS = ((0, 1544, 128), (128, 1800, 128), (256, 2056, 128), (384, 1672, 128), (512, 1928, 128), (640, 2184, 128),
                (OFF_Z, 0, SSD_WIDTH), (OFF_DT, 1536, SSD_HEADS), (OFF_XBC, 512, CONV_DIM), (OFF_P, 2312, POOL_WIDTH))


def _in_column_map():
    m = np.full((D_INP,), -1, np.int64)
    for at, orig, n in _IN_SEGMENTS:
        cols = np.arange(orig, orig + n)
        m[at:at + n] = (cols // COLS_IN) * SHARD_IN + cols % COLS_IN
    return m


def _runs(idx):
    out, i = [], 0
    while i < len(idx):
        j = i + 1
        while j < len(idx) and ((idx[i] < 0 and idx[j] < 0) or (idx[i] >= 0 and idx[j] == idx[j - 1] + 1)):
            j += 1
        out.append((int(idx[i]), j - i))
        i = j
    return out


def _take_rows(a, idx):
    parts = []
    for first, n in _runs(idx):
        parts.append(jnp.zeros((a.shape[0], n, a.shape[2]), a.dtype) if first < 0 else a[:, first:first + n])
    return jnp.concatenate(parts, axis=1)


def _in_weight_layout(staged):
    return _take_rows(staged, _in_column_map())


def _in_gradient_layout(dwt):
    fwd = _in_column_map()
    inv = np.full((N_CHIPS * SHARD_IN,), -1, np.int64)
    inv[fwd[fwd >= 0]] = np.nonzero(fwd >= 0)[0]
    return _take_rows(dwt, inv)


SMALL_NAMES = ("norm1_w", "conv_w", "conv_b", "dt_bias", "a_log", "d_skip", "ssd_norm_w", "pool_w", "pool_b",
               "pool_scale", "norm2_w", "final_norm_w")
SMALL_ROWS = 104


def _pack_small(parts):
    flat = jnp.concatenate([p.reshape(-1) for p in parts])
    return jnp.pad(flat, (0, SMALL_ROWS * D_MODEL - flat.shape[0])).reshape(SMALL_ROWS, D_MODEL)


def _unpack_small(flat, shapes):
    flat = flat.reshape(-1)
    out, at = [], 0
    for shp in shapes:
        n = int(np.prod(shp))
        out.append(flat[at:at + n].reshape(shp))
        at += n
    return out


def kernel(x, norm1_w, w_in, conv_w, conv_b, dt_bias, a_log, d_skip, ssd_norm_w, pool_w, pool_b, pool_scale, w_out, norm2_w, w_gate, w_up, w_down, final_norm_w, loss_target, m_norm1_w, m_w_in, m_conv_w, m_conv_b, m_dt_bias, m_a_log, m_d_skip, m_ssd_norm_w, m_pool_w, m_pool_b, m_pool_scale, m_w_out, m_norm2_w, m_w_gate, m_w_up, m_w_down, m_final_norm_w, v_norm1_w, v_w_in, v_conv_w, v_conv_b, v_dt_bias, v_a_log, v_d_skip, v_ssd_norm_w, v_pool_w, v_pool_b, v_pool_scale, v_w_out, v_norm2_w, v_w_gate, v_w_up, v_w_down, v_final_norm_w):
    px, py, pc = _position()
    chip = 2 * px + py
    chip_arr = jnp.reshape(chip, (1,)).astype(jnp.int32)
    half_arr = jnp.reshape(pc, (1,)).astype(jnp.int32)

    w_in_t = jnp.pad(jnp.swapaxes(w_in, 1, 2).astype(BF16), ((0, 0), (0, SHARD_IN - COLS_IN), (0, 0)))
    shards = [w_in_t, w_out.astype(BF16), jnp.swapaxes(w_gate, 1, 2).astype(BF16),
              jnp.swapaxes(w_up, 1, 2).astype(BF16), w_down.astype(BF16)]
    w_in_st, w_out_f, w_gate_t, w_up_t, w_down_f = gather_weights(shards)
    w_in_f = _in_weight_layout(w_in_st)

    pad_heads = lambda v: jnp.pad(v, ((0, 0), (0, LANES - SSD_HEADS)))[:, None, :]
    dtb, alog = pad_heads(dt_bias), pad_heads(a_log)
    dskip_x = jnp.repeat(d_skip, HEAD_DIM, axis=1)[:, None, :]
    eye = jnp.eye(len(POOL_WINDOWS), dtype=F32)
    wbd = (pool_w[:, :, :, None, :] * eye[None, :, None, :, None]).reshape(DEPTH, POOL_WIDTH, POOL_WIDTH).astype(BF16)
    pool_b2 = pool_b.reshape(DEPTH, 1, POOL_WIDTH)
    cw_cols = lax.dynamic_update_slice(jnp.zeros((DEPTH, CONV_WIDTH, CONV_DIM), F32), conv_w,
                                       (0, 0, chip * (CONV_DIM // N_CHIPS)))
    cw_cols = jnp.where(pc == 0, cw_cols, 0.0)
    cw_rows = (DEPTH * CONV_WIDTH * CONV_DIM) // D_MODEL
    conv_w_f = small_allreduce(jnp.pad(cw_cols.reshape(cw_rows, D_MODEL), ((0, 8), (0, 0))), "gather_conv_w")
    conv_w_f = conv_w_f[:cw_rows].reshape(DEPTH, CONV_WIDTH, CONV_DIM)
    cw8 = jnp.pad(conv_w_f, ((0, 0), (0, 8 - CONV_WIDTH), (0, 0)))

    h = x[0]
    saved = []
    for l in range(DEPTH):
        proj = rms_matmul(h, norm1_w[l][None], w_in_f[l], "in_proj")
        xc = conv_forward(proj, cw8[l], conv_b[l][None], "conv_fwd")
        y_all, ycore, states = ssd_forward(proj, xc, dtb[l], alog[l], dskip_x[l], ssd_norm_w[l][None], "ssd_fwd")
        y_all, o_sb = sb_forward(proj, y_all, "sb_fwd")
        y_all = pool_forward(proj, wbd[l], pool_b2[l], pool_scale[l][None], y_all, "pool_fwd")
        x1 = matmul_residual(y_all, w_out_f[l], h, "out_proj")
        x2, g, u = ffn_forward(x1, norm2_w[l][None], w_gate_t[l], w_up_t[l], w_down_f[l], "ffn_fwd")
        saved.append((h, proj, xc, ycore, states, o_sb, y_all, x1, g, u))
        h = x2

    loss_part, dx, dxb, d_final = loss_head(h, final_norm_w[None], loss_target[0], "loss_head")
    loss = lax.psum(loss_part[0, 0], ("x", "y", "c"))

    small = {n: [None] * DEPTH for n in SMALL_NAMES if n != "final_norm_w"}
    stacks = [None] * 5
    for l in reversed(range(DEPTH)):
        xin, proj, xc, ycore, states, o_sb, y_all, x1, g, u = saved[l]
        dg, du, act = ffn_backward_act(dxb, g, u, w_down_f[l], "ffn_bwd_act")
        dx1, dx1b, h2, dn2 = rms_backward([dg, du], [w_gate_t[l], w_up_t[l]], x1, norm2_w[l][None], dx, "ffn_bwd_norm", 256)
        stacks[4] = matmul_tn(act, dxb, "dw_down", stacks[4], l)
        stacks[2] = matmul_tn(dg, h2, "dw_gate", stacks[2], l)
        stacks[3] = matmul_tn(du, h2, "dw_up", stacks[3], l)
        dyall = matmul_nt(dx1b, w_out_f[l], "out_proj_bwd")
        stacks[1] = matmul_tn(y_all, dx1b, "dw_out", stacks[1], l)
        dxc, dproj, dsn, ddsk, ddtb, dalog = ssd_backward(proj, xc, ycore, dyall, states, dtb[l], alog[l],
                                                          dskip_x[l], ssd_norm_w[l][None], "ssd_bwd")
        dproj, dcw, dcb = conv_backward(proj, dxc, cw8[l], conv_b[l][None], dproj, "conv_bwd")
        dproj = sb_backward(proj, o_sb, dyall, dproj, "sb_bwd")
        dproj, dwbd, dpb, dps = pool_backward(proj, dyall, wbd[l], pool_b2[l], pool_scale[l][None], dproj, "pool_bwd")
        dx, dxb, h1, dn1 = rms_backward([dproj], [w_in_f[l]], xin, norm1_w[l][None], dx1, "in_proj_bwd", 256)
        stacks[0] = matmul_tn(dproj, h1, "dw_in", stacks[0], l)
        small["norm1_w"][l] = dn1[0]
        small["conv_w"][l] = dcw[:CONV_WIDTH]
        small["conv_b"][l] = dcb[0]
        small["dt_bias"][l] = ddtb[0, :SSD_HEADS]
        small["a_log"][l] = dalog[0, :SSD_HEADS]
        small["d_skip"][l] = ddsk.reshape(SSD_HEADS, HEAD_DIM).sum(axis=1)
        small["ssd_norm_w"][l] = dsn[0]
        small["pool_w"][l] = jnp.stack([dwbd[64 * k:64 * k + 64, 64 * k:64 * k + 64] for k in range(len(POOL_WINDOWS))])
        small["pool_b"][l] = dpb.reshape(len(POOL_WINDOWS), -1)
        small["pool_scale"][l] = dps[0]
        small["norm2_w"][l] = dn2[0]
    grad_x = dx[None]

    stacks[0] = _in_gradient_layout(stacks[0])
    g_in_t, g_out, g_gate_t, g_up_t, g_down = reduce_gradients(stacks, half_arr, chip_arr)
    g_big = dict(w_in=g_in_t[:, :COLS_IN], w_out=g_out, w_gate=g_gate_t, w_up=g_up_t, w_down=g_down)
    transposed = ("w_in", "w_gate", "w_up")

    small_parts = [d_final if n == "final_norm_w" else jnp.stack(small[n]) for n in SMALL_NAMES]
    small_shapes = [p.shape for p in small_parts]
    g_small = dict(zip(SMALL_NAMES, _unpack_small(small_allreduce(_pack_small(small_parts), "reduce_small"), small_shapes)))
    g_small["final_norm_w"] = g_small["final_norm_w"].reshape(final_norm_w.shape)
    g_small["conv_w"] = lax.dynamic_slice_in_dim(g_small["conv_w"], chip * (CONV_DIM // N_CHIPS), CONV_DIM // N_CHIPS, axis=2)

    given = dict(norm1_w=(norm1_w, m_norm1_w, v_norm1_w), w_in=(w_in, m_w_in, v_w_in), conv_w=(conv_w, m_conv_w, v_conv_w),
                 conv_b=(conv_b, m_conv_b, v_conv_b), dt_bias=(dt_bias, m_dt_bias, v_dt_bias), a_log=(a_log, m_a_log, v_a_log),
                 d_skip=(d_skip, m_d_skip, v_d_skip), ssd_norm_w=(ssd_norm_w, m_ssd_norm_w, v_ssd_norm_w),
                 pool_w=(pool_w, m_pool_w, v_pool_w), pool_b=(pool_b, m_pool_b, v_pool_b),
                 pool_scale=(pool_scale, m_pool_scale, v_pool_scale), w_out=(w_out, m_w_out, v_w_out),
                 norm2_w=(norm2_w, m_norm2_w, v_norm2_w), w_gate=(w_gate, m_w_gate, v_w_gate), w_up=(w_up, m_w_up, v_w_up),
                 w_down=(w_down, m_w_down, v_w_down), final_norm_w=(final_norm_w, m_final_norm_w, v_final_norm_w))
    order = ("norm1_w", "w_in", "conv_w", "conv_b", "dt_bias", "a_log", "d_skip", "ssd_norm_w", "pool_w", "pool_b",
             "pool_scale", "w_out", "norm2_w", "w_gate", "w_up", "w_down", "final_norm_w")
    grads = dict(g_small)
    results = {}
    for n in ("w_in", "w_out", "w_gate", "w_up", "w_down"):
        w, m, v = given[n]
        if n in transposed:
            out = adamw(jnp.swapaxes(w, 1, 2), g_big[n], jnp.swapaxes(m, 1, 2), jnp.swapaxes(v, 1, 2), "adamw_" + n)
            results[n] = tuple(jnp.swapaxes(o, 1, 2) for o in out)
            grads[n] = jnp.swapaxes(g_big[n], 1, 2)
        else:
            results[n] = adamw(w, g_big[n], m, v, "adamw_" + n)
            grads[n] = g_big[n]
    small_shapes = [given[n][0].shape for n in SMALL_NAMES]
    packed = [_pack_small([given[n][k] for n in SMALL_NAMES])[None] for k in range(3)]
    packed_g = _pack_small([grads[n] for n in SMALL_NAMES])[None]
    small_out = adamw(packed[0], packed_g, packed[1], packed[2], "adamw_small")
    small_out = [_unpack_small(o[0], small_shapes) for o in small_out]
    for i, n in enumerate(SMALL_NAMES):
        results[n] = tuple(small_out[k][i] for k in range(3))

    return (loss, grad_x, *[grads[n] for n in order], *[results[n][0] for n in order],
            *[results[n][1] for n in order], *[results[n][2] for n in order])
```

```python
import numpy as np
import jax
import jax.numpy as jnp
from jax import lax
from jax.experimental import pallas as pl
from jax.experimental.pallas import tpu as pltpu

F32 = jnp.float32
BF16 = jnp.bfloat16
MESH = pl.DeviceIdType.MESH
ANY = pl.BlockSpec(memory_space=pl.ANY)

D_MODEL = 1024
DEPTH = 4
EPS = 1e-6
SSD_WIDTH = 512
SSD_HEADS = 8
HEAD_DIM = 64
D_STATE = 128
CHUNK = 128
CONV_WIDTH = 4
CONV_DIM = 1024
SB_WIDTH = 256
POOL_WIDTH = 256
POOL_WINDOWS = (2, 4, 8, 16)
D_FF = 2816
D_IN = 2568
N_CHIPS = 4
OFF_QKV, OFF_Z, OFF_DT, OFF_XBC, OFF_P = 0, 768, 1280, 1536, 2560
D_INP = 2816
ZDT = 768
SHARD_IN, SHARD_OUT, SHARD_FF = 672, 256, 704
COLS_IN = 642
ADAM_LR, ADAM_B1, ADAM_B2, ADAM_EPS, ADAM_WD, ADAM_STEP = 0.001, 0.9, 0.999, 1e-08, 0.01, 10
LANES = 128
VMEM_LIMIT = 56 * 1024 * 1024


def _params(sem=None):
    return pltpu.CompilerParams(dimension_semantics=sem, vmem_limit_bytes=VMEM_LIMIT)


def _tile(n, cap):
    best = None
    for t in range(LANES, min(n, cap) + 1, LANES):
        if n % t == 0:
            best = t
    assert best is not None, (n, cap)
    return best


def _nt(a, b):
    return lax.dot_general(a, b, (((1,), (1,)), ((), ())), preferred_element_type=F32)


def _tn(a, b):
    return lax.dot_general(a, b, (((0,), (0,)), ((), ())), preferred_element_type=F32)


def _nn(a, b):
    return jnp.dot(a, b, preferred_element_type=F32)


def _split_dot(a, b_exact, terms=3, dot=_nn):
    acc = None
    rest = a
    for _ in range(terms):
        hi = rest.astype(BF16)
        part = dot(hi, b_exact)
        acc = part if acc is None else acc + part
        rest = rest - hi.astype(F32)
    return acc


def _split_dot_left(a_exact, b, terms=3):
    acc = None
    rest = b
    for _ in range(terms):
        hi = rest.astype(BF16)
        part = _nn(a_exact, hi)
        acc = part if acc is None else acc + part
        rest = rest - hi.astype(F32)
    return acc


def _sigmoid(x):
    return 1.0 / (1.0 + jnp.exp(-x))


def _softplus(x):
    return jnp.maximum(x, 0.0) + jnp.log(1.0 + jnp.exp(-jnp.abs(x)))


def _iota2(shape, dim):
    return lax.broadcasted_iota(jnp.int32, shape, dim)


def rms_matmul(x, nw, wt, name):
    s, d = x.shape
    n = wt.shape[0]
    tm, tn = _tile(s, 512), _tile(n, 1408)

    def body(x_ref, nw_ref, w_ref, o_ref, h_ref):
        @pl.when(pl.program_id(1) == 0)
        def _():
            xv = x_ref[...]
            r = lax.rsqrt(jnp.mean(xv * xv, axis=-1, keepdims=True) + EPS)
            h_ref[...] = (xv * r * nw_ref[...]).astype(BF16)
        o_ref[...] = _nt(h_ref[...], w_ref[...])

    return pl.pallas_call(
        body, name=name, grid=(s // tm, n // tn),
        in_specs=[pl.BlockSpec((tm, d), lambda i, j: (i, 0)), pl.BlockSpec((1, d), lambda i, j: (0, 0)),
                  pl.BlockSpec((tn, d), lambda i, j: (j, 0))],
        out_specs=pl.BlockSpec((tm, tn), lambda i, j: (i, j)),
        out_shape=jax.ShapeDtypeStruct((s, n), F32),
        scratch_shapes=[pltpu.VMEM((tm, d), BF16)],
        compiler_params=_params(("parallel", "arbitrary")),
    )(x, nw, wt)


def matmul_residual(a, w, res, name):
    s, k = a.shape
    n = w.shape[1]
    tm, tn = _tile(s, 512), _tile(n, 512)

    def body(a_ref, w_ref, r_ref, o_ref):
        o_ref[...] = r_ref[...] + _nn(a_ref[...], w_ref[...])

    return pl.pallas_call(
        body, name=name, grid=(s // tm, n // tn),
        in_specs=[pl.BlockSpec((tm, k), lambda i, j: (i, 0)), pl.BlockSpec((k, tn), lambda i, j: (0, j)),
                  pl.BlockSpec((tm, tn), lambda i, j: (i, j))],
        out_specs=pl.BlockSpec((tm, tn), lambda i, j: (i, j)),
        out_shape=jax.ShapeDtypeStruct((s, n), F32),
        compiler_params=_params(("parallel", "parallel")),
    )(a, w, res)


def matmul_nt(a, w, name, out_dtype=F32):
    s, n = a.shape
    k = w.shape[0]
    tm, tk = _tile(s, 512), _tile(k, 512)

    def body(a_ref, w_ref, o_ref):
        o_ref[...] = _nt(a_ref[...], w_ref[...]).astype(out_dtype)

    return pl.pallas_call(
        body, name=name, grid=(s // tm, k // tk),
        in_specs=[pl.BlockSpec((tm, n), lambda i, j: (i, 0)), pl.BlockSpec((tk, n), lambda i, j: (j, 0))],
        out_specs=pl.BlockSpec((tm, tk), lambda i, j: (i, j)),
        out_shape=jax.ShapeDtypeStruct((s, k), out_dtype),
        compiler_params=_params(("parallel", "parallel")),
    )(a, w)


def _after(after):
    return ([], []) if after is None else ([ANY], [after])


def matmul_tn(a, b, name, after=None):
    s, m = a.shape
    n = b.shape[1]
    tm, tn = _tile(m, 512), _tile(n, 512)

    def body(a_ref, b_ref, *rest):
        rest[-1][...] = _tn(a_ref[...], b_ref[...]).astype(BF16)

    specs, ops = _after(after)
    return pl.pallas_call(
        body, name=name, grid=(m // tm, n // tn),
        in_specs=[pl.BlockSpec((s, tm), lambda i, j: (0, i)), pl.BlockSpec((s, tn), lambda i, j: (0, j))] + specs,
        out_specs=pl.BlockSpec((tm, tn), lambda i, j: (i, j)),
        out_shape=jax.ShapeDtypeStruct((m, n), BF16),
        compiler_params=_params(("parallel", "parallel")),
    )(a, b, *ops)


def ffn_forward(x1, nw, wgt, wut, wd, name):
    s, d = x1.shape
    f = wgt.shape[0]
    tm, tf = _tile(s, 1024), _tile(f, 256)

    def body(x_ref, nw_ref, wg_ref, wu_ref, wd_ref, o_ref, g_ref, u_ref, h_ref, acc_ref):
        j = pl.program_id(1)

        @pl.when(j == 0)
        def _():
            xv = x_ref[...]
            r = lax.rsqrt(jnp.mean(xv * xv, axis=-1, keepdims=True) + EPS)
            h_ref[...] = (xv * r * nw_ref[...]).astype(BF16)
            acc_ref[...] = xv

        h = h_ref[...]
        g = _nt(h, wg_ref[...])
        u = _nt(h, wu_ref[...])
        g_ref[...] = g.astype(BF16)
        u_ref[...] = u.astype(BF16)
        a = (g * _sigmoid(g) * u).astype(BF16)
        acc_ref[...] += _nn(a, wd_ref[...])

        @pl.when(j == pl.num_programs(1) - 1)
        def _():
            o_ref[...] = acc_ref[...]

    wblk = pl.BlockSpec((tf, d), lambda i, j: (j, 0))
    return pl.pallas_call(
        body, name=name, grid=(s // tm, f // tf),
        in_specs=[pl.BlockSpec((tm, d), lambda i, j: (i, 0)), pl.BlockSpec((1, d), lambda i, j: (0, 0)), wblk, wblk, wblk],
        out_specs=[pl.BlockSpec((tm, d), lambda i, j: (i, 0)), pl.BlockSpec((tm, tf), lambda i, j: (i, j)),
                   pl.BlockSpec((tm, tf), lambda i, j: (i, j))],
        out_shape=[jax.ShapeDtypeStruct((s, d), F32), jax.ShapeDtypeStruct((s, f), BF16),
                   jax.ShapeDtypeStruct((s, f), BF16)],
        scratch_shapes=[pltpu.VMEM((tm, d), BF16), pltpu.VMEM((tm, d), F32)],
        compiler_params=_params(("parallel", "arbitrary")),
    )(x1, nw, wgt, wut, wd)


def ffn_backward_act(dx2, g, u, wd, name, after=None):
    s, d = dx2.shape
    f = wd.shape[0]
    tm, tf = _tile(s, 512), _tile(f, 1408)
    specs, ops = _after(after)

    def body(dx_ref, g_ref, u_ref, wd_ref, *rest):
        dg_ref, du_ref, a_ref = rest[len(ops):]
        da = _nt(dx_ref[...], wd_ref[...])
        gv = g_ref[...].astype(F32)
        uv = u_ref[...].astype(F32)
        sg = _sigmoid(gv)
        silu = gv * sg
        dg_ref[...] = (da * uv * (sg * (1.0 + gv * (1.0 - sg)))).astype(BF16)
        du_ref[...] = (da * silu).astype(BF16)
        a_ref[...] = (silu * uv).astype(BF16)

    blk = pl.BlockSpec((tm, tf), lambda i, j: (i, j))
    return pl.pallas_call(
        body, name=name, grid=(s // tm, f // tf),
        in_specs=[pl.BlockSpec((tm, d), lambda i, j: (i, 0)), blk, blk, pl.BlockSpec((tf, d), lambda i, j: (j, 0))] + specs,
        out_specs=[blk, blk, blk],
        out_shape=[jax.ShapeDtypeStruct((s, f), BF16)] * 3,
        compiler_params=_params(("parallel", "parallel")),
    )(dx2, g, u, wd, *ops)


def rms_backward(dzs, wts, x, nw, dres, name, tm):
    s, d = x.shape
    nz = len(dzs)

    def body(*refs):
        dz_refs, w_refs = refs[:nz], refs[nz:2 * nz]
        x_ref, nw_ref, dres_ref, dx_ref, dxb_ref, h_ref, dnw_ref = refs[2 * nz:]
        dh = _nn(dz_refs[0][...], w_refs[0][...])
        for k in range(1, nz):
            dh = dh + _nn(dz_refs[k][...], w_refs[k][...])
        xv = x_ref[...]
        r = lax.rsqrt(jnp.mean(xv * xv, axis=-1, keepdims=True) + EPS)
        xhat = xv * r
        nwv = nw_ref[...]
        h_ref[...] = (xhat * nwv).astype(BF16)

        @pl.when(pl.program_id(0) == 0)
        def _():
            dnw_ref[...] = jnp.zeros_like(dnw_ref)

        dnw_ref[...] += jnp.sum(dh * xhat, axis=0, keepdims=True)
        gdh = dh * nwv
        dx = dres_ref[...] + r * (gdh - xhat * jnp.mean(gdh * xhat, axis=-1, keepdims=True))
        dx_ref[...] = dx
        dxb_ref[...] = dx.astype(BF16)

    row = pl.BlockSpec((tm, d), lambda i: (i, 0))
    in_specs = [pl.BlockSpec((tm, dz.shape[1]), lambda i: (i, 0)) for dz in dzs]
    in_specs += [pl.BlockSpec(w.shape, lambda i: (0, 0)) for w in wts]
    in_specs += [row, pl.BlockSpec((1, d), lambda i: (0, 0)), row]
    return pl.pallas_call(
        body, name=name, grid=(s // tm,),
        in_specs=in_specs,
        out_specs=[row, row, row, pl.BlockSpec((1, d), lambda i: (0, 0))],
        out_shape=[jax.ShapeDtypeStruct((s, d), F32), jax.ShapeDtypeStruct((s, d), BF16),
                   jax.ShapeDtypeStruct((s, d), BF16), jax.ShapeDtypeStruct((1, d), F32)],
        compiler_params=_params(("arbitrary",)),
    )(*dzs, *wts, x, nw, dres)


def loss_head(x, nw, target, name):
    s, d = x.shape
    tm = _tile(s, 512)

    def body(x_ref, nw_ref, t_ref, loss_ref, dx_ref, dxb_ref, dnw_ref):
        xv = x_ref[...]
        r = lax.rsqrt(jnp.mean(xv * xv, axis=-1, keepdims=True) + EPS)
        xhat = xv * r
        nwv = nw_ref[...]
        err = xhat * nwv - t_ref[...]

        @pl.when(pl.program_id(0) == 0)
        def _():
            dnw_ref[...] = jnp.zeros_like(dnw_ref)
            loss_ref[...] = jnp.zeros_like(loss_ref)

        part = jnp.sum(jnp.sum(err * err, axis=-1, keepdims=True), axis=0, keepdims=True) * (0.5 / d)
        loss_ref[...] += jnp.broadcast_to(part, loss_ref.shape)
        dout = err * (1.0 / d)
        dnw_ref[...] += jnp.sum(dout * xhat, axis=0, keepdims=True)
        gdh = dout * nwv
        dx = r * (gdh - xhat * jnp.mean(gdh * xhat, axis=-1, keepdims=True))
        dx_ref[...] = dx
        dxb_ref[...] = dx.astype(BF16)

    row = pl.BlockSpec((tm, d), lambda i: (i, 0))
    return pl.pallas_call(
        body, name=name, grid=(s // tm,),
        in_specs=[row, pl.BlockSpec((1, d), lambda i: (0, 0)), row],
        out_specs=[pl.BlockSpec((1, LANES), lambda i: (0, 0)), row, row, pl.BlockSpec((1, d), lambda i: (0, 0))],
        out_shape=[jax.ShapeDtypeStruct((1, LANES), F32), jax.ShapeDtypeStruct((s, d), F32),
                   jax.ShapeDtypeStruct((s, d), BF16), jax.ShapeDtypeStruct((1, d), F32)],
        compiler_params=_params(("arbitrary",)),
    )(x, nw, target)


def _shift_down(x, k):
    return jnp.where(_iota2(x.shape, 0) >= k, pltpu.roll(x, k, axis=0), 0.0)


def _shift_up(x, k):
    s = x.shape[0]
    return jnp.where(_iota2(x.shape, 0) < s - k, pltpu.roll(x, s - k, axis=0), 0.0)


CONV_TILE = 256


def conv_forward(proj, cw, cb, name):
    s = proj.shape[0]
    tn = CONV_TILE
    off = OFF_XBC // tn

    def body(u_ref, w_ref, b_ref, o_ref):
        u = u_ref[...]
        pre = b_ref[...] + w_ref[CONV_WIDTH - 1:CONV_WIDTH, :] * u
        for i in range(CONV_WIDTH - 1):
            pre = pre + w_ref[i:i + 1, :] * _shift_down(u, CONV_WIDTH - 1 - i)
        o_ref[...] = pre * _sigmoid(pre)

    return pl.pallas_call(
        body, name=name, grid=(CONV_DIM // tn,),
        in_specs=[pl.BlockSpec((s, tn), lambda j: (0, off + j)), pl.BlockSpec((8, tn), lambda j: (0, j)),
                  pl.BlockSpec((1, tn), lambda j: (0, j))],
        out_specs=pl.BlockSpec((s, tn), lambda j: (0, j)),
        out_shape=jax.ShapeDtypeStruct((s, CONV_DIM), F32),
        compiler_params=_params(("parallel",)),
    )(proj, cw, cb)


def conv_backward(proj, dxc, cw, cb, dproj, name):
    s = proj.shape[0]
    tn = CONV_TILE
    off = OFF_XBC // tn

    def body(u_ref, d_ref, w_ref, b_ref, _, du_ref, dw_ref, db_ref):
        u = u_ref[...]
        shifted = [_shift_down(u, CONV_WIDTH - 1 - i) for i in range(CONV_WIDTH - 1)] + [u]
        pre = b_ref[...] + w_ref[CONV_WIDTH - 1:CONV_WIDTH, :] * u
        for i in range(CONV_WIDTH - 1):
            pre = pre + w_ref[i:i + 1, :] * shifted[i]
        sg = _sigmoid(pre)
        dpre = d_ref[...] * (sg * (1.0 + pre * (1.0 - sg)))
        du = w_ref[CONV_WIDTH - 1:CONV_WIDTH, :] * dpre
        for i in range(CONV_WIDTH - 1):
            du = du + w_ref[i:i + 1, :] * _shift_up(dpre, CONV_WIDTH - 1 - i)
        du_ref[...] = du.astype(BF16)
        rows = [jnp.sum(dpre * shifted[i], axis=0, keepdims=True) for i in range(CONV_WIDTH)]
        rows.append(jnp.zeros((8 - CONV_WIDTH, tn), F32))
        dw_ref[...] = jnp.concatenate(rows, axis=0)
        db_ref[...] = jnp.sum(dpre, axis=0, keepdims=True)

    return pl.pallas_call(
        body, name=name, grid=(CONV_DIM // tn,),
        in_specs=[pl.BlockSpec((s, tn), lambda j: (0, off + j)), pl.BlockSpec((s, tn), lambda j: (0, j)),
                  pl.BlockSpec((8, tn), lambda j: (0, j)), pl.BlockSpec((1, tn), lambda j: (0, j)), ANY],
        out_specs=[pl.BlockSpec((s, tn), lambda j: (0, off + j)), pl.BlockSpec((8, tn), lambda j: (0, j)),
                   pl.BlockSpec((1, tn), lambda j: (0, j))],
        out_shape=[jax.ShapeDtypeStruct(dproj.shape, BF16), jax.ShapeDtypeStruct((8, CONV_DIM), F32),
                   jax.ShapeDtypeStruct((1, CONV_DIM), F32)],
        input_output_aliases={4: 0},
        compiler_params=_params(("parallel",)),
    )(proj, dxc, cw, cb, dproj)


def _pool_lane_window(shape):
    grp = _iota2(shape, 1) // (POOL_WIDTH // len(POOL_WINDOWS))
    win = jnp.full(shape, POOL_WINDOWS[-1], jnp.int32)
    for gi in range(len(POOL_WINDOWS) - 2, -1, -1):
        win = jnp.where(grp == gi, POOL_WINDOWS[gi], win)
    return grp, win


def _pool_select(grp, sums):
    out = sums[-1]
    for gi in range(len(sums) - 2, -1, -1):
        out = jnp.where(grp == gi, sums[gi], out)
    return out


def _pool_pooled(p):
    grp, win = _pool_lane_window(p.shape)
    inv_count = 1.0 / jnp.minimum(_iota2(p.shape, 0) + 1, win).astype(F32)
    sums, acc, k = [], p, 1
    for _ in POOL_WINDOWS:
        acc = acc + _shift_down(acc, k)
        sums.append(acc)
        k *= 2
    return _pool_select(grp, sums) * inv_count - p, grp, inv_count


def pool_forward(proj, wbd, pb, ps, y_all, name, after=None):
    s = proj.shape[0]
    specs, ops = _after(after)

    def body(p_ref, w_ref, b_ref, s_ref, *rest):
        o_ref = rest[-1]
        pooled, _, _ = _pool_pooled(p_ref[...])
        mixed = _nn(pooled.astype(BF16), w_ref[...]) + b_ref[...]
        o_ref[...] = (mixed * s_ref[...]).astype(BF16)

    vec = pl.BlockSpec((1, POOL_WIDTH), lambda j: (0, 0))
    return pl.pallas_call(
        body, name=name, grid=(1,),
        in_specs=[pl.BlockSpec((s, POOL_WIDTH), lambda j: (0, OFF_P // POOL_WIDTH)),
                  pl.BlockSpec((POOL_WIDTH, POOL_WIDTH), lambda j: (0, 0)), vec, vec, ANY] + specs,
        out_specs=pl.BlockSpec((s, POOL_WIDTH), lambda j: (0, (SSD_WIDTH + SB_WIDTH) // POOL_WIDTH)),
        out_shape=jax.ShapeDtypeStruct(y_all.shape, BF16),
        input_output_aliases={4: 0},
        compiler_params=_params(("arbitrary",)),
    )(proj, wbd, pb, ps, y_all, *ops)


def pool_backward(proj, dyall, wbd, pb, ps, dproj, name):
    s = proj.shape[0]

    def body(p_ref, dy_ref, w_ref, b_ref, s_ref, _, dp_ref, dw_ref, db_ref, ds_ref):
        pooled, grp, inv_count = _pool_pooled(p_ref[...])
        pooled_b = pooled.astype(BF16)
        mixed = _nn(pooled_b, w_ref[...]) + b_ref[...]
        dy = dy_ref[...]
        ds_ref[...] = jnp.sum(dy * mixed, axis=0, keepdims=True)
        dmixed = dy * s_ref[...]
        db_ref[...] = jnp.sum(dmixed, axis=0, keepdims=True)
        dmixed_b = dmixed.astype(BF16)
        dw_ref[...] = _tn(pooled_b, dmixed_b)
        dpooled = _nt(dmixed_b, w_ref[...])
        sums, acc, k = [], dpooled * inv_count, 1
        for _ in POOL_WINDOWS:
            acc = acc + _shift_up(acc, k)
            sums.append(acc)
            k *= 2
        dp_ref[...] = (_pool_select(grp, sums) - dpooled).astype(BF16)

    vec = pl.BlockSpec((1, POOL_WIDTH), lambda j: (0, 0))
    mat = pl.BlockSpec((POOL_WIDTH, POOL_WIDTH), lambda j: (0, 0))
    pcol = pl.BlockSpec((s, POOL_WIDTH), lambda j: (0, OFF_P // POOL_WIDTH))
    return pl.pallas_call(
        body, name=name, grid=(1,),
        in_specs=[pcol, pl.BlockSpec((s, POOL_WIDTH), lambda j: (0, (SSD_WIDTH + SB_WIDTH) // POOL_WIDTH)), mat, vec, vec, ANY],
        out_specs=[pcol, mat, vec, vec],
        out_shape=[jax.ShapeDtypeStruct(dproj.shape, BF16), jax.ShapeDtypeStruct((POOL_WIDTH, POOL_WIDTH), F32),
                   jax.ShapeDtypeStruct((1, POOL_WIDTH), F32), jax.ShapeDtypeStruct((1, POOL_WIDTH), F32)],
        input_output_aliases={5: 0},
        compiler_params=_params(("arbitrary",)),
    )(proj, dyall, wbd, pb, ps, dproj)


N_PAIRS = SSD_HEADS // 2


def _ssd_common(xc, dtraw, dtb, alog):
    c = CHUNK
    dt = _softplus(dtraw + dtb)
    a = -jnp.exp(alog)
    ltri = (_iota2((c, c), 0) >= _iota2((c, c), 1)).astype(BF16)
    acum = _split_dot_left(ltri, dt * a)
    expand = (_iota2((c, SSD_WIDTH), 1) // HEAD_DIM == _iota2((c, SSD_WIDTH), 0)).astype(BF16)
    expand_wide = (_iota2((c, SSD_HEADS * c), 1) // c == _iota2((c, SSD_HEADS * c), 0)).astype(BF16)
    acum_x = _split_dot(acum, expand)
    dt_x = _split_dot(dt, expand)
    alast_x = acum_x[c - 1:c, :]
    return dict(dt=dt, a=a, acum=acum, acum_x=acum_x, dt_x=dt_x, ea_x=jnp.exp(acum_x),
                dte_x=jnp.exp(alast_x - acum_x), eal_x=jnp.exp(alast_x),
                acol=_split_dot(acum, expand_wide), acum_t=acum.T,
                xs=xc[:, :SSD_WIDTH], causal=_iota2((c, c), 0) >= _iota2((c, c), 1),
                left=_iota2((c, c), 1) < HEAD_DIM)


def _ssd_group(xc, g):
    b = xc[:, SSD_WIDTH + D_STATE * g:SSD_WIDTH + D_STATE * (g + 1)]
    cm = xc[:, SSD_WIDTH + 2 * D_STATE + D_STATE * g:SSD_WIDTH + 2 * D_STATE + D_STATE * (g + 1)]
    return b, cm


def _ssd_decay(q, hh):
    col = q["acol"][:, CHUNK * hh:CHUNK * (hh + 1)]
    row = q["acum_t"][hh:hh + 1, :]
    return jnp.where(q["causal"], jnp.exp(jnp.minimum(col - row, 0.0)), 0.0)


def ssd_forward(proj, xc, dtb, alog, dskip_x, nw, name):
    s = xc.shape[0]
    nc = s // CHUNK

    def body(xc_ref, zdt_ref, dtb_ref, alog_ref, dsk_ref, nw_ref, y_ref, yc_ref, st_ref, state):
        @pl.when(pl.program_id(0) == 0)
        def _():
            state[...] = jnp.zeros_like(state)

        xcv = xc_ref[...]
        q = _ssd_common(xcv, zdt_ref[:, SSD_WIDTH:SSD_WIDTH + LANES], dtb_ref[...], alog_ref[...])
        x = q["xs"] * q["dt_x"]
        xb = x.astype(BF16)
        xd = (x * q["dte_x"]).astype(BF16)
        pieces = []
        for g in range(2):
            bg, cg = _ssd_group(xcv, g)
            bgb, cgb = bg.astype(BF16), cg.astype(BF16)
            cb = _nt(cgb, bgb)
            bgt = bg.T.astype(BF16)
            for pr in (2 * g, 2 * g + 1):
                sl = slice(CHUNK * pr, CHUNK * (pr + 1))
                st = state[pr]
                st_ref[0, pr] = st
                yp = _nn(cgb, st.astype(BF16)) * q["ea_x"][:, sl]
                for k, hh in enumerate((2 * pr, 2 * pr + 1)):
                    w = (cb * _ssd_decay(q, hh)).astype(BF16)
                    mask = q["left"] if k == 0 else jnp.logical_not(q["left"])
                    yp = yp + _nn(w, jnp.where(mask, xb[:, sl], jnp.zeros_like(xb[:, sl])))
                state[pr] = st * q["eal_x"][:, sl] + _nn(bgt, xd[:, sl])
                pieces.append(yp)
        y = jnp.concatenate(pieces, axis=1) + q["xs"] * dsk_ref[...]
        yc_ref[...] = y
        zv = zdt_ref[:, :SSD_WIDTH]
        yg = y * (zv * _sigmoid(zv))
        r = lax.rsqrt(jnp.mean(yg * yg, axis=-1, keepdims=True) + EPS)
        y_ref[...] = (yg * r * nw_ref[...]).astype(BF16)

    vec = lambda n: pl.BlockSpec((1, n), lambda c: (0, 0))
    return pl.pallas_call(
        body, name=name, grid=(nc,),
        in_specs=[pl.BlockSpec((CHUNK, CONV_DIM), lambda c: (c, 0)),
                  pl.BlockSpec((CHUNK, ZDT), lambda c: (c, OFF_Z // ZDT)),
                  vec(LANES), vec(LANES), vec(SSD_WIDTH), vec(SSD_WIDTH)],
        out_specs=[pl.BlockSpec((CHUNK, SSD_WIDTH), lambda c: (c, 0)), pl.BlockSpec((CHUNK, SSD_WIDTH), lambda c: (c, 0)),
                   pl.BlockSpec((1, N_PAIRS, D_STATE, CHUNK), lambda c: (c, 0, 0, 0))],
        out_shape=[jax.ShapeDtypeStruct((s, D_MODEL), BF16), jax.ShapeDtypeStruct((s, SSD_WIDTH), F32),
                   jax.ShapeDtypeStruct((nc, N_PAIRS, D_STATE, CHUNK), F32)],
        scratch_shapes=[pltpu.VMEM((N_PAIRS, D_STATE, CHUNK), F32)],
        compiler_params=_params(("arbitrary",)),
    )(xc, proj, dtb, alog, dskip_x, nw)


def ssd_backward(proj, xc, ycore, dyall, states, dtb, alog, dskip_x, nw, name):
    s = xc.shape[0]
    nc = s // CHUNK
    c = CHUNK

    def body(xc_ref, zdt_ref, yc_ref, dy_ref, st_ref, dtb_ref, alog_ref, dsk_ref, nw_ref,
             dxc_ref, dzdt_ref, dnw_ref, ddsk_ref, ddtb_ref, dalog_ref, dstate):
        @pl.when(pl.program_id(0) == 0)
        def _():
            dstate[...] = jnp.zeros_like(dstate)
            dnw_ref[...] = jnp.zeros_like(dnw_ref)
            ddsk_ref[...] = jnp.zeros_like(ddsk_ref)
            ddtb_ref[...] = jnp.zeros_like(ddtb_ref)
            dalog_ref[...] = jnp.zeros_like(dalog_ref)

        xcv = xc_ref[...]
        dtraw = zdt_ref[:, SSD_WIDTH:SSD_WIDTH + LANES]
        q = _ssd_common(xcv, dtraw, dtb_ref[...], alog_ref[...])
        xs = q["xs"]
        x = xs * q["dt_x"]
        zv, yc, dy, nwv = zdt_ref[:, :SSD_WIDTH], yc_ref[...], dy_ref[...], nw_ref[...]
        sgz = _sigmoid(zv)
        siluz = zv * sgz
        yg = yc * siluz
        r = lax.rsqrt(jnp.mean(yg * yg, axis=-1, keepdims=True) + EPS)
        dnw_ref[...] += jnp.sum(dy * yg * r, axis=0, keepdims=True)
        g1 = dy * nwv
        dyg = r * (g1 - yg * (r * r) * jnp.mean(g1 * yg, axis=-1, keepdims=True))
        dyv = dyg * siluz
        dz = (dyg * yc * (sgz * (1.0 + zv * (1.0 - sgz)))).astype(BF16)
        ddsk_ref[...] += jnp.sum(dyv * xs, axis=0, keepdims=True)
        dye = dyv * q["ea_x"]
        dx_parts, yoff_parts, u_parts, v_parts, e_parts = [], [], [], [], []
        db_parts, dc_parts = [], []
        for g in range(2):
            bg, cg = _ssd_group(xcv, g)
            bgb, cgb = bg.astype(BF16), cg.astype(BF16)
            cb = _nt(cgb, bgb)
            cgt = cg.T.astype(BF16)
            dgsum = jnp.zeros((c, c), F32)
            dbg = jnp.zeros((c, D_STATE), F32)
            dcg = jnp.zeros((c, D_STATE), F32)
            for pr in (2 * g, 2 * g + 1):
                sl = slice(c * pr, c * (pr + 1))
                st = st_ref[0, pr]
                dst = dstate[pr]
                stb, dstb = st.astype(BF16), dst.astype(BF16)
                xp = x[:, sl]
                xpb = xp.astype(BF16)
                dyp = dyv[:, sl]
                xdp = xp * q["dte_x"][:, sl]
                yoff_parts.append(_nn(cgb, stb) * q["ea_x"][:, sl])
                rr = _nn(bgb, dstb)
                dxp = rr * q["dte_x"][:, sl]
                u_parts.append(rr * xdp)
                v_parts.append(dst * st * q["eal_x"][:, sl])
                for k, hh in enumerate((2 * pr, 2 * pr + 1)):
                    decay = _ssd_decay(q, hh)
                    w = cb * decay
                    mask = q["left"] if k == 0 else jnp.logical_not(q["left"])
                    dym = jnp.where(mask, dyp, 0.0).astype(BF16)
                    dw = _nt(dym, xpb)
                    dgsum = dgsum + dw * decay
                    e_parts.append(dw * w)
                    dxp = dxp + _nn(w.T.astype(BF16), dym)
                dyeb = dye[:, sl].astype(BF16)
                dcg = dcg + _nt(dyeb, stb)
                dbg = dbg + _nt(xdp.astype(BF16), dstb)
                dstate[pr] = dst * q["eal_x"][:, sl] + _nn(cgt, dyeb)
                dx_parts.append(dxp)
            dcg = dcg + _nn(dgsum.astype(BF16), bgb)
            dbg = dbg + _nn(dgsum.T.astype(BF16), cgb)
            db_parts.append(dbg)
            dc_parts.append(dcg)
        dx = jnp.concatenate(dx_parts, axis=1)
        yoff = jnp.concatenate(yoff_parts, axis=1)
        u = jnp.concatenate(u_parts, axis=1)
        v = jnp.concatenate(v_parts, axis=1)
        reduce_heads = (_iota2((SSD_WIDTH, c), 0) // HEAD_DIM == _iota2((SSD_WIDTH, c), 1)).astype(BF16)
        to_head = (_iota2((SSD_HEADS * c, c), 0) // c == _iota2((SSD_HEADS * c, c), 1)).astype(BF16)
        da = _split_dot(dyv * yoff - u, reduce_heads, 2)
        da = da + _split_dot(jnp.concatenate(e_parts, axis=1), to_head, 2)
        da = da - _split_dot(jnp.concatenate(e_parts, axis=0), to_head, 2, dot=_tn)
        dalast = jnp.sum(_split_dot(u + v, reduce_heads, 2), axis=0, keepdims=True)
        da = da + jnp.where(_iota2((c, c), 0) == c - 1, dalast, 0.0)
        utri = (_iota2((c, c), 1) >= _iota2((c, c), 0)).astype(BF16)
        dda = _split_dot_left(utri, da)
        ddt = dda * q["a"] + _split_dot(dx * xs, reduce_heads, 2)
        dalog_ref[...] += jnp.sum(dda * q["dt"], axis=0, keepdims=True) * q["a"]
        ddtraw = jnp.where(_iota2((c, c), 1) < SSD_HEADS, ddt * _sigmoid(dtraw + dtb_ref[...]), 0.0)
        ddtb_ref[...] += jnp.sum(ddtraw, axis=0, keepdims=True)
        dzdt_ref[...] = jnp.concatenate([dz, ddtraw.astype(BF16), jnp.zeros((c, ZDT - SSD_WIDTH - LANES), BF16)], axis=1)
        dxs = dx * q["dt_x"] + dyv * dsk_ref[...]
        dxc_ref[...] = jnp.concatenate([dxs] + db_parts + dc_parts, axis=1)

    rev = lambda i: nc - 1 - i
    vec = lambda n: pl.BlockSpec((1, n), lambda i: (0, 0))
    wide = pl.BlockSpec((c, SSD_WIDTH), lambda i: (rev(i), 0))
    zdt = pl.BlockSpec((c, ZDT), lambda i: (rev(i), OFF_Z // ZDT))
    return pl.pallas_call(
        body, name=name, grid=(nc,),
        in_specs=[pl.BlockSpec((c, CONV_DIM), lambda i: (rev(i), 0)), zdt, wide, wide,
                  pl.BlockSpec((1, N_PAIRS, D_STATE, c), lambda i: (rev(i), 0, 0, 0)),
                  vec(LANES), vec(LANES), vec(SSD_WIDTH), vec(SSD_WIDTH)],
        out_specs=[pl.BlockSpec((c, CONV_DIM), lambda i: (rev(i), 0)), zdt,
                   vec(SSD_WIDTH), vec(SSD_WIDTH), vec(LANES), vec(LANES)],
        out_shape=[jax.ShapeDtypeStruct((s, CONV_DIM), F32), jax.ShapeDtypeStruct((s, D_INP), BF16),
                   jax.ShapeDtypeStruct((1, SSD_WIDTH), F32),
                   jax.ShapeDtypeStruct((1, SSD_WIDTH), F32), jax.ShapeDtypeStruct((1, LANES), F32),
                   jax.ShapeDtypeStruct((1, LANES), F32)],
        scratch_shapes=[pltpu.VMEM((N_PAIRS, D_STATE, c), F32)],
        compiler_params=_params(("arbitrary",)),
    )(xc, proj, ycore, dyall, states, dtb, alog, dskip_x, nw)


SB_BLOCK = 256
SB_SCALE = HEAD_DIM ** -0.5


def _sb_masks(i, j):
    t = SB_BLOCK
    return (_iota2((t, t), 1) + j * t) < (_iota2((t, t), 0) + i * t)


def _sb_weights(qm, kb, valid, run_lk, strict_after):
    z = _nt(qm, kb)
    ls = -_softplus(-z)
    lk = jnp.where(valid, ls - z, 0.0)
    after = _split_dot(lk, strict_after, 2) + run_lk
    w = jnp.where(valid, jnp.exp(ls + after), 0.0)
    return z, lk, w


def sb_forward(proj, y_all, name):
    s = proj.shape[0]
    t = SB_BLOCK
    nq = s // t

    def body(q_ref, k_ref, v_ref, _, y_ref, o_ref):
        i = pl.program_id(1)
        left = _iota2((t, LANES), 1) < HEAD_DIM
        qv = q_ref[...] * SB_SCALE
        zero = jnp.zeros_like(qv)
        qms = (jnp.where(left, qv, zero).astype(BF16), jnp.where(left, zero, qv).astype(BF16))
        strict_after = (_iota2((t, t), 0) > _iota2((t, t), 1)).astype(BF16)

        def step(jj, carry):
            o, runs = carry[0], carry[1:]
            j = i - jj
            rows = pl.ds(pl.multiple_of(j * t, t), t)
            kb = k_ref[rows, :].astype(BF16)
            vv = v_ref[rows, :]
            valid = _sb_masks(i, j)
            new_runs = []
            for k in range(2):
                _, lk, w = _sb_weights(qms[k], kb, valid, runs[k], strict_after)
                vm = jnp.where(left if k == 0 else jnp.logical_not(left), vv, 0.0).astype(BF16)
                o = o + _nn(w.astype(BF16), vm)
                new_runs.append(runs[k] + jnp.sum(lk, axis=1, keepdims=True))
            return (o, *new_runs)

        init = (jnp.zeros((t, LANES), F32), jnp.zeros((t, 1), F32), jnp.zeros((t, 1), F32))
        o = lax.fori_loop(0, i + 1, step, init)[0]
        o_ref[...] = o
        y_ref[...] = o.astype(BF16)

    return pl.pallas_call(
        body, name=name, grid=(2, nq),
        in_specs=[pl.BlockSpec((t, LANES), lambda p, i: (i, 3 * p)),
                  pl.BlockSpec((s, LANES), lambda p, i: (0, 3 * p + 1)),
                  pl.BlockSpec((s, LANES), lambda p, i: (0, 3 * p + 2)), ANY],
        out_specs=[pl.BlockSpec((t, LANES), lambda p, i: (i, SSD_WIDTH // LANES + p)),
                   pl.BlockSpec((t, LANES), lambda p, i: (i, p))],
        out_shape=[jax.ShapeDtypeStruct(y_all.shape, BF16), jax.ShapeDtypeStruct((s, SB_WIDTH), F32)],
        input_output_aliases={3: 0},
        compiler_params=_params(("parallel", "arbitrary")),
    )(proj, proj, proj, y_all)


def sb_backward(proj, o, dyall, dproj, name):
    s = proj.shape[0]
    t = SB_BLOCK
    nq = s // t

    def body(q_ref, k_ref, v_ref, o_ref, do_ref, _, dqkv_ref, dk_acc, dv_acc):
        dk_acc[...] = jnp.zeros_like(dk_acc)
        dv_acc[...] = jnp.zeros_like(dv_acc)
        left = _iota2((t, LANES), 1) < HEAD_DIM
        lane_masks = (left, jnp.logical_not(left))
        strict_after = (_iota2((t, t), 0) > _iota2((t, t), 1)).astype(BF16)
        from_here = (_iota2((t, t), 0) >= _iota2((t, t), 1)).astype(BF16)

        def query_block(i, _):
            qrows = pl.ds(pl.multiple_of(i * t, t), t)
            qv = q_ref[qrows, :] * SB_SCALE
            dov = do_ref[qrows, :]
            zero = jnp.zeros_like(qv)
            qb = qv.astype(BF16)
            dob = dov.astype(BF16)
            prod = dob.astype(F32) * o_ref[qrows, :]
            qms = [jnp.where(m, qv, zero).astype(BF16) for m in lane_masks]
            doms = [jnp.where(m, dov, zero).astype(BF16) for m in lane_masks]
            deltas = [jnp.sum(jnp.where(m, prod, zero), axis=1, keepdims=True) for m in lane_masks]

            def step(jj, carry):
                dq = carry[0]
                run_lk, run_e = carry[1:3], carry[3:5]
                j = i - jj
                rows = pl.ds(pl.multiple_of(j * t, t), t)
                kb = k_ref[rows, :].astype(BF16)
                vb = v_ref[rows, :].astype(BF16)
                valid = _sb_masks(i, j)
                dkj = jnp.zeros((t, LANES), F32)
                dvj = jnp.zeros((t, LANES), F32)
                new_lk, new_e = [], []
                for k in range(2):
                    z, lk, w = _sb_weights(qms[k], kb, valid, run_lk[k], strict_after)
                    sg = _sigmoid(z)
                    wb = w.astype(BF16)
                    e = _nt(doms[k], vb) * wb.astype(F32)
                    before = deltas[k] - _split_dot(e, from_here, 2) - run_e[k]
                    dz = jnp.where(valid, e * (1.0 - sg) - sg * before, 0.0).astype(BF16)
                    m = lane_masks[k]
                    dvj = dvj + jnp.where(m, _tn(wb, dob), 0.0)
                    dkj = dkj + jnp.where(m, _tn(dz, qb), 0.0)
                    dq = dq + jnp.where(m, _nn(dz, kb), 0.0)
                    new_lk.append(run_lk[k] + jnp.sum(lk, axis=1, keepdims=True))
                    new_e.append(run_e[k] + jnp.sum(e, axis=1, keepdims=True))
                dk_acc[rows, :] += dkj
                dv_acc[rows, :] += dvj
                return (dq, *new_lk, *new_e)

            col = jnp.zeros((t, 1), F32)
            dq = lax.fori_loop(0, i + 1, step, (jnp.zeros((t, LANES), F32), col, col, col, col))[0]
            dqkv_ref[qrows, 0:LANES] = (dq * SB_SCALE).astype(BF16)
            return 0

        lax.fori_loop(0, nq, query_block, 0)
        dqkv_ref[:, LANES:2 * LANES] = dk_acc[...].astype(BF16)
        dqkv_ref[:, 2 * LANES:3 * LANES] = dv_acc[...].astype(BF16)

    col = lambda f: pl.BlockSpec((s, LANES), f)
    return pl.pallas_call(
        body, name=name, grid=(2,),
        in_specs=[col(lambda p: (0, 3 * p)), col(lambda p: (0, 3 * p + 1)), col(lambda p: (0, 3 * p + 2)),
                  col(lambda p: (0, p)), col(lambda p: (0, SSD_WIDTH // LANES + p)), ANY],
        out_specs=pl.BlockSpec((s, 3 * LANES), lambda p: (0, p)),
        out_shape=jax.ShapeDtypeStruct(dproj.shape, BF16),
        input_output_aliases={5: 0},
        scratch_shapes=[pltpu.VMEM((s, LANES), F32), pltpu.VMEM((s, LANES), F32)],
        compiler_params=_params(("parallel",)),
    )(proj, proj, proj, o, dyall, dproj)


def adamw(w, g, m, v, name):
    b, r, c = w.shape
    tr = max([t for t in range(8, min(r, 512) + 1, 8) if r % t == 0], default=r)

    def body(w_ref, g_ref, m_ref, v_ref, d_ref, nm_ref, nv_ref):
        gv = g_ref[...]
        nm = ADAM_B1 * m_ref[...] + (1.0 - ADAM_B1) * gv
        nv = ADAM_B2 * v_ref[...] + (1.0 - ADAM_B2) * (gv * gv)
        m_hat = nm / (1.0 - ADAM_B1 ** ADAM_STEP)
        v_hat = nv / (1.0 - ADAM_B2 ** ADAM_STEP)
        d_ref[...] = -ADAM_LR * (m_hat / (jnp.sqrt(v_hat) + ADAM_EPS) + ADAM_WD * w_ref[...])
        nm_ref[...] = nm
        nv_ref[...] = nv

    blk = pl.BlockSpec((1, tr, c), lambda i, j: (i, j, 0))
    return pl.pallas_call(
        body, name=name, grid=(b, r // tr),
        in_specs=[blk] * 4, out_specs=[blk] * 3,
        out_shape=[jax.ShapeDtypeStruct(w.shape, F32)] * 3,
        compiler_params=_params(("parallel", "parallel")),
    )(w, g, m, v)


def _position():
    return lax.axis_index("x"), lax.axis_index("y"), lax.axis_index("c")


def _flipped(pos, flip):
    return tuple((1 - p) if f else p for p, f in zip(pos, flip))


FLIP_C = (0, 0, 1)
CHIP_FLIPS = {1: (0, 1, 0), 2: (1, 0, 0), 3: (1, 1, 0)}
SHARD_ROWS = (SHARD_IN, SHARD_OUT, SHARD_FF, SHARD_FF, SHARD_FF)


def _rows(start, size):
    return pl.ds(pl.multiple_of(start, 16), size)


HBM = pl.BlockSpec(memory_space=pltpu.HBM)
SEM = pl.BlockSpec(memory_space=pltpu.SEMAPHORE)
EFFECT = pltpu.SideEffectType.DATAFLOW_SIDE_EFFECTING


def _in_hbm(a):
    return pltpu.with_memory_space_constraint(a, pltpu.HBM)


def _landing(shape, dtype):
    return _in_hbm(lax.empty(shape, dtype))


def _copies(plan, pos, src_refs, land_refs, send_sems, recv_sems):
    return [pltpu.make_async_remote_copy(src_ref=src, dst_ref=dst, send_sem=send_sems.at[k], recv_sem=recv_sems.at[k],
                                         device_id=_flipped(pos, flip), device_id_type=MESH)
            for k, (src, dst, flip) in enumerate(plan(pos, src_refs, land_refs))]


def exchange_start(name, srcs, lands, n, plan):
    ns, nl = len(srcs), len(lands)

    def body(*refs):
        src_refs, land_refs = refs[:ns], refs[ns:ns + nl]
        send_sems, recv_sems, token = refs[ns + nl], refs[ns + nl + 1], refs[-1]
        for cp in _copies(plan, _position(), src_refs, land_refs, send_sems, recv_sems):
            cp.start()
        token[...] = jnp.zeros_like(token)

    thru = [pltpu.HBM(a.shape, a.dtype) for a in list(srcs) + list(lands)]
    out = pl.pallas_call(
        body, name=name,
        out_shape=(pltpu.SemaphoreType.DMA((n,)), pltpu.SemaphoreType.DMA((n,)), *thru, jax.ShapeDtypeStruct((8, LANES), F32)),
        in_specs=[HBM] * (ns + nl),
        out_specs=(SEM, SEM, *([HBM] * (ns + nl)), pl.BlockSpec(memory_space=pltpu.VMEM)),
        input_output_aliases={k: 2 + k for k in range(ns + nl)},
        compiler_params=pltpu.CompilerParams(has_side_effects=EFFECT),
    )(*[_in_hbm(a) for a in srcs], *lands)
    return out[0], out[1], list(out[2:2 + ns]), list(out[2 + ns:2 + ns + nl]), out[-1]


def exchange_wait(name, started, after, plan):
    send_sems, recv_sems, srcs, lands, _ = started
    ns, nl = len(srcs), len(lands)

    def body(*refs):
        src_refs, land_refs = refs[:ns], refs[ns:ns + nl]
        send_sems, recv_sems = refs[ns + nl], refs[ns + nl + 1]
        for cp in _copies(plan, _position(), src_refs, land_refs, send_sems, recv_sems):
            cp.wait_send()
            cp.wait_recv()

    out = pl.pallas_call(
        body, name=name,
        out_shape=tuple(pltpu.HBM(a.shape, a.dtype) for a in list(srcs) + list(lands)),
        in_specs=[HBM] * (ns + nl) + [SEM, SEM, ANY],
        out_specs=tuple([HBM] * (ns + nl)),
        input_output_aliases={k: k for k in range(ns + nl)},
        compiler_params=pltpu.CompilerParams(has_side_effects=EFFECT),
    )(*srcs, *lands, send_sems, recv_sems, after)
    return list(out[:ns]), list(out[ns:])


def _gather_ici_plan(pos, srcs, lands):
    chip, c = 2 * pos[0] + pos[1], pos[2]
    copies = []
    for src, dst, r in zip(srcs, lands, SHARD_ROWS):
        h = r // 2
        for f in (1, 2, 3):
            copies.append((src.at[_rows(c * h, h)], dst.at[_rows(chip * r + c * h, h)], CHIP_FLIPS[f]))
    return copies


def _gather_d2d_plan(pos, srcs, lands):
    chip, c = 2 * pos[0] + pos[1], pos[2]
    copies = []
    for own, dst, r in zip(srcs, lands, SHARD_ROWS):
        h = r // 2
        copies.append((own, dst.at[_rows(chip * r, r)], FLIP_C))
        for f in (1, 2, 3):
            at = _rows(lax.bitwise_xor(chip, f) * r + c * h, h)
            copies.append((dst.at[at], dst.at[at], FLIP_C))
    return copies


def gather_ici_start(shards):
    lands = [_landing((N_CHIPS * r, D_MODEL), BF16) for r in SHARD_ROWS]
    return exchange_start("gather_ici_start", shards, lands, 15, _gather_ici_plan)


def gather_d2d_start(shards, fulls):
    return exchange_start("gather_d2d_start", shards, fulls, 20, _gather_d2d_plan)


def _reduce_d2d_plan(pos, srcs, lands):
    c = pos[2]
    return [(src.at[:, _rows((1 - c) * (r // 2), r // 2)], dst, FLIP_C) for src, dst, r in zip(srcs, lands, SHARD_ROWS)]


def _reduce_ici_plan(pos, srcs, lands):
    chip = 2 * pos[0] + pos[1]
    return [(src.at[lax.bitwise_xor(chip, f)], dst.at[f - 1], CHIP_FLIPS[f]) for src, dst in zip(srcs, lands) for f in (1, 2, 3)]


def _reduce_swap_plan(pos, srcs, lands):
    c = pos[2]
    copies = []
    for dst, r in zip(lands, SHARD_ROWS):
        at = _rows(c * (r // 2), r // 2)
        copies.append((dst.at[at], dst.at[at], FLIP_C))
    return copies


def reduce_d2d_start(grads):
    lands = [_landing((N_CHIPS, r // 2, D_MODEL), BF16) for r in SHARD_ROWS]
    return exchange_start("reduce_d2d_start", grads, lands, 5, _reduce_d2d_plan)


def reduce_ici_start(chip_sums):
    lands = [_landing((N_CHIPS - 1, r // 2, D_MODEL), BF16) for r in SHARD_ROWS]
    return exchange_start("reduce_ici_start", chip_sums, lands, 15, _reduce_ici_plan)


def reduce_swap_start(mine):
    return exchange_start("reduce_swap_start", [], mine, 5, _reduce_swap_plan)


def add_halves(d, recv, half, name):
    nch, r, c = d.shape
    h = r // 2

    def body(half_ref, d_ref, r_ref, o_ref):
        o_ref[...] = (d_ref[...].astype(F32) + r_ref[...].astype(F32)).astype(BF16)

    return pl.pallas_call(
        body, name=name,
        grid_spec=pltpu.PrefetchScalarGridSpec(
            num_scalar_prefetch=1, grid=(nch,),
            in_specs=[pl.BlockSpec((1, h, c), lambda j, hf: (j, hf[0], 0)),
                      pl.BlockSpec((1, h, c), lambda j, hf: (j, 0, 0))],
            out_specs=pl.BlockSpec((1, h, c), lambda j, hf: (j, 0, 0))),
        out_shape=jax.ShapeDtypeStruct(recv.shape, BF16),
        compiler_params=_params(("parallel",)),
    )(half, d, recv)


def add_chips(p, recv, chip, name):
    _, r, c = p.shape

    def body(chip_ref, p_ref, r_ref, o_ref):
        acc = p_ref[0].astype(F32)
        for k in range(N_CHIPS - 1):
            acc = acc + r_ref[k].astype(F32)
        o_ref[...] = acc

    return pl.pallas_call(
        body, name=name,
        grid_spec=pltpu.PrefetchScalarGridSpec(
            num_scalar_prefetch=1, grid=(1,),
            in_specs=[pl.BlockSpec((1, r, c), lambda i, ch: (ch[0], 0, 0)),
                      pl.BlockSpec((N_CHIPS - 1, r, c), lambda i, ch: (0, 0, 0))],
            out_specs=pl.BlockSpec((r, c), lambda i, ch: (ch[1], 0))),
        out_shape=jax.ShapeDtypeStruct((2 * r, c), F32),
        compiler_params=_params(("arbitrary",)),
    )(chip, p, recv)


def adamw_layers(w, gs, m, v, name):
    b, r, c = w.shape
    tr = max([t for t in range(8, min(r, 512) + 1, 8) if r % t == 0], default=r)

    def body(w_ref, m_ref, v_ref, *rest):
        g_refs, (g_ref, d_ref, nm_ref, nv_ref) = rest[:b], rest[b:]
        layer = pl.program_id(0)
        gv = g_refs[0][...]
        for l in range(1, b):
            gv = jnp.where(layer == l, g_refs[l][...], gv)
        nm = ADAM_B1 * m_ref[0] + (1.0 - ADAM_B1) * gv
        nv = ADAM_B2 * v_ref[0] + (1.0 - ADAM_B2) * (gv * gv)
        m_hat = nm / (1.0 - ADAM_B1 ** ADAM_STEP)
        v_hat = nv / (1.0 - ADAM_B2 ** ADAM_STEP)
        g_ref[0] = gv
        d_ref[0] = -ADAM_LR * (m_hat / (jnp.sqrt(v_hat) + ADAM_EPS) + ADAM_WD * w_ref[0])
        nm_ref[0] = nm
        nv_ref[0] = nv

    nr, tc = r // tr, (c if tr < r else _tile(c, 256))
    steps = nr * (c // tc)
    blk = pl.BlockSpec((1, tr, tc), lambda i, j: (i, j % nr, j // nr))
    g_specs = [pl.BlockSpec((tr, tc), lambda i, j, l=l: (jnp.where(i == l, j % nr, jnp.where(i < l, 0, nr - 1)),
                                                         jnp.where(i == l, j // nr, jnp.where(i < l, 0, c // tc - 1))))
               for l in range(b)]
    return pl.pallas_call(
        body, name=name, grid=(b, steps),
        in_specs=[blk] * 3 + g_specs, out_specs=[blk] * 4,
        out_shape=[jax.ShapeDtypeStruct(w.shape, F32)] * 4,
        compiler_params=_params(("arbitrary", "arbitrary")),
    )(w, m, v, *gs)


def small_allreduce(v, name):
    r, c = v.shape

    def body(v_ref, o_ref, buf, send_sems, recv_sems):
        pos = _position()
        me = 4 * pos[0] + 2 * pos[1] + pos[2]
        buf[0] = v_ref[...]
        copies = []
        for f in range(1, 8):
            flip = ((f >> 2) & 1, (f >> 1) & 1, f & 1)
            cp = pltpu.make_async_remote_copy(
                src_ref=v_ref, dst_ref=buf.at[f], send_sem=send_sems.at[f - 1], recv_sem=recv_sems.at[f - 1],
                device_id=_flipped(pos, flip), device_id_type=MESH)
            cp.start()
            copies.append(cp)
        for cp in copies:
            cp.wait()
        acc = buf[me]
        for d in range(1, 8):
            acc = acc + buf[lax.bitwise_xor(me, d)]
        o_ref[...] = acc

    return pl.pallas_call(
        body, name=name,
        in_specs=[pl.BlockSpec(memory_space=pltpu.VMEM)], out_specs=pl.BlockSpec(memory_space=pltpu.VMEM),
        out_shape=jax.ShapeDtypeStruct((r, c), F32),
        scratch_shapes=[pltpu.VMEM((8, r, c), F32), pltpu.SemaphoreType.DMA((7,)), pltpu.SemaphoreType.DMA((7,))],
    )(v)


_IN_SEGMENTS = ((0, 1544, 128), (128, 1800, 128), (256, 2056, 128), (384, 1672, 128), (512, 1928, 128), (640, 2184, 128),
                (OFF_Z, 0, SSD_WIDTH), (OFF_DT, 1536, SSD_HEADS), (OFF_XBC, 512, CONV_DIM), (OFF_P, 2312, POOL_WIDTH))


def _in_column_map():
    m = np.full((D_INP,), -1, np.int64)
    for at, orig, n in _IN_SEGMENTS:
        cols = np.arange(orig, orig + n)
        m[at:at + n] = (cols // COLS_IN) * SHARD_IN + cols % COLS_IN
    return m


def take_rows(a, idx, name):
    dep, r_in, c = a.shape
    blk = LANES
    n_out, n_in = len(idx) // blk, r_in // blk
    assert len(idx) % blk == 0 and r_in % blk == 0
    sources = [sorted({int(v) // blk for v in idx[blk * i:blk * (i + 1)] if v >= 0}) for i in range(n_out)]
    width = max(len(s) for s in sources)
    table = np.zeros((n_out, width), np.int32)
    for i, s in enumerate(sources):
        spare = [b for b in range(n_in) if b not in s][:width - len(s)]
        table[i] = s + spare

    def body(tbl_ref, idx_ref, *refs):
        in_refs, o_ref = refs[:width], refs[width]
        i = pl.program_id(1)
        src = idx_ref[...]
        acc = jnp.zeros((blk, c), F32)
        for k in range(width):
            pick = (src == tbl_ref[i, k] * blk + _iota2((blk, blk), 1)).astype(BF16)
            acc = acc + _nn(pick, in_refs[k][0])
        o_ref[0] = acc.astype(BF16)

    return pl.pallas_call(
        body, name=name,
        grid_spec=pltpu.PrefetchScalarGridSpec(
            num_scalar_prefetch=1, grid=(dep, n_out),
            in_specs=[pl.BlockSpec((blk, 1), lambda l, i, t: (i, 0))] +
                     [pl.BlockSpec((1, blk, c), lambda l, i, t, k=k: (l, t[i, k], 0)) for k in range(width)],
            out_specs=pl.BlockSpec((1, blk, c), lambda l, i, t: (l, i, 0))),
        out_shape=jax.ShapeDtypeStruct((dep, len(idx), c), BF16),
        compiler_params=_params(("parallel", "parallel")),
    )(jnp.asarray(table), jnp.asarray(np.asarray(idx, np.int32).reshape(-1, 1)), *([a] * width))


def _in_weight_layout(staged):
    return take_rows(staged, _in_column_map(), "w_in_layout")


def _in_gradient_layout(dwt):
    fwd = _in_column_map()
    inv = np.full((N_CHIPS * SHARD_IN,), -1, np.int64)
    inv[fwd[fwd >= 0]] = np.nonzero(fwd >= 0)[0]
    return take_rows(dwt, inv, "dw_in_layout")


SMALL_NAMES = ("norm1_w", "conv_w", "conv_b", "dt_bias", "a_log", "d_skip", "ssd_norm_w", "pool_w", "pool_b",
               "pool_scale", "norm2_w", "final_norm_w")
SMALL_ROWS = 104


def _pack_small(parts):
    flat = jnp.concatenate([p.reshape(-1) for p in parts])
    return jnp.pad(flat, (0, SMALL_ROWS * D_MODEL - flat.shape[0])).reshape(SMALL_ROWS, D_MODEL)


def _unpack_small(flat, shapes):
    flat = flat.reshape(-1)
    out, at = [], 0
    for shp in shapes:
        n = int(np.prod(shp))
        out.append(flat[at:at + n].reshape(shp))
        at += n
    return out


def kernel(x, norm1_w, w_in, conv_w, conv_b, dt_bias, a_log, d_skip, ssd_norm_w, pool_w, pool_b, pool_scale, w_out, norm2_w, w_gate, w_up, w_down, final_norm_w, loss_target, m_norm1_w, m_w_in, m_conv_w, m_conv_b, m_dt_bias, m_a_log, m_d_skip, m_ssd_norm_w, m_pool_w, m_pool_b, m_pool_scale, m_w_out, m_norm2_w, m_w_gate, m_w_up, m_w_down, m_final_norm_w, v_norm1_w, v_w_in, v_conv_w, v_conv_b, v_dt_bias, v_a_log, v_d_skip, v_ssd_norm_w, v_pool_w, v_pool_b, v_pool_scale, v_w_out, v_norm2_w, v_w_gate, v_w_up, v_w_down, v_final_norm_w):
    px, py, pc = _position()
    chip = 2 * px + py
    chip_arr = jnp.reshape(chip, (1,)).astype(jnp.int32)
    half_arr = jnp.reshape(pc, (1,)).astype(jnp.int32)

    def layer_shards(l):
        w_in_t = jnp.pad(jnp.swapaxes(w_in[l], 0, 1).astype(BF16), ((0, SHARD_IN - COLS_IN), (0, 0)))
        return [w_in_t, w_out[l].astype(BF16), jnp.swapaxes(w_gate[l], 0, 1).astype(BF16),
                jnp.swapaxes(w_up[l], 0, 1).astype(BF16), w_down[l].astype(BF16)]

    over_ici = [gather_ici_start(layer_shards(l)) for l in range(DEPTH)]

    def pass_on(l, after):
        own, arrived = exchange_wait("gather_ici_wait", over_ici[l], after, _gather_ici_plan)
        return gather_d2d_start(own, arrived)

    def weights_of(swap, after):
        _, (w_in_st, w_out_l, w_gate_t, w_up_t, w_down_l) = exchange_wait("gather_d2d_wait", swap, after, _gather_d2d_plan)
        return _in_weight_layout(w_in_st[None])[0], w_out_l, w_gate_t, w_up_t, w_down_l

    pad_heads = lambda v: jnp.pad(v, ((0, 0), (0, LANES - SSD_HEADS)))[:, None, :]
    dtb, alog = pad_heads(dt_bias), pad_heads(a_log)
    dskip_x = jnp.repeat(d_skip, HEAD_DIM, axis=1)[:, None, :]
    eye = jnp.eye(len(POOL_WINDOWS), dtype=F32)
    wbd = (pool_w[:, :, :, None, :] * eye[None, :, None, :, None]).reshape(DEPTH, POOL_WIDTH, POOL_WIDTH).astype(BF16)
    pool_b2 = pool_b.reshape(DEPTH, 1, POOL_WIDTH)
    cw_cols = lax.dynamic_update_slice(jnp.zeros((DEPTH, CONV_WIDTH, CONV_DIM), F32), conv_w,
                                       (0, 0, chip * (CONV_DIM // N_CHIPS)))
    cw_cols = jnp.where(pc == 0, cw_cols, 0.0)
    cw_rows = (DEPTH * CONV_WIDTH * CONV_DIM) // D_MODEL
    conv_w_f = small_allreduce(jnp.pad(cw_cols.reshape(cw_rows, D_MODEL), ((0, 8), (0, 0))), "gather_conv_w")
    conv_w_f = conv_w_f[:cw_rows].reshape(DEPTH, CONV_WIDTH, CONV_DIM)
    cw8 = jnp.pad(conv_w_f, ((0, 0), (0, 8 - CONV_WIDTH), (0, 0)))

    h = x[0]
    saved, weights = [], []
    swap = pass_on(0, over_ici[DEPTH - 1][4])
    weights.append(weights_of(swap, swap[4]))
    for l in range(DEPTH):
        w_in_f, w_out_f, w_gate_t, w_up_t, w_down_f = weights[l]
        proj = rms_matmul(h, norm1_w[l][None], w_in_f, "in_proj")
        xc = conv_forward(proj, cw8[l], conv_b[l][None], "conv_fwd")
        y_all, ycore, states = ssd_forward(proj, xc, dtb[l], alog[l], dskip_x[l], ssd_norm_w[l][None], "ssd_fwd")
        y_all, o_sb = sb_forward(proj, y_all, "sb_fwd")
        swap = pass_on(l + 1, o_sb) if l + 1 < DEPTH else None
        y_all = pool_forward(proj, wbd[l], pool_b2[l], pool_scale[l][None], y_all, "pool_fwd",
                             after=None if swap is None else swap[4])
        x1 = matmul_residual(y_all, w_out_f, h, "out_proj")
        x2, g, u = ffn_forward(x1, norm2_w[l][None], w_gate_t, w_up_t, w_down_f, "ffn_fwd")
        if swap is not None:
            weights.append(weights_of(swap, x2))
        saved.append((h, proj, xc, ycore, states, o_sb, y_all, x1, g, u))
        h = x2

    loss_part, dx, dxb, d_final = loss_head(h, final_norm_w[None], loss_target[0], "loss_head")
    loss = lax.psum(loss_part[0, 0], ("x", "y", "c"))

    small = {n: [None] * DEPTH for n in SMALL_NAMES if n != "final_norm_w"}
    chip_half = jnp.concatenate([chip_arr, half_arr])
    reduced = [None] * DEPTH
    d2d = ici = swap = None

    def add_cores(d2d, after):
        mine, theirs = exchange_wait("reduce_d2d_wait", d2d[1], after, _reduce_d2d_plan)
        return d2d[0], reduce_ici_start([add_halves(d, t, half_arr, "reduce_add_halves") for d, t in zip(mine, theirs)])

    def add_all(ici, after):
        sums, theirs = exchange_wait("reduce_ici_wait", ici[1], after, _reduce_ici_plan)
        return ici[0], reduce_swap_start([add_chips(p, t, chip_half, "reduce_add_chips") for p, t in zip(sums, theirs)])

    def finish(swap, after):
        reduced[swap[0]] = exchange_wait("reduce_swap_wait", swap[1], after, _reduce_swap_plan)[1]

    for l in reversed(range(DEPTH)):
        xin, proj, xc, ycore, states, o_sb, y_all, x1, g, u = saved[l]
        w_in_f, w_out_f, w_gate_t, w_up_t, w_down_f = weights[l]
        dg, du, act = ffn_backward_act(dxb, g, u, w_down_f, "ffn_bwd_act", after=None if d2d is None else d2d[1][4])
        if swap is not None:
            finish(swap, dg)
            swap = None
        dx1, dx1b, h2, dn2 = rms_backward([dg, du], [w_gate_t, w_up_t], x1, norm2_w[l][None], dx, "ffn_bwd_norm", 256)
        if d2d is not None:
            ici = add_cores(d2d, dx1b)
        dw_down = matmul_tn(act, dxb, "dw_down", after=None if ici is None else ici[1][4])
        dw_gate = matmul_tn(dg, h2, "dw_gate")
        dw_up = matmul_tn(du, h2, "dw_up")
        dyall = matmul_nt(dx1b, w_out_f, "out_proj_bwd")
        dw_out = matmul_tn(y_all, dx1b, "dw_out")
        dxc, dproj, dsn, ddsk, ddtb, dalog = ssd_backward(proj, xc, ycore, dyall, states, dtb[l], alog[l],
                                                          dskip_x[l], ssd_norm_w[l][None], "ssd_bwd")
        dproj, dcw, dcb = conv_backward(proj, dxc, cw8[l], conv_b[l][None], dproj, "conv_bwd")
        dproj = sb_backward(proj, o_sb, dyall, dproj, "sb_bwd")
        dproj, dwbd, dpb, dps = pool_backward(proj, dyall, wbd[l], pool_b2[l], pool_scale[l][None], dproj, "pool_bwd")
        dx, dxb, h1, dn1 = rms_backward([dproj], [w_in_f], xin, norm1_w[l][None], dx1, "in_proj_bwd", 256)
        dw_in = _in_gradient_layout(matmul_tn(dproj, h1, "dw_in")[None])[0]
        if ici is not None:
            swap = add_all(ici, dw_in)
            ici = None
        d2d = (l, reduce_d2d_start([dw.reshape(N_CHIPS, r, D_MODEL) for dw, r in
                                    zip((dw_in, dw_out, dw_gate, dw_up, dw_down), SHARD_ROWS)]))
        small["norm1_w"][l] = dn1[0]
        small["conv_w"][l] = dcw[:CONV_WIDTH]
        small["conv_b"][l] = dcb[0]
        small["dt_bias"][l] = ddtb[0, :SSD_HEADS]
        small["a_log"][l] = dalog[0, :SSD_HEADS]
        small["d_skip"][l] = ddsk.reshape(SSD_HEADS, HEAD_DIM).sum(axis=1)
        small["ssd_norm_w"][l] = dsn[0]
        small["pool_w"][l] = jnp.stack([dwbd[64 * k:64 * k + 64, 64 * k:64 * k + 64] for k in range(len(POOL_WINDOWS))])
        small["pool_b"][l] = dpb.reshape(len(POOL_WINDOWS), -1)
        small["pool_scale"][l] = dps[0]
        small["norm2_w"][l] = dn2[0]
    grad_x = dx[None]

    finish(swap, d2d[1][4])
    ici = add_cores(d2d, reduced[1][0])
    swap = add_all(ici, ici[1][4])
    finish(swap, swap[1][4])
    g_big = {n: [reduced[l][k] for l in range(DEPTH)] for k, n in enumerate(("w_in", "w_out", "w_gate", "w_up", "w_down"))}
    g_big["w_in"] = [gl[:COLS_IN] for gl in g_big["w_in"]]
    transposed = ("w_in", "w_gate", "w_up")

    small_parts = [d_final if n == "final_norm_w" else jnp.stack(small[n]) for n in SMALL_NAMES]
    small_shapes = [p.shape for p in small_parts]
    g_small = dict(zip(SMALL_NAMES, _unpack_small(small_allreduce(_pack_small(small_parts), "reduce_small"), small_shapes)))
    g_small["final_norm_w"] = g_small["final_norm_w"].reshape(final_norm_w.shape)
    g_small["conv_w"] = lax.dynamic_slice_in_dim(g_small["conv_w"], chip * (CONV_DIM // N_CHIPS), CONV_DIM // N_CHIPS, axis=2)

    given = dict(norm1_w=(norm1_w, m_norm1_w, v_norm1_w), w_in=(w_in, m_w_in, v_w_in), conv_w=(conv_w, m_conv_w, v_conv_w),
                 conv_b=(conv_b, m_conv_b, v_conv_b), dt_bias=(dt_bias, m_dt_bias, v_dt_bias), a_log=(a_log, m_a_log, v_a_log),
                 d_skip=(d_skip, m_d_skip, v_d_skip), ssd_norm_w=(ssd_norm_w, m_ssd_norm_w, v_ssd_norm_w),
                 pool_w=(pool_w, m_pool_w, v_pool_w), pool_b=(pool_b, m_pool_b, v_pool_b),
                 pool_scale=(pool_scale, m_pool_scale, v_pool_scale), w_out=(w_out, m_w_out, v_w_out),
                 norm2_w=(norm2_w, m_norm2_w, v_norm2_w), w_gate=(w_gate, m_w_gate, v_w_gate), w_up=(w_up, m_w_up, v_w_up),
                 w_down=(w_down, m_w_down, v_w_down), final_norm_w=(final_norm_w, m_final_norm_w, v_final_norm_w))
    order = ("norm1_w", "w_in", "conv_w", "conv_b", "dt_bias", "a_log", "d_skip", "ssd_norm_w", "pool_w", "pool_b",
             "pool_scale", "w_out", "norm2_w", "w_gate", "w_up", "w_down", "final_norm_w")
    grads = dict(g_small)
    results = {}
    for n in ("w_in", "w_out", "w_gate", "w_up", "w_down"):
        w, m, v = given[n]
        if n in transposed:
            out = adamw_layers(jnp.swapaxes(w, 1, 2), g_big[n], jnp.swapaxes(m, 1, 2), jnp.swapaxes(v, 1, 2), "adamw_" + n)
            out = [jnp.swapaxes(o, 1, 2) for o in out]
        else:
            out = adamw_layers(w, g_big[n], m, v, "adamw_" + n)
        grads[n], results[n] = out[0], tuple(out[1:])
    small_shapes = [given[n][0].shape for n in SMALL_NAMES]
    packed = [_pack_small([given[n][k] for n in SMALL_NAMES])[None] for k in range(3)]
    packed_g = _pack_small([grads[n] for n in SMALL_NAMES])[None]
    small_out = adamw(packed[0], packed_g, packed[1], packed[2], "adamw_small")
    small_out = [_unpack_small(o[0], small_shapes) for o in small_out]
    for i, n in enumerate(SMALL_NAMES):
        results[n] = tuple(small_out[k][i] for k in range(3))

    return (loss, grad_x, *[grads[n] for n in order], *[results[n][0] for n in order],
            *[results[n][1] for n in order], *[results[n][2] for n in order])
```

```python
import numpy as np
import jax
import jax.numpy as jnp
from jax import lax
from jax.experimental import pallas as pl
from jax.experimental.pallas import tpu as pltpu

F32 = jnp.float32
BF16 = jnp.bfloat16
MESH = pl.DeviceIdType.MESH
ANY = pl.BlockSpec(memory_space=pl.ANY)

D_MODEL = 1024
DEPTH = 4
EPS = 1e-6
SSD_WIDTH = 512
SSD_HEADS = 8
HEAD_DIM = 64
D_STATE = 128
CHUNK = 128
CONV_WIDTH = 4
CONV_DIM = 1024
SB_WIDTH = 256
POOL_WIDTH = 256
POOL_WINDOWS = (2, 4, 8, 16)
D_FF = 2816
D_IN = 2568
N_CHIPS = 4
OFF_QKV, OFF_Z, OFF_DT, OFF_XBC, OFF_P = 0, 768, 1280, 1536, 2560
D_INP = 2816
ZDT = 768
SHARD_IN, SHARD_OUT, SHARD_FF = 672, 256, 704
COLS_IN = 642
ADAM_LR, ADAM_B1, ADAM_B2, ADAM_EPS, ADAM_WD, ADAM_STEP = 0.001, 0.9, 0.999, 1e-08, 0.01, 10
LANES = 128
VMEM_LIMIT = 56 * 1024 * 1024


def _params(sem=None):
    return pltpu.CompilerParams(dimension_semantics=sem, vmem_limit_bytes=VMEM_LIMIT)


def _tile(n, cap):
    best = None
    for t in range(LANES, min(n, cap) + 1, LANES):
        if n % t == 0:
            best = t
    assert best is not None, (n, cap)
    return best


def _nt(a, b):
    return lax.dot_general(a, b, (((1,), (1,)), ((), ())), preferred_element_type=F32)


def _tn(a, b):
    return lax.dot_general(a, b, (((0,), (0,)), ((), ())), preferred_element_type=F32)


def _nn(a, b):
    return jnp.dot(a, b, preferred_element_type=F32)


def _split_dot(a, b_exact, terms=3, dot=_nn):
    acc = None
    rest = a
    for _ in range(terms):
        hi = rest.astype(BF16)
        part = dot(hi, b_exact)
        acc = part if acc is None else acc + part
        rest = rest - hi.astype(F32)
    return acc


def _split_dot_left(a_exact, b, terms=3):
    acc = None
    rest = b
    for _ in range(terms):
        hi = rest.astype(BF16)
        part = _nn(a_exact, hi)
        acc = part if acc is None else acc + part
        rest = rest - hi.astype(F32)
    return acc


def _sigmoid(x):
    return 1.0 / (1.0 + jnp.exp(-x))


def _softplus(x):
    return jnp.maximum(x, 0.0) + jnp.log(1.0 + jnp.exp(-jnp.abs(x)))


def _iota2(shape, dim):
    return lax.broadcasted_iota(jnp.int32, shape, dim)


def _after(after):
    ops = [] if after is None else list(after) if isinstance(after, (list, tuple)) else [after]
    return [ANY] * len(ops), ops


def rms_matmul(x, nw, wt, name, after=None):
    s, d = x.shape
    n = wt.shape[0]
    tm, tn = _tile(s, 512), _tile(n, 1408)
    specs, ops = _after(after)

    def body(x_ref, nw_ref, w_ref, *rest):
        o_ref, h_ref = rest[len(ops):]

        @pl.when(pl.program_id(1) == 0)
        def _():
            xv = x_ref[...]
            r = lax.rsqrt(jnp.mean(xv * xv, axis=-1, keepdims=True) + EPS)
            h_ref[...] = (xv * r * nw_ref[...]).astype(BF16)
        o_ref[...] = _nt(h_ref[...], w_ref[...])

    return pl.pallas_call(
        body, name=name, grid=(s // tm, n // tn),
        in_specs=[pl.BlockSpec((tm, d), lambda i, j: (i, 0)), pl.BlockSpec((1, d), lambda i, j: (0, 0)),
                  pl.BlockSpec((tn, d), lambda i, j: (j, 0))] + specs,
        out_specs=pl.BlockSpec((tm, tn), lambda i, j: (i, j)),
        out_shape=jax.ShapeDtypeStruct((s, n), F32),
        scratch_shapes=[pltpu.VMEM((tm, d), BF16)],
        compiler_params=_params(("parallel", "arbitrary")),
    )(x, nw, wt, *ops)


def matmul_residual(a, w, res, name):
    s, k = a.shape
    n = w.shape[1]
    tm, tn = _tile(s, 512), _tile(n, 512)

    def body(a_ref, w_ref, r_ref, o_ref):
        o_ref[...] = r_ref[...] + _nn(a_ref[...], w_ref[...])

    return pl.pallas_call(
        body, name=name, grid=(s // tm, n // tn),
        in_specs=[pl.BlockSpec((tm, k), lambda i, j: (i, 0)), pl.BlockSpec((k, tn), lambda i, j: (0, j)),
                  pl.BlockSpec((tm, tn), lambda i, j: (i, j))],
        out_specs=pl.BlockSpec((tm, tn), lambda i, j: (i, j)),
        out_shape=jax.ShapeDtypeStruct((s, n), F32),
        compiler_params=_params(("parallel", "parallel")),
    )(a, w, res)


def matmul_nt(a, w, name, out_dtype=F32, after=None):
    s, n = a.shape
    k = w.shape[0]
    tm, tk = _tile(s, 512), _tile(k, 512)
    specs, ops = _after(after)

    def body(a_ref, w_ref, *rest):
        rest[-1][...] = _nt(a_ref[...], w_ref[...]).astype(out_dtype)

    return pl.pallas_call(
        body, name=name, grid=(s // tm, k // tk),
        in_specs=[pl.BlockSpec((tm, n), lambda i, j: (i, 0)), pl.BlockSpec((tk, n), lambda i, j: (j, 0))] + specs,
        out_specs=pl.BlockSpec((tm, tk), lambda i, j: (i, j)),
        out_shape=jax.ShapeDtypeStruct((s, k), out_dtype),
        compiler_params=_params(("parallel", "parallel")),
    )(a, w, *ops)


def matmul_tn(a, b, name, after=None):
    s, m = a.shape
    n = b.shape[1]
    tm, tn = _tile(m, 512), _tile(n, 512)

    def body(a_ref, b_ref, *rest):
        rest[-1][...] = _tn(a_ref[...], b_ref[...]).astype(BF16)

    specs, ops = _after(after)
    return pl.pallas_call(
        body, name=name, grid=(m // tm, n // tn),
        in_specs=[pl.BlockSpec((s, tm), lambda i, j: (0, i)), pl.BlockSpec((s, tn), lambda i, j: (0, j))] + specs,
        out_specs=pl.BlockSpec((tm, tn), lambda i, j: (i, j)),
        out_shape=jax.ShapeDtypeStruct((m, n), BF16),
        compiler_params=_params(("parallel", "parallel")),
    )(a, b, *ops)


def ffn_forward(x1, nw, wgt, wut, wd, name):
    s, d = x1.shape
    f = wgt.shape[0]
    tm, tf = _tile(s, 1024), _tile(f, 256)

    def body(x_ref, nw_ref, wg_ref, wu_ref, wd_ref, o_ref, g_ref, u_ref, h_ref, acc_ref):
        j = pl.program_id(1)

        @pl.when(j == 0)
        def _():
            xv = x_ref[...]
            r = lax.rsqrt(jnp.mean(xv * xv, axis=-1, keepdims=True) + EPS)
            h_ref[...] = (xv * r * nw_ref[...]).astype(BF16)
            acc_ref[...] = xv

        h = h_ref[...]
        g = _nt(h, wg_ref[...])
        u = _nt(h, wu_ref[...])
        g_ref[...] = g.astype(BF16)
        u_ref[...] = u.astype(BF16)
        a = (g * _sigmoid(g) * u).astype(BF16)
        acc_ref[...] += _nn(a, wd_ref[...])

        @pl.when(j == pl.num_programs(1) - 1)
        def _():
            o_ref[...] = acc_ref[...]

    wblk = pl.BlockSpec((tf, d), lambda i, j: (j, 0))
    return pl.pallas_call(
        body, name=name, grid=(s // tm, f // tf),
        in_specs=[pl.BlockSpec((tm, d), lambda i, j: (i, 0)), pl.BlockSpec((1, d), lambda i, j: (0, 0)), wblk, wblk, wblk],
        out_specs=[pl.BlockSpec((tm, d), lambda i, j: (i, 0)), pl.BlockSpec((tm, tf), lambda i, j: (i, j)),
                   pl.BlockSpec((tm, tf), lambda i, j: (i, j))],
        out_shape=[jax.ShapeDtypeStruct((s, d), F32), jax.ShapeDtypeStruct((s, f), BF16),
                   jax.ShapeDtypeStruct((s, f), BF16)],
        scratch_shapes=[pltpu.VMEM((tm, d), BF16), pltpu.VMEM((tm, d), F32)],
        compiler_params=_params(("parallel", "arbitrary")),
    )(x1, nw, wgt, wut, wd)


def ffn_backward_act(dx2, g, u, wd, name, after=None):
    s, d = dx2.shape
    f = wd.shape[0]
    tm, tf = _tile(s, 512), _tile(f, 1408)
    specs, ops = _after(after)

    def body(dx_ref, g_ref, u_ref, wd_ref, *rest):
        dg_ref, du_ref, a_ref = rest[len(ops):]
        da = _nt(dx_ref[...], wd_ref[...])
        gv = g_ref[...].astype(F32)
        uv = u_ref[...].astype(F32)
        sg = _sigmoid(gv)
        silu = gv * sg
        dg_ref[...] = (da * uv * (sg * (1.0 + gv * (1.0 - sg)))).astype(BF16)
        du_ref[...] = (da * silu).astype(BF16)
        a_ref[...] = (silu * uv).astype(BF16)

    blk = pl.BlockSpec((tm, tf), lambda i, j: (i, j))
    return pl.pallas_call(
        body, name=name, grid=(s // tm, f // tf),
        in_specs=[pl.BlockSpec((tm, d), lambda i, j: (i, 0)), blk, blk, pl.BlockSpec((tf, d), lambda i, j: (j, 0))] + specs,
        out_specs=[blk, blk, blk],
        out_shape=[jax.ShapeDtypeStruct((s, f), BF16)] * 3,
        compiler_params=_params(("parallel", "parallel")),
    )(dx2, g, u, wd, *ops)


def rms_backward(dzs, wts, x, nw, dres, name, tm, after=None):
    s, d = x.shape
    nz = len(dzs)
    specs, ops = _after(after)

    def body(*refs):
        dz_refs, w_refs = refs[:nz], refs[nz:2 * nz]
        x_ref, nw_ref, dres_ref = refs[2 * nz:2 * nz + 3]
        dx_ref, dxb_ref, h_ref, dnw_ref = refs[2 * nz + 3 + len(ops):]
        dh = _nn(dz_refs[0][...], w_refs[0][...])
        for k in range(1, nz):
            dh = dh + _nn(dz_refs[k][...], w_refs[k][...])
        xv = x_ref[...]
        r = lax.rsqrt(jnp.mean(xv * xv, axis=-1, keepdims=True) + EPS)
        xhat = xv * r
        nwv = nw_ref[...]
        h_ref[...] = (xhat * nwv).astype(BF16)

        @pl.when(pl.program_id(0) == 0)
        def _():
            dnw_ref[...] = jnp.zeros_like(dnw_ref)

        dnw_ref[...] += jnp.sum(dh * xhat, axis=0, keepdims=True)
        gdh = dh * nwv
        dx = dres_ref[...] + r * (gdh - xhat * jnp.mean(gdh * xhat, axis=-1, keepdims=True))
        dx_ref[...] = dx
        dxb_ref[...] = dx.astype(BF16)

    row = pl.BlockSpec((tm, d), lambda i: (i, 0))
    in_specs = [pl.BlockSpec((tm, dz.shape[1]), lambda i: (i, 0)) for dz in dzs]
    in_specs += [pl.BlockSpec(w.shape, lambda i: (0, 0)) for w in wts]
    in_specs += [row, pl.BlockSpec((1, d), lambda i: (0, 0)), row] + specs
    return pl.pallas_call(
        body, name=name, grid=(s // tm,),
        in_specs=in_specs,
        out_specs=[row, row, row, pl.BlockSpec((1, d), lambda i: (0, 0))],
        out_shape=[jax.ShapeDtypeStruct((s, d), F32), jax.ShapeDtypeStruct((s, d), BF16),
                   jax.ShapeDtypeStruct((s, d), BF16), jax.ShapeDtypeStruct((1, d), F32)],
        compiler_params=_params(("arbitrary",)),
    )(*dzs, *wts, x, nw, dres, *ops)


def loss_head(x, nw, target, name):
    s, d = x.shape
    tm = _tile(s, 512)

    def body(x_ref, nw_ref, t_ref, loss_ref, dx_ref, dxb_ref, dnw_ref):
        xv = x_ref[...]
        r = lax.rsqrt(jnp.mean(xv * xv, axis=-1, keepdims=True) + EPS)
        xhat = xv * r
        nwv = nw_ref[...]
        err = xhat * nwv - t_ref[...]

        @pl.when(pl.program_id(0) == 0)
        def _():
            dnw_ref[...] = jnp.zeros_like(dnw_ref)
            loss_ref[...] = jnp.zeros_like(loss_ref)

        part = jnp.sum(jnp.sum(err * err, axis=-1, keepdims=True), axis=0, keepdims=True) * (0.5 / d)
        loss_ref[...] += jnp.broadcast_to(part, loss_ref.shape)
        dout = err * (1.0 / d)
        dnw_ref[...] += jnp.sum(dout * xhat, axis=0, keepdims=True)
        gdh = dout * nwv
        dx = r * (gdh - xhat * jnp.mean(gdh * xhat, axis=-1, keepdims=True))
        dx_ref[...] = dx
        dxb_ref[...] = dx.astype(BF16)

    row = pl.BlockSpec((tm, d), lambda i: (i, 0))
    return pl.pallas_call(
        body, name=name, grid=(s // tm,),
        in_specs=[row, pl.BlockSpec((1, d), lambda i: (0, 0)), row],
        out_specs=[pl.BlockSpec((1, LANES), lambda i: (0, 0)), row, row, pl.BlockSpec((1, d), lambda i: (0, 0))],
        out_shape=[jax.ShapeDtypeStruct((1, LANES), F32), jax.ShapeDtypeStruct((s, d), F32),
                   jax.ShapeDtypeStruct((s, d), BF16), jax.ShapeDtypeStruct((1, d), F32)],
        compiler_params=_params(("arbitrary",)),
    )(x, nw, target)


def _shift_down(x, k):
    return jnp.where(_iota2(x.shape, 0) >= k, pltpu.roll(x, k, axis=0), 0.0)


def _shift_up(x, k):
    s = x.shape[0]
    return jnp.where(_iota2(x.shape, 0) < s - k, pltpu.roll(x, s - k, axis=0), 0.0)


CONV_TILE = 256


def conv_forward(proj, cw, cb, name):
    s = proj.shape[0]
    tn = CONV_TILE
    off = OFF_XBC // tn

    def body(u_ref, w_ref, b_ref, o_ref):
        u = u_ref[...]
        pre = b_ref[...] + w_ref[CONV_WIDTH - 1:CONV_WIDTH, :] * u
        for i in range(CONV_WIDTH - 1):
            pre = pre + w_ref[i:i + 1, :] * _shift_down(u, CONV_WIDTH - 1 - i)
        o_ref[...] = pre * _sigmoid(pre)

    return pl.pallas_call(
        body, name=name, grid=(CONV_DIM // tn,),
        in_specs=[pl.BlockSpec((s, tn), lambda j: (0, off + j)), pl.BlockSpec((8, tn), lambda j: (0, j)),
                  pl.BlockSpec((1, tn), lambda j: (0, j))],
        out_specs=pl.BlockSpec((s, tn), lambda j: (0, j)),
        out_shape=jax.ShapeDtypeStruct((s, CONV_DIM), F32),
        compiler_params=_params(("parallel",)),
    )(proj, cw, cb)


def conv_backward(proj, dxc, cw, cb, dproj, name):
    s = proj.shape[0]
    tn = CONV_TILE
    off = OFF_XBC // tn

    def body(u_ref, d_ref, w_ref, b_ref, _, du_ref, dw_ref, db_ref):
        u = u_ref[...]
        shifted = [_shift_down(u, CONV_WIDTH - 1 - i) for i in range(CONV_WIDTH - 1)] + [u]
        pre = b_ref[...] + w_ref[CONV_WIDTH - 1:CONV_WIDTH, :] * u
        for i in range(CONV_WIDTH - 1):
            pre = pre + w_ref[i:i + 1, :] * shifted[i]
        sg = _sigmoid(pre)
        dpre = d_ref[...] * (sg * (1.0 + pre * (1.0 - sg)))
        du = w_ref[CONV_WIDTH - 1:CONV_WIDTH, :] * dpre
        for i in range(CONV_WIDTH - 1):
            du = du + w_ref[i:i + 1, :] * _shift_up(dpre, CONV_WIDTH - 1 - i)
        du_ref[...] = du.astype(BF16)
        rows = [jnp.sum(dpre * shifted[i], axis=0, keepdims=True) for i in range(CONV_WIDTH)]
        rows.append(jnp.zeros((8 - CONV_WIDTH, tn), F32))
        dw_ref[...] = jnp.concatenate(rows, axis=0)
        db_ref[...] = jnp.sum(dpre, axis=0, keepdims=True)

    return pl.pallas_call(
        body, name=name, grid=(CONV_DIM // tn,),
        in_specs=[pl.BlockSpec((s, tn), lambda j: (0, off + j)), pl.BlockSpec((s, tn), lambda j: (0, j)),
                  pl.BlockSpec((8, tn), lambda j: (0, j)), pl.BlockSpec((1, tn), lambda j: (0, j)), ANY],
        out_specs=[pl.BlockSpec((s, tn), lambda j: (0, off + j)), pl.BlockSpec((8, tn), lambda j: (0, j)),
                   pl.BlockSpec((1, tn), lambda j: (0, j))],
        out_shape=[jax.ShapeDtypeStruct(dproj.shape, BF16), jax.ShapeDtypeStruct((8, CONV_DIM), F32),
                   jax.ShapeDtypeStruct((1, CONV_DIM), F32)],
        input_output_aliases={4: 0},
        compiler_params=_params(("parallel",)),
    )(proj, dxc, cw, cb, dproj)


def _pool_lane_window(shape):
    grp = _iota2(shape, 1) // (POOL_WIDTH // len(POOL_WINDOWS))
    win = jnp.full(shape, POOL_WINDOWS[-1], jnp.int32)
    for gi in range(len(POOL_WINDOWS) - 2, -1, -1):
        win = jnp.where(grp == gi, POOL_WINDOWS[gi], win)
    return grp, win


def _pool_select(grp, sums):
    out = sums[-1]
    for gi in range(len(sums) - 2, -1, -1):
        out = jnp.where(grp == gi, sums[gi], out)
    return out


def _pool_pooled(p):
    grp, win = _pool_lane_window(p.shape)
    inv_count = 1.0 / jnp.minimum(_iota2(p.shape, 0) + 1, win).astype(F32)
    sums, acc, k = [], p, 1
    for _ in POOL_WINDOWS:
        acc = acc + _shift_down(acc, k)
        sums.append(acc)
        k *= 2
    return _pool_select(grp, sums) * inv_count - p, grp, inv_count


def pool_forward(proj, wbd, pb, ps, y_all, name, after=None):
    s = proj.shape[0]
    specs, ops = _after(after)

    def body(p_ref, w_ref, b_ref, s_ref, *rest):
        o_ref = rest[-1]
        pooled, _, _ = _pool_pooled(p_ref[...])
        mixed = _nn(pooled.astype(BF16), w_ref[...]) + b_ref[...]
        o_ref[...] = (mixed * s_ref[...]).astype(BF16)

    vec = pl.BlockSpec((1, POOL_WIDTH), lambda j: (0, 0))
    return pl.pallas_call(
        body, name=name, grid=(1,),
        in_specs=[pl.BlockSpec((s, POOL_WIDTH), lambda j: (0, OFF_P // POOL_WIDTH)),
                  pl.BlockSpec((POOL_WIDTH, POOL_WIDTH), lambda j: (0, 0)), vec, vec, ANY] + specs,
        out_specs=pl.BlockSpec((s, POOL_WIDTH), lambda j: (0, (SSD_WIDTH + SB_WIDTH) // POOL_WIDTH)),
        out_shape=jax.ShapeDtypeStruct(y_all.shape, BF16),
        input_output_aliases={4: 0},
        compiler_params=_params(("arbitrary",)),
    )(proj, wbd, pb, ps, y_all, *ops)


def pool_backward(proj, dyall, wbd, pb, ps, dproj, name):
    s = proj.shape[0]

    def body(p_ref, dy_ref, w_ref, b_ref, s_ref, _, dp_ref, dw_ref, db_ref, ds_ref):
        pooled, grp, inv_count = _pool_pooled(p_ref[...])
        pooled_b = pooled.astype(BF16)
        mixed = _nn(pooled_b, w_ref[...]) + b_ref[...]
        dy = dy_ref[...]
        ds_ref[...] = jnp.sum(dy * mixed, axis=0, keepdims=True)
        dmixed = dy * s_ref[...]
        db_ref[...] = jnp.sum(dmixed, axis=0, keepdims=True)
        dmixed_b = dmixed.astype(BF16)
        dw_ref[...] = _tn(pooled_b, dmixed_b)
        dpooled = _nt(dmixed_b, w_ref[...])
        sums, acc, k = [], dpooled * inv_count, 1
        for _ in POOL_WINDOWS:
            acc = acc + _shift_up(acc, k)
            sums.append(acc)
            k *= 2
        dp_ref[...] = (_pool_select(grp, sums) - dpooled).astype(BF16)

    vec = pl.BlockSpec((1, POOL_WIDTH), lambda j: (0, 0))
    mat = pl.BlockSpec((POOL_WIDTH, POOL_WIDTH), lambda j: (0, 0))
    pcol = pl.BlockSpec((s, POOL_WIDTH), lambda j: (0, OFF_P // POOL_WIDTH))
    return pl.pallas_call(
        body, name=name, grid=(1,),
        in_specs=[pcol, pl.BlockSpec((s, POOL_WIDTH), lambda j: (0, (SSD_WIDTH + SB_WIDTH) // POOL_WIDTH)), mat, vec, vec, ANY],
        out_specs=[pcol, mat, vec, vec],
        out_shape=[jax.ShapeDtypeStruct(dproj.shape, BF16), jax.ShapeDtypeStruct((POOL_WIDTH, POOL_WIDTH), F32),
                   jax.ShapeDtypeStruct((1, POOL_WIDTH), F32), jax.ShapeDtypeStruct((1, POOL_WIDTH), F32)],
        input_output_aliases={5: 0},
        compiler_params=_params(("arbitrary",)),
    )(proj, dyall, wbd, pb, ps, dproj)


N_PAIRS = SSD_HEADS // 2


def _ssd_common(xc, dtraw, dtb, alog):
    c = CHUNK
    dt = _softplus(dtraw + dtb)
    a = -jnp.exp(alog)
    ltri = (_iota2((c, c), 0) >= _iota2((c, c), 1)).astype(BF16)
    acum = _split_dot_left(ltri, dt * a)
    expand = (_iota2((c, SSD_WIDTH), 1) // HEAD_DIM == _iota2((c, SSD_WIDTH), 0)).astype(BF16)
    expand_wide = (_iota2((c, SSD_HEADS * c), 1) // c == _iota2((c, SSD_HEADS * c), 0)).astype(BF16)
    acum_x = _split_dot(acum, expand)
    dt_x = _split_dot(dt, expand)
    alast_x = acum_x[c - 1:c, :]
    return dict(dt=dt, a=a, acum=acum, acum_x=acum_x, dt_x=dt_x, ea_x=jnp.exp(acum_x),
                dte_x=jnp.exp(alast_x - acum_x), eal_x=jnp.exp(alast_x),
                acol=_split_dot(acum, expand_wide), acum_t=acum.T,
                xs=xc[:, :SSD_WIDTH], causal=_iota2((c, c), 0) >= _iota2((c, c), 1),
                left=_iota2((c, c), 1) < HEAD_DIM)


def _ssd_group(xc, g):
    b = xc[:, SSD_WIDTH + D_STATE * g:SSD_WIDTH + D_STATE * (g + 1)]
    cm = xc[:, SSD_WIDTH + 2 * D_STATE + D_STATE * g:SSD_WIDTH + 2 * D_STATE + D_STATE * (g + 1)]
    return b, cm


def _ssd_decay(q, hh):
    col = q["acol"][:, CHUNK * hh:CHUNK * (hh + 1)]
    row = q["acum_t"][hh:hh + 1, :]
    return jnp.where(q["causal"], jnp.exp(jnp.minimum(col - row, 0.0)), 0.0)


def ssd_forward(proj, xc, dtb, alog, dskip_x, nw, name):
    s = xc.shape[0]
    nc = s // CHUNK

    def body(xc_ref, zdt_ref, dtb_ref, alog_ref, dsk_ref, nw_ref, y_ref, yc_ref, st_ref, state):
        @pl.when(pl.program_id(0) == 0)
        def _():
            state[...] = jnp.zeros_like(state)

        xcv = xc_ref[...]
        q = _ssd_common(xcv, zdt_ref[:, SSD_WIDTH:SSD_WIDTH + LANES], dtb_ref[...], alog_ref[...])
        x = q["xs"] * q["dt_x"]
        xb = x.astype(BF16)
        xd = (x * q["dte_x"]).astype(BF16)
        pieces = []
        for g in range(2):
            bg, cg = _ssd_group(xcv, g)
            bgb, cgb = bg.astype(BF16), cg.astype(BF16)
            cb = _nt(cgb, bgb)
            bgt = bg.T.astype(BF16)
            for pr in (2 * g, 2 * g + 1):
                sl = slice(CHUNK * pr, CHUNK * (pr + 1))
                st = state[pr]
                st_ref[0, pr] = st
                yp = _nn(cgb, st.astype(BF16)) * q["ea_x"][:, sl]
                for k, hh in enumerate((2 * pr, 2 * pr + 1)):
                    w = (cb * _ssd_decay(q, hh)).astype(BF16)
                    mask = q["left"] if k == 0 else jnp.logical_not(q["left"])
                    yp = yp + _nn(w, jnp.where(mask, xb[:, sl], jnp.zeros_like(xb[:, sl])))
                state[pr] = st * q["eal_x"][:, sl] + _nn(bgt, xd[:, sl])
                pieces.append(yp)
        y = jnp.concatenate(pieces, axis=1) + q["xs"] * dsk_ref[...]
        yc_ref[...] = y
        zv = zdt_ref[:, :SSD_WIDTH]
        yg = y * (zv * _sigmoid(zv))
        r = lax.rsqrt(jnp.mean(yg * yg, axis=-1, keepdims=True) + EPS)
        y_ref[...] = (yg * r * nw_ref[...]).astype(BF16)

    vec = lambda n: pl.BlockSpec((1, n), lambda c: (0, 0))
    return pl.pallas_call(
        body, name=name, grid=(nc,),
        in_specs=[pl.BlockSpec((CHUNK, CONV_DIM), lambda c: (c, 0)),
                  pl.BlockSpec((CHUNK, ZDT), lambda c: (c, OFF_Z // ZDT)),
                  vec(LANES), vec(LANES), vec(SSD_WIDTH), vec(SSD_WIDTH)],
        out_specs=[pl.BlockSpec((CHUNK, SSD_WIDTH), lambda c: (c, 0)), pl.BlockSpec((CHUNK, SSD_WIDTH), lambda c: (c, 0)),
                   pl.BlockSpec((1, N_PAIRS, D_STATE, CHUNK), lambda c: (c, 0, 0, 0))],
        out_shape=[jax.ShapeDtypeStruct((s, D_MODEL), BF16), jax.ShapeDtypeStruct((s, SSD_WIDTH), F32),
                   jax.ShapeDtypeStruct((nc, N_PAIRS, D_STATE, CHUNK), F32)],
        scratch_shapes=[pltpu.VMEM((N_PAIRS, D_STATE, CHUNK), F32)],
        compiler_params=_params(("arbitrary",)),
    )(xc, proj, dtb, alog, dskip_x, nw)


def ssd_backward(proj, xc, ycore, dyall, states, dtb, alog, dskip_x, nw, name):
    s = xc.shape[0]
    nc = s // CHUNK
    c = CHUNK

    def body(xc_ref, zdt_ref, yc_ref, dy_ref, st_ref, dtb_ref, alog_ref, dsk_ref, nw_ref,
             dxc_ref, dzdt_ref, dnw_ref, ddsk_ref, ddtb_ref, dalog_ref, dstate):
        @pl.when(pl.program_id(0) == 0)
        def _():
            dstate[...] = jnp.zeros_like(dstate)
            dnw_ref[...] = jnp.zeros_like(dnw_ref)
            ddsk_ref[...] = jnp.zeros_like(ddsk_ref)
            ddtb_ref[...] = jnp.zeros_like(ddtb_ref)
            dalog_ref[...] = jnp.zeros_like(dalog_ref)

        xcv = xc_ref[...]
        dtraw = zdt_ref[:, SSD_WIDTH:SSD_WIDTH + LANES]
        q = _ssd_common(xcv, dtraw, dtb_ref[...], alog_ref[...])
        xs = q["xs"]
        x = xs * q["dt_x"]
        zv, yc, dy, nwv = zdt_ref[:, :SSD_WIDTH], yc_ref[...], dy_ref[...], nw_ref[...]
        sgz = _sigmoid(zv)
        siluz = zv * sgz
        yg = yc * siluz
        r = lax.rsqrt(jnp.mean(yg * yg, axis=-1, keepdims=True) + EPS)
        dnw_ref[...] += jnp.sum(dy * yg * r, axis=0, keepdims=True)
        g1 = dy * nwv
        dyg = r * (g1 - yg * (r * r) * jnp.mean(g1 * yg, axis=-1, keepdims=True))
        dyv = dyg * siluz
        dz = (dyg * yc * (sgz * (1.0 + zv * (1.0 - sgz)))).astype(BF16)
        ddsk_ref[...] += jnp.sum(dyv * xs, axis=0, keepdims=True)
        dye = dyv * q["ea_x"]
        dx_parts, yoff_parts, u_parts, v_parts, e_parts = [], [], [], [], []
        db_parts, dc_parts = [], []
        for g in range(2):
            bg, cg = _ssd_group(xcv, g)
            bgb, cgb = bg.astype(BF16), cg.astype(BF16)
            cb = _nt(cgb, bgb)
            cgt = cg.T.astype(BF16)
            dgsum = jnp.zeros((c, c), F32)
            dbg = jnp.zeros((c, D_STATE), F32)
            dcg = jnp.zeros((c, D_STATE), F32)
            for pr in (2 * g, 2 * g + 1):
                sl = slice(c * pr, c * (pr + 1))
                st = st_ref[0, pr]
                dst = dstate[pr]
                stb, dstb = st.astype(BF16), dst.astype(BF16)
                xp = x[:, sl]
                xpb = xp.astype(BF16)
                dyp = dyv[:, sl]
                xdp = xp * q["dte_x"][:, sl]
                yoff_parts.append(_nn(cgb, stb) * q["ea_x"][:, sl])
                rr = _nn(bgb, dstb)
                dxp = rr * q["dte_x"][:, sl]
                u_parts.append(rr * xdp)
                v_parts.append(dst * st * q["eal_x"][:, sl])
                for k, hh in enumerate((2 * pr, 2 * pr + 1)):
                    decay = _ssd_decay(q, hh)
                    w = cb * decay
                    mask = q["left"] if k == 0 else jnp.logical_not(q["left"])
                    dym = jnp.where(mask, dyp, 0.0).astype(BF16)
                    dw = _nt(dym, xpb)
                    dgsum = dgsum + dw * decay
                    e_parts.append(dw * w)
                    dxp = dxp + _nn(w.T.astype(BF16), dym)
                dyeb = dye[:, sl].astype(BF16)
                dcg = dcg + _nt(dyeb, stb)
                dbg = dbg + _nt(xdp.astype(BF16), dstb)
                dstate[pr] = dst * q["eal_x"][:, sl] + _nn(cgt, dyeb)
                dx_parts.append(dxp)
            dcg = dcg + _nn(dgsum.astype(BF16), bgb)
            dbg = dbg + _nn(dgsum.T.astype(BF16), cgb)
            db_parts.append(dbg)
            dc_parts.append(dcg)
        dx = jnp.concatenate(dx_parts, axis=1)
        yoff = jnp.concatenate(yoff_parts, axis=1)
        u = jnp.concatenate(u_parts, axis=1)
        v = jnp.concatenate(v_parts, axis=1)
        reduce_heads = (_iota2((SSD_WIDTH, c), 0) // HEAD_DIM == _iota2((SSD_WIDTH, c), 1)).astype(BF16)
        to_head = (_iota2((SSD_HEADS * c, c), 0) // c == _iota2((SSD_HEADS * c, c), 1)).astype(BF16)
        da = _split_dot(dyv * yoff - u, reduce_heads, 2)
        da = da + _split_dot(jnp.concatenate(e_parts, axis=1), to_head, 2)
        da = da - _split_dot(jnp.concatenate(e_parts, axis=0), to_head, 2, dot=_tn)
        dalast = jnp.sum(_split_dot(u + v, reduce_heads, 2), axis=0, keepdims=True)
        da = da + jnp.where(_iota2((c, c), 0) == c - 1, dalast, 0.0)
        utri = (_iota2((c, c), 1) >= _iota2((c, c), 0)).astype(BF16)
        dda = _split_dot_left(utri, da)
        ddt = dda * q["a"] + _split_dot(dx * xs, reduce_heads, 2)
        dalog_ref[...] += jnp.sum(dda * q["dt"], axis=0, keepdims=True) * q["a"]
        ddtraw = jnp.where(_iota2((c, c), 1) < SSD_HEADS, ddt * _sigmoid(dtraw + dtb_ref[...]), 0.0)
        ddtb_ref[...] += jnp.sum(ddtraw, axis=0, keepdims=True)
        dzdt_ref[...] = jnp.concatenate([dz, ddtraw.astype(BF16), jnp.zeros((c, ZDT - SSD_WIDTH - LANES), BF16)], axis=1)
        dxs = dx * q["dt_x"] + dyv * dsk_ref[...]
        dxc_ref[...] = jnp.concatenate([dxs] + db_parts + dc_parts, axis=1)

    rev = lambda i: nc - 1 - i
    vec = lambda n: pl.BlockSpec((1, n), lambda i: (0, 0))
    wide = pl.BlockSpec((c, SSD_WIDTH), lambda i: (rev(i), 0))
    zdt = pl.BlockSpec((c, ZDT), lambda i: (rev(i), OFF_Z // ZDT))
    return pl.pallas_call(
        body, name=name, grid=(nc,),
        in_specs=[pl.BlockSpec((c, CONV_DIM), lambda i: (rev(i), 0)), zdt, wide, wide,
                  pl.BlockSpec((1, N_PAIRS, D_STATE, c), lambda i: (rev(i), 0, 0, 0)),
                  vec(LANES), vec(LANES), vec(SSD_WIDTH), vec(SSD_WIDTH)],
        out_specs=[pl.BlockSpec((c, CONV_DIM), lambda i: (rev(i), 0)), zdt,
                   vec(SSD_WIDTH), vec(SSD_WIDTH), vec(LANES), vec(LANES)],
        out_shape=[jax.ShapeDtypeStruct((s, CONV_DIM), F32), jax.ShapeDtypeStruct((s, D_INP), BF16),
                   jax.ShapeDtypeStruct((1, SSD_WIDTH), F32),
                   jax.ShapeDtypeStruct((1, SSD_WIDTH), F32), jax.ShapeDtypeStruct((1, LANES), F32),
                   jax.ShapeDtypeStruct((1, LANES), F32)],
        scratch_shapes=[pltpu.VMEM((N_PAIRS, D_STATE, c), F32)],
        compiler_params=_params(("arbitrary",)),
    )(xc, proj, ycore, dyall, states, dtb, alog, dskip_x, nw)


SB_BLOCK = 256
SB_SCALE = HEAD_DIM ** -0.5


def _sb_masks(i, j):
    t = SB_BLOCK
    return (_iota2((t, t), 1) + j * t) < (_iota2((t, t), 0) + i * t)


def _sb_weights(qm, kb, valid, run_lk, strict_after):
    z = _nt(qm, kb)
    ls = -_softplus(-z)
    lk = jnp.where(valid, ls - z, 0.0)
    after = _split_dot(lk, strict_after, 2) + run_lk
    w = jnp.where(valid, jnp.exp(ls + after), 0.0)
    return z, lk, w


def sb_forward(proj, y_all, name):
    s = proj.shape[0]
    t = SB_BLOCK
    nq = s // t

    def body(q_ref, k_ref, v_ref, _, y_ref, o_ref):
        i = pl.program_id(1)
        left = _iota2((t, LANES), 1) < HEAD_DIM
        qv = q_ref[...] * SB_SCALE
        zero = jnp.zeros_like(qv)
        qms = (jnp.where(left, qv, zero).astype(BF16), jnp.where(left, zero, qv).astype(BF16))
        strict_after = (_iota2((t, t), 0) > _iota2((t, t), 1)).astype(BF16)

        def step(jj, carry):
            o, runs = carry[0], carry[1:]
            j = i - jj
            rows = pl.ds(pl.multiple_of(j * t, t), t)
            kb = k_ref[rows, :].astype(BF16)
            vv = v_ref[rows, :]
            valid = _sb_masks(i, j)
            new_runs = []
            for k in range(2):
                _, lk, w = _sb_weights(qms[k], kb, valid, runs[k], strict_after)
                vm = jnp.where(left if k == 0 else jnp.logical_not(left), vv, 0.0).astype(BF16)
                o = o + _nn(w.astype(BF16), vm)
                new_runs.append(runs[k] + jnp.sum(lk, axis=1, keepdims=True))
            return (o, *new_runs)

        init = (jnp.zeros((t, LANES), F32), jnp.zeros((t, 1), F32), jnp.zeros((t, 1), F32))
        o = lax.fori_loop(0, i + 1, step, init)[0]
        o_ref[...] = o
        y_ref[...] = o.astype(BF16)

    return pl.pallas_call(
        body, name=name, grid=(2, nq),
        in_specs=[pl.BlockSpec((t, LANES), lambda p, i: (i, 3 * p)),
                  pl.BlockSpec((s, LANES), lambda p, i: (0, 3 * p + 1)),
                  pl.BlockSpec((s, LANES), lambda p, i: (0, 3 * p + 2)), ANY],
        out_specs=[pl.BlockSpec((t, LANES), lambda p, i: (i, SSD_WIDTH // LANES + p)),
                   pl.BlockSpec((t, LANES), lambda p, i: (i, p))],
        out_shape=[jax.ShapeDtypeStruct(y_all.shape, BF16), jax.ShapeDtypeStruct((s, SB_WIDTH), F32)],
        input_output_aliases={3: 0},
        compiler_params=_params(("parallel", "arbitrary")),
    )(proj, proj, proj, y_all)


def sb_backward(proj, o, dyall, dproj, name):
    s = proj.shape[0]
    t = SB_BLOCK
    nq = s // t

    def body(q_ref, k_ref, v_ref, o_ref, do_ref, _, dqkv_ref, dk_acc, dv_acc):
        dk_acc[...] = jnp.zeros_like(dk_acc)
        dv_acc[...] = jnp.zeros_like(dv_acc)
        left = _iota2((t, LANES), 1) < HEAD_DIM
        lane_masks = (left, jnp.logical_not(left))
        strict_after = (_iota2((t, t), 0) > _iota2((t, t), 1)).astype(BF16)
        from_here = (_iota2((t, t), 0) >= _iota2((t, t), 1)).astype(BF16)

        def query_block(i, _):
            qrows = pl.ds(pl.multiple_of(i * t, t), t)
            qv = q_ref[qrows, :] * SB_SCALE
            dov = do_ref[qrows, :]
            zero = jnp.zeros_like(qv)
            qb = qv.astype(BF16)
            dob = dov.astype(BF16)
            prod = dob.astype(F32) * o_ref[qrows, :]
            qms = [jnp.where(m, qv, zero).astype(BF16) for m in lane_masks]
            doms = [jnp.where(m, dov, zero).astype(BF16) for m in lane_masks]
            deltas = [jnp.sum(jnp.where(m, prod, zero), axis=1, keepdims=True) for m in lane_masks]

            def step(jj, carry):
                dq = carry[0]
                run_lk, run_e = carry[1:3], carry[3:5]
                j = i - jj
                rows = pl.ds(pl.multiple_of(j * t, t), t)
                kb = k_ref[rows, :].astype(BF16)
                vb = v_ref[rows, :].astype(BF16)
                valid = _sb_masks(i, j)
                dkj = jnp.zeros((t, LANES), F32)
                dvj = jnp.zeros((t, LANES), F32)
                new_lk, new_e = [], []
                for k in range(2):
                    z, lk, w = _sb_weights(qms[k], kb, valid, run_lk[k], strict_after)
                    sg = _sigmoid(z)
                    wb = w.astype(BF16)
                    e = _nt(doms[k], vb) * wb.astype(F32)
                    before = deltas[k] - _split_dot(e, from_here, 2) - run_e[k]
                    dz = jnp.where(valid, e * (1.0 - sg) - sg * before, 0.0).astype(BF16)
                    m = lane_masks[k]
                    dvj = dvj + jnp.where(m, _tn(wb, dob), 0.0)
                    dkj = dkj + jnp.where(m, _tn(dz, qb), 0.0)
                    dq = dq + jnp.where(m, _nn(dz, kb), 0.0)
                    new_lk.append(run_lk[k] + jnp.sum(lk, axis=1, keepdims=True))
                    new_e.append(run_e[k] + jnp.sum(e, axis=1, keepdims=True))
                dk_acc[rows, :] += dkj
                dv_acc[rows, :] += dvj
                return (dq, *new_lk, *new_e)

            col = jnp.zeros((t, 1), F32)
            dq = lax.fori_loop(0, i + 1, step, (jnp.zeros((t, LANES), F32), col, col, col, col))[0]
            dqkv_ref[qrows, 0:LANES] = (dq * SB_SCALE).astype(BF16)
            return 0

        lax.fori_loop(0, nq, query_block, 0)
        dqkv_ref[:, LANES:2 * LANES] = dk_acc[...].astype(BF16)
        dqkv_ref[:, 2 * LANES:3 * LANES] = dv_acc[...].astype(BF16)

    col = lambda f: pl.BlockSpec((s, LANES), f)
    return pl.pallas_call(
        body, name=name, grid=(2,),
        in_specs=[col(lambda p: (0, 3 * p)), col(lambda p: (0, 3 * p + 1)), col(lambda p: (0, 3 * p + 2)),
                  col(lambda p: (0, p)), col(lambda p: (0, SSD_WIDTH // LANES + p)), ANY],
        out_specs=pl.BlockSpec((s, 3 * LANES), lambda p: (0, p)),
        out_shape=jax.ShapeDtypeStruct(dproj.shape, BF16),
        input_output_aliases={5: 0},
        scratch_shapes=[pltpu.VMEM((s, LANES), F32), pltpu.VMEM((s, LANES), F32)],
        compiler_params=_params(("parallel",)),
    )(proj, proj, proj, o, dyall, dproj)


def adamw(w, g, m, v, name):
    b, r, c = w.shape
    tr = max([t for t in range(8, min(r, 512) + 1, 8) if r % t == 0], default=r)

    def body(w_ref, g_ref, m_ref, v_ref, d_ref, nm_ref, nv_ref):
        gv = g_ref[...]
        nm = ADAM_B1 * m_ref[...] + (1.0 - ADAM_B1) * gv
        nv = ADAM_B2 * v_ref[...] + (1.0 - ADAM_B2) * (gv * gv)
        m_hat = nm / (1.0 - ADAM_B1 ** ADAM_STEP)
        v_hat = nv / (1.0 - ADAM_B2 ** ADAM_STEP)
        d_ref[...] = -ADAM_LR * (m_hat / (jnp.sqrt(v_hat) + ADAM_EPS) + ADAM_WD * w_ref[...])
        nm_ref[...] = nm
        nv_ref[...] = nv

    blk = pl.BlockSpec((1, tr, c), lambda i, j: (i, j, 0))
    return pl.pallas_call(
        body, name=name, grid=(b, r // tr),
        in_specs=[blk] * 4, out_specs=[blk] * 3,
        out_shape=[jax.ShapeDtypeStruct(w.shape, F32)] * 3,
        compiler_params=_params(("parallel", "parallel")),
    )(w, g, m, v)


def _position():
    return lax.axis_index("x"), lax.axis_index("y"), lax.axis_index("c")


def _flipped(pos, flip):
    return tuple((1 - p) if f else p for p, f in zip(pos, flip))


FLIP_C = (0, 0, 1)
CHIP_FLIPS = {1: (0, 1, 0), 2: (1, 0, 0), 3: (1, 1, 0)}
SHARD_ROWS = (SHARD_IN, SHARD_OUT, SHARD_FF, SHARD_FF, SHARD_FF)


def _rows(start, size):
    return pl.ds(pl.multiple_of(start, 16), size)


HBM = pl.BlockSpec(memory_space=pltpu.HBM)
SEM = pl.BlockSpec(memory_space=pltpu.SEMAPHORE)
EFFECT = pltpu.SideEffectType.DATAFLOW_SIDE_EFFECTING


def _in_hbm(a):
    return pltpu.with_memory_space_constraint(a, pltpu.HBM)


def _landing(shape, dtype):
    return _in_hbm(lax.empty(shape, dtype))


def _copies(plan, pos, src_refs, land_refs, send_sems, recv_sems):
    return [pltpu.make_async_remote_copy(src_ref=src, dst_ref=dst, send_sem=send_sems.at[k], recv_sem=recv_sems.at[k],
                                         device_id=_flipped(pos, flip), device_id_type=MESH)
            for k, (src, dst, flip) in enumerate(plan(pos, src_refs, land_refs))]


def exchange_start(name, srcs, lands, n, plan, after=None):
    ns, nl = len(srcs), len(lands)
    specs, ops = _after(after)

    def body(*refs):
        src_refs, land_refs = refs[:ns], refs[ns:ns + nl]
        send_sems, recv_sems, token = refs[ns + nl + len(ops)], refs[ns + nl + len(ops) + 1], refs[-1]
        for cp in _copies(plan, _position(), src_refs, land_refs, send_sems, recv_sems):
            cp.start()
        token[...] = jnp.zeros_like(token)

    thru = [pltpu.HBM(a.shape, a.dtype) for a in list(srcs) + list(lands)]
    out = pl.pallas_call(
        body, name=name,
        out_shape=(pltpu.SemaphoreType.DMA((n,)), pltpu.SemaphoreType.DMA((n,)), *thru, jax.ShapeDtypeStruct((8, LANES), F32)),
        in_specs=[HBM] * (ns + nl) + specs,
        out_specs=(SEM, SEM, *([HBM] * (ns + nl)), pl.BlockSpec(memory_space=pltpu.VMEM)),
        input_output_aliases={k: 2 + k for k in range(ns + nl)},
        compiler_params=pltpu.CompilerParams(has_side_effects=EFFECT),
    )(*[_in_hbm(a) for a in srcs], *lands, *ops)
    return out[0], out[1], list(out[2:2 + ns]), list(out[2 + ns:2 + ns + nl]), out[-1]


def exchange_wait(name, started, after, plan):
    send_sems, recv_sems, srcs, lands, _ = started
    ns, nl = len(srcs), len(lands)
    specs, ops = _after(after)

    def body(*refs):
        src_refs, land_refs = refs[:ns], refs[ns:ns + nl]
        send_sems, recv_sems = refs[ns + nl], refs[ns + nl + 1]
        for cp in _copies(plan, _position(), src_refs, land_refs, send_sems, recv_sems):
            cp.wait_send()
            cp.wait_recv()

    out = pl.pallas_call(
        body, name=name,
        out_shape=tuple(pltpu.HBM(a.shape, a.dtype) for a in list(srcs) + list(lands)),
        in_specs=[HBM] * (ns + nl) + [SEM, SEM] + specs,
        out_specs=tuple([HBM] * (ns + nl)),
        input_output_aliases={k: k for k in range(ns + nl)},
        compiler_params=pltpu.CompilerParams(has_side_effects=EFFECT),
    )(*srcs, *lands, send_sems, recv_sems, *ops)
    return list(out[:ns]), list(out[ns:])


def _gather_ici_plan(pos, srcs, lands):
    chip, c = 2 * pos[0] + pos[1], pos[2]
    copies = []
    for src, dst, r in zip(srcs, lands, SHARD_ROWS):
        h = r // 2
        for f in (1, 2, 3):
            copies.append((src.at[_rows(c * h, h)], dst.at[_rows(chip * r + c * h, h)], CHIP_FLIPS[f]))
    return copies


def _gather_d2d_plan(pos, srcs, lands):
    chip, c = 2 * pos[0] + pos[1], pos[2]
    copies = []
    for own, dst, r in zip(srcs, lands, SHARD_ROWS):
        h = r // 2
        copies.append((own, dst.at[_rows(chip * r, r)], FLIP_C))
        for f in (1, 2, 3):
            at = _rows(lax.bitwise_xor(chip, f) * r + c * h, h)
            copies.append((dst.at[at], dst.at[at], FLIP_C))
    return copies


def gather_ici_start(shards, after=None):
    lands = [_landing((N_CHIPS * r, D_MODEL), BF16) for r in SHARD_ROWS]
    return exchange_start("gather_ici_start", shards, lands, 15, _gather_ici_plan, after=after)


def gather_d2d_start(shards, fulls):
    return exchange_start("gather_d2d_start", shards, fulls, 20, _gather_d2d_plan)


def _reduce_d2d_plan(pos, srcs, lands):
    c = pos[2]
    return [(src.at[:, _rows((1 - c) * (r // 2), r // 2)], dst, FLIP_C) for src, dst, r in zip(srcs, lands, SHARD_ROWS)]


def _reduce_ici_plan(pos, srcs, lands):
    chip = 2 * pos[0] + pos[1]
    return [(src.at[lax.bitwise_xor(chip, f)], dst.at[f - 1], CHIP_FLIPS[f]) for src, dst in zip(srcs, lands) for f in (1, 2, 3)]


def _reduce_swap_plan(pos, srcs, lands):
    c = pos[2]
    copies = []
    for dst, r in zip(lands, SHARD_ROWS):
        at = _rows(c * (r // 2), r // 2)
        copies.append((dst.at[at], dst.at[at], FLIP_C))
    return copies


def reduce_d2d_start(grads):
    lands = [_landing((N_CHIPS, r // 2, D_MODEL), BF16) for r in SHARD_ROWS]
    return exchange_start("reduce_d2d_start", grads, lands, 5, _reduce_d2d_plan)


def reduce_ici_start(chip_sums):
    lands = [_landing((N_CHIPS - 1, r // 2, D_MODEL), BF16) for r in SHARD_ROWS]
    return exchange_start("reduce_ici_start", chip_sums, lands, 15, _reduce_ici_plan)


def reduce_swap_start(mine):
    return exchange_start("reduce_swap_start", [], mine, 5, _reduce_swap_plan)


def add_halves(d, recv, half, name):
    nch, r, c = d.shape
    h = r // 2

    def body(half_ref, d_ref, r_ref, o_ref):
        o_ref[...] = (d_ref[...].astype(F32) + r_ref[...].astype(F32)).astype(BF16)

    return pl.pallas_call(
        body, name=name,
        grid_spec=pltpu.PrefetchScalarGridSpec(
            num_scalar_prefetch=1, grid=(nch,),
            in_specs=[pl.BlockSpec((1, h, c), lambda j, hf: (j, hf[0], 0)),
                      pl.BlockSpec((1, h, c), lambda j, hf: (j, 0, 0))],
            out_specs=pl.BlockSpec((1, h, c), lambda j, hf: (j, 0, 0))),
        out_shape=jax.ShapeDtypeStruct(recv.shape, BF16),
        compiler_params=_params(("parallel",)),
    )(half, d, recv)


def add_chips(p, recv, chip, name):
    _, r, c = p.shape

    def body(chip_ref, p_ref, r_ref, o_ref):
        acc = p_ref[0].astype(F32)
        for k in range(N_CHIPS - 1):
            acc = acc + r_ref[k].astype(F32)
        o_ref[...] = acc

    return pl.pallas_call(
        body, name=name,
        grid_spec=pltpu.PrefetchScalarGridSpec(
            num_scalar_prefetch=1, grid=(1,),
            in_specs=[pl.BlockSpec((1, r, c), lambda i, ch: (ch[0], 0, 0)),
                      pl.BlockSpec((N_CHIPS - 1, r, c), lambda i, ch: (0, 0, 0))],
            out_specs=pl.BlockSpec((r, c), lambda i, ch: (ch[1], 0))),
        out_shape=jax.ShapeDtypeStruct((2 * r, c), F32),
        compiler_params=_params(("arbitrary",)),
    )(chip, p, recv)


def adamw_layers(w, gs, m, v, name):
    b, r, c = w.shape
    tr = max([t for t in range(8, min(r, 512) + 1, 8) if r % t == 0], default=r)

    def body(w_ref, m_ref, v_ref, *rest):
        g_refs, (g_ref, d_ref, nm_ref, nv_ref) = rest[:b], rest[b:]
        layer = pl.program_id(0)
        gv = g_refs[0][...]
        for l in range(1, b):
            gv = jnp.where(layer == l, g_refs[l][...], gv)
        nm = ADAM_B1 * m_ref[0] + (1.0 - ADAM_B1) * gv
        nv = ADAM_B2 * v_ref[0] + (1.0 - ADAM_B2) * (gv * gv)
        m_hat = nm / (1.0 - ADAM_B1 ** ADAM_STEP)
        v_hat = nv / (1.0 - ADAM_B2 ** ADAM_STEP)
        g_ref[0] = gv
        d_ref[0] = -ADAM_LR * (m_hat / (jnp.sqrt(v_hat) + ADAM_EPS) + ADAM_WD * w_ref[0])
        nm_ref[0] = nm
        nv_ref[0] = nv

    nr, tc = r // tr, (c if tr < r else _tile(c, 256))
    steps = nr * (c // tc)
    blk = pl.BlockSpec((1, tr, tc), lambda i, j: (i, j % nr, j // nr))
    g_specs = [pl.BlockSpec((tr, tc), lambda i, j, l=l: (jnp.where(i == l, j % nr, jnp.where(i < l, 0, nr - 1)),
                                                         jnp.where(i == l, j // nr, jnp.where(i < l, 0, c // tc - 1))))
               for l in range(b)]
    return pl.pallas_call(
        body, name=name, grid=(b, steps),
        in_specs=[blk] * 3 + g_specs, out_specs=[blk] * 4,
        out_shape=[jax.ShapeDtypeStruct(w.shape, F32)] * 4,
        compiler_params=_params(("arbitrary", "arbitrary")),
    )(w, m, v, *gs)


def small_allreduce(v, name):
    r, c = v.shape

    def body(v_ref, o_ref, buf, send_sems, recv_sems):
        pos = _position()
        me = 4 * pos[0] + 2 * pos[1] + pos[2]
        buf[0] = v_ref[...]
        copies = []
        for f in range(1, 8):
            flip = ((f >> 2) & 1, (f >> 1) & 1, f & 1)
            cp = pltpu.make_async_remote_copy(
                src_ref=v_ref, dst_ref=buf.at[f], send_sem=send_sems.at[f - 1], recv_sem=recv_sems.at[f - 1],
                device_id=_flipped(pos, flip), device_id_type=MESH)
            cp.start()
            copies.append(cp)
        for cp in copies:
            cp.wait()
        acc = buf[me]
        for d in range(1, 8):
            acc = acc + buf[lax.bitwise_xor(me, d)]
        o_ref[...] = acc

    return pl.pallas_call(
        body, name=name,
        in_specs=[pl.BlockSpec(memory_space=pltpu.VMEM)], out_specs=pl.BlockSpec(memory_space=pltpu.VMEM),
        out_shape=jax.ShapeDtypeStruct((r, c), F32),
        scratch_shapes=[pltpu.VMEM((8, r, c), F32), pltpu.SemaphoreType.DMA((7,)), pltpu.SemaphoreType.DMA((7,))],
    )(v)


_IN_SEGMENTS = ((0, 1544, 128), (128, 1800, 128), (256, 2056, 128), (384, 1672, 128), (512, 1928, 128), (640, 2184, 128),
                (OFF_Z, 0, SSD_WIDTH), (OFF_DT, 1536, SSD_HEADS), (OFF_XBC, 512, CONV_DIM), (OFF_P, 2312, POOL_WIDTH))


def _in_column_map():
    m = np.full((D_INP,), -1, np.int64)
    for at, orig, n in _IN_SEGMENTS:
        cols = np.arange(orig, orig + n)
        m[at:at + n] = (cols // COLS_IN) * SHARD_IN + cols % COLS_IN
    return m


def take_rows(a, idx, name):
    dep, r_in, c = a.shape
    blk = LANES
    n_out, n_in = len(idx) // blk, r_in // blk
    assert len(idx) % blk == 0 and r_in % blk == 0
    sources = [sorted({int(v) // blk for v in idx[blk * i:blk * (i + 1)] if v >= 0}) for i in range(n_out)]
    width = max(len(s) for s in sources)
    table = np.zeros((n_out, width), np.int32)
    for i, s in enumerate(sources):
        spare = [b for b in range(n_in) if b not in s][:width - len(s)]
        table[i] = s + spare

    def body(tbl_ref, idx_ref, *refs):
        in_refs, o_ref = refs[:width], refs[width]
        i = pl.program_id(1)
        src = idx_ref[...]
        acc = jnp.zeros((blk, c), F32)
        for k in range(width):
            pick = (src == tbl_ref[i, k] * blk + _iota2((blk, blk), 1)).astype(BF16)
            acc = acc + _nn(pick, in_refs[k][0])
        o_ref[0] = acc.astype(BF16)

    return pl.pallas_call(
        body, name=name,
        grid_spec=pltpu.PrefetchScalarGridSpec(
            num_scalar_prefetch=1, grid=(dep, n_out),
            in_specs=[pl.BlockSpec((blk, 1), lambda l, i, t: (i, 0))] +
                     [pl.BlockSpec((1, blk, c), lambda l, i, t, k=k: (l, t[i, k], 0)) for k in range(width)],
            out_specs=pl.BlockSpec((1, blk, c), lambda l, i, t: (l, i, 0))),
        out_shape=jax.ShapeDtypeStruct((dep, len(idx), c), BF16),
        compiler_params=_params(("parallel", "parallel")),
    )(jnp.asarray(table), jnp.asarray(np.asarray(idx, np.int32).reshape(-1, 1)), *([a] * width))


def _in_weight_layout(staged):
    return take_rows(staged, _in_column_map(), "w_in_layout")


def _in_gradient_layout(dwt):
    fwd = _in_column_map()
    inv = np.full((N_CHIPS * SHARD_IN,), -1, np.int64)
    inv[fwd[fwd >= 0]] = np.nonzero(fwd >= 0)[0]
    return take_rows(dwt, inv, "dw_in_layout")


SMALL_NAMES = ("norm1_w", "conv_w", "conv_b", "dt_bias", "a_log", "d_skip", "ssd_norm_w", "pool_w", "pool_b",
               "pool_scale", "norm2_w", "final_norm_w")
SMALL_ROWS = 104


def _pack_small(parts):
    flat = jnp.concatenate([p.reshape(-1) for p in parts])
    return jnp.pad(flat, (0, SMALL_ROWS * D_MODEL - flat.shape[0])).reshape(SMALL_ROWS, D_MODEL)


def _unpack_small(flat, shapes):
    flat = flat.reshape(-1)
    out, at = [], 0
    for shp in shapes:
        n = int(np.prod(shp))
        out.append(flat[at:at + n].reshape(shp))
        at += n
    return out


def kernel(x, norm1_w, w_in, conv_w, conv_b, dt_bias, a_log, d_skip, ssd_norm_w, pool_w, pool_b, pool_scale, w_out, norm2_w, w_gate, w_up, w_down, final_norm_w, loss_target, m_norm1_w, m_w_in, m_conv_w, m_conv_b, m_dt_bias, m_a_log, m_d_skip, m_ssd_norm_w, m_pool_w, m_pool_b, m_pool_scale, m_w_out, m_norm2_w, m_w_gate, m_w_up, m_w_down, m_final_norm_w, v_norm1_w, v_w_in, v_conv_w, v_conv_b, v_dt_bias, v_a_log, v_d_skip, v_ssd_norm_w, v_pool_w, v_pool_b, v_pool_scale, v_w_out, v_norm2_w, v_w_gate, v_w_up, v_w_down, v_final_norm_w):
    px, py, pc = _position()
    chip = 2 * px + py
    chip_arr = jnp.reshape(chip, (1,)).astype(jnp.int32)
    half_arr = jnp.reshape(pc, (1,)).astype(jnp.int32)

    def layer_shards(l):
        w_in_t = jnp.pad(jnp.swapaxes(w_in[l], 0, 1).astype(BF16), ((0, SHARD_IN - COLS_IN), (0, 0)))
        return [w_in_t, w_out[l].astype(BF16), jnp.swapaxes(w_gate[l], 0, 1).astype(BF16),
                jnp.swapaxes(w_up[l], 0, 1).astype(BF16), w_down[l].astype(BF16)]

    over_ici = {0: gather_ici_start(layer_shards(0))}

    def pass_on(l, after):
        own, arrived = exchange_wait("gather_ici_wait", over_ici[l], after, _gather_ici_plan)
        swap = gather_d2d_start(own, arrived)
        tokens = [swap[4]]
        if l + 1 < DEPTH:
            over_ici[l + 1] = gather_ici_start(layer_shards(l + 1), after=swap[4])
            tokens.append(over_ici[l + 1][4])
        return swap, tokens

    def weights_of(swap, after):
        _, (w_in_st, w_out_l, w_gate_t, w_up_t, w_down_l) = exchange_wait("gather_d2d_wait", swap, after, _gather_d2d_plan)
        return _in_weight_layout(w_in_st[None])[0], w_out_l, w_gate_t, w_up_t, w_down_l

    pad_heads = lambda v: jnp.pad(v, ((0, 0), (0, LANES - SSD_HEADS)))[:, None, :]
    dtb, alog = pad_heads(dt_bias), pad_heads(a_log)
    dskip_x = jnp.repeat(d_skip, HEAD_DIM, axis=1)[:, None, :]
    eye = jnp.eye(len(POOL_WINDOWS), dtype=F32)
    wbd = (pool_w[:, :, :, None, :] * eye[None, :, None, :, None]).reshape(DEPTH, POOL_WIDTH, POOL_WIDTH).astype(BF16)
    pool_b2 = pool_b.reshape(DEPTH, 1, POOL_WIDTH)
    cw_cols = lax.dynamic_update_slice(jnp.zeros((DEPTH, CONV_WIDTH, CONV_DIM), F32), conv_w,
                                       (0, 0, chip * (CONV_DIM // N_CHIPS)))
    cw_cols = jnp.where(pc == 0, cw_cols, 0.0)
    cw_rows = (DEPTH * CONV_WIDTH * CONV_DIM) // D_MODEL
    conv_w_f = small_allreduce(jnp.pad(cw_cols.reshape(cw_rows, D_MODEL), ((0, 8), (0, 0))), "gather_conv_w")
    conv_w_f = conv_w_f[:cw_rows].reshape(DEPTH, CONV_WIDTH, CONV_DIM)
    cw8 = jnp.pad(conv_w_f, ((0, 0), (0, 8 - CONV_WIDTH), (0, 0)))

    h = x[0]
    saved, weights = [], []
    swap, tokens = pass_on(0, over_ici[0][4])
    weights.append(weights_of(swap, tokens))
    for l in range(DEPTH):
        w_in_f, w_out_f, w_gate_t, w_up_t, w_down_f = weights[l]
        proj = rms_matmul(h, norm1_w[l][None], w_in_f, "in_proj")
        xc = conv_forward(proj, cw8[l], conv_b[l][None], "conv_fwd")
        y_all, ycore, states = ssd_forward(proj, xc, dtb[l], alog[l], dskip_x[l], ssd_norm_w[l][None], "ssd_fwd")
        y_all, o_sb = sb_forward(proj, y_all, "sb_fwd")
        swap, tokens = pass_on(l + 1, o_sb) if l + 1 < DEPTH else (None, None)
        y_all = pool_forward(proj, wbd[l], pool_b2[l], pool_scale[l][None], y_all, "pool_fwd", after=tokens)
        x1 = matmul_residual(y_all, w_out_f, h, "out_proj")
        x2, g, u = ffn_forward(x1, norm2_w[l][None], w_gate_t, w_up_t, w_down_f, "ffn_fwd")
        if swap is not None:
            weights.append(weights_of(swap, x2))
        saved.append((h, proj, xc, ycore, states, o_sb, y_all, x1, g, u))
        h = x2

    loss_part, dx, dxb, d_final = loss_head(h, final_norm_w[None], loss_target[0], "loss_head")
    loss = lax.psum(loss_part[0, 0], ("x", "y", "c"))

    small = {n: [None] * DEPTH for n in SMALL_NAMES if n != "final_norm_w"}
    chip_half = jnp.concatenate([chip_arr, half_arr])
    reduced = [None] * DEPTH
    d2d = ici = None

    def add_cores(d2d, after):
        mine, theirs = exchange_wait("reduce_d2d_wait", d2d[1], after, _reduce_d2d_plan)
        return d2d[0], reduce_ici_start([add_halves(d, t, half_arr, "reduce_add_halves") for d, t in zip(mine, theirs)])

    def add_all(ici, after):
        sums, theirs = exchange_wait("reduce_ici_wait", ici[1], after, _reduce_ici_plan)
        return ici[0], reduce_swap_start([add_chips(p, t, chip_half, "reduce_add_chips") for p, t in zip(sums, theirs)])

    def finish(swap, after):
        reduced[swap[0]] = exchange_wait("reduce_swap_wait", swap[1], after, _reduce_swap_plan)[1]

    swaps = []
    for l in reversed(range(DEPTH)):
        xin, proj, xc, ycore, states, o_sb, y_all, x1, g, u = saved[l]
        w_in_f, w_out_f, w_gate_t, w_up_t, w_down_f = weights[l]
        dg, du, act = ffn_backward_act(dxb, g, u, w_down_f, "ffn_bwd_act", after=None if d2d is None else d2d[1][4])
        dx1, dx1b, h2, dn2 = rms_backward([dg, du], [w_gate_t, w_up_t], x1, norm2_w[l][None], dx, "ffn_bwd_norm", 256)
        if d2d is not None:
            ici = add_cores(d2d, dx1b)
        dyall = matmul_nt(dx1b, w_out_f, "out_proj_bwd", after=None if ici is None else ici[1][4])
        dw_down = matmul_tn(act, dxb, "dw_down")
        dw_gate = matmul_tn(dg, h2, "dw_gate")
        dw_up = matmul_tn(du, h2, "dw_up")
        dw_out = matmul_tn(y_all, dx1b, "dw_out")
        dxc, dproj, dsn, ddsk, ddtb, dalog = ssd_backward(proj, xc, ycore, dyall, states, dtb[l], alog[l],
                                                          dskip_x[l], ssd_norm_w[l][None], "ssd_bwd")
        dproj, dcw, dcb = conv_backward(proj, dxc, cw8[l], conv_b[l][None], dproj, "conv_bwd")
        dproj = sb_backward(proj, o_sb, dyall, dproj, "sb_bwd")
        dproj, dwbd, dpb, dps = pool_backward(proj, dyall, wbd[l], pool_b2[l], pool_scale[l][None], dproj, "pool_bwd")
        if ici is not None:
            swaps.append(add_all(ici, dproj))
            ici = None
        dx, dxb, h1, dn1 = rms_backward([dproj], [w_in_f], xin, norm1_w[l][None], dx1, "in_proj_bwd", 256,
                                        after=swaps[-1][1][4] if swaps else None)
        dw_in = _in_gradient_layout(matmul_tn(dproj, h1, "dw_in")[None])[0]
        d2d = (l, reduce_d2d_start([dw.reshape(N_CHIPS, r, D_MODEL) for dw, r in
                                    zip((dw_in, dw_out, dw_gate, dw_up, dw_down), SHARD_ROWS)]))
        small["norm1_w"][l] = dn1[0]
        small["conv_w"][l] = dcw[:CONV_WIDTH]
        small["conv_b"][l] = dcb[0]
        small["dt_bias"][l] = ddtb[0, :SSD_HEADS]
        small["a_log"][l] = dalog[0, :SSD_HEADS]
        small["d_skip"][l] = ddsk.reshape(SSD_HEADS, HEAD_DIM).sum(axis=1)
        small["ssd_norm_w"][l] = dsn[0]
        small["pool_w"][l] = jnp.stack([dwbd[64 * k:64 * k + 64, 64 * k:64 * k + 64] for k in range(len(POOL_WINDOWS))])
        small["pool_b"][l] = dpb.reshape(len(POOL_WINDOWS), -1)
        small["pool_scale"][l] = dps[0]
        small["norm2_w"][l] = dn2[0]
    grad_x = dx[None]

    ici = add_cores(d2d, d2d[1][4])
    swaps.append(add_all(ici, ici[1][4]))
    for swap in swaps:
        finish(swap, swaps[-1][1][4])
    g_big = {n: [reduced[l][k] for l in range(DEPTH)] for k, n in enumerate(("w_in", "w_out", "w_gate", "w_up", "w_down"))}
    g_big["w_in"] = [gl[:COLS_IN] for gl in g_big["w_in"]]
    transposed = ("w_in", "w_gate", "w_up")

    small_parts = [d_final if n == "final_norm_w" else jnp.stack(small[n]) for n in SMALL_NAMES]
    small_shapes = [p.shape for p in small_parts]
    g_small = dict(zip(SMALL_NAMES, _unpack_small(small_allreduce(_pack_small(small_parts), "reduce_small"), small_shapes)))
    g_small["final_norm_w"] = g_small["final_norm_w"].reshape(final_norm_w.shape)
    g_small["conv_w"] = lax.dynamic_slice_in_dim(g_small["conv_w"], chip * (CONV_DIM // N_CHIPS), CONV_DIM // N_CHIPS, axis=2)

    given = dict(norm1_w=(norm1_w, m_norm1_w, v_norm1_w), w_in=(w_in, m_w_in, v_w_in), conv_w=(conv_w, m_conv_w, v_conv_w),
                 conv_b=(conv_b, m_conv_b, v_conv_b), dt_bias=(dt_bias, m_dt_bias, v_dt_bias), a_log=(a_log, m_a_log, v_a_log),
                 d_skip=(d_skip, m_d_skip, v_d_skip), ssd_norm_w=(ssd_norm_w, m_ssd_norm_w, v_ssd_norm_w),
                 pool_w=(pool_w, m_pool_w, v_pool_w), pool_b=(pool_b, m_pool_b, v_pool_b),
                 pool_scale=(pool_scale, m_pool_scale, v_pool_scale), w_out=(w_out, m_w_out, v_w_out),
                 norm2_w=(norm2_w, m_norm2_w, v_norm2_w), w_gate=(w_gate, m_w_gate, v_w_gate), w_up=(w_up, m_w_up, v_w_up),
                 w_down=(w_down, m_w_down, v_w_down), final_norm_w=(final_norm_w, m_final_norm_w, v_final_norm_w))
    order = ("norm1_w", "w_in", "conv_w", "conv_b", "dt_bias", "a_log", "d_skip", "ssd_norm_w", "pool_w", "pool_b",
             "pool_scale", "w_out", "norm2_w", "w_gate", "w_up", "w_down", "final_norm_w")
    grads = dict(g_small)
    results = {}
    for n in ("w_in", "w_out", "w_gate", "w_up", "w_down"):
        w, m, v = given[n]
        if n in transposed:
            out = adamw_layers(jnp.swapaxes(w, 1, 2), g_big[n], jnp.swapaxes(m, 1, 2), jnp.swapaxes(v, 1, 2), "adamw_" + n)
            out = [jnp.swapaxes(o, 1, 2) for o in out]
        else:
            out = adamw_layers(w, g_big[n], m, v, "adamw_" + n)
        grads[n], results[n] = out[0], tuple(out[1:])
    small_shapes = [given[n][0].shape for n in SMALL_NAMES]
    packed = [_pack_small([given[n][k] for n in SMALL_NAMES])[None] for k in range(3)]
    packed_g = _pack_small([grads[n] for n in SMALL_NAMES])[None]
    small_out = adamw(packed[0], packed_g, packed[1], packed[2], "adamw_small")
    small_out = [_unpack_small(o[0], small_shapes) for o in small_out]
    for i, n in enumerate(SMALL_NAMES):
        results[n] = tuple(small_out[k][i] for k in range(3))

    return (loss, grad_x, *[grads[n] for n in order], *[results[n][0] for n in order],
            *[results[n][1] for n in order], *[results[n][2] for n in order])
```

```python
import numpy as np
import jax
import jax.numpy as jnp
from jax import lax
from jax.experimental import pallas as pl
from jax.experimental.pallas import tpu as pltpu

F32 = jnp.float32
BF16 = jnp.bfloat16
MESH = pl.DeviceIdType.MESH
ANY = pl.BlockSpec(memory_space=pl.ANY)

D_MODEL = 1024
DEPTH = 4
EPS = 1e-6
SSD_WIDTH = 512
SSD_HEADS = 8
HEAD_DIM = 64
D_STATE = 128
CHUNK = 128
CONV_WIDTH = 4
CONV_DIM = 1024
SB_WIDTH = 256
POOL_WIDTH = 256
POOL_WINDOWS = (2, 4, 8, 16)
D_FF = 2816
D_IN = 2568
N_CHIPS = 4
OFF_QKV, OFF_Z, OFF_DT, OFF_XBC, OFF_P = 0, 768, 1280, 1536, 2560
D_INP = 2816
ZDT = 768
SHARD_IN, SHARD_OUT, SHARD_FF = 672, 256, 704
COLS_IN = 642
ADAM_LR, ADAM_B1, ADAM_B2, ADAM_EPS, ADAM_WD, ADAM_STEP = 0.001, 0.9, 0.999, 1e-08, 0.01, 10
LANES = 128
VMEM_LIMIT = 56 * 1024 * 1024


def _params(sem=None):
    return pltpu.CompilerParams(dimension_semantics=sem, vmem_limit_bytes=VMEM_LIMIT)


def _tile(n, cap):
    best = None
    for t in range(LANES, min(n, cap) + 1, LANES):
        if n % t == 0:
            best = t
    assert best is not None, (n, cap)
    return best


def _nt(a, b):
    return lax.dot_general(a, b, (((1,), (1,)), ((), ())), preferred_element_type=F32)


def _tn(a, b):
    return lax.dot_general(a, b, (((0,), (0,)), ((), ())), preferred_element_type=F32)


def _nn(a, b):
    return jnp.dot(a, b, preferred_element_type=F32)


def _split_dot(a, b_exact, terms=3, dot=_nn):
    acc = None
    rest = a
    for _ in range(terms):
        hi = rest.astype(BF16)
        part = dot(hi, b_exact)
        acc = part if acc is None else acc + part
        rest = rest - hi.astype(F32)
    return acc


def _split_dot_left(a_exact, b, terms=3):
    acc = None
    rest = b
    for _ in range(terms):
        hi = rest.astype(BF16)
        part = _nn(a_exact, hi)
        acc = part if acc is None else acc + part
        rest = rest - hi.astype(F32)
    return acc


def _sigmoid(x):
    return 1.0 / (1.0 + jnp.exp(-x))


def _softplus(x):
    return jnp.maximum(x, 0.0) + jnp.log(1.0 + jnp.exp(-jnp.abs(x)))


def _iota2(shape, dim):
    return lax.broadcasted_iota(jnp.int32, shape, dim)


def _after(after):
    ops = [] if after is None else list(after) if isinstance(after, (list, tuple)) else [after]
    return [ANY] * len(ops), ops


def rms_matmul(x, nw, wt, name, after=None):
    s, d = x.shape
    n = wt.shape[0]
    tm, tn = _tile(s, 512), _tile(n, 2816)
    specs, ops = _after(after)

    def body(x_ref, nw_ref, w_ref, *rest):
        o_ref, h_ref = rest[len(ops):]

        @pl.when(pl.program_id(1) == 0)
        def _():
            xv = x_ref[...]
            r = lax.rsqrt(jnp.mean(xv * xv, axis=-1, keepdims=True) + EPS)
            h_ref[...] = (xv * r * nw_ref[...]).astype(BF16)
        o_ref[...] = _nt(h_ref[...], w_ref[...])

    return pl.pallas_call(
        body, name=name, grid=(s // tm, n // tn),
        in_specs=[pl.BlockSpec((tm, d), lambda i, j: (i, 0)), pl.BlockSpec((1, d), lambda i, j: (0, 0)),
                  pl.BlockSpec((tn, d), lambda i, j: (j, 0))] + specs,
        out_specs=pl.BlockSpec((tm, tn), lambda i, j: (i, j)),
        out_shape=jax.ShapeDtypeStruct((s, n), F32),
        scratch_shapes=[pltpu.VMEM((tm, d), BF16)],
        compiler_params=_params(("parallel", "arbitrary")),
    )(x, nw, wt, *ops)


def matmul_residual(a, w, res, name):
    s, k = a.shape
    n = w.shape[1]
    tm, tn = _tile(s, 512), _tile(n, 512)

    def body(a_ref, w_ref, r_ref, o_ref):
        o_ref[...] = r_ref[...] + _nn(a_ref[...], w_ref[...])

    return pl.pallas_call(
        body, name=name, grid=(s // tm, n // tn),
        in_specs=[pl.BlockSpec((tm, k), lambda i, j: (i, 0)), pl.BlockSpec((k, tn), lambda i, j: (0, j)),
                  pl.BlockSpec((tm, tn), lambda i, j: (i, j))],
        out_specs=pl.BlockSpec((tm, tn), lambda i, j: (i, j)),
        out_shape=jax.ShapeDtypeStruct((s, n), F32),
        compiler_params=_params(("parallel", "parallel")),
    )(a, w, res)


def matmul_nt(a, w, name, out_dtype=F32, after=None):
    s, n = a.shape
    k = w.shape[0]
    tm, tk = _tile(s, 512), _tile(k, 512)
    specs, ops = _after(after)

    def body(a_ref, w_ref, *rest):
        rest[-1][...] = _nt(a_ref[...], w_ref[...]).astype(out_dtype)

    return pl.pallas_call(
        body, name=name, grid=(s // tm, k // tk),
        in_specs=[pl.BlockSpec((tm, n), lambda i, j: (i, 0)), pl.BlockSpec((tk, n), lambda i, j: (j, 0))] + specs,
        out_specs=pl.BlockSpec((tm, tk), lambda i, j: (i, j)),
        out_shape=jax.ShapeDtypeStruct((s, k), out_dtype),
        compiler_params=_params(("parallel", "parallel")),
    )(a, w, *ops)


def matmul_tn(a, b, name, after=None):
    s, m = a.shape
    n = b.shape[1]
    tm, tn = _tile(m, 512), _tile(n, 1024)

    def body(a_ref, b_ref, *rest):
        rest[-1][...] = _tn(a_ref[...], b_ref[...]).astype(BF16)

    specs, ops = _after(after)
    return pl.pallas_call(
        body, name=name, grid=(m // tm, n // tn),
        in_specs=[pl.BlockSpec((s, tm), lambda i, j: (0, i)), pl.BlockSpec((s, tn), lambda i, j: (0, j))] + specs,
        out_specs=pl.BlockSpec((tm, tn), lambda i, j: (i, j)),
        out_shape=jax.ShapeDtypeStruct((m, n), BF16),
        compiler_params=_params(("parallel", "parallel")),
    )(a, b, *ops)


def ffn_forward(x1, nw, wgt, wut, wd, name):
    s, d = x1.shape
    f = wgt.shape[0]
    tm, tf = _tile(s, 1024), _tile(f, 256)

    def body(x_ref, nw_ref, wg_ref, wu_ref, wd_ref, o_ref, g_ref, u_ref, h_ref, acc_ref):
        j = pl.program_id(1)

        @pl.when(j == 0)
        def _():
            xv = x_ref[...]
            r = lax.rsqrt(jnp.mean(xv * xv, axis=-1, keepdims=True) + EPS)
            h_ref[...] = (xv * r * nw_ref[...]).astype(BF16)
            acc_ref[...] = xv

        h = h_ref[...]
        g = _nt(h, wg_ref[...])
        u = _nt(h, wu_ref[...])
        g_ref[...] = g.astype(BF16)
        u_ref[...] = u.astype(BF16)
        a = (g * _sigmoid(g) * u).astype(BF16)
        acc_ref[...] += _nn(a, wd_ref[...])

        @pl.when(j == pl.num_programs(1) - 1)
        def _():
            o_ref[...] = acc_ref[...]

    wblk = pl.BlockSpec((tf, d), lambda i, j: (j, 0))
    return pl.pallas_call(
        body, name=name, grid=(s // tm, f // tf),
        in_specs=[pl.BlockSpec((tm, d), lambda i, j: (i, 0)), pl.BlockSpec((1, d), lambda i, j: (0, 0)), wblk, wblk, wblk],
        out_specs=[pl.BlockSpec((tm, d), lambda i, j: (i, 0)), pl.BlockSpec((tm, tf), lambda i, j: (i, j)),
                   pl.BlockSpec((tm, tf), lambda i, j: (i, j))],
        out_shape=[jax.ShapeDtypeStruct((s, d), F32), jax.ShapeDtypeStruct((s, f), BF16),
                   jax.ShapeDtypeStruct((s, f), BF16)],
        scratch_shapes=[pltpu.VMEM((tm, d), BF16), pltpu.VMEM((tm, d), F32)],
        compiler_params=_params(("parallel", "arbitrary")),
    )(x1, nw, wgt, wut, wd)


def ffn_backward_act(dx2, g, u, wd, name, after=None):
    s, d = dx2.shape
    f = wd.shape[0]
    tm, tf = _tile(s, 256), _tile(f, 2816)
    specs, ops = _after(after)

    def body(dx_ref, g_ref, u_ref, wd_ref, *rest):
        dg_ref, du_ref, a_ref = rest[len(ops):]
        da = _nt(dx_ref[...], wd_ref[...])
        gv = g_ref[...].astype(F32)
        uv = u_ref[...].astype(F32)
        sg = _sigmoid(gv)
        silu = gv * sg
        dg_ref[...] = (da * uv * (sg * (1.0 + gv * (1.0 - sg)))).astype(BF16)
        du_ref[...] = (da * silu).astype(BF16)
        a_ref[...] = (silu * uv).astype(BF16)

    blk = pl.BlockSpec((tm, tf), lambda i, j: (i, j))
    return pl.pallas_call(
        body, name=name, grid=(s // tm, f // tf),
        in_specs=[pl.BlockSpec((tm, d), lambda i, j: (i, 0)), blk, blk, pl.BlockSpec((tf, d), lambda i, j: (j, 0))] + specs,
        out_specs=[blk, blk, blk],
        out_shape=[jax.ShapeDtypeStruct((s, f), BF16)] * 3,
        compiler_params=_params(("parallel", "parallel")),
    )(dx2, g, u, wd, *ops)


def rms_backward(dzs, wts, x, nw, dres, name, tm, after=None):
    s, d = x.shape
    nz = len(dzs)
    specs, ops = _after(after)

    def body(*refs):
        dz_refs, w_refs = refs[:nz], refs[nz:2 * nz]
        x_ref, nw_ref, dres_ref = refs[2 * nz:2 * nz + 3]
        dx_ref, dxb_ref, h_ref, dnw_ref = refs[2 * nz + 3 + len(ops):]
        dh = _nn(dz_refs[0][...], w_refs[0][...])
        for k in range(1, nz):
            dh = dh + _nn(dz_refs[k][...], w_refs[k][...])
        xv = x_ref[...]
        r = lax.rsqrt(jnp.mean(xv * xv, axis=-1, keepdims=True) + EPS)
        xhat = xv * r
        nwv = nw_ref[...]
        h_ref[...] = (xhat * nwv).astype(BF16)

        @pl.when(pl.program_id(0) == 0)
        def _():
            dnw_ref[...] = jnp.zeros_like(dnw_ref)

        dnw_ref[...] += jnp.sum(dh * xhat, axis=0, keepdims=True)
        gdh = dh * nwv
        dx = dres_ref[...] + r * (gdh - xhat * jnp.mean(gdh * xhat, axis=-1, keepdims=True))
        dx_ref[...] = dx
        dxb_ref[...] = dx.astype(BF16)

    row = pl.BlockSpec((tm, d), lambda i: (i, 0))
    in_specs = [pl.BlockSpec((tm, dz.shape[1]), lambda i: (i, 0)) for dz in dzs]
    in_specs += [pl.BlockSpec(w.shape, lambda i: (0, 0)) for w in wts]
    in_specs += [row, pl.BlockSpec((1, d), lambda i: (0, 0)), row] + specs
    return pl.pallas_call(
        body, name=name, grid=(s // tm,),
        in_specs=in_specs,
        out_specs=[row, row, row, pl.BlockSpec((1, d), lambda i: (0, 0))],
        out_shape=[jax.ShapeDtypeStruct((s, d), F32), jax.ShapeDtypeStruct((s, d), BF16),
                   jax.ShapeDtypeStruct((s, d), BF16), jax.ShapeDtypeStruct((1, d), F32)],
        compiler_params=_params(("arbitrary",)),
    )(*dzs, *wts, x, nw, dres, *ops)


def loss_head(x, nw, target, name):
    s, d = x.shape
    tm = _tile(s, 512)

    def body(x_ref, nw_ref, t_ref, loss_ref, dx_ref, dxb_ref, dnw_ref):
        xv = x_ref[...]
        r = lax.rsqrt(jnp.mean(xv * xv, axis=-1, keepdims=True) + EPS)
        xhat = xv * r
        nwv = nw_ref[...]
        err = xhat * nwv - t_ref[...]

        @pl.when(pl.program_id(0) == 0)
        def _():
            dnw_ref[...] = jnp.zeros_like(dnw_ref)
            loss_ref[...] = jnp.zeros_like(loss_ref)

        part = jnp.sum(jnp.sum(err * err, axis=-1, keepdims=True), axis=0, keepdims=True) * (0.5 / d)
        loss_ref[...] += jnp.broadcast_to(part, loss_ref.shape)
        dout = err * (1.0 / d)
        dnw_ref[...] += jnp.sum(dout * xhat, axis=0, keepdims=True)
        gdh = dout * nwv
        dx = r * (gdh - xhat * jnp.mean(gdh * xhat, axis=-1, keepdims=True))
        dx_ref[...] = dx
        dxb_ref[...] = dx.astype(BF16)

    row = pl.BlockSpec((tm, d), lambda i: (i, 0))
    return pl.pallas_call(
        body, name=name, grid=(s // tm,),
        in_specs=[row, pl.BlockSpec((1, d), lambda i: (0, 0)), row],
        out_specs=[pl.BlockSpec((1, LANES), lambda i: (0, 0)), row, row, pl.BlockSpec((1, d), lambda i: (0, 0))],
        out_shape=[jax.ShapeDtypeStruct((1, LANES), F32), jax.ShapeDtypeStruct((s, d), F32),
                   jax.ShapeDtypeStruct((s, d), BF16), jax.ShapeDtypeStruct((1, d), F32)],
        compiler_params=_params(("arbitrary",)),
    )(x, nw, target)


def _shift_down(x, k):
    return jnp.where(_iota2(x.shape, 0) >= k, pltpu.roll(x, k, axis=0), 0.0)


def _shift_up(x, k):
    s = x.shape[0]
    return jnp.where(_iota2(x.shape, 0) < s - k, pltpu.roll(x, s - k, axis=0), 0.0)


CONV_TILE = 256


def conv_forward(proj, cw, cb, name):
    s = proj.shape[0]
    tn = CONV_TILE
    off = OFF_XBC // tn

    def body(u_ref, w_ref, b_ref, o_ref):
        u = u_ref[...]
        pre = b_ref[...] + w_ref[CONV_WIDTH - 1:CONV_WIDTH, :] * u
        for i in range(CONV_WIDTH - 1):
            pre = pre + w_ref[i:i + 1, :] * _shift_down(u, CONV_WIDTH - 1 - i)
        o_ref[...] = pre * _sigmoid(pre)

    return pl.pallas_call(
        body, name=name, grid=(CONV_DIM // tn,),
        in_specs=[pl.BlockSpec((s, tn), lambda j: (0, off + j)), pl.BlockSpec((8, tn), lambda j: (0, j)),
                  pl.BlockSpec((1, tn), lambda j: (0, j))],
        out_specs=pl.BlockSpec((s, tn), lambda j: (0, j)),
        out_shape=jax.ShapeDtypeStruct((s, CONV_DIM), F32),
        compiler_params=_params(("parallel",)),
    )(proj, cw, cb)


def conv_backward(proj, dxc, cw, cb, dproj, name):
    s = proj.shape[0]
    tn = CONV_TILE
    off = OFF_XBC // tn

    def body(u_ref, d_ref, w_ref, b_ref, _, du_ref, dw_ref, db_ref):
        u = u_ref[...]
        shifted = [_shift_down(u, CONV_WIDTH - 1 - i) for i in range(CONV_WIDTH - 1)] + [u]
        pre = b_ref[...] + w_ref[CONV_WIDTH - 1:CONV_WIDTH, :] * u
        for i in range(CONV_WIDTH - 1):
            pre = pre + w_ref[i:i + 1, :] * shifted[i]
        sg = _sigmoid(pre)
        dpre = d_ref[...] * (sg * (1.0 + pre * (1.0 - sg)))
        du = w_ref[CONV_WIDTH - 1:CONV_WIDTH, :] * dpre
        for i in range(CONV_WIDTH - 1):
            du = du + w_ref[i:i + 1, :] * _shift_up(dpre, CONV_WIDTH - 1 - i)
        du_ref[...] = du.astype(BF16)
        rows = [jnp.sum(dpre * shifted[i], axis=0, keepdims=True) for i in range(CONV_WIDTH)]
        rows.append(jnp.zeros((8 - CONV_WIDTH, tn), F32))
        dw_ref[...] = jnp.concatenate(rows, axis=0)
        db_ref[...] = jnp.sum(dpre, axis=0, keepdims=True)

    return pl.pallas_call(
        body, name=name, grid=(CONV_DIM // tn,),
        in_specs=[pl.BlockSpec((s, tn), lambda j: (0, off + j)), pl.BlockSpec((s, tn), lambda j: (0, j)),
                  pl.BlockSpec((8, tn), lambda j: (0, j)), pl.BlockSpec((1, tn), lambda j: (0, j)), ANY],
        out_specs=[pl.BlockSpec((s, tn), lambda j: (0, off + j)), pl.BlockSpec((8, tn), lambda j: (0, j)),
                   pl.BlockSpec((1, tn), lambda j: (0, j))],
        out_shape=[jax.ShapeDtypeStruct(dproj.shape, BF16), jax.ShapeDtypeStruct((8, CONV_DIM), F32),
                   jax.ShapeDtypeStruct((1, CONV_DIM), F32)],
        input_output_aliases={4: 0},
        compiler_params=_params(("parallel",)),
    )(proj, dxc, cw, cb, dproj)


def _pool_lane_window(shape):
    grp = _iota2(shape, 1) // (POOL_WIDTH // len(POOL_WINDOWS))
    win = jnp.full(shape, POOL_WINDOWS[-1], jnp.int32)
    for gi in range(len(POOL_WINDOWS) - 2, -1, -1):
        win = jnp.where(grp == gi, POOL_WINDOWS[gi], win)
    return grp, win


def _pool_select(grp, sums):
    out = sums[-1]
    for gi in range(len(sums) - 2, -1, -1):
        out = jnp.where(grp == gi, sums[gi], out)
    return out


def _pool_pooled(p):
    grp, win = _pool_lane_window(p.shape)
    inv_count = 1.0 / jnp.minimum(_iota2(p.shape, 0) + 1, win).astype(F32)
    sums, acc, k = [], p, 1
    for _ in POOL_WINDOWS:
        acc = acc + _shift_down(acc, k)
        sums.append(acc)
        k *= 2
    return _pool_select(grp, sums) * inv_count - p, grp, inv_count


def pool_forward(proj, wbd, pb, ps, y_all, name, after=None):
    s = proj.shape[0]
    specs, ops = _after(after)

    def body(p_ref, w_ref, b_ref, s_ref, *rest):
        o_ref = rest[-1]
        pooled, _, _ = _pool_pooled(p_ref[...])
        mixed = _nn(pooled.astype(BF16), w_ref[...]) + b_ref[...]
        o_ref[...] = (mixed * s_ref[...]).astype(BF16)

    vec = pl.BlockSpec((1, POOL_WIDTH), lambda j: (0, 0))
    return pl.pallas_call(
        body, name=name, grid=(1,),
        in_specs=[pl.BlockSpec((s, POOL_WIDTH), lambda j: (0, OFF_P // POOL_WIDTH)),
                  pl.BlockSpec((POOL_WIDTH, POOL_WIDTH), lambda j: (0, 0)), vec, vec, ANY] + specs,
        out_specs=pl.BlockSpec((s, POOL_WIDTH), lambda j: (0, (SSD_WIDTH + SB_WIDTH) // POOL_WIDTH)),
        out_shape=jax.ShapeDtypeStruct(y_all.shape, BF16),
        input_output_aliases={4: 0},
        compiler_params=_params(("arbitrary",)),
    )(proj, wbd, pb, ps, y_all, *ops)


def pool_backward(proj, dyall, wbd, pb, ps, dproj, name):
    s = proj.shape[0]

    def body(p_ref, dy_ref, w_ref, b_ref, s_ref, _, dp_ref, dw_ref, db_ref, ds_ref):
        pooled, grp, inv_count = _pool_pooled(p_ref[...])
        pooled_b = pooled.astype(BF16)
        mixed = _nn(pooled_b, w_ref[...]) + b_ref[...]
        dy = dy_ref[...]
        ds_ref[...] = jnp.sum(dy * mixed, axis=0, keepdims=True)
        dmixed = dy * s_ref[...]
        db_ref[...] = jnp.sum(dmixed, axis=0, keepdims=True)
        dmixed_b = dmixed.astype(BF16)
        dw_ref[...] = _tn(pooled_b, dmixed_b)
        dpooled = _nt(dmixed_b, w_ref[...])
        sums, acc, k = [], dpooled * inv_count, 1
        for _ in POOL_WINDOWS:
            acc = acc + _shift_up(acc, k)
            sums.append(acc)
            k *= 2
        dp_ref[...] = (_pool_select(grp, sums) - dpooled).astype(BF16)

    vec = pl.BlockSpec((1, POOL_WIDTH), lambda j: (0, 0))
    mat = pl.BlockSpec((POOL_WIDTH, POOL_WIDTH), lambda j: (0, 0))
    pcol = pl.BlockSpec((s, POOL_WIDTH), lambda j: (0, OFF_P // POOL_WIDTH))
    return pl.pallas_call(
        body, name=name, grid=(1,),
        in_specs=[pcol, pl.BlockSpec((s, POOL_WIDTH), lambda j: (0, (SSD_WIDTH + SB_WIDTH) // POOL_WIDTH)), mat, vec, vec, ANY],
        out_specs=[pcol, mat, vec, vec],
        out_shape=[jax.ShapeDtypeStruct(dproj.shape, BF16), jax.ShapeDtypeStruct((POOL_WIDTH, POOL_WIDTH), F32),
                   jax.ShapeDtypeStruct((1, POOL_WIDTH), F32), jax.ShapeDtypeStruct((1, POOL_WIDTH), F32)],
        input_output_aliases={5: 0},
        compiler_params=_params(("arbitrary",)),
    )(proj, dyall, wbd, pb, ps, dproj)


N_PAIRS = SSD_HEADS // 2


def _ssd_common(xc, dtraw, dtb, alog):
    c = CHUNK
    dt = _softplus(dtraw + dtb)
    a = -jnp.exp(alog)
    ltri = (_iota2((c, c), 0) >= _iota2((c, c), 1)).astype(BF16)
    acum = _split_dot_left(ltri, dt * a)
    expand = (_iota2((c, SSD_WIDTH), 1) // HEAD_DIM == _iota2((c, SSD_WIDTH), 0)).astype(BF16)
    expand_wide = (_iota2((c, SSD_HEADS * c), 1) // c == _iota2((c, SSD_HEADS * c), 0)).astype(BF16)
    acum_x = _split_dot(acum, expand)
    dt_x = _split_dot(dt, expand)
    alast_x = acum_x[c - 1:c, :]
    return dict(dt=dt, a=a, acum=acum, acum_x=acum_x, dt_x=dt_x, ea_x=jnp.exp(acum_x),
                dte_x=jnp.exp(alast_x - acum_x), eal_x=jnp.exp(alast_x),
                acol=_split_dot(acum, expand_wide), acum_t=acum.T,
                xs=xc[:, :SSD_WIDTH], causal=_iota2((c, c), 0) >= _iota2((c, c), 1),
                left=_iota2((c, c), 1) < HEAD_DIM)


def _ssd_group(xc, g):
    b = xc[:, SSD_WIDTH + D_STATE * g:SSD_WIDTH + D_STATE * (g + 1)]
    cm = xc[:, SSD_WIDTH + 2 * D_STATE + D_STATE * g:SSD_WIDTH + 2 * D_STATE + D_STATE * (g + 1)]
    return b, cm


def _ssd_decay(q, hh):
    col = q["acol"][:, CHUNK * hh:CHUNK * (hh + 1)]
    row = q["acum_t"][hh:hh + 1, :]
    return jnp.where(q["causal"], jnp.exp(jnp.minimum(col - row, 0.0)), 0.0)


def ssd_forward(proj, xc, dtb, alog, dskip_x, nw, name):
    s = xc.shape[0]
    nc = s // CHUNK

    def body(xc_ref, zdt_ref, dtb_ref, alog_ref, dsk_ref, nw_ref, y_ref, yc_ref, st_ref, state):
        @pl.when(pl.program_id(0) == 0)
        def _():
            state[...] = jnp.zeros_like(state)

        xcv = xc_ref[...]
        q = _ssd_common(xcv, zdt_ref[:, SSD_WIDTH:SSD_WIDTH + LANES], dtb_ref[...], alog_ref[...])
        x = q["xs"] * q["dt_x"]
        xb = x.astype(BF16)
        xd = (x * q["dte_x"]).astype(BF16)
        pieces = []
        for g in range(2):
            bg, cg = _ssd_group(xcv, g)
            bgb, cgb = bg.astype(BF16), cg.astype(BF16)
            cb = _nt(cgb, bgb)
            bgt = bg.T.astype(BF16)
            for pr in (2 * g, 2 * g + 1):
                sl = slice(CHUNK * pr, CHUNK * (pr + 1))
                st = state[pr]
                st_ref[0, pr] = st
                yp = _nn(cgb, st.astype(BF16)) * q["ea_x"][:, sl]
                for k, hh in enumerate((2 * pr, 2 * pr + 1)):
                    w = (cb * _ssd_decay(q, hh)).astype(BF16)
                    mask = q["left"] if k == 0 else jnp.logical_not(q["left"])
                    yp = yp + _nn(w, jnp.where(mask, xb[:, sl], jnp.zeros_like(xb[:, sl])))
                state[pr] = st * q["eal_x"][:, sl] + _nn(bgt, xd[:, sl])
                pieces.append(yp)
        y = jnp.concatenate(pieces, axis=1) + q["xs"] * dsk_ref[...]
        yc_ref[...] = y
        zv = zdt_ref[:, :SSD_WIDTH]
        yg = y * (zv * _sigmoid(zv))
        r = lax.rsqrt(jnp.mean(yg * yg, axis=-1, keepdims=True) + EPS)
        y_ref[...] = (yg * r * nw_ref[...]).astype(BF16)

    vec = lambda n: pl.BlockSpec((1, n), lambda c: (0, 0))
    return pl.pallas_call(
        body, name=name, grid=(nc,),
        in_specs=[pl.BlockSpec((CHUNK, CONV_DIM), lambda c: (c, 0)),
                  pl.BlockSpec((CHUNK, ZDT), lambda c: (c, OFF_Z // ZDT)),
                  vec(LANES), vec(LANES), vec(SSD_WIDTH), vec(SSD_WIDTH)],
        out_specs=[pl.BlockSpec((CHUNK, SSD_WIDTH), lambda c: (c, 0)), pl.BlockSpec((CHUNK, SSD_WIDTH), lambda c: (c, 0)),
                   pl.BlockSpec((1, N_PAIRS, D_STATE, CHUNK), lambda c: (c, 0, 0, 0))],
        out_shape=[jax.ShapeDtypeStruct((s, D_MODEL), BF16), jax.ShapeDtypeStruct((s, SSD_WIDTH), F32),
                   jax.ShapeDtypeStruct((nc, N_PAIRS, D_STATE, CHUNK), F32)],
        scratch_shapes=[pltpu.VMEM((N_PAIRS, D_STATE, CHUNK), F32)],
        compiler_params=_params(("arbitrary",)),
    )(xc, proj, dtb, alog, dskip_x, nw)


def ssd_backward(proj, xc, ycore, dyall, states, dtb, alog, dskip_x, nw, name):
    s = xc.shape[0]
    nc = s // CHUNK
    c = CHUNK

    def body(xc_ref, zdt_ref, yc_ref, dy_ref, st_ref, dtb_ref, alog_ref, dsk_ref, nw_ref,
             dxc_ref, dzdt_ref, dnw_ref, ddsk_ref, ddtb_ref, dalog_ref, dstate):
        @pl.when(pl.program_id(0) == 0)
        def _():
            dstate[...] = jnp.zeros_like(dstate)
            dnw_ref[...] = jnp.zeros_like(dnw_ref)
            ddsk_ref[...] = jnp.zeros_like(ddsk_ref)
            ddtb_ref[...] = jnp.zeros_like(ddtb_ref)
            dalog_ref[...] = jnp.zeros_like(dalog_ref)

        xcv = xc_ref[...]
        dtraw = zdt_ref[:, SSD_WIDTH:SSD_WIDTH + LANES]
        q = _ssd_common(xcv, dtraw, dtb_ref[...], alog_ref[...])
        xs = q["xs"]
        x = xs * q["dt_x"]
        zv, yc, dy, nwv = zdt_ref[:, :SSD_WIDTH], yc_ref[...], dy_ref[...], nw_ref[...]
        sgz = _sigmoid(zv)
        siluz = zv * sgz
        yg = yc * siluz
        r = lax.rsqrt(jnp.mean(yg * yg, axis=-1, keepdims=True) + EPS)
        dnw_ref[...] += jnp.sum(dy * yg * r, axis=0, keepdims=True)
        g1 = dy * nwv
        dyg = r * (g1 - yg * (r * r) * jnp.mean(g1 * yg, axis=-1, keepdims=True))
        dyv = dyg * siluz
        dz = (dyg * yc * (sgz * (1.0 + zv * (1.0 - sgz)))).astype(BF16)
        ddsk_ref[...] += jnp.sum(dyv * xs, axis=0, keepdims=True)
        dye = dyv * q["ea_x"]
        dx_parts, yoff_parts, u_parts, v_parts, e_parts = [], [], [], [], []
        db_parts, dc_parts = [], []
        for g in range(2):
            bg, cg = _ssd_group(xcv, g)
            bgb, cgb = bg.astype(BF16), cg.astype(BF16)
            cb = _nt(cgb, bgb)
            cgt = cg.T.astype(BF16)
            dgsum = jnp.zeros((c, c), F32)
            dbg = jnp.zeros((c, D_STATE), F32)
            dcg = jnp.zeros((c, D_STATE), F32)
            for pr in (2 * g, 2 * g + 1):
                sl = slice(c * pr, c * (pr + 1))
                st = st_ref[0, pr]
                dst = dstate[pr]
                stb, dstb = st.astype(BF16), dst.astype(BF16)
                xp = x[:, sl]
                xpb = xp.astype(BF16)
                dyp = dyv[:, sl]
                xdp = xp * q["dte_x"][:, sl]
                yoff_parts.append(_nn(cgb, stb) * q["ea_x"][:, sl])
                rr = _nn(bgb, dstb)
                dxp = rr * q["dte_x"][:, sl]
                u_parts.append(rr * xdp)
                v_parts.append(dst * st * q["eal_x"][:, sl])
                for k, hh in enumerate((2 * pr, 2 * pr + 1)):
                    decay = _ssd_decay(q, hh)
                    w = cb * decay
                    mask = q["left"] if k == 0 else jnp.logical_not(q["left"])
                    dym = jnp.where(mask, dyp, 0.0).astype(BF16)
                    dw = _nt(dym, xpb)
                    dgsum = dgsum + dw * decay
                    e_parts.append(dw * w)
                    dxp = dxp + _nn(w.T.astype(BF16), dym)
                dyeb = dye[:, sl].astype(BF16)
                dcg = dcg + _nt(dyeb, stb)
                dbg = dbg + _nt(xdp.astype(BF16), dstb)
                dstate[pr] = dst * q["eal_x"][:, sl] + _nn(cgt, dyeb)
                dx_parts.append(dxp)
            dcg = dcg + _nn(dgsum.astype(BF16), bgb)
            dbg = dbg + _nn(dgsum.T.astype(BF16), cgb)
            db_parts.append(dbg)
            dc_parts.append(dcg)
        dx = jnp.concatenate(dx_parts, axis=1)
        yoff = jnp.concatenate(yoff_parts, axis=1)
        u = jnp.concatenate(u_parts, axis=1)
        v = jnp.concatenate(v_parts, axis=1)
        reduce_heads = (_iota2((SSD_WIDTH, c), 0) // HEAD_DIM == _iota2((SSD_WIDTH, c), 1)).astype(BF16)
        to_head = (_iota2((SSD_HEADS * c, c), 0) // c == _iota2((SSD_HEADS * c, c), 1)).astype(BF16)
        da = _split_dot(dyv * yoff - u, reduce_heads, 2)
        da = da + _split_dot(jnp.concatenate(e_parts, axis=1), to_head, 2)
        da = da - _split_dot(jnp.concatenate(e_parts, axis=0), to_head, 2, dot=_tn)
        dalast = jnp.sum(_split_dot(u + v, reduce_heads, 2), axis=0, keepdims=True)
        da = da + jnp.where(_iota2((c, c), 0) == c - 1, dalast, 0.0)
        utri = (_iota2((c, c), 1) >= _iota2((c, c), 0)).astype(BF16)
        dda = _split_dot_left(utri, da)
        ddt = dda * q["a"] + _split_dot(dx * xs, reduce_heads, 2)
        dalog_ref[...] += jnp.sum(dda * q["dt"], axis=0, keepdims=True) * q["a"]
        ddtraw = jnp.where(_iota2((c, c), 1) < SSD_HEADS, ddt * _sigmoid(dtraw + dtb_ref[...]), 0.0)
        ddtb_ref[...] += jnp.sum(ddtraw, axis=0, keepdims=True)
        dzdt_ref[...] = jnp.concatenate([dz, ddtraw.astype(BF16), jnp.zeros((c, ZDT - SSD_WIDTH - LANES), BF16)], axis=1)
        dxs = dx * q["dt_x"] + dyv * dsk_ref[...]
        dxc_ref[...] = jnp.concatenate([dxs] + db_parts + dc_parts, axis=1)

    rev = lambda i: nc - 1 - i
    vec = lambda n: pl.BlockSpec((1, n), lambda i: (0, 0))
    wide = pl.BlockSpec((c, SSD_WIDTH), lambda i: (rev(i), 0))
    zdt = pl.BlockSpec((c, ZDT), lambda i: (rev(i), OFF_Z // ZDT))
    return pl.pallas_call(
        body, name=name, grid=(nc,),
        in_specs=[pl.BlockSpec((c, CONV_DIM), lambda i: (rev(i), 0)), zdt, wide, wide,
                  pl.BlockSpec((1, N_PAIRS, D_STATE, c), lambda i: (rev(i), 0, 0, 0)),
                  vec(LANES), vec(LANES), vec(SSD_WIDTH), vec(SSD_WIDTH)],
        out_specs=[pl.BlockSpec((c, CONV_DIM), lambda i: (rev(i), 0)), zdt,
                   vec(SSD_WIDTH), vec(SSD_WIDTH), vec(LANES), vec(LANES)],
        out_shape=[jax.ShapeDtypeStruct((s, CONV_DIM), F32), jax.ShapeDtypeStruct((s, D_INP), BF16),
                   jax.ShapeDtypeStruct((1, SSD_WIDTH), F32),
                   jax.ShapeDtypeStruct((1, SSD_WIDTH), F32), jax.ShapeDtypeStruct((1, LANES), F32),
                   jax.ShapeDtypeStruct((1, LANES), F32)],
        scratch_shapes=[pltpu.VMEM((N_PAIRS, D_STATE, c), F32)],
        compiler_params=_params(("arbitrary",)),
    )(xc, proj, ycore, dyall, states, dtb, alog, dskip_x, nw)


SB_BLOCK = 256
SB_SCALE = HEAD_DIM ** -0.5


def _sb_weights(qm, kb, diagonal, run_lk, strict_after):
    z = _nt(qm, kb)
    nz = -z
    tail = jnp.log(1.0 + jnp.exp(jnp.minimum(z, nz)))
    ls = jnp.minimum(z, 0.0) - tail
    lk = jnp.minimum(nz, 0.0) - tail
    if diagonal:
        valid = _iota2(z.shape, 1) < _iota2(z.shape, 0)
        lk = jnp.where(valid, lk, 0.0)
    w = jnp.exp(ls + _split_dot(lk, strict_after, 2) + run_lk)
    if diagonal:
        w = jnp.where(valid, w, 0.0)
    return ls, lk, w


def sb_forward(proj, y_all, name):
    s = proj.shape[0]
    t = SB_BLOCK
    nq = s // t

    def body(q_ref, k_ref, v_ref, _, y_ref, o_ref):
        i = pl.program_id(1)
        left = _iota2((t, LANES), 1) < HEAD_DIM
        qv = q_ref[...] * SB_SCALE
        zero = jnp.zeros_like(qv)
        qms = (jnp.where(left, qv, zero).astype(BF16), jnp.where(left, zero, qv).astype(BF16))
        strict_after = (_iota2((t, t), 0) > _iota2((t, t), 1)).astype(BF16)

        def block(j, carry, diagonal):
            o, runs = carry[0], carry[1:]
            rows = pl.ds(pl.multiple_of(j * t, t), t)
            kb = k_ref[rows, :].astype(BF16)
            vv = v_ref[rows, :]
            new_runs = []
            for k in range(2):
                _, lk, w = _sb_weights(qms[k], kb, diagonal, runs[k], strict_after)
                vm = jnp.where(left if k == 0 else jnp.logical_not(left), vv, 0.0).astype(BF16)
                o = o + _nn(w.astype(BF16), vm)
                new_runs.append(runs[k] + jnp.sum(lk, axis=1, keepdims=True))
            return (o, *new_runs)

        init = (jnp.zeros((t, LANES), F32), jnp.zeros((t, 1), F32), jnp.zeros((t, 1), F32))
        o = lax.fori_loop(1, i + 1, lambda jj, carry: block(i - jj, carry, False), block(i, init, True))[0]
        o_ref[...] = o
        y_ref[...] = o.astype(BF16)

    return pl.pallas_call(
        body, name=name, grid=(2, nq),
        in_specs=[pl.BlockSpec((t, LANES), lambda p, i: (i, 3 * p)),
                  pl.BlockSpec((s, LANES), lambda p, i: (0, 3 * p + 1)),
                  pl.BlockSpec((s, LANES), lambda p, i: (0, 3 * p + 2)), ANY],
        out_specs=[pl.BlockSpec((t, LANES), lambda p, i: (i, SSD_WIDTH // LANES + p)),
                   pl.BlockSpec((t, LANES), lambda p, i: (i, p))],
        out_shape=[jax.ShapeDtypeStruct(y_all.shape, BF16), jax.ShapeDtypeStruct((s, SB_WIDTH), F32)],
        input_output_aliases={3: 0},
        compiler_params=_params(("parallel", "arbitrary")),
    )(proj, proj, proj, y_all)


def sb_backward(proj, o, dyall, dproj, name):
    s = proj.shape[0]
    t = SB_BLOCK
    nq = s // t

    def body(q_ref, k_ref, v_ref, o_ref, do_ref, _, dqkv_ref, dk_acc, dv_acc):
        dk_acc[...] = jnp.zeros_like(dk_acc)
        dv_acc[...] = jnp.zeros_like(dv_acc)
        left = _iota2((t, LANES), 1) < HEAD_DIM
        lane_masks = (left, jnp.logical_not(left))
        strict_after = (_iota2((t, t), 0) > _iota2((t, t), 1)).astype(BF16)
        from_here = (_iota2((t, t), 0) >= _iota2((t, t), 1)).astype(BF16)

        def query_block(i, _):
            qrows = pl.ds(pl.multiple_of(i * t, t), t)
            qv = q_ref[qrows, :] * SB_SCALE
            dov = do_ref[qrows, :]
            zero = jnp.zeros_like(qv)
            qb = qv.astype(BF16)
            dob = dov.astype(BF16)
            prod = dob.astype(F32) * o_ref[qrows, :]
            qms = [jnp.where(m, qv, zero).astype(BF16) for m in lane_masks]
            doms = [jnp.where(m, dov, zero).astype(BF16) for m in lane_masks]
            deltas = [jnp.sum(jnp.where(m, prod, zero), axis=1, keepdims=True) for m in lane_masks]

            def block(j, carry, diagonal):
                dq = carry[0]
                run_lk, run_e = carry[1:3], carry[3:5]
                rows = pl.ds(pl.multiple_of(j * t, t), t)
                kb = k_ref[rows, :].astype(BF16)
                vb = v_ref[rows, :].astype(BF16)
                dkj = jnp.zeros((t, LANES), F32)
                dvj = jnp.zeros((t, LANES), F32)
                new_lk, new_e = [], []
                for k in range(2):
                    ls, lk, w = _sb_weights(qms[k], kb, diagonal, run_lk[k], strict_after)
                    wb = w.astype(BF16)
                    e = _nt(doms[k], vb) * wb.astype(F32)
                    before = deltas[k] - _split_dot(e, from_here, 2) - run_e[k]
                    dz = e - jnp.exp(ls) * (e + before)
                    if diagonal:
                        dz = jnp.where(_iota2(dz.shape, 1) < _iota2(dz.shape, 0), dz, 0.0)
                    dz = dz.astype(BF16)
                    m = lane_masks[k]
                    dvj = dvj + jnp.where(m, _tn(wb, dob), 0.0)
                    dkj = dkj + jnp.where(m, _tn(dz, qb), 0.0)
                    dq = dq + jnp.where(m, _nn(dz, kb), 0.0)
                    new_lk.append(run_lk[k] + jnp.sum(lk, axis=1, keepdims=True))
                    new_e.append(run_e[k] + jnp.sum(e, axis=1, keepdims=True))
                dk_acc[rows, :] += dkj
                dv_acc[rows, :] += dvj
                return (dq, *new_lk, *new_e)

            col = jnp.zeros((t, 1), F32)
            first = block(i, (jnp.zeros((t, LANES), F32), col, col, col, col), True)
            dq = lax.fori_loop(1, i + 1, lambda jj, carry: block(i - jj, carry, False), first)[0]
            dqkv_ref[qrows, 0:LANES] = (dq * SB_SCALE).astype(BF16)
            return 0

        lax.fori_loop(0, nq, query_block, 0)
        dqkv_ref[:, LANES:2 * LANES] = dk_acc[...].astype(BF16)
        dqkv_ref[:, 2 * LANES:3 * LANES] = dv_acc[...].astype(BF16)

    col = lambda f: pl.BlockSpec((s, LANES), f)
    return pl.pallas_call(
        body, name=name, grid=(2,),
        in_specs=[col(lambda p: (0, 3 * p)), col(lambda p: (0, 3 * p + 1)), col(lambda p: (0, 3 * p + 2)),
                  col(lambda p: (0, p)), col(lambda p: (0, SSD_WIDTH // LANES + p)), ANY],
        out_specs=pl.BlockSpec((s, 3 * LANES), lambda p: (0, p)),
        out_shape=jax.ShapeDtypeStruct(dproj.shape, BF16),
        input_output_aliases={5: 0},
        scratch_shapes=[pltpu.VMEM((s, LANES), F32), pltpu.VMEM((s, LANES), F32)],
        compiler_params=_params(("parallel",)),
    )(proj, proj, proj, o, dyall, dproj)


def adamw(w, g, m, v, name):
    b, r, c = w.shape
    tr = max([t for t in range(8, min(r, 512) + 1, 8) if r % t == 0], default=r)

    def body(w_ref, g_ref, m_ref, v_ref, d_ref, nm_ref, nv_ref):
        gv = g_ref[...]
        nm = ADAM_B1 * m_ref[...] + (1.0 - ADAM_B1) * gv
        nv = ADAM_B2 * v_ref[...] + (1.0 - ADAM_B2) * (gv * gv)
        m_hat = nm / (1.0 - ADAM_B1 ** ADAM_STEP)
        v_hat = nv / (1.0 - ADAM_B2 ** ADAM_STEP)
        d_ref[...] = -ADAM_LR * (m_hat / (jnp.sqrt(v_hat) + ADAM_EPS) + ADAM_WD * w_ref[...])
        nm_ref[...] = nm
        nv_ref[...] = nv

    blk = pl.BlockSpec((1, tr, c), lambda i, j: (i, j, 0))
    return pl.pallas_call(
        body, name=name, grid=(b, r // tr),
        in_specs=[blk] * 4, out_specs=[blk] * 3,
        out_shape=[jax.ShapeDtypeStruct(w.shape, F32)] * 3,
        compiler_params=_params(("parallel", "parallel")),
    )(w, g, m, v)


def _position():
    return lax.axis_index("x"), lax.axis_index("y"), lax.axis_index("c")


def _flipped(pos, flip):
    return tuple((1 - p) if f else p for p, f in zip(pos, flip))


FLIP_C = (0, 0, 1)
CHIP_FLIPS = {1: (0, 1, 0), 2: (1, 0, 0), 3: (1, 1, 0)}
SHARD_ROWS = (SHARD_IN, SHARD_OUT, SHARD_FF, SHARD_FF, SHARD_FF)


def _rows(start, size):
    return pl.ds(pl.multiple_of(start, 16), size)


HBM = pl.BlockSpec(memory_space=pltpu.HBM)
SEM = pl.BlockSpec(memory_space=pltpu.SEMAPHORE)
EFFECT = pltpu.SideEffectType.DATAFLOW_SIDE_EFFECTING


def _in_hbm(a):
    return pltpu.with_memory_space_constraint(a, pltpu.HBM)


def _landing(shape, dtype):
    return _in_hbm(lax.empty(shape, dtype))


def _copies(plan, pos, src_refs, land_refs, send_sems, recv_sems):
    return [pltpu.make_async_remote_copy(src_ref=src, dst_ref=dst, send_sem=send_sems.at[k], recv_sem=recv_sems.at[k],
                                         device_id=_flipped(pos, flip), device_id_type=MESH)
            for k, (src, dst, flip) in enumerate(plan(pos, src_refs, land_refs))]


def exchange_start(name, srcs, lands, n, plan, after=None):
    ns, nl = len(srcs), len(lands)
    specs, ops = _after(after)

    def body(*refs):
        src_refs, land_refs = refs[:ns], refs[ns:ns + nl]
        send_sems, recv_sems, token = refs[ns + nl + len(ops)], refs[ns + nl + len(ops) + 1], refs[-1]
        for cp in _copies(plan, _position(), src_refs, land_refs, send_sems, recv_sems):
            cp.start()
        token[...] = jnp.zeros_like(token)

    thru = [pltpu.HBM(a.shape, a.dtype) for a in list(srcs) + list(lands)]
    out = pl.pallas_call(
        body, name=name,
        out_shape=(pltpu.SemaphoreType.DMA((n,)), pltpu.SemaphoreType.DMA((n,)), *thru, jax.ShapeDtypeStruct((8, LANES), F32)),
        in_specs=[HBM] * (ns + nl) + specs,
        out_specs=(SEM, SEM, *([HBM] * (ns + nl)), pl.BlockSpec(memory_space=pltpu.VMEM)),
        input_output_aliases={k: 2 + k for k in range(ns + nl)},
        compiler_params=pltpu.CompilerParams(has_side_effects=EFFECT),
    )(*[_in_hbm(a) for a in srcs], *lands, *ops)
    return out[0], out[1], list(out[2:2 + ns]), list(out[2 + ns:2 + ns + nl]), out[-1]


def exchange_wait(name, started, after, plan):
    send_sems, recv_sems, srcs, lands, _ = started
    ns, nl = len(srcs), len(lands)
    specs, ops = _after(after)

    def body(*refs):
        src_refs, land_refs = refs[:ns], refs[ns:ns + nl]
        send_sems, recv_sems = refs[ns + nl], refs[ns + nl + 1]
        for cp in _copies(plan, _position(), src_refs, land_refs, send_sems, recv_sems):
            cp.wait_send()
            cp.wait_recv()

    out = pl.pallas_call(
        body, name=name,
        out_shape=tuple(pltpu.HBM(a.shape, a.dtype) for a in list(srcs) + list(lands)),
        in_specs=[HBM] * (ns + nl) + [SEM, SEM] + specs,
        out_specs=tuple([HBM] * (ns + nl)),
        input_output_aliases={k: k for k in range(ns + nl)},
        compiler_params=pltpu.CompilerParams(has_side_effects=EFFECT),
    )(*srcs, *lands, send_sems, recv_sems, *ops)
    return list(out[:ns]), list(out[ns:])


def _gather_ici_plan(pos, srcs, lands):
    chip, c = 2 * pos[0] + pos[1], pos[2]
    copies = []
    for src, dst, r in zip(srcs, lands, SHARD_ROWS):
        h = r // 2
        for f in (1, 2, 3):
            copies.append((src.at[_rows(c * h, h)], dst.at[_rows(chip * r + c * h, h)], CHIP_FLIPS[f]))
    return copies


def _gather_d2d_plan(pos, srcs, lands):
    chip, c = 2 * pos[0] + pos[1], pos[2]
    copies = []
    for own, dst, r in zip(srcs, lands, SHARD_ROWS):
        h = r // 2
        copies.append((own, dst.at[_rows(chip * r, r)], FLIP_C))
        for f in (1, 2, 3):
            at = _rows(lax.bitwise_xor(chip, f) * r + c * h, h)
            copies.append((dst.at[at], dst.at[at], FLIP_C))
    return copies


def gather_ici_start(shards, after=None):
    lands = [_landing((N_CHIPS * r, D_MODEL), BF16) for r in SHARD_ROWS]
    return exchange_start("gather_ici_start", shards, lands, 15, _gather_ici_plan, after=after)


def gather_d2d_start(shards, fulls):
    return exchange_start("gather_d2d_start", shards, fulls, 20, _gather_d2d_plan)


def _reduce_d2d_plan(pos, srcs, lands):
    c = pos[2]
    return [(src.at[:, _rows((1 - c) * (r // 2), r // 2)], dst, FLIP_C) for src, dst, r in zip(srcs, lands, SHARD_ROWS)]


def _reduce_ici_plan(pos, srcs, lands):
    chip = 2 * pos[0] + pos[1]
    return [(src.at[lax.bitwise_xor(chip, f)], dst.at[f - 1], CHIP_FLIPS[f]) for src, dst in zip(srcs, lands) for f in (1, 2, 3)]


def _reduce_swap_plan(pos, srcs, lands):
    c = pos[2]
    copies = []
    for dst, r in zip(lands, SHARD_ROWS):
        at = _rows(c * (r // 2), r // 2)
        copies.append((dst.at[at], dst.at[at], FLIP_C))
    return copies


def reduce_d2d_start(grads):
    lands = [_landing((N_CHIPS, r // 2, D_MODEL), BF16) for r in SHARD_ROWS]
    return exchange_start("reduce_d2d_start", grads, lands, 5, _reduce_d2d_plan)


def reduce_ici_start(chip_sums):
    lands = [_landing((N_CHIPS - 1, r // 2, D_MODEL), BF16) for r in SHARD_ROWS]
    return exchange_start("reduce_ici_start", chip_sums, lands, 15, _reduce_ici_plan)


def reduce_swap_start(mine):
    return exchange_start("reduce_swap_start", [], mine, 5, _reduce_swap_plan)


def add_halves(d, recv, half, name):
    nch, r, c = d.shape
    h = r // 2

    def body(half_ref, d_ref, r_ref, o_ref):
        o_ref[...] = (d_ref[...].astype(F32) + r_ref[...].astype(F32)).astype(BF16)

    return pl.pallas_call(
        body, name=name,
        grid_spec=pltpu.PrefetchScalarGridSpec(
            num_scalar_prefetch=1, grid=(nch,),
            in_specs=[pl.BlockSpec((1, h, c), lambda j, hf: (j, hf[0], 0)),
                      pl.BlockSpec((1, h, c), lambda j, hf: (j, 0, 0))],
            out_specs=pl.BlockSpec((1, h, c), lambda j, hf: (j, 0, 0))),
        out_shape=jax.ShapeDtypeStruct(recv.shape, BF16),
        compiler_params=_params(("parallel",)),
    )(half, d, recv)


def add_chips(p, recv, chip, name):
    _, r, c = p.shape

    def body(chip_ref, p_ref, r_ref, o_ref):
        acc = p_ref[0].astype(F32)
        for k in range(N_CHIPS - 1):
            acc = acc + r_ref[k].astype(F32)
        o_ref[...] = acc

    return pl.pallas_call(
        body, name=name,
        grid_spec=pltpu.PrefetchScalarGridSpec(
            num_scalar_prefetch=1, grid=(1,),
            in_specs=[pl.BlockSpec((1, r, c), lambda i, ch: (ch[0], 0, 0)),
                      pl.BlockSpec((N_CHIPS - 1, r, c), lambda i, ch: (0, 0, 0))],
            out_specs=pl.BlockSpec((r, c), lambda i, ch: (ch[1], 0))),
        out_shape=jax.ShapeDtypeStruct((2 * r, c), F32),
        compiler_params=_params(("arbitrary",)),
    )(chip, p, recv)


def adamw_layers(w, gs, m, v, name):
    b, r, c = w.shape
    tr = max([t for t in range(8, min(r, 512) + 1, 8) if r % t == 0], default=r)

    def body(w_ref, m_ref, v_ref, *rest):
        g_refs, (g_ref, d_ref, nm_ref, nv_ref) = rest[:b], rest[b:]
        layer = pl.program_id(0)
        gv = g_refs[0][...]
        for l in range(1, b):
            gv = jnp.where(layer == l, g_refs[l][...], gv)
        nm = ADAM_B1 * m_ref[0] + (1.0 - ADAM_B1) * gv
        nv = ADAM_B2 * v_ref[0] + (1.0 - ADAM_B2) * (gv * gv)
        m_hat = nm / (1.0 - ADAM_B1 ** ADAM_STEP)
        v_hat = nv / (1.0 - ADAM_B2 ** ADAM_STEP)
        g_ref[0] = gv
        d_ref[0] = -ADAM_LR * (m_hat / (jnp.sqrt(v_hat) + ADAM_EPS) + ADAM_WD * w_ref[0])
        nm_ref[0] = nm
        nv_ref[0] = nv

    nr, tc = r // tr, (c if tr < r else _tile(c, 256))
    steps = nr * (c // tc)
    blk = pl.BlockSpec((1, tr, tc), lambda i, j: (i, j % nr, j // nr))
    g_specs = [pl.BlockSpec((tr, tc), lambda i, j, l=l: (jnp.where(i == l, j % nr, jnp.where(i < l, 0, nr - 1)),
                                                         jnp.where(i == l, j // nr, jnp.where(i < l, 0, c // tc - 1))))
               for l in range(b)]
    return pl.pallas_call(
        body, name=name, grid=(b, steps),
        in_specs=[blk] * 3 + g_specs, out_specs=[blk] * 4,
        out_shape=[jax.ShapeDtypeStruct(w.shape, F32)] * 4,
        compiler_params=_params(("arbitrary", "arbitrary")),
    )(w, m, v, *gs)


def small_allreduce(v, name):
    r, c = v.shape

    def body(v_ref, o_ref, buf, send_sems, recv_sems):
        pos = _position()
        me = 4 * pos[0] + 2 * pos[1] + pos[2]
        buf[0] = v_ref[...]
        copies = []
        for f in range(1, 8):
            flip = ((f >> 2) & 1, (f >> 1) & 1, f & 1)
            cp = pltpu.make_async_remote_copy(
                src_ref=v_ref, dst_ref=buf.at[f], send_sem=send_sems.at[f - 1], recv_sem=recv_sems.at[f - 1],
                device_id=_flipped(pos, flip), device_id_type=MESH)
            cp.start()
            copies.append(cp)
        for cp in copies:
            cp.wait()
        acc = buf[me]
        for d in range(1, 8):
            acc = acc + buf[lax.bitwise_xor(me, d)]
        o_ref[...] = acc

    return pl.pallas_call(
        body, name=name,
        in_specs=[pl.BlockSpec(memory_space=pltpu.VMEM)], out_specs=pl.BlockSpec(memory_space=pltpu.VMEM),
        out_shape=jax.ShapeDtypeStruct((r, c), F32),
        scratch_shapes=[pltpu.VMEM((8, r, c), F32), pltpu.SemaphoreType.DMA((7,)), pltpu.SemaphoreType.DMA((7,))],
    )(v)


_IN_SEGMENTS = ((0, 1544, 128), (128, 1800, 128), (256, 2056, 128), (384, 1672, 128), (512, 1928, 128), (640, 2184, 128),
                (OFF_Z, 0, SSD_WIDTH), (OFF_DT, 1536, SSD_HEADS), (OFF_XBC, 512, CONV_DIM), (OFF_P, 2312, POOL_WIDTH))


def _in_column_map():
    m = np.full((D_INP,), -1, np.int64)
    for at, orig, n in _IN_SEGMENTS:
        cols = np.arange(orig, orig + n)
        m[at:at + n] = (cols // COLS_IN) * SHARD_IN + cols % COLS_IN
    return m


def take_rows(a, idx, name):
    dep, r_in, c = a.shape
    blk = LANES
    n_out, n_in = len(idx) // blk, r_in // blk
    assert len(idx) % blk == 0 and r_in % blk == 0
    sources = [sorted({int(v) // blk for v in idx[blk * i:blk * (i + 1)] if v >= 0}) for i in range(n_out)]
    width = max(len(s) for s in sources)
    table = np.zeros((n_out, width), np.int32)
    for i, s in enumerate(sources):
        spare = [b for b in range(n_in) if b not in s][:width - len(s)]
        table[i] = s + spare

    def body(tbl_ref, idx_ref, *refs):
        in_refs, o_ref = refs[:width], refs[width]
        i = pl.program_id(1)
        src = idx_ref[...]
        acc = jnp.zeros((blk, c), F32)
        for k in range(width):
            pick = (src == tbl_ref[i, k] * blk + _iota2((blk, blk), 1)).astype(BF16)
            acc = acc + _nn(pick, in_refs[k][0])
        o_ref[0] = acc.astype(BF16)

    return pl.pallas_call(
        body, name=name,
        grid_spec=pltpu.PrefetchScalarGridSpec(
            num_scalar_prefetch=1, grid=(dep, n_out),
            in_specs=[pl.BlockSpec((blk, 1), lambda l, i, t: (i, 0))] +
                     [pl.BlockSpec((1, blk, c), lambda l, i, t, k=k: (l, t[i, k], 0)) for k in range(width)],
            out_specs=pl.BlockSpec((1, blk, c), lambda l, i, t: (l, i, 0))),
        out_shape=jax.ShapeDtypeStruct((dep, len(idx), c), BF16),
        compiler_params=_params(("parallel", "parallel")),
    )(jnp.asarray(table), jnp.asarray(np.asarray(idx, np.int32).reshape(-1, 1)), *([a] * width))


def _in_weight_layout(staged):
    return take_rows(staged, _in_column_map(), "w_in_layout")


def _in_gradient_layout(dwt):
    fwd = _in_column_map()
    inv = np.full((N_CHIPS * SHARD_IN,), -1, np.int64)
    inv[fwd[fwd >= 0]] = np.nonzero(fwd >= 0)[0]
    return take_rows(dwt, inv, "dw_in_layout")


SMALL_NAMES = ("norm1_w", "conv_w", "conv_b", "dt_bias", "a_log", "d_skip", "ssd_norm_w", "pool_w", "pool_b",
               "pool_scale", "norm2_w", "final_norm_w")
SMALL_ROWS = 104


def _pack_small(parts):
    flat = jnp.concatenate([p.reshape(-1) for p in parts])
    return jnp.pad(flat, (0, SMALL_ROWS * D_MODEL - flat.shape[0])).reshape(SMALL_ROWS, D_MODEL)


def _unpack_small(flat, shapes):
    flat = flat.reshape(-1)
    out, at = [], 0
    for shp in shapes:
        n = int(np.prod(shp))
        out.append(flat[at:at + n].reshape(shp))
        at += n
    return out


def kernel(x, norm1_w, w_in, conv_w, conv_b, dt_bias, a_log, d_skip, ssd_norm_w, pool_w, pool_b, pool_scale, w_out, norm2_w, w_gate, w_up, w_down, final_norm_w, loss_target, m_norm1_w, m_w_in, m_conv_w, m_conv_b, m_dt_bias, m_a_log, m_d_skip, m_ssd_norm_w, m_pool_w, m_pool_b, m_pool_scale, m_w_out, m_norm2_w, m_w_gate, m_w_up, m_w_down, m_final_norm_w, v_norm1_w, v_w_in, v_conv_w, v_conv_b, v_dt_bias, v_a_log, v_d_skip, v_ssd_norm_w, v_pool_w, v_pool_b, v_pool_scale, v_w_out, v_norm2_w, v_w_gate, v_w_up, v_w_down, v_final_norm_w):
    px, py, pc = _position()
    chip = 2 * px + py
    chip_arr = jnp.reshape(chip, (1,)).astype(jnp.int32)
    half_arr = jnp.reshape(pc, (1,)).astype(jnp.int32)

    def layer_shards(l):
        w_in_t = jnp.pad(jnp.swapaxes(w_in[l], 0, 1).astype(BF16), ((0, SHARD_IN - COLS_IN), (0, 0)))
        return [w_in_t, w_out[l].astype(BF16), jnp.swapaxes(w_gate[l], 0, 1).astype(BF16),
                jnp.swapaxes(w_up[l], 0, 1).astype(BF16), w_down[l].astype(BF16)]

    over_ici = {0: gather_ici_start(layer_shards(0))}

    def pass_on(l, after):
        own, arrived = exchange_wait("gather_ici_wait", over_ici[l], after, _gather_ici_plan)
        swap = gather_d2d_start(own, arrived)
        tokens = [swap[4]]
        if l + 1 < DEPTH:
            over_ici[l + 1] = gather_ici_start(layer_shards(l + 1), after=swap[4])
            tokens.append(over_ici[l + 1][4])
        return swap, tokens

    def weights_of(swap, after):
        _, (w_in_st, w_out_l, w_gate_t, w_up_t, w_down_l) = exchange_wait("gather_d2d_wait", swap, after, _gather_d2d_plan)
        return _in_weight_layout(w_in_st[None])[0], w_out_l, w_gate_t, w_up_t, w_down_l

    pad_heads = lambda v: jnp.pad(v, ((0, 0), (0, LANES - SSD_HEADS)))[:, None, :]
    dtb, alog = pad_heads(dt_bias), pad_heads(a_log)
    dskip_x = jnp.repeat(d_skip, HEAD_DIM, axis=1)[:, None, :]
    eye = jnp.eye(len(POOL_WINDOWS), dtype=F32)
    wbd = (pool_w[:, :, :, None, :] * eye[None, :, None, :, None]).reshape(DEPTH, POOL_WIDTH, POOL_WIDTH).astype(BF16)
    pool_b2 = pool_b.reshape(DEPTH, 1, POOL_WIDTH)
    cw_cols = lax.dynamic_update_slice(jnp.zeros((DEPTH, CONV_WIDTH, CONV_DIM), F32), conv_w,
                                       (0, 0, chip * (CONV_DIM // N_CHIPS)))
    cw_cols = jnp.where(pc == 0, cw_cols, 0.0)
    cw_rows = (DEPTH * CONV_WIDTH * CONV_DIM) // D_MODEL
    conv_w_f = small_allreduce(jnp.pad(cw_cols.reshape(cw_rows, D_MODEL), ((0, 8), (0, 0))), "gather_conv_w")
    conv_w_f = conv_w_f[:cw_rows].reshape(DEPTH, CONV_WIDTH, CONV_DIM)
    cw8 = jnp.pad(conv_w_f, ((0, 0), (0, 8 - CONV_WIDTH), (0, 0)))

    h = x[0]
    saved, weights = [], []
    swap, tokens = pass_on(0, over_ici[0][4])
    weights.append(weights_of(swap, tokens))
    for l in range(DEPTH):
        w_in_f, w_out_f, w_gate_t, w_up_t, w_down_f = weights[l]
        proj = rms_matmul(h, norm1_w[l][None], w_in_f, "in_proj")
        xc = conv_forward(proj, cw8[l], conv_b[l][None], "conv_fwd")
        y_all, ycore, states = ssd_forward(proj, xc, dtb[l], alog[l], dskip_x[l], ssd_norm_w[l][None], "ssd_fwd")
        y_all, o_sb = sb_forward(proj, y_all, "sb_fwd")
        swap, tokens = pass_on(l + 1, o_sb) if l + 1 < DEPTH else (None, None)
        y_all = pool_forward(proj, wbd[l], pool_b2[l], pool_scale[l][None], y_all, "pool_fwd", after=tokens)
        x1 = matmul_residual(y_all, w_out_f, h, "out_proj")
        x2, g, u = ffn_forward(x1, norm2_w[l][None], w_gate_t, w_up_t, w_down_f, "ffn_fwd")
        if swap is not None:
            weights.append(weights_of(swap, x2))
        saved.append((h, proj, xc, ycore, states, o_sb, y_all, x1, g, u))
        h = x2

    loss_part, dx, dxb, d_final = loss_head(h, final_norm_w[None], loss_target[0], "loss_head")
    loss = lax.psum(loss_part[0, 0], ("x", "y", "c"))

    small = {n: [None] * DEPTH for n in SMALL_NAMES if n != "final_norm_w"}
    chip_half = jnp.concatenate([chip_arr, half_arr])
    reduced = [None] * DEPTH
    d2d = ici = None

    def add_cores(d2d, after):
        mine, theirs = exchange_wait("reduce_d2d_wait", d2d[1], after, _reduce_d2d_plan)
        return d2d[0], reduce_ici_start([add_halves(d, t, half_arr, "reduce_add_halves") for d, t in zip(mine, theirs)])

    def add_all(ici, after):
        sums, theirs = exchange_wait("reduce_ici_wait", ici[1], after, _reduce_ici_plan)
        return ici[0], reduce_swap_start([add_chips(p, t, chip_half, "reduce_add_chips") for p, t in zip(sums, theirs)])

    def finish(swap, after):
        reduced[swap[0]] = exchange_wait("reduce_swap_wait", swap[1], after, _reduce_swap_plan)[1]

    swaps = []
    for l in reversed(range(DEPTH)):
        xin, proj, xc, ycore, states, o_sb, y_all, x1, g, u = saved[l]
        w_in_f, w_out_f, w_gate_t, w_up_t, w_down_f = weights[l]
        dg, du, act = ffn_backward_act(dxb, g, u, w_down_f, "ffn_bwd_act", after=None if d2d is None else d2d[1][4])
        dx1, dx1b, h2, dn2 = rms_backward([dg, du], [w_gate_t, w_up_t], x1, norm2_w[l][None], dx, "ffn_bwd_norm", 256)
        if d2d is not None:
            ici = add_cores(d2d, dx1b)
        dyall = matmul_nt(dx1b, w_out_f, "out_proj_bwd", after=None if ici is None else ici[1][4])
        dw_down = matmul_tn(act, dxb, "dw_down")
        dw_gate = matmul_tn(dg, h2, "dw_gate")
        dw_up = matmul_tn(du, h2, "dw_up")
        dw_out = matmul_tn(y_all, dx1b, "dw_out")
        dxc, dproj, dsn, ddsk, ddtb, dalog = ssd_backward(proj, xc, ycore, dyall, states, dtb[l], alog[l],
                                                          dskip_x[l], ssd_norm_w[l][None], "ssd_bwd")
        dproj, dcw, dcb = conv_backward(proj, dxc, cw8[l], conv_b[l][None], dproj, "conv_bwd")
        dproj = sb_backward(proj, o_sb, dyall, dproj, "sb_bwd")
        dproj, dwbd, dpb, dps = pool_backward(proj, dyall, wbd[l], pool_b2[l], pool_scale[l][None], dproj, "pool_bwd")
        if ici is not None:
            swaps.append(add_all(ici, dproj))
            ici = None
        dx, dxb, h1, dn1 = rms_backward([dproj], [w_in_f], xin, norm1_w[l][None], dx1, "in_proj_bwd", 256,
                                        after=swaps[-1][1][4] if swaps else None)
        dw_in = _in_gradient_layout(matmul_tn(dproj, h1, "dw_in")[None])[0]
        d2d = (l, reduce_d2d_start([dw.reshape(N_CHIPS, r, D_MODEL) for dw, r in
                                    zip((dw_in, dw_out, dw_gate, dw_up, dw_down), SHARD_ROWS)]))
        small["norm1_w"][l] = dn1[0]
        small["conv_w"][l] = dcw[:CONV_WIDTH]
        small["conv_b"][l] = dcb[0]
        small["dt_bias"][l] = ddtb[0, :SSD_HEADS]
        small["a_log"][l] = dalog[0, :SSD_HEADS]
        small["d_skip"][l] = ddsk.reshape(SSD_HEADS, HEAD_DIM).sum(axis=1)
        small["ssd_norm_w"][l] = dsn[0]
        small["pool_w"][l] = jnp.stack([dwbd[64 * k:64 * k + 64, 64 * k:64 * k + 64] for k in range(len(POOL_WINDOWS))])
        small["pool_b"][l] = dpb.reshape(len(POOL_WINDOWS), -1)
        small["pool_scale"][l] = dps[0]
        small["norm2_w"][l] = dn2[0]
    grad_x = dx[None]

    ici = add_cores(d2d, d2d[1][4])
    swaps.append(add_all(ici, ici[1][4]))
    for swap in swaps:
        finish(swap, swaps[-1][1][4])
    g_big = {n: [reduced[l][k] for l in range(DEPTH)] for k, n in enumerate(("w_in", "w_out", "w_gate", "w_up", "w_down"))}
    g_big["w_in"] = [gl[:COLS_IN] for gl in g_big["w_in"]]
    transposed = ("w_in", "w_gate", "w_up")

    small_parts = [d_final if n == "final_norm_w" else jnp.stack(small[n]) for n in SMALL_NAMES]
    small_shapes = [p.shape for p in small_parts]
    g_small = dict(zip(SMALL_NAMES, _unpack_small(small_allreduce(_pack_small(small_parts), "reduce_small"), small_shapes)))
    g_small["final_norm_w"] = g_small["final_norm_w"].reshape(final_norm_w.shape)
    g_small["conv_w"] = lax.dynamic_slice_in_dim(g_small["conv_w"], chip * (CONV_DIM // N_CHIPS), CONV_DIM // N_CHIPS, axis=2)

    given = dict(norm1_w=(norm1_w, m_norm1_w, v_norm1_w), w_in=(w_in, m_w_in, v_w_in), conv_w=(conv_w, m_conv_w, v_conv_w),
                 conv_b=(conv_b, m_conv_b, v_conv_b), dt_bias=(dt_bias, m_dt_bias, v_dt_bias), a_log=(a_log, m_a_log, v_a_log),
                 d_skip=(d_skip, m_d_skip, v_d_skip), ssd_norm_w=(ssd_norm_w, m_ssd_norm_w, v_ssd_norm_w),
                 pool_w=(pool_w, m_pool_w, v_pool_w), pool_b=(pool_b, m_pool_b, v_pool_b),
                 pool_scale=(pool_scale, m_pool_scale, v_pool_scale), w_out=(w_out, m_w_out, v_w_out),
                 norm2_w=(norm2_w, m_norm2_w, v_norm2_w), w_gate=(w_gate, m_w_gate, v_w_gate), w_up=(w_up, m_w_up, v_w_up),
                 w_down=(w_down, m_w_down, v_w_down), final_norm_w=(final_norm_w, m_final_norm_w, v_final_norm_w))
    order = ("norm1_w", "w_in", "conv_w", "conv_b", "dt_bias", "a_log", "d_skip", "ssd_norm_w", "pool_w", "pool_b",
             "pool_scale", "w_out", "norm2_w", "w_gate", "w_up", "w_down", "final_norm_w")
    grads = dict(g_small)
    results = {}
    for n in ("w_in", "w_out", "w_gate", "w_up", "w_down"):
        w, m, v = given[n]
        if n in transposed:
            out = adamw_layers(jnp.swapaxes(w, 1, 2), g_big[n], jnp.swapaxes(m, 1, 2), jnp.swapaxes(v, 1, 2), "adamw_" + n)
            out = [jnp.swapaxes(o, 1, 2) for o in out]
        else:
            out = adamw_layers(w, g_big[n], m, v, "adamw_" + n)
        grads[n], results[n] = out[0], tuple(out[1:])
    small_shapes = [given[n][0].shape for n in SMALL_NAMES]
    packed = [_pack_small([given[n][k] for n in SMALL_NAMES])[None] for k in range(3)]
    packed_g = _pack_small([grads[n] for n in SMALL_NAMES])[None]
    small_out = adamw(packed[0], packed_g, packed[1], packed[2], "adamw_small")
    small_out = [_unpack_small(o[0], small_shapes) for o in small_out]
    for i, n in enumerate(SMALL_NAMES):
        results[n] = tuple(small_out[k][i] for k in range(3))

    return (loss, grad_x, *[grads[n] for n in order], *[results[n][0] for n in order],
            *[results[n][1] for n in order], *[results[n][2] for n in order])
```

```python
import numpy as np
import jax
import jax.numpy as jnp
from jax import lax
from jax.experimental import pallas as pl
from jax.experimental.pallas import tpu as pltpu

F32 = jnp.float32
BF16 = jnp.bfloat16
MESH = pl.DeviceIdType.MESH
ANY = pl.BlockSpec(memory_space=pl.ANY)

D_MODEL = 1024
DEPTH = 4
EPS = 1e-6
SSD_WIDTH = 512
SSD_HEADS = 8
HEAD_DIM = 64
D_STATE = 128
CHUNK = 128
CONV_WIDTH = 4
CONV_DIM = 1024
SB_WIDTH = 256
POOL_WIDTH = 256
POOL_WINDOWS = (2, 4, 8, 16)
D_FF = 2816
D_IN = 2568
N_CHIPS = 4
OFF_QKV, OFF_Z, OFF_DT, OFF_XBC, OFF_P = 0, 768, 1280, 1536, 2560
D_INP = 2816
ZDT = 768
SHARD_IN, SHARD_OUT, SHARD_FF = 672, 256, 704
COLS_IN = 642
ADAM_LR, ADAM_B1, ADAM_B2, ADAM_EPS, ADAM_WD, ADAM_STEP = 0.001, 0.9, 0.999, 1e-08, 0.01, 10
LANES = 128
VMEM_LIMIT = 56 * 1024 * 1024


def _params(sem=None):
    return pltpu.CompilerParams(dimension_semantics=sem, vmem_limit_bytes=VMEM_LIMIT)


def _tile(n, cap):
    best = None
    for t in range(LANES, min(n, cap) + 1, LANES):
        if n % t == 0:
            best = t
    assert best is not None, (n, cap)
    return best


def _nt(a, b):
    return lax.dot_general(a, b, (((1,), (1,)), ((), ())), preferred_element_type=F32)


def _tn(a, b):
    return lax.dot_general(a, b, (((0,), (0,)), ((), ())), preferred_element_type=F32)


def _nn(a, b):
    return jnp.dot(a, b, preferred_element_type=F32)


def _split_dot(a, b_exact, terms=3, dot=_nn):
    acc = None
    rest = a
    for _ in range(terms):
        hi = rest.astype(BF16)
        part = dot(hi, b_exact)
        acc = part if acc is None else acc + part
        rest = rest - hi.astype(F32)
    return acc


def _split_dot_left(a_exact, b, terms=3):
    acc = None
    rest = b
    for _ in range(terms):
        hi = rest.astype(BF16)
        part = _nn(a_exact, hi)
        acc = part if acc is None else acc + part
        rest = rest - hi.astype(F32)
    return acc


def _sigmoid(x):
    return 1.0 / (1.0 + jnp.exp(-x))


def _softplus(x):
    return jnp.maximum(x, 0.0) + jnp.log(1.0 + jnp.exp(-jnp.abs(x)))


def _iota2(shape, dim):
    return lax.broadcasted_iota(jnp.int32, shape, dim)


def _after(after):
    ops = [] if after is None else list(after) if isinstance(after, (list, tuple)) else [after]
    return [ANY] * len(ops), ops


def rms_matmul(x, nw, wt, name, after=None):
    s, d = x.shape
    n = wt.shape[0]
    tm, tn = _tile(s, 512), _tile(n, 2816)
    specs, ops = _after(after)

    def body(x_ref, nw_ref, w_ref, *rest):
        o_ref, h_ref = rest[len(ops):]

        @pl.when(pl.program_id(1) == 0)
        def _():
            xv = x_ref[...]
            r = lax.rsqrt(jnp.mean(xv * xv, axis=-1, keepdims=True) + EPS)
            h_ref[...] = (xv * r * nw_ref[...]).astype(BF16)
        o_ref[...] = _nt(h_ref[...], w_ref[...])

    return pl.pallas_call(
        body, name=name, grid=(s // tm, n // tn),
        in_specs=[pl.BlockSpec((tm, d), lambda i, j: (i, 0)), pl.BlockSpec((1, d), lambda i, j: (0, 0)),
                  pl.BlockSpec((tn, d), lambda i, j: (j, 0))] + specs,
        out_specs=pl.BlockSpec((tm, tn), lambda i, j: (i, j)),
        out_shape=jax.ShapeDtypeStruct((s, n), F32),
        scratch_shapes=[pltpu.VMEM((tm, d), BF16)],
        compiler_params=_params(("parallel", "arbitrary")),
    )(x, nw, wt, *ops)


def matmul_residual(a, w, res, name):
    s, k = a.shape
    n = w.shape[1]
    tm, tn = _tile(s, 512), _tile(n, 512)

    def body(a_ref, w_ref, r_ref, o_ref):
        o_ref[...] = r_ref[...] + _nn(a_ref[...], w_ref[...])

    return pl.pallas_call(
        body, name=name, grid=(s // tm, n // tn),
        in_specs=[pl.BlockSpec((tm, k), lambda i, j: (i, 0)), pl.BlockSpec((k, tn), lambda i, j: (0, j)),
                  pl.BlockSpec((tm, tn), lambda i, j: (i, j))],
        out_specs=pl.BlockSpec((tm, tn), lambda i, j: (i, j)),
        out_shape=jax.ShapeDtypeStruct((s, n), F32),
        compiler_params=_params(("parallel", "parallel")),
    )(a, w, res)


def matmul_nt(a, w, name, out_dtype=F32, after=None):
    s, n = a.shape
    k = w.shape[0]
    tm, tk = _tile(s, 512), _tile(k, 512)
    specs, ops = _after(after)

    def body(a_ref, w_ref, *rest):
        rest[-1][...] = _nt(a_ref[...], w_ref[...]).astype(out_dtype)

    return pl.pallas_call(
        body, name=name, grid=(s // tm, k // tk),
        in_specs=[pl.BlockSpec((tm, n), lambda i, j: (i, 0)), pl.BlockSpec((tk, n), lambda i, j: (j, 0))] + specs,
        out_specs=pl.BlockSpec((tm, tk), lambda i, j: (i, j)),
        out_shape=jax.ShapeDtypeStruct((s, k), out_dtype),
        compiler_params=_params(("parallel", "parallel")),
    )(a, w, *ops)


def matmul_tn(a, b, name, after=None):
    s, m = a.shape
    n = b.shape[1]
    tm, tn = _tile(m, 512), _tile(n, 1024)

    def body(a_ref, b_ref, *rest):
        rest[-1][...] = _tn(a_ref[...], b_ref[...]).astype(BF16)

    specs, ops = _after(after)
    return pl.pallas_call(
        body, name=name, grid=(m // tm, n // tn),
        in_specs=[pl.BlockSpec((s, tm), lambda i, j: (0, i)), pl.BlockSpec((s, tn), lambda i, j: (0, j))] + specs,
        out_specs=pl.BlockSpec((tm, tn), lambda i, j: (i, j)),
        out_shape=jax.ShapeDtypeStruct((m, n), BF16),
        compiler_params=_params(("parallel", "parallel")),
    )(a, b, *ops)


def ffn_forward(x1, nw, wgt, wut, wd, name):
    s, d = x1.shape
    f = wgt.shape[0]
    tm, tf = _tile(s, 1024), _tile(f, 256)

    def body(x_ref, nw_ref, wg_ref, wu_ref, wd_ref, o_ref, g_ref, u_ref, h_ref, acc_ref):
        j = pl.program_id(1)

        @pl.when(j == 0)
        def _():
            xv = x_ref[...]
            r = lax.rsqrt(jnp.mean(xv * xv, axis=-1, keepdims=True) + EPS)
            h_ref[...] = (xv * r * nw_ref[...]).astype(BF16)
            acc_ref[...] = xv

        h = h_ref[...]
        g = _nt(h, wg_ref[...])
        u = _nt(h, wu_ref[...])
        g_ref[...] = g.astype(BF16)
        u_ref[...] = u.astype(BF16)
        a = (g * _sigmoid(g) * u).astype(BF16)
        acc_ref[...] += _nn(a, wd_ref[...])

        @pl.when(j == pl.num_programs(1) - 1)
        def _():
            o_ref[...] = acc_ref[...]

    wblk = pl.BlockSpec((tf, d), lambda i, j: (j, 0))
    return pl.pallas_call(
        body, name=name, grid=(s // tm, f // tf),
        in_specs=[pl.BlockSpec((tm, d), lambda i, j: (i, 0)), pl.BlockSpec((1, d), lambda i, j: (0, 0)), wblk, wblk, wblk],
        out_specs=[pl.BlockSpec((tm, d), lambda i, j: (i, 0)), pl.BlockSpec((tm, tf), lambda i, j: (i, j)),
                   pl.BlockSpec((tm, tf), lambda i, j: (i, j))],
        out_shape=[jax.ShapeDtypeStruct((s, d), F32), jax.ShapeDtypeStruct((s, f), BF16),
                   jax.ShapeDtypeStruct((s, f), BF16)],
        scratch_shapes=[pltpu.VMEM((tm, d), BF16), pltpu.VMEM((tm, d), F32)],
        compiler_params=_params(("parallel", "arbitrary")),
    )(x1, nw, wgt, wut, wd)


def ffn_backward_act(dx2, g, u, wd, name, after=None):
    s, d = dx2.shape
    f = wd.shape[0]
    tm, tf = _tile(s, 256), _tile(f, 2816)
    specs, ops = _after(after)

    def body(dx_ref, g_ref, u_ref, wd_ref, *rest):
        dg_ref, du_ref, a_ref = rest[len(ops):]
        da = _nt(dx_ref[...], wd_ref[...])
        gv = g_ref[...].astype(F32)
        uv = u_ref[...].astype(F32)
        sg = _sigmoid(gv)
        silu = gv * sg
        dg_ref[...] = (da * uv * (sg * (1.0 + gv * (1.0 - sg)))).astype(BF16)
        du_ref[...] = (da * silu).astype(BF16)
        a_ref[...] = (silu * uv).astype(BF16)

    blk = pl.BlockSpec((tm, tf), lambda i, j: (i, j))
    return pl.pallas_call(
        body, name=name, grid=(s // tm, f // tf),
        in_specs=[pl.BlockSpec((tm, d), lambda i, j: (i, 0)), blk, blk, pl.BlockSpec((tf, d), lambda i, j: (j, 0))] + specs,
        out_specs=[blk, blk, blk],
        out_shape=[jax.ShapeDtypeStruct((s, f), BF16)] * 3,
        compiler_params=_params(("parallel", "parallel")),
    )(dx2, g, u, wd, *ops)


def rms_backward(dzs, wts, x, nw, dres, name, tm, after=None):
    s, d = x.shape
    nz = len(dzs)
    specs, ops = _after(after)

    def body(*refs):
        dz_refs, w_refs = refs[:nz], refs[nz:2 * nz]
        x_ref, nw_ref, dres_ref = refs[2 * nz:2 * nz + 3]
        dx_ref, dxb_ref, h_ref, dnw_ref = refs[2 * nz + 3 + len(ops):]
        dh = _nn(dz_refs[0][...], w_refs[0][...])
        for k in range(1, nz):
            dh = dh + _nn(dz_refs[k][...], w_refs[k][...])
        xv = x_ref[...]
        r = lax.rsqrt(jnp.mean(xv * xv, axis=-1, keepdims=True) + EPS)
        xhat = xv * r
        nwv = nw_ref[...]
        h_ref[...] = (xhat * nwv).astype(BF16)

        @pl.when(pl.program_id(0) == 0)
        def _():
            dnw_ref[...] = jnp.zeros_like(dnw_ref)

        dnw_ref[...] += jnp.sum(dh * xhat, axis=0, keepdims=True)
        gdh = dh * nwv
        dx = dres_ref[...] + r * (gdh - xhat * jnp.mean(gdh * xhat, axis=-1, keepdims=True))
        dx_ref[...] = dx
        dxb_ref[...] = dx.astype(BF16)

    row = pl.BlockSpec((tm, d), lambda i: (i, 0))
    in_specs = [pl.BlockSpec((tm, dz.shape[1]), lambda i: (i, 0)) for dz in dzs]
    in_specs += [pl.BlockSpec(w.shape, lambda i: (0, 0)) for w in wts]
    in_specs += [row, pl.BlockSpec((1, d), lambda i: (0, 0)), row] + specs
    return pl.pallas_call(
        body, name=name, grid=(s // tm,),
        in_specs=in_specs,
        out_specs=[row, row, row, pl.BlockSpec((1, d), lambda i: (0, 0))],
        out_shape=[jax.ShapeDtypeStruct((s, d), F32), jax.ShapeDtypeStruct((s, d), BF16),
                   jax.ShapeDtypeStruct((s, d), BF16), jax.ShapeDtypeStruct((1, d), F32)],
        compiler_params=_params(("arbitrary",)),
    )(*dzs, *wts, x, nw, dres, *ops)


def loss_head(x, nw, target, name):
    s, d = x.shape
    tm = _tile(s, 512)

    def body(x_ref, nw_ref, t_ref, loss_ref, dx_ref, dxb_ref, dnw_ref):
        xv = x_ref[...]
        r = lax.rsqrt(jnp.mean(xv * xv, axis=-1, keepdims=True) + EPS)
        xhat = xv * r
        nwv = nw_ref[...]
        err = xhat * nwv - t_ref[...]

        @pl.when(pl.program_id(0) == 0)
        def _():
            dnw_ref[...] = jnp.zeros_like(dnw_ref)
            loss_ref[...] = jnp.zeros_like(loss_ref)

        part = jnp.sum(jnp.sum(err * err, axis=-1, keepdims=True), axis=0, keepdims=True) * (0.5 / d)
        loss_ref[...] += jnp.broadcast_to(part, loss_ref.shape)
        dout = err * (1.0 / d)
        dnw_ref[...] += jnp.sum(dout * xhat, axis=0, keepdims=True)
        gdh = dout * nwv
        dx = r * (gdh - xhat * jnp.mean(gdh * xhat, axis=-1, keepdims=True))
        dx_ref[...] = dx
        dxb_ref[...] = dx.astype(BF16)

    row = pl.BlockSpec((tm, d), lambda i: (i, 0))
    return pl.pallas_call(
        body, name=name, grid=(s // tm,),
        in_specs=[row, pl.BlockSpec((1, d), lambda i: (0, 0)), row],
        out_specs=[pl.BlockSpec((1, LANES), lambda i: (0, 0)), row, row, pl.BlockSpec((1, d), lambda i: (0, 0))],
        out_shape=[jax.ShapeDtypeStruct((1, LANES), F32), jax.ShapeDtypeStruct((s, d), F32),
                   jax.ShapeDtypeStruct((s, d), BF16), jax.ShapeDtypeStruct((1, d), F32)],
        compiler_params=_params(("arbitrary",)),
    )(x, nw, target)


def _shift_down(x, k):
    return jnp.where(_iota2(x.shape, 0) >= k, pltpu.roll(x, k, axis=0), 0.0)


def _shift_up(x, k):
    s = x.shape[0]
    return jnp.where(_iota2(x.shape, 0) < s - k, pltpu.roll(x, s - k, axis=0), 0.0)


CONV_TILE = 256


def conv_forward(proj, cw, cb, name):
    s = proj.shape[0]
    tn = CONV_TILE
    off = OFF_XBC // tn

    def body(u_ref, w_ref, b_ref, o_ref):
        u = u_ref[...]
        pre = b_ref[...] + w_ref[CONV_WIDTH - 1:CONV_WIDTH, :] * u
        for i in range(CONV_WIDTH - 1):
            pre = pre + w_ref[i:i + 1, :] * _shift_down(u, CONV_WIDTH - 1 - i)
        o_ref[...] = pre * _sigmoid(pre)

    return pl.pallas_call(
        body, name=name, grid=(CONV_DIM // tn,),
        in_specs=[pl.BlockSpec((s, tn), lambda j: (0, off + j)), pl.BlockSpec((8, tn), lambda j: (0, j)),
                  pl.BlockSpec((1, tn), lambda j: (0, j))],
        out_specs=pl.BlockSpec((s, tn), lambda j: (0, j)),
        out_shape=jax.ShapeDtypeStruct((s, CONV_DIM), F32),
        compiler_params=_params(("parallel",)),
    )(proj, cw, cb)


def conv_backward(proj, dxc, cw, cb, dproj, name, after=None):
    s = proj.shape[0]
    tn = CONV_TILE
    off = OFF_XBC // tn

    specs, ops = _after(after)

    def body(u_ref, d_ref, w_ref, b_ref, *rest):
        du_ref, dw_ref, db_ref = rest[-3:]
        u = u_ref[...]
        shifted = [_shift_down(u, CONV_WIDTH - 1 - i) for i in range(CONV_WIDTH - 1)] + [u]
        pre = b_ref[...] + w_ref[CONV_WIDTH - 1:CONV_WIDTH, :] * u
        for i in range(CONV_WIDTH - 1):
            pre = pre + w_ref[i:i + 1, :] * shifted[i]
        sg = _sigmoid(pre)
        dpre = d_ref[...] * (sg * (1.0 + pre * (1.0 - sg)))
        du = w_ref[CONV_WIDTH - 1:CONV_WIDTH, :] * dpre
        for i in range(CONV_WIDTH - 1):
            du = du + w_ref[i:i + 1, :] * _shift_up(dpre, CONV_WIDTH - 1 - i)
        du_ref[...] = du.astype(BF16)
        rows = [jnp.sum(dpre * shifted[i], axis=0, keepdims=True) for i in range(CONV_WIDTH)]
        rows.append(jnp.zeros((8 - CONV_WIDTH, tn), F32))
        dw_ref[...] = jnp.concatenate(rows, axis=0)
        db_ref[...] = jnp.sum(dpre, axis=0, keepdims=True)

    return pl.pallas_call(
        body, name=name, grid=(CONV_DIM // tn,),
        in_specs=[pl.BlockSpec((s, tn), lambda j: (0, off + j)), pl.BlockSpec((s, tn), lambda j: (0, j)),
                  pl.BlockSpec((8, tn), lambda j: (0, j)), pl.BlockSpec((1, tn), lambda j: (0, j)), ANY] + specs,
        out_specs=[pl.BlockSpec((s, tn), lambda j: (0, off + j)), pl.BlockSpec((8, tn), lambda j: (0, j)),
                   pl.BlockSpec((1, tn), lambda j: (0, j))],
        out_shape=[jax.ShapeDtypeStruct(dproj.shape, BF16), jax.ShapeDtypeStruct((8, CONV_DIM), F32),
                   jax.ShapeDtypeStruct((1, CONV_DIM), F32)],
        input_output_aliases={4: 0},
        compiler_params=_params(("parallel",)),
    )(proj, dxc, cw, cb, dproj, *ops)


def _pool_lane_window(shape):
    grp = _iota2(shape, 1) // (POOL_WIDTH // len(POOL_WINDOWS))
    win = jnp.full(shape, POOL_WINDOWS[-1], jnp.int32)
    for gi in range(len(POOL_WINDOWS) - 2, -1, -1):
        win = jnp.where(grp == gi, POOL_WINDOWS[gi], win)
    return grp, win


def _pool_select(grp, sums):
    out = sums[-1]
    for gi in range(len(sums) - 2, -1, -1):
        out = jnp.where(grp == gi, sums[gi], out)
    return out


def _pool_pooled(p):
    grp, win = _pool_lane_window(p.shape)
    inv_count = 1.0 / jnp.minimum(_iota2(p.shape, 0) + 1, win).astype(F32)
    sums, acc, k = [], p, 1
    for _ in POOL_WINDOWS:
        acc = acc + _shift_down(acc, k)
        sums.append(acc)
        k *= 2
    return _pool_select(grp, sums) * inv_count - p, grp, inv_count


def pool_forward(proj, wbd, pb, ps, y_all, name, after=None):
    s = proj.shape[0]
    specs, ops = _after(after)

    def body(p_ref, w_ref, b_ref, s_ref, *rest):
        o_ref = rest[-1]
        pooled, _, _ = _pool_pooled(p_ref[...])
        mixed = _nn(pooled.astype(BF16), w_ref[...]) + b_ref[...]
        o_ref[...] = (mixed * s_ref[...]).astype(BF16)

    vec = pl.BlockSpec((1, POOL_WIDTH), lambda j: (0, 0))
    return pl.pallas_call(
        body, name=name, grid=(1,),
        in_specs=[pl.BlockSpec((s, POOL_WIDTH), lambda j: (0, OFF_P // POOL_WIDTH)),
                  pl.BlockSpec((POOL_WIDTH, POOL_WIDTH), lambda j: (0, 0)), vec, vec, ANY] + specs,
        out_specs=pl.BlockSpec((s, POOL_WIDTH), lambda j: (0, (SSD_WIDTH + SB_WIDTH) // POOL_WIDTH)),
        out_shape=jax.ShapeDtypeStruct(y_all.shape, BF16),
        input_output_aliases={4: 0},
        compiler_params=_params(("arbitrary",)),
    )(proj, wbd, pb, ps, y_all, *ops)


def pool_backward(proj, dyall, wbd, pb, ps, dproj, name):
    s = proj.shape[0]

    def body(p_ref, dy_ref, w_ref, b_ref, s_ref, _, dp_ref, dw_ref, db_ref, ds_ref):
        pooled, grp, inv_count = _pool_pooled(p_ref[...])
        pooled_b = pooled.astype(BF16)
        mixed = _nn(pooled_b, w_ref[...]) + b_ref[...]
        dy = dy_ref[...]
        ds_ref[...] = jnp.sum(dy * mixed, axis=0, keepdims=True)
        dmixed = dy * s_ref[...]
        db_ref[...] = jnp.sum(dmixed, axis=0, keepdims=True)
        dmixed_b = dmixed.astype(BF16)
        dw_ref[...] = _tn(pooled_b, dmixed_b)
        dpooled = _nt(dmixed_b, w_ref[...])
        sums, acc, k = [], dpooled * inv_count, 1
        for _ in POOL_WINDOWS:
            acc = acc + _shift_up(acc, k)
            sums.append(acc)
            k *= 2
        dp_ref[...] = (_pool_select(grp, sums) - dpooled).astype(BF16)

    vec = pl.BlockSpec((1, POOL_WIDTH), lambda j: (0, 0))
    mat = pl.BlockSpec((POOL_WIDTH, POOL_WIDTH), lambda j: (0, 0))
    pcol = pl.BlockSpec((s, POOL_WIDTH), lambda j: (0, OFF_P // POOL_WIDTH))
    return pl.pallas_call(
        body, name=name, grid=(1,),
        in_specs=[pcol, pl.BlockSpec((s, POOL_WIDTH), lambda j: (0, (SSD_WIDTH + SB_WIDTH) // POOL_WIDTH)), mat, vec, vec, ANY],
        out_specs=[pcol, mat, vec, vec],
        out_shape=[jax.ShapeDtypeStruct(dproj.shape, BF16), jax.ShapeDtypeStruct((POOL_WIDTH, POOL_WIDTH), F32),
                   jax.ShapeDtypeStruct((1, POOL_WIDTH), F32), jax.ShapeDtypeStruct((1, POOL_WIDTH), F32)],
        input_output_aliases={5: 0},
        compiler_params=_params(("arbitrary",)),
    )(proj, dyall, wbd, pb, ps, dproj)


N_PAIRS = SSD_HEADS // 2


def _ssd_common(xc, dtraw, dtb, alog):
    c = CHUNK
    dt = _softplus(dtraw + dtb)
    a = -jnp.exp(alog)
    ltri = (_iota2((c, c), 0) >= _iota2((c, c), 1)).astype(BF16)
    acum = _split_dot_left(ltri, dt * a)
    expand = (_iota2((c, SSD_WIDTH), 1) // HEAD_DIM == _iota2((c, SSD_WIDTH), 0)).astype(BF16)
    expand_wide = (_iota2((c, SSD_HEADS * c), 1) // c == _iota2((c, SSD_HEADS * c), 0)).astype(BF16)
    acum_x = _split_dot(acum, expand)
    dt_x = _split_dot(dt, expand)
    alast_x = acum_x[c - 1:c, :]
    return dict(dt=dt, a=a, acum=acum, acum_x=acum_x, dt_x=dt_x, ea_x=jnp.exp(acum_x),
                dte_x=jnp.exp(alast_x - acum_x), eal_x=jnp.exp(alast_x),
                acol=_split_dot(acum, expand_wide), acum_t=acum.T,
                xs=xc[:, :SSD_WIDTH], causal=_iota2((c, c), 0) >= _iota2((c, c), 1),
                left=_iota2((c, c), 1) < HEAD_DIM)


def _ssd_group(xc, g):
    b = xc[:, SSD_WIDTH + D_STATE * g:SSD_WIDTH + D_STATE * (g + 1)]
    cm = xc[:, SSD_WIDTH + 2 * D_STATE + D_STATE * g:SSD_WIDTH + 2 * D_STATE + D_STATE * (g + 1)]
    return b, cm


def _ssd_decay(q, hh):
    col = q["acol"][:, CHUNK * hh:CHUNK * (hh + 1)]
    row = q["acum_t"][hh:hh + 1, :]
    return jnp.where(q["causal"], jnp.exp(jnp.minimum(col - row, 0.0)), 0.0)


def ssd_forward(proj, xc, dtb, alog, dskip_x, nw, name):
    s = xc.shape[0]
    nc = s // CHUNK

    def body(xc_ref, zdt_ref, dtb_ref, alog_ref, dsk_ref, nw_ref, y_ref, yc_ref, st_ref, state):
        @pl.when(pl.program_id(0) == 0)
        def _():
            state[...] = jnp.zeros_like(state)

        xcv = xc_ref[...]
        q = _ssd_common(xcv, zdt_ref[:, SSD_WIDTH:SSD_WIDTH + LANES], dtb_ref[...], alog_ref[...])
        x = q["xs"] * q["dt_x"]
        xb = x.astype(BF16)
        xd = (x * q["dte_x"]).astype(BF16)
        pieces = []
        for g in range(2):
            bg, cg = _ssd_group(xcv, g)
            bgb, cgb = bg.astype(BF16), cg.astype(BF16)
            cb = _nt(cgb, bgb)
            bgt = bg.T.astype(BF16)
            for pr in (2 * g, 2 * g + 1):
                sl = slice(CHUNK * pr, CHUNK * (pr + 1))
                st = state[pr]
                st_ref[0, pr] = st
                yp = _nn(cgb, st.astype(BF16)) * q["ea_x"][:, sl]
                for k, hh in enumerate((2 * pr, 2 * pr + 1)):
                    w = (cb * _ssd_decay(q, hh)).astype(BF16)
                    mask = q["left"] if k == 0 else jnp.logical_not(q["left"])
                    yp = yp + _nn(w, jnp.where(mask, xb[:, sl], jnp.zeros_like(xb[:, sl])))
                state[pr] = st * q["eal_x"][:, sl] + _nn(bgt, xd[:, sl])
                pieces.append(yp)
        y = jnp.concatenate(pieces, axis=1) + q["xs"] * dsk_ref[...]
        yc_ref[...] = y
        zv = zdt_ref[:, :SSD_WIDTH]
        yg = y * (zv * _sigmoid(zv))
        r = lax.rsqrt(jnp.mean(yg * yg, axis=-1, keepdims=True) + EPS)
        y_ref[...] = (yg * r * nw_ref[...]).astype(BF16)

    vec = lambda n: pl.BlockSpec((1, n), lambda c: (0, 0))
    return pl.pallas_call(
        body, name=name, grid=(nc,),
        in_specs=[pl.BlockSpec((CHUNK, CONV_DIM), lambda c: (c, 0)),
                  pl.BlockSpec((CHUNK, ZDT), lambda c: (c, OFF_Z // ZDT)),
                  vec(LANES), vec(LANES), vec(SSD_WIDTH), vec(SSD_WIDTH)],
        out_specs=[pl.BlockSpec((CHUNK, SSD_WIDTH), lambda c: (c, 0)), pl.BlockSpec((CHUNK, SSD_WIDTH), lambda c: (c, 0)),
                   pl.BlockSpec((1, N_PAIRS, D_STATE, CHUNK), lambda c: (c, 0, 0, 0))],
        out_shape=[jax.ShapeDtypeStruct((s, D_MODEL), BF16), jax.ShapeDtypeStruct((s, SSD_WIDTH), F32),
                   jax.ShapeDtypeStruct((nc, N_PAIRS, D_STATE, CHUNK), F32)],
        scratch_shapes=[pltpu.VMEM((N_PAIRS, D_STATE, CHUNK), F32)],
        compiler_params=_params(("arbitrary",)),
    )(xc, proj, dtb, alog, dskip_x, nw)


def ssd_backward(proj, xc, ycore, dyall, states, dtb, alog, dskip_x, nw, name):
    s = xc.shape[0]
    nc = s // CHUNK
    c = CHUNK

    def body(xc_ref, zdt_ref, yc_ref, dy_ref, st_ref, dtb_ref, alog_ref, dsk_ref, nw_ref,
             dxc_ref, dzdt_ref, dnw_ref, ddsk_ref, ddtb_ref, dalog_ref, dstate):
        @pl.when(pl.program_id(0) == 0)
        def _():
            dstate[...] = jnp.zeros_like(dstate)
            dnw_ref[...] = jnp.zeros_like(dnw_ref)
            ddsk_ref[...] = jnp.zeros_like(ddsk_ref)
            ddtb_ref[...] = jnp.zeros_like(ddtb_ref)
            dalog_ref[...] = jnp.zeros_like(dalog_ref)

        xcv = xc_ref[...]
        dtraw = zdt_ref[:, SSD_WIDTH:SSD_WIDTH + LANES]
        q = _ssd_common(xcv, dtraw, dtb_ref[...], alog_ref[...])
        xs = q["xs"]
        x = xs * q["dt_x"]
        zv, yc, dy, nwv = zdt_ref[:, :SSD_WIDTH], yc_ref[...], dy_ref[...], nw_ref[...]
        sgz = _sigmoid(zv)
        siluz = zv * sgz
        yg = yc * siluz
        r = lax.rsqrt(jnp.mean(yg * yg, axis=-1, keepdims=True) + EPS)
        dnw_ref[...] += jnp.sum(dy * yg * r, axis=0, keepdims=True)
        g1 = dy * nwv
        dyg = r * (g1 - yg * (r * r) * jnp.mean(g1 * yg, axis=-1, keepdims=True))
        dyv = dyg * siluz
        dz = (dyg * yc * (sgz * (1.0 + zv * (1.0 - sgz)))).astype(BF16)
        ddsk_ref[...] += jnp.sum(dyv * xs, axis=0, keepdims=True)
        dye = dyv * q["ea_x"]
        dx_parts, yoff_parts, u_parts, v_parts, e_parts = [], [], [], [], []
        db_parts, dc_parts = [], []
        for g in range(2):
            bg, cg = _ssd_group(xcv, g)
            bgb, cgb = bg.astype(BF16), cg.astype(BF16)
            cb = _nt(cgb, bgb)
            cgt = cg.T.astype(BF16)
            dgsum = jnp.zeros((c, c), F32)
            dbg = jnp.zeros((c, D_STATE), F32)
            dcg = jnp.zeros((c, D_STATE), F32)
            for pr in (2 * g, 2 * g + 1):
                sl = slice(c * pr, c * (pr + 1))
                st = st_ref[0, pr]
                dst = dstate[pr]
                stb, dstb = st.astype(BF16), dst.astype(BF16)
                xp = x[:, sl]
                xpb = xp.astype(BF16)
                dyp = dyv[:, sl]
                xdp = xp * q["dte_x"][:, sl]
                yoff_parts.append(_nn(cgb, stb) * q["ea_x"][:, sl])
                rr = _nn(bgb, dstb)
                dxp = rr * q["dte_x"][:, sl]
                u_parts.append(rr * xdp)
                v_parts.append(dst * st * q["eal_x"][:, sl])
                for k, hh in enumerate((2 * pr, 2 * pr + 1)):
                    decay = _ssd_decay(q, hh)
                    w = cb * decay
                    mask = q["left"] if k == 0 else jnp.logical_not(q["left"])
                    dym = jnp.where(mask, dyp, 0.0).astype(BF16)
                    dw = _nt(dym, xpb)
                    dgsum = dgsum + dw * decay
                    e_parts.append(dw * w)
                    dxp = dxp + _nn(w.T.astype(BF16), dym)
                dyeb = dye[:, sl].astype(BF16)
                dcg = dcg + _nt(dyeb, stb)
                dbg = dbg + _nt(xdp.astype(BF16), dstb)
                dstate[pr] = dst * q["eal_x"][:, sl] + _nn(cgt, dyeb)
                dx_parts.append(dxp)
            dcg = dcg + _nn(dgsum.astype(BF16), bgb)
            dbg = dbg + _nn(dgsum.T.astype(BF16), cgb)
            db_parts.append(dbg)
            dc_parts.append(dcg)
        dx = jnp.concatenate(dx_parts, axis=1)
        yoff = jnp.concatenate(yoff_parts, axis=1)
        u = jnp.concatenate(u_parts, axis=1)
        v = jnp.concatenate(v_parts, axis=1)
        reduce_heads = (_iota2((SSD_WIDTH, c), 0) // HEAD_DIM == _iota2((SSD_WIDTH, c), 1)).astype(BF16)
        to_head = (_iota2((SSD_HEADS * c, c), 0) // c == _iota2((SSD_HEADS * c, c), 1)).astype(BF16)
        da = _split_dot(dyv * yoff - u, reduce_heads, 2)
        da = da + _split_dot(jnp.concatenate(e_parts, axis=1), to_head, 2)
        da = da - _split_dot(jnp.concatenate(e_parts, axis=0), to_head, 2, dot=_tn)
        dalast = jnp.sum(_split_dot(u + v, reduce_heads, 2), axis=0, keepdims=True)
        da = da + jnp.where(_iota2((c, c), 0) == c - 1, dalast, 0.0)
        utri = (_iota2((c, c), 1) >= _iota2((c, c), 0)).astype(BF16)
        dda = _split_dot_left(utri, da)
        ddt = dda * q["a"] + _split_dot(dx * xs, reduce_heads, 2)
        dalog_ref[...] += jnp.sum(dda * q["dt"], axis=0, keepdims=True) * q["a"]
        ddtraw = jnp.where(_iota2((c, c), 1) < SSD_HEADS, ddt * _sigmoid(dtraw + dtb_ref[...]), 0.0)
        ddtb_ref[...] += jnp.sum(ddtraw, axis=0, keepdims=True)
        dzdt_ref[...] = jnp.concatenate([dz, ddtraw.astype(BF16), jnp.zeros((c, ZDT - SSD_WIDTH - LANES), BF16)], axis=1)
        dxs = dx * q["dt_x"] + dyv * dsk_ref[...]
        dxc_ref[...] = jnp.concatenate([dxs] + db_parts + dc_parts, axis=1)

    rev = lambda i: nc - 1 - i
    vec = lambda n: pl.BlockSpec((1, n), lambda i: (0, 0))
    wide = pl.BlockSpec((c, SSD_WIDTH), lambda i: (rev(i), 0))
    zdt = pl.BlockSpec((c, ZDT), lambda i: (rev(i), OFF_Z // ZDT))
    return pl.pallas_call(
        body, name=name, grid=(nc,),
        in_specs=[pl.BlockSpec((c, CONV_DIM), lambda i: (rev(i), 0)), zdt, wide, wide,
                  pl.BlockSpec((1, N_PAIRS, D_STATE, c), lambda i: (rev(i), 0, 0, 0)),
                  vec(LANES), vec(LANES), vec(SSD_WIDTH), vec(SSD_WIDTH)],
        out_specs=[pl.BlockSpec((c, CONV_DIM), lambda i: (rev(i), 0)), zdt,
                   vec(SSD_WIDTH), vec(SSD_WIDTH), vec(LANES), vec(LANES)],
        out_shape=[jax.ShapeDtypeStruct((s, CONV_DIM), F32), jax.ShapeDtypeStruct((s, D_INP), BF16),
                   jax.ShapeDtypeStruct((1, SSD_WIDTH), F32),
                   jax.ShapeDtypeStruct((1, SSD_WIDTH), F32), jax.ShapeDtypeStruct((1, LANES), F32),
                   jax.ShapeDtypeStruct((1, LANES), F32)],
        scratch_shapes=[pltpu.VMEM((N_PAIRS, D_STATE, c), F32)],
        compiler_params=_params(("arbitrary",)),
    )(xc, proj, ycore, dyall, states, dtb, alog, dskip_x, nw)


SB_BLOCK = 256
SB_SCALE = HEAD_DIM ** -0.5


def _sb_weights(qm, kb, diagonal, run_lk, strict_after):
    z = _nt(qm, kb)
    nz = -z
    tail = jnp.log(1.0 + jnp.exp(jnp.minimum(z, nz)))
    ls = jnp.minimum(z, 0.0) - tail
    lk = jnp.minimum(nz, 0.0) - tail
    if diagonal:
        valid = _iota2(z.shape, 1) < _iota2(z.shape, 0)
        lk = jnp.where(valid, lk, 0.0)
    w = jnp.exp(ls + _split_dot(lk, strict_after, 2) + run_lk)
    if diagonal:
        w = jnp.where(valid, w, 0.0)
    return ls, lk, w


def _sb_sweep(i, block, init):
    return lax.fori_loop(1, i + 1, lambda jj, carry: block(i - jj, carry, False), block(i, init, True))


def sb_forward(proj, y_all, name):
    s = proj.shape[0]
    t = SB_BLOCK
    nq = s // t

    def body(q_ref, k_ref, v_ref, _, y_ref, o_ref):
        i = pl.program_id(1)
        left = _iota2((t, LANES), 1) < HEAD_DIM
        qv = q_ref[...] * SB_SCALE
        zero = jnp.zeros_like(qv)
        qms = (jnp.where(left, qv, zero).astype(BF16), jnp.where(left, zero, qv).astype(BF16))
        strict_after = (_iota2((t, t), 0) > _iota2((t, t), 1)).astype(BF16)

        def block(j, carry, diagonal):
            o, runs = carry[0], carry[1:]
            rows = pl.ds(pl.multiple_of(j * t, t), t)
            kb = k_ref[rows, :].astype(BF16)
            vv = v_ref[rows, :]
            new_runs = []
            for k in range(2):
                _, lk, w = _sb_weights(qms[k], kb, diagonal, runs[k], strict_after)
                vm = jnp.where(left if k == 0 else jnp.logical_not(left), vv, 0.0).astype(BF16)
                o = o + _nn(w.astype(BF16), vm)
                new_runs.append(runs[k] + jnp.sum(lk, axis=1, keepdims=True))
            return (o, *new_runs)

        init = (jnp.zeros((t, LANES), F32), jnp.zeros((t, 1), F32), jnp.zeros((t, 1), F32))
        o = _sb_sweep(i, block, init)[0]
        o_ref[...] = o
        y_ref[...] = o.astype(BF16)

    return pl.pallas_call(
        body, name=name, grid=(2, nq),
        in_specs=[pl.BlockSpec((t, LANES), lambda p, i: (i, 3 * p)),
                  pl.BlockSpec((s, LANES), lambda p, i: (0, 3 * p + 1)),
                  pl.BlockSpec((s, LANES), lambda p, i: (0, 3 * p + 2)), ANY],
        out_specs=[pl.BlockSpec((t, LANES), lambda p, i: (i, SSD_WIDTH // LANES + p)),
                   pl.BlockSpec((t, LANES), lambda p, i: (i, p))],
        out_shape=[jax.ShapeDtypeStruct(y_all.shape, BF16), jax.ShapeDtypeStruct((s, SB_WIDTH), F32)],
        input_output_aliases={3: 0},
        compiler_params=_params(("parallel", "arbitrary")),
    )(proj, proj, proj, y_all)


def sb_backward(proj, o, dyall, dproj, name, after=None):
    s = proj.shape[0]
    t = SB_BLOCK
    nq = s // t

    specs, ops = _after(after)

    def body(q_ref, k_ref, v_ref, o_ref, do_ref, *rest):
        dqkv_ref, dk_acc, dv_acc = rest[-3:]
        dk_acc[...] = jnp.zeros_like(dk_acc)
        dv_acc[...] = jnp.zeros_like(dv_acc)
        left = _iota2((t, LANES), 1) < HEAD_DIM
        lane_masks = (left, jnp.logical_not(left))
        strict_after = (_iota2((t, t), 0) > _iota2((t, t), 1)).astype(BF16)
        from_here = (_iota2((t, t), 0) >= _iota2((t, t), 1)).astype(BF16)

        def query_block(i, _):
            qrows = pl.ds(pl.multiple_of(i * t, t), t)
            qv = q_ref[qrows, :] * SB_SCALE
            dov = do_ref[qrows, :]
            zero = jnp.zeros_like(qv)
            qb = qv.astype(BF16)
            dob = dov.astype(BF16)
            prod = dob.astype(F32) * o_ref[qrows, :]
            qms = [jnp.where(m, qv, zero).astype(BF16) for m in lane_masks]
            doms = [jnp.where(m, dov, zero).astype(BF16) for m in lane_masks]
            deltas = [jnp.sum(jnp.where(m, prod, zero), axis=1, keepdims=True) for m in lane_masks]

            def block(j, carry, diagonal):
                dq = carry[0]
                run_lk, run_e = carry[1:3], carry[3:5]
                rows = pl.ds(pl.multiple_of(j * t, t), t)
                kb = k_ref[rows, :].astype(BF16)
                vb = v_ref[rows, :].astype(BF16)
                dkj = jnp.zeros((t, LANES), F32)
                dvj = jnp.zeros((t, LANES), F32)
                new_lk, new_e = [], []
                for k in range(2):
                    ls, lk, w = _sb_weights(qms[k], kb, diagonal, run_lk[k], strict_after)
                    wb = w.astype(BF16)
                    e = _nt(doms[k], vb) * wb.astype(F32)
                    before = deltas[k] - _split_dot(e, from_here, 2) - run_e[k]
                    dz = e - jnp.exp(ls) * (e + before)
                    if diagonal:
                        dz = jnp.where(_iota2(dz.shape, 1) < _iota2(dz.shape, 0), dz, 0.0)
                    dz = dz.astype(BF16)
                    m = lane_masks[k]
                    dvj = dvj + jnp.where(m, _tn(wb, dob), 0.0)
                    dkj = dkj + jnp.where(m, _tn(dz, qb), 0.0)
                    dq = dq + jnp.where(m, _nn(dz, kb), 0.0)
                    new_lk.append(run_lk[k] + jnp.sum(lk, axis=1, keepdims=True))
                    new_e.append(run_e[k] + jnp.sum(e, axis=1, keepdims=True))
                dk_acc[rows, :] += dkj
                dv_acc[rows, :] += dvj
                return (dq, *new_lk, *new_e)

            col = jnp.zeros((t, 1), F32)
            dq = _sb_sweep(i, block, (jnp.zeros((t, LANES), F32), col, col, col, col))[0]
            dqkv_ref[qrows, 0:LANES] = (dq * SB_SCALE).astype(BF16)
            return 0

        lax.fori_loop(0, nq, query_block, 0)
        dqkv_ref[:, LANES:2 * LANES] = dk_acc[...].astype(BF16)
        dqkv_ref[:, 2 * LANES:3 * LANES] = dv_acc[...].astype(BF16)

    col = lambda f: pl.BlockSpec((s, LANES), f)
    return pl.pallas_call(
        body, name=name, grid=(2,),
        in_specs=[col(lambda p: (0, 3 * p)), col(lambda p: (0, 3 * p + 1)), col(lambda p: (0, 3 * p + 2)),
                  col(lambda p: (0, p)), col(lambda p: (0, SSD_WIDTH // LANES + p)), ANY] + specs,
        out_specs=pl.BlockSpec((s, 3 * LANES), lambda p: (0, p)),
        out_shape=jax.ShapeDtypeStruct(dproj.shape, BF16),
        input_output_aliases={5: 0},
        scratch_shapes=[pltpu.VMEM((s, LANES), F32), pltpu.VMEM((s, LANES), F32)],
        compiler_params=_params(("parallel",)),
    )(proj, proj, proj, o, dyall, dproj, *ops)


def adamw(w, g, m, v, name):
    b, r, c = w.shape
    tr = max([t for t in range(8, min(r, 512) + 1, 8) if r % t == 0], default=r)

    def body(w_ref, g_ref, m_ref, v_ref, d_ref, nm_ref, nv_ref):
        gv = g_ref[...]
        nm = ADAM_B1 * m_ref[...] + (1.0 - ADAM_B1) * gv
        nv = ADAM_B2 * v_ref[...] + (1.0 - ADAM_B2) * (gv * gv)
        m_hat = nm / (1.0 - ADAM_B1 ** ADAM_STEP)
        v_hat = nv / (1.0 - ADAM_B2 ** ADAM_STEP)
        d_ref[...] = -ADAM_LR * (m_hat / (jnp.sqrt(v_hat) + ADAM_EPS) + ADAM_WD * w_ref[...])
        nm_ref[...] = nm
        nv_ref[...] = nv

    blk = pl.BlockSpec((1, tr, c), lambda i, j: (i, j, 0))
    return pl.pallas_call(
        body, name=name, grid=(b, r // tr),
        in_specs=[blk] * 4, out_specs=[blk] * 3,
        out_shape=[jax.ShapeDtypeStruct(w.shape, F32)] * 3,
        compiler_params=_params(("parallel", "parallel")),
    )(w, g, m, v)


def _position():
    return lax.axis_index("x"), lax.axis_index("y"), lax.axis_index("c")


def _flipped(pos, flip):
    return tuple((1 - p) if f else p for p, f in zip(pos, flip))


FLIP_C = (0, 0, 1)
CHIP_FLIPS = {1: (0, 1, 0), 2: (1, 0, 0), 3: (1, 1, 0)}
SHARD_ROWS = (SHARD_IN, SHARD_OUT, SHARD_FF, SHARD_FF, SHARD_FF)


def _rows(start, size):
    return pl.ds(pl.multiple_of(start, 16), size)


HBM = pl.BlockSpec(memory_space=pltpu.HBM)
SEM = pl.BlockSpec(memory_space=pltpu.SEMAPHORE)
EFFECT = pltpu.SideEffectType.DATAFLOW_SIDE_EFFECTING


def _in_hbm(a):
    return pltpu.with_memory_space_constraint(a, pltpu.HBM)


def _landing(shape, dtype):
    return _in_hbm(lax.empty(shape, dtype))


def _copies(plan, pos, src_refs, land_refs, send_sems, recv_sems):
    return [pltpu.make_async_remote_copy(src_ref=src, dst_ref=dst, send_sem=send_sems.at[k], recv_sem=recv_sems.at[k],
                                         device_id=_flipped(pos, flip), device_id_type=MESH)
            for k, (src, dst, flip) in enumerate(plan(pos, src_refs, land_refs))]


def exchange_start(name, srcs, lands, n, plan, after=None):
    ns, nl = len(srcs), len(lands)
    specs, ops = _after(after)

    def body(*refs):
        src_refs, land_refs = refs[:ns], refs[ns:ns + nl]
        send_sems, recv_sems, token = refs[ns + nl + len(ops)], refs[ns + nl + len(ops) + 1], refs[-1]
        for cp in _copies(plan, _position(), src_refs, land_refs, send_sems, recv_sems):
            cp.start()
        token[...] = jnp.zeros_like(token)

    thru = [pltpu.HBM(a.shape, a.dtype) for a in list(srcs) + list(lands)]
    out = pl.pallas_call(
        body, name=name,
        out_shape=(pltpu.SemaphoreType.DMA((n,)), pltpu.SemaphoreType.DMA((n,)), *thru, jax.ShapeDtypeStruct((8, LANES), F32)),
        in_specs=[HBM] * (ns + nl) + specs,
        out_specs=(SEM, SEM, *([HBM] * (ns + nl)), pl.BlockSpec(memory_space=pltpu.VMEM)),
        input_output_aliases={k: 2 + k for k in range(ns + nl)},
        compiler_params=pltpu.CompilerParams(has_side_effects=EFFECT),
    )(*[_in_hbm(a) for a in srcs], *lands, *ops)
    return out[0], out[1], list(out[2:2 + ns]), list(out[2 + ns:2 + ns + nl]), out[-1]


def exchange_wait(name, started, after, plan):
    send_sems, recv_sems, srcs, lands, _ = started
    ns, nl = len(srcs), len(lands)
    specs, ops = _after(after)

    def body(*refs):
        src_refs, land_refs = refs[:ns], refs[ns:ns + nl]
        send_sems, recv_sems = refs[ns + nl], refs[ns + nl + 1]
        for cp in _copies(plan, _position(), src_refs, land_refs, send_sems, recv_sems):
            cp.wait_send()
            cp.wait_recv()

    out = pl.pallas_call(
        body, name=name,
        out_shape=tuple(pltpu.HBM(a.shape, a.dtype) for a in list(srcs) + list(lands)),
        in_specs=[HBM] * (ns + nl) + [SEM, SEM] + specs,
        out_specs=tuple([HBM] * (ns + nl)),
        input_output_aliases={k: k for k in range(ns + nl)},
        compiler_params=pltpu.CompilerParams(has_side_effects=EFFECT),
    )(*srcs, *lands, send_sems, recv_sems, *ops)
    return list(out[:ns]), list(out[ns:])


def _gather_ici_plan(pos, srcs, lands):
    chip, c = 2 * pos[0] + pos[1], pos[2]
    copies = []
    for src, dst in zip(srcs, lands):
        r = src.shape[0]
        h = r // 2
        for f in (1, 2, 3):
            copies.append((src.at[_rows(c * h, h)], dst.at[_rows(chip * r + c * h, h)], CHIP_FLIPS[f]))
    return copies


def _gather_d2d_plan(pos, srcs, lands):
    chip, c = 2 * pos[0] + pos[1], pos[2]
    copies = []
    for own, dst in zip(srcs, lands):
        r = own.shape[0]
        h = r // 2
        copies.append((own, dst.at[_rows(chip * r, r)], FLIP_C))
        for f in (1, 2, 3):
            at = _rows(lax.bitwise_xor(chip, f) * r + c * h, h)
            copies.append((dst.at[at], dst.at[at], FLIP_C))
    return copies


def gather_ici_start(shards, after=None):
    lands = [_landing((N_CHIPS * a.shape[0], D_MODEL), BF16) for a in shards]
    return exchange_start("gather_ici_start", shards, lands, 3 * len(shards), _gather_ici_plan, after=after)


def gather_d2d_start(shards, fulls, after=None):
    return exchange_start("gather_d2d_start", shards, fulls, 4 * len(shards), _gather_d2d_plan, after=after)


def _reduce_d2d_plan(pos, srcs, lands):
    c = pos[2]
    return [(src.at[:, _rows((1 - c) * (src.shape[1] // 2), src.shape[1] // 2)], dst, FLIP_C) for src, dst in zip(srcs, lands)]


def _reduce_ici_plan(pos, srcs, lands):
    chip = 2 * pos[0] + pos[1]
    return [(src.at[lax.bitwise_xor(chip, f)], dst.at[f - 1], CHIP_FLIPS[f]) for src, dst in zip(srcs, lands) for f in (1, 2, 3)]


def _reduce_swap_plan(pos, srcs, lands):
    c = pos[2]
    copies = []
    for dst in lands:
        h = dst.shape[0] // 2
        at = _rows(c * h, h)
        copies.append((dst.at[at], dst.at[at], FLIP_C))
    return copies


def reduce_d2d_start(grads):
    lands = [_landing((N_CHIPS, g.shape[1] // 2, D_MODEL), BF16) for g in grads]
    return exchange_start("reduce_d2d_start", grads, lands, len(grads), _reduce_d2d_plan)


def reduce_ici_start(chip_sums):
    lands = [_landing((N_CHIPS - 1,) + p.shape[1:], BF16) for p in chip_sums]
    return exchange_start("reduce_ici_start", chip_sums, lands, 3 * len(chip_sums), _reduce_ici_plan)


def reduce_swap_start(mine):
    return exchange_start("reduce_swap_start", [], mine, len(mine), _reduce_swap_plan)


def add_halves(d, recv, half, name):
    nch, r, c = d.shape
    h = r // 2

    def body(half_ref, d_ref, r_ref, o_ref):
        o_ref[...] = (d_ref[...].astype(F32) + r_ref[...].astype(F32)).astype(BF16)

    return pl.pallas_call(
        body, name=name,
        grid_spec=pltpu.PrefetchScalarGridSpec(
            num_scalar_prefetch=1, grid=(nch,),
            in_specs=[pl.BlockSpec((1, h, c), lambda j, hf: (j, hf[0], 0)),
                      pl.BlockSpec((1, h, c), lambda j, hf: (j, 0, 0))],
            out_specs=pl.BlockSpec((1, h, c), lambda j, hf: (j, 0, 0))),
        out_shape=jax.ShapeDtypeStruct(recv.shape, BF16),
        compiler_params=_params(("parallel",)),
    )(half, d, recv)


def add_chips(p, recv, chip, name):
    _, r, c = p.shape

    def body(chip_ref, p_ref, r_ref, o_ref):
        acc = p_ref[0].astype(F32)
        for k in range(N_CHIPS - 1):
            acc = acc + r_ref[k].astype(F32)
        o_ref[...] = acc

    return pl.pallas_call(
        body, name=name,
        grid_spec=pltpu.PrefetchScalarGridSpec(
            num_scalar_prefetch=1, grid=(1,),
            in_specs=[pl.BlockSpec((1, r, c), lambda i, ch: (ch[0], 0, 0)),
                      pl.BlockSpec((N_CHIPS - 1, r, c), lambda i, ch: (0, 0, 0))],
            out_specs=pl.BlockSpec((r, c), lambda i, ch: (ch[1], 0))),
        out_shape=jax.ShapeDtypeStruct((2 * r, c), F32),
        compiler_params=_params(("arbitrary",)),
    )(chip, p, recv)


def adamw_layers(w, gs, m, v, name):
    b, r, c = w.shape
    tr = max([t for t in range(8, min(r, 512) + 1, 8) if r % t == 0], default=r)

    def body(w_ref, m_ref, v_ref, *rest):
        g_refs, (g_ref, d_ref, nm_ref, nv_ref) = rest[:b], rest[b:]
        layer = pl.program_id(0)
        gv = g_refs[0][...]
        for l in range(1, b):
            gv = jnp.where(layer == l, g_refs[l][...], gv)
        nm = ADAM_B1 * m_ref[0] + (1.0 - ADAM_B1) * gv
        nv = ADAM_B2 * v_ref[0] + (1.0 - ADAM_B2) * (gv * gv)
        m_hat = nm / (1.0 - ADAM_B1 ** ADAM_STEP)
        v_hat = nv / (1.0 - ADAM_B2 ** ADAM_STEP)
        g_ref[0] = gv
        d_ref[0] = -ADAM_LR * (m_hat / (jnp.sqrt(v_hat) + ADAM_EPS) + ADAM_WD * w_ref[0])
        nm_ref[0] = nm
        nv_ref[0] = nv

    nr, tc = r // tr, (c if tr < r else _tile(c, 256))
    steps = nr * (c // tc)
    blk = pl.BlockSpec((1, tr, tc), lambda i, j: (i, j % nr, j // nr))
    g_specs = [pl.BlockSpec((tr, tc), lambda i, j, l=l: (jnp.where(i == l, j % nr, jnp.where(i < l, 0, nr - 1)),
                                                         jnp.where(i == l, j // nr, jnp.where(i < l, 0, c // tc - 1))))
               for l in range(b)]
    return pl.pallas_call(
        body, name=name, grid=(b, steps),
        in_specs=[blk] * 3 + g_specs, out_specs=[blk] * 4,
        out_shape=[jax.ShapeDtypeStruct(w.shape, F32)] * 4,
        compiler_params=_params(("arbitrary", "arbitrary")),
    )(w, m, v, *gs)


def small_allreduce(v, name):
    r, c = v.shape

    def body(v_ref, o_ref, buf, send_sems, recv_sems):
        pos = _position()
        me = 4 * pos[0] + 2 * pos[1] + pos[2]
        buf[0] = v_ref[...]
        copies = []
        for f in range(1, 8):
            flip = ((f >> 2) & 1, (f >> 1) & 1, f & 1)
            cp = pltpu.make_async_remote_copy(
                src_ref=v_ref, dst_ref=buf.at[f], send_sem=send_sems.at[f - 1], recv_sem=recv_sems.at[f - 1],
                device_id=_flipped(pos, flip), device_id_type=MESH)
            cp.start()
            copies.append(cp)
        for cp in copies:
            cp.wait()
        acc = buf[me]
        for d in range(1, 8):
            acc = acc + buf[lax.bitwise_xor(me, d)]
        o_ref[...] = acc

    return pl.pallas_call(
        body, name=name,
        in_specs=[pl.BlockSpec(memory_space=pltpu.VMEM)], out_specs=pl.BlockSpec(memory_space=pltpu.VMEM),
        out_shape=jax.ShapeDtypeStruct((r, c), F32),
        scratch_shapes=[pltpu.VMEM((8, r, c), F32), pltpu.SemaphoreType.DMA((7,)), pltpu.SemaphoreType.DMA((7,))],
    )(v)


_IN_SEGMENTS = ((0, 1544, 128), (128, 1800, 128), (256, 2056, 128), (384, 1672, 128), (512, 1928, 128), (640, 2184, 128),
                (OFF_Z, 0, SSD_WIDTH), (OFF_DT, 1536, SSD_HEADS), (OFF_XBC, 512, CONV_DIM), (OFF_P, 2312, POOL_WIDTH))


def _in_column_map():
    m = np.full((D_INP,), -1, np.int64)
    for at, orig, n in _IN_SEGMENTS:
        cols = np.arange(orig, orig + n)
        m[at:at + n] = (cols // COLS_IN) * SHARD_IN + cols % COLS_IN
    return m


def take_rows(a, idx, name):
    dep, r_in, c = a.shape
    blk = LANES
    n_out, n_in = len(idx) // blk, r_in // blk
    assert len(idx) % blk == 0 and r_in % blk == 0
    sources = [sorted({int(v) // blk for v in idx[blk * i:blk * (i + 1)] if v >= 0}) for i in range(n_out)]
    width = max(len(s) for s in sources)
    table = np.zeros((n_out, width), np.int32)
    for i, s in enumerate(sources):
        spare = [b for b in range(n_in) if b not in s][:width - len(s)]
        table[i] = s + spare

    def body(tbl_ref, idx_ref, *refs):
        in_refs, o_ref = refs[:width], refs[width]
        i = pl.program_id(1)
        src = idx_ref[...]
        acc = jnp.zeros((blk, c), F32)
        for k in range(width):
            pick = (src == tbl_ref[i, k] * blk + _iota2((blk, blk), 1)).astype(BF16)
            acc = acc + _nn(pick, in_refs[k][0])
        o_ref[0] = acc.astype(BF16)

    return pl.pallas_call(
        body, name=name,
        grid_spec=pltpu.PrefetchScalarGridSpec(
            num_scalar_prefetch=1, grid=(dep, n_out),
            in_specs=[pl.BlockSpec((blk, 1), lambda l, i, t: (i, 0))] +
                     [pl.BlockSpec((1, blk, c), lambda l, i, t, k=k: (l, t[i, k], 0)) for k in range(width)],
            out_specs=pl.BlockSpec((1, blk, c), lambda l, i, t: (l, i, 0))),
        out_shape=jax.ShapeDtypeStruct((dep, len(idx), c), BF16),
        compiler_params=_params(("parallel", "parallel")),
    )(jnp.asarray(table), jnp.asarray(np.asarray(idx, np.int32).reshape(-1, 1)), *([a] * width))


def _in_weight_layout(staged):
    return take_rows(staged, _in_column_map(), "w_in_layout")


def _in_gradient_layout(dwt):
    fwd = _in_column_map()
    inv = np.full((N_CHIPS * SHARD_IN,), -1, np.int64)
    inv[fwd[fwd >= 0]] = np.nonzero(fwd >= 0)[0]
    return take_rows(dwt, inv, "dw_in_layout")


SMALL_NAMES = ("norm1_w", "conv_w", "conv_b", "dt_bias", "a_log", "d_skip", "ssd_norm_w", "pool_w", "pool_b",
               "pool_scale", "norm2_w", "final_norm_w")
SMALL_ROWS = 104


def _pack_small(parts):
    flat = jnp.concatenate([p.reshape(-1) for p in parts])
    return jnp.pad(flat, (0, SMALL_ROWS * D_MODEL - flat.shape[0])).reshape(SMALL_ROWS, D_MODEL)


def _unpack_small(flat, shapes):
    flat = flat.reshape(-1)
    out, at = [], 0
    for shp in shapes:
        n = int(np.prod(shp))
        out.append(flat[at:at + n].reshape(shp))
        at += n
    return out


def kernel(x, norm1_w, w_in, conv_w, conv_b, dt_bias, a_log, d_skip, ssd_norm_w, pool_w, pool_b, pool_scale, w_out, norm2_w, w_gate, w_up, w_down, final_norm_w, loss_target, m_norm1_w, m_w_in, m_conv_w, m_conv_b, m_dt_bias, m_a_log, m_d_skip, m_ssd_norm_w, m_pool_w, m_pool_b, m_pool_scale, m_w_out, m_norm2_w, m_w_gate, m_w_up, m_w_down, m_final_norm_w, v_norm1_w, v_w_in, v_conv_w, v_conv_b, v_dt_bias, v_a_log, v_d_skip, v_ssd_norm_w, v_pool_w, v_pool_b, v_pool_scale, v_w_out, v_norm2_w, v_w_gate, v_w_up, v_w_down, v_final_norm_w):
    px, py, pc = _position()
    chip = 2 * px + py
    chip_arr = jnp.reshape(chip, (1,)).astype(jnp.int32)
    half_arr = jnp.reshape(pc, (1,)).astype(jnp.int32)

    def layer_shards(l):
        w_in_t = jnp.pad(jnp.swapaxes(w_in[l], 0, 1).astype(BF16), ((0, SHARD_IN - COLS_IN), (0, 0)))
        return [w_in_t, w_out[l].astype(BF16), jnp.swapaxes(w_gate[l], 0, 1).astype(BF16),
                jnp.swapaxes(w_up[l], 0, 1).astype(BF16), w_down[l].astype(BF16)]

    over_ici = {0: gather_ici_start(layer_shards(0))}

    def pass_on(l, after):
        own, arrived = exchange_wait("gather_ici_wait", over_ici[l], after, _gather_ici_plan)
        swap = gather_d2d_start(own, arrived)
        tokens = [swap[4]]
        if l + 1 < DEPTH:
            over_ici[l + 1] = gather_ici_start(layer_shards(l + 1), after=swap[4])
            tokens.append(over_ici[l + 1][4])
        return swap, tokens

    def weights_of(swap, after):
        _, (w_in_st, w_out_l, w_gate_t, w_up_t, w_down_l) = exchange_wait("gather_d2d_wait", swap, after, _gather_d2d_plan)
        return _in_weight_layout(w_in_st[None])[0], w_out_l, w_gate_t, w_up_t, w_down_l

    pad_heads = lambda v: jnp.pad(v, ((0, 0), (0, LANES - SSD_HEADS)))[:, None, :]
    dtb, alog = pad_heads(dt_bias), pad_heads(a_log)
    dskip_x = jnp.repeat(d_skip, HEAD_DIM, axis=1)[:, None, :]
    eye = jnp.eye(len(POOL_WINDOWS), dtype=F32)
    wbd = (pool_w[:, :, :, None, :] * eye[None, :, None, :, None]).reshape(DEPTH, POOL_WIDTH, POOL_WIDTH).astype(BF16)
    pool_b2 = pool_b.reshape(DEPTH, 1, POOL_WIDTH)
    cw_cols = lax.dynamic_update_slice(jnp.zeros((DEPTH, CONV_WIDTH, CONV_DIM), F32), conv_w,
                                       (0, 0, chip * (CONV_DIM // N_CHIPS)))
    cw_cols = jnp.where(pc == 0, cw_cols, 0.0)
    cw_rows = (DEPTH * CONV_WIDTH * CONV_DIM) // D_MODEL
    conv_w_f = small_allreduce(jnp.pad(cw_cols.reshape(cw_rows, D_MODEL), ((0, 8), (0, 0))), "gather_conv_w")
    conv_w_f = conv_w_f[:cw_rows].reshape(DEPTH, CONV_WIDTH, CONV_DIM)
    cw8 = jnp.pad(conv_w_f, ((0, 0), (0, 8 - CONV_WIDTH), (0, 0)))

    h = x[0]
    saved, weights = [], []
    swap, tokens = pass_on(0, over_ici[0][4])
    weights.append(weights_of(swap, tokens))
    for l in range(DEPTH):
        w_in_f, w_out_f, w_gate_t, w_up_t, w_down_f = weights[l]
        proj = rms_matmul(h, norm1_w[l][None], w_in_f, "in_proj")
        xc = conv_forward(proj, cw8[l], conv_b[l][None], "conv_fwd")
        y_all, ycore, states = ssd_forward(proj, xc, dtb[l], alog[l], dskip_x[l], ssd_norm_w[l][None], "ssd_fwd")
        y_all, o_sb = sb_forward(proj, y_all, "sb_fwd")
        swap, tokens = pass_on(l + 1, o_sb) if l + 1 < DEPTH else (None, None)
        y_all = pool_forward(proj, wbd[l], pool_b2[l], pool_scale[l][None], y_all, "pool_fwd", after=tokens)
        x1 = matmul_residual(y_all, w_out_f, h, "out_proj")
        x2, g, u = ffn_forward(x1, norm2_w[l][None], w_gate_t, w_up_t, w_down_f, "ffn_fwd")
        if swap is not None:
            weights.append(weights_of(swap, x2))
        saved.append((h, proj, xc, ycore, states, o_sb, y_all, x1, g, u))
        h = x2

    loss_part, dx, dxb, d_final = loss_head(h, final_norm_w[None], loss_target[0], "loss_head")
    loss = lax.psum(loss_part[0, 0], ("x", "y", "c"))

    small = {n: [None] * DEPTH for n in SMALL_NAMES if n != "final_norm_w"}
    chip_half = jnp.concatenate([chip_arr, half_arr])
    reduced = {}
    d2d = ici = early = None

    def add_cores(d2d, after):
        mine, theirs = exchange_wait("reduce_d2d_wait", d2d[1], after, _reduce_d2d_plan)
        return d2d[0], reduce_ici_start([add_halves(d, t, half_arr, "reduce_add_halves") for d, t in zip(mine, theirs)])

    def add_all(ici, after):
        sums, theirs = exchange_wait("reduce_ici_wait", ici[1], after, _reduce_ici_plan)
        return ici[0], reduce_swap_start([add_chips(p, t, chip_half, "reduce_add_chips") for p, t in zip(sums, theirs)])

    def finish(swap, after):
        reduced[swap[0]] = exchange_wait("reduce_swap_wait", swap[1], after, _reduce_swap_plan)[1]

    swaps = []
    for l in reversed(range(DEPTH)):
        xin, proj, xc, ycore, states, o_sb, y_all, x1, g, u = saved[l]
        w_in_f, w_out_f, w_gate_t, w_up_t, w_down_f = weights[l]
        dg, du, act = ffn_backward_act(dxb, g, u, w_down_f, "ffn_bwd_act", after=None if d2d is None else d2d[1][4])
        dx1, dx1b, h2, dn2 = rms_backward([dg, du], [w_gate_t, w_up_t], x1, norm2_w[l][None], dx, "ffn_bwd_norm", 256)
        if d2d is not None:
            ici = add_cores(d2d, dx1b)
        dyall = matmul_nt(dx1b, w_out_f, "out_proj_bwd", after=None if ici is None else ici[1][4])
        dw_down = matmul_tn(act, dxb, "dw_down")
        dw_gate = matmul_tn(dg, h2, "dw_gate")
        dw_up = matmul_tn(du, h2, "dw_up")
        dw_out = matmul_tn(y_all, dx1b, "dw_out")
        late = [dw.reshape(N_CHIPS, r, D_MODEL) for dw, r in zip((dw_out, dw_gate, dw_up, dw_down), SHARD_ROWS[1:])]
        if l == 0:
            early = ("0 late", reduce_d2d_start(late))
        dxc, dproj, dsn, ddsk, ddtb, dalog = ssd_backward(proj, xc, ycore, dyall, states, dtb[l], alog[l],
                                                          dskip_x[l], ssd_norm_w[l][None], "ssd_bwd")
        dproj, dcw, dcb = conv_backward(proj, dxc, cw8[l], conv_b[l][None], dproj, "conv_bwd",
                                        after=None if early is None else early[1][4])
        if early is not None:
            early = add_cores(early, dproj)
        dproj = sb_backward(proj, o_sb, dyall, dproj, "sb_bwd", after=None if early is None else early[1][4])
        dproj, dwbd, dpb, dps = pool_backward(proj, dyall, wbd[l], pool_b2[l], pool_scale[l][None], dproj, "pool_bwd")
        if ici is not None:
            swaps.append(add_all(ici, dproj))
            ici = None
        dx, dxb, h1, dn1 = rms_backward([dproj], [w_in_f], xin, norm1_w[l][None], dx1, "in_proj_bwd", 256,
                                        after=swaps[-1][1][4] if swaps else None)
        dw_in = _in_gradient_layout(matmul_tn(dproj, h1, "dw_in")[None])[0].reshape(N_CHIPS, SHARD_IN, D_MODEL)
        d2d = (l, reduce_d2d_start([dw_in] if l == 0 else [dw_in] + late))
        small["norm1_w"][l] = dn1[0]
        small["conv_w"][l] = dcw[:CONV_WIDTH]
        small["conv_b"][l] = dcb[0]
        small["dt_bias"][l] = ddtb[0, :SSD_HEADS]
        small["a_log"][l] = dalog[0, :SSD_HEADS]
        small["d_skip"][l] = ddsk.reshape(SSD_HEADS, HEAD_DIM).sum(axis=1)
        small["ssd_norm_w"][l] = dsn[0]
        small["pool_w"][l] = jnp.stack([dwbd[64 * k:64 * k + 64, 64 * k:64 * k + 64] for k in range(len(POOL_WINDOWS))])
        small["pool_b"][l] = dpb.reshape(len(POOL_WINDOWS), -1)
        small["pool_scale"][l] = dps[0]
        small["norm2_w"][l] = dn2[0]
    grad_x = dx[None]

    ici = add_cores(d2d, d2d[1][4])
    swaps.append(add_all(early, ici[1][4]))
    swaps.append(add_all(ici, swaps[-1][1][4]))
    for swap in swaps:
        finish(swap, swaps[-1][1][4])
    reduced[0] = reduced[0] + reduced["0 late"]
    g_big = {n: [reduced[l][k] for l in range(DEPTH)] for k, n in enumerate(("w_in", "w_out", "w_gate", "w_up", "w_down"))}
    g_big["w_in"] = [gl[:COLS_IN] for gl in g_big["w_in"]]
    transposed = ("w_in", "w_gate", "w_up")

    small_parts = [d_final if n == "final_norm_w" else jnp.stack(small[n]) for n in SMALL_NAMES]
    small_shapes = [p.shape for p in small_parts]
    g_small = dict(zip(SMALL_NAMES, _unpack_small(small_allreduce(_pack_small(small_parts), "reduce_small"), small_shapes)))
    g_small["final_norm_w"] = g_small["final_norm_w"].reshape(final_norm_w.shape)
    g_small["conv_w"] = lax.dynamic_slice_in_dim(g_small["conv_w"], chip * (CONV_DIM // N_CHIPS), CONV_DIM // N_CHIPS, axis=2)

    given = dict(norm1_w=(norm1_w, m_norm1_w, v_norm1_w), w_in=(w_in, m_w_in, v_w_in), conv_w=(conv_w, m_conv_w, v_conv_w),
                 conv_b=(conv_b, m_conv_b, v_conv_b), dt_bias=(dt_bias, m_dt_bias, v_dt_bias), a_log=(a_log, m_a_log, v_a_log),
                 d_skip=(d_skip, m_d_skip, v_d_skip), ssd_norm_w=(ssd_norm_w, m_ssd_norm_w, v_ssd_norm_w),
                 pool_w=(pool_w, m_pool_w, v_pool_w), pool_b=(pool_b, m_pool_b, v_pool_b),
                 pool_scale=(pool_scale, m_pool_scale, v_pool_scale), w_out=(w_out, m_w_out, v_w_out),
                 norm2_w=(norm2_w, m_norm2_w, v_norm2_w), w_gate=(w_gate, m_w_gate, v_w_gate), w_up=(w_up, m_w_up, v_w_up),
                 w_down=(w_down, m_w_down, v_w_down), final_norm_w=(final_norm_w, m_final_norm_w, v_final_norm_w))
    order = ("norm1_w", "w_in", "conv_w", "conv_b", "dt_bias", "a_log", "d_skip", "ssd_norm_w", "pool_w", "pool_b",
             "pool_scale", "w_out", "norm2_w", "w_gate", "w_up", "w_down", "final_norm_w")
    grads = dict(g_small)
    results = {}
    for n in ("w_in", "w_out", "w_gate", "w_up", "w_down"):
        w, m, v = given[n]
        if n in transposed:
            out = adamw_layers(jnp.swapaxes(w, 1, 2), g_big[n], jnp.swapaxes(m, 1, 2), jnp.swapaxes(v, 1, 2), "adamw_" + n)
            out = [jnp.swapaxes(o, 1, 2) for o in out]
        else:
            out = adamw_layers(w, g_big[n], m, v, "adamw_" + n)
        grads[n], results[n] = out[0], tuple(out[1:])
    small_shapes = [given[n][0].shape for n in SMALL_NAMES]
    packed = [_pack_small([given[n][k] for n in SMALL_NAMES])[None] for k in range(3)]
    packed_g = _pack_small([grads[n] for n in SMALL_NAMES])[None]
    small_out = adamw(packed[0], packed_g, packed[1], packed[2], "adamw_small")
    small_out = [_unpack_small(o[0], small_shapes) for o in small_out]
    for i, n in enumerate(SMALL_NAMES):
        results[n] = tuple(small_out[k][i] for k in range(3))

    return (loss, grad_x, *[grads[n] for n in order], *[results[n][0] for n in order],
            *[results[n][1] for n in order], *[results[n][2] for n in order])
```

```python
import numpy as np
import jax
import jax.numpy as jnp
from jax import lax
from jax.experimental import pallas as pl
from jax.experimental.pallas import tpu as pltpu

F32 = jnp.float32
BF16 = jnp.bfloat16
MESH = pl.DeviceIdType.MESH
ANY = pl.BlockSpec(memory_space=pl.ANY)

D_MODEL = 1024
DEPTH = 4
EPS = 1e-6
SSD_WIDTH = 512
SSD_HEADS = 8
HEAD_DIM = 64
D_STATE = 128
CHUNK = 128
CONV_WIDTH = 4
CONV_DIM = 1024
SB_WIDTH = 256
POOL_WIDTH = 256
POOL_WINDOWS = (2, 4, 8, 16)
D_FF = 2816
D_IN = 2568
N_CHIPS = 4
OFF_QKV, OFF_Z, OFF_DT, OFF_XBC, OFF_P = 0, 768, 1280, 1536, 2560
D_INP = 2816
ZDT = 768
SHARD_IN, SHARD_OUT, SHARD_FF = 672, 256, 704
COLS_IN = 642
ADAM_LR, ADAM_B1, ADAM_B2, ADAM_EPS, ADAM_WD, ADAM_STEP = 0.001, 0.9, 0.999, 1e-08, 0.01, 10
LANES = 128
VMEM_LIMIT = 56 * 1024 * 1024


def _params(sem=None):
    return pltpu.CompilerParams(dimension_semantics=sem, vmem_limit_bytes=VMEM_LIMIT)


def _tile(n, cap):
    best = None
    for t in range(LANES, min(n, cap) + 1, LANES):
        if n % t == 0:
            best = t
    assert best is not None, (n, cap)
    return best


def _nt(a, b):
    return lax.dot_general(a, b, (((1,), (1,)), ((), ())), preferred_element_type=F32)


def _tn(a, b):
    return lax.dot_general(a, b, (((0,), (0,)), ((), ())), preferred_element_type=F32)


def _nn(a, b):
    return jnp.dot(a, b, preferred_element_type=F32)


def _split_dot(a, b_exact, terms=3, dot=_nn):
    acc = None
    rest = a
    for _ in range(terms):
        hi = rest.astype(BF16)
        part = dot(hi, b_exact)
        acc = part if acc is None else acc + part
        rest = rest - hi.astype(F32)
    return acc


def _split_dot_left(a_exact, b, terms=3):
    acc = None
    rest = b
    for _ in range(terms):
        hi = rest.astype(BF16)
        part = _nn(a_exact, hi)
        acc = part if acc is None else acc + part
        rest = rest - hi.astype(F32)
    return acc


def _sigmoid(x):
    return 1.0 / (1.0 + jnp.exp(-x))


def _softplus(x):
    return jnp.maximum(x, 0.0) + jnp.log(1.0 + jnp.exp(-jnp.abs(x)))


def _iota2(shape, dim):
    return lax.broadcasted_iota(jnp.int32, shape, dim)


def _after(after):
    ops = [] if after is None else list(after) if isinstance(after, (list, tuple)) else [after]
    return [ANY] * len(ops), ops


def rms_matmul(x, nw, wt, name, after=None):
    s, d = x.shape
    n = wt.shape[0]
    tm, tn = _tile(s, 512), _tile(n, 2816)
    specs, ops = _after(after)

    def body(x_ref, nw_ref, w_ref, *rest):
        o_ref, h_ref = rest[len(ops):]

        @pl.when(pl.program_id(1) == 0)
        def _():
            xv = x_ref[...]
            r = lax.rsqrt(jnp.mean(xv * xv, axis=-1, keepdims=True) + EPS)
            h_ref[...] = (xv * r * nw_ref[...]).astype(BF16)
        o_ref[...] = _nt(h_ref[...], w_ref[...])

    return pl.pallas_call(
        body, name=name, grid=(s // tm, n // tn),
        in_specs=[pl.BlockSpec((tm, d), lambda i, j: (i, 0)), pl.BlockSpec((1, d), lambda i, j: (0, 0)),
                  pl.BlockSpec((tn, d), lambda i, j: (j, 0))] + specs,
        out_specs=pl.BlockSpec((tm, tn), lambda i, j: (i, j)),
        out_shape=jax.ShapeDtypeStruct((s, n), F32),
        scratch_shapes=[pltpu.VMEM((tm, d), BF16)],
        compiler_params=_params(("parallel", "arbitrary")),
    )(x, nw, wt, *ops)


def matmul_residual(a, w, res, name):
    s, k = a.shape
    n = w.shape[1]
    tm, tn = _tile(s, 512), _tile(n, 512)

    def body(a_ref, w_ref, r_ref, o_ref):
        o_ref[...] = r_ref[...] + _nn(a_ref[...], w_ref[...])

    return pl.pallas_call(
        body, name=name, grid=(s // tm, n // tn),
        in_specs=[pl.BlockSpec((tm, k), lambda i, j: (i, 0)), pl.BlockSpec((k, tn), lambda i, j: (0, j)),
                  pl.BlockSpec((tm, tn), lambda i, j: (i, j))],
        out_specs=pl.BlockSpec((tm, tn), lambda i, j: (i, j)),
        out_shape=jax.ShapeDtypeStruct((s, n), F32),
        compiler_params=_params(("parallel", "parallel")),
    )(a, w, res)


def matmul_nt(a, w, name, out_dtype=F32, after=None):
    s, n = a.shape
    k = w.shape[0]
    tm, tk = _tile(s, 512), _tile(k, 512)
    specs, ops = _after(after)

    def body(a_ref, w_ref, *rest):
        rest[-1][...] = _nt(a_ref[...], w_ref[...]).astype(out_dtype)

    return pl.pallas_call(
        body, name=name, grid=(s // tm, k // tk),
        in_specs=[pl.BlockSpec((tm, n), lambda i, j: (i, 0)), pl.BlockSpec((tk, n), lambda i, j: (j, 0))] + specs,
        out_specs=pl.BlockSpec((tm, tk), lambda i, j: (i, j)),
        out_shape=jax.ShapeDtypeStruct((s, k), out_dtype),
        compiler_params=_params(("parallel", "parallel")),
    )(a, w, *ops)


def matmul_tn(a, b, name, after=None):
    s, m = a.shape
    n = b.shape[1]
    tm, tn = _tile(m, 512), _tile(n, 1024)

    def body(a_ref, b_ref, *rest):
        rest[-1][...] = _tn(a_ref[...], b_ref[...]).astype(BF16)

    specs, ops = _after(after)
    return pl.pallas_call(
        body, name=name, grid=(m // tm, n // tn),
        in_specs=[pl.BlockSpec((s, tm), lambda i, j: (0, i)), pl.BlockSpec((s, tn), lambda i, j: (0, j))] + specs,
        out_specs=pl.BlockSpec((tm, tn), lambda i, j: (i, j)),
        out_shape=jax.ShapeDtypeStruct((m, n), BF16),
        compiler_params=_params(("parallel", "parallel")),
    )(a, b, *ops)


def ffn_forward(x1, nw, wgt, wut, wd, name):
    s, d = x1.shape
    f = wgt.shape[0]
    tm, tf = _tile(s, 1024), _tile(f, 256)

    def body(x_ref, nw_ref, wg_ref, wu_ref, wd_ref, o_ref, g_ref, u_ref, h_ref, acc_ref):
        j = pl.program_id(1)

        @pl.when(j == 0)
        def _():
            xv = x_ref[...]
            r = lax.rsqrt(jnp.mean(xv * xv, axis=-1, keepdims=True) + EPS)
            h_ref[...] = (xv * r * nw_ref[...]).astype(BF16)
            acc_ref[...] = xv

        h = h_ref[...]
        g = _nt(h, wg_ref[...])
        u = _nt(h, wu_ref[...])
        g_ref[...] = g.astype(BF16)
        u_ref[...] = u.astype(BF16)
        a = (g * _sigmoid(g) * u).astype(BF16)
        acc_ref[...] += _nn(a, wd_ref[...])

        @pl.when(j == pl.num_programs(1) - 1)
        def _():
            o_ref[...] = acc_ref[...]

    wblk = pl.BlockSpec((tf, d), lambda i, j: (j, 0))
    return pl.pallas_call(
        body, name=name, grid=(s // tm, f // tf),
        in_specs=[pl.BlockSpec((tm, d), lambda i, j: (i, 0)), pl.BlockSpec((1, d), lambda i, j: (0, 0)), wblk, wblk, wblk],
        out_specs=[pl.BlockSpec((tm, d), lambda i, j: (i, 0)), pl.BlockSpec((tm, tf), lambda i, j: (i, j)),
                   pl.BlockSpec((tm, tf), lambda i, j: (i, j))],
        out_shape=[jax.ShapeDtypeStruct((s, d), F32), jax.ShapeDtypeStruct((s, f), BF16),
                   jax.ShapeDtypeStruct((s, f), BF16)],
        scratch_shapes=[pltpu.VMEM((tm, d), BF16), pltpu.VMEM((tm, d), F32)],
        compiler_params=_params(("parallel", "arbitrary")),
    )(x1, nw, wgt, wut, wd)


def ffn_backward_act(dx2, g, u, wd, name, after=None):
    s, d = dx2.shape
    f = wd.shape[0]
    tm, tf = _tile(s, 256), _tile(f, 2816)
    specs, ops = _after(after)

    def body(dx_ref, g_ref, u_ref, wd_ref, *rest):
        dg_ref, du_ref, a_ref = rest[len(ops):]
        da = _nt(dx_ref[...], wd_ref[...])
        gv = g_ref[...].astype(F32)
        uv = u_ref[...].astype(F32)
        sg = _sigmoid(gv)
        silu = gv * sg
        dg_ref[...] = (da * uv * (sg * (1.0 + gv * (1.0 - sg)))).astype(BF16)
        du_ref[...] = (da * silu).astype(BF16)
        a_ref[...] = (silu * uv).astype(BF16)

    blk = pl.BlockSpec((tm, tf), lambda i, j: (i, j))
    return pl.pallas_call(
        body, name=name, grid=(s // tm, f // tf),
        in_specs=[pl.BlockSpec((tm, d), lambda i, j: (i, 0)), blk, blk, pl.BlockSpec((tf, d), lambda i, j: (j, 0))] + specs,
        out_specs=[blk, blk, blk],
        out_shape=[jax.ShapeDtypeStruct((s, f), BF16)] * 3,
        compiler_params=_params(("parallel", "parallel")),
    )(dx2, g, u, wd, *ops)


def rms_backward(dzs, wts, x, nw, dres, name, tm, after=None):
    s, d = x.shape
    nz = len(dzs)
    specs, ops = _after(after)

    def body(*refs):
        dz_refs, w_refs = refs[:nz], refs[nz:2 * nz]
        x_ref, nw_ref, dres_ref = refs[2 * nz:2 * nz + 3]
        dx_ref, dxb_ref, h_ref, dnw_ref = refs[2 * nz + 3 + len(ops):]
        dh = _nn(dz_refs[0][...], w_refs[0][...])
        for k in range(1, nz):
            dh = dh + _nn(dz_refs[k][...], w_refs[k][...])
        xv = x_ref[...]
        r = lax.rsqrt(jnp.mean(xv * xv, axis=-1, keepdims=True) + EPS)
        xhat = xv * r
        nwv = nw_ref[...]
        h_ref[...] = (xhat * nwv).astype(BF16)

        @pl.when(pl.program_id(0) == 0)
        def _():
            dnw_ref[...] = jnp.zeros_like(dnw_ref)

        dnw_ref[...] += jnp.sum(dh * xhat, axis=0, keepdims=True)
        gdh = dh * nwv
        dx = dres_ref[...] + r * (gdh - xhat * jnp.mean(gdh * xhat, axis=-1, keepdims=True))
        dx_ref[...] = dx
        dxb_ref[...] = dx.astype(BF16)

    row = pl.BlockSpec((tm, d), lambda i: (i, 0))
    in_specs = [pl.BlockSpec((tm, dz.shape[1]), lambda i: (i, 0)) for dz in dzs]
    in_specs += [pl.BlockSpec(w.shape, lambda i: (0, 0)) for w in wts]
    in_specs += [row, pl.BlockSpec((1, d), lambda i: (0, 0)), row] + specs
    return pl.pallas_call(
        body, name=name, grid=(s // tm,),
        in_specs=in_specs,
        out_specs=[row, row, row, pl.BlockSpec((1, d), lambda i: (0, 0))],
        out_shape=[jax.ShapeDtypeStruct((s, d), F32), jax.ShapeDtypeStruct((s, d), BF16),
                   jax.ShapeDtypeStruct((s, d), BF16), jax.ShapeDtypeStruct((1, d), F32)],
        compiler_params=_params(("arbitrary",)),
    )(*dzs, *wts, x, nw, dres, *ops)


def loss_head(x, nw, target, name):
    s, d = x.shape
    tm = _tile(s, 512)

    def body(x_ref, nw_ref, t_ref, loss_ref, dx_ref, dxb_ref, dnw_ref):
        xv = x_ref[...]
        r = lax.rsqrt(jnp.mean(xv * xv, axis=-1, keepdims=True) + EPS)
        xhat = xv * r
        nwv = nw_ref[...]
        err = xhat * nwv - t_ref[...]

        @pl.when(pl.program_id(0) == 0)
        def _():
            dnw_ref[...] = jnp.zeros_like(dnw_ref)
            loss_ref[...] = jnp.zeros_like(loss_ref)

        part = jnp.sum(jnp.sum(err * err, axis=-1, keepdims=True), axis=0, keepdims=True) * (0.5 / d)
        loss_ref[...] += jnp.broadcast_to(part, loss_ref.shape)
        dout = err * (1.0 / d)
        dnw_ref[...] += jnp.sum(dout * xhat, axis=0, keepdims=True)
        gdh = dout * nwv
        dx = r * (gdh - xhat * jnp.mean(gdh * xhat, axis=-1, keepdims=True))
        dx_ref[...] = dx
        dxb_ref[...] = dx.astype(BF16)

    row = pl.BlockSpec((tm, d), lambda i: (i, 0))
    return pl.pallas_call(
        body, name=name, grid=(s // tm,),
        in_specs=[row, pl.BlockSpec((1, d), lambda i: (0, 0)), row],
        out_specs=[pl.BlockSpec((1, LANES), lambda i: (0, 0)), row, row, pl.BlockSpec((1, d), lambda i: (0, 0))],
        out_shape=[jax.ShapeDtypeStruct((1, LANES), F32), jax.ShapeDtypeStruct((s, d), F32),
                   jax.ShapeDtypeStruct((s, d), BF16), jax.ShapeDtypeStruct((1, d), F32)],
        compiler_params=_params(("arbitrary",)),
    )(x, nw, target)


def _shift_down(x, k):
    return jnp.where(_iota2(x.shape, 0) >= k, pltpu.roll(x, k, axis=0), 0.0)


def _shift_up(x, k):
    s = x.shape[0]
    return jnp.where(_iota2(x.shape, 0) < s - k, pltpu.roll(x, s - k, axis=0), 0.0)


CONV_TILE = 256


def conv_forward(proj, cw, cb, name):
    s = proj.shape[0]
    tn = CONV_TILE
    off = OFF_XBC // tn

    def body(u_ref, w_ref, b_ref, o_ref):
        u = u_ref[...]
        pre = b_ref[...] + w_ref[CONV_WIDTH - 1:CONV_WIDTH, :] * u
        for i in range(CONV_WIDTH - 1):
            pre = pre + w_ref[i:i + 1, :] * _shift_down(u, CONV_WIDTH - 1 - i)
        o_ref[...] = pre * _sigmoid(pre)

    return pl.pallas_call(
        body, name=name, grid=(CONV_DIM // tn,),
        in_specs=[pl.BlockSpec((s, tn), lambda j: (0, off + j)), pl.BlockSpec((8, tn), lambda j: (0, j)),
                  pl.BlockSpec((1, tn), lambda j: (0, j))],
        out_specs=pl.BlockSpec((s, tn), lambda j: (0, j)),
        out_shape=jax.ShapeDtypeStruct((s, CONV_DIM), F32),
        compiler_params=_params(("parallel",)),
    )(proj, cw, cb)


def conv_backward(proj, dxc, cw, cb, dproj, name, after=None):
    s = proj.shape[0]
    tn = CONV_TILE
    off = OFF_XBC // tn

    specs, ops = _after(after)

    def body(u_ref, d_ref, w_ref, b_ref, *rest):
        du_ref, dw_ref, db_ref = rest[-3:]
        u = u_ref[...]
        shifted = [_shift_down(u, CONV_WIDTH - 1 - i) for i in range(CONV_WIDTH - 1)] + [u]
        pre = b_ref[...] + w_ref[CONV_WIDTH - 1:CONV_WIDTH, :] * u
        for i in range(CONV_WIDTH - 1):
            pre = pre + w_ref[i:i + 1, :] * shifted[i]
        sg = _sigmoid(pre)
        dpre = d_ref[...] * (sg * (1.0 + pre * (1.0 - sg)))
        du = w_ref[CONV_WIDTH - 1:CONV_WIDTH, :] * dpre
        for i in range(CONV_WIDTH - 1):
            du = du + w_ref[i:i + 1, :] * _shift_up(dpre, CONV_WIDTH - 1 - i)
        du_ref[...] = du.astype(BF16)
        rows = [jnp.sum(dpre * shifted[i], axis=0, keepdims=True) for i in range(CONV_WIDTH)]
        rows.append(jnp.zeros((8 - CONV_WIDTH, tn), F32))
        dw_ref[...] = jnp.concatenate(rows, axis=0)
        db_ref[...] = jnp.sum(dpre, axis=0, keepdims=True)

    return pl.pallas_call(
        body, name=name, grid=(CONV_DIM // tn,),
        in_specs=[pl.BlockSpec((s, tn), lambda j: (0, off + j)), pl.BlockSpec((s, tn), lambda j: (0, j)),
                  pl.BlockSpec((8, tn), lambda j: (0, j)), pl.BlockSpec((1, tn), lambda j: (0, j)), ANY] + specs,
        out_specs=[pl.BlockSpec((s, tn), lambda j: (0, off + j)), pl.BlockSpec((8, tn), lambda j: (0, j)),
                   pl.BlockSpec((1, tn), lambda j: (0, j))],
        out_shape=[jax.ShapeDtypeStruct(dproj.shape, BF16), jax.ShapeDtypeStruct((8, CONV_DIM), F32),
                   jax.ShapeDtypeStruct((1, CONV_DIM), F32)],
        input_output_aliases={4: 0},
        compiler_params=_params(("parallel",)),
    )(proj, dxc, cw, cb, dproj, *ops)


def _pool_lane_window(shape):
    grp = _iota2(shape, 1) // (POOL_WIDTH // len(POOL_WINDOWS))
    win = jnp.full(shape, POOL_WINDOWS[-1], jnp.int32)
    for gi in range(len(POOL_WINDOWS) - 2, -1, -1):
        win = jnp.where(grp == gi, POOL_WINDOWS[gi], win)
    return grp, win


def _pool_select(grp, sums):
    out = sums[-1]
    for gi in range(len(sums) - 2, -1, -1):
        out = jnp.where(grp == gi, sums[gi], out)
    return out


def _pool_pooled(p):
    grp, win = _pool_lane_window(p.shape)
    inv_count = 1.0 / jnp.minimum(_iota2(p.shape, 0) + 1, win).astype(F32)
    sums, acc, k = [], p, 1
    for _ in POOL_WINDOWS:
        acc = acc + _shift_down(acc, k)
        sums.append(acc)
        k *= 2
    return _pool_select(grp, sums) * inv_count - p, grp, inv_count


def pool_forward(proj, wbd, pb, ps, y_all, name, after=None):
    s = proj.shape[0]
    specs, ops = _after(after)

    def body(p_ref, w_ref, b_ref, s_ref, *rest):
        o_ref = rest[-1]
        pooled, _, _ = _pool_pooled(p_ref[...])
        mixed = _nn(pooled.astype(BF16), w_ref[...]) + b_ref[...]
        o_ref[...] = (mixed * s_ref[...]).astype(BF16)

    vec = pl.BlockSpec((1, POOL_WIDTH), lambda j: (0, 0))
    return pl.pallas_call(
        body, name=name, grid=(1,),
        in_specs=[pl.BlockSpec((s, POOL_WIDTH), lambda j: (0, OFF_P // POOL_WIDTH)),
                  pl.BlockSpec((POOL_WIDTH, POOL_WIDTH), lambda j: (0, 0)), vec, vec, ANY] + specs,
        out_specs=pl.BlockSpec((s, POOL_WIDTH), lambda j: (0, (SSD_WIDTH + SB_WIDTH) // POOL_WIDTH)),
        out_shape=jax.ShapeDtypeStruct(y_all.shape, BF16),
        input_output_aliases={4: 0},
        compiler_params=_params(("arbitrary",)),
    )(proj, wbd, pb, ps, y_all, *ops)


def pool_backward(proj, dyall, wbd, pb, ps, dproj, name):
    s = proj.shape[0]

    def body(p_ref, dy_ref, w_ref, b_ref, s_ref, _, dp_ref, dw_ref, db_ref, ds_ref):
        pooled, grp, inv_count = _pool_pooled(p_ref[...])
        pooled_b = pooled.astype(BF16)
        mixed = _nn(pooled_b, w_ref[...]) + b_ref[...]
        dy = dy_ref[...]
        ds_ref[...] = jnp.sum(dy * mixed, axis=0, keepdims=True)
        dmixed = dy * s_ref[...]
        db_ref[...] = jnp.sum(dmixed, axis=0, keepdims=True)
        dmixed_b = dmixed.astype(BF16)
        dw_ref[...] = _tn(pooled_b, dmixed_b)
        dpooled = _nt(dmixed_b, w_ref[...])
        sums, acc, k = [], dpooled * inv_count, 1
        for _ in POOL_WINDOWS:
            acc = acc + _shift_up(acc, k)
            sums.append(acc)
            k *= 2
        dp_ref[...] = (_pool_select(grp, sums) - dpooled).astype(BF16)

    vec = pl.BlockSpec((1, POOL_WIDTH), lambda j: (0, 0))
    mat = pl.BlockSpec((POOL_WIDTH, POOL_WIDTH), lambda j: (0, 0))
    pcol = pl.BlockSpec((s, POOL_WIDTH), lambda j: (0, OFF_P // POOL_WIDTH))
    return pl.pallas_call(
        body, name=name, grid=(1,),
        in_specs=[pcol, pl.BlockSpec((s, POOL_WIDTH), lambda j: (0, (SSD_WIDTH + SB_WIDTH) // POOL_WIDTH)), mat, vec, vec, ANY],
        out_specs=[pcol, mat, vec, vec],
        out_shape=[jax.ShapeDtypeStruct(dproj.shape, BF16), jax.ShapeDtypeStruct((POOL_WIDTH, POOL_WIDTH), F32),
                   jax.ShapeDtypeStruct((1, POOL_WIDTH), F32), jax.ShapeDtypeStruct((1, POOL_WIDTH), F32)],
        input_output_aliases={5: 0},
        compiler_params=_params(("arbitrary",)),
    )(proj, dyall, wbd, pb, ps, dproj)


N_PAIRS = SSD_HEADS // 2


def _ssd_common(xc, dtraw, dtb, alog):
    c = CHUNK
    dt = _softplus(dtraw + dtb)
    a = -jnp.exp(alog)
    ltri = (_iota2((c, c), 0) >= _iota2((c, c), 1)).astype(BF16)
    acum = _split_dot_left(ltri, dt * a)
    expand = (_iota2((c, SSD_WIDTH), 1) // HEAD_DIM == _iota2((c, SSD_WIDTH), 0)).astype(BF16)
    expand_wide = (_iota2((c, SSD_HEADS * c), 1) // c == _iota2((c, SSD_HEADS * c), 0)).astype(BF16)
    acum_x = _split_dot(acum, expand)
    dt_x = _split_dot(dt, expand)
    alast_x = acum_x[c - 1:c, :]
    return dict(dt=dt, a=a, acum=acum, acum_x=acum_x, dt_x=dt_x, ea_x=jnp.exp(acum_x),
                dte_x=jnp.exp(alast_x - acum_x), eal_x=jnp.exp(alast_x),
                acol=_split_dot(acum, expand_wide), acum_t=acum.T,
                xs=xc[:, :SSD_WIDTH], causal=_iota2((c, c), 0) >= _iota2((c, c), 1),
                left=_iota2((c, c), 1) < HEAD_DIM)


def _ssd_group(xc, g):
    b = xc[:, SSD_WIDTH + D_STATE * g:SSD_WIDTH + D_STATE * (g + 1)]
    cm = xc[:, SSD_WIDTH + 2 * D_STATE + D_STATE * g:SSD_WIDTH + 2 * D_STATE + D_STATE * (g + 1)]
    return b, cm


def _ssd_decay(q, hh):
    col = q["acol"][:, CHUNK * hh:CHUNK * (hh + 1)]
    row = q["acum_t"][hh:hh + 1, :]
    return jnp.where(q["causal"], jnp.exp(jnp.minimum(col - row, 0.0)), 0.0)


def ssd_forward(proj, xc, dtb, alog, dskip_x, nw, name):
    s = xc.shape[0]
    nc = s // CHUNK

    def body(xc_ref, zdt_ref, dtb_ref, alog_ref, dsk_ref, nw_ref, y_ref, yc_ref, st_ref, state):
        @pl.when(pl.program_id(0) == 0)
        def _():
            state[...] = jnp.zeros_like(state)

        xcv = xc_ref[...]
        q = _ssd_common(xcv, zdt_ref[:, SSD_WIDTH:SSD_WIDTH + LANES], dtb_ref[...], alog_ref[...])
        x = q["xs"] * q["dt_x"]
        xb = x.astype(BF16)
        xd = (x * q["dte_x"]).astype(BF16)
        pieces = []
        for g in range(2):
            bg, cg = _ssd_group(xcv, g)
            bgb, cgb = bg.astype(BF16), cg.astype(BF16)
            cb = _nt(cgb, bgb)
            bgt = bg.T.astype(BF16)
            for pr in (2 * g, 2 * g + 1):
                sl = slice(CHUNK * pr, CHUNK * (pr + 1))
                st = state[pr]
                st_ref[0, pr] = st
                yp = _nn(cgb, st.astype(BF16)) * q["ea_x"][:, sl]
                for k, hh in enumerate((2 * pr, 2 * pr + 1)):
                    w = (cb * _ssd_decay(q, hh)).astype(BF16)
                    mask = q["left"] if k == 0 else jnp.logical_not(q["left"])
                    yp = yp + _nn(w, jnp.where(mask, xb[:, sl], jnp.zeros_like(xb[:, sl])))
                state[pr] = st * q["eal_x"][:, sl] + _nn(bgt, xd[:, sl])
                pieces.append(yp)
        y = jnp.concatenate(pieces, axis=1) + q["xs"] * dsk_ref[...]
        yc_ref[...] = y
        zv = zdt_ref[:, :SSD_WIDTH]
        yg = y * (zv * _sigmoid(zv))
        r = lax.rsqrt(jnp.mean(yg * yg, axis=-1, keepdims=True) + EPS)
        y_ref[...] = (yg * r * nw_ref[...]).astype(BF16)

    vec = lambda n: pl.BlockSpec((1, n), lambda c: (0, 0))
    return pl.pallas_call(
        body, name=name, grid=(nc,),
        in_specs=[pl.BlockSpec((CHUNK, CONV_DIM), lambda c: (c, 0)),
                  pl.BlockSpec((CHUNK, ZDT), lambda c: (c, OFF_Z // ZDT)),
                  vec(LANES), vec(LANES), vec(SSD_WIDTH), vec(SSD_WIDTH)],
        out_specs=[pl.BlockSpec((CHUNK, SSD_WIDTH), lambda c: (c, 0)), pl.BlockSpec((CHUNK, SSD_WIDTH), lambda c: (c, 0)),
                   pl.BlockSpec((1, N_PAIRS, D_STATE, CHUNK), lambda c: (c, 0, 0, 0))],
        out_shape=[jax.ShapeDtypeStruct((s, D_MODEL), BF16), jax.ShapeDtypeStruct((s, SSD_WIDTH), F32),
                   jax.ShapeDtypeStruct((nc, N_PAIRS, D_STATE, CHUNK), F32)],
        scratch_shapes=[pltpu.VMEM((N_PAIRS, D_STATE, CHUNK), F32)],
        compiler_params=_params(("arbitrary",)),
    )(xc, proj, dtb, alog, dskip_x, nw)


def ssd_backward(proj, xc, ycore, dyall, states, dtb, alog, dskip_x, nw, name):
    s = xc.shape[0]
    nc = s // CHUNK
    c = CHUNK

    def body(xc_ref, zdt_ref, yc_ref, dy_ref, st_ref, dtb_ref, alog_ref, dsk_ref, nw_ref,
             dxc_ref, dzdt_ref, dnw_ref, ddsk_ref, ddtb_ref, dalog_ref, dstate):
        @pl.when(pl.program_id(0) == 0)
        def _():
            dstate[...] = jnp.zeros_like(dstate)
            dnw_ref[...] = jnp.zeros_like(dnw_ref)
            ddsk_ref[...] = jnp.zeros_like(ddsk_ref)
            ddtb_ref[...] = jnp.zeros_like(ddtb_ref)
            dalog_ref[...] = jnp.zeros_like(dalog_ref)

        xcv = xc_ref[...]
        dtraw = zdt_ref[:, SSD_WIDTH:SSD_WIDTH + LANES]
        q = _ssd_common(xcv, dtraw, dtb_ref[...], alog_ref[...])
        xs = q["xs"]
        x = xs * q["dt_x"]
        zv, yc, dy, nwv = zdt_ref[:, :SSD_WIDTH], yc_ref[...], dy_ref[...], nw_ref[...]
        sgz = _sigmoid(zv)
        siluz = zv * sgz
        yg = yc * siluz
        r = lax.rsqrt(jnp.mean(yg * yg, axis=-1, keepdims=True) + EPS)
        dnw_ref[...] += jnp.sum(dy * yg * r, axis=0, keepdims=True)
        g1 = dy * nwv
        dyg = r * (g1 - yg * (r * r) * jnp.mean(g1 * yg, axis=-1, keepdims=True))
        dyv = dyg * siluz
        dz = (dyg * yc * (sgz * (1.0 + zv * (1.0 - sgz)))).astype(BF16)
        ddsk_ref[...] += jnp.sum(dyv * xs, axis=0, keepdims=True)
        dye = dyv * q["ea_x"]
        dx_parts, yoff_parts, u_parts, v_parts, e_parts = [], [], [], [], []
        db_parts, dc_parts = [], []
        for g in range(2):
            bg, cg = _ssd_group(xcv, g)
            bgb, cgb = bg.astype(BF16), cg.astype(BF16)
            cb = _nt(cgb, bgb)
            cgt = cg.T.astype(BF16)
            dgsum = jnp.zeros((c, c), F32)
            dbg = jnp.zeros((c, D_STATE), F32)
            dcg = jnp.zeros((c, D_STATE), F32)
            for pr in (2 * g, 2 * g + 1):
                sl = slice(c * pr, c * (pr + 1))
                st = st_ref[0, pr]
                dst = dstate[pr]
                stb, dstb = st.astype(BF16), dst.astype(BF16)
                xp = x[:, sl]
                xpb = xp.astype(BF16)
                dyp = dyv[:, sl]
                xdp = xp * q["dte_x"][:, sl]
                yoff_parts.append(_nn(cgb, stb) * q["ea_x"][:, sl])
                rr = _nn(bgb, dstb)
                dxp = rr * q["dte_x"][:, sl]
                u_parts.append(rr * xdp)
                v_parts.append(dst * st * q["eal_x"][:, sl])
                for k, hh in enumerate((2 * pr, 2 * pr + 1)):
                    decay = _ssd_decay(q, hh)
                    w = cb * decay
                    mask = q["left"] if k == 0 else jnp.logical_not(q["left"])
                    dym = jnp.where(mask, dyp, 0.0).astype(BF16)
                    dw = _nt(dym, xpb)
                    dgsum = dgsum + dw * decay
                    e_parts.append(dw * w)
                    dxp = dxp + _nn(w.T.astype(BF16), dym)
                dyeb = dye[:, sl].astype(BF16)
                dcg = dcg + _nt(dyeb, stb)
                dbg = dbg + _nt(xdp.astype(BF16), dstb)
                dstate[pr] = dst * q["eal_x"][:, sl] + _nn(cgt, dyeb)
                dx_parts.append(dxp)
            dcg = dcg + _nn(dgsum.astype(BF16), bgb)
            dbg = dbg + _nn(dgsum.T.astype(BF16), cgb)
            db_parts.append(dbg)
            dc_parts.append(dcg)
        dx = jnp.concatenate(dx_parts, axis=1)
        yoff = jnp.concatenate(yoff_parts, axis=1)
        u = jnp.concatenate(u_parts, axis=1)
        v = jnp.concatenate(v_parts, axis=1)
        reduce_heads = (_iota2((SSD_WIDTH, c), 0) // HEAD_DIM == _iota2((SSD_WIDTH, c), 1)).astype(BF16)
        to_head = (_iota2((SSD_HEADS * c, c), 0) // c == _iota2((SSD_HEADS * c, c), 1)).astype(BF16)
        da = _split_dot(dyv * yoff - u, reduce_heads, 2)
        da = da + _split_dot(jnp.concatenate(e_parts, axis=1), to_head, 2)
        da = da - _split_dot(jnp.concatenate(e_parts, axis=0), to_head, 2, dot=_tn)
        dalast = jnp.sum(_split_dot(u + v, reduce_heads, 2), axis=0, keepdims=True)
        da = da + jnp.where(_iota2((c, c), 0) == c - 1, dalast, 0.0)
        utri = (_iota2((c, c), 1) >= _iota2((c, c), 0)).astype(BF16)
        dda = _split_dot_left(utri, da)
        ddt = dda * q["a"] + _split_dot(dx * xs, reduce_heads, 2)
        dalog_ref[...] += jnp.sum(dda * q["dt"], axis=0, keepdims=True) * q["a"]
        ddtraw = jnp.where(_iota2((c, c), 1) < SSD_HEADS, ddt * _sigmoid(dtraw + dtb_ref[...]), 0.0)
        ddtb_ref[...] += jnp.sum(ddtraw, axis=0, keepdims=True)
        dzdt_ref[...] = jnp.concatenate([dz, ddtraw.astype(BF16), jnp.zeros((c, ZDT - SSD_WIDTH - LANES), BF16)], axis=1)
        dxs = dx * q["dt_x"] + dyv * dsk_ref[...]
        dxc_ref[...] = jnp.concatenate([dxs] + db_parts + dc_parts, axis=1)

    rev = lambda i: nc - 1 - i
    vec = lambda n: pl.BlockSpec((1, n), lambda i: (0, 0))
    wide = pl.BlockSpec((c, SSD_WIDTH), lambda i: (rev(i), 0))
    zdt = pl.BlockSpec((c, ZDT), lambda i: (rev(i), OFF_Z // ZDT))
    return pl.pallas_call(
        body, name=name, grid=(nc,),
        in_specs=[pl.BlockSpec((c, CONV_DIM), lambda i: (rev(i), 0)), zdt, wide, wide,
                  pl.BlockSpec((1, N_PAIRS, D_STATE, c), lambda i: (rev(i), 0, 0, 0)),
                  vec(LANES), vec(LANES), vec(SSD_WIDTH), vec(SSD_WIDTH)],
        out_specs=[pl.BlockSpec((c, CONV_DIM), lambda i: (rev(i), 0)), zdt,
                   vec(SSD_WIDTH), vec(SSD_WIDTH), vec(LANES), vec(LANES)],
        out_shape=[jax.ShapeDtypeStruct((s, CONV_DIM), F32), jax.ShapeDtypeStruct((s, D_INP), BF16),
                   jax.ShapeDtypeStruct((1, SSD_WIDTH), F32),
                   jax.ShapeDtypeStruct((1, SSD_WIDTH), F32), jax.ShapeDtypeStruct((1, LANES), F32),
                   jax.ShapeDtypeStruct((1, LANES), F32)],
        scratch_shapes=[pltpu.VMEM((N_PAIRS, D_STATE, c), F32)],
        compiler_params=_params(("arbitrary",)),
    )(xc, proj, ycore, dyall, states, dtb, alog, dskip_x, nw)


SB_Q, SB_K = 256, 512
SB_SCALE = HEAD_DIM ** -0.5


def _sb_weights(qm, kb, diagonal, run_lk, strict_after):
    z = _nt(qm, kb)
    nz = -z
    tail = jnp.log(1.0 + jnp.exp(jnp.minimum(z, nz)))
    ls = jnp.minimum(z, 0.0) - tail
    lk = jnp.minimum(nz, 0.0) - tail
    if diagonal is not None:
        valid = _iota2(z.shape, 1) < _iota2(z.shape, 0) + diagonal
        lk = jnp.where(valid, lk, 0.0)
    w = jnp.exp(ls + _split_dot(lk, strict_after, 2) + run_lk)
    if diagonal is not None:
        w = jnp.where(valid, w, 0.0)
    return ls, lk, w


def _sb_sweep(i, block, init):
    own = (i * SB_Q) // SB_K
    first = block(own, init, i * SB_Q - own * SB_K)
    return lax.fori_loop(1, own + 1, lambda jj, carry: block(own - jj, carry, None), first)


def sb_forward(proj, y_all, name):
    s = proj.shape[0]
    t, tk = SB_Q, SB_K
    nq = s // t

    def body(q_ref, k_ref, v_ref, _, y_ref, o_ref):
        i = pl.program_id(1)
        left = _iota2((t, LANES), 1) < HEAD_DIM
        left_k = _iota2((tk, LANES), 1) < HEAD_DIM
        qv = q_ref[...] * SB_SCALE
        zero = jnp.zeros_like(qv)
        qms = (jnp.where(left, qv, zero).astype(BF16), jnp.where(left, zero, qv).astype(BF16))
        strict_after = (_iota2((tk, tk), 0) > _iota2((tk, tk), 1)).astype(BF16)

        def block(j, carry, diagonal):
            o, runs = carry[0], carry[1:]
            rows = pl.ds(pl.multiple_of(j * tk, tk), tk)
            kb = k_ref[rows, :].astype(BF16)
            vv = v_ref[rows, :]
            new_runs = []
            for k in range(2):
                _, lk, w = _sb_weights(qms[k], kb, diagonal, runs[k], strict_after)
                vm = jnp.where(left_k if k == 0 else jnp.logical_not(left_k), vv, 0.0).astype(BF16)
                o = o + _nn(w.astype(BF16), vm)
                new_runs.append(runs[k] + jnp.sum(lk, axis=1, keepdims=True))
            return (o, *new_runs)

        init = (jnp.zeros((t, LANES), F32), jnp.zeros((t, 1), F32), jnp.zeros((t, 1), F32))
        o = _sb_sweep(i, block, init)[0]
        o_ref[...] = o
        y_ref[...] = o.astype(BF16)

    return pl.pallas_call(
        body, name=name, grid=(2, nq),
        in_specs=[pl.BlockSpec((t, LANES), lambda p, i: (i, 3 * p)),
                  pl.BlockSpec((s, LANES), lambda p, i: (0, 3 * p + 1)),
                  pl.BlockSpec((s, LANES), lambda p, i: (0, 3 * p + 2)), ANY],
        out_specs=[pl.BlockSpec((t, LANES), lambda p, i: (i, SSD_WIDTH // LANES + p)),
                   pl.BlockSpec((t, LANES), lambda p, i: (i, p))],
        out_shape=[jax.ShapeDtypeStruct(y_all.shape, BF16), jax.ShapeDtypeStruct((s, SB_WIDTH), F32)],
        input_output_aliases={3: 0},
        compiler_params=_params(("parallel", "arbitrary")),
    )(proj, proj, proj, y_all)


def sb_backward(proj, o, dyall, dproj, name, after=None):
    s = proj.shape[0]
    t, tk = SB_Q, SB_K
    nq = s // t
    specs, ops = _after(after)

    def body(q_ref, k_ref, v_ref, o_ref, do_ref, *rest):
        dqkv_ref, dk_acc, dv_acc = rest[-3:]
        dk_acc[...] = jnp.zeros_like(dk_acc)
        dv_acc[...] = jnp.zeros_like(dv_acc)
        left = _iota2((t, LANES), 1) < HEAD_DIM
        lane_masks = (left, jnp.logical_not(left))
        left_k = _iota2((tk, LANES), 1) < HEAD_DIM
        key_masks = (left_k, jnp.logical_not(left_k))
        strict_after = (_iota2((tk, tk), 0) > _iota2((tk, tk), 1)).astype(BF16)
        from_here = (_iota2((tk, tk), 0) >= _iota2((tk, tk), 1)).astype(BF16)

        def query_block(i, _):
            qrows = pl.ds(pl.multiple_of(i * t, t), t)
            qv = q_ref[qrows, :] * SB_SCALE
            dov = do_ref[qrows, :]
            zero = jnp.zeros_like(qv)
            qb = qv.astype(BF16)
            dob = dov.astype(BF16)
            prod = dob.astype(F32) * o_ref[qrows, :]
            qms = [jnp.where(m, qv, zero).astype(BF16) for m in lane_masks]
            doms = [jnp.where(m, dov, zero).astype(BF16) for m in lane_masks]
            deltas = [jnp.sum(jnp.where(m, prod, zero), axis=1, keepdims=True) for m in lane_masks]

            def block(j, carry, diagonal):
                dq = carry[0]
                run_lk, run_e = carry[1:3], carry[3:5]
                rows = pl.ds(pl.multiple_of(j * tk, tk), tk)
                kb = k_ref[rows, :].astype(BF16)
                vb = v_ref[rows, :].astype(BF16)
                dkj = jnp.zeros((tk, LANES), F32)
                dvj = jnp.zeros((tk, LANES), F32)
                new_lk, new_e = [], []
                for k in range(2):
                    ls, lk, w = _sb_weights(qms[k], kb, diagonal, run_lk[k], strict_after)
                    wb = w.astype(BF16)
                    e = _nt(doms[k], vb) * wb.astype(F32)
                    before = deltas[k] - _split_dot(e, from_here, 2) - run_e[k]
                    dz = e - jnp.exp(ls) * (e + before)
                    if diagonal is not None:
                        dz = jnp.where(_iota2(dz.shape, 1) < _iota2(dz.shape, 0) + diagonal, dz, 0.0)
                    dz = dz.astype(BF16)
                    m = lane_masks[k]
                    dvj = dvj + jnp.where(key_masks[k], _tn(wb, dob), 0.0)
                    dkj = dkj + jnp.where(key_masks[k], _tn(dz, qb), 0.0)
                    dq = dq + jnp.where(m, _nn(dz, kb), 0.0)
                    new_lk.append(run_lk[k] + jnp.sum(lk, axis=1, keepdims=True))
                    new_e.append(run_e[k] + jnp.sum(e, axis=1, keepdims=True))
                dk_acc[rows, :] += dkj
                dv_acc[rows, :] += dvj
                return (dq, *new_lk, *new_e)

            col = jnp.zeros((t, 1), F32)
            dq = _sb_sweep(i, block, (jnp.zeros((t, LANES), F32), col, col, col, col))[0]
            dqkv_ref[qrows, 0:LANES] = (dq * SB_SCALE).astype(BF16)
            return 0

        lax.fori_loop(0, nq, query_block, 0)
        dqkv_ref[:, LANES:2 * LANES] = dk_acc[...].astype(BF16)
        dqkv_ref[:, 2 * LANES:3 * LANES] = dv_acc[...].astype(BF16)

    col = lambda f: pl.BlockSpec((s, LANES), f)
    return pl.pallas_call(
        body, name=name, grid=(2,),
        in_specs=[col(lambda p: (0, 3 * p)), col(lambda p: (0, 3 * p + 1)), col(lambda p: (0, 3 * p + 2)),
                  col(lambda p: (0, p)), col(lambda p: (0, SSD_WIDTH // LANES + p)), ANY] + specs,
        out_specs=pl.BlockSpec((s, 3 * LANES), lambda p: (0, p)),
        out_shape=jax.ShapeDtypeStruct(dproj.shape, BF16),
        input_output_aliases={5: 0},
        scratch_shapes=[pltpu.VMEM((s, LANES), F32), pltpu.VMEM((s, LANES), F32)],
        compiler_params=_params(("parallel",)),
    )(proj, proj, proj, o, dyall, dproj, *ops)


def adamw(w, g, m, v, name):
    b, r, c = w.shape
    tr = max([t for t in range(8, min(r, 512) + 1, 8) if r % t == 0], default=r)

    def body(w_ref, g_ref, m_ref, v_ref, d_ref, nm_ref, nv_ref):
        gv = g_ref[...]
        nm = ADAM_B1 * m_ref[...] + (1.0 - ADAM_B1) * gv
        nv = ADAM_B2 * v_ref[...] + (1.0 - ADAM_B2) * (gv * gv)
        m_hat = nm / (1.0 - ADAM_B1 ** ADAM_STEP)
        v_hat = nv / (1.0 - ADAM_B2 ** ADAM_STEP)
        d_ref[...] = -ADAM_LR * (m_hat / (jnp.sqrt(v_hat) + ADAM_EPS) + ADAM_WD * w_ref[...])
        nm_ref[...] = nm
        nv_ref[...] = nv

    blk = pl.BlockSpec((1, tr, c), lambda i, j: (i, j, 0))
    return pl.pallas_call(
        body, name=name, grid=(b, r // tr),
        in_specs=[blk] * 4, out_specs=[blk] * 3,
        out_shape=[jax.ShapeDtypeStruct(w.shape, F32)] * 3,
        compiler_params=_params(("parallel", "parallel")),
    )(w, g, m, v)


def _position():
    return lax.axis_index("x"), lax.axis_index("y"), lax.axis_index("c")


def _flipped(pos, flip):
    return tuple((1 - p) if f else p for p, f in zip(pos, flip))


FLIP_C = (0, 0, 1)
CHIP_FLIPS = {1: (0, 1, 0), 2: (1, 0, 0), 3: (1, 1, 0)}
SHARD_ROWS = (SHARD_IN, SHARD_OUT, SHARD_FF, SHARD_FF, SHARD_FF)


def _rows(start, size):
    return pl.ds(pl.multiple_of(start, 16), size)


HBM = pl.BlockSpec(memory_space=pltpu.HBM)
SEM = pl.BlockSpec(memory_space=pltpu.SEMAPHORE)
EFFECT = pltpu.SideEffectType.DATAFLOW_SIDE_EFFECTING


def _in_hbm(a):
    return pltpu.with_memory_space_constraint(a, pltpu.HBM)


def _landing(shape, dtype):
    return _in_hbm(lax.empty(shape, dtype))


def _copies(plan, pos, src_refs, land_refs, send_sems, recv_sems):
    return [pltpu.make_async_remote_copy(src_ref=src, dst_ref=dst, send_sem=send_sems.at[k], recv_sem=recv_sems.at[k],
                                         device_id=_flipped(pos, flip), device_id_type=MESH)
            for k, (src, dst, flip) in enumerate(plan(pos, src_refs, land_refs))]


def exchange_start(name, srcs, lands, n, plan, after=None):
    ns, nl = len(srcs), len(lands)
    specs, ops = _after(after)

    def body(*refs):
        src_refs, land_refs = refs[:ns], refs[ns:ns + nl]
        send_sems, recv_sems, token = refs[ns + nl + len(ops)], refs[ns + nl + len(ops) + 1], refs[-1]
        for cp in _copies(plan, _position(), src_refs, land_refs, send_sems, recv_sems):
            cp.start()
        token[...] = jnp.zeros_like(token)

    thru = [pltpu.HBM(a.shape, a.dtype) for a in list(srcs) + list(lands)]
    out = pl.pallas_call(
        body, name=name,
        out_shape=(pltpu.SemaphoreType.DMA((n,)), pltpu.SemaphoreType.DMA((n,)), *thru, jax.ShapeDtypeStruct((8, LANES), F32)),
        in_specs=[HBM] * (ns + nl) + specs,
        out_specs=(SEM, SEM, *([HBM] * (ns + nl)), pl.BlockSpec(memory_space=pltpu.VMEM)),
        input_output_aliases={k: 2 + k for k in range(ns + nl)},
        compiler_params=pltpu.CompilerParams(has_side_effects=EFFECT),
    )(*[_in_hbm(a) for a in srcs], *lands, *ops)
    return out[0], out[1], list(out[2:2 + ns]), list(out[2 + ns:2 + ns + nl]), out[-1]


def exchange_wait(name, started, after, plan):
    send_sems, recv_sems, srcs, lands, _ = started
    ns, nl = len(srcs), len(lands)
    specs, ops = _after(after)

    def body(*refs):
        src_refs, land_refs = refs[:ns], refs[ns:ns + nl]
        send_sems, recv_sems = refs[ns + nl], refs[ns + nl + 1]
        for cp in _copies(plan, _position(), src_refs, land_refs, send_sems, recv_sems):
            cp.wait_send()
            cp.wait_recv()

    out = pl.pallas_call(
        body, name=name,
        out_shape=tuple(pltpu.HBM(a.shape, a.dtype) for a in list(srcs) + list(lands)),
        in_specs=[HBM] * (ns + nl) + [SEM, SEM] + specs,
        out_specs=tuple([HBM] * (ns + nl)),
        input_output_aliases={k: k for k in range(ns + nl)},
        compiler_params=pltpu.CompilerParams(has_side_effects=EFFECT),
    )(*srcs, *lands, send_sems, recv_sems, *ops)
    return list(out[:ns]), list(out[ns:])


def _gather_ici_plan(pos, srcs, lands):
    chip, c = 2 * pos[0] + pos[1], pos[2]
    copies = []
    for src, dst in zip(srcs, lands):
        r = src.shape[0]
        h = r // 2
        for f in (1, 2, 3):
            copies.append((src.at[_rows(c * h, h)], dst.at[_rows(chip * r + c * h, h)], CHIP_FLIPS[f]))
    return copies


def _gather_d2d_plan(pos, srcs, lands):
    chip, c = 2 * pos[0] + pos[1], pos[2]
    copies = []
    for own, dst in zip(srcs, lands):
        r = own.shape[0]
        h = r // 2
        copies.append((own, dst.at[_rows(chip * r, r)], FLIP_C))
        for f in (1, 2, 3):
            at = _rows(lax.bitwise_xor(chip, f) * r + c * h, h)
            copies.append((dst.at[at], dst.at[at], FLIP_C))
    return copies


def gather_ici_start(shards, after=None):
    lands = [_landing((N_CHIPS * a.shape[0], D_MODEL), BF16) for a in shards]
    return exchange_start("gather_ici_start", shards, lands, 3 * len(shards), _gather_ici_plan, after=after)


def gather_d2d_start(shards, fulls, after=None):
    return exchange_start("gather_d2d_start", shards, fulls, 4 * len(shards), _gather_d2d_plan, after=after)


def _reduce_d2d_plan(pos, srcs, lands):
    c = pos[2]
    return [(src.at[:, _rows((1 - c) * (src.shape[1] // 2), src.shape[1] // 2)], dst, FLIP_C) for src, dst in zip(srcs, lands)]


def _reduce_ici_plan(pos, srcs, lands):
    chip = 2 * pos[0] + pos[1]
    return [(src.at[lax.bitwise_xor(chip, f)], dst.at[f - 1], CHIP_FLIPS[f]) for src, dst in zip(srcs, lands) for f in (1, 2, 3)]


def _reduce_swap_plan(pos, srcs, lands):
    c = pos[2]
    copies = []
    for dst in lands:
        h = dst.shape[0] // 2
        at = _rows(c * h, h)
        copies.append((dst.at[at], dst.at[at], FLIP_C))
    return copies


def reduce_d2d_start(grads):
    lands = [_landing((N_CHIPS, g.shape[1] // 2, D_MODEL), BF16) for g in grads]
    return exchange_start("reduce_d2d_start", grads, lands, len(grads), _reduce_d2d_plan)


def reduce_ici_start(chip_sums):
    lands = [_landing((N_CHIPS - 1,) + p.shape[1:], BF16) for p in chip_sums]
    return exchange_start("reduce_ici_start", chip_sums, lands, 3 * len(chip_sums), _reduce_ici_plan)


def reduce_swap_start(mine):
    return exchange_start("reduce_swap_start", [], mine, len(mine), _reduce_swap_plan)


def add_halves(d, recv, half, name):
    nch, r, c = d.shape
    h = r // 2

    def body(half_ref, d_ref, r_ref, o_ref):
        o_ref[...] = (d_ref[...].astype(F32) + r_ref[...].astype(F32)).astype(BF16)

    return pl.pallas_call(
        body, name=name,
        grid_spec=pltpu.PrefetchScalarGridSpec(
            num_scalar_prefetch=1, grid=(nch,),
            in_specs=[pl.BlockSpec((1, h, c), lambda j, hf: (j, hf[0], 0)),
                      pl.BlockSpec((1, h, c), lambda j, hf: (j, 0, 0))],
            out_specs=pl.BlockSpec((1, h, c), lambda j, hf: (j, 0, 0))),
        out_shape=jax.ShapeDtypeStruct(recv.shape, BF16),
        compiler_params=_params(("parallel",)),
    )(half, d, recv)


def add_chips(p, recv, chip, name):
    _, r, c = p.shape

    def body(chip_ref, p_ref, r_ref, o_ref):
        acc = p_ref[0].astype(F32)
        for k in range(N_CHIPS - 1):
            acc = acc + r_ref[k].astype(F32)
        o_ref[...] = acc

    return pl.pallas_call(
        body, name=name,
        grid_spec=pltpu.PrefetchScalarGridSpec(
            num_scalar_prefetch=1, grid=(1,),
            in_specs=[pl.BlockSpec((1, r, c), lambda i, ch: (ch[0], 0, 0)),
                      pl.BlockSpec((N_CHIPS - 1, r, c), lambda i, ch: (0, 0, 0))],
            out_specs=pl.BlockSpec((r, c), lambda i, ch: (ch[1], 0))),
        out_shape=jax.ShapeDtypeStruct((2 * r, c), F32),
        compiler_params=_params(("arbitrary",)),
    )(chip, p, recv)


def adamw_layers(w, gs, m, v, name):
    b, r, c = w.shape
    tr = max([t for t in range(8, min(r, 512) + 1, 8) if r % t == 0], default=r)

    def body(w_ref, m_ref, v_ref, *rest):
        g_refs, (g_ref, d_ref, nm_ref, nv_ref) = rest[:b], rest[b:]
        layer = pl.program_id(0)
        gv = g_refs[0][...]
        for l in range(1, b):
            gv = jnp.where(layer == l, g_refs[l][...], gv)
        nm = ADAM_B1 * m_ref[0] + (1.0 - ADAM_B1) * gv
        nv = ADAM_B2 * v_ref[0] + (1.0 - ADAM_B2) * (gv * gv)
        m_hat = nm / (1.0 - ADAM_B1 ** ADAM_STEP)
        v_hat = nv / (1.0 - ADAM_B2 ** ADAM_STEP)
        g_ref[0] = gv
        d_ref[0] = -ADAM_LR * (m_hat / (jnp.sqrt(v_hat) + ADAM_EPS) + ADAM_WD * w_ref[0])
        nm_ref[0] = nm
        nv_ref[0] = nv

    nr, tc = r // tr, (c if tr < r else _tile(c, 256))
    steps = nr * (c // tc)
    blk = pl.BlockSpec((1, tr, tc), lambda i, j: (i, j % nr, j // nr))
    g_specs = [pl.BlockSpec((tr, tc), lambda i, j, l=l: (jnp.where(i == l, j % nr, jnp.where(i < l, 0, nr - 1)),
                                                         jnp.where(i == l, j // nr, jnp.where(i < l, 0, c // tc - 1))))
               for l in range(b)]
    return pl.pallas_call(
        body, name=name, grid=(b, steps),
        in_specs=[blk] * 3 + g_specs, out_specs=[blk] * 4,
        out_shape=[jax.ShapeDtypeStruct(w.shape, F32)] * 4,
        compiler_params=_params(("arbitrary", "arbitrary")),
    )(w, m, v, *gs)


def small_allreduce(v, name, after=None):
    r, c = v.shape
    specs, ops = _after(after)

    def body(v_ref, *rest):
        o_ref, buf, send_sems, recv_sems = rest[len(ops):]
        pos = _position()
        me = 4 * pos[0] + 2 * pos[1] + pos[2]
        buf[0] = v_ref[...]
        copies = []
        for f in range(1, 8):
            flip = ((f >> 2) & 1, (f >> 1) & 1, f & 1)
            cp = pltpu.make_async_remote_copy(
                src_ref=v_ref, dst_ref=buf.at[f], send_sem=send_sems.at[f - 1], recv_sem=recv_sems.at[f - 1],
                device_id=_flipped(pos, flip), device_id_type=MESH)
            cp.start()
            copies.append(cp)
        for cp in copies:
            cp.wait()
        acc = buf[me]
        for d in range(1, 8):
            acc = acc + buf[lax.bitwise_xor(me, d)]
        o_ref[...] = acc

    return pl.pallas_call(
        body, name=name,
        in_specs=[pl.BlockSpec(memory_space=pltpu.VMEM)] + specs, out_specs=pl.BlockSpec(memory_space=pltpu.VMEM),
        out_shape=jax.ShapeDtypeStruct((r, c), F32),
        scratch_shapes=[pltpu.VMEM((8, r, c), F32), pltpu.SemaphoreType.DMA((7,)), pltpu.SemaphoreType.DMA((7,))],
    )(v, *ops)


_IN_SEGMENTS = ((0, 1544, 128), (128, 1800, 128), (256, 2056, 128), (384, 1672, 128), (512, 1928, 128), (640, 2184, 128),
                (OFF_Z, 0, SSD_WIDTH), (OFF_DT, 1536, SSD_HEADS), (OFF_XBC, 512, CONV_DIM), (OFF_P, 2312, POOL_WIDTH))


def _in_column_map():
    m = np.full((D_INP,), -1, np.int64)
    for at, orig, n in _IN_SEGMENTS:
        cols = np.arange(orig, orig + n)
        m[at:at + n] = (cols // COLS_IN) * SHARD_IN + cols % COLS_IN
    return m


def take_rows(a, idx, name):
    dep, r_in, c = a.shape
    blk = LANES
    n_out, n_in = len(idx) // blk, r_in // blk
    assert len(idx) % blk == 0 and r_in % blk == 0
    sources = [sorted({int(v) // blk for v in idx[blk * i:blk * (i + 1)] if v >= 0}) for i in range(n_out)]
    width = max(len(s) for s in sources)
    table = np.zeros((n_out, width), np.int32)
    for i, s in enumerate(sources):
        spare = [b for b in range(n_in) if b not in s][:width - len(s)]
        table[i] = s + spare

    def body(tbl_ref, idx_ref, *refs):
        in_refs, o_ref = refs[:width], refs[width]
        i = pl.program_id(1)
        src = idx_ref[...]
        acc = jnp.zeros((blk, c), F32)
        for k in range(width):
            pick = (src == tbl_ref[i, k] * blk + _iota2((blk, blk), 1)).astype(BF16)
            acc = acc + _nn(pick, in_refs[k][0])
        o_ref[0] = acc.astype(BF16)

    return pl.pallas_call(
        body, name=name,
        grid_spec=pltpu.PrefetchScalarGridSpec(
            num_scalar_prefetch=1, grid=(dep, n_out),
            in_specs=[pl.BlockSpec((blk, 1), lambda l, i, t: (i, 0))] +
                     [pl.BlockSpec((1, blk, c), lambda l, i, t, k=k: (l, t[i, k], 0)) for k in range(width)],
            out_specs=pl.BlockSpec((1, blk, c), lambda l, i, t: (l, i, 0))),
        out_shape=jax.ShapeDtypeStruct((dep, len(idx), c), BF16),
        compiler_params=_params(("parallel", "parallel")),
    )(jnp.asarray(table), jnp.asarray(np.asarray(idx, np.int32).reshape(-1, 1)), *([a] * width))


def _in_weight_layout(staged):
    return take_rows(staged, _in_column_map(), "w_in_layout")


def _in_gradient_layout(dwt):
    fwd = _in_column_map()
    inv = np.full((N_CHIPS * SHARD_IN,), -1, np.int64)
    inv[fwd[fwd >= 0]] = np.nonzero(fwd >= 0)[0]
    return take_rows(dwt, inv, "dw_in_layout")


SMALL_NAMES = ("norm1_w", "conv_w", "conv_b", "dt_bias", "a_log", "d_skip", "ssd_norm_w", "pool_w", "pool_b",
               "pool_scale", "norm2_w", "final_norm_w")
SMALL_ROWS = 104


def _pack_small(parts):
    flat = jnp.concatenate([p.reshape(-1) for p in parts])
    return jnp.pad(flat, (0, SMALL_ROWS * D_MODEL - flat.shape[0])).reshape(SMALL_ROWS, D_MODEL)


def _unpack_small(flat, shapes):
    flat = flat.reshape(-1)
    out, at = [], 0
    for shp in shapes:
        n = int(np.prod(shp))
        out.append(flat[at:at + n].reshape(shp))
        at += n
    return out


def kernel(x, norm1_w, w_in, conv_w, conv_b, dt_bias, a_log, d_skip, ssd_norm_w, pool_w, pool_b, pool_scale, w_out, norm2_w, w_gate, w_up, w_down, final_norm_w, loss_target, m_norm1_w, m_w_in, m_conv_w, m_conv_b, m_dt_bias, m_a_log, m_d_skip, m_ssd_norm_w, m_pool_w, m_pool_b, m_pool_scale, m_w_out, m_norm2_w, m_w_gate, m_w_up, m_w_down, m_final_norm_w, v_norm1_w, v_w_in, v_conv_w, v_conv_b, v_dt_bias, v_a_log, v_d_skip, v_ssd_norm_w, v_pool_w, v_pool_b, v_pool_scale, v_w_out, v_norm2_w, v_w_gate, v_w_up, v_w_down, v_final_norm_w):
    px, py, pc = _position()
    chip = 2 * px + py
    chip_arr = jnp.reshape(chip, (1,)).astype(jnp.int32)
    half_arr = jnp.reshape(pc, (1,)).astype(jnp.int32)

    def layer_shards(l):
        w_in_t = jnp.pad(jnp.swapaxes(w_in[l], 0, 1).astype(BF16), ((0, SHARD_IN - COLS_IN), (0, 0)))
        return [w_in_t, w_out[l].astype(BF16), jnp.swapaxes(w_gate[l], 0, 1).astype(BF16),
                jnp.swapaxes(w_up[l], 0, 1).astype(BF16), w_down[l].astype(BF16)]

    over_ici = {0: gather_ici_start(layer_shards(0))}

    def pass_on(l, after):
        own, arrived = exchange_wait("gather_ici_wait", over_ici[l], after, _gather_ici_plan)
        swap = gather_d2d_start(own, arrived)
        tokens = [swap[4]]
        if l + 1 < DEPTH:
            over_ici[l + 1] = gather_ici_start(layer_shards(l + 1), after=swap[4])
            tokens.append(over_ici[l + 1][4])
        return swap, tokens

    def weights_of(swap, after):
        _, (w_in_st, w_out_l, w_gate_t, w_up_t, w_down_l) = exchange_wait("gather_d2d_wait", swap, after, _gather_d2d_plan)
        return _in_weight_layout(w_in_st[None])[0], w_out_l, w_gate_t, w_up_t, w_down_l

    pad_heads = lambda v: jnp.pad(v, ((0, 0), (0, LANES - SSD_HEADS)))[:, None, :]
    dtb, alog = pad_heads(dt_bias), pad_heads(a_log)
    dskip_x = jnp.repeat(d_skip, HEAD_DIM, axis=1)[:, None, :]
    eye = jnp.eye(len(POOL_WINDOWS), dtype=F32)
    wbd = (pool_w[:, :, :, None, :] * eye[None, :, None, :, None]).reshape(DEPTH, POOL_WIDTH, POOL_WIDTH).astype(BF16)
    pool_b2 = pool_b.reshape(DEPTH, 1, POOL_WIDTH)
    cw_cols = lax.dynamic_update_slice(jnp.zeros((DEPTH, CONV_WIDTH, CONV_DIM), F32), conv_w,
                                       (0, 0, chip * (CONV_DIM // N_CHIPS)))
    cw_cols = jnp.where(pc == 0, cw_cols, 0.0)
    cw_rows = (DEPTH * CONV_WIDTH * CONV_DIM) // D_MODEL
    conv_w_f = small_allreduce(jnp.pad(cw_cols.reshape(cw_rows, D_MODEL), ((0, 8), (0, 0))), "gather_conv_w")
    conv_w_f = conv_w_f[:cw_rows].reshape(DEPTH, CONV_WIDTH, CONV_DIM)
    cw8 = jnp.pad(conv_w_f, ((0, 0), (0, 8 - CONV_WIDTH), (0, 0)))

    h = x[0]
    saved, weights = [], []
    swap, tokens = pass_on(0, over_ici[0][4])
    weights.append(weights_of(swap, tokens))
    for l in range(DEPTH):
        w_in_f, w_out_f, w_gate_t, w_up_t, w_down_f = weights[l]
        proj = rms_matmul(h, norm1_w[l][None], w_in_f, "in_proj")
        xc = conv_forward(proj, cw8[l], conv_b[l][None], "conv_fwd")
        y_all, ycore, states = ssd_forward(proj, xc, dtb[l], alog[l], dskip_x[l], ssd_norm_w[l][None], "ssd_fwd")
        y_all, o_sb = sb_forward(proj, y_all, "sb_fwd")
        swap, tokens = pass_on(l + 1, o_sb) if l + 1 < DEPTH else (None, None)
        y_all = pool_forward(proj, wbd[l], pool_b2[l], pool_scale[l][None], y_all, "pool_fwd", after=tokens)
        x1 = matmul_residual(y_all, w_out_f, h, "out_proj")
        x2, g, u = ffn_forward(x1, norm2_w[l][None], w_gate_t, w_up_t, w_down_f, "ffn_fwd")
        if swap is not None:
            weights.append(weights_of(swap, x2))
        saved.append((h, proj, xc, ycore, states, o_sb, y_all, x1, g, u))
        h = x2

    loss_part, dx, dxb, d_final = loss_head(h, final_norm_w[None], loss_target[0], "loss_head")
    loss = lax.psum(loss_part[0, 0], ("x", "y", "c"))

    small = {n: [None] * DEPTH for n in SMALL_NAMES if n != "final_norm_w"}
    chip_half = jnp.concatenate([chip_arr, half_arr])
    reduced = {}
    d2d = ici = early = None

    def add_cores(d2d, after):
        mine, theirs = exchange_wait("reduce_d2d_wait", d2d[1], after, _reduce_d2d_plan)
        return d2d[0], reduce_ici_start([add_halves(d, t, half_arr, "reduce_add_halves") for d, t in zip(mine, theirs)])

    def add_all(ici, after):
        sums, theirs = exchange_wait("reduce_ici_wait", ici[1], after, _reduce_ici_plan)
        return ici[0], reduce_swap_start([add_chips(p, t, chip_half, "reduce_add_chips") for p, t in zip(sums, theirs)])

    def finish(swap, after):
        reduced[swap[0]] = exchange_wait("reduce_swap_wait", swap[1], after, _reduce_swap_plan)[1]

    swaps = []
    for l in reversed(range(DEPTH)):
        xin, proj, xc, ycore, states, o_sb, y_all, x1, g, u = saved[l]
        w_in_f, w_out_f, w_gate_t, w_up_t, w_down_f = weights[l]
        dg, du, act = ffn_backward_act(dxb, g, u, w_down_f, "ffn_bwd_act", after=None if d2d is None else d2d[1][4])
        dx1, dx1b, h2, dn2 = rms_backward([dg, du], [w_gate_t, w_up_t], x1, norm2_w[l][None], dx, "ffn_bwd_norm", 256)
        if d2d is not None:
            ici = add_cores(d2d, dx1b)
        dyall = matmul_nt(dx1b, w_out_f, "out_proj_bwd", after=None if ici is None else ici[1][4])
        dw_down = matmul_tn(act, dxb, "dw_down")
        dw_gate = matmul_tn(dg, h2, "dw_gate")
        dw_up = matmul_tn(du, h2, "dw_up")
        dw_out = matmul_tn(y_all, dx1b, "dw_out")
        late = [dw.reshape(N_CHIPS, r, D_MODEL) for dw, r in zip((dw_out, dw_gate, dw_up, dw_down), SHARD_ROWS[1:])]
        if l == 0:
            early = ("0 late", reduce_d2d_start(late))
        dxc, dproj, dsn, ddsk, ddtb, dalog = ssd_backward(proj, xc, ycore, dyall, states, dtb[l], alog[l],
                                                          dskip_x[l], ssd_norm_w[l][None], "ssd_bwd")
        dproj, dcw, dcb = conv_backward(proj, dxc, cw8[l], conv_b[l][None], dproj, "conv_bwd",
                                        after=None if early is None else early[1][4])
        if early is not None:
            early = add_cores(early, dproj)
        dproj = sb_backward(proj, o_sb, dyall, dproj, "sb_bwd", after=None if early is None else early[1][4])
        dproj, dwbd, dpb, dps = pool_backward(proj, dyall, wbd[l], pool_b2[l], pool_scale[l][None], dproj, "pool_bwd")
        if ici is not None:
            swaps.append(add_all(ici, dproj))
            ici = None
        dx, dxb, h1, dn1 = rms_backward([dproj], [w_in_f], xin, norm1_w[l][None], dx1, "in_proj_bwd", 256,
                                        after=swaps[-1][1][4] if swaps else None)
        dw_in = _in_gradient_layout(matmul_tn(dproj, h1, "dw_in")[None])[0].reshape(N_CHIPS, SHARD_IN, D_MODEL)
        d2d = (l, reduce_d2d_start([dw_in] if l == 0 else [dw_in] + late))
        small["norm1_w"][l] = dn1[0]
        small["conv_w"][l] = dcw[:CONV_WIDTH]
        small["conv_b"][l] = dcb[0]
        small["dt_bias"][l] = ddtb[0, :SSD_HEADS]
        small["a_log"][l] = dalog[0, :SSD_HEADS]
        small["d_skip"][l] = ddsk.reshape(SSD_HEADS, HEAD_DIM).sum(axis=1)
        small["ssd_norm_w"][l] = dsn[0]
        small["pool_w"][l] = jnp.stack([dwbd[64 * k:64 * k + 64, 64 * k:64 * k + 64] for k in range(len(POOL_WINDOWS))])
        small["pool_b"][l] = dpb.reshape(len(POOL_WINDOWS), -1)
        small["pool_scale"][l] = dps[0]
        small["norm2_w"][l] = dn2[0]
    grad_x = dx[None]

    ici = add_cores(d2d, d2d[1][4])
    small_parts = [d_final if n == "final_norm_w" else jnp.stack(small[n]) for n in SMALL_NAMES]
    small_sum = small_allreduce(_pack_small(small_parts), "reduce_small", after=ici[1][4])
    swaps.append(add_all(early, small_sum))
    swaps.append(add_all(ici, swaps[-1][1][4]))
    for swap in swaps:
        finish(swap, swaps[-1][1][4])
    reduced[0] = reduced[0] + reduced["0 late"]
    g_big = {n: [reduced[l][k] for l in range(DEPTH)] for k, n in enumerate(("w_in", "w_out", "w_gate", "w_up", "w_down"))}
    g_big["w_in"] = [gl[:COLS_IN] for gl in g_big["w_in"]]
    transposed = ("w_in", "w_gate", "w_up")

    g_small = dict(zip(SMALL_NAMES, _unpack_small(small_sum, [p.shape for p in small_parts])))
    g_small["final_norm_w"] = g_small["final_norm_w"].reshape(final_norm_w.shape)
    g_small["conv_w"] = lax.dynamic_slice_in_dim(g_small["conv_w"], chip * (CONV_DIM // N_CHIPS), CONV_DIM // N_CHIPS, axis=2)

    given = dict(norm1_w=(norm1_w, m_norm1_w, v_norm1_w), w_in=(w_in, m_w_in, v_w_in), conv_w=(conv_w, m_conv_w, v_conv_w),
                 conv_b=(conv_b, m_conv_b, v_conv_b), dt_bias=(dt_bias, m_dt_bias, v_dt_bias), a_log=(a_log, m_a_log, v_a_log),
                 d_skip=(d_skip, m_d_skip, v_d_skip), ssd_norm_w=(ssd_norm_w, m_ssd_norm_w, v_ssd_norm_w),
                 pool_w=(pool_w, m_pool_w, v_pool_w), pool_b=(pool_b, m_pool_b, v_pool_b),
                 pool_scale=(pool_scale, m_pool_scale, v_pool_scale), w_out=(w_out, m_w_out, v_w_out),
                 norm2_w=(norm2_w, m_norm2_w, v_norm2_w), w_gate=(w_gate, m_w_gate, v_w_gate), w_up=(w_up, m_w_up, v_w_up),
                 w_down=(w_down, m_w_down, v_w_down), final_norm_w=(final_norm_w, m_final_norm_w, v_final_norm_w))
    order = ("norm1_w", "w_in", "conv_w", "conv_b", "dt_bias", "a_log", "d_skip", "ssd_norm_w", "pool_w", "pool_b",
             "pool_scale", "w_out", "norm2_w", "w_gate", "w_up", "w_down", "final_norm_w")
    grads = dict(g_small)
    results = {}
    for n in ("w_in", "w_out", "w_gate", "w_up", "w_down"):
        w, m, v = given[n]
        if n in transposed:
            out = adamw_layers(jnp.swapaxes(w, 1, 2), g_big[n], jnp.swapaxes(m, 1, 2), jnp.swapaxes(v, 1, 2), "adamw_" + n)
            out = [jnp.swapaxes(o, 1, 2) for o in out]
        else:
            out = adamw_layers(w, g_big[n], m, v, "adamw_" + n)
        grads[n], results[n] = out[0], tuple(out[1:])
    small_shapes = [given[n][0].shape for n in SMALL_NAMES]
    packed = [_pack_small([given[n][k] for n in SMALL_NAMES])[None] for k in range(3)]
    packed_g = _pack_small([grads[n] for n in SMALL_NAMES])[None]
    small_out = adamw(packed[0], packed_g, packed[1], packed[2], "adamw_small")
    small_out = [_unpack_small(o[0], small_shapes) for o in small_out]
    for i, n in enumerate(SMALL_NAMES):
        results[n] = tuple(small_out[k][i] for k in range(3))

    return (loss, grad_x, *[grads[n] for n in order], *[results[n][0] for n in order],
            *[results[n][1] for n in order], *[results[n][2] for n in order])
```

```python
import numpy as np
import jax
import jax.numpy as jnp
from jax import lax
from jax.experimental import pallas as pl
from jax.experimental.pallas import tpu as pltpu

F32 = jnp.float32
BF16 = jnp.bfloat16
MESH = pl.DeviceIdType.MESH
ANY = pl.BlockSpec(memory_space=pl.ANY)

D_MODEL = 1024
DEPTH = 4
EPS = 1e-6
SSD_WIDTH = 512
SSD_HEADS = 8
HEAD_DIM = 64
D_STATE = 128
CHUNK = 128
CONV_WIDTH = 4
CONV_DIM = 1024
SB_WIDTH = 256
POOL_WIDTH = 256
POOL_WINDOWS = (2, 4, 8, 16)
D_FF = 2816
D_IN = 2568
N_CHIPS = 4
OFF_QKV, OFF_Z, OFF_DT, OFF_XBC, OFF_P = 0, 768, 1280, 1536, 2560
D_INP = 2816
ZDT = 768
SHARD_IN, SHARD_OUT, SHARD_FF = 672, 256, 704
COLS_IN = 642
ADAM_LR, ADAM_B1, ADAM_B2, ADAM_EPS, ADAM_WD, ADAM_STEP = 0.001, 0.9, 0.999, 1e-08, 0.01, 10
LANES = 128
VMEM_LIMIT = 56 * 1024 * 1024


def _params(sem=None):
    return pltpu.CompilerParams(dimension_semantics=sem, vmem_limit_bytes=VMEM_LIMIT)


def _tile(n, cap):
    best = None
    for t in range(LANES, min(n, cap) + 1, LANES):
        if n % t == 0:
            best = t
    assert best is not None, (n, cap)
    return best


def _nt(a, b):
    return lax.dot_general(a, b, (((1,), (1,)), ((), ())), preferred_element_type=F32)


def _tn(a, b):
    return lax.dot_general(a, b, (((0,), (0,)), ((), ())), preferred_element_type=F32)


def _nn(a, b):
    return jnp.dot(a, b, preferred_element_type=F32)


def _split_dot(a, b_exact, terms=3, dot=_nn):
    acc = None
    rest = a
    for _ in range(terms):
        hi = rest.astype(BF16)
        part = dot(hi, b_exact)
        acc = part if acc is None else acc + part
        rest = rest - hi.astype(F32)
    return acc


def _split_dot_left(a_exact, b, terms=3):
    acc = None
    rest = b
    for _ in range(terms):
        hi = rest.astype(BF16)
        part = _nn(a_exact, hi)
        acc = part if acc is None else acc + part
        rest = rest - hi.astype(F32)
    return acc


def _sigmoid(x):
    return 1.0 / (1.0 + jnp.exp(-x))


def _softplus(x):
    return jnp.maximum(x, 0.0) + jnp.log(1.0 + jnp.exp(-jnp.abs(x)))


def _iota2(shape, dim):
    return lax.broadcasted_iota(jnp.int32, shape, dim)


def _after(after):
    ops = [] if after is None else list(after) if isinstance(after, (list, tuple)) else [after]
    return [ANY] * len(ops), ops


def rms_matmul(x, nw, wt, name, after=None):
    s, d = x.shape
    n = wt.shape[0]
    tm, tn = _tile(s, 512), _tile(n, 2816)
    specs, ops = _after(after)

    def body(x_ref, nw_ref, w_ref, *rest):
        o_ref, h_ref = rest[len(ops):]

        @pl.when(pl.program_id(1) == 0)
        def _():
            xv = x_ref[...]
            r = lax.rsqrt(jnp.mean(xv * xv, axis=-1, keepdims=True) + EPS)
            h_ref[...] = (xv * r * nw_ref[...]).astype(BF16)
        o_ref[...] = _nt(h_ref[...], w_ref[...])

    return pl.pallas_call(
        body, name=name, grid=(s // tm, n // tn),
        in_specs=[pl.BlockSpec((tm, d), lambda i, j: (i, 0)), pl.BlockSpec((1, d), lambda i, j: (0, 0)),
                  pl.BlockSpec((tn, d), lambda i, j: (j, 0))] + specs,
        out_specs=pl.BlockSpec((tm, tn), lambda i, j: (i, j)),
        out_shape=jax.ShapeDtypeStruct((s, n), F32),
        scratch_shapes=[pltpu.VMEM((tm, d), BF16)],
        compiler_params=_params(("parallel", "arbitrary")),
    )(x, nw, wt, *ops)


def matmul_residual(a, w, res, name):
    s, k = a.shape
    n = w.shape[1]
    tm, tn = _tile(s, 512), _tile(n, 1024)

    def body(a_ref, w_ref, r_ref, o_ref):
        o_ref[...] = r_ref[...] + _nn(a_ref[...], w_ref[...])

    return pl.pallas_call(
        body, name=name, grid=(s // tm, n // tn),
        in_specs=[pl.BlockSpec((tm, k), lambda i, j: (i, 0)), pl.BlockSpec((k, tn), lambda i, j: (0, j)),
                  pl.BlockSpec((tm, tn), lambda i, j: (i, j))],
        out_specs=pl.BlockSpec((tm, tn), lambda i, j: (i, j)),
        out_shape=jax.ShapeDtypeStruct((s, n), F32),
        compiler_params=_params(("parallel", "parallel")),
    )(a, w, res)


def matmul_nt(a, w, name, out_dtype=F32, after=None):
    s, n = a.shape
    k = w.shape[0]
    tm, tk = _tile(s, 512), _tile(k, 1024)
    specs, ops = _after(after)

    def body(a_ref, w_ref, *rest):
        rest[-1][...] = _nt(a_ref[...], w_ref[...]).astype(out_dtype)

    return pl.pallas_call(
        body, name=name, grid=(s // tm, k // tk),
        in_specs=[pl.BlockSpec((tm, n), lambda i, j: (i, 0)), pl.BlockSpec((tk, n), lambda i, j: (j, 0))] + specs,
        out_specs=pl.BlockSpec((tm, tk), lambda i, j: (i, j)),
        out_shape=jax.ShapeDtypeStruct((s, k), out_dtype),
        compiler_params=_params(("parallel", "parallel")),
    )(a, w, *ops)


def matmul_tn(a, b, name, after=None):
    s, m = a.shape
    n = b.shape[1]
    tm, tn = _tile(m, 512), _tile(n, 1024)

    def body(a_ref, b_ref, *rest):
        rest[-1][...] = _tn(a_ref[...], b_ref[...]).astype(BF16)

    specs, ops = _after(after)
    return pl.pallas_call(
        body, name=name, grid=(m // tm, n // tn),
        in_specs=[pl.BlockSpec((s, tm), lambda i, j: (0, i)), pl.BlockSpec((s, tn), lambda i, j: (0, j))] + specs,
        out_specs=pl.BlockSpec((tm, tn), lambda i, j: (i, j)),
        out_shape=jax.ShapeDtypeStruct((m, n), BF16),
        compiler_params=_params(("parallel", "parallel")),
    )(a, b, *ops)


def ffn_forward(x1, nw, wgt, wut, wd, name):
    s, d = x1.shape
    f = wgt.shape[0]
    tm, tf = _tile(s, 1024), _tile(f, 256)

    def body(x_ref, nw_ref, wg_ref, wu_ref, wd_ref, o_ref, g_ref, u_ref, h_ref, acc_ref):
        j = pl.program_id(1)

        @pl.when(j == 0)
        def _():
            xv = x_ref[...]
            r = lax.rsqrt(jnp.mean(xv * xv, axis=-1, keepdims=True) + EPS)
            h_ref[...] = (xv * r * nw_ref[...]).astype(BF16)
            acc_ref[...] = xv

        h = h_ref[...]
        g = _nt(h, wg_ref[...])
        u = _nt(h, wu_ref[...])
        g_ref[...] = g.astype(BF16)
        u_ref[...] = u.astype(BF16)
        a = (g * _sigmoid(g) * u).astype(BF16)
        acc_ref[...] += _nn(a, wd_ref[...])

        @pl.when(j == pl.num_programs(1) - 1)
        def _():
            o_ref[...] = acc_ref[...]

    wblk = pl.BlockSpec((tf, d), lambda i, j: (j, 0))
    return pl.pallas_call(
        body, name=name, grid=(s // tm, f // tf),
        in_specs=[pl.BlockSpec((tm, d), lambda i, j: (i, 0)), pl.BlockSpec((1, d), lambda i, j: (0, 0)), wblk, wblk, wblk],
        out_specs=[pl.BlockSpec((tm, d), lambda i, j: (i, 0)), pl.BlockSpec((tm, tf), lambda i, j: (i, j)),
                   pl.BlockSpec((tm, tf), lambda i, j: (i, j))],
        out_shape=[jax.ShapeDtypeStruct((s, d), F32), jax.ShapeDtypeStruct((s, f), BF16),
                   jax.ShapeDtypeStruct((s, f), BF16)],
        scratch_shapes=[pltpu.VMEM((tm, d), BF16), pltpu.VMEM((tm, d), F32)],
        compiler_params=_params(("parallel", "arbitrary")),
    )(x1, nw, wgt, wut, wd)


def ffn_backward_act(dx2, g, u, wd, name, after=None):
    s, d = dx2.shape
    f = wd.shape[0]
    tm, tf = _tile(s, 256), _tile(f, 2816)
    specs, ops = _after(after)

    def body(dx_ref, g_ref, u_ref, wd_ref, *rest):
        dg_ref, du_ref, a_ref = rest[len(ops):]
        da = _nt(dx_ref[...], wd_ref[...])
        gv = g_ref[...].astype(F32)
        uv = u_ref[...].astype(F32)
        sg = _sigmoid(gv)
        silu = gv * sg
        dg_ref[...] = (da * uv * (sg * (1.0 + gv * (1.0 - sg)))).astype(BF16)
        du_ref[...] = (da * silu).astype(BF16)
        a_ref[...] = (silu * uv).astype(BF16)

    blk = pl.BlockSpec((tm, tf), lambda i, j: (i, j))
    return pl.pallas_call(
        body, name=name, grid=(s // tm, f // tf),
        in_specs=[pl.BlockSpec((tm, d), lambda i, j: (i, 0)), blk, blk, pl.BlockSpec((tf, d), lambda i, j: (j, 0))] + specs,
        out_specs=[blk, blk, blk],
        out_shape=[jax.ShapeDtypeStruct((s, f), BF16)] * 3,
        compiler_params=_params(("parallel", "parallel")),
    )(dx2, g, u, wd, *ops)


def rms_backward(dzs, wts, x, nw, dres, name, tm, after=None):
    s, d = x.shape
    nz = len(dzs)
    specs, ops = _after(after)

    def body(*refs):
        dz_refs, w_refs = refs[:nz], refs[nz:2 * nz]
        x_ref, nw_ref, dres_ref = refs[2 * nz:2 * nz + 3]
        dx_ref, dxb_ref, h_ref, dnw_ref = refs[2 * nz + 3 + len(ops):]
        dh = _nn(dz_refs[0][...], w_refs[0][...])
        for k in range(1, nz):
            dh = dh + _nn(dz_refs[k][...], w_refs[k][...])
        xv = x_ref[...]
        r = lax.rsqrt(jnp.mean(xv * xv, axis=-1, keepdims=True) + EPS)
        xhat = xv * r
        nwv = nw_ref[...]
        h_ref[...] = (xhat * nwv).astype(BF16)

        @pl.when(pl.program_id(0) == 0)
        def _():
            dnw_ref[...] = jnp.zeros_like(dnw_ref)

        dnw_ref[...] += jnp.sum(dh * xhat, axis=0, keepdims=True)
        gdh = dh * nwv
        dx = dres_ref[...] + r * (gdh - xhat * jnp.mean(gdh * xhat, axis=-1, keepdims=True))
        dx_ref[...] = dx
        dxb_ref[...] = dx.astype(BF16)

    row = pl.BlockSpec((tm, d), lambda i: (i, 0))
    in_specs = [pl.BlockSpec((tm, dz.shape[1]), lambda i: (i, 0)) for dz in dzs]
    in_specs += [pl.BlockSpec(w.shape, lambda i: (0, 0)) for w in wts]
    in_specs += [row, pl.BlockSpec((1, d), lambda i: (0, 0)), row] + specs
    return pl.pallas_call(
        body, name=name, grid=(s // tm,),
        in_specs=in_specs,
        out_specs=[row, row, row, pl.BlockSpec((1, d), lambda i: (0, 0))],
        out_shape=[jax.ShapeDtypeStruct((s, d), F32), jax.ShapeDtypeStruct((s, d), BF16),
                   jax.ShapeDtypeStruct((s, d), BF16), jax.ShapeDtypeStruct((1, d), F32)],
        compiler_params=_params(("arbitrary",)),
    )(*dzs, *wts, x, nw, dres, *ops)


def loss_head(x, nw, target, name):
    s, d = x.shape
    tm = _tile(s, 512)

    def body(x_ref, nw_ref, t_ref, loss_ref, dx_ref, dxb_ref, dnw_ref):
        xv = x_ref[...]
        r = lax.rsqrt(jnp.mean(xv * xv, axis=-1, keepdims=True) + EPS)
        xhat = xv * r
        nwv = nw_ref[...]
        err = xhat * nwv - t_ref[...]

        @pl.when(pl.program_id(0) == 0)
        def _():
            dnw_ref[...] = jnp.zeros_like(dnw_ref)
            loss_ref[...] = jnp.zeros_like(loss_ref)

        part = jnp.sum(jnp.sum(err * err, axis=-1, keepdims=True), axis=0, keepdims=True) * (0.5 / d)
        loss_ref[...] += jnp.broadcast_to(part, loss_ref.shape)
        dout = err * (1.0 / d)
        dnw_ref[...] += jnp.sum(dout * xhat, axis=0, keepdims=True)
        gdh = dout * nwv
        dx = r * (gdh - xhat * jnp.mean(gdh * xhat, axis=-1, keepdims=True))
        dx_ref[...] = dx
        dxb_ref[...] = dx.astype(BF16)

    row = pl.BlockSpec((tm, d), lambda i: (i, 0))
    return pl.pallas_call(
        body, name=name, grid=(s // tm,),
        in_specs=[row, pl.BlockSpec((1, d), lambda i: (0, 0)), row],
        out_specs=[pl.BlockSpec((1, LANES), lambda i: (0, 0)), row, row, pl.BlockSpec((1, d), lambda i: (0, 0))],
        out_shape=[jax.ShapeDtypeStruct((1, LANES), F32), jax.ShapeDtypeStruct((s, d), F32),
                   jax.ShapeDtypeStruct((s, d), BF16), jax.ShapeDtypeStruct((1, d), F32)],
        compiler_params=_params(("arbitrary",)),
    )(x, nw, target)


def _shift_down(x, k):
    return jnp.where(_iota2(x.shape, 0) >= k, pltpu.roll(x, k, axis=0), 0.0)


def _shift_up(x, k):
    s = x.shape[0]
    return jnp.where(_iota2(x.shape, 0) < s - k, pltpu.roll(x, s - k, axis=0), 0.0)


CONV_TILE = 256


def conv_forward(proj, cw, cb, name):
    s = proj.shape[0]
    tn = CONV_TILE
    off = OFF_XBC // tn

    def body(u_ref, w_ref, b_ref, o_ref):
        u = u_ref[...]
        pre = b_ref[...] + w_ref[CONV_WIDTH - 1:CONV_WIDTH, :] * u
        for i in range(CONV_WIDTH - 1):
            pre = pre + w_ref[i:i + 1, :] * _shift_down(u, CONV_WIDTH - 1 - i)
        o_ref[...] = pre * _sigmoid(pre)

    return pl.pallas_call(
        body, name=name, grid=(CONV_DIM // tn,),
        in_specs=[pl.BlockSpec((s, tn), lambda j: (0, off + j)), pl.BlockSpec((8, tn), lambda j: (0, j)),
                  pl.BlockSpec((1, tn), lambda j: (0, j))],
        out_specs=pl.BlockSpec((s, tn), lambda j: (0, j)),
        out_shape=jax.ShapeDtypeStruct((s, CONV_DIM), F32),
        compiler_params=_params(("parallel",)),
    )(proj, cw, cb)


def conv_backward(proj, dxc, cw, cb, dproj, name, after=None):
    s = proj.shape[0]
    tn = CONV_TILE
    off = OFF_XBC // tn

    specs, ops = _after(after)

    def body(u_ref, d_ref, w_ref, b_ref, *rest):
        du_ref, dw_ref, db_ref = rest[-3:]
        u = u_ref[...]
        shifted = [_shift_down(u, CONV_WIDTH - 1 - i) for i in range(CONV_WIDTH - 1)] + [u]
        pre = b_ref[...] + w_ref[CONV_WIDTH - 1:CONV_WIDTH, :] * u
        for i in range(CONV_WIDTH - 1):
            pre = pre + w_ref[i:i + 1, :] * shifted[i]
        sg = _sigmoid(pre)
        dpre = d_ref[...] * (sg * (1.0 + pre * (1.0 - sg)))
        du = w_ref[CONV_WIDTH - 1:CONV_WIDTH, :] * dpre
        for i in range(CONV_WIDTH - 1):
            du = du + w_ref[i:i + 1, :] * _shift_up(dpre, CONV_WIDTH - 1 - i)
        du_ref[...] = du.astype(BF16)
        rows = [jnp.sum(dpre * shifted[i], axis=0, keepdims=True) for i in range(CONV_WIDTH)]
        rows.append(jnp.zeros((8 - CONV_WIDTH, tn), F32))
        dw_ref[...] = jnp.concatenate(rows, axis=0)
        db_ref[...] = jnp.sum(dpre, axis=0, keepdims=True)

    return pl.pallas_call(
        body, name=name, grid=(CONV_DIM // tn,),
        in_specs=[pl.BlockSpec((s, tn), lambda j: (0, off + j)), pl.BlockSpec((s, tn), lambda j: (0, j)),
                  pl.BlockSpec((8, tn), lambda j: (0, j)), pl.BlockSpec((1, tn), lambda j: (0, j)), ANY] + specs,
        out_specs=[pl.BlockSpec((s, tn), lambda j: (0, off + j)), pl.BlockSpec((8, tn), lambda j: (0, j)),
                   pl.BlockSpec((1, tn), lambda j: (0, j))],
        out_shape=[jax.ShapeDtypeStruct(dproj.shape, BF16), jax.ShapeDtypeStruct((8, CONV_DIM), F32),
                   jax.ShapeDtypeStruct((1, CONV_DIM), F32)],
        input_output_aliases={4: 0},
        compiler_params=_params(("parallel",)),
    )(proj, dxc, cw, cb, dproj, *ops)


def _pool_lane_window(shape):
    grp = _iota2(shape, 1) // (POOL_WIDTH // len(POOL_WINDOWS))
    win = jnp.full(shape, POOL_WINDOWS[-1], jnp.int32)
    for gi in range(len(POOL_WINDOWS) - 2, -1, -1):
        win = jnp.where(grp == gi, POOL_WINDOWS[gi], win)
    return grp, win


def _pool_select(grp, sums):
    out = sums[-1]
    for gi in range(len(sums) - 2, -1, -1):
        out = jnp.where(grp == gi, sums[gi], out)
    return out


def _pool_pooled(p):
    grp, win = _pool_lane_window(p.shape)
    inv_count = 1.0 / jnp.minimum(_iota2(p.shape, 0) + 1, win).astype(F32)
    sums, acc, k = [], p, 1
    for _ in POOL_WINDOWS:
        acc = acc + _shift_down(acc, k)
        sums.append(acc)
        k *= 2
    return _pool_select(grp, sums) * inv_count - p, grp, inv_count


def pool_forward(proj, wbd, pb, ps, y_all, name, after=None):
    s = proj.shape[0]
    specs, ops = _after(after)

    def body(p_ref, w_ref, b_ref, s_ref, *rest):
        o_ref = rest[-1]
        pooled, _, _ = _pool_pooled(p_ref[...])
        mixed = _nn(pooled.astype(BF16), w_ref[...]) + b_ref[...]
        o_ref[...] = (mixed * s_ref[...]).astype(BF16)

    vec = pl.BlockSpec((1, POOL_WIDTH), lambda j: (0, 0))
    return pl.pallas_call(
        body, name=name, grid=(1,),
        in_specs=[pl.BlockSpec((s, POOL_WIDTH), lambda j: (0, OFF_P // POOL_WIDTH)),
                  pl.BlockSpec((POOL_WIDTH, POOL_WIDTH), lambda j: (0, 0)), vec, vec, ANY] + specs,
        out_specs=pl.BlockSpec((s, POOL_WIDTH), lambda j: (0, (SSD_WIDTH + SB_WIDTH) // POOL_WIDTH)),
        out_shape=jax.ShapeDtypeStruct(y_all.shape, BF16),
        input_output_aliases={4: 0},
        compiler_params=_params(("arbitrary",)),
    )(proj, wbd, pb, ps, y_all, *ops)


def pool_backward(proj, dyall, wbd, pb, ps, dproj, name):
    s = proj.shape[0]

    def body(p_ref, dy_ref, w_ref, b_ref, s_ref, _, dp_ref, dw_ref, db_ref, ds_ref):
        pooled, grp, inv_count = _pool_pooled(p_ref[...])
        pooled_b = pooled.astype(BF16)
        mixed = _nn(pooled_b, w_ref[...]) + b_ref[...]
        dy = dy_ref[...]
        ds_ref[...] = jnp.sum(dy * mixed, axis=0, keepdims=True)
        dmixed = dy * s_ref[...]
        db_ref[...] = jnp.sum(dmixed, axis=0, keepdims=True)
        dmixed_b = dmixed.astype(BF16)
        dw_ref[...] = _tn(pooled_b, dmixed_b)
        dpooled = _nt(dmixed_b, w_ref[...])
        sums, acc, k = [], dpooled * inv_count, 1
        for _ in POOL_WINDOWS:
            acc = acc + _shift_up(acc, k)
            sums.append(acc)
            k *= 2
        dp_ref[...] = (_pool_select(grp, sums) - dpooled).astype(BF16)

    vec = pl.BlockSpec((1, POOL_WIDTH), lambda j: (0, 0))
    mat = pl.BlockSpec((POOL_WIDTH, POOL_WIDTH), lambda j: (0, 0))
    pcol = pl.BlockSpec((s, POOL_WIDTH), lambda j: (0, OFF_P // POOL_WIDTH))
    return pl.pallas_call(
        body, name=name, grid=(1,),
        in_specs=[pcol, pl.BlockSpec((s, POOL_WIDTH), lambda j: (0, (SSD_WIDTH + SB_WIDTH) // POOL_WIDTH)), mat, vec, vec, ANY],
        out_specs=[pcol, mat, vec, vec],
        out_shape=[jax.ShapeDtypeStruct(dproj.shape, BF16), jax.ShapeDtypeStruct((POOL_WIDTH, POOL_WIDTH), F32),
                   jax.ShapeDtypeStruct((1, POOL_WIDTH), F32), jax.ShapeDtypeStruct((1, POOL_WIDTH), F32)],
        input_output_aliases={5: 0},
        compiler_params=_params(("arbitrary",)),
    )(proj, dyall, wbd, pb, ps, dproj)


N_PAIRS = SSD_HEADS // 2


def _ssd_common(xc, dtraw, dtb, alog):
    c = CHUNK
    dt = _softplus(dtraw + dtb)
    a = -jnp.exp(alog)
    ltri = (_iota2((c, c), 0) >= _iota2((c, c), 1)).astype(BF16)
    acum = _split_dot_left(ltri, dt * a)
    expand = (_iota2((c, SSD_WIDTH), 1) // HEAD_DIM == _iota2((c, SSD_WIDTH), 0)).astype(BF16)
    expand_wide = (_iota2((c, SSD_HEADS * c), 1) // c == _iota2((c, SSD_HEADS * c), 0)).astype(BF16)
    acum_x = _split_dot(acum, expand)
    dt_x = _split_dot(dt, expand)
    alast_x = acum_x[c - 1:c, :]
    return dict(dt=dt, a=a, acum=acum, acum_x=acum_x, dt_x=dt_x, ea_x=jnp.exp(acum_x),
                dte_x=jnp.exp(alast_x - acum_x), eal_x=jnp.exp(alast_x),
                acol=_split_dot(acum, expand_wide), acum_t=acum.T,
                xs=xc[:, :SSD_WIDTH], causal=_iota2((c, c), 0) >= _iota2((c, c), 1),
                left=_iota2((c, c), 1) < HEAD_DIM)


def _ssd_group(xc, g):
    b = xc[:, SSD_WIDTH + D_STATE * g:SSD_WIDTH + D_STATE * (g + 1)]
    cm = xc[:, SSD_WIDTH + 2 * D_STATE + D_STATE * g:SSD_WIDTH + 2 * D_STATE + D_STATE * (g + 1)]
    return b, cm


def _ssd_decay(q, hh):
    col = q["acol"][:, CHUNK * hh:CHUNK * (hh + 1)]
    row = q["acum_t"][hh:hh + 1, :]
    return jnp.where(q["causal"], jnp.exp(jnp.minimum(col - row, 0.0)), 0.0)


def ssd_forward(proj, xc, dtb, alog, dskip_x, nw, name):
    s = xc.shape[0]
    nc = s // CHUNK

    def body(xc_ref, zdt_ref, dtb_ref, alog_ref, dsk_ref, nw_ref, y_ref, yc_ref, st_ref, state):
        @pl.when(pl.program_id(0) == 0)
        def _():
            state[...] = jnp.zeros_like(state)

        xcv = xc_ref[...]
        q = _ssd_common(xcv, zdt_ref[:, SSD_WIDTH:SSD_WIDTH + LANES], dtb_ref[...], alog_ref[...])
        x = q["xs"] * q["dt_x"]
        xb = x.astype(BF16)
        xd = (x * q["dte_x"]).astype(BF16)
        pieces = []
        for g in range(2):
            bg, cg = _ssd_group(xcv, g)
            bgb, cgb = bg.astype(BF16), cg.astype(BF16)
            cb = _nt(cgb, bgb)
            bgt = bg.T.astype(BF16)
            for pr in (2 * g, 2 * g + 1):
                sl = slice(CHUNK * pr, CHUNK * (pr + 1))
                st = state[pr]
                st_ref[0, pr] = st
                yp = _nn(cgb, st.astype(BF16)) * q["ea_x"][:, sl]
                for k, hh in enumerate((2 * pr, 2 * pr + 1)):
                    w = (cb * _ssd_decay(q, hh)).astype(BF16)
                    mask = q["left"] if k == 0 else jnp.logical_not(q["left"])
                    yp = yp + _nn(w, jnp.where(mask, xb[:, sl], jnp.zeros_like(xb[:, sl])))
                state[pr] = st * q["eal_x"][:, sl] + _nn(bgt, xd[:, sl])
                pieces.append(yp)
        y = jnp.concatenate(pieces, axis=1) + q["xs"] * dsk_ref[...]
        yc_ref[...] = y
        zv = zdt_ref[:, :SSD_WIDTH]
        yg = y * (zv * _sigmoid(zv))
        r = lax.rsqrt(jnp.mean(yg * yg, axis=-1, keepdims=True) + EPS)
        y_ref[...] = (yg * r * nw_ref[...]).astype(BF16)

    vec = lambda n: pl.BlockSpec((1, n), lambda c: (0, 0))
    return pl.pallas_call(
        body, name=name, grid=(nc,),
        in_specs=[pl.BlockSpec((CHUNK, CONV_DIM), lambda c: (c, 0)),
                  pl.BlockSpec((CHUNK, ZDT), lambda c: (c, OFF_Z // ZDT)),
                  vec(LANES), vec(LANES), vec(SSD_WIDTH), vec(SSD_WIDTH)],
        out_specs=[pl.BlockSpec((CHUNK, SSD_WIDTH), lambda c: (c, 0)), pl.BlockSpec((CHUNK, SSD_WIDTH), lambda c: (c, 0)),
                   pl.BlockSpec((1, N_PAIRS, D_STATE, CHUNK), lambda c: (c, 0, 0, 0))],
        out_shape=[jax.ShapeDtypeStruct((s, D_MODEL), BF16), jax.ShapeDtypeStruct((s, SSD_WIDTH), F32),
                   jax.ShapeDtypeStruct((nc, N_PAIRS, D_STATE, CHUNK), F32)],
        scratch_shapes=[pltpu.VMEM((N_PAIRS, D_STATE, CHUNK), F32)],
        compiler_params=_params(("arbitrary",)),
    )(xc, proj, dtb, alog, dskip_x, nw)


def ssd_backward(proj, xc, ycore, dyall, states, dtb, alog, dskip_x, nw, name):
    s = xc.shape[0]
    nc = s // CHUNK
    c = CHUNK

    def body(xc_ref, zdt_ref, yc_ref, dy_ref, st_ref, dtb_ref, alog_ref, dsk_ref, nw_ref,
             dxc_ref, dzdt_ref, dnw_ref, ddsk_ref, ddtb_ref, dalog_ref, dstate):
        @pl.when(pl.program_id(0) == 0)
        def _():
            dstate[...] = jnp.zeros_like(dstate)
            dnw_ref[...] = jnp.zeros_like(dnw_ref)
            ddsk_ref[...] = jnp.zeros_like(ddsk_ref)
            ddtb_ref[...] = jnp.zeros_like(ddtb_ref)
            dalog_ref[...] = jnp.zeros_like(dalog_ref)

        xcv = xc_ref[...]
        dtraw = zdt_ref[:, SSD_WIDTH:SSD_WIDTH + LANES]
        q = _ssd_common(xcv, dtraw, dtb_ref[...], alog_ref[...])
        xs = q["xs"]
        x = xs * q["dt_x"]
        zv, yc, dy, nwv = zdt_ref[:, :SSD_WIDTH], yc_ref[...], dy_ref[...], nw_ref[...]
        sgz = _sigmoid(zv)
        siluz = zv * sgz
        yg = yc * siluz
        r = lax.rsqrt(jnp.mean(yg * yg, axis=-1, keepdims=True) + EPS)
        dnw_ref[...] += jnp.sum(dy * yg * r, axis=0, keepdims=True)
        g1 = dy * nwv
        dyg = r * (g1 - yg * (r * r) * jnp.mean(g1 * yg, axis=-1, keepdims=True))
        dyv = dyg * siluz
        dz = (dyg * yc * (sgz * (1.0 + zv * (1.0 - sgz)))).astype(BF16)
        ddsk_ref[...] += jnp.sum(dyv * xs, axis=0, keepdims=True)
        dye = dyv * q["ea_x"]
        dx_parts, yoff_parts, u_parts, v_parts, e_parts = [], [], [], [], []
        db_parts, dc_parts = [], []
        for g in range(2):
            bg, cg = _ssd_group(xcv, g)
            bgb, cgb = bg.astype(BF16), cg.astype(BF16)
            cb = _nt(cgb, bgb)
            cgt = cg.T.astype(BF16)
            dgsum = jnp.zeros((c, c), F32)
            dbg = jnp.zeros((c, D_STATE), F32)
            dcg = jnp.zeros((c, D_STATE), F32)
            for pr in (2 * g, 2 * g + 1):
                sl = slice(c * pr, c * (pr + 1))
                st = st_ref[0, pr]
                dst = dstate[pr]
                stb, dstb = st.astype(BF16), dst.astype(BF16)
                xp = x[:, sl]
                xpb = xp.astype(BF16)
                dyp = dyv[:, sl]
                xdp = xp * q["dte_x"][:, sl]
                yoff_parts.append(_nn(cgb, stb) * q["ea_x"][:, sl])
                rr = _nn(bgb, dstb)
                dxp = rr * q["dte_x"][:, sl]
                u_parts.append(rr * xdp)
                v_parts.append(dst * st * q["eal_x"][:, sl])
                for k, hh in enumerate((2 * pr, 2 * pr + 1)):
                    decay = _ssd_decay(q, hh)
                    w = cb * decay
                    mask = q["left"] if k == 0 else jnp.logical_not(q["left"])
                    dym = jnp.where(mask, dyp, 0.0).astype(BF16)
                    dw = _nt(dym, xpb)
                    dgsum = dgsum + dw * decay
                    e_parts.append(dw * w)
                    dxp = dxp + _nn(w.T.astype(BF16), dym)
                dyeb = dye[:, sl].astype(BF16)
                dcg = dcg + _nt(dyeb, stb)
                dbg = dbg + _nt(xdp.astype(BF16), dstb)
                dstate[pr] = dst * q["eal_x"][:, sl] + _nn(cgt, dyeb)
                dx_parts.append(dxp)
            dcg = dcg + _nn(dgsum.astype(BF16), bgb)
            dbg = dbg + _nn(dgsum.T.astype(BF16), cgb)
            db_parts.append(dbg)
            dc_parts.append(dcg)
        dx = jnp.concatenate(dx_parts, axis=1)
        yoff = jnp.concatenate(yoff_parts, axis=1)
        u = jnp.concatenate(u_parts, axis=1)
        v = jnp.concatenate(v_parts, axis=1)
        reduce_heads = (_iota2((SSD_WIDTH, c), 0) // HEAD_DIM == _iota2((SSD_WIDTH, c), 1)).astype(BF16)
        to_head = (_iota2((SSD_HEADS * c, c), 0) // c == _iota2((SSD_HEADS * c, c), 1)).astype(BF16)
        da = _split_dot(dyv * yoff - u, reduce_heads, 2)
        da = da + _split_dot(jnp.concatenate(e_parts, axis=1), to_head, 2)
        da = da - _split_dot(jnp.concatenate(e_parts, axis=0), to_head, 2, dot=_tn)
        dalast = jnp.sum(_split_dot(u + v, reduce_heads, 2), axis=0, keepdims=True)
        da = da + jnp.where(_iota2((c, c), 0) == c - 1, dalast, 0.0)
        utri = (_iota2((c, c), 1) >= _iota2((c, c), 0)).astype(BF16)
        dda = _split_dot_left(utri, da)
        ddt = dda * q["a"] + _split_dot(dx * xs, reduce_heads, 2)
        dalog_ref[...] += jnp.sum(dda * q["dt"], axis=0, keepdims=True) * q["a"]
        ddtraw = jnp.where(_iota2((c, c), 1) < SSD_HEADS, ddt * _sigmoid(dtraw + dtb_ref[...]), 0.0)
        ddtb_ref[...] += jnp.sum(ddtraw, axis=0, keepdims=True)
        dzdt_ref[...] = jnp.concatenate([dz, ddtraw.astype(BF16), jnp.zeros((c, ZDT - SSD_WIDTH - LANES), BF16)], axis=1)
        dxs = dx * q["dt_x"] + dyv * dsk_ref[...]
        dxc_ref[...] = jnp.concatenate([dxs] + db_parts + dc_parts, axis=1)

    rev = lambda i: nc - 1 - i
    vec = lambda n: pl.BlockSpec((1, n), lambda i: (0, 0))
    wide = pl.BlockSpec((c, SSD_WIDTH), lambda i: (rev(i), 0))
    zdt = pl.BlockSpec((c, ZDT), lambda i: (rev(i), OFF_Z // ZDT))
    return pl.pallas_call(
        body, name=name, grid=(nc,),
        in_specs=[pl.BlockSpec((c, CONV_DIM), lambda i: (rev(i), 0)), zdt, wide, wide,
                  pl.BlockSpec((1, N_PAIRS, D_STATE, c), lambda i: (rev(i), 0, 0, 0)),
                  vec(LANES), vec(LANES), vec(SSD_WIDTH), vec(SSD_WIDTH)],
        out_specs=[pl.BlockSpec((c, CONV_DIM), lambda i: (rev(i), 0)), zdt,
                   vec(SSD_WIDTH), vec(SSD_WIDTH), vec(LANES), vec(LANES)],
        out_shape=[jax.ShapeDtypeStruct((s, CONV_DIM), F32), jax.ShapeDtypeStruct((s, D_INP), BF16),
                   jax.ShapeDtypeStruct((1, SSD_WIDTH), F32),
                   jax.ShapeDtypeStruct((1, SSD_WIDTH), F32), jax.ShapeDtypeStruct((1, LANES), F32),
                   jax.ShapeDtypeStruct((1, LANES), F32)],
        scratch_shapes=[pltpu.VMEM((N_PAIRS, D_STATE, c), F32)],
        compiler_params=_params(("arbitrary",)),
    )(xc, proj, ycore, dyall, states, dtb, alog, dskip_x, nw)


SB_Q, SB_K = 256, 512
SB_SCALE = HEAD_DIM ** -0.5


def _sb_weights(qm, kb, diagonal, run_lk, strict_after):
    z = _nt(qm, kb)
    nz = -z
    tail = jnp.log(1.0 + jnp.exp(jnp.minimum(z, nz)))
    ls = jnp.minimum(z, 0.0) - tail
    lk = jnp.minimum(nz, 0.0) - tail
    if diagonal is not None:
        valid = _iota2(z.shape, 1) < _iota2(z.shape, 0) + diagonal
        lk = jnp.where(valid, lk, 0.0)
    w = jnp.exp(ls + _split_dot(lk, strict_after, 2) + run_lk)
    if diagonal is not None:
        w = jnp.where(valid, w, 0.0)
    return ls, lk, w


def _sb_sweep(i, block, init):
    own = (i * SB_Q) // SB_K
    first = block(own, init, i * SB_Q - own * SB_K)
    return lax.fori_loop(1, own + 1, lambda jj, carry: block(own - jj, carry, None), first)


def sb_forward(proj, y_all, name):
    s = proj.shape[0]
    t, tk = SB_Q, SB_K
    nq = s // t

    def body(q_ref, k_ref, v_ref, _, y_ref, o_ref):
        i = pl.program_id(1)
        left = _iota2((t, LANES), 1) < HEAD_DIM
        left_k = _iota2((tk, LANES), 1) < HEAD_DIM
        qv = q_ref[...] * SB_SCALE
        zero = jnp.zeros_like(qv)
        qms = (jnp.where(left, qv, zero).astype(BF16), jnp.where(left, zero, qv).astype(BF16))
        strict_after = (_iota2((tk, tk), 0) > _iota2((tk, tk), 1)).astype(BF16)

        def block(j, carry, diagonal):
            o, runs = carry[0], carry[1:]
            rows = pl.ds(pl.multiple_of(j * tk, tk), tk)
            kb = k_ref[rows, :].astype(BF16)
            vv = v_ref[rows, :]
            new_runs = []
            for k in range(2):
                _, lk, w = _sb_weights(qms[k], kb, diagonal, runs[k], strict_after)
                vm = jnp.where(left_k if k == 0 else jnp.logical_not(left_k), vv, 0.0).astype(BF16)
                o = o + _nn(w.astype(BF16), vm)
                new_runs.append(runs[k] + jnp.sum(lk, axis=1, keepdims=True))
            return (o, *new_runs)

        init = (jnp.zeros((t, LANES), F32), jnp.zeros((t, 1), F32), jnp.zeros((t, 1), F32))
        o = _sb_sweep(i, block, init)[0]
        o_ref[...] = o
        y_ref[...] = o.astype(BF16)

    return pl.pallas_call(
        body, name=name, grid=(2, nq),
        in_specs=[pl.BlockSpec((t, LANES), lambda p, i: (i, 3 * p)),
                  pl.BlockSpec((s, LANES), lambda p, i: (0, 3 * p + 1)),
                  pl.BlockSpec((s, LANES), lambda p, i: (0, 3 * p + 2)), ANY],
        out_specs=[pl.BlockSpec((t, LANES), lambda p, i: (i, SSD_WIDTH // LANES + p)),
                   pl.BlockSpec((t, LANES), lambda p, i: (i, p))],
        out_shape=[jax.ShapeDtypeStruct(y_all.shape, BF16), jax.ShapeDtypeStruct((s, SB_WIDTH), F32)],
        input_output_aliases={3: 0},
        compiler_params=_params(("parallel", "arbitrary")),
    )(proj, proj, proj, y_all)


def sb_backward(proj, o, dyall, dproj, name, after=None):
    s = proj.shape[0]
    t, tk = SB_Q, SB_K
    nq = s // t
    specs, ops = _after(after)

    def body(q_ref, k_ref, v_ref, o_ref, do_ref, *rest):
        dqkv_ref, dk_acc, dv_acc = rest[-3:]
        dk_acc[...] = jnp.zeros_like(dk_acc)
        dv_acc[...] = jnp.zeros_like(dv_acc)
        left = _iota2((t, LANES), 1) < HEAD_DIM
        lane_masks = (left, jnp.logical_not(left))
        left_k = _iota2((tk, LANES), 1) < HEAD_DIM
        key_masks = (left_k, jnp.logical_not(left_k))
        strict_after = (_iota2((tk, tk), 0) > _iota2((tk, tk), 1)).astype(BF16)
        from_here = (_iota2((tk, tk), 0) >= _iota2((tk, tk), 1)).astype(BF16)

        def query_block(i, _):
            qrows = pl.ds(pl.multiple_of(i * t, t), t)
            qv = q_ref[qrows, :] * SB_SCALE
            dov = do_ref[qrows, :]
            zero = jnp.zeros_like(qv)
            qb = qv.astype(BF16)
            dob = dov.astype(BF16)
            prod = dob.astype(F32) * o_ref[qrows, :]
            qms = [jnp.where(m, qv, zero).astype(BF16) for m in lane_masks]
            doms = [jnp.where(m, dov, zero).astype(BF16) for m in lane_masks]
            deltas = [jnp.sum(jnp.where(m, prod, zero), axis=1, keepdims=True) for m in lane_masks]

            def block(j, carry, diagonal):
                dq = carry[0]
                run_lk, run_e = carry[1:3], carry[3:5]
                rows = pl.ds(pl.multiple_of(j * tk, tk), tk)
                kb = k_ref[rows, :].astype(BF16)
                vb = v_ref[rows, :].astype(BF16)
                dkj = jnp.zeros((tk, LANES), F32)
                dvj = jnp.zeros((tk, LANES), F32)
                new_lk, new_e = [], []
                for k in range(2):
                    ls, lk, w = _sb_weights(qms[k], kb, diagonal, run_lk[k], strict_after)
                    wb = w.astype(BF16)
                    e = _nt(doms[k], vb) * wb.astype(F32)
                    before = deltas[k] - _split_dot(e, from_here, 2) - run_e[k]
                    dz = e - jnp.exp(ls) * (e + before)
                    if diagonal is not None:
                        dz = jnp.where(_iota2(dz.shape, 1) < _iota2(dz.shape, 0) + diagonal, dz, 0.0)
                    dz = dz.astype(BF16)
                    m = lane_masks[k]
                    dvj = dvj + jnp.where(key_masks[k], _tn(wb, dob), 0.0)
                    dkj = dkj + jnp.where(key_masks[k], _tn(dz, qb), 0.0)
                    dq = dq + jnp.where(m, _nn(dz, kb), 0.0)
                    new_lk.append(run_lk[k] + jnp.sum(lk, axis=1, keepdims=True))
                    new_e.append(run_e[k] + jnp.sum(e, axis=1, keepdims=True))
                dk_acc[rows, :] += dkj
                dv_acc[rows, :] += dvj
                return (dq, *new_lk, *new_e)

            col = jnp.zeros((t, 1), F32)
            dq = _sb_sweep(i, block, (jnp.zeros((t, LANES), F32), col, col, col, col))[0]
            dqkv_ref[qrows, 0:LANES] = (dq * SB_SCALE).astype(BF16)
            return 0

        lax.fori_loop(0, nq, query_block, 0)
        dqkv_ref[:, LANES:2 * LANES] = dk_acc[...].astype(BF16)
        dqkv_ref[:, 2 * LANES:3 * LANES] = dv_acc[...].astype(BF16)

    col = lambda f: pl.BlockSpec((s, LANES), f)
    return pl.pallas_call(
        body, name=name, grid=(2,),
        in_specs=[col(lambda p: (0, 3 * p)), col(lambda p: (0, 3 * p + 1)), col(lambda p: (0, 3 * p + 2)),
                  col(lambda p: (0, p)), col(lambda p: (0, SSD_WIDTH // LANES + p)), ANY] + specs,
        out_specs=pl.BlockSpec((s, 3 * LANES), lambda p: (0, p)),
        out_shape=jax.ShapeDtypeStruct(dproj.shape, BF16),
        input_output_aliases={5: 0},
        scratch_shapes=[pltpu.VMEM((s, LANES), F32), pltpu.VMEM((s, LANES), F32)],
        compiler_params=_params(("parallel",)),
    )(proj, proj, proj, o, dyall, dproj, *ops)


def adamw(w, g, m, v, name):
    b, r, c = w.shape
    tr = max([t for t in range(8, min(r, 512) + 1, 8) if r % t == 0], default=r)

    def body(w_ref, g_ref, m_ref, v_ref, d_ref, nm_ref, nv_ref):
        gv = g_ref[...]
        nm = ADAM_B1 * m_ref[...] + (1.0 - ADAM_B1) * gv
        nv = ADAM_B2 * v_ref[...] + (1.0 - ADAM_B2) * (gv * gv)
        m_hat = nm / (1.0 - ADAM_B1 ** ADAM_STEP)
        v_hat = nv / (1.0 - ADAM_B2 ** ADAM_STEP)
        d_ref[...] = -ADAM_LR * (m_hat / (jnp.sqrt(v_hat) + ADAM_EPS) + ADAM_WD * w_ref[...])
        nm_ref[...] = nm
        nv_ref[...] = nv

    blk = pl.BlockSpec((1, tr, c), lambda i, j: (i, j, 0))
    return pl.pallas_call(
        body, name=name, grid=(b, r // tr),
        in_specs=[blk] * 4, out_specs=[blk] * 3,
        out_shape=[jax.ShapeDtypeStruct(w.shape, F32)] * 3,
        compiler_params=_params(("parallel", "parallel")),
    )(w, g, m, v)


def _position():
    return lax.axis_index("x"), lax.axis_index("y"), lax.axis_index("c")


def _flipped(pos, flip):
    return tuple((1 - p) if f else p for p, f in zip(pos, flip))


FLIP_C = (0, 0, 1)
CHIP_FLIPS = {1: (0, 1, 0), 2: (1, 0, 0), 3: (1, 1, 0)}
SHARD_ROWS = (SHARD_IN, SHARD_OUT, SHARD_FF, SHARD_FF, SHARD_FF)


def _rows(start, size):
    return pl.ds(pl.multiple_of(start, 16), size)


HBM = pl.BlockSpec(memory_space=pltpu.HBM)
SEM = pl.BlockSpec(memory_space=pltpu.SEMAPHORE)
EFFECT = pltpu.SideEffectType.DATAFLOW_SIDE_EFFECTING


def _in_hbm(a):
    return pltpu.with_memory_space_constraint(a, pltpu.HBM)


def _landing(shape, dtype):
    return _in_hbm(lax.empty(shape, dtype))


def _copies(plan, pos, src_refs, land_refs, send_sems, recv_sems):
    return [pltpu.make_async_remote_copy(src_ref=src, dst_ref=dst, send_sem=send_sems.at[k], recv_sem=recv_sems.at[k],
                                         device_id=_flipped(pos, flip), device_id_type=MESH)
            for k, (src, dst, flip) in enumerate(plan(pos, src_refs, land_refs))]


def exchange_start(name, srcs, lands, n, plan, after=None):
    ns, nl = len(srcs), len(lands)
    specs, ops = _after(after)

    def body(*refs):
        src_refs, land_refs = refs[:ns], refs[ns:ns + nl]
        send_sems, recv_sems, token = refs[ns + nl + len(ops)], refs[ns + nl + len(ops) + 1], refs[-1]
        for cp in _copies(plan, _position(), src_refs, land_refs, send_sems, recv_sems):
            cp.start()
        token[...] = jnp.zeros_like(token)

    thru = [pltpu.HBM(a.shape, a.dtype) for a in list(srcs) + list(lands)]
    out = pl.pallas_call(
        body, name=name,
        out_shape=(pltpu.SemaphoreType.DMA((n,)), pltpu.SemaphoreType.DMA((n,)), *thru, jax.ShapeDtypeStruct((8, LANES), F32)),
        in_specs=[HBM] * (ns + nl) + specs,
        out_specs=(SEM, SEM, *([HBM] * (ns + nl)), pl.BlockSpec(memory_space=pltpu.VMEM)),
        input_output_aliases={k: 2 + k for k in range(ns + nl)},
        compiler_params=pltpu.CompilerParams(has_side_effects=EFFECT),
    )(*[_in_hbm(a) for a in srcs], *lands, *ops)
    return out[0], out[1], list(out[2:2 + ns]), list(out[2 + ns:2 + ns + nl]), out[-1]


def exchange_wait(name, started, after, plan):
    send_sems, recv_sems, srcs, lands, _ = started
    ns, nl = len(srcs), len(lands)
    specs, ops = _after(after)

    def body(*refs):
        src_refs, land_refs = refs[:ns], refs[ns:ns + nl]
        send_sems, recv_sems = refs[ns + nl], refs[ns + nl + 1]
        for cp in _copies(plan, _position(), src_refs, land_refs, send_sems, recv_sems):
            cp.wait_send()
            cp.wait_recv()

    out = pl.pallas_call(
        body, name=name,
        out_shape=tuple(pltpu.HBM(a.shape, a.dtype) for a in list(srcs) + list(lands)),
        in_specs=[HBM] * (ns + nl) + [SEM, SEM] + specs,
        out_specs=tuple([HBM] * (ns + nl)),
        input_output_aliases={k: k for k in range(ns + nl)},
        compiler_params=pltpu.CompilerParams(has_side_effects=EFFECT),
    )(*srcs, *lands, send_sems, recv_sems, *ops)
    return list(out[:ns]), list(out[ns:])


def _gather_ici_plan(pos, srcs, lands):
    chip, c = 2 * pos[0] + pos[1], pos[2]
    copies = []
    for src, dst in zip(srcs, lands):
        r = src.shape[0]
        h = r // 2
        for f in (1, 2, 3):
            copies.append((src.at[_rows(c * h, h)], dst.at[_rows(chip * r + c * h, h)], CHIP_FLIPS[f]))
    return copies


def _gather_d2d_plan(pos, srcs, lands):
    chip, c = 2 * pos[0] + pos[1], pos[2]
    copies = []
    for own, dst in zip(srcs, lands):
        r = own.shape[0]
        h = r // 2
        copies.append((own, dst.at[_rows(chip * r, r)], FLIP_C))
        for f in (1, 2, 3):
            at = _rows(lax.bitwise_xor(chip, f) * r + c * h, h)
            copies.append((dst.at[at], dst.at[at], FLIP_C))
    return copies


def gather_ici_start(shards, after=None):
    lands = [_landing((N_CHIPS * a.shape[0], D_MODEL), BF16) for a in shards]
    return exchange_start("gather_ici_start", shards, lands, 3 * len(shards), _gather_ici_plan, after=after)


def gather_d2d_start(shards, fulls, after=None):
    return exchange_start("gather_d2d_start", shards, fulls, 4 * len(shards), _gather_d2d_plan, after=after)


def _reduce_d2d_plan(pos, srcs, lands):
    c = pos[2]
    return [(src.at[:, _rows((1 - c) * (src.shape[1] // 2), src.shape[1] // 2)], dst, FLIP_C) for src, dst in zip(srcs, lands)]


def _reduce_ici_plan(pos, srcs, lands):
    chip = 2 * pos[0] + pos[1]
    return [(src.at[lax.bitwise_xor(chip, f)], dst.at[f - 1], CHIP_FLIPS[f]) for src, dst in zip(srcs, lands) for f in (1, 2, 3)]


def _reduce_swap_plan(pos, srcs, lands):
    c = pos[2]
    copies = []
    for dst in lands:
        h = dst.shape[0] // 2
        at = _rows(c * h, h)
        copies.append((dst.at[at], dst.at[at], FLIP_C))
    return copies


def reduce_d2d_start(grads):
    lands = [_landing((N_CHIPS, g.shape[1] // 2, D_MODEL), BF16) for g in grads]
    return exchange_start("reduce_d2d_start", grads, lands, len(grads), _reduce_d2d_plan)


def reduce_ici_start(chip_sums):
    lands = [_landing((N_CHIPS - 1,) + p.shape[1:], BF16) for p in chip_sums]
    return exchange_start("reduce_ici_start", chip_sums, lands, 3 * len(chip_sums), _reduce_ici_plan)


def reduce_swap_start(mine):
    return exchange_start("reduce_swap_start", [], mine, len(mine), _reduce_swap_plan)


def _by_shape(fn, *lists):
    groups, out = {}, [None] * len(lists[0])
    for k, a in enumerate(lists[0]):
        groups.setdefault(a.shape, []).append(k)
    for idx in groups.values():
        for k, r in zip(idx, fn(*[[l[k] for k in idx] for l in lists])):
            out[k] = r
    return out


def add_halves(ds, recvs, half, name):
    n = len(ds)
    nch, r, c = ds[0].shape
    h = r // 2

    def body(half_ref, *refs):
        for k in range(n):
            refs[2 * n + k][...] = (refs[k][...].astype(F32) + refs[n + k][...].astype(F32)).astype(BF16)

    mine = pl.BlockSpec((1, h, c), lambda j, hf: (j, hf[0], 0))
    whole = pl.BlockSpec((1, h, c), lambda j, hf: (j, 0, 0))
    return pl.pallas_call(
        body, name=name,
        grid_spec=pltpu.PrefetchScalarGridSpec(
            num_scalar_prefetch=1, grid=(nch,), in_specs=[mine] * n + [whole] * n, out_specs=[whole] * n),
        out_shape=[jax.ShapeDtypeStruct(rv.shape, BF16) for rv in recvs],
        compiler_params=_params(("parallel",)),
    )(half, *ds, *recvs)


def add_chips(ps, recvs, chip, name):
    n = len(ps)
    _, r, c = ps[0].shape

    def body(chip_ref, *refs):
        for k in range(n):
            acc = refs[k][0].astype(F32)
            for f in range(N_CHIPS - 1):
                acc = acc + refs[n + k][f].astype(F32)
            refs[2 * n + k][...] = acc

    return pl.pallas_call(
        body, name=name,
        grid_spec=pltpu.PrefetchScalarGridSpec(
            num_scalar_prefetch=1, grid=(1,),
            in_specs=[pl.BlockSpec((1, r, c), lambda i, ch: (ch[0], 0, 0))] * n +
                     [pl.BlockSpec((N_CHIPS - 1, r, c), lambda i, ch: (0, 0, 0))] * n,
            out_specs=[pl.BlockSpec((r, c), lambda i, ch: (ch[1], 0))] * n),
        out_shape=[jax.ShapeDtypeStruct((2 * r, c), F32)] * n,
        compiler_params=_params(("arbitrary",)),
    )(chip, *ps, *recvs)


def adamw_layers(w, gs, m, v, name):
    b, r, c = w.shape
    tr = max([t for t in range(8, min(r, 512) + 1, 8) if r % t == 0], default=r)

    def body(w_ref, m_ref, v_ref, *rest):
        g_refs, (g_ref, d_ref, nm_ref, nv_ref) = rest[:b], rest[b:]
        layer = pl.program_id(0)
        gv = g_refs[0][...]
        for l in range(1, b):
            gv = jnp.where(layer == l, g_refs[l][...], gv)
        nm = ADAM_B1 * m_ref[0] + (1.0 - ADAM_B1) * gv
        nv = ADAM_B2 * v_ref[0] + (1.0 - ADAM_B2) * (gv * gv)
        m_hat = nm / (1.0 - ADAM_B1 ** ADAM_STEP)
        v_hat = nv / (1.0 - ADAM_B2 ** ADAM_STEP)
        g_ref[0] = gv
        d_ref[0] = -ADAM_LR * (m_hat / (jnp.sqrt(v_hat) + ADAM_EPS) + ADAM_WD * w_ref[0])
        nm_ref[0] = nm
        nv_ref[0] = nv

    nr, tc = r // tr, (c if tr < r else _tile(c, 256))
    steps = nr * (c // tc)
    blk = pl.BlockSpec((1, tr, tc), lambda i, j: (i, j % nr, j // nr))
    g_specs = [pl.BlockSpec((tr, tc), lambda i, j, l=l: (jnp.where(i == l, j % nr, jnp.where(i < l, 0, nr - 1)),
                                                         jnp.where(i == l, j // nr, jnp.where(i < l, 0, c // tc - 1))))
               for l in range(b)]
    return pl.pallas_call(
        body, name=name, grid=(b, steps),
        in_specs=[blk] * 3 + g_specs, out_specs=[blk] * 4,
        out_shape=[jax.ShapeDtypeStruct(w.shape, F32)] * 4,
        compiler_params=_params(("arbitrary", "arbitrary")),
    )(w, m, v, *gs)


def small_allreduce(v, name, after=None):
    r, c = v.shape
    specs, ops = _after(after)

    def body(v_ref, *rest):
        o_ref, buf, send_sems, recv_sems = rest[len(ops):]
        pos = _position()
        me = 4 * pos[0] + 2 * pos[1] + pos[2]
        buf[0] = v_ref[...]
        copies = []
        for f in range(1, 8):
            flip = ((f >> 2) & 1, (f >> 1) & 1, f & 1)
            cp = pltpu.make_async_remote_copy(
                src_ref=v_ref, dst_ref=buf.at[f], send_sem=send_sems.at[f - 1], recv_sem=recv_sems.at[f - 1],
                device_id=_flipped(pos, flip), device_id_type=MESH)
            cp.start()
            copies.append(cp)
        for cp in copies:
            cp.wait()
        acc = buf[me]
        for d in range(1, 8):
            acc = acc + buf[lax.bitwise_xor(me, d)]
        o_ref[...] = acc

    return pl.pallas_call(
        body, name=name,
        in_specs=[pl.BlockSpec(memory_space=pltpu.VMEM)] + specs, out_specs=pl.BlockSpec(memory_space=pltpu.VMEM),
        out_shape=jax.ShapeDtypeStruct((r, c), F32),
        scratch_shapes=[pltpu.VMEM((8, r, c), F32), pltpu.SemaphoreType.DMA((7,)), pltpu.SemaphoreType.DMA((7,))],
    )(v, *ops)


_IN_SEGMENTS = ((0, 1544, 128), (128, 1800, 128), (256, 2056, 128), (384, 1672, 128), (512, 1928, 128), (640, 2184, 128),
                (OFF_Z, 0, SSD_WIDTH), (OFF_DT, 1536, SSD_HEADS), (OFF_XBC, 512, CONV_DIM), (OFF_P, 2312, POOL_WIDTH))


def _in_column_map():
    m = np.full((D_INP,), -1, np.int64)
    for at, orig, n in _IN_SEGMENTS:
        cols = np.arange(orig, orig + n)
        m[at:at + n] = (cols // COLS_IN) * SHARD_IN + cols % COLS_IN
    return m


def take_rows(a, idx, name):
    dep, r_in, c = a.shape
    blk = LANES
    n_out, n_in = len(idx) // blk, r_in // blk
    assert len(idx) % blk == 0 and r_in % blk == 0
    sources = [sorted({int(v) // blk for v in idx[blk * i:blk * (i + 1)] if v >= 0}) for i in range(n_out)]
    width = max(len(s) for s in sources)
    table = np.zeros((n_out, width), np.int32)
    for i, s in enumerate(sources):
        spare = [b for b in range(n_in) if b not in s][:width - len(s)]
        table[i] = s + spare

    def body(tbl_ref, idx_ref, *refs):
        in_refs, o_ref = refs[:width], refs[width]
        i = pl.program_id(1)
        src = idx_ref[...]
        acc = jnp.zeros((blk, c), F32)
        for k in range(width):
            pick = (src == tbl_ref[i, k] * blk + _iota2((blk, blk), 1)).astype(BF16)
            acc = acc + _nn(pick, in_refs[k][0])
        o_ref[0] = acc.astype(BF16)

    return pl.pallas_call(
        body, name=name,
        grid_spec=pltpu.PrefetchScalarGridSpec(
            num_scalar_prefetch=1, grid=(dep, n_out),
            in_specs=[pl.BlockSpec((blk, 1), lambda l, i, t: (i, 0))] +
                     [pl.BlockSpec((1, blk, c), lambda l, i, t, k=k: (l, t[i, k], 0)) for k in range(width)],
            out_specs=pl.BlockSpec((1, blk, c), lambda l, i, t: (l, i, 0))),
        out_shape=jax.ShapeDtypeStruct((dep, len(idx), c), BF16),
        compiler_params=_params(("parallel", "parallel")),
    )(jnp.asarray(table), jnp.asarray(np.asarray(idx, np.int32).reshape(-1, 1)), *([a] * width))


def _in_weight_layout(staged):
    return take_rows(staged, _in_column_map(), "w_in_layout")


def _in_gradient_layout(dwt):
    fwd = _in_column_map()
    inv = np.full((N_CHIPS * SHARD_IN,), -1, np.int64)
    inv[fwd[fwd >= 0]] = np.nonzero(fwd >= 0)[0]
    return take_rows(dwt, inv, "dw_in_layout")


SMALL_NAMES = ("norm1_w", "conv_w", "conv_b", "dt_bias", "a_log", "d_skip", "ssd_norm_w", "pool_w", "pool_b",
               "pool_scale", "norm2_w", "final_norm_w")
SMALL_ROWS = 104


def _pack_small(parts):
    flat = jnp.concatenate([p.reshape(-1) for p in parts])
    return jnp.pad(flat, (0, SMALL_ROWS * D_MODEL - flat.shape[0])).reshape(SMALL_ROWS, D_MODEL)


def _unpack_small(flat, shapes):
    flat = flat.reshape(-1)
    out, at = [], 0
    for shp in shapes:
        n = int(np.prod(shp))
        out.append(flat[at:at + n].reshape(shp))
        at += n
    return out


def kernel(x, norm1_w, w_in, conv_w, conv_b, dt_bias, a_log, d_skip, ssd_norm_w, pool_w, pool_b, pool_scale, w_out, norm2_w, w_gate, w_up, w_down, final_norm_w, loss_target, m_norm1_w, m_w_in, m_conv_w, m_conv_b, m_dt_bias, m_a_log, m_d_skip, m_ssd_norm_w, m_pool_w, m_pool_b, m_pool_scale, m_w_out, m_norm2_w, m_w_gate, m_w_up, m_w_down, m_final_norm_w, v_norm1_w, v_w_in, v_conv_w, v_conv_b, v_dt_bias, v_a_log, v_d_skip, v_ssd_norm_w, v_pool_w, v_pool_b, v_pool_scale, v_w_out, v_norm2_w, v_w_gate, v_w_up, v_w_down, v_final_norm_w):
    px, py, pc = _position()
    chip = 2 * px + py
    chip_arr = jnp.reshape(chip, (1,)).astype(jnp.int32)
    half_arr = jnp.reshape(pc, (1,)).astype(jnp.int32)

    def layer_shards(l):
        w_in_t = jnp.pad(jnp.swapaxes(w_in[l], 0, 1).astype(BF16), ((0, SHARD_IN - COLS_IN), (0, 0)))
        return [w_in_t, w_out[l].astype(BF16), jnp.swapaxes(w_gate[l], 0, 1).astype(BF16),
                jnp.swapaxes(w_up[l], 0, 1).astype(BF16), w_down[l].astype(BF16)]

    shards0 = layer_shards(0)
    head = gather_ici_start(shards0[:1])
    over_ici = {}

    def pass_on(l, after):
        own, arrived = exchange_wait("gather_ici_wait", over_ici[l], after, _gather_ici_plan)
        swap = gather_d2d_start(own, arrived)
        tokens = [swap[4]]
        if l + 1 < DEPTH:
            over_ici[l + 1] = gather_ici_start(layer_shards(l + 1), after=swap[4])
            tokens.append(over_ici[l + 1][4])
        return swap, tokens

    def weights_of(swap, after):
        _, (w_in_st, w_out_l, w_gate_t, w_up_t, w_down_l) = exchange_wait("gather_d2d_wait", swap, after, _gather_d2d_plan)
        return _in_weight_layout(w_in_st[None])[0], w_out_l, w_gate_t, w_up_t, w_down_l

    pad_heads = lambda v: jnp.pad(v, ((0, 0), (0, LANES - SSD_HEADS)))[:, None, :]
    dtb, alog = pad_heads(dt_bias), pad_heads(a_log)
    dskip_x = jnp.repeat(d_skip, HEAD_DIM, axis=1)[:, None, :]
    eye = jnp.eye(len(POOL_WINDOWS), dtype=F32)
    wbd = (pool_w[:, :, :, None, :] * eye[None, :, None, :, None]).reshape(DEPTH, POOL_WIDTH, POOL_WIDTH).astype(BF16)
    pool_b2 = pool_b.reshape(DEPTH, 1, POOL_WIDTH)
    cw_cols = lax.dynamic_update_slice(jnp.zeros((DEPTH, CONV_WIDTH, CONV_DIM), F32), conv_w,
                                       (0, 0, chip * (CONV_DIM // N_CHIPS)))
    cw_cols = jnp.where(pc == 0, cw_cols, 0.0)
    cw_rows = (DEPTH * CONV_WIDTH * CONV_DIM) // D_MODEL
    conv_w_f = small_allreduce(jnp.pad(cw_cols.reshape(cw_rows, D_MODEL), ((0, 8), (0, 0))), "gather_conv_w")
    conv_w_f = conv_w_f[:cw_rows].reshape(DEPTH, CONV_WIDTH, CONV_DIM)
    cw8 = jnp.pad(conv_w_f, ((0, 0), (0, 8 - CONV_WIDTH), (0, 0)))

    h = x[0]
    saved, weights = [], []
    own, arrived = exchange_wait("gather_ici_wait", head, head[4], _gather_ici_plan)
    head = gather_d2d_start(own, arrived)
    over_ici[0] = gather_ici_start(shards0[1:], after=head[4])
    w_in_f = _in_weight_layout(exchange_wait("gather_d2d_wait", head, [head[4], over_ici[0][4]], _gather_d2d_plan)[1][0][None])[0]
    for l in range(DEPTH):
        if l > 0:
            w_in_f, w_out_f, w_gate_t, w_up_t, w_down_f = weights[l]
        proj = rms_matmul(h, norm1_w[l][None], w_in_f, "in_proj")
        xc = conv_forward(proj, cw8[l], conv_b[l][None], "conv_fwd")
        y_all, ycore, states = ssd_forward(proj, xc, dtb[l], alog[l], dskip_x[l], ssd_norm_w[l][None], "ssd_fwd")
        y_all, o_sb = sb_forward(proj, y_all, "sb_fwd")
        swap, tokens = pass_on(l + 1 if l else 0, o_sb) if l + 1 < DEPTH else (None, None)
        y_all = pool_forward(proj, wbd[l], pool_b2[l], pool_scale[l][None], y_all, "pool_fwd", after=tokens)
        if l == 0:
            w_out_f, w_gate_t, w_up_t, w_down_f = exchange_wait("gather_d2d_wait", swap, y_all, _gather_d2d_plan)[1]
            weights.append((w_in_f, w_out_f, w_gate_t, w_up_t, w_down_f))
        x1 = matmul_residual(y_all, w_out_f, h, "out_proj")
        x2, g, u = ffn_forward(x1, norm2_w[l][None], w_gate_t, w_up_t, w_down_f, "ffn_fwd")
        if l == 0:
            swap, tokens = pass_on(1, x2)
            weights.append(weights_of(swap, tokens))
        elif swap is not None:
            weights.append(weights_of(swap, x2))
        saved.append((h, proj, xc, ycore, states, o_sb, y_all, x1, g, u))
        h = x2

    loss_part, dx, dxb, d_final = loss_head(h, final_norm_w[None], loss_target[0], "loss_head")
    loss = lax.psum(loss_part[0, 0], ("x", "y", "c"))

    small = {n: [None] * DEPTH for n in SMALL_NAMES if n != "final_norm_w"}
    chip_half = jnp.concatenate([chip_arr, half_arr])
    reduced = {}
    d2d = ici = early = None

    def add_cores(d2d, after):
        mine, theirs = exchange_wait("reduce_d2d_wait", d2d[1], after, _reduce_d2d_plan)
        return d2d[0], reduce_ici_start(_by_shape(lambda ds, ts: add_halves(ds, ts, half_arr, "reduce_add_halves"), mine, theirs))

    def add_all(ici, after):
        sums, theirs = exchange_wait("reduce_ici_wait", ici[1], after, _reduce_ici_plan)
        return ici[0], reduce_swap_start(_by_shape(lambda ps, ts: add_chips(ps, ts, chip_half, "reduce_add_chips"), sums, theirs))

    def finish(swap, after):
        reduced[swap[0]] = exchange_wait("reduce_swap_wait", swap[1], after, _reduce_swap_plan)[1]

    swaps = []
    for l in reversed(range(DEPTH)):
        xin, proj, xc, ycore, states, o_sb, y_all, x1, g, u = saved[l]
        w_in_f, w_out_f, w_gate_t, w_up_t, w_down_f = weights[l]
        dg, du, act = ffn_backward_act(dxb, g, u, w_down_f, "ffn_bwd_act", after=None if d2d is None else d2d[1][4])
        dx1, dx1b, h2, dn2 = rms_backward([dg, du], [w_gate_t, w_up_t], x1, norm2_w[l][None], dx, "ffn_bwd_norm", 256)
        if d2d is not None:
            ici = add_cores(d2d, dx1b)
        dyall = matmul_nt(dx1b, w_out_f, "out_proj_bwd", after=None if ici is None else ici[1][4])
        dw_down = matmul_tn(act, dxb, "dw_down")
        dw_gate = matmul_tn(dg, h2, "dw_gate")
        dw_up = matmul_tn(du, h2, "dw_up")
        dw_out = matmul_tn(y_all, dx1b, "dw_out")
        late = [dw.reshape(N_CHIPS, r, D_MODEL) for dw, r in zip((dw_out, dw_gate, dw_up, dw_down), SHARD_ROWS[1:])]
        if l == 0:
            early = ("0 late", reduce_d2d_start(late))
        dxc, dproj, dsn, ddsk, ddtb, dalog = ssd_backward(proj, xc, ycore, dyall, states, dtb[l], alog[l],
                                                          dskip_x[l], ssd_norm_w[l][None], "ssd_bwd")
        dproj, dcw, dcb = conv_backward(proj, dxc, cw8[l], conv_b[l][None], dproj, "conv_bwd",
                                        after=None if early is None else early[1][4])
        if early is not None:
            early = add_cores(early, dproj)
        dproj = sb_backward(proj, o_sb, dyall, dproj, "sb_bwd", after=None if early is None else early[1][4])
        dproj, dwbd, dpb, dps = pool_backward(proj, dyall, wbd[l], pool_b2[l], pool_scale[l][None], dproj, "pool_bwd")
        if ici is not None:
            swaps.append(add_all(ici, dproj))
            ici = None
        dx, dxb, h1, dn1 = rms_backward([dproj], [w_in_f], xin, norm1_w[l][None], dx1, "in_proj_bwd", 256,
                                        after=swaps[-1][1][4] if swaps else None)
        dw_in = _in_gradient_layout(matmul_tn(dproj, h1, "dw_in")[None])[0].reshape(N_CHIPS, SHARD_IN, D_MODEL)
        d2d = (l, reduce_d2d_start([dw_in] if l == 0 else [dw_in] + late))
        small["norm1_w"][l] = dn1[0]
        small["conv_w"][l] = dcw[:CONV_WIDTH]
        small["conv_b"][l] = dcb[0]
        small["dt_bias"][l] = ddtb[0, :SSD_HEADS]
        small["a_log"][l] = dalog[0, :SSD_HEADS]
        small["d_skip"][l] = ddsk.reshape(SSD_HEADS, HEAD_DIM).sum(axis=1)
        small["ssd_norm_w"][l] = dsn[0]
        small["pool_w"][l] = jnp.stack([dwbd[64 * k:64 * k + 64, 64 * k:64 * k + 64] for k in range(len(POOL_WINDOWS))])
        small["pool_b"][l] = dpb.reshape(len(POOL_WINDOWS), -1)
        small["pool_scale"][l] = dps[0]
        small["norm2_w"][l] = dn2[0]
    grad_x = dx[None]

    ici = add_cores(d2d, d2d[1][4])
    small_parts = [d_final if n == "final_norm_w" else jnp.stack(small[n]) for n in SMALL_NAMES]
    small_sum = small_allreduce(_pack_small(small_parts), "reduce_small", after=ici[1][4])
    swaps.append(add_all(early, small_sum))
    swaps.append(add_all(ici, swaps[-1][1][4]))
    for swap in swaps:
        finish(swap, swaps[-1][1][4])
    reduced[0] = reduced[0] + reduced["0 late"]
    g_big = {n: [reduced[l][k] for l in range(DEPTH)] for k, n in enumerate(("w_in", "w_out", "w_gate", "w_up", "w_down"))}
    g_big["w_in"] = [gl[:COLS_IN] for gl in g_big["w_in"]]
    transposed = ("w_in", "w_gate", "w_up")

    g_small = dict(zip(SMALL_NAMES, _unpack_small(small_sum, [p.shape for p in small_parts])))
    g_small["final_norm_w"] = g_small["final_norm_w"].reshape(final_norm_w.shape)
    g_small["conv_w"] = lax.dynamic_slice_in_dim(g_small["conv_w"], chip * (CONV_DIM // N_CHIPS), CONV_DIM // N_CHIPS, axis=2)

    given = dict(norm1_w=(norm1_w, m_norm1_w, v_norm1_w), w_in=(w_in, m_w_in, v_w_in), conv_w=(conv_w, m_conv_w, v_conv_w),
                 conv_b=(conv_b, m_conv_b, v_conv_b), dt_bias=(dt_bias, m_dt_bias, v_dt_bias), a_log=(a_log, m_a_log, v_a_log),
                 d_skip=(d_skip, m_d_skip, v_d_skip), ssd_norm_w=(ssd_norm_w, m_ssd_norm_w, v_ssd_norm_w),
                 pool_w=(pool_w, m_pool_w, v_pool_w), pool_b=(pool_b, m_pool_b, v_pool_b),
                 pool_scale=(pool_scale, m_pool_scale, v_pool_scale), w_out=(w_out, m_w_out, v_w_out),
                 norm2_w=(norm2_w, m_norm2_w, v_norm2_w), w_gate=(w_gate, m_w_gate, v_w_gate), w_up=(w_up, m_w_up, v_w_up),
                 w_down=(w_down, m_w_down, v_w_down), final_norm_w=(final_norm_w, m_final_norm_w, v_final_norm_w))
    order = ("norm1_w", "w_in", "conv_w", "conv_b", "dt_bias", "a_log", "d_skip", "ssd_norm_w", "pool_w", "pool_b",
             "pool_scale", "w_out", "norm2_w", "w_gate", "w_up", "w_down", "final_norm_w")
    grads = dict(g_small)
    results = {}
    for n in ("w_in", "w_out", "w_gate", "w_up", "w_down"):
        w, m, v = given[n]
        if n in transposed:
            out = adamw_layers(jnp.swapaxes(w, 1, 2), g_big[n], jnp.swapaxes(m, 1, 2), jnp.swapaxes(v, 1, 2), "adamw_" + n)
            out = [jnp.swapaxes(o, 1, 2) for o in out]
        else:
            out = adamw_layers(w, g_big[n], m, v, "adamw_" + n)
        grads[n], results[n] = out[0], tuple(out[1:])
    small_shapes = [given[n][0].shape for n in SMALL_NAMES]
    packed = [_pack_small([given[n][k] for n in SMALL_NAMES])[None] for k in range(3)]
    packed_g = _pack_small([grads[n] for n in SMALL_NAMES])[None]
    small_out = adamw(packed[0], packed_g, packed[1], packed[2], "adamw_small")
    small_out = [_unpack_small(o[0], small_shapes) for o in small_out]
    for i, n in enumerate(SMALL_NAMES):
        results[n] = tuple(small_out[k][i] for k in range(3))

    return (loss, grad_x, *[grads[n] for n in order], *[results[n][0] for n in order],
            *[results[n][1] for n in order], *[results[n][2] for n in order])
```

```python
import numpy as np
import jax
import jax.numpy as jnp
from jax import lax
from jax.experimental import pallas as pl
from jax.experimental.pallas import tpu as pltpu

F32 = jnp.float32
BF16 = jnp.bfloat16
MESH = pl.DeviceIdType.MESH
ANY = pl.BlockSpec(memory_space=pl.ANY)

D_MODEL = 1024
DEPTH = 4
EPS = 1e-6
SSD_WIDTH = 512
SSD_HEADS = 8
HEAD_DIM = 64
D_STATE = 128
CHUNK = 128
CONV_WIDTH = 4
CONV_DIM = 1024
SB_WIDTH = 256
POOL_WIDTH = 256
POOL_WINDOWS = (2, 4, 8, 16)
D_FF = 2816
D_IN = 2568
N_CHIPS = 4
OFF_QKV, OFF_Z, OFF_DT, OFF_XBC, OFF_P = 0, 768, 1280, 1536, 2560
D_INP = 2816
ZDT = 768
SHARD_IN, SHARD_OUT, SHARD_FF = 672, 256, 704
COLS_IN = 642
ADAM_LR, ADAM_B1, ADAM_B2, ADAM_EPS, ADAM_WD, ADAM_STEP = 0.001, 0.9, 0.999, 1e-08, 0.01, 10
LANES = 128
VMEM_LIMIT = 56 * 1024 * 1024


def _params(sem=None):
    return pltpu.CompilerParams(dimension_semantics=sem, vmem_limit_bytes=VMEM_LIMIT)


def _tile(n, cap):
    best = None
    for t in range(LANES, min(n, cap) + 1, LANES):
        if n % t == 0:
            best = t
    assert best is not None, (n, cap)
    return best


def _nt(a, b):
    return lax.dot_general(a, b, (((1,), (1,)), ((), ())), preferred_element_type=F32)


def _tn(a, b):
    return lax.dot_general(a, b, (((0,), (0,)), ((), ())), preferred_element_type=F32)


def _nn(a, b):
    return jnp.dot(a, b, preferred_element_type=F32)


def _split_dot(a, b_exact, terms=3, dot=_nn):
    acc = None
    rest = a
    for _ in range(terms):
        hi = rest.astype(BF16)
        part = dot(hi, b_exact)
        acc = part if acc is None else acc + part
        rest = rest - hi.astype(F32)
    return acc


def _split_dot_left(a_exact, b, terms=3):
    acc = None
    rest = b
    for _ in range(terms):
        hi = rest.astype(BF16)
        part = _nn(a_exact, hi)
        acc = part if acc is None else acc + part
        rest = rest - hi.astype(F32)
    return acc


def _sigmoid(x):
    return 1.0 / (1.0 + jnp.exp(-x))


def _softplus(x):
    return jnp.maximum(x, 0.0) + jnp.log(1.0 + jnp.exp(-jnp.abs(x)))


def _iota2(shape, dim):
    return lax.broadcasted_iota(jnp.int32, shape, dim)


def _after(after):
    ops = [] if after is None else list(after) if isinstance(after, (list, tuple)) else [after]
    return [ANY] * len(ops), ops


def rms_matmul(x, nw, wt, name, after=None):
    s, d = x.shape
    n = wt.shape[0]
    tm, tn = _tile(s, 512), _tile(n, 2816)
    specs, ops = _after(after)

    def body(x_ref, nw_ref, w_ref, *rest):
        o_ref, h_ref = rest[len(ops):]

        @pl.when(pl.program_id(1) == 0)
        def _():
            xv = x_ref[...]
            r = lax.rsqrt(jnp.mean(xv * xv, axis=-1, keepdims=True) + EPS)
            h_ref[...] = (xv * r * nw_ref[...]).astype(BF16)
        o_ref[...] = _nt(h_ref[...], w_ref[...])

    return pl.pallas_call(
        body, name=name, grid=(s // tm, n // tn),
        in_specs=[pl.BlockSpec((tm, d), lambda i, j: (i, 0)), pl.BlockSpec((1, d), lambda i, j: (0, 0)),
                  pl.BlockSpec((tn, d), lambda i, j: (j, 0))] + specs,
        out_specs=pl.BlockSpec((tm, tn), lambda i, j: (i, j)),
        out_shape=jax.ShapeDtypeStruct((s, n), F32),
        scratch_shapes=[pltpu.VMEM((tm, d), BF16)],
        compiler_params=_params(("parallel", "arbitrary")),
    )(x, nw, wt, *ops)


def matmul_residual(a, w, res, name):
    s, k = a.shape
    n = w.shape[1]
    tm, tn = _tile(s, 512), _tile(n, 1024)

    def body(a_ref, w_ref, r_ref, o_ref):
        o_ref[...] = r_ref[...] + _nn(a_ref[...], w_ref[...])

    return pl.pallas_call(
        body, name=name, grid=(s // tm, n // tn),
        in_specs=[pl.BlockSpec((tm, k), lambda i, j: (i, 0)), pl.BlockSpec((k, tn), lambda i, j: (0, j)),
                  pl.BlockSpec((tm, tn), lambda i, j: (i, j))],
        out_specs=pl.BlockSpec((tm, tn), lambda i, j: (i, j)),
        out_shape=jax.ShapeDtypeStruct((s, n), F32),
        compiler_params=_params(("parallel", "parallel")),
    )(a, w, res)


def matmul_nt(a, w, name, out_dtype=F32, after=None):
    s, n = a.shape
    k = w.shape[0]
    tm, tk = _tile(s, 512), _tile(k, 1024)
    specs, ops = _after(after)

    def body(a_ref, w_ref, *rest):
        rest[-1][...] = _nt(a_ref[...], w_ref[...]).astype(out_dtype)

    return pl.pallas_call(
        body, name=name, grid=(s // tm, k // tk),
        in_specs=[pl.BlockSpec((tm, n), lambda i, j: (i, 0)), pl.BlockSpec((tk, n), lambda i, j: (j, 0))] + specs,
        out_specs=pl.BlockSpec((tm, tk), lambda i, j: (i, j)),
        out_shape=jax.ShapeDtypeStruct((s, k), out_dtype),
        compiler_params=_params(("parallel", "parallel")),
    )(a, w, *ops)


def matmul_tn(a, b, name, after=None):
    s, m = a.shape
    n = b.shape[1]
    tm, tn = _tile(m, 512), _tile(n, 1024)

    def body(a_ref, b_ref, *rest):
        rest[-1][...] = _tn(a_ref[...], b_ref[...]).astype(BF16)

    specs, ops = _after(after)
    return pl.pallas_call(
        body, name=name, grid=(m // tm, n // tn),
        in_specs=[pl.BlockSpec((s, tm), lambda i, j: (0, i)), pl.BlockSpec((s, tn), lambda i, j: (0, j))] + specs,
        out_specs=pl.BlockSpec((tm, tn), lambda i, j: (i, j)),
        out_shape=jax.ShapeDtypeStruct((m, n), BF16),
        compiler_params=_params(("parallel", "parallel")),
    )(a, b, *ops)


def ffn_forward(x1, nw, wgt, wut, wd, name):
    s, d = x1.shape
    f = wgt.shape[0]
    tm, tf = _tile(s, 1024), _tile(f, 256)

    def body(x_ref, nw_ref, wg_ref, wu_ref, wd_ref, o_ref, g_ref, u_ref, h_ref, acc_ref):
        j = pl.program_id(1)

        @pl.when(j == 0)
        def _():
            xv = x_ref[...]
            r = lax.rsqrt(jnp.mean(xv * xv, axis=-1, keepdims=True) + EPS)
            h_ref[...] = (xv * r * nw_ref[...]).astype(BF16)
            acc_ref[...] = xv

        h = h_ref[...]
        g = _nt(h, wg_ref[...])
        u = _nt(h, wu_ref[...])
        g_ref[...] = g.astype(BF16)
        u_ref[...] = u.astype(BF16)
        a = (g * _sigmoid(g) * u).astype(BF16)
        acc_ref[...] += _nn(a, wd_ref[...])

        @pl.when(j == pl.num_programs(1) - 1)
        def _():
            o_ref[...] = acc_ref[...]

    wblk = pl.BlockSpec((tf, d), lambda i, j: (j, 0))
    return pl.pallas_call(
        body, name=name, grid=(s // tm, f // tf),
        in_specs=[pl.BlockSpec((tm, d), lambda i, j: (i, 0)), pl.BlockSpec((1, d), lambda i, j: (0, 0)), wblk, wblk, wblk],
        out_specs=[pl.BlockSpec((tm, d), lambda i, j: (i, 0)), pl.BlockSpec((tm, tf), lambda i, j: (i, j)),
                   pl.BlockSpec((tm, tf), lambda i, j: (i, j))],
        out_shape=[jax.ShapeDtypeStruct((s, d), F32), jax.ShapeDtypeStruct((s, f), BF16),
                   jax.ShapeDtypeStruct((s, f), BF16)],
        scratch_shapes=[pltpu.VMEM((tm, d), BF16), pltpu.VMEM((tm, d), F32)],
        compiler_params=_params(("parallel", "arbitrary")),
    )(x1, nw, wgt, wut, wd)


def ffn_backward_act(dx2, g, u, wd, name, after=None):
    s, d = dx2.shape
    f = wd.shape[0]
    tm, tf = _tile(s, 256), _tile(f, 2816)
    specs, ops = _after(after)

    def body(dx_ref, g_ref, u_ref, wd_ref, *rest):
        dg_ref, du_ref, a_ref = rest[len(ops):]
        da = _nt(dx_ref[...], wd_ref[...])
        gv = g_ref[...].astype(F32)
        uv = u_ref[...].astype(F32)
        sg = _sigmoid(gv)
        silu = gv * sg
        dg_ref[...] = (da * uv * (sg * (1.0 + gv * (1.0 - sg)))).astype(BF16)
        du_ref[...] = (da * silu).astype(BF16)
        a_ref[...] = (silu * uv).astype(BF16)

    blk = pl.BlockSpec((tm, tf), lambda i, j: (i, j))
    return pl.pallas_call(
        body, name=name, grid=(s // tm, f // tf),
        in_specs=[pl.BlockSpec((tm, d), lambda i, j: (i, 0)), blk, blk, pl.BlockSpec((tf, d), lambda i, j: (j, 0))] + specs,
        out_specs=[blk, blk, blk],
        out_shape=[jax.ShapeDtypeStruct((s, f), BF16)] * 3,
        compiler_params=_params(("parallel", "parallel")),
    )(dx2, g, u, wd, *ops)


def rms_backward(dzs, wts, x, nw, dres, name, tm, after=None):
    s, d = x.shape
    nz = len(dzs)
    specs, ops = _after(after)

    def body(*refs):
        dz_refs, w_refs = refs[:nz], refs[nz:2 * nz]
        x_ref, nw_ref, dres_ref = refs[2 * nz:2 * nz + 3]
        dx_ref, dxb_ref, h_ref, dnw_ref = refs[2 * nz + 3 + len(ops):]
        dh = _nn(dz_refs[0][...], w_refs[0][...])
        for k in range(1, nz):
            dh = dh + _nn(dz_refs[k][...], w_refs[k][...])
        xv = x_ref[...]
        r = lax.rsqrt(jnp.mean(xv * xv, axis=-1, keepdims=True) + EPS)
        xhat = xv * r
        nwv = nw_ref[...]
        h_ref[...] = (xhat * nwv).astype(BF16)

        @pl.when(pl.program_id(0) == 0)
        def _():
            dnw_ref[...] = jnp.zeros_like(dnw_ref)

        dnw_ref[...] += jnp.sum(dh * xhat, axis=0, keepdims=True)
        gdh = dh * nwv
        dx = dres_ref[...] + r * (gdh - xhat * jnp.mean(gdh * xhat, axis=-1, keepdims=True))
        dx_ref[...] = dx
        dxb_ref[...] = dx.astype(BF16)

    row = pl.BlockSpec((tm, d), lambda i: (i, 0))
    in_specs = [pl.BlockSpec((tm, dz.shape[1]), lambda i: (i, 0)) for dz in dzs]
    in_specs += [pl.BlockSpec(w.shape, lambda i: (0, 0)) for w in wts]
    in_specs += [row, pl.BlockSpec((1, d), lambda i: (0, 0)), row] + specs
    return pl.pallas_call(
        body, name=name, grid=(s // tm,),
        in_specs=in_specs,
        out_specs=[row, row, row, pl.BlockSpec((1, d), lambda i: (0, 0))],
        out_shape=[jax.ShapeDtypeStruct((s, d), F32), jax.ShapeDtypeStruct((s, d), BF16),
                   jax.ShapeDtypeStruct((s, d), BF16), jax.ShapeDtypeStruct((1, d), F32)],
        compiler_params=_params(("arbitrary",)),
    )(*dzs, *wts, x, nw, dres, *ops)


def loss_head(x, nw, target, name):
    s, d = x.shape
    tm = _tile(s, 512)

    def body(x_ref, nw_ref, t_ref, loss_ref, dx_ref, dxb_ref, dnw_ref):
        xv = x_ref[...]
        r = lax.rsqrt(jnp.mean(xv * xv, axis=-1, keepdims=True) + EPS)
        xhat = xv * r
        nwv = nw_ref[...]
        err = xhat * nwv - t_ref[...]

        @pl.when(pl.program_id(0) == 0)
        def _():
            dnw_ref[...] = jnp.zeros_like(dnw_ref)
            loss_ref[...] = jnp.zeros_like(loss_ref)

        part = jnp.sum(jnp.sum(err * err, axis=-1, keepdims=True), axis=0, keepdims=True) * (0.5 / d)
        loss_ref[...] += jnp.broadcast_to(part, loss_ref.shape)
        dout = err * (1.0 / d)
        dnw_ref[...] += jnp.sum(dout * xhat, axis=0, keepdims=True)
        gdh = dout * nwv
        dx = r * (gdh - xhat * jnp.mean(gdh * xhat, axis=-1, keepdims=True))
        dx_ref[...] = dx
        dxb_ref[...] = dx.astype(BF16)

    row = pl.BlockSpec((tm, d), lambda i: (i, 0))
    return pl.pallas_call(
        body, name=name, grid=(s // tm,),
        in_specs=[row, pl.BlockSpec((1, d), lambda i: (0, 0)), row],
        out_specs=[pl.BlockSpec((1, LANES), lambda i: (0, 0)), row, row, pl.BlockSpec((1, d), lambda i: (0, 0))],
        out_shape=[jax.ShapeDtypeStruct((1, LANES), F32), jax.ShapeDtypeStruct((s, d), F32),
                   jax.ShapeDtypeStruct((s, d), BF16), jax.ShapeDtypeStruct((1, d), F32)],
        compiler_params=_params(("arbitrary",)),
    )(x, nw, target)


def _shift_down(x, k):
    return jnp.where(_iota2(x.shape, 0) >= k, pltpu.roll(x, k, axis=0), 0.0)


def _shift_up(x, k):
    s = x.shape[0]
    return jnp.where(_iota2(x.shape, 0) < s - k, pltpu.roll(x, s - k, axis=0), 0.0)


CONV_TILE = 256


def conv_forward(proj, cw, cb, name):
    s = proj.shape[0]
    tn = CONV_TILE
    off = OFF_XBC // tn

    def body(u_ref, w_ref, b_ref, o_ref):
        u = u_ref[...]
        pre = b_ref[...] + w_ref[CONV_WIDTH - 1:CONV_WIDTH, :] * u
        for i in range(CONV_WIDTH - 1):
            pre = pre + w_ref[i:i + 1, :] * _shift_down(u, CONV_WIDTH - 1 - i)
        o_ref[...] = pre * _sigmoid(pre)

    return pl.pallas_call(
        body, name=name, grid=(CONV_DIM // tn,),
        in_specs=[pl.BlockSpec((s, tn), lambda j: (0, off + j)), pl.BlockSpec((8, tn), lambda j: (0, j)),
                  pl.BlockSpec((1, tn), lambda j: (0, j))],
        out_specs=pl.BlockSpec((s, tn), lambda j: (0, j)),
        out_shape=jax.ShapeDtypeStruct((s, CONV_DIM), F32),
        compiler_params=_params(("parallel",)),
    )(proj, cw, cb)


def conv_backward(proj, dxc, cw, cb, dproj, name, after=None):
    s = proj.shape[0]
    tn = CONV_TILE
    off = OFF_XBC // tn

    specs, ops = _after(after)

    def body(u_ref, d_ref, w_ref, b_ref, *rest):
        du_ref, dw_ref, db_ref = rest[-3:]
        u = u_ref[...]
        shifted = [_shift_down(u, CONV_WIDTH - 1 - i) for i in range(CONV_WIDTH - 1)] + [u]
        pre = b_ref[...] + w_ref[CONV_WIDTH - 1:CONV_WIDTH, :] * u
        for i in range(CONV_WIDTH - 1):
            pre = pre + w_ref[i:i + 1, :] * shifted[i]
        sg = _sigmoid(pre)
        dpre = d_ref[...] * (sg * (1.0 + pre * (1.0 - sg)))
        du = w_ref[CONV_WIDTH - 1:CONV_WIDTH, :] * dpre
        for i in range(CONV_WIDTH - 1):
            du = du + w_ref[i:i + 1, :] * _shift_up(dpre, CONV_WIDTH - 1 - i)
        du_ref[...] = du.astype(BF16)
        rows = [jnp.sum(dpre * shifted[i], axis=0, keepdims=True) for i in range(CONV_WIDTH)]
        rows.append(jnp.zeros((8 - CONV_WIDTH, tn), F32))
        dw_ref[...] = jnp.concatenate(rows, axis=0)
        db_ref[...] = jnp.sum(dpre, axis=0, keepdims=True)

    return pl.pallas_call(
        body, name=name, grid=(CONV_DIM // tn,),
        in_specs=[pl.BlockSpec((s, tn), lambda j: (0, off + j)), pl.BlockSpec((s, tn), lambda j: (0, j)),
                  pl.BlockSpec((8, tn), lambda j: (0, j)), pl.BlockSpec((1, tn), lambda j: (0, j)), ANY] + specs,
        out_specs=[pl.BlockSpec((s, tn), lambda j: (0, off + j)), pl.BlockSpec((8, tn), lambda j: (0, j)),
                   pl.BlockSpec((1, tn), lambda j: (0, j))],
        out_shape=[jax.ShapeDtypeStruct(dproj.shape, BF16), jax.ShapeDtypeStruct((8, CONV_DIM), F32),
                   jax.ShapeDtypeStruct((1, CONV_DIM), F32)],
        input_output_aliases={4: 0},
        compiler_params=_params(("parallel",)),
    )(proj, dxc, cw, cb, dproj, *ops)


def _pool_lane_window(shape):
    grp = _iota2(shape, 1) // (POOL_WIDTH // len(POOL_WINDOWS))
    win = jnp.full(shape, POOL_WINDOWS[-1], jnp.int32)
    for gi in range(len(POOL_WINDOWS) - 2, -1, -1):
        win = jnp.where(grp == gi, POOL_WINDOWS[gi], win)
    return grp, win


def _pool_select(grp, sums):
    out = sums[-1]
    for gi in range(len(sums) - 2, -1, -1):
        out = jnp.where(grp == gi, sums[gi], out)
    return out


def _pool_pooled(p):
    grp, win = _pool_lane_window(p.shape)
    inv_count = 1.0 / jnp.minimum(_iota2(p.shape, 0) + 1, win).astype(F32)
    sums, acc, k = [], p, 1
    for _ in POOL_WINDOWS:
        acc = acc + _shift_down(acc, k)
        sums.append(acc)
        k *= 2
    return _pool_select(grp, sums) * inv_count - p, grp, inv_count


def pool_forward(proj, wbd, pb, ps, y_all, name, after=None):
    s = proj.shape[0]
    specs, ops = _after(after)

    def body(p_ref, w_ref, b_ref, s_ref, *rest):
        o_ref = rest[-1]
        pooled, _, _ = _pool_pooled(p_ref[...])
        mixed = _nn(pooled.astype(BF16), w_ref[...]) + b_ref[...]
        o_ref[...] = (mixed * s_ref[...]).astype(BF16)

    vec = pl.BlockSpec((1, POOL_WIDTH), lambda j: (0, 0))
    return pl.pallas_call(
        body, name=name, grid=(1,),
        in_specs=[pl.BlockSpec((s, POOL_WIDTH), lambda j: (0, OFF_P // POOL_WIDTH)),
                  pl.BlockSpec((POOL_WIDTH, POOL_WIDTH), lambda j: (0, 0)), vec, vec, ANY] + specs,
        out_specs=pl.BlockSpec((s, POOL_WIDTH), lambda j: (0, (SSD_WIDTH + SB_WIDTH) // POOL_WIDTH)),
        out_shape=jax.ShapeDtypeStruct(y_all.shape, BF16),
        input_output_aliases={4: 0},
        compiler_params=_params(("arbitrary",)),
    )(proj, wbd, pb, ps, y_all, *ops)


def pool_backward(proj, dyall, wbd, pb, ps, dproj, name):
    s = proj.shape[0]

    def body(p_ref, dy_ref, w_ref, b_ref, s_ref, _, dp_ref, dw_ref, db_ref, ds_ref):
        pooled, grp, inv_count = _pool_pooled(p_ref[...])
        pooled_b = pooled.astype(BF16)
        mixed = _nn(pooled_b, w_ref[...]) + b_ref[...]
        dy = dy_ref[...]
        ds_ref[...] = jnp.sum(dy * mixed, axis=0, keepdims=True)
        dmixed = dy * s_ref[...]
        db_ref[...] = jnp.sum(dmixed, axis=0, keepdims=True)
        dmixed_b = dmixed.astype(BF16)
        dw_ref[...] = _tn(pooled_b, dmixed_b)
        dpooled = _nt(dmixed_b, w_ref[...])
        sums, acc, k = [], dpooled * inv_count, 1
        for _ in POOL_WINDOWS:
            acc = acc + _shift_up(acc, k)
            sums.append(acc)
            k *= 2
        dp_ref[...] = (_pool_select(grp, sums) - dpooled).astype(BF16)

    vec = pl.BlockSpec((1, POOL_WIDTH), lambda j: (0, 0))
    mat = pl.BlockSpec((POOL_WIDTH, POOL_WIDTH), lambda j: (0, 0))
    pcol = pl.BlockSpec((s, POOL_WIDTH), lambda j: (0, OFF_P // POOL_WIDTH))
    return pl.pallas_call(
        body, name=name, grid=(1,),
        in_specs=[pcol, pl.BlockSpec((s, POOL_WIDTH), lambda j: (0, (SSD_WIDTH + SB_WIDTH) // POOL_WIDTH)), mat, vec, vec, ANY],
        out_specs=[pcol, mat, vec, vec],
        out_shape=[jax.ShapeDtypeStruct(dproj.shape, BF16), jax.ShapeDtypeStruct((POOL_WIDTH, POOL_WIDTH), F32),
                   jax.ShapeDtypeStruct((1, POOL_WIDTH), F32), jax.ShapeDtypeStruct((1, POOL_WIDTH), F32)],
        input_output_aliases={5: 0},
        compiler_params=_params(("arbitrary",)),
    )(proj, dyall, wbd, pb, ps, dproj)


N_PAIRS = SSD_HEADS // 2


def _ssd_common(xc, dtraw, dtb, alog):
    c = CHUNK
    dt = _softplus(dtraw + dtb)
    a = -jnp.exp(alog)
    ltri = (_iota2((c, c), 0) >= _iota2((c, c), 1)).astype(BF16)
    acum = _split_dot_left(ltri, dt * a)
    expand = (_iota2((c, SSD_WIDTH), 1) // HEAD_DIM == _iota2((c, SSD_WIDTH), 0)).astype(BF16)
    expand_wide = (_iota2((c, SSD_HEADS * c), 1) // c == _iota2((c, SSD_HEADS * c), 0)).astype(BF16)
    acum_x = _split_dot(acum, expand)
    dt_x = _split_dot(dt, expand)
    alast_x = acum_x[c - 1:c, :]
    return dict(dt=dt, a=a, acum=acum, acum_x=acum_x, dt_x=dt_x, ea_x=jnp.exp(acum_x),
                dte_x=jnp.exp(alast_x - acum_x), eal_x=jnp.exp(alast_x),
                acol=_split_dot(acum, expand_wide), acum_t=acum.T,
                xs=xc[:, :SSD_WIDTH], causal=_iota2((c, c), 0) >= _iota2((c, c), 1),
                left=_iota2((c, c), 1) < HEAD_DIM)


def _ssd_group(xc, g):
    b = xc[:, SSD_WIDTH + D_STATE * g:SSD_WIDTH + D_STATE * (g + 1)]
    cm = xc[:, SSD_WIDTH + 2 * D_STATE + D_STATE * g:SSD_WIDTH + 2 * D_STATE + D_STATE * (g + 1)]
    return b, cm


def _ssd_decay(q, hh):
    col = q["acol"][:, CHUNK * hh:CHUNK * (hh + 1)]
    row = q["acum_t"][hh:hh + 1, :]
    return jnp.where(q["causal"], jnp.exp(jnp.minimum(col - row, 0.0)), 0.0)


def ssd_forward(proj, xc, dtb, alog, dskip_x, nw, name):
    s = xc.shape[0]
    nc = s // CHUNK

    def body(xc_ref, zdt_ref, dtb_ref, alog_ref, dsk_ref, nw_ref, y_ref, yc_ref, st_ref, state):
        @pl.when(pl.program_id(0) == 0)
        def _():
            state[...] = jnp.zeros_like(state)

        xcv = xc_ref[...]
        q = _ssd_common(xcv, zdt_ref[:, SSD_WIDTH:SSD_WIDTH + LANES], dtb_ref[...], alog_ref[...])
        x = q["xs"] * q["dt_x"]
        xb = x.astype(BF16)
        xd = (x * q["dte_x"]).astype(BF16)
        pieces = []
        for g in range(2):
            bg, cg = _ssd_group(xcv, g)
            bgb, cgb = bg.astype(BF16), cg.astype(BF16)
            cb = _nt(cgb, bgb)
            bgt = bg.T.astype(BF16)
            for pr in (2 * g, 2 * g + 1):
                sl = slice(CHUNK * pr, CHUNK * (pr + 1))
                st = state[pr]
                st_ref[0, pr] = st
                yp = _nn(cgb, st.astype(BF16)) * q["ea_x"][:, sl]
                for k, hh in enumerate((2 * pr, 2 * pr + 1)):
                    w = (cb * _ssd_decay(q, hh)).astype(BF16)
                    mask = q["left"] if k == 0 else jnp.logical_not(q["left"])
                    yp = yp + _nn(w, jnp.where(mask, xb[:, sl], jnp.zeros_like(xb[:, sl])))
                state[pr] = st * q["eal_x"][:, sl] + _nn(bgt, xd[:, sl])
                pieces.append(yp)
        y = jnp.concatenate(pieces, axis=1) + q["xs"] * dsk_ref[...]
        yc_ref[...] = y
        zv = zdt_ref[:, :SSD_WIDTH]
        yg = y * (zv * _sigmoid(zv))
        r = lax.rsqrt(jnp.mean(yg * yg, axis=-1, keepdims=True) + EPS)
        y_ref[...] = (yg * r * nw_ref[...]).astype(BF16)

    vec = lambda n: pl.BlockSpec((1, n), lambda c: (0, 0))
    return pl.pallas_call(
        body, name=name, grid=(nc,),
        in_specs=[pl.BlockSpec((CHUNK, CONV_DIM), lambda c: (c, 0)),
                  pl.BlockSpec((CHUNK, ZDT), lambda c: (c, OFF_Z // ZDT)),
                  vec(LANES), vec(LANES), vec(SSD_WIDTH), vec(SSD_WIDTH)],
        out_specs=[pl.BlockSpec((CHUNK, SSD_WIDTH), lambda c: (c, 0)), pl.BlockSpec((CHUNK, SSD_WIDTH), lambda c: (c, 0)),
                   pl.BlockSpec((1, N_PAIRS, D_STATE, CHUNK), lambda c: (c, 0, 0, 0))],
        out_shape=[jax.ShapeDtypeStruct((s, D_MODEL), BF16), jax.ShapeDtypeStruct((s, SSD_WIDTH), F32),
                   jax.ShapeDtypeStruct((nc, N_PAIRS, D_STATE, CHUNK), F32)],
        scratch_shapes=[pltpu.VMEM((N_PAIRS, D_STATE, CHUNK), F32)],
        compiler_params=_params(("arbitrary",)),
    )(xc, proj, dtb, alog, dskip_x, nw)


def ssd_backward(proj, xc, ycore, dyall, states, dtb, alog, dskip_x, nw, name):
    s = xc.shape[0]
    nc = s // CHUNK
    c = CHUNK

    def body(xc_ref, zdt_ref, yc_ref, dy_ref, st_ref, dtb_ref, alog_ref, dsk_ref, nw_ref,
             dxc_ref, dzdt_ref, dnw_ref, ddsk_ref, ddtb_ref, dalog_ref, dstate):
        @pl.when(pl.program_id(0) == 0)
        def _():
            dstate[...] = jnp.zeros_like(dstate)
            dnw_ref[...] = jnp.zeros_like(dnw_ref)
            ddsk_ref[...] = jnp.zeros_like(ddsk_ref)
            ddtb_ref[...] = jnp.zeros_like(ddtb_ref)
            dalog_ref[...] = jnp.zeros_like(dalog_ref)

        xcv = xc_ref[...]
        dtraw = zdt_ref[:, SSD_WIDTH:SSD_WIDTH + LANES]
        q = _ssd_common(xcv, dtraw, dtb_ref[...], alog_ref[...])
        xs = q["xs"]
        x = xs * q["dt_x"]
        zv, yc, dy, nwv = zdt_ref[:, :SSD_WIDTH], yc_ref[...], dy_ref[...], nw_ref[...]
        sgz = _sigmoid(zv)
        siluz = zv * sgz
        yg = yc * siluz
        r = lax.rsqrt(jnp.mean(yg * yg, axis=-1, keepdims=True) + EPS)
        dnw_ref[...] += jnp.sum(dy * yg * r, axis=0, keepdims=True)
        g1 = dy * nwv
        dyg = r * (g1 - yg * (r * r) * jnp.mean(g1 * yg, axis=-1, keepdims=True))
        dyv = dyg * siluz
        dz = (dyg * yc * (sgz * (1.0 + zv * (1.0 - sgz)))).astype(BF16)
        ddsk_ref[...] += jnp.sum(dyv * xs, axis=0, keepdims=True)
        dye = dyv * q["ea_x"]
        dx_parts, yoff_parts, u_parts, v_parts, e_parts = [], [], [], [], []
        db_parts, dc_parts = [], []
        for g in range(2):
            bg, cg = _ssd_group(xcv, g)
            bgb, cgb = bg.astype(BF16), cg.astype(BF16)
            cb = _nt(cgb, bgb)
            cgt = cg.T.astype(BF16)
            dgsum = jnp.zeros((c, c), F32)
            dbg = jnp.zeros((c, D_STATE), F32)
            dcg = jnp.zeros((c, D_STATE), F32)
            for pr in (2 * g, 2 * g + 1):
                sl = slice(c * pr, c * (pr + 1))
                st = st_ref[0, pr]
                dst = dstate[pr]
                stb, dstb = st.astype(BF16), dst.astype(BF16)
                xp = x[:, sl]
                xpb = xp.astype(BF16)
                dyp = dyv[:, sl]
                xdp = xp * q["dte_x"][:, sl]
                yoff_parts.append(_nn(cgb, stb) * q["ea_x"][:, sl])
                rr = _nn(bgb, dstb)
                dxp = rr * q["dte_x"][:, sl]
                u_parts.append(rr * xdp)
                v_parts.append(dst * st * q["eal_x"][:, sl])
                for k, hh in enumerate((2 * pr, 2 * pr + 1)):
                    decay = _ssd_decay(q, hh)
                    w = cb * decay
                    mask = q["left"] if k == 0 else jnp.logical_not(q["left"])
                    dym = jnp.where(mask, dyp, 0.0).astype(BF16)
                    dw = _nt(dym, xpb)
                    dgsum = dgsum + dw * decay
                    e_parts.append(dw * w)
                    dxp = dxp + _nn(w.T.astype(BF16), dym)
                dyeb = dye[:, sl].astype(BF16)
                dcg = dcg + _nt(dyeb, stb)
                dbg = dbg + _nt(xdp.astype(BF16), dstb)
                dstate[pr] = dst * q["eal_x"][:, sl] + _nn(cgt, dyeb)
                dx_parts.append(dxp)
            dcg = dcg + _nn(dgsum.astype(BF16), bgb)
            dbg = dbg + _nn(dgsum.T.astype(BF16), cgb)
            db_parts.append(dbg)
            dc_parts.append(dcg)
        dx = jnp.concatenate(dx_parts, axis=1)
        yoff = jnp.concatenate(yoff_parts, axis=1)
        u = jnp.concatenate(u_parts, axis=1)
        v = jnp.concatenate(v_parts, axis=1)
        reduce_heads = (_iota2((SSD_WIDTH, c), 0) // HEAD_DIM == _iota2((SSD_WIDTH, c), 1)).astype(BF16)
        to_head = (_iota2((SSD_HEADS * c, c), 0) // c == _iota2((SSD_HEADS * c, c), 1)).astype(BF16)
        da = _split_dot(dyv * yoff - u, reduce_heads, 2)
        da = da + _split_dot(jnp.concatenate(e_parts, axis=1), to_head, 2)
        da = da - _split_dot(jnp.concatenate(e_parts, axis=0), to_head, 2, dot=_tn)
        dalast = jnp.sum(_split_dot(u + v, reduce_heads, 2), axis=0, keepdims=True)
        da = da + jnp.where(_iota2((c, c), 0) == c - 1, dalast, 0.0)
        utri = (_iota2((c, c), 1) >= _iota2((c, c), 0)).astype(BF16)
        dda = _split_dot_left(utri, da)
        ddt = dda * q["a"] + _split_dot(dx * xs, reduce_heads, 2)
        dalog_ref[...] += jnp.sum(dda * q["dt"], axis=0, keepdims=True) * q["a"]
        ddtraw = jnp.where(_iota2((c, c), 1) < SSD_HEADS, ddt * _sigmoid(dtraw + dtb_ref[...]), 0.0)
        ddtb_ref[...] += jnp.sum(ddtraw, axis=0, keepdims=True)
        dzdt_ref[...] = jnp.concatenate([dz, ddtraw.astype(BF16), jnp.zeros((c, ZDT - SSD_WIDTH - LANES), BF16)], axis=1)
        dxs = dx * q["dt_x"] + dyv * dsk_ref[...]
        dxc_ref[...] = jnp.concatenate([dxs] + db_parts + dc_parts, axis=1)

    rev = lambda i: nc - 1 - i
    vec = lambda n: pl.BlockSpec((1, n), lambda i: (0, 0))
    wide = pl.BlockSpec((c, SSD_WIDTH), lambda i: (rev(i), 0))
    zdt = pl.BlockSpec((c, ZDT), lambda i: (rev(i), OFF_Z // ZDT))
    return pl.pallas_call(
        body, name=name, grid=(nc,),
        in_specs=[pl.BlockSpec((c, CONV_DIM), lambda i: (rev(i), 0)), zdt, wide, wide,
                  pl.BlockSpec((1, N_PAIRS, D_STATE, c), lambda i: (rev(i), 0, 0, 0)),
                  vec(LANES), vec(LANES), vec(SSD_WIDTH), vec(SSD_WIDTH)],
        out_specs=[pl.BlockSpec((c, CONV_DIM), lambda i: (rev(i), 0)), zdt,
                   vec(SSD_WIDTH), vec(SSD_WIDTH), vec(LANES), vec(LANES)],
        out_shape=[jax.ShapeDtypeStruct((s, CONV_DIM), F32), jax.ShapeDtypeStruct((s, D_INP), BF16),
                   jax.ShapeDtypeStruct((1, SSD_WIDTH), F32),
                   jax.ShapeDtypeStruct((1, SSD_WIDTH), F32), jax.ShapeDtypeStruct((1, LANES), F32),
                   jax.ShapeDtypeStruct((1, LANES), F32)],
        scratch_shapes=[pltpu.VMEM((N_PAIRS, D_STATE, c), F32)],
        compiler_params=_params(("arbitrary",)),
    )(xc, proj, ycore, dyall, states, dtb, alog, dskip_x, nw)


SB_Q, SB_K = 256, 512
SB_SCALE = HEAD_DIM ** -0.5


def _sb_weights(qm, kb, diagonal, run_lk, strict_after):
    z = _nt(qm, kb)
    nz = -z
    tail = jnp.log(1.0 + jnp.exp(jnp.minimum(z, nz)))
    ls = jnp.minimum(z, 0.0) - tail
    lk = jnp.minimum(nz, 0.0) - tail
    if diagonal is not None:
        valid = _iota2(z.shape, 1) < _iota2(z.shape, 0) + diagonal
        lk = jnp.where(valid, lk, 0.0)
    w = jnp.exp(ls + _nn(lk.astype(BF16), strict_after) + run_lk)
    if diagonal is not None:
        w = jnp.where(valid, w, 0.0)
    return ls, lk, w


def _sb_sweep(i, block, init):
    own = (i * SB_Q) // SB_K
    first = block(own, init, i * SB_Q - own * SB_K)
    return lax.fori_loop(1, own + 1, lambda jj, carry: block(own - jj, carry, None), first)


def sb_forward(proj, y_all, name):
    s = proj.shape[0]
    t, tk = SB_Q, SB_K
    nq = s // t

    def body(q_ref, k_ref, v_ref, _, y_ref, o_ref):
        i = pl.program_id(1)
        left = _iota2((t, LANES), 1) < HEAD_DIM
        left_k = _iota2((tk, LANES), 1) < HEAD_DIM
        qv = q_ref[...] * SB_SCALE
        zero = jnp.zeros_like(qv)
        qms = (jnp.where(left, qv, zero).astype(BF16), jnp.where(left, zero, qv).astype(BF16))
        strict_after = (_iota2((tk, tk), 0) > _iota2((tk, tk), 1)).astype(BF16)

        def block(j, carry, diagonal):
            o, runs = carry[0], carry[1:]
            rows = pl.ds(pl.multiple_of(j * tk, tk), tk)
            kb = k_ref[rows, :].astype(BF16)
            vv = v_ref[rows, :]
            new_runs = []
            for k in range(2):
                _, lk, w = _sb_weights(qms[k], kb, diagonal, runs[k], strict_after)
                vm = jnp.where(left_k if k == 0 else jnp.logical_not(left_k), vv, 0.0).astype(BF16)
                o = o + _nn(w.astype(BF16), vm)
                new_runs.append(runs[k] + jnp.sum(lk, axis=1, keepdims=True))
            return (o, *new_runs)

        init = (jnp.zeros((t, LANES), F32), jnp.zeros((t, 1), F32), jnp.zeros((t, 1), F32))
        o = _sb_sweep(i, block, init)[0]
        o_ref[...] = o
        y_ref[...] = o.astype(BF16)

    return pl.pallas_call(
        body, name=name, grid=(2, nq),
        in_specs=[pl.BlockSpec((t, LANES), lambda p, i: (i, 3 * p)),
                  pl.BlockSpec((s, LANES), lambda p, i: (0, 3 * p + 1)),
                  pl.BlockSpec((s, LANES), lambda p, i: (0, 3 * p + 2)), ANY],
        out_specs=[pl.BlockSpec((t, LANES), lambda p, i: (i, SSD_WIDTH // LANES + p)),
                   pl.BlockSpec((t, LANES), lambda p, i: (i, p))],
        out_shape=[jax.ShapeDtypeStruct(y_all.shape, BF16), jax.ShapeDtypeStruct((s, SB_WIDTH), F32)],
        input_output_aliases={3: 0},
        compiler_params=_params(("parallel", "arbitrary")),
    )(proj, proj, proj, y_all)


def sb_backward(proj, o, dyall, dproj, name, after=None):
    s = proj.shape[0]
    t, tk = SB_Q, SB_K
    nq = s // t
    specs, ops = _after(after)

    def body(q_ref, k_ref, v_ref, o_ref, do_ref, *rest):
        dqkv_ref, dk_acc, dv_acc = rest[-3:]
        dk_acc[...] = jnp.zeros_like(dk_acc)
        dv_acc[...] = jnp.zeros_like(dv_acc)
        left = _iota2((t, LANES), 1) < HEAD_DIM
        lane_masks = (left, jnp.logical_not(left))
        left_k = _iota2((tk, LANES), 1) < HEAD_DIM
        key_masks = (left_k, jnp.logical_not(left_k))
        strict_after = (_iota2((tk, tk), 0) > _iota2((tk, tk), 1)).astype(BF16)
        from_here = (_iota2((tk, tk), 0) >= _iota2((tk, tk), 1)).astype(BF16)

        def query_block(i, _):
            qrows = pl.ds(pl.multiple_of(i * t, t), t)
            qv = q_ref[qrows, :] * SB_SCALE
            dov = do_ref[qrows, :]
            zero = jnp.zeros_like(qv)
            qb = qv.astype(BF16)
            dob = dov.astype(BF16)
            prod = dob.astype(F32) * o_ref[qrows, :]
            qms = [jnp.where(m, qv, zero).astype(BF16) for m in lane_masks]
            doms = [jnp.where(m, dov, zero).astype(BF16) for m in lane_masks]
            deltas = [jnp.sum(jnp.where(m, prod, zero), axis=1, keepdims=True) for m in lane_masks]

            def block(j, carry, diagonal):
                dq = carry[0]
                run_lk, run_e = carry[1:3], carry[3:5]
                rows = pl.ds(pl.multiple_of(j * tk, tk), tk)
                kb = k_ref[rows, :].astype(BF16)
                vb = v_ref[rows, :].astype(BF16)
                dkj = jnp.zeros((tk, LANES), F32)
                dvj = jnp.zeros((tk, LANES), F32)
                new_lk, new_e = [], []
                for k in range(2):
                    ls, lk, w = _sb_weights(qms[k], kb, diagonal, run_lk[k], strict_after)
                    wb = w.astype(BF16)
                    e = _nt(doms[k], vb) * wb.astype(F32)
                    before = deltas[k] - _split_dot(e, from_here, 2) - run_e[k]
                    dz = e - jnp.exp(ls) * (e + before)
                    if diagonal is not None:
                        dz = jnp.where(_iota2(dz.shape, 1) < _iota2(dz.shape, 0) + diagonal, dz, 0.0)
                    dz = dz.astype(BF16)
                    m = lane_masks[k]
                    dvj = dvj + jnp.where(key_masks[k], _tn(wb, dob), 0.0)
                    dkj = dkj + jnp.where(key_masks[k], _tn(dz, qb), 0.0)
                    dq = dq + jnp.where(m, _nn(dz, kb), 0.0)
                    new_lk.append(run_lk[k] + jnp.sum(lk, axis=1, keepdims=True))
                    new_e.append(run_e[k] + jnp.sum(e, axis=1, keepdims=True))
                dk_acc[rows, :] += dkj
                dv_acc[rows, :] += dvj
                return (dq, *new_lk, *new_e)

            col = jnp.zeros((t, 1), F32)
            dq = _sb_sweep(i, block, (jnp.zeros((t, LANES), F32), col, col, col, col))[0]
            dqkv_ref[qrows, 0:LANES] = (dq * SB_SCALE).astype(BF16)
            return 0

        lax.fori_loop(0, nq, query_block, 0)
        dqkv_ref[:, LANES:2 * LANES] = dk_acc[...].astype(BF16)
        dqkv_ref[:, 2 * LANES:3 * LANES] = dv_acc[...].astype(BF16)

    col = lambda f: pl.BlockSpec((s, LANES), f)
    return pl.pallas_call(
        body, name=name, grid=(2,),
        in_specs=[col(lambda p: (0, 3 * p)), col(lambda p: (0, 3 * p + 1)), col(lambda p: (0, 3 * p + 2)),
                  col(lambda p: (0, p)), col(lambda p: (0, SSD_WIDTH // LANES + p)), ANY] + specs,
        out_specs=pl.BlockSpec((s, 3 * LANES), lambda p: (0, p)),
        out_shape=jax.ShapeDtypeStruct(dproj.shape, BF16),
        input_output_aliases={5: 0},
        scratch_shapes=[pltpu.VMEM((s, LANES), F32), pltpu.VMEM((s, LANES), F32)],
        compiler_params=_params(("parallel",)),
    )(proj, proj, proj, o, dyall, dproj, *ops)


def adamw(w, g, m, v, name):
    b, r, c = w.shape
    tr = max([t for t in range(8, min(r, 512) + 1, 8) if r % t == 0], default=r)

    def body(w_ref, g_ref, m_ref, v_ref, d_ref, nm_ref, nv_ref):
        gv = g_ref[...]
        nm = ADAM_B1 * m_ref[...] + (1.0 - ADAM_B1) * gv
        nv = ADAM_B2 * v_ref[...] + (1.0 - ADAM_B2) * (gv * gv)
        m_hat = nm / (1.0 - ADAM_B1 ** ADAM_STEP)
        v_hat = nv / (1.0 - ADAM_B2 ** ADAM_STEP)
        d_ref[...] = -ADAM_LR * (m_hat / (jnp.sqrt(v_hat) + ADAM_EPS) + ADAM_WD * w_ref[...])
        nm_ref[...] = nm
        nv_ref[...] = nv

    blk = pl.BlockSpec((1, tr, c), lambda i, j: (i, j, 0))
    return pl.pallas_call(
        body, name=name, grid=(b, r // tr),
        in_specs=[blk] * 4, out_specs=[blk] * 3,
        out_shape=[jax.ShapeDtypeStruct(w.shape, F32)] * 3,
        compiler_params=_params(("parallel", "parallel")),
    )(w, g, m, v)


def _position():
    return lax.axis_index("x"), lax.axis_index("y"), lax.axis_index("c")


def _flipped(pos, flip):
    return tuple((1 - p) if f else p for p, f in zip(pos, flip))


FLIP_C = (0, 0, 1)
CHIP_FLIPS = {1: (0, 1, 0), 2: (1, 0, 0), 3: (1, 1, 0)}
SHARD_ROWS = (SHARD_IN, SHARD_OUT, SHARD_FF, SHARD_FF, SHARD_FF)


def _rows(start, size):
    return pl.ds(pl.multiple_of(start, 16), size)


HBM = pl.BlockSpec(memory_space=pltpu.HBM)
SEM = pl.BlockSpec(memory_space=pltpu.SEMAPHORE)
EFFECT = pltpu.SideEffectType.DATAFLOW_SIDE_EFFECTING


def _in_hbm(a):
    return pltpu.with_memory_space_constraint(a, pltpu.HBM)


def _landing(shape, dtype):
    return _in_hbm(lax.empty(shape, dtype))


def _copies(plan, pos, src_refs, land_refs, send_sems, recv_sems):
    return [pltpu.make_async_remote_copy(src_ref=src, dst_ref=dst, send_sem=send_sems.at[k], recv_sem=recv_sems.at[k],
                                         device_id=_flipped(pos, flip), device_id_type=MESH)
            for k, (src, dst, flip) in enumerate(plan(pos, src_refs, land_refs))]


def exchange_start(name, srcs, lands, n, plan, after=None):
    ns, nl = len(srcs), len(lands)
    specs, ops = _after(after)

    def body(*refs):
        src_refs, land_refs = refs[:ns], refs[ns:ns + nl]
        send_sems, recv_sems, token = refs[ns + nl + len(ops)], refs[ns + nl + len(ops) + 1], refs[-1]
        for cp in _copies(plan, _position(), src_refs, land_refs, send_sems, recv_sems):
            cp.start()
        token[...] = jnp.zeros_like(token)

    thru = [pltpu.HBM(a.shape, a.dtype) for a in list(srcs) + list(lands)]
    out = pl.pallas_call(
        body, name=name,
        out_shape=(pltpu.SemaphoreType.DMA((n,)), pltpu.SemaphoreType.DMA((n,)), *thru, jax.ShapeDtypeStruct((8, LANES), F32)),
        in_specs=[HBM] * (ns + nl) + specs,
        out_specs=(SEM, SEM, *([HBM] * (ns + nl)), pl.BlockSpec(memory_space=pltpu.VMEM)),
        input_output_aliases={k: 2 + k for k in range(ns + nl)},
        compiler_params=pltpu.CompilerParams(has_side_effects=EFFECT),
    )(*[_in_hbm(a) for a in srcs], *lands, *ops)
    return out[0], out[1], list(out[2:2 + ns]), list(out[2 + ns:2 + ns + nl]), out[-1]


def exchange_wait(name, started, after, plan):
    send_sems, recv_sems, srcs, lands, _ = started
    ns, nl = len(srcs), len(lands)
    specs, ops = _after(after)

    def body(*refs):
        src_refs, land_refs = refs[:ns], refs[ns:ns + nl]
        send_sems, recv_sems = refs[ns + nl], refs[ns + nl + 1]
        for cp in _copies(plan, _position(), src_refs, land_refs, send_sems, recv_sems):
            cp.wait_send()
            cp.wait_recv()

    out = pl.pallas_call(
        body, name=name,
        out_shape=tuple(pltpu.HBM(a.shape, a.dtype) for a in list(srcs) + list(lands)),
        in_specs=[HBM] * (ns + nl) + [SEM, SEM] + specs,
        out_specs=tuple([HBM] * (ns + nl)),
        input_output_aliases={k: k for k in range(ns + nl)},
        compiler_params=pltpu.CompilerParams(has_side_effects=EFFECT),
    )(*srcs, *lands, send_sems, recv_sems, *ops)
    return list(out[:ns]), list(out[ns:])


def _gather_ici_plan(pos, srcs, lands):
    chip, c = 2 * pos[0] + pos[1], pos[2]
    copies = []
    for src, dst in zip(srcs, lands):
        r = src.shape[0]
        h = r // 2
        for f in (1, 2, 3):
            copies.append((src.at[_rows(c * h, h)], dst.at[_rows(chip * r + c * h, h)], CHIP_FLIPS[f]))
    return copies


def _gather_d2d_plan(pos, srcs, lands):
    chip, c = 2 * pos[0] + pos[1], pos[2]
    copies = []
    for own, dst in zip(srcs, lands):
        r = own.shape[0]
        h = r // 2
        copies.append((own, dst.at[_rows(chip * r, r)], FLIP_C))
        for f in (1, 2, 3):
            at = _rows(lax.bitwise_xor(chip, f) * r + c * h, h)
            copies.append((dst.at[at], dst.at[at], FLIP_C))
    return copies


def gather_ici_start(shards, after=None):
    lands = [_landing((N_CHIPS * a.shape[0], D_MODEL), BF16) for a in shards]
    return exchange_start("gather_ici_start", shards, lands, 3 * len(shards), _gather_ici_plan, after=after)


def gather_d2d_start(shards, fulls, after=None):
    return exchange_start("gather_d2d_start", shards, fulls, 4 * len(shards), _gather_d2d_plan, after=after)


def _reduce_d2d_plan(pos, srcs, lands):
    c = pos[2]
    return [(src.at[:, _rows((1 - c) * (src.shape[1] // 2), src.shape[1] // 2)], dst, FLIP_C) for src, dst in zip(srcs, lands)]


def _reduce_ici_plan(pos, srcs, lands):
    chip = 2 * pos[0] + pos[1]
    return [(src.at[lax.bitwise_xor(chip, f)], dst.at[f - 1], CHIP_FLIPS[f]) for src, dst in zip(srcs, lands) for f in (1, 2, 3)]


def _reduce_swap_plan(pos, srcs, lands):
    c = pos[2]
    copies = []
    for dst in lands:
        h = dst.shape[0] // 2
        at = _rows(c * h, h)
        copies.append((dst.at[at], dst.at[at], FLIP_C))
    return copies


def reduce_d2d_start(grads):
    lands = [_landing((N_CHIPS, g.shape[1] // 2, D_MODEL), BF16) for g in grads]
    return exchange_start("reduce_d2d_start", grads, lands, len(grads), _reduce_d2d_plan)


def reduce_ici_start(chip_sums):
    lands = [_landing((N_CHIPS - 1,) + p.shape[1:], BF16) for p in chip_sums]
    return exchange_start("reduce_ici_start", chip_sums, lands, 3 * len(chip_sums), _reduce_ici_plan)


def reduce_swap_start(mine):
    return exchange_start("reduce_swap_start", [], mine, len(mine), _reduce_swap_plan)


def _by_shape(fn, *lists):
    groups, out = {}, [None] * len(lists[0])
    for k, a in enumerate(lists[0]):
        groups.setdefault(a.shape, []).append(k)
    for idx in groups.values():
        for k, r in zip(idx, fn(*[[l[k] for k in idx] for l in lists])):
            out[k] = r
    return out


def add_halves(ds, recvs, half, name):
    n = len(ds)
    nch, r, c = ds[0].shape
    h = r // 2

    def body(half_ref, *refs):
        for k in range(n):
            refs[2 * n + k][...] = (refs[k][...].astype(F32) + refs[n + k][...].astype(F32)).astype(BF16)

    mine = pl.BlockSpec((1, h, c), lambda j, hf: (j, hf[0], 0))
    whole = pl.BlockSpec((1, h, c), lambda j, hf: (j, 0, 0))
    return pl.pallas_call(
        body, name=name,
        grid_spec=pltpu.PrefetchScalarGridSpec(
            num_scalar_prefetch=1, grid=(nch,), in_specs=[mine] * n + [whole] * n, out_specs=[whole] * n),
        out_shape=[jax.ShapeDtypeStruct(rv.shape, BF16) for rv in recvs],
        compiler_params=_params(("parallel",)),
    )(half, *ds, *recvs)


def add_chips(ps, recvs, chip, name):
    n = len(ps)
    _, r, c = ps[0].shape

    def body(chip_ref, *refs):
        for k in range(n):
            acc = refs[k][0].astype(F32)
            for f in range(N_CHIPS - 1):
                acc = acc + refs[n + k][f].astype(F32)
            refs[2 * n + k][...] = acc

    return pl.pallas_call(
        body, name=name,
        grid_spec=pltpu.PrefetchScalarGridSpec(
            num_scalar_prefetch=1, grid=(1,),
            in_specs=[pl.BlockSpec((1, r, c), lambda i, ch: (ch[0], 0, 0))] * n +
                     [pl.BlockSpec((N_CHIPS - 1, r, c), lambda i, ch: (0, 0, 0))] * n,
            out_specs=[pl.BlockSpec((r, c), lambda i, ch: (ch[1], 0))] * n),
        out_shape=[jax.ShapeDtypeStruct((2 * r, c), F32)] * n,
        compiler_params=_params(("arbitrary",)),
    )(chip, *ps, *recvs)


def adamw_layers(w, gs, m, v, name):
    b, r, c = w.shape
    tr = max([t for t in range(8, min(r, 512) + 1, 8) if r % t == 0], default=r)

    def body(w_ref, m_ref, v_ref, *rest):
        g_refs, (g_ref, d_ref, nm_ref, nv_ref) = rest[:b], rest[b:]
        layer = pl.program_id(0)
        gv = g_refs[0][...]
        for l in range(1, b):
            gv = jnp.where(layer == l, g_refs[l][...], gv)
        nm = ADAM_B1 * m_ref[0] + (1.0 - ADAM_B1) * gv
        nv = ADAM_B2 * v_ref[0] + (1.0 - ADAM_B2) * (gv * gv)
        m_hat = nm / (1.0 - ADAM_B1 ** ADAM_STEP)
        v_hat = nv / (1.0 - ADAM_B2 ** ADAM_STEP)
        g_ref[0] = gv
        d_ref[0] = -ADAM_LR * (m_hat / (jnp.sqrt(v_hat) + ADAM_EPS) + ADAM_WD * w_ref[0])
        nm_ref[0] = nm
        nv_ref[0] = nv

    nr, tc = r // tr, (c if tr < r else _tile(c, 256))
    steps = nr * (c // tc)
    blk = pl.BlockSpec((1, tr, tc), lambda i, j: (i, j % nr, j // nr))
    g_specs = [pl.BlockSpec((tr, tc), lambda i, j, l=l: (jnp.where(i == l, j % nr, jnp.where(i < l, 0, nr - 1)),
                                                         jnp.where(i == l, j // nr, jnp.where(i < l, 0, c // tc - 1))))
               for l in range(b)]
    return pl.pallas_call(
        body, name=name, grid=(b, steps),
        in_specs=[blk] * 3 + g_specs, out_specs=[blk] * 4,
        out_shape=[jax.ShapeDtypeStruct(w.shape, F32)] * 4,
        compiler_params=_params(("arbitrary", "arbitrary")),
    )(w, m, v, *gs)


def small_allreduce(v, name, after=None):
    r, c = v.shape
    specs, ops = _after(after)

    def body(v_ref, *rest):
        o_ref, buf, send_sems, recv_sems = rest[len(ops):]
        pos = _position()
        me = 4 * pos[0] + 2 * pos[1] + pos[2]
        buf[0] = v_ref[...]
        copies = []
        for f in range(1, 8):
            flip = ((f >> 2) & 1, (f >> 1) & 1, f & 1)
            cp = pltpu.make_async_remote_copy(
                src_ref=v_ref, dst_ref=buf.at[f], send_sem=send_sems.at[f - 1], recv_sem=recv_sems.at[f - 1],
                device_id=_flipped(pos, flip), device_id_type=MESH)
            cp.start()
            copies.append(cp)
        for cp in copies:
            cp.wait()
        acc = buf[me]
        for d in range(1, 8):
            acc = acc + buf[lax.bitwise_xor(me, d)]
        o_ref[...] = acc

    return pl.pallas_call(
        body, name=name,
        in_specs=[pl.BlockSpec(memory_space=pltpu.VMEM)] + specs, out_specs=pl.BlockSpec(memory_space=pltpu.VMEM),
        out_shape=jax.ShapeDtypeStruct((r, c), F32),
        scratch_shapes=[pltpu.VMEM((8, r, c), F32), pltpu.SemaphoreType.DMA((7,)), pltpu.SemaphoreType.DMA((7,))],
    )(v, *ops)


def _all_devices_plan(pos, srcs, lands):
    return [(srcs[0], lands[0].at[f], ((f >> 2) & 1, (f >> 1) & 1, f & 1)) for f in range(1, 8)]


def sum_devices(v, gathered, me, name):
    r, c = v.shape

    def body(me_ref, v_ref, g_ref, o_ref):
        own = v_ref[...]
        acc = None
        for d in range(8):
            slot = lax.bitwise_xor(me_ref[0], d)
            term = jnp.where(slot == 0, own, g_ref[slot])
            acc = term if acc is None else acc + term
        o_ref[...] = acc

    return pl.pallas_call(
        body, name=name,
        grid_spec=pltpu.PrefetchScalarGridSpec(
            num_scalar_prefetch=1, grid=(1,),
            in_specs=[pl.BlockSpec((r, c), lambda i, m: (0, 0)), pl.BlockSpec((8, r, c), lambda i, m: (0, 0, 0))],
            out_specs=pl.BlockSpec((r, c), lambda i, m: (0, 0))),
        out_shape=jax.ShapeDtypeStruct((r, c), F32),
        compiler_params=_params(("arbitrary",)),
    )(me, v, gathered)


_IN_SEGMENTS = ((0, 1544, 128), (128, 1800, 128), (256, 2056, 128), (384, 1672, 128), (512, 1928, 128), (640, 2184, 128),
                (OFF_Z, 0, SSD_WIDTH), (OFF_DT, 1536, SSD_HEADS), (OFF_XBC, 512, CONV_DIM), (OFF_P, 2312, POOL_WIDTH))


def _in_column_map():
    m = np.full((D_INP,), -1, np.int64)
    for at, orig, n in _IN_SEGMENTS:
        cols = np.arange(orig, orig + n)
        m[at:at + n] = (cols // COLS_IN) * SHARD_IN + cols % COLS_IN
    return m


def take_rows(a, idx, name):
    dep, r_in, c = a.shape
    blk = LANES
    n_out, n_in = len(idx) // blk, r_in // blk
    assert len(idx) % blk == 0 and r_in % blk == 0
    sources = [sorted({int(v) // blk for v in idx[blk * i:blk * (i + 1)] if v >= 0}) for i in range(n_out)]
    width = max(len(s) for s in sources)
    table = np.zeros((n_out, width), np.int32)
    for i, s in enumerate(sources):
        spare = [b for b in range(n_in) if b not in s][:width - len(s)]
        table[i] = s + spare

    def body(tbl_ref, idx_ref, *refs):
        in_refs, o_ref = refs[:width], refs[width]
        i = pl.program_id(1)
        src = idx_ref[...]
        acc = jnp.zeros((blk, c), F32)
        for k in range(width):
            pick = (src == tbl_ref[i, k] * blk + _iota2((blk, blk), 1)).astype(BF16)
            acc = acc + _nn(pick, in_refs[k][0])
        o_ref[0] = acc.astype(BF16)

    return pl.pallas_call(
        body, name=name,
        grid_spec=pltpu.PrefetchScalarGridSpec(
            num_scalar_prefetch=1, grid=(dep, n_out),
            in_specs=[pl.BlockSpec((blk, 1), lambda l, i, t: (i, 0))] +
                     [pl.BlockSpec((1, blk, c), lambda l, i, t, k=k: (l, t[i, k], 0)) for k in range(width)],
            out_specs=pl.BlockSpec((1, blk, c), lambda l, i, t: (l, i, 0))),
        out_shape=jax.ShapeDtypeStruct((dep, len(idx), c), BF16),
        compiler_params=_params(("parallel", "parallel")),
    )(jnp.asarray(table), jnp.asarray(np.asarray(idx, np.int32).reshape(-1, 1)), *([a] * width))


def _in_weight_layout(staged):
    return take_rows(staged, _in_column_map(), "w_in_layout")


def _in_gradient_layout(dwt):
    fwd = _in_column_map()
    inv = np.full((N_CHIPS * SHARD_IN,), -1, np.int64)
    inv[fwd[fwd >= 0]] = np.nonzero(fwd >= 0)[0]
    return take_rows(dwt, inv, "dw_in_layout")


SMALL_NAMES = ("norm1_w", "conv_w", "conv_b", "dt_bias", "a_log", "d_skip", "ssd_norm_w", "pool_w", "pool_b",
               "pool_scale", "norm2_w", "final_norm_w")
SMALL_ROWS = 104


def _pack_small(parts):
    flat = jnp.concatenate([p.reshape(-1) for p in parts])
    return jnp.pad(flat, (0, SMALL_ROWS * D_MODEL - flat.shape[0])).reshape(SMALL_ROWS, D_MODEL)


def _unpack_small(flat, shapes):
    flat = flat.reshape(-1)
    out, at = [], 0
    for shp in shapes:
        n = int(np.prod(shp))
        out.append(flat[at:at + n].reshape(shp))
        at += n
    return out


def kernel(x, norm1_w, w_in, conv_w, conv_b, dt_bias, a_log, d_skip, ssd_norm_w, pool_w, pool_b, pool_scale, w_out, norm2_w, w_gate, w_up, w_down, final_norm_w, loss_target, m_norm1_w, m_w_in, m_conv_w, m_conv_b, m_dt_bias, m_a_log, m_d_skip, m_ssd_norm_w, m_pool_w, m_pool_b, m_pool_scale, m_w_out, m_norm2_w, m_w_gate, m_w_up, m_w_down, m_final_norm_w, v_norm1_w, v_w_in, v_conv_w, v_conv_b, v_dt_bias, v_a_log, v_d_skip, v_ssd_norm_w, v_pool_w, v_pool_b, v_pool_scale, v_w_out, v_norm2_w, v_w_gate, v_w_up, v_w_down, v_final_norm_w):
    px, py, pc = _position()
    chip = 2 * px + py
    chip_arr = jnp.reshape(chip, (1,)).astype(jnp.int32)
    half_arr = jnp.reshape(pc, (1,)).astype(jnp.int32)

    def layer_shards(l):
        w_in_t = jnp.pad(jnp.swapaxes(w_in[l], 0, 1).astype(BF16), ((0, SHARD_IN - COLS_IN), (0, 0)))
        return [w_in_t, w_out[l].astype(BF16), jnp.swapaxes(w_gate[l], 0, 1).astype(BF16),
                jnp.swapaxes(w_up[l], 0, 1).astype(BF16), w_down[l].astype(BF16)]

    shards0 = layer_shards(0)
    head = gather_ici_start(shards0[:1])
    over_ici = {}

    def pass_on(l, after):
        own, arrived = exchange_wait("gather_ici_wait", over_ici[l], after, _gather_ici_plan)
        swap = gather_d2d_start(own, arrived)
        tokens = [swap[4]]
        if l + 1 < DEPTH:
            over_ici[l + 1] = gather_ici_start(layer_shards(l + 1), after=swap[4])
            tokens.append(over_ici[l + 1][4])
        return swap, tokens

    def weights_of(swap, after):
        _, (w_in_st, w_out_l, w_gate_t, w_up_t, w_down_l) = exchange_wait("gather_d2d_wait", swap, after, _gather_d2d_plan)
        return _in_weight_layout(w_in_st[None])[0], w_out_l, w_gate_t, w_up_t, w_down_l

    pad_heads = lambda v: jnp.pad(v, ((0, 0), (0, LANES - SSD_HEADS)))[:, None, :]
    dtb, alog = pad_heads(dt_bias), pad_heads(a_log)
    dskip_x = jnp.repeat(d_skip, HEAD_DIM, axis=1)[:, None, :]
    eye = jnp.eye(len(POOL_WINDOWS), dtype=F32)
    wbd = (pool_w[:, :, :, None, :] * eye[None, :, None, :, None]).reshape(DEPTH, POOL_WIDTH, POOL_WIDTH).astype(BF16)
    pool_b2 = pool_b.reshape(DEPTH, 1, POOL_WIDTH)
    cw_cols = lax.dynamic_update_slice(jnp.zeros((DEPTH, CONV_WIDTH, CONV_DIM), F32), conv_w,
                                       (0, 0, chip * (CONV_DIM // N_CHIPS)))
    cw_cols = jnp.where(pc == 0, cw_cols, 0.0)
    cw_rows = (DEPTH * CONV_WIDTH * CONV_DIM) // D_MODEL
    conv_w_f = small_allreduce(jnp.pad(cw_cols.reshape(cw_rows, D_MODEL), ((0, 8), (0, 0))), "gather_conv_w")
    conv_w_f = conv_w_f[:cw_rows].reshape(DEPTH, CONV_WIDTH, CONV_DIM)
    cw8 = jnp.pad(conv_w_f, ((0, 0), (0, 8 - CONV_WIDTH), (0, 0)))

    h = x[0]
    saved, weights = [], []
    own, arrived = exchange_wait("gather_ici_wait", head, head[4], _gather_ici_plan)
    head = gather_d2d_start(own, arrived)
    over_ici[0] = gather_ici_start(shards0[1:], after=head[4])
    w_in_f = _in_weight_layout(exchange_wait("gather_d2d_wait", head, [head[4], over_ici[0][4]], _gather_d2d_plan)[1][0][None])[0]
    for l in range(DEPTH):
        if l > 0:
            w_in_f, w_out_f, w_gate_t, w_up_t, w_down_f = weights[l]
        proj = rms_matmul(h, norm1_w[l][None], w_in_f, "in_proj")
        xc = conv_forward(proj, cw8[l], conv_b[l][None], "conv_fwd")
        y_all, ycore, states = ssd_forward(proj, xc, dtb[l], alog[l], dskip_x[l], ssd_norm_w[l][None], "ssd_fwd")
        y_all, o_sb = sb_forward(proj, y_all, "sb_fwd")
        swap, tokens = pass_on(l + 1 if l else 0, o_sb) if l + 1 < DEPTH else (None, None)
        y_all = pool_forward(proj, wbd[l], pool_b2[l], pool_scale[l][None], y_all, "pool_fwd", after=tokens)
        if l == 0:
            w_out_f, w_gate_t, w_up_t, w_down_f = exchange_wait("gather_d2d_wait", swap, y_all, _gather_d2d_plan)[1]
            weights.append((w_in_f, w_out_f, w_gate_t, w_up_t, w_down_f))
        x1 = matmul_residual(y_all, w_out_f, h, "out_proj")
        x2, g, u = ffn_forward(x1, norm2_w[l][None], w_gate_t, w_up_t, w_down_f, "ffn_fwd")
        if l == 0:
            swap, tokens = pass_on(1, x2)
            weights.append(weights_of(swap, tokens))
        elif swap is not None:
            weights.append(weights_of(swap, x2))
        saved.append((h, proj, xc, ycore, states, o_sb, y_all, x1, g, u))
        h = x2

    loss_part, dx, dxb, d_final = loss_head(h, final_norm_w[None], loss_target[0], "loss_head")
    loss = lax.psum(loss_part[0, 0], ("x", "y", "c"))

    small = {n: [None] * DEPTH for n in SMALL_NAMES if n != "final_norm_w"}
    chip_half = jnp.concatenate([chip_arr, half_arr])
    reduced = {}
    d2d = ici = early = None

    def add_cores(d2d, after):
        mine, theirs = exchange_wait("reduce_d2d_wait", d2d[1], after, _reduce_d2d_plan)
        return d2d[0], reduce_ici_start(_by_shape(lambda ds, ts: add_halves(ds, ts, half_arr, "reduce_add_halves"), mine, theirs))

    def add_all(ici, after):
        sums, theirs = exchange_wait("reduce_ici_wait", ici[1], after, _reduce_ici_plan)
        return ici[0], reduce_swap_start(_by_shape(lambda ps, ts: add_chips(ps, ts, chip_half, "reduce_add_chips"), sums, theirs))

    def finish(swap, after):
        reduced[swap[0]] = exchange_wait("reduce_swap_wait", swap[1], after, _reduce_swap_plan)[1]

    swaps = []
    for l in reversed(range(DEPTH)):
        xin, proj, xc, ycore, states, o_sb, y_all, x1, g, u = saved[l]
        w_in_f, w_out_f, w_gate_t, w_up_t, w_down_f = weights[l]
        dg, du, act = ffn_backward_act(dxb, g, u, w_down_f, "ffn_bwd_act", after=None if d2d is None else d2d[1][4])
        dx1, dx1b, h2, dn2 = rms_backward([dg, du], [w_gate_t, w_up_t], x1, norm2_w[l][None], dx, "ffn_bwd_norm", 256)
        if d2d is not None:
            ici = add_cores(d2d, dx1b)
        dyall = matmul_nt(dx1b, w_out_f, "out_proj_bwd", after=None if ici is None else ici[1][4])
        dw_down = matmul_tn(act, dxb, "dw_down")
        dw_gate = matmul_tn(dg, h2, "dw_gate")
        dw_up = matmul_tn(du, h2, "dw_up")
        dw_out = matmul_tn(y_all, dx1b, "dw_out")
        late = [dw.reshape(N_CHIPS, r, D_MODEL) for dw, r in zip((dw_out, dw_gate, dw_up, dw_down), SHARD_ROWS[1:])]
        if l == 0:
            early = ("0 late", reduce_d2d_start(late))
        dxc, dproj, dsn, ddsk, ddtb, dalog = ssd_backward(proj, xc, ycore, dyall, states, dtb[l], alog[l],
                                                          dskip_x[l], ssd_norm_w[l][None], "ssd_bwd")
        dproj, dcw, dcb = conv_backward(proj, dxc, cw8[l], conv_b[l][None], dproj, "conv_bwd",
                                        after=None if early is None else early[1][4])
        if early is not None:
            early = add_cores(early, dproj)
        dproj = sb_backward(proj, o_sb, dyall, dproj, "sb_bwd", after=None if early is None else early[1][4])
        dproj, dwbd, dpb, dps = pool_backward(proj, dyall, wbd[l], pool_b2[l], pool_scale[l][None], dproj, "pool_bwd")
        if ici is not None:
            swaps.append(add_all(ici, dproj))
            ici = None
        dx, dxb, h1, dn1 = rms_backward([dproj], [w_in_f], xin, norm1_w[l][None], dx1, "in_proj_bwd", 256,
                                        after=swaps[-1][1][4] if swaps else None)
        dw_in = _in_gradient_layout(matmul_tn(dproj, h1, "dw_in")[None])[0].reshape(N_CHIPS, SHARD_IN, D_MODEL)
        d2d = (l, reduce_d2d_start([dw_in] if l == 0 else [dw_in] + late))
        small["norm1_w"][l] = dn1[0]
        small["conv_w"][l] = dcw[:CONV_WIDTH]
        small["conv_b"][l] = dcb[0]
        small["dt_bias"][l] = ddtb[0, :SSD_HEADS]
        small["a_log"][l] = dalog[0, :SSD_HEADS]
        small["d_skip"][l] = ddsk.reshape(SSD_HEADS, HEAD_DIM).sum(axis=1)
        small["ssd_norm_w"][l] = dsn[0]
        small["pool_w"][l] = jnp.stack([dwbd[64 * k:64 * k + 64, 64 * k:64 * k + 64] for k in range(len(POOL_WINDOWS))])
        small["pool_b"][l] = dpb.reshape(len(POOL_WINDOWS), -1)
        small["pool_scale"][l] = dps[0]
        small["norm2_w"][l] = dn2[0]
    grad_x = dx[None]

    ici = add_cores(d2d, d2d[1][4])
    small_parts = [d_final if n == "final_norm_w" else jnp.stack(small[n]) for n in SMALL_NAMES]
    small_start = exchange_start("reduce_small_start", [_pack_small(small_parts)],
                                 [_landing((8, SMALL_ROWS, D_MODEL), F32)], 7, _all_devices_plan, after=ici[1][4])
    swaps.append(add_all(early, small_start[4]))
    swaps.append(add_all(ici, swaps[-1][1][4]))
    for swap in swaps:
        finish(swap, swaps[-1][1][4])
    (small_own,), (small_all,) = exchange_wait("reduce_small_wait", small_start, reduced[0][0], _all_devices_plan)
    small_sum = sum_devices(small_own, small_all, jnp.reshape(4 * px + 2 * py + pc, (1,)).astype(jnp.int32), "reduce_small_sum")
    reduced[0] = reduced[0] + reduced["0 late"]
    g_big = {n: [reduced[l][k] for l in range(DEPTH)] for k, n in enumerate(("w_in", "w_out", "w_gate", "w_up", "w_down"))}
    g_big["w_in"] = [gl[:COLS_IN] for gl in g_big["w_in"]]
    transposed = ("w_in", "w_gate", "w_up")

    g_small = dict(zip(SMALL_NAMES, _unpack_small(small_sum, [p.shape for p in small_parts])))
    g_small["final_norm_w"] = g_small["final_norm_w"].reshape(final_norm_w.shape)
    g_small["conv_w"] = lax.dynamic_slice_in_dim(g_small["conv_w"], chip * (CONV_DIM // N_CHIPS), CONV_DIM // N_CHIPS, axis=2)

    given = dict(norm1_w=(norm1_w, m_norm1_w, v_norm1_w), w_in=(w_in, m_w_in, v_w_in), conv_w=(conv_w, m_conv_w, v_conv_w),
                 conv_b=(conv_b, m_conv_b, v_conv_b), dt_bias=(dt_bias, m_dt_bias, v_dt_bias), a_log=(a_log, m_a_log, v_a_log),
                 d_skip=(d_skip, m_d_skip, v_d_skip), ssd_norm_w=(ssd_norm_w, m_ssd_norm_w, v_ssd_norm_w),
                 pool_w=(pool_w, m_pool_w, v_pool_w), pool_b=(pool_b, m_pool_b, v_pool_b),
                 pool_scale=(pool_scale, m_pool_scale, v_pool_scale), w_out=(w_out, m_w_out, v_w_out),
                 norm2_w=(norm2_w, m_norm2_w, v_norm2_w), w_gate=(w_gate, m_w_gate, v_w_gate), w_up=(w_up, m_w_up, v_w_up),
                 w_down=(w_down, m_w_down, v_w_down), final_norm_w=(final_norm_w, m_final_norm_w, v_final_norm_w))
    order = ("norm1_w", "w_in", "conv_w", "conv_b", "dt_bias", "a_log", "d_skip", "ssd_norm_w", "pool_w", "pool_b",
             "pool_scale", "w_out", "norm2_w", "w_gate", "w_up", "w_down", "final_norm_w")
    grads = dict(g_small)
    results = {}
    for n in ("w_in", "w_out", "w_gate", "w_up", "w_down"):
        w, m, v = given[n]
        if n in transposed:
            out = adamw_layers(jnp.swapaxes(w, 1, 2), g_big[n], jnp.swapaxes(m, 1, 2), jnp.swapaxes(v, 1, 2), "adamw_" + n)
            out = [jnp.swapaxes(o, 1, 2) for o in out]
        else:
            out = adamw_layers(w, g_big[n], m, v, "adamw_" + n)
        grads[n], results[n] = out[0], tuple(out[1:])
    small_shapes = [given[n][0].shape for n in SMALL_NAMES]
    packed = [_pack_small([given[n][k] for n in SMALL_NAMES])[None] for k in range(3)]
    packed_g = _pack_small([grads[n] for n in SMALL_NAMES])[None]
    small_out = adamw(packed[0], packed_g, packed[1], packed[2], "adamw_small")
    small_out = [_unpack_small(o[0], small_shapes) for o in small_out]
    for i, n in enumerate(SMALL_NAMES):
        results[n] = tuple(small_out[k][i] for k in range(3))

    return (loss, grad_x, *[grads[n] for n in order], *[results[n][0] for n in order],
            *[results[n][1] for n in order], *[results[n][2] for n in order])
```

```python
import numpy as np
import jax
import jax.numpy as jnp
from jax import lax
from jax.experimental import pallas as pl
from jax.experimental.pallas import tpu as pltpu

F32 = jnp.float32
BF16 = jnp.bfloat16
MESH = pl.DeviceIdType.MESH
ANY = pl.BlockSpec(memory_space=pl.ANY)

D_MODEL = 1024
DEPTH = 4
EPS = 1e-6
SSD_WIDTH = 512
SSD_HEADS = 8
HEAD_DIM = 64
D_STATE = 128
CHUNK = 128
CONV_WIDTH = 4
CONV_DIM = 1024
SB_WIDTH = 256
POOL_WIDTH = 256
POOL_WINDOWS = (2, 4, 8, 16)
D_FF = 2816
D_IN = 2568
N_CHIPS = 4
OFF_QKV, OFF_Z, OFF_DT, OFF_XBC, OFF_P = 0, 768, 1280, 1536, 2560
D_INP = 2816
ZDT = 768
SHARD_IN, SHARD_OUT, SHARD_FF = 672, 256, 704
COLS_IN = 642
ADAM_LR, ADAM_B1, ADAM_B2, ADAM_EPS, ADAM_WD, ADAM_STEP = 0.001, 0.9, 0.999, 1e-08, 0.01, 10
LANES = 128
VMEM_LIMIT = 56 * 1024 * 1024


def _params(sem=None):
    return pltpu.CompilerParams(dimension_semantics=sem, vmem_limit_bytes=VMEM_LIMIT)


def _tile(n, cap):
    best = None
    for t in range(LANES, min(n, cap) + 1, LANES):
        if n % t == 0:
            best = t
    assert best is not None, (n, cap)
    return best


def _nt(a, b):
    return lax.dot_general(a, b, (((1,), (1,)), ((), ())), preferred_element_type=F32)


def _tn(a, b):
    return lax.dot_general(a, b, (((0,), (0,)), ((), ())), preferred_element_type=F32)


def _nn(a, b):
    return jnp.dot(a, b, preferred_element_type=F32)


def _split_dot(a, b_exact, terms=3, dot=_nn):
    acc = None
    rest = a
    for _ in range(terms):
        hi = rest.astype(BF16)
        part = dot(hi, b_exact)
        acc = part if acc is None else acc + part
        rest = rest - hi.astype(F32)
    return acc


def _split_dot_left(a_exact, b, terms=3):
    acc = None
    rest = b
    for _ in range(terms):
        hi = rest.astype(BF16)
        part = _nn(a_exact, hi)
        acc = part if acc is None else acc + part
        rest = rest - hi.astype(F32)
    return acc


def _sigmoid(x):
    return 1.0 / (1.0 + jnp.exp(-x))


def _softplus(x):
    return jnp.maximum(x, 0.0) + jnp.log(1.0 + jnp.exp(-jnp.abs(x)))


def _iota2(shape, dim):
    return lax.broadcasted_iota(jnp.int32, shape, dim)


def _after(after):
    ops = [] if after is None else list(after) if isinstance(after, (list, tuple)) else [after]
    return [ANY] * len(ops), ops


def rms_matmul(x, nw, wt, name, after=None):
    s, d = x.shape
    n = wt.shape[0]
    tm, tn = _tile(s, 512), _tile(n, 2816)
    specs, ops = _after(after)

    def body(x_ref, nw_ref, w_ref, *rest):
        o_ref, h_ref = rest[len(ops):]

        @pl.when(pl.program_id(1) == 0)
        def _():
            xv = x_ref[...]
            r = lax.rsqrt(jnp.mean(xv * xv, axis=-1, keepdims=True) + EPS)
            h_ref[...] = (xv * r * nw_ref[...]).astype(BF16)
        o_ref[...] = _nt(h_ref[...], w_ref[...])

    return pl.pallas_call(
        body, name=name, grid=(s // tm, n // tn),
        in_specs=[pl.BlockSpec((tm, d), lambda i, j: (i, 0)), pl.BlockSpec((1, d), lambda i, j: (0, 0)),
                  pl.BlockSpec((tn, d), lambda i, j: (j, 0))] + specs,
        out_specs=pl.BlockSpec((tm, tn), lambda i, j: (i, j)),
        out_shape=jax.ShapeDtypeStruct((s, n), F32),
        scratch_shapes=[pltpu.VMEM((tm, d), BF16)],
        compiler_params=_params(("parallel", "arbitrary")),
    )(x, nw, wt, *ops)


def matmul_residual(a, w, res, name):
    s, k = a.shape
    n = w.shape[1]
    tm, tn = _tile(s, 512), _tile(n, 1024)

    def body(a_ref, w_ref, r_ref, o_ref):
        o_ref[...] = r_ref[...] + _nn(a_ref[...], w_ref[...])

    return pl.pallas_call(
        body, name=name, grid=(s // tm, n // tn),
        in_specs=[pl.BlockSpec((tm, k), lambda i, j: (i, 0)), pl.BlockSpec((k, tn), lambda i, j: (0, j)),
                  pl.BlockSpec((tm, tn), lambda i, j: (i, j))],
        out_specs=pl.BlockSpec((tm, tn), lambda i, j: (i, j)),
        out_shape=jax.ShapeDtypeStruct((s, n), F32),
        compiler_params=_params(("parallel", "parallel")),
    )(a, w, res)


def matmul_nt(a, w, name, out_dtype=F32, after=None):
    s, n = a.shape
    k = w.shape[0]
    tm, tk = _tile(s, 512), _tile(k, 1024)
    specs, ops = _after(after)

    def body(a_ref, w_ref, *rest):
        rest[-1][...] = _nt(a_ref[...], w_ref[...]).astype(out_dtype)

    return pl.pallas_call(
        body, name=name, grid=(s // tm, k // tk),
        in_specs=[pl.BlockSpec((tm, n), lambda i, j: (i, 0)), pl.BlockSpec((tk, n), lambda i, j: (j, 0))] + specs,
        out_specs=pl.BlockSpec((tm, tk), lambda i, j: (i, j)),
        out_shape=jax.ShapeDtypeStruct((s, k), out_dtype),
        compiler_params=_params(("parallel", "parallel")),
    )(a, w, *ops)


def matmul_tn(a, b, name, after=None):
    s, m = a.shape
    n = b.shape[1]
    tm, tn = _tile(m, 512), _tile(n, 1024)

    def body(a_ref, b_ref, *rest):
        rest[-1][...] = _tn(a_ref[...], b_ref[...]).astype(BF16)

    specs, ops = _after(after)
    return pl.pallas_call(
        body, name=name, grid=(m // tm, n // tn),
        in_specs=[pl.BlockSpec((s, tm), lambda i, j: (0, i)), pl.BlockSpec((s, tn), lambda i, j: (0, j))] + specs,
        out_specs=pl.BlockSpec((tm, tn), lambda i, j: (i, j)),
        out_shape=jax.ShapeDtypeStruct((m, n), BF16),
        compiler_params=_params(("parallel", "parallel")),
    )(a, b, *ops)


def ffn_forward(x1, nw, wgt, wut, wd, name):
    s, d = x1.shape
    f = wgt.shape[0]
    tm, tf = _tile(s, 1024), _tile(f, 256)

    def body(x_ref, nw_ref, wg_ref, wu_ref, wd_ref, o_ref, g_ref, u_ref, h_ref, acc_ref):
        j = pl.program_id(1)

        @pl.when(j == 0)
        def _():
            xv = x_ref[...]
            r = lax.rsqrt(jnp.mean(xv * xv, axis=-1, keepdims=True) + EPS)
            h_ref[...] = (xv * r * nw_ref[...]).astype(BF16)
            acc_ref[...] = xv

        h = h_ref[...]
        g = _nt(h, wg_ref[...])
        u = _nt(h, wu_ref[...])
        g_ref[...] = g.astype(BF16)
        u_ref[...] = u.astype(BF16)
        a = (g * _sigmoid(g) * u).astype(BF16)
        acc_ref[...] += _nn(a, wd_ref[...])

        @pl.when(j == pl.num_programs(1) - 1)
        def _():
            o_ref[...] = acc_ref[...]

    wblk = pl.BlockSpec((tf, d), lambda i, j: (j, 0))
    return pl.pallas_call(
        body, name=name, grid=(s // tm, f // tf),
        in_specs=[pl.BlockSpec((tm, d), lambda i, j: (i, 0)), pl.BlockSpec((1, d), lambda i, j: (0, 0)), wblk, wblk, wblk],
        out_specs=[pl.BlockSpec((tm, d), lambda i, j: (i, 0)), pl.BlockSpec((tm, tf), lambda i, j: (i, j)),
                   pl.BlockSpec((tm, tf), lambda i, j: (i, j))],
        out_shape=[jax.ShapeDtypeStruct((s, d), F32), jax.ShapeDtypeStruct((s, f), BF16),
                   jax.ShapeDtypeStruct((s, f), BF16)],
        scratch_shapes=[pltpu.VMEM((tm, d), BF16), pltpu.VMEM((tm, d), F32)],
        compiler_params=_params(("parallel", "arbitrary")),
    )(x1, nw, wgt, wut, wd)


def ffn_backward_act(dx2, g, u, wd, name, after=None):
    s, d = dx2.shape
    f = wd.shape[0]
    tm, tf = _tile(s, 256), _tile(f, 2816)
    specs, ops = _after(after)

    def body(dx_ref, g_ref, u_ref, wd_ref, *rest):
        dg_ref, du_ref, a_ref = rest[len(ops):]
        da = _nt(dx_ref[...], wd_ref[...])
        gv = g_ref[...].astype(F32)
        uv = u_ref[...].astype(F32)
        sg = _sigmoid(gv)
        silu = gv * sg
        dg_ref[...] = (da * uv * (sg * (1.0 + gv * (1.0 - sg)))).astype(BF16)
        du_ref[...] = (da * silu).astype(BF16)
        a_ref[...] = (silu * uv).astype(BF16)

    blk = pl.BlockSpec((tm, tf), lambda i, j: (i, j))
    return pl.pallas_call(
        body, name=name, grid=(s // tm, f // tf),
        in_specs=[pl.BlockSpec((tm, d), lambda i, j: (i, 0)), blk, blk, pl.BlockSpec((tf, d), lambda i, j: (j, 0))] + specs,
        out_specs=[blk, blk, blk],
        out_shape=[jax.ShapeDtypeStruct((s, f), BF16)] * 3,
        compiler_params=_params(("parallel", "parallel")),
    )(dx2, g, u, wd, *ops)


def rms_backward(dzs, wts, x, nw, dres, name, tm, after=None):
    s, d = x.shape
    nz = len(dzs)
    specs, ops = _after(after)

    def body(*refs):
        dz_refs, w_refs = refs[:nz], refs[nz:2 * nz]
        x_ref, nw_ref, dres_ref = refs[2 * nz:2 * nz + 3]
        dx_ref, dxb_ref, h_ref, dnw_ref = refs[2 * nz + 3 + len(ops):]
        dh = _nn(dz_refs[0][...], w_refs[0][...])
        for k in range(1, nz):
            dh = dh + _nn(dz_refs[k][...], w_refs[k][...])
        xv = x_ref[...]
        r = lax.rsqrt(jnp.mean(xv * xv, axis=-1, keepdims=True) + EPS)
        xhat = xv * r
        nwv = nw_ref[...]
        h_ref[...] = (xhat * nwv).astype(BF16)

        @pl.when(pl.program_id(0) == 0)
        def _():
            dnw_ref[...] = jnp.zeros_like(dnw_ref)

        dnw_ref[...] += jnp.sum(dh * xhat, axis=0, keepdims=True)
        gdh = dh * nwv
        dx = dres_ref[...] + r * (gdh - xhat * jnp.mean(gdh * xhat, axis=-1, keepdims=True))
        dx_ref[...] = dx
        dxb_ref[...] = dx.astype(BF16)

    row = pl.BlockSpec((tm, d), lambda i: (i, 0))
    in_specs = [pl.BlockSpec((tm, dz.shape[1]), lambda i: (i, 0)) for dz in dzs]
    in_specs += [pl.BlockSpec(w.shape, lambda i: (0, 0)) for w in wts]
    in_specs += [row, pl.BlockSpec((1, d), lambda i: (0, 0)), row] + specs
    return pl.pallas_call(
        body, name=name, grid=(s // tm,),
        in_specs=in_specs,
        out_specs=[row, row, row, pl.BlockSpec((1, d), lambda i: (0, 0))],
        out_shape=[jax.ShapeDtypeStruct((s, d), F32), jax.ShapeDtypeStruct((s, d), BF16),
                   jax.ShapeDtypeStruct((s, d), BF16), jax.ShapeDtypeStruct((1, d), F32)],
        compiler_params=_params(("arbitrary",)),
    )(*dzs, *wts, x, nw, dres, *ops)


def loss_head(x, nw, target, name):
    s, d = x.shape
    tm = _tile(s, 512)

    def body(x_ref, nw_ref, t_ref, loss_ref, dx_ref, dxb_ref, dnw_ref):
        xv = x_ref[...]
        r = lax.rsqrt(jnp.mean(xv * xv, axis=-1, keepdims=True) + EPS)
        xhat = xv * r
        nwv = nw_ref[...]
        err = xhat * nwv - t_ref[...]

        @pl.when(pl.program_id(0) == 0)
        def _():
            dnw_ref[...] = jnp.zeros_like(dnw_ref)
            loss_ref[...] = jnp.zeros_like(loss_ref)

        part = jnp.sum(jnp.sum(err * err, axis=-1, keepdims=True), axis=0, keepdims=True) * (0.5 / d)
        loss_ref[...] += jnp.broadcast_to(part, loss_ref.shape)
        dout = err * (1.0 / d)
        dnw_ref[...] += jnp.sum(dout * xhat, axis=0, keepdims=True)
        gdh = dout * nwv
        dx = r * (gdh - xhat * jnp.mean(gdh * xhat, axis=-1, keepdims=True))
        dx_ref[...] = dx
        dxb_ref[...] = dx.astype(BF16)

    row = pl.BlockSpec((tm, d), lambda i: (i, 0))
    return pl.pallas_call(
        body, name=name, grid=(s // tm,),
        in_specs=[row, pl.BlockSpec((1, d), lambda i: (0, 0)), row],
        out_specs=[pl.BlockSpec((1, LANES), lambda i: (0, 0)), row, row, pl.BlockSpec((1, d), lambda i: (0, 0))],
        out_shape=[jax.ShapeDtypeStruct((1, LANES), F32), jax.ShapeDtypeStruct((s, d), F32),
                   jax.ShapeDtypeStruct((s, d), BF16), jax.ShapeDtypeStruct((1, d), F32)],
        compiler_params=_params(("arbitrary",)),
    )(x, nw, target)


def _shift_down(x, k):
    return jnp.where(_iota2(x.shape, 0) >= k, pltpu.roll(x, k, axis=0), 0.0)


def _shift_up(x, k):
    s = x.shape[0]
    return jnp.where(_iota2(x.shape, 0) < s - k, pltpu.roll(x, s - k, axis=0), 0.0)


CONV_TILE = 256


def conv_forward(proj, cw, cb, name):
    s = proj.shape[0]
    tn = CONV_TILE
    off = OFF_XBC // tn

    def body(u_ref, w_ref, b_ref, o_ref):
        u = u_ref[...]
        pre = b_ref[...] + w_ref[CONV_WIDTH - 1:CONV_WIDTH, :] * u
        for i in range(CONV_WIDTH - 1):
            pre = pre + w_ref[i:i + 1, :] * _shift_down(u, CONV_WIDTH - 1 - i)
        o_ref[...] = pre * _sigmoid(pre)

    return pl.pallas_call(
        body, name=name, grid=(CONV_DIM // tn,),
        in_specs=[pl.BlockSpec((s, tn), lambda j: (0, off + j)), pl.BlockSpec((8, tn), lambda j: (0, j)),
                  pl.BlockSpec((1, tn), lambda j: (0, j))],
        out_specs=pl.BlockSpec((s, tn), lambda j: (0, j)),
        out_shape=jax.ShapeDtypeStruct((s, CONV_DIM), F32),
        compiler_params=_params(("parallel",)),
    )(proj, cw, cb)


def conv_backward(proj, dxc, cw, cb, dproj, name, after=None):
    s = proj.shape[0]
    tn = CONV_TILE
    off = OFF_XBC // tn

    specs, ops = _after(after)

    def body(u_ref, d_ref, w_ref, b_ref, *rest):
        du_ref, dw_ref, db_ref = rest[-3:]
        u = u_ref[...]
        shifted = [_shift_down(u, CONV_WIDTH - 1 - i) for i in range(CONV_WIDTH - 1)] + [u]
        pre = b_ref[...] + w_ref[CONV_WIDTH - 1:CONV_WIDTH, :] * u
        for i in range(CONV_WIDTH - 1):
            pre = pre + w_ref[i:i + 1, :] * shifted[i]
        sg = _sigmoid(pre)
        dpre = d_ref[...] * (sg * (1.0 + pre * (1.0 - sg)))
        du = w_ref[CONV_WIDTH - 1:CONV_WIDTH, :] * dpre
        for i in range(CONV_WIDTH - 1):
            du = du + w_ref[i:i + 1, :] * _shift_up(dpre, CONV_WIDTH - 1 - i)
        du_ref[...] = du.astype(BF16)
        rows = [jnp.sum(dpre * shifted[i], axis=0, keepdims=True) for i in range(CONV_WIDTH)]
        rows.append(jnp.zeros((8 - CONV_WIDTH, tn), F32))
        dw_ref[...] = jnp.concatenate(rows, axis=0)
        db_ref[...] = jnp.sum(dpre, axis=0, keepdims=True)

    return pl.pallas_call(
        body, name=name, grid=(CONV_DIM // tn,),
        in_specs=[pl.BlockSpec((s, tn), lambda j: (0, off + j)), pl.BlockSpec((s, tn), lambda j: (0, j)),
                  pl.BlockSpec((8, tn), lambda j: (0, j)), pl.BlockSpec((1, tn), lambda j: (0, j)), ANY] + specs,
        out_specs=[pl.BlockSpec((s, tn), lambda j: (0, off + j)), pl.BlockSpec((8, tn), lambda j: (0, j)),
                   pl.BlockSpec((1, tn), lambda j: (0, j))],
        out_shape=[jax.ShapeDtypeStruct(dproj.shape, BF16), jax.ShapeDtypeStruct((8, CONV_DIM), F32),
                   jax.ShapeDtypeStruct((1, CONV_DIM), F32)],
        input_output_aliases={4: 0},
        compiler_params=_params(("parallel",)),
    )(proj, dxc, cw, cb, dproj, *ops)


def _pool_lane_window(shape):
    grp = _iota2(shape, 1) // (POOL_WIDTH // len(POOL_WINDOWS))
    win = jnp.full(shape, POOL_WINDOWS[-1], jnp.int32)
    for gi in range(len(POOL_WINDOWS) - 2, -1, -1):
        win = jnp.where(grp == gi, POOL_WINDOWS[gi], win)
    return grp, win


def _pool_select(grp, sums):
    out = sums[-1]
    for gi in range(len(sums) - 2, -1, -1):
        out = jnp.where(grp == gi, sums[gi], out)
    return out


def _pool_pooled(p):
    grp, win = _pool_lane_window(p.shape)
    inv_count = 1.0 / jnp.minimum(_iota2(p.shape, 0) + 1, win).astype(F32)
    sums, acc, k = [], p, 1
    for _ in POOL_WINDOWS:
        acc = acc + _shift_down(acc, k)
        sums.append(acc)
        k *= 2
    return _pool_select(grp, sums) * inv_count - p, grp, inv_count


def pool_forward(proj, wbd, pb, ps, y_all, name, after=None):
    s = proj.shape[0]
    specs, ops = _after(after)

    def body(p_ref, w_ref, b_ref, s_ref, *rest):
        o_ref = rest[-1]
        pooled, _, _ = _pool_pooled(p_ref[...])
        mixed = _nn(pooled.astype(BF16), w_ref[...]) + b_ref[...]
        o_ref[...] = (mixed * s_ref[...]).astype(BF16)

    vec = pl.BlockSpec((1, POOL_WIDTH), lambda j: (0, 0))
    return pl.pallas_call(
        body, name=name, grid=(1,),
        in_specs=[pl.BlockSpec((s, POOL_WIDTH), lambda j: (0, OFF_P // POOL_WIDTH)),
                  pl.BlockSpec((POOL_WIDTH, POOL_WIDTH), lambda j: (0, 0)), vec, vec, ANY] + specs,
        out_specs=pl.BlockSpec((s, POOL_WIDTH), lambda j: (0, (SSD_WIDTH + SB_WIDTH) // POOL_WIDTH)),
        out_shape=jax.ShapeDtypeStruct(y_all.shape, BF16),
        input_output_aliases={4: 0},
        compiler_params=_params(("arbitrary",)),
    )(proj, wbd, pb, ps, y_all, *ops)


def pool_backward(proj, dyall, wbd, pb, ps, dproj, name):
    s = proj.shape[0]

    def body(p_ref, dy_ref, w_ref, b_ref, s_ref, _, dp_ref, dw_ref, db_ref, ds_ref):
        pooled, grp, inv_count = _pool_pooled(p_ref[...])
        pooled_b = pooled.astype(BF16)
        mixed = _nn(pooled_b, w_ref[...]) + b_ref[...]
        dy = dy_ref[...]
        ds_ref[...] = jnp.sum(dy * mixed, axis=0, keepdims=True)
        dmixed = dy * s_ref[...]
        db_ref[...] = jnp.sum(dmixed, axis=0, keepdims=True)
        dmixed_b = dmixed.astype(BF16)
        dw_ref[...] = _tn(pooled_b, dmixed_b)
        dpooled = _nt(dmixed_b, w_ref[...])
        sums, acc, k = [], dpooled * inv_count, 1
        for _ in POOL_WINDOWS:
            acc = acc + _shift_up(acc, k)
            sums.append(acc)
            k *= 2
        dp_ref[...] = (_pool_select(grp, sums) - dpooled).astype(BF16)

    vec = pl.BlockSpec((1, POOL_WIDTH), lambda j: (0, 0))
    mat = pl.BlockSpec((POOL_WIDTH, POOL_WIDTH), lambda j: (0, 0))
    pcol = pl.BlockSpec((s, POOL_WIDTH), lambda j: (0, OFF_P // POOL_WIDTH))
    return pl.pallas_call(
        body, name=name, grid=(1,),
        in_specs=[pcol, pl.BlockSpec((s, POOL_WIDTH), lambda j: (0, (SSD_WIDTH + SB_WIDTH) // POOL_WIDTH)), mat, vec, vec, ANY],
        out_specs=[pcol, mat, vec, vec],
        out_shape=[jax.ShapeDtypeStruct(dproj.shape, BF16), jax.ShapeDtypeStruct((POOL_WIDTH, POOL_WIDTH), F32),
                   jax.ShapeDtypeStruct((1, POOL_WIDTH), F32), jax.ShapeDtypeStruct((1, POOL_WIDTH), F32)],
        input_output_aliases={5: 0},
        compiler_params=_params(("arbitrary",)),
    )(proj, dyall, wbd, pb, ps, dproj)


N_PAIRS = SSD_HEADS // 2


def _ssd_common(xc, dtraw, dtb, alog):
    c = CHUNK
    dt = _softplus(dtraw + dtb)
    a = -jnp.exp(alog)
    ltri = (_iota2((c, c), 0) >= _iota2((c, c), 1)).astype(BF16)
    acum = _split_dot_left(ltri, dt * a)
    expand = (_iota2((c, SSD_WIDTH), 1) // HEAD_DIM == _iota2((c, SSD_WIDTH), 0)).astype(BF16)
    expand_wide = (_iota2((c, SSD_HEADS * c), 1) // c == _iota2((c, SSD_HEADS * c), 0)).astype(BF16)
    acum_x = _split_dot(acum, expand, 2)
    dt_x = _split_dot(dt, expand, 2)
    alast_x = acum_x[c - 1:c, :]
    return dict(dt=dt, a=a, acum=acum, acum_x=acum_x, dt_x=dt_x, ea_x=jnp.exp(acum_x),
                dte_x=jnp.exp(alast_x - acum_x), eal_x=jnp.exp(alast_x),
                acol=_split_dot(acum, expand_wide, 2), acum_t=acum.T,
                xs=xc[:, :SSD_WIDTH], causal=_iota2((c, c), 0) >= _iota2((c, c), 1),
                left=_iota2((c, c), 1) < HEAD_DIM)


def _ssd_group(xc, g):
    b = xc[:, SSD_WIDTH + D_STATE * g:SSD_WIDTH + D_STATE * (g + 1)]
    cm = xc[:, SSD_WIDTH + 2 * D_STATE + D_STATE * g:SSD_WIDTH + 2 * D_STATE + D_STATE * (g + 1)]
    return b, cm


def _ssd_decay(q, hh):
    col = q["acol"][:, CHUNK * hh:CHUNK * (hh + 1)]
    row = q["acum_t"][hh:hh + 1, :]
    return jnp.where(q["causal"], jnp.exp(jnp.minimum(col - row, 0.0)), 0.0)


def ssd_forward(proj, xc, dtb, alog, dskip_x, nw, name):
    s = xc.shape[0]
    nc = s // CHUNK

    def body(xc_ref, zdt_ref, dtb_ref, alog_ref, dsk_ref, nw_ref, y_ref, yc_ref, st_ref, state):
        @pl.when(pl.program_id(0) == 0)
        def _():
            state[...] = jnp.zeros_like(state)

        xcv = xc_ref[...]
        q = _ssd_common(xcv, zdt_ref[:, SSD_WIDTH:SSD_WIDTH + LANES], dtb_ref[...], alog_ref[...])
        x = q["xs"] * q["dt_x"]
        xb = x.astype(BF16)
        xd = (x * q["dte_x"]).astype(BF16)
        pieces = []
        for g in range(2):
            bg, cg = _ssd_group(xcv, g)
            bgb, cgb = bg.astype(BF16), cg.astype(BF16)
            cb = _nt(cgb, bgb)
            bgt = bg.T.astype(BF16)
            for pr in (2 * g, 2 * g + 1):
                sl = slice(CHUNK * pr, CHUNK * (pr + 1))
                st = state[pr]
                st_ref[0, pr] = st
                yp = _nn(cgb, st.astype(BF16)) * q["ea_x"][:, sl]
                for k, hh in enumerate((2 * pr, 2 * pr + 1)):
                    w = (cb * _ssd_decay(q, hh)).astype(BF16)
                    mask = q["left"] if k == 0 else jnp.logical_not(q["left"])
                    yp = yp + _nn(w, jnp.where(mask, xb[:, sl], jnp.zeros_like(xb[:, sl])))
                state[pr] = st * q["eal_x"][:, sl] + _nn(bgt, xd[:, sl])
                pieces.append(yp)
        y = jnp.concatenate(pieces, axis=1) + q["xs"] * dsk_ref[...]
        yc_ref[...] = y
        zv = zdt_ref[:, :SSD_WIDTH]
        yg = y * (zv * _sigmoid(zv))
        r = lax.rsqrt(jnp.mean(yg * yg, axis=-1, keepdims=True) + EPS)
        y_ref[...] = (yg * r * nw_ref[...]).astype(BF16)

    vec = lambda n: pl.BlockSpec((1, n), lambda c: (0, 0))
    return pl.pallas_call(
        body, name=name, grid=(nc,),
        in_specs=[pl.BlockSpec((CHUNK, CONV_DIM), lambda c: (c, 0)),
                  pl.BlockSpec((CHUNK, ZDT), lambda c: (c, OFF_Z // ZDT)),
                  vec(LANES), vec(LANES), vec(SSD_WIDTH), vec(SSD_WIDTH)],
        out_specs=[pl.BlockSpec((CHUNK, SSD_WIDTH), lambda c: (c, 0)), pl.BlockSpec((CHUNK, SSD_WIDTH), lambda c: (c, 0)),
                   pl.BlockSpec((1, N_PAIRS, D_STATE, CHUNK), lambda c: (c, 0, 0, 0))],
        out_shape=[jax.ShapeDtypeStruct((s, D_MODEL), BF16), jax.ShapeDtypeStruct((s, SSD_WIDTH), F32),
                   jax.ShapeDtypeStruct((nc, N_PAIRS, D_STATE, CHUNK), F32)],
        scratch_shapes=[pltpu.VMEM((N_PAIRS, D_STATE, CHUNK), F32)],
        compiler_params=_params(("arbitrary",)),
    )(xc, proj, dtb, alog, dskip_x, nw)


def ssd_backward(proj, xc, ycore, dyall, states, dtb, alog, dskip_x, nw, name):
    s = xc.shape[0]
    nc = s // CHUNK
    c = CHUNK

    def body(xc_ref, zdt_ref, yc_ref, dy_ref, st_ref, dtb_ref, alog_ref, dsk_ref, nw_ref,
             dxc_ref, dzdt_ref, dnw_ref, ddsk_ref, ddtb_ref, dalog_ref, dstate):
        @pl.when(pl.program_id(0) == 0)
        def _():
            dstate[...] = jnp.zeros_like(dstate)
            dnw_ref[...] = jnp.zeros_like(dnw_ref)
            ddsk_ref[...] = jnp.zeros_like(ddsk_ref)
            ddtb_ref[...] = jnp.zeros_like(ddtb_ref)
            dalog_ref[...] = jnp.zeros_like(dalog_ref)

        xcv = xc_ref[...]
        dtraw = zdt_ref[:, SSD_WIDTH:SSD_WIDTH + LANES]
        q = _ssd_common(xcv, dtraw, dtb_ref[...], alog_ref[...])
        xs = q["xs"]
        x = xs * q["dt_x"]
        zv, yc, dy, nwv = zdt_ref[:, :SSD_WIDTH], yc_ref[...], dy_ref[...], nw_ref[...]
        sgz = _sigmoid(zv)
        siluz = zv * sgz
        yg = yc * siluz
        r = lax.rsqrt(jnp.mean(yg * yg, axis=-1, keepdims=True) + EPS)
        dnw_ref[...] += jnp.sum(dy * yg * r, axis=0, keepdims=True)
        g1 = dy * nwv
        dyg = r * (g1 - yg * (r * r) * jnp.mean(g1 * yg, axis=-1, keepdims=True))
        dyv = dyg * siluz
        dz = (dyg * yc * (sgz * (1.0 + zv * (1.0 - sgz)))).astype(BF16)
        ddsk_ref[...] += jnp.sum(dyv * xs, axis=0, keepdims=True)
        dye = dyv * q["ea_x"]
        dx_parts, yoff_parts, u_parts, v_parts, e_parts = [], [], [], [], []
        db_parts, dc_parts = [], []
        for g in range(2):
            bg, cg = _ssd_group(xcv, g)
            bgb, cgb = bg.astype(BF16), cg.astype(BF16)
            cb = _nt(cgb, bgb)
            cgt = cg.T.astype(BF16)
            dgsum = jnp.zeros((c, c), F32)
            dbg = jnp.zeros((c, D_STATE), F32)
            dcg = jnp.zeros((c, D_STATE), F32)
            for pr in (2 * g, 2 * g + 1):
                sl = slice(c * pr, c * (pr + 1))
                st = st_ref[0, pr]
                dst = dstate[pr]
                stb, dstb = st.astype(BF16), dst.astype(BF16)
                xp = x[:, sl]
                xpb = xp.astype(BF16)
                dyp = dyv[:, sl]
                xdp = xp * q["dte_x"][:, sl]
                yoff_parts.append(_nn(cgb, stb) * q["ea_x"][:, sl])
                rr = _nn(bgb, dstb)
                dxp = rr * q["dte_x"][:, sl]
                u_parts.append(rr * xdp)
                v_parts.append(dst * st * q["eal_x"][:, sl])
                for k, hh in enumerate((2 * pr, 2 * pr + 1)):
                    decay = _ssd_decay(q, hh)
                    w = cb * decay
                    mask = q["left"] if k == 0 else jnp.logical_not(q["left"])
                    dym = jnp.where(mask, dyp, 0.0).astype(BF16)
                    dw = _nt(dym, xpb)
                    dgsum = dgsum + dw * decay
                    e_parts.append(dw * w)
                    dxp = dxp + _nn(w.T.astype(BF16), dym)
                dyeb = dye[:, sl].astype(BF16)
                dcg = dcg + _nt(dyeb, stb)
                dbg = dbg + _nt(xdp.astype(BF16), dstb)
                dstate[pr] = dst * q["eal_x"][:, sl] + _nn(cgt, dyeb)
                dx_parts.append(dxp)
            dcg = dcg + _nn(dgsum.astype(BF16), bgb)
            dbg = dbg + _nn(dgsum.T.astype(BF16), cgb)
            db_parts.append(dbg)
            dc_parts.append(dcg)
        dx = jnp.concatenate(dx_parts, axis=1)
        yoff = jnp.concatenate(yoff_parts, axis=1)
        u = jnp.concatenate(u_parts, axis=1)
        v = jnp.concatenate(v_parts, axis=1)
        reduce_heads = (_iota2((SSD_WIDTH, c), 0) // HEAD_DIM == _iota2((SSD_WIDTH, c), 1)).astype(BF16)
        to_head = (_iota2((SSD_HEADS * c, c), 0) // c == _iota2((SSD_HEADS * c, c), 1)).astype(BF16)
        da = _split_dot(dyv * yoff - u, reduce_heads, 2)
        da = da + _split_dot(jnp.concatenate(e_parts, axis=1), to_head, 2)
        da = da - _split_dot(jnp.concatenate(e_parts, axis=0), to_head, 2, dot=_tn)
        dalast = jnp.sum(_split_dot(u + v, reduce_heads, 2), axis=0, keepdims=True)
        da = da + jnp.where(_iota2((c, c), 0) == c - 1, dalast, 0.0)
        utri = (_iota2((c, c), 1) >= _iota2((c, c), 0)).astype(BF16)
        dda = _split_dot_left(utri, da)
        ddt = dda * q["a"] + _split_dot(dx * xs, reduce_heads, 2)
        dalog_ref[...] += jnp.sum(dda * q["dt"], axis=0, keepdims=True) * q["a"]
        ddtraw = jnp.where(_iota2((c, c), 1) < SSD_HEADS, ddt * _sigmoid(dtraw + dtb_ref[...]), 0.0)
        ddtb_ref[...] += jnp.sum(ddtraw, axis=0, keepdims=True)
        dzdt_ref[...] = jnp.concatenate([dz, ddtraw.astype(BF16), jnp.zeros((c, ZDT - SSD_WIDTH - LANES), BF16)], axis=1)
        dxs = dx * q["dt_x"] + dyv * dsk_ref[...]
        dxc_ref[...] = jnp.concatenate([dxs] + db_parts + dc_parts, axis=1)

    rev = lambda i: nc - 1 - i
    vec = lambda n: pl.BlockSpec((1, n), lambda i: (0, 0))
    wide = pl.BlockSpec((c, SSD_WIDTH), lambda i: (rev(i), 0))
    zdt = pl.BlockSpec((c, ZDT), lambda i: (rev(i), OFF_Z // ZDT))
    return pl.pallas_call(
        body, name=name, grid=(nc,),
        in_specs=[pl.BlockSpec((c, CONV_DIM), lambda i: (rev(i), 0)), zdt, wide, wide,
                  pl.BlockSpec((1, N_PAIRS, D_STATE, c), lambda i: (rev(i), 0, 0, 0)),
                  vec(LANES), vec(LANES), vec(SSD_WIDTH), vec(SSD_WIDTH)],
        out_specs=[pl.BlockSpec((c, CONV_DIM), lambda i: (rev(i), 0)), zdt,
                   vec(SSD_WIDTH), vec(SSD_WIDTH), vec(LANES), vec(LANES)],
        out_shape=[jax.ShapeDtypeStruct((s, CONV_DIM), F32), jax.ShapeDtypeStruct((s, D_INP), BF16),
                   jax.ShapeDtypeStruct((1, SSD_WIDTH), F32),
                   jax.ShapeDtypeStruct((1, SSD_WIDTH), F32), jax.ShapeDtypeStruct((1, LANES), F32),
                   jax.ShapeDtypeStruct((1, LANES), F32)],
        scratch_shapes=[pltpu.VMEM((N_PAIRS, D_STATE, c), F32)],
        compiler_params=_params(("arbitrary",)),
    )(xc, proj, ycore, dyall, states, dtb, alog, dskip_x, nw)


SB_Q, SB_K = 256, 512
SB_T = 256
SB_SCALE = HEAD_DIM ** -0.5


def _key_suffix(x, tri, terms):
    runs = [x[:, SB_T * k:SB_T * (k + 1)] for k in range(SB_K // SB_T)]
    sums = [_split_dot(r, tri, terms) for r in runs]
    later = None
    for k in range(len(runs) - 1, -1, -1):
        if later is not None:
            sums[k] = sums[k] + later
        total = jnp.sum(runs[k], axis=1, keepdims=True)
        later = total if later is None else later + total
    return jnp.concatenate(sums, axis=1), later


def _sb_weights(qm, kb, diagonal, run_lk, strict_after):
    z = _nt(qm, kb)
    nz = -z
    tail = jnp.log(1.0 + jnp.exp(jnp.minimum(z, nz)))
    ls = jnp.minimum(z, 0.0) - tail
    lk = jnp.minimum(nz, 0.0) - tail
    if diagonal is not None:
        valid = _iota2(z.shape, 1) < _iota2(z.shape, 0) + diagonal
        lk = jnp.where(valid, lk, 0.0)
    after, total = _key_suffix(lk, strict_after, 1)
    w = jnp.exp(ls + after + run_lk)
    if diagonal is not None:
        w = jnp.where(valid, w, 0.0)
    return ls, total, w


def _sb_sweep(i, block, init):
    own = (i * SB_Q) // SB_K
    first = block(own, init, i * SB_Q - own * SB_K)
    return lax.fori_loop(1, own + 1, lambda jj, carry: block(own - jj, carry, None), first)


def sb_forward(proj, y_all, name):
    s = proj.shape[0]
    t, tk = SB_Q, SB_K
    nq = s // t

    def body(q_ref, k_ref, v_ref, _, y_ref, o_ref):
        i = pl.program_id(1)
        left = _iota2((t, LANES), 1) < HEAD_DIM
        left_k = _iota2((tk, LANES), 1) < HEAD_DIM
        qv = q_ref[...] * SB_SCALE
        zero = jnp.zeros_like(qv)
        qms = (jnp.where(left, qv, zero).astype(BF16), jnp.where(left, zero, qv).astype(BF16))
        strict_after = (_iota2((SB_T, SB_T), 0) > _iota2((SB_T, SB_T), 1)).astype(BF16)

        def block(j, carry, diagonal):
            o, runs = carry[0], carry[1:]
            rows = pl.ds(pl.multiple_of(j * tk, tk), tk)
            kb = k_ref[rows, :].astype(BF16)
            vv = v_ref[rows, :]
            new_runs = []
            for k in range(2):
                _, total, w = _sb_weights(qms[k], kb, diagonal, runs[k], strict_after)
                vm = jnp.where(left_k if k == 0 else jnp.logical_not(left_k), vv, 0.0).astype(BF16)
                o = o + _nn(w.astype(BF16), vm)
                new_runs.append(runs[k] + total)
            return (o, *new_runs)

        init = (jnp.zeros((t, LANES), F32), jnp.zeros((t, 1), F32), jnp.zeros((t, 1), F32))
        o = _sb_sweep(i, block, init)[0]
        o_ref[...] = o
        y_ref[...] = o.astype(BF16)

    return pl.pallas_call(
        body, name=name, grid=(2, nq),
        in_specs=[pl.BlockSpec((t, LANES), lambda p, i: (i, 3 * p)),
                  pl.BlockSpec((s, LANES), lambda p, i: (0, 3 * p + 1)),
                  pl.BlockSpec((s, LANES), lambda p, i: (0, 3 * p + 2)), ANY],
        out_specs=[pl.BlockSpec((t, LANES), lambda p, i: (i, SSD_WIDTH // LANES + p)),
                   pl.BlockSpec((t, LANES), lambda p, i: (i, p))],
        out_shape=[jax.ShapeDtypeStruct(y_all.shape, BF16), jax.ShapeDtypeStruct((s, SB_WIDTH), F32)],
        input_output_aliases={3: 0},
        compiler_params=_params(("parallel", "arbitrary")),
    )(proj, proj, proj, y_all)


def sb_backward(proj, o, dyall, dproj, name, after=None):
    s = proj.shape[0]
    t, tk = SB_Q, SB_K
    nq = s // t
    specs, ops = _after(after)

    def body(q_ref, k_ref, v_ref, o_ref, do_ref, *rest):
        dqkv_ref, dk_acc, dv_acc = rest[-3:]
        dk_acc[...] = jnp.zeros_like(dk_acc)
        dv_acc[...] = jnp.zeros_like(dv_acc)
        left = _iota2((t, LANES), 1) < HEAD_DIM
        lane_masks = (left, jnp.logical_not(left))
        left_k = _iota2((tk, LANES), 1) < HEAD_DIM
        key_masks = (left_k, jnp.logical_not(left_k))
        strict_after = (_iota2((SB_T, SB_T), 0) > _iota2((SB_T, SB_T), 1)).astype(BF16)
        from_here = (_iota2((SB_T, SB_T), 0) >= _iota2((SB_T, SB_T), 1)).astype(BF16)

        def query_block(i, _):
            qrows = pl.ds(pl.multiple_of(i * t, t), t)
            qv = q_ref[qrows, :] * SB_SCALE
            dov = do_ref[qrows, :]
            zero = jnp.zeros_like(qv)
            qb = qv.astype(BF16)
            dob = dov.astype(BF16)
            prod = dob.astype(F32) * o_ref[qrows, :]
            qms = [jnp.where(m, qv, zero).astype(BF16) for m in lane_masks]
            doms = [jnp.where(m, dov, zero).astype(BF16) for m in lane_masks]
            deltas = [jnp.sum(jnp.where(m, prod, zero), axis=1, keepdims=True) for m in lane_masks]

            def block(j, carry, diagonal):
                dq = carry[0]
                run_lk, run_e = carry[1:3], carry[3:5]
                rows = pl.ds(pl.multiple_of(j * tk, tk), tk)
                kb = k_ref[rows, :].astype(BF16)
                vb = v_ref[rows, :].astype(BF16)
                dkj = jnp.zeros((tk, LANES), F32)
                dvj = jnp.zeros((tk, LANES), F32)
                new_lk, new_e = [], []
                for k in range(2):
                    ls, total, w = _sb_weights(qms[k], kb, diagonal, run_lk[k], strict_after)
                    wb = w.astype(BF16)
                    e = _nt(doms[k], vb) * wb.astype(F32)
                    e_from_here, e_total = _key_suffix(e, from_here, 2)
                    before = deltas[k] - e_from_here - run_e[k]
                    dz = e - jnp.exp(ls) * (e + before)
                    if diagonal is not None:
                        dz = jnp.where(_iota2(dz.shape, 1) < _iota2(dz.shape, 0) + diagonal, dz, 0.0)
                    dz = dz.astype(BF16)
                    m = lane_masks[k]
                    dvj = dvj + jnp.where(key_masks[k], _tn(wb, dob), 0.0)
                    dkj = dkj + jnp.where(key_masks[k], _tn(dz, qb), 0.0)
                    dq = dq + jnp.where(m, _nn(dz, kb), 0.0)
                    new_lk.append(run_lk[k] + total)
                    new_e.append(run_e[k] + e_total)
                dk_acc[rows, :] += dkj
                dv_acc[rows, :] += dvj
                return (dq, *new_lk, *new_e)

            col = jnp.zeros((t, 1), F32)
            dq = _sb_sweep(i, block, (jnp.zeros((t, LANES), F32), col, col, col, col))[0]
            dqkv_ref[qrows, 0:LANES] = (dq * SB_SCALE).astype(BF16)
            return 0

        lax.fori_loop(0, nq, query_block, 0)
        dqkv_ref[:, LANES:2 * LANES] = dk_acc[...].astype(BF16)
        dqkv_ref[:, 2 * LANES:3 * LANES] = dv_acc[...].astype(BF16)

    col = lambda f: pl.BlockSpec((s, LANES), f)
    return pl.pallas_call(
        body, name=name, grid=(2,),
        in_specs=[col(lambda p: (0, 3 * p)), col(lambda p: (0, 3 * p + 1)), col(lambda p: (0, 3 * p + 2)),
                  col(lambda p: (0, p)), col(lambda p: (0, SSD_WIDTH // LANES + p)), ANY] + specs,
        out_specs=pl.BlockSpec((s, 3 * LANES), lambda p: (0, p)),
        out_shape=jax.ShapeDtypeStruct(dproj.shape, BF16),
        input_output_aliases={5: 0},
        scratch_shapes=[pltpu.VMEM((s, LANES), F32), pltpu.VMEM((s, LANES), F32)],
        compiler_params=_params(("parallel",)),
    )(proj, proj, proj, o, dyall, dproj, *ops)


def adamw(w, g, m, v, name):
    b, r, c = w.shape
    tr = max([t for t in range(8, min(r, 512) + 1, 8) if r % t == 0], default=r)

    def body(w_ref, g_ref, m_ref, v_ref, d_ref, nm_ref, nv_ref):
        gv = g_ref[...]
        nm = ADAM_B1 * m_ref[...] + (1.0 - ADAM_B1) * gv
        nv = ADAM_B2 * v_ref[...] + (1.0 - ADAM_B2) * (gv * gv)
        m_hat = nm / (1.0 - ADAM_B1 ** ADAM_STEP)
        v_hat = nv / (1.0 - ADAM_B2 ** ADAM_STEP)
        d_ref[...] = -ADAM_LR * (m_hat / (jnp.sqrt(v_hat) + ADAM_EPS) + ADAM_WD * w_ref[...])
        nm_ref[...] = nm
        nv_ref[...] = nv

    blk = pl.BlockSpec((1, tr, c), lambda i, j: (i, j, 0))
    return pl.pallas_call(
        body, name=name, grid=(b, r // tr),
        in_specs=[blk] * 4, out_specs=[blk] * 3,
        out_shape=[jax.ShapeDtypeStruct(w.shape, F32)] * 3,
        compiler_params=_params(("parallel", "parallel")),
    )(w, g, m, v)


def _position():
    return lax.axis_index("x"), lax.axis_index("y"), lax.axis_index("c")


def _flipped(pos, flip):
    return tuple((1 - p) if f else p for p, f in zip(pos, flip))


FLIP_C = (0, 0, 1)
CHIP_FLIPS = {1: (0, 1, 0), 2: (1, 0, 0), 3: (1, 1, 0)}
SHARD_ROWS = (SHARD_IN, SHARD_OUT, SHARD_FF, SHARD_FF, SHARD_FF)


def _rows(start, size):
    return pl.ds(pl.multiple_of(start, 16), size)


HBM = pl.BlockSpec(memory_space=pltpu.HBM)
SEM = pl.BlockSpec(memory_space=pltpu.SEMAPHORE)
EFFECT = pltpu.SideEffectType.DATAFLOW_SIDE_EFFECTING


def _in_hbm(a):
    return pltpu.with_memory_space_constraint(a, pltpu.HBM)


def _landing(shape, dtype):
    return _in_hbm(lax.empty(shape, dtype))


def _copies(plan, pos, src_refs, land_refs, send_sems, recv_sems):
    return [pltpu.make_async_remote_copy(src_ref=src, dst_ref=dst, send_sem=send_sems.at[k], recv_sem=recv_sems.at[k],
                                         device_id=_flipped(pos, flip), device_id_type=MESH)
            for k, (src, dst, flip) in enumerate(plan(pos, src_refs, land_refs))]


def exchange_start(name, srcs, lands, n, plan, after=None):
    ns, nl = len(srcs), len(lands)
    specs, ops = _after(after)

    def body(*refs):
        src_refs, land_refs = refs[:ns], refs[ns:ns + nl]
        send_sems, recv_sems, token = refs[ns + nl + len(ops)], refs[ns + nl + len(ops) + 1], refs[-1]
        for cp in _copies(plan, _position(), src_refs, land_refs, send_sems, recv_sems):
            cp.start()
        token[...] = jnp.zeros_like(token)

    thru = [pltpu.HBM(a.shape, a.dtype) for a in list(srcs) + list(lands)]
    out = pl.pallas_call(
        body, name=name,
        out_shape=(pltpu.SemaphoreType.DMA((n,)), pltpu.SemaphoreType.DMA((n,)), *thru, jax.ShapeDtypeStruct((8, LANES), F32)),
        in_specs=[HBM] * (ns + nl) + specs,
        out_specs=(SEM, SEM, *([HBM] * (ns + nl)), pl.BlockSpec(memory_space=pltpu.VMEM)),
        input_output_aliases={k: 2 + k for k in range(ns + nl)},
        compiler_params=pltpu.CompilerParams(has_side_effects=EFFECT),
    )(*[_in_hbm(a) for a in srcs], *lands, *ops)
    return out[0], out[1], list(out[2:2 + ns]), list(out[2 + ns:2 + ns + nl]), out[-1]


def exchange_wait(name, started, after, plan):
    send_sems, recv_sems, srcs, lands, _ = started
    ns, nl = len(srcs), len(lands)
    specs, ops = _after(after)

    def body(*refs):
        src_refs, land_refs = refs[:ns], refs[ns:ns + nl]
        send_sems, recv_sems = refs[ns + nl], refs[ns + nl + 1]
        for cp in _copies(plan, _position(), src_refs, land_refs, send_sems, recv_sems):
            cp.wait_send()
            cp.wait_recv()

    out = pl.pallas_call(
        body, name=name,
        out_shape=tuple(pltpu.HBM(a.shape, a.dtype) for a in list(srcs) + list(lands)),
        in_specs=[HBM] * (ns + nl) + [SEM, SEM] + specs,
        out_specs=tuple([HBM] * (ns + nl)),
        input_output_aliases={k: k for k in range(ns + nl)},
        compiler_params=pltpu.CompilerParams(has_side_effects=EFFECT),
    )(*srcs, *lands, send_sems, recv_sems, *ops)
    return list(out[:ns]), list(out[ns:])


def _gather_ici_plan(pos, srcs, lands):
    chip, c = 2 * pos[0] + pos[1], pos[2]
    copies = []
    for src, dst in zip(srcs, lands):
        r = src.shape[0]
        h = r // 2
        for f in (1, 2, 3):
            copies.append((src.at[_rows(c * h, h)], dst.at[_rows(chip * r + c * h, h)], CHIP_FLIPS[f]))
    return copies


def _gather_d2d_plan(pos, srcs, lands):
    chip, c = 2 * pos[0] + pos[1], pos[2]
    copies = []
    for own, dst in zip(srcs, lands):
        r = own.shape[0]
        h = r // 2
        copies.append((own, dst.at[_rows(chip * r, r)], FLIP_C))
        for f in (1, 2, 3):
            at = _rows(lax.bitwise_xor(chip, f) * r + c * h, h)
            copies.append((dst.at[at], dst.at[at], FLIP_C))
    return copies


def gather_ici_start(shards, after=None):
    lands = [_landing((N_CHIPS * a.shape[0], D_MODEL), BF16) for a in shards]
    return exchange_start("gather_ici_start", shards, lands, 3 * len(shards), _gather_ici_plan, after=after)


def gather_d2d_start(shards, fulls, after=None):
    return exchange_start("gather_d2d_start", shards, fulls, 4 * len(shards), _gather_d2d_plan, after=after)


def _reduce_d2d_plan(pos, srcs, lands):
    c = pos[2]
    return [(src.at[:, _rows((1 - c) * (src.shape[1] // 2), src.shape[1] // 2)], dst, FLIP_C) for src, dst in zip(srcs, lands)]


def _reduce_ici_plan(pos, srcs, lands):
    chip = 2 * pos[0] + pos[1]
    return [(src.at[lax.bitwise_xor(chip, f)], dst.at[f - 1], CHIP_FLIPS[f]) for src, dst in zip(srcs, lands) for f in (1, 2, 3)]


def _reduce_swap_plan(pos, srcs, lands):
    c = pos[2]
    copies = []
    for dst in lands:
        h = dst.shape[0] // 2
        at = _rows(c * h, h)
        copies.append((dst.at[at], dst.at[at], FLIP_C))
    return copies


def reduce_d2d_start(grads):
    lands = [_landing((N_CHIPS, g.shape[1] // 2, D_MODEL), BF16) for g in grads]
    return exchange_start("reduce_d2d_start", grads, lands, len(grads), _reduce_d2d_plan)


def reduce_ici_start(chip_sums):
    lands = [_landing((N_CHIPS - 1,) + p.shape[1:], BF16) for p in chip_sums]
    return exchange_start("reduce_ici_start", chip_sums, lands, 3 * len(chip_sums), _reduce_ici_plan)


def reduce_swap_start(mine):
    return exchange_start("reduce_swap_start", [], mine, len(mine), _reduce_swap_plan)


def _by_shape(fn, *lists):
    groups, out = {}, [None] * len(lists[0])
    for k, a in enumerate(lists[0]):
        groups.setdefault(a.shape, []).append(k)
    for idx in groups.values():
        for k, r in zip(idx, fn(*[[l[k] for k in idx] for l in lists])):
            out[k] = r
    return out


def add_halves(ds, recvs, half, name):
    n = len(ds)
    nch, r, c = ds[0].shape
    h = r // 2

    def body(half_ref, *refs):
        for k in range(n):
            refs[2 * n + k][...] = (refs[k][...].astype(F32) + refs[n + k][...].astype(F32)).astype(BF16)

    mine = pl.BlockSpec((1, h, c), lambda j, hf: (j, hf[0], 0))
    whole = pl.BlockSpec((1, h, c), lambda j, hf: (j, 0, 0))
    return pl.pallas_call(
        body, name=name,
        grid_spec=pltpu.PrefetchScalarGridSpec(
            num_scalar_prefetch=1, grid=(nch,), in_specs=[mine] * n + [whole] * n, out_specs=[whole] * n),
        out_shape=[jax.ShapeDtypeStruct(rv.shape, BF16) for rv in recvs],
        compiler_params=_params(("parallel",)),
    )(half, *ds, *recvs)


def add_chips(ps, recvs, chip, name):
    n = len(ps)
    _, r, c = ps[0].shape

    def body(chip_ref, *refs):
        for k in range(n):
            acc = refs[k][0].astype(F32)
            for f in range(N_CHIPS - 1):
                acc = acc + refs[n + k][f].astype(F32)
            refs[2 * n + k][...] = acc

    return pl.pallas_call(
        body, name=name,
        grid_spec=pltpu.PrefetchScalarGridSpec(
            num_scalar_prefetch=1, grid=(1,),
            in_specs=[pl.BlockSpec((1, r, c), lambda i, ch: (ch[0], 0, 0))] * n +
                     [pl.BlockSpec((N_CHIPS - 1, r, c), lambda i, ch: (0, 0, 0))] * n,
            out_specs=[pl.BlockSpec((r, c), lambda i, ch: (ch[1], 0))] * n),
        out_shape=[jax.ShapeDtypeStruct((2 * r, c), F32)] * n,
        compiler_params=_params(("arbitrary",)),
    )(chip, *ps, *recvs)


def adamw_layers(w, gs, m, v, name):
    b, r, c = w.shape
    tr = max([t for t in range(8, min(r, 512) + 1, 8) if r % t == 0], default=r)

    def body(w_ref, m_ref, v_ref, *rest):
        g_refs, (g_ref, d_ref, nm_ref, nv_ref) = rest[:b], rest[b:]
        layer = pl.program_id(0)
        gv = g_refs[0][...]
        for l in range(1, b):
            gv = jnp.where(layer == l, g_refs[l][...], gv)
        nm = ADAM_B1 * m_ref[0] + (1.0 - ADAM_B1) * gv
        nv = ADAM_B2 * v_ref[0] + (1.0 - ADAM_B2) * (gv * gv)
        m_hat = nm / (1.0 - ADAM_B1 ** ADAM_STEP)
        v_hat = nv / (1.0 - ADAM_B2 ** ADAM_STEP)
        g_ref[0] = gv
        d_ref[0] = -ADAM_LR * (m_hat / (jnp.sqrt(v_hat) + ADAM_EPS) + ADAM_WD * w_ref[0])
        nm_ref[0] = nm
        nv_ref[0] = nv

    nr, tc = r // tr, (c if tr < r else _tile(c, 256))
    steps = nr * (c // tc)
    blk = pl.BlockSpec((1, tr, tc), lambda i, j: (i, j % nr, j // nr))
    g_specs = [pl.BlockSpec((tr, tc), lambda i, j, l=l: (jnp.where(i == l, j % nr, jnp.where(i < l, 0, nr - 1)),
                                                         jnp.where(i == l, j // nr, jnp.where(i < l, 0, c // tc - 1))))
               for l in range(b)]
    return pl.pallas_call(
        body, name=name, grid=(b, steps),
        in_specs=[blk] * 3 + g_specs, out_specs=[blk] * 4,
        out_shape=[jax.ShapeDtypeStruct(w.shape, F32)] * 4,
        compiler_params=_params(("arbitrary", "arbitrary")),
    )(w, m, v, *gs)


def small_allreduce(v, name, after=None):
    r, c = v.shape
    specs, ops = _after(after)

    def body(v_ref, *rest):
        o_ref, buf, send_sems, recv_sems = rest[len(ops):]
        pos = _position()
        me = 4 * pos[0] + 2 * pos[1] + pos[2]
        buf[0] = v_ref[...]
        copies = []
        for f in range(1, 8):
            flip = ((f >> 2) & 1, (f >> 1) & 1, f & 1)
            cp = pltpu.make_async_remote_copy(
                src_ref=v_ref, dst_ref=buf.at[f], send_sem=send_sems.at[f - 1], recv_sem=recv_sems.at[f - 1],
                device_id=_flipped(pos, flip), device_id_type=MESH)
            cp.start()
            copies.append(cp)
        for cp in copies:
            cp.wait()
        acc = buf[me]
        for d in range(1, 8):
            acc = acc + buf[lax.bitwise_xor(me, d)]
        o_ref[...] = acc

    return pl.pallas_call(
        body, name=name,
        in_specs=[pl.BlockSpec(memory_space=pltpu.VMEM)] + specs, out_specs=pl.BlockSpec(memory_space=pltpu.VMEM),
        out_shape=jax.ShapeDtypeStruct((r, c), F32),
        scratch_shapes=[pltpu.VMEM((8, r, c), F32), pltpu.SemaphoreType.DMA((7,)), pltpu.SemaphoreType.DMA((7,))],
    )(v, *ops)


def _all_devices_plan(pos, srcs, lands):
    return [(srcs[0], lands[0].at[f], ((f >> 2) & 1, (f >> 1) & 1, f & 1)) for f in range(1, 8)]


def sum_devices(v, gathered, me, name):
    r, c = v.shape

    def body(me_ref, v_ref, g_ref, o_ref):
        own = v_ref[...]
        acc = None
        for d in range(8):
            slot = lax.bitwise_xor(me_ref[0], d)
            term = jnp.where(slot == 0, own, g_ref[slot])
            acc = term if acc is None else acc + term
        o_ref[...] = acc

    return pl.pallas_call(
        body, name=name,
        grid_spec=pltpu.PrefetchScalarGridSpec(
            num_scalar_prefetch=1, grid=(1,),
            in_specs=[pl.BlockSpec((r, c), lambda i, m: (0, 0)), pl.BlockSpec((8, r, c), lambda i, m: (0, 0, 0))],
            out_specs=pl.BlockSpec((r, c), lambda i, m: (0, 0))),
        out_shape=jax.ShapeDtypeStruct((r, c), F32),
        compiler_params=_params(("arbitrary",)),
    )(me, v, gathered)


_IN_SEGMENTS = ((0, 1544, 128), (128, 1800, 128), (256, 2056, 128), (384, 1672, 128), (512, 1928, 128), (640, 2184, 128),
                (OFF_Z, 0, SSD_WIDTH), (OFF_DT, 1536, SSD_HEADS), (OFF_XBC, 512, CONV_DIM), (OFF_P, 2312, POOL_WIDTH))


def _in_column_map():
    m = np.full((D_INP,), -1, np.int64)
    for at, orig, n in _IN_SEGMENTS:
        cols = np.arange(orig, orig + n)
        m[at:at + n] = (cols // COLS_IN) * SHARD_IN + cols % COLS_IN
    return m


def take_rows(a, idx, name):
    dep, r_in, c = a.shape
    blk = LANES
    n_out, n_in = len(idx) // blk, r_in // blk
    assert len(idx) % blk == 0 and r_in % blk == 0
    sources = [sorted({int(v) // blk for v in idx[blk * i:blk * (i + 1)] if v >= 0}) for i in range(n_out)]
    width = max(len(s) for s in sources)
    table = np.zeros((n_out, width), np.int32)
    for i, s in enumerate(sources):
        spare = [b for b in range(n_in) if b not in s][:width - len(s)]
        table[i] = s + spare

    def body(tbl_ref, idx_ref, *refs):
        in_refs, o_ref = refs[:width], refs[width]
        i = pl.program_id(1)
        src = idx_ref[...]
        acc = jnp.zeros((blk, c), F32)
        for k in range(width):
            pick = (src == tbl_ref[i, k] * blk + _iota2((blk, blk), 1)).astype(BF16)
            acc = acc + _nn(pick, in_refs[k][0])
        o_ref[0] = acc.astype(BF16)

    return pl.pallas_call(
        body, name=name,
        grid_spec=pltpu.PrefetchScalarGridSpec(
            num_scalar_prefetch=1, grid=(dep, n_out),
            in_specs=[pl.BlockSpec((blk, 1), lambda l, i, t: (i, 0))] +
                     [pl.BlockSpec((1, blk, c), lambda l, i, t, k=k: (l, t[i, k], 0)) for k in range(width)],
            out_specs=pl.BlockSpec((1, blk, c), lambda l, i, t: (l, i, 0))),
        out_shape=jax.ShapeDtypeStruct((dep, len(idx), c), BF16),
        compiler_params=_params(("parallel", "parallel")),
    )(jnp.asarray(table), jnp.asarray(np.asarray(idx, np.int32).reshape(-1, 1)), *([a] * width))


def _in_weight_layout(staged):
    return take_rows(staged, _in_column_map(), "w_in_layout")


def _in_gradient_layout(dwt):
    fwd = _in_column_map()
    inv = np.full((N_CHIPS * SHARD_IN,), -1, np.int64)
    inv[fwd[fwd >= 0]] = np.nonzero(fwd >= 0)[0]
    return take_rows(dwt, inv, "dw_in_layout")


SMALL_NAMES = ("norm1_w", "conv_w", "conv_b", "dt_bias", "a_log", "d_skip", "ssd_norm_w", "pool_w", "pool_b",
               "pool_scale", "norm2_w", "final_norm_w")
SMALL_ROWS = 104


def _pack_small(parts):
    flat = jnp.concatenate([p.reshape(-1) for p in parts])
    return jnp.pad(flat, (0, SMALL_ROWS * D_MODEL - flat.shape[0])).reshape(SMALL_ROWS, D_MODEL)


def _unpack_small(flat, shapes):
    flat = flat.reshape(-1)
    out, at = [], 0
    for shp in shapes:
        n = int(np.prod(shp))
        out.append(flat[at:at + n].reshape(shp))
        at += n
    return out


def kernel(x, norm1_w, w_in, conv_w, conv_b, dt_bias, a_log, d_skip, ssd_norm_w, pool_w, pool_b, pool_scale, w_out, norm2_w, w_gate, w_up, w_down, final_norm_w, loss_target, m_norm1_w, m_w_in, m_conv_w, m_conv_b, m_dt_bias, m_a_log, m_d_skip, m_ssd_norm_w, m_pool_w, m_pool_b, m_pool_scale, m_w_out, m_norm2_w, m_w_gate, m_w_up, m_w_down, m_final_norm_w, v_norm1_w, v_w_in, v_conv_w, v_conv_b, v_dt_bias, v_a_log, v_d_skip, v_ssd_norm_w, v_pool_w, v_pool_b, v_pool_scale, v_w_out, v_norm2_w, v_w_gate, v_w_up, v_w_down, v_final_norm_w):
    px, py, pc = _position()
    chip = 2 * px + py
    chip_arr = jnp.reshape(chip, (1,)).astype(jnp.int32)
    half_arr = jnp.reshape(pc, (1,)).astype(jnp.int32)

    def layer_shards(l):
        w_in_t = jnp.pad(jnp.swapaxes(w_in[l], 0, 1).astype(BF16), ((0, SHARD_IN - COLS_IN), (0, 0)))
        return [w_in_t, w_out[l].astype(BF16), jnp.swapaxes(w_gate[l], 0, 1).astype(BF16),
                jnp.swapaxes(w_up[l], 0, 1).astype(BF16), w_down[l].astype(BF16)]

    shards0 = layer_shards(0)
    head = gather_ici_start(shards0[:1])
    over_ici = {}

    def pass_on(l, after):
        own, arrived = exchange_wait("gather_ici_wait", over_ici[l], after, _gather_ici_plan)
        swap = gather_d2d_start(own, arrived)
        tokens = [swap[4]]
        if l + 1 < DEPTH:
            over_ici[l + 1] = gather_ici_start(layer_shards(l + 1), after=swap[4])
            tokens.append(over_ici[l + 1][4])
        return swap, tokens

    def weights_of(swap, after):
        _, (w_in_st, w_out_l, w_gate_t, w_up_t, w_down_l) = exchange_wait("gather_d2d_wait", swap, after, _gather_d2d_plan)
        return _in_weight_layout(w_in_st[None])[0], w_out_l, w_gate_t, w_up_t, w_down_l

    pad_heads = lambda v: jnp.pad(v, ((0, 0), (0, LANES - SSD_HEADS)))[:, None, :]
    dtb, alog = pad_heads(dt_bias), pad_heads(a_log)
    dskip_x = jnp.repeat(d_skip, HEAD_DIM, axis=1)[:, None, :]
    eye = jnp.eye(len(POOL_WINDOWS), dtype=F32)
    wbd = (pool_w[:, :, :, None, :] * eye[None, :, None, :, None]).reshape(DEPTH, POOL_WIDTH, POOL_WIDTH).astype(BF16)
    pool_b2 = pool_b.reshape(DEPTH, 1, POOL_WIDTH)
    cw_cols = lax.dynamic_update_slice(jnp.zeros((DEPTH, CONV_WIDTH, CONV_DIM), F32), conv_w,
                                       (0, 0, chip * (CONV_DIM // N_CHIPS)))
    cw_cols = jnp.where(pc == 0, cw_cols, 0.0)
    cw_rows = (DEPTH * CONV_WIDTH * CONV_DIM) // D_MODEL
    conv_w_f = small_allreduce(jnp.pad(cw_cols.reshape(cw_rows, D_MODEL), ((0, 8), (0, 0))), "gather_conv_w")
    conv_w_f = conv_w_f[:cw_rows].reshape(DEPTH, CONV_WIDTH, CONV_DIM)
    cw8 = jnp.pad(conv_w_f, ((0, 0), (0, 8 - CONV_WIDTH), (0, 0)))

    h = x[0]
    saved, weights = [], []
    own, arrived = exchange_wait("gather_ici_wait", head, head[4], _gather_ici_plan)
    head = gather_d2d_start(own, arrived)
    over_ici[0] = gather_ici_start(shards0[1:], after=head[4])
    w_in_f = _in_weight_layout(exchange_wait("gather_d2d_wait", head, [head[4], over_ici[0][4]], _gather_d2d_plan)[1][0][None])[0]
    for l in range(DEPTH):
        if l > 0:
            w_in_f, w_out_f, w_gate_t, w_up_t, w_down_f = weights[l]
        proj = rms_matmul(h, norm1_w[l][None], w_in_f, "in_proj")
        xc = conv_forward(proj, cw8[l], conv_b[l][None], "conv_fwd")
        y_all, ycore, states = ssd_forward(proj, xc, dtb[l], alog[l], dskip_x[l], ssd_norm_w[l][None], "ssd_fwd")
        y_all, o_sb = sb_forward(proj, y_all, "sb_fwd")
        swap, tokens = pass_on(l + 1 if l else 0, o_sb) if l + 1 < DEPTH else (None, None)
        y_all = pool_forward(proj, wbd[l], pool_b2[l], pool_scale[l][None], y_all, "pool_fwd", after=tokens)
        if l == 0:
            w_out_f, w_gate_t, w_up_t, w_down_f = exchange_wait("gather_d2d_wait", swap, y_all, _gather_d2d_plan)[1]
            weights.append((w_in_f, w_out_f, w_gate_t, w_up_t, w_down_f))
        x1 = matmul_residual(y_all, w_out_f, h, "out_proj")
        x2, g, u = ffn_forward(x1, norm2_w[l][None], w_gate_t, w_up_t, w_down_f, "ffn_fwd")
        if l == 0:
            swap, tokens = pass_on(1, x2)
            weights.append(weights_of(swap, tokens))
        elif swap is not None:
            weights.append(weights_of(swap, x2))
        saved.append((h, proj, xc, ycore, states, o_sb, y_all, x1, g, u))
        h = x2

    loss_part, dx, dxb, d_final = loss_head(h, final_norm_w[None], loss_target[0], "loss_head")
    loss = lax.psum(loss_part[0, 0], ("x", "y", "c"))

    small = {n: [None] * DEPTH for n in SMALL_NAMES if n != "final_norm_w"}
    chip_half = jnp.concatenate([chip_arr, half_arr])
    reduced = {}
    d2d = ici = early = None

    def add_cores(d2d, after):
        mine, theirs = exchange_wait("reduce_d2d_wait", d2d[1], after, _reduce_d2d_plan)
        return d2d[0], reduce_ici_start(_by_shape(lambda ds, ts: add_halves(ds, ts, half_arr, "reduce_add_halves"), mine, theirs))

    def add_all(ici, after):
        sums, theirs = exchange_wait("reduce_ici_wait", ici[1], after, _reduce_ici_plan)
        return ici[0], reduce_swap_start(_by_shape(lambda ps, ts: add_chips(ps, ts, chip_half, "reduce_add_chips"), sums, theirs))

    def finish(swap, after):
        reduced[swap[0]] = exchange_wait("reduce_swap_wait", swap[1], after, _reduce_swap_plan)[1]

    swaps = []
    for l in reversed(range(DEPTH)):
        xin, proj, xc, ycore, states, o_sb, y_all, x1, g, u = saved[l]
        w_in_f, w_out_f, w_gate_t, w_up_t, w_down_f = weights[l]
        dg, du, act = ffn_backward_act(dxb, g, u, w_down_f, "ffn_bwd_act", after=None if d2d is None else d2d[1][4])
        dx1, dx1b, h2, dn2 = rms_backward([dg, du], [w_gate_t, w_up_t], x1, norm2_w[l][None], dx, "ffn_bwd_norm", 256)
        if d2d is not None:
            ici = add_cores(d2d, dx1b)
        dyall = matmul_nt(dx1b, w_out_f, "out_proj_bwd", after=None if ici is None else ici[1][4])
        dw_down = matmul_tn(act, dxb, "dw_down")
        dw_gate = matmul_tn(dg, h2, "dw_gate")
        dw_up = matmul_tn(du, h2, "dw_up")
        dw_out = matmul_tn(y_all, dx1b, "dw_out")
        late = [dw.reshape(N_CHIPS, r, D_MODEL) for dw, r in zip((dw_out, dw_gate, dw_up, dw_down), SHARD_ROWS[1:])]
        if l == 0:
            early = ("0 late", reduce_d2d_start(late))
        dxc, dproj, dsn, ddsk, ddtb, dalog = ssd_backward(proj, xc, ycore, dyall, states, dtb[l], alog[l],
                                                          dskip_x[l], ssd_norm_w[l][None], "ssd_bwd")
        dproj, dcw, dcb = conv_backward(proj, dxc, cw8[l], conv_b[l][None], dproj, "conv_bwd",
                                        after=None if early is None else early[1][4])
        if early is not None:
            early = add_cores(early, dproj)
        dproj = sb_backward(proj, o_sb, dyall, dproj, "sb_bwd", after=None if early is None else early[1][4])
        dproj, dwbd, dpb, dps = pool_backward(proj, dyall, wbd[l], pool_b2[l], pool_scale[l][None], dproj, "pool_bwd")
        if ici is not None:
            swaps.append(add_all(ici, dproj))
            ici = None
        dx, dxb, h1, dn1 = rms_backward([dproj], [w_in_f], xin, norm1_w[l][None], dx1, "in_proj_bwd", 256,
                                        after=swaps[-1][1][4] if swaps else None)
        dw_in = _in_gradient_layout(matmul_tn(dproj, h1, "dw_in")[None])[0].reshape(N_CHIPS, SHARD_IN, D_MODEL)
        d2d = (l, reduce_d2d_start([dw_in] if l == 0 else [dw_in] + late))
        small["norm1_w"][l] = dn1[0]
        small["conv_w"][l] = dcw[:CONV_WIDTH]
        small["conv_b"][l] = dcb[0]
        small["dt_bias"][l] = ddtb[0, :SSD_HEADS]
        small["a_log"][l] = dalog[0, :SSD_HEADS]
        small["d_skip"][l] = ddsk.reshape(SSD_HEADS, HEAD_DIM).sum(axis=1)
        small["ssd_norm_w"][l] = dsn[0]
        small["pool_w"][l] = jnp.stack([dwbd[64 * k:64 * k + 64, 64 * k:64 * k + 64] for k in range(len(POOL_WINDOWS))])
        small["pool_b"][l] = dpb.reshape(len(POOL_WINDOWS), -1)
        small["pool_scale"][l] = dps[0]
        small["norm2_w"][l] = dn2[0]
    grad_x = dx[None]

    ici = add_cores(d2d, d2d[1][4])
    small_parts = [d_final if n == "final_norm_w" else jnp.stack(small[n]) for n in SMALL_NAMES]
    small_start = exchange_start("reduce_small_start", [_pack_small(small_parts)],
                                 [_landing((8, SMALL_ROWS, D_MODEL), F32)], 7, _all_devices_plan, after=ici[1][4])
    swaps.append(add_all(early, small_start[4]))
    swaps.append(add_all(ici, swaps[-1][1][4]))
    for swap in swaps:
        finish(swap, swaps[-1][1][4])
    (small_own,), (small_all,) = exchange_wait("reduce_small_wait", small_start, reduced[0][0], _all_devices_plan)
    small_sum = sum_devices(small_own, small_all, jnp.reshape(4 * px + 2 * py + pc, (1,)).astype(jnp.int32), "reduce_small_sum")
    reduced[0] = reduced[0] + reduced["0 late"]
    g_big = {n: [reduced[l][k] for l in range(DEPTH)] for k, n in enumerate(("w_in", "w_out", "w_gate", "w_up", "w_down"))}
    g_big["w_in"] = [gl[:COLS_IN] for gl in g_big["w_in"]]
    transposed = ("w_in", "w_gate", "w_up")

    g_small = dict(zip(SMALL_NAMES, _unpack_small(small_sum, [p.shape for p in small_parts])))
    g_small["final_norm_w"] = g_small["final_norm_w"].reshape(final_norm_w.shape)
    g_small["conv_w"] = lax.dynamic_slice_in_dim(g_small["conv_w"], chip * (CONV_DIM // N_CHIPS), CONV_DIM // N_CHIPS, axis=2)

    given = dict(norm1_w=(norm1_w, m_norm1_w, v_norm1_w), w_in=(w_in, m_w_in, v_w_in), conv_w=(conv_w, m_conv_w, v_conv_w),
                 conv_b=(conv_b, m_conv_b, v_conv_b), dt_bias=(dt_bias, m_dt_bias, v_dt_bias), a_log=(a_log, m_a_log, v_a_log),
                 d_skip=(d_skip, m_d_skip, v_d_skip), ssd_norm_w=(ssd_norm_w, m_ssd_norm_w, v_ssd_norm_w),
                 pool_w=(pool_w, m_pool_w, v_pool_w), pool_b=(pool_b, m_pool_b, v_pool_b),
                 pool_scale=(pool_scale, m_pool_scale, v_pool_scale), w_out=(w_out, m_w_out, v_w_out),
                 norm2_w=(norm2_w, m_norm2_w, v_norm2_w), w_gate=(w_gate, m_w_gate, v_w_gate), w_up=(w_up, m_w_up, v_w_up),
                 w_down=(w_down, m_w_down, v_w_down), final_norm_w=(final_norm_w, m_final_norm_w, v_final_norm_w))
    order = ("norm1_w", "w_in", "conv_w", "conv_b", "dt_bias", "a_log", "d_skip", "ssd_norm_w", "pool_w", "pool_b",
             "pool_scale", "w_out", "norm2_w", "w_gate", "w_up", "w_down", "final_norm_w")
    grads = dict(g_small)
    results = {}
    for n in ("w_in", "w_out", "w_gate", "w_up", "w_down"):
        w, m, v = given[n]
        if n in transposed:
            out = adamw_layers(jnp.swapaxes(w, 1, 2), g_big[n], jnp.swapaxes(m, 1, 2), jnp.swapaxes(v, 1, 2), "adamw_" + n)
            out = [jnp.swapaxes(o, 1, 2) for o in out]
        else:
            out = adamw_layers(w, g_big[n], m, v, "adamw_" + n)
        grads[n], results[n] = out[0], tuple(out[1:])
    small_shapes = [given[n][0].shape for n in SMALL_NAMES]
    packed = [_pack_small([given[n][k] for n in SMALL_NAMES])[None] for k in range(3)]
    packed_g = _pack_small([grads[n] for n in SMALL_NAMES])[None]
    small_out = adamw(packed[0], packed_g, packed[1], packed[2], "adamw_small")
    small_out = [_unpack_small(o[0], small_shapes) for o in small_out]
    for i, n in enumerate(SMALL_NAMES):
        results[n] = tuple(small_out[k][i] for k in range(3))

    return (loss, grad_x, *[grads[n] for n in order], *[results[n][0] for n in order],
            *[results[n][1] for n in order], *[results[n][2] for n in order])
```

```python
import numpy as np
import jax
import jax.numpy as jnp
from jax import lax
from jax.experimental import pallas as pl
from jax.experimental.pallas import tpu as pltpu

F32 = jnp.float32
BF16 = jnp.bfloat16
MESH = pl.DeviceIdType.MESH
ANY = pl.BlockSpec(memory_space=pl.ANY)

D_MODEL = 1024
DEPTH = 4
EPS = 1e-6
SSD_WIDTH = 512
SSD_HEADS = 8
HEAD_DIM = 64
D_STATE = 128
CHUNK = 128
CONV_WIDTH = 4
CONV_DIM = 1024
SB_WIDTH = 256
POOL_WIDTH = 256
POOL_WINDOWS = (2, 4, 8, 16)
D_FF = 2816
D_IN = 2568
N_CHIPS = 4
OFF_QKV, OFF_Z, OFF_DT, OFF_XBC, OFF_P = 0, 768, 1280, 1536, 2560
D_INP = 2816
ZDT = 768
SHARD_IN, SHARD_OUT, SHARD_FF = 672, 256, 704
COLS_IN = 642
ADAM_LR, ADAM_B1, ADAM_B2, ADAM_EPS, ADAM_WD, ADAM_STEP = 0.001, 0.9, 0.999, 1e-08, 0.01, 10
LANES = 128
VMEM_LIMIT = 56 * 1024 * 1024


def _params(sem=None):
    return pltpu.CompilerParams(dimension_semantics=sem, vmem_limit_bytes=VMEM_LIMIT)


def _tile(n, cap):
    best = None
    for t in range(LANES, min(n, cap) + 1, LANES):
        if n % t == 0:
            best = t
    assert best is not None, (n, cap)
    return best


def _nt(a, b):
    return lax.dot_general(a, b, (((1,), (1,)), ((), ())), preferred_element_type=F32)


def _tn(a, b):
    return lax.dot_general(a, b, (((0,), (0,)), ((), ())), preferred_element_type=F32)


def _nn(a, b):
    return jnp.dot(a, b, preferred_element_type=F32)


def _split_dot(a, b_exact, terms=3, dot=_nn):
    acc = None
    rest = a
    for _ in range(terms):
        hi = rest.astype(BF16)
        part = dot(hi, b_exact)
        acc = part if acc is None else acc + part
        rest = rest - hi.astype(F32)
    return acc


def _split_dot_left(a_exact, b, terms=3):
    acc = None
    rest = b
    for _ in range(terms):
        hi = rest.astype(BF16)
        part = _nn(a_exact, hi)
        acc = part if acc is None else acc + part
        rest = rest - hi.astype(F32)
    return acc


def _sigmoid(x):
    return 1.0 / (1.0 + jnp.exp(-x))


def _softplus(x):
    return jnp.maximum(x, 0.0) + jnp.log(1.0 + jnp.exp(-jnp.abs(x)))


def _iota2(shape, dim):
    return lax.broadcasted_iota(jnp.int32, shape, dim)


def _after(after):
    ops = [] if after is None else list(after) if isinstance(after, (list, tuple)) else [after]
    return [ANY] * len(ops), ops


def rms_matmul(x, nw, wt, name, after=None):
    s, d = x.shape
    n = wt.shape[0]
    tm, tn = _tile(s, 512), _tile(n, 2816)
    specs, ops = _after(after)

    def body(x_ref, nw_ref, w_ref, *rest):
        o_ref, h_ref = rest[len(ops):]

        @pl.when(pl.program_id(1) == 0)
        def _():
            xv = x_ref[...]
            r = lax.rsqrt(jnp.mean(xv * xv, axis=-1, keepdims=True) + EPS)
            h_ref[...] = (xv * r * nw_ref[...]).astype(BF16)
        o_ref[...] = _nt(h_ref[...], w_ref[...])

    return pl.pallas_call(
        body, name=name, grid=(s // tm, n // tn),
        in_specs=[pl.BlockSpec((tm, d), lambda i, j: (i, 0)), pl.BlockSpec((1, d), lambda i, j: (0, 0)),
                  pl.BlockSpec((tn, d), lambda i, j: (j, 0))] + specs,
        out_specs=pl.BlockSpec((tm, tn), lambda i, j: (i, j)),
        out_shape=jax.ShapeDtypeStruct((s, n), F32),
        scratch_shapes=[pltpu.VMEM((tm, d), BF16)],
        compiler_params=_params(("parallel", "arbitrary")),
    )(x, nw, wt, *ops)


def matmul_residual(a, w, res, name):
    s, k = a.shape
    n = w.shape[1]
    tm, tn = _tile(s, 512), _tile(n, 1024)

    def body(a_ref, w_ref, r_ref, o_ref):
        o_ref[...] = r_ref[...] + _nn(a_ref[...], w_ref[...])

    return pl.pallas_call(
        body, name=name, grid=(s // tm, n // tn),
        in_specs=[pl.BlockSpec((tm, k), lambda i, j: (i, 0)), pl.BlockSpec((k, tn), lambda i, j: (0, j)),
                  pl.BlockSpec((tm, tn), lambda i, j: (i, j))],
        out_specs=pl.BlockSpec((tm, tn), lambda i, j: (i, j)),
        out_shape=jax.ShapeDtypeStruct((s, n), F32),
        compiler_params=_params(("parallel", "parallel")),
    )(a, w, res)


def matmul_nt(a, w, name, out_dtype=F32, after=None):
    s, n = a.shape
    k = w.shape[0]
    tm, tk = _tile(s, 512), _tile(k, 1024)
    specs, ops = _after(after)

    def body(a_ref, w_ref, *rest):
        rest[-1][...] = _nt(a_ref[...], w_ref[...]).astype(out_dtype)

    return pl.pallas_call(
        body, name=name, grid=(s // tm, k // tk),
        in_specs=[pl.BlockSpec((tm, n), lambda i, j: (i, 0)), pl.BlockSpec((tk, n), lambda i, j: (j, 0))] + specs,
        out_specs=pl.BlockSpec((tm, tk), lambda i, j: (i, j)),
        out_shape=jax.ShapeDtypeStruct((s, k), out_dtype),
        compiler_params=_params(("parallel", "parallel")),
    )(a, w, *ops)


def matmul_tn(a, b, name, after=None):
    s, m = a.shape
    n = b.shape[1]
    tm, tn = _tile(m, 512), _tile(n, 1024)

    def body(a_ref, b_ref, *rest):
        rest[-1][...] = _tn(a_ref[...], b_ref[...]).astype(BF16)

    specs, ops = _after(after)
    return pl.pallas_call(
        body, name=name, grid=(m // tm, n // tn),
        in_specs=[pl.BlockSpec((s, tm), lambda i, j: (0, i)), pl.BlockSpec((s, tn), lambda i, j: (0, j))] + specs,
        out_specs=pl.BlockSpec((tm, tn), lambda i, j: (i, j)),
        out_shape=jax.ShapeDtypeStruct((m, n), BF16),
        compiler_params=_params(("parallel", "parallel")),
    )(a, b, *ops)


def ffn_forward(x1, nw, wgt, wut, wd, name):
    s, d = x1.shape
    f = wgt.shape[0]
    tm, tf = _tile(s, 1024), _tile(f, 256)

    def body(x_ref, nw_ref, wg_ref, wu_ref, wd_ref, o_ref, g_ref, u_ref, h_ref, acc_ref):
        j = pl.program_id(1)

        @pl.when(j == 0)
        def _():
            xv = x_ref[...]
            r = lax.rsqrt(jnp.mean(xv * xv, axis=-1, keepdims=True) + EPS)
            h_ref[...] = (xv * r * nw_ref[...]).astype(BF16)
            acc_ref[...] = xv

        h = h_ref[...]
        g = _nt(h, wg_ref[...])
        u = _nt(h, wu_ref[...])
        g_ref[...] = g.astype(BF16)
        u_ref[...] = u.astype(BF16)
        a = (g * _sigmoid(g) * u).astype(BF16)
        acc_ref[...] += _nn(a, wd_ref[...])

        @pl.when(j == pl.num_programs(1) - 1)
        def _():
            o_ref[...] = acc_ref[...]

    wblk = pl.BlockSpec((tf, d), lambda i, j: (j, 0))
    return pl.pallas_call(
        body, name=name, grid=(s // tm, f // tf),
        in_specs=[pl.BlockSpec((tm, d), lambda i, j: (i, 0)), pl.BlockSpec((1, d), lambda i, j: (0, 0)), wblk, wblk, wblk],
        out_specs=[pl.BlockSpec((tm, d), lambda i, j: (i, 0)), pl.BlockSpec((tm, tf), lambda i, j: (i, j)),
                   pl.BlockSpec((tm, tf), lambda i, j: (i, j))],
        out_shape=[jax.ShapeDtypeStruct((s, d), F32), jax.ShapeDtypeStruct((s, f), BF16),
                   jax.ShapeDtypeStruct((s, f), BF16)],
        scratch_shapes=[pltpu.VMEM((tm, d), BF16), pltpu.VMEM((tm, d), F32)],
        compiler_params=_params(("parallel", "arbitrary")),
    )(x1, nw, wgt, wut, wd)


def ffn_backward_act(dx2, g, u, wd, name, after=None):
    s, d = dx2.shape
    f = wd.shape[0]
    tm, tf = _tile(s, 256), _tile(f, 2816)
    specs, ops = _after(after)

    def body(dx_ref, g_ref, u_ref, wd_ref, *rest):
        dg_ref, du_ref, a_ref = rest[len(ops):]
        da = _nt(dx_ref[...], wd_ref[...])
        gv = g_ref[...].astype(F32)
        uv = u_ref[...].astype(F32)
        sg = _sigmoid(gv)
        silu = gv * sg
        dg_ref[...] = (da * uv * (sg * (1.0 + gv * (1.0 - sg)))).astype(BF16)
        du_ref[...] = (da * silu).astype(BF16)
        a_ref[...] = (silu * uv).astype(BF16)

    blk = pl.BlockSpec((tm, tf), lambda i, j: (i, j))
    return pl.pallas_call(
        body, name=name, grid=(s // tm, f // tf),
        in_specs=[pl.BlockSpec((tm, d), lambda i, j: (i, 0)), blk, blk, pl.BlockSpec((tf, d), lambda i, j: (j, 0))] + specs,
        out_specs=[blk, blk, blk],
        out_shape=[jax.ShapeDtypeStruct((s, f), BF16)] * 3,
        compiler_params=_params(("parallel", "parallel")),
    )(dx2, g, u, wd, *ops)


def rms_backward(dzs, wts, x, nw, dres, name, tm, after=None):
    s, d = x.shape
    nz = len(dzs)
    specs, ops = _after(after)

    def body(*refs):
        dz_refs, w_refs = refs[:nz], refs[nz:2 * nz]
        x_ref, nw_ref, dres_ref = refs[2 * nz:2 * nz + 3]
        dx_ref, dxb_ref, h_ref, dnw_ref = refs[2 * nz + 3 + len(ops):]
        dh = _nn(dz_refs[0][...], w_refs[0][...])
        for k in range(1, nz):
            dh = dh + _nn(dz_refs[k][...], w_refs[k][...])
        xv = x_ref[...]
        r = lax.rsqrt(jnp.mean(xv * xv, axis=-1, keepdims=True) + EPS)
        xhat = xv * r
        nwv = nw_ref[...]
        h_ref[...] = (xhat * nwv).astype(BF16)

        @pl.when(pl.program_id(0) == 0)
        def _():
            dnw_ref[...] = jnp.zeros_like(dnw_ref)

        dnw_ref[...] += jnp.sum(dh * xhat, axis=0, keepdims=True)
        gdh = dh * nwv
        dx = dres_ref[...] + r * (gdh - xhat * jnp.mean(gdh * xhat, axis=-1, keepdims=True))
        dx_ref[...] = dx
        dxb_ref[...] = dx.astype(BF16)

    row = pl.BlockSpec((tm, d), lambda i: (i, 0))
    in_specs = [pl.BlockSpec((tm, dz.shape[1]), lambda i: (i, 0)) for dz in dzs]
    in_specs += [pl.BlockSpec(w.shape, lambda i: (0, 0), pipeline_mode=pl.Buffered(1)) for w in wts]
    in_specs += [row, pl.BlockSpec((1, d), lambda i: (0, 0)), row] + specs
    return pl.pallas_call(
        body, name=name, grid=(s // tm,),
        in_specs=in_specs,
        out_specs=[row, row, row, pl.BlockSpec((1, d), lambda i: (0, 0))],
        out_shape=[jax.ShapeDtypeStruct((s, d), F32), jax.ShapeDtypeStruct((s, d), BF16),
                   jax.ShapeDtypeStruct((s, d), BF16), jax.ShapeDtypeStruct((1, d), F32)],
        compiler_params=_params(("arbitrary",)),
    )(*dzs, *wts, x, nw, dres, *ops)


def loss_head(x, nw, target, name):
    s, d = x.shape
    tm = _tile(s, 512)

    def body(x_ref, nw_ref, t_ref, loss_ref, dx_ref, dxb_ref, dnw_ref):
        xv = x_ref[...]
        r = lax.rsqrt(jnp.mean(xv * xv, axis=-1, keepdims=True) + EPS)
        xhat = xv * r
        nwv = nw_ref[...]
        err = xhat * nwv - t_ref[...]

        @pl.when(pl.program_id(0) == 0)
        def _():
            dnw_ref[...] = jnp.zeros_like(dnw_ref)
            loss_ref[...] = jnp.zeros_like(loss_ref)

        part = jnp.sum(jnp.sum(err * err, axis=-1, keepdims=True), axis=0, keepdims=True) * (0.5 / d)
        loss_ref[...] += jnp.broadcast_to(part, loss_ref.shape)
        dout = err * (1.0 / d)
        dnw_ref[...] += jnp.sum(dout * xhat, axis=0, keepdims=True)
        gdh = dout * nwv
        dx = r * (gdh - xhat * jnp.mean(gdh * xhat, axis=-1, keepdims=True))
        dx_ref[...] = dx
        dxb_ref[...] = dx.astype(BF16)

    row = pl.BlockSpec((tm, d), lambda i: (i, 0))
    return pl.pallas_call(
        body, name=name, grid=(s // tm,),
        in_specs=[row, pl.BlockSpec((1, d), lambda i: (0, 0)), row],
        out_specs=[pl.BlockSpec((1, LANES), lambda i: (0, 0)), row, row, pl.BlockSpec((1, d), lambda i: (0, 0))],
        out_shape=[jax.ShapeDtypeStruct((1, LANES), F32), jax.ShapeDtypeStruct((s, d), F32),
                   jax.ShapeDtypeStruct((s, d), BF16), jax.ShapeDtypeStruct((1, d), F32)],
        compiler_params=_params(("arbitrary",)),
    )(x, nw, target)


def _shift_down(x, k):
    return jnp.where(_iota2(x.shape, 0) >= k, pltpu.roll(x, k, axis=0), 0.0)


def _shift_up(x, k):
    s = x.shape[0]
    return jnp.where(_iota2(x.shape, 0) < s - k, pltpu.roll(x, s - k, axis=0), 0.0)


CONV_TILE = 256


def conv_forward(proj, cw, cb, name):
    s = proj.shape[0]
    tn = CONV_TILE
    off = OFF_XBC // tn

    def body(u_ref, w_ref, b_ref, o_ref):
        u = u_ref[...]
        pre = b_ref[...] + w_ref[CONV_WIDTH - 1:CONV_WIDTH, :] * u
        for i in range(CONV_WIDTH - 1):
            pre = pre + w_ref[i:i + 1, :] * _shift_down(u, CONV_WIDTH - 1 - i)
        o_ref[...] = pre * _sigmoid(pre)

    return pl.pallas_call(
        body, name=name, grid=(CONV_DIM // tn,),
        in_specs=[pl.BlockSpec((s, tn), lambda j: (0, off + j)), pl.BlockSpec((8, tn), lambda j: (0, j)),
                  pl.BlockSpec((1, tn), lambda j: (0, j))],
        out_specs=pl.BlockSpec((s, tn), lambda j: (0, j)),
        out_shape=jax.ShapeDtypeStruct((s, CONV_DIM), F32),
        compiler_params=_params(("parallel",)),
    )(proj, cw, cb)


def conv_backward(proj, dxc, cw, cb, dproj, name, after=None):
    s = proj.shape[0]
    tn = CONV_TILE
    off = OFF_XBC // tn

    specs, ops = _after(after)

    def body(u_ref, d_ref, w_ref, b_ref, *rest):
        du_ref, dw_ref, db_ref = rest[-3:]
        u = u_ref[...]
        shifted = [_shift_down(u, CONV_WIDTH - 1 - i) for i in range(CONV_WIDTH - 1)] + [u]
        pre = b_ref[...] + w_ref[CONV_WIDTH - 1:CONV_WIDTH, :] * u
        for i in range(CONV_WIDTH - 1):
            pre = pre + w_ref[i:i + 1, :] * shifted[i]
        sg = _sigmoid(pre)
        dpre = d_ref[...] * (sg * (1.0 + pre * (1.0 - sg)))
        du = w_ref[CONV_WIDTH - 1:CONV_WIDTH, :] * dpre
        for i in range(CONV_WIDTH - 1):
            du = du + w_ref[i:i + 1, :] * _shift_up(dpre, CONV_WIDTH - 1 - i)
        du_ref[...] = du.astype(BF16)
        rows = [jnp.sum(dpre * shifted[i], axis=0, keepdims=True) for i in range(CONV_WIDTH)]
        rows.append(jnp.zeros((8 - CONV_WIDTH, tn), F32))
        dw_ref[...] = jnp.concatenate(rows, axis=0)
        db_ref[...] = jnp.sum(dpre, axis=0, keepdims=True)

    return pl.pallas_call(
        body, name=name, grid=(CONV_DIM // tn,),
        in_specs=[pl.BlockSpec((s, tn), lambda j: (0, off + j)), pl.BlockSpec((s, tn), lambda j: (0, j)),
                  pl.BlockSpec((8, tn), lambda j: (0, j)), pl.BlockSpec((1, tn), lambda j: (0, j)), ANY] + specs,
        out_specs=[pl.BlockSpec((s, tn), lambda j: (0, off + j)), pl.BlockSpec((8, tn), lambda j: (0, j)),
                   pl.BlockSpec((1, tn), lambda j: (0, j))],
        out_shape=[jax.ShapeDtypeStruct(dproj.shape, BF16), jax.ShapeDtypeStruct((8, CONV_DIM), F32),
                   jax.ShapeDtypeStruct((1, CONV_DIM), F32)],
        input_output_aliases={4: 0},
        compiler_params=_params(("parallel",)),
    )(proj, dxc, cw, cb, dproj, *ops)


def _pool_lane_window(shape):
    grp = _iota2(shape, 1) // (POOL_WIDTH // len(POOL_WINDOWS))
    win = jnp.full(shape, POOL_WINDOWS[-1], jnp.int32)
    for gi in range(len(POOL_WINDOWS) - 2, -1, -1):
        win = jnp.where(grp == gi, POOL_WINDOWS[gi], win)
    return grp, win


def _pool_select(grp, sums):
    out = sums[-1]
    for gi in range(len(sums) - 2, -1, -1):
        out = jnp.where(grp == gi, sums[gi], out)
    return out


def _pool_pooled(p):
    grp, win = _pool_lane_window(p.shape)
    inv_count = 1.0 / jnp.minimum(_iota2(p.shape, 0) + 1, win).astype(F32)
    sums, acc, k = [], p, 1
    for _ in POOL_WINDOWS:
        acc = acc + _shift_down(acc, k)
        sums.append(acc)
        k *= 2
    return _pool_select(grp, sums) * inv_count - p, grp, inv_count


def pool_forward(proj, wbd, pb, ps, y_all, name, after=None):
    s = proj.shape[0]
    specs, ops = _after(after)

    def body(p_ref, w_ref, b_ref, s_ref, *rest):
        o_ref = rest[-1]
        pooled, _, _ = _pool_pooled(p_ref[...])
        mixed = _nn(pooled.astype(BF16), w_ref[...]) + b_ref[...]
        o_ref[...] = (mixed * s_ref[...]).astype(BF16)

    vec = pl.BlockSpec((1, POOL_WIDTH), lambda j: (0, 0))
    return pl.pallas_call(
        body, name=name, grid=(1,),
        in_specs=[pl.BlockSpec((s, POOL_WIDTH), lambda j: (0, OFF_P // POOL_WIDTH)),
                  pl.BlockSpec((POOL_WIDTH, POOL_WIDTH), lambda j: (0, 0)), vec, vec, ANY] + specs,
        out_specs=pl.BlockSpec((s, POOL_WIDTH), lambda j: (0, (SSD_WIDTH + SB_WIDTH) // POOL_WIDTH)),
        out_shape=jax.ShapeDtypeStruct(y_all.shape, BF16),
        input_output_aliases={4: 0},
        compiler_params=_params(("arbitrary",)),
    )(proj, wbd, pb, ps, y_all, *ops)


def pool_backward(proj, dyall, wbd, pb, ps, dproj, name):
    s = proj.shape[0]

    def body(p_ref, dy_ref, w_ref, b_ref, s_ref, _, dp_ref, dw_ref, db_ref, ds_ref):
        pooled, grp, inv_count = _pool_pooled(p_ref[...])
        pooled_b = pooled.astype(BF16)
        mixed = _nn(pooled_b, w_ref[...]) + b_ref[...]
        dy = dy_ref[...]
        ds_ref[...] = jnp.sum(dy * mixed, axis=0, keepdims=True)
        dmixed = dy * s_ref[...]
        db_ref[...] = jnp.sum(dmixed, axis=0, keepdims=True)
        dmixed_b = dmixed.astype(BF16)
        dw_ref[...] = _tn(pooled_b, dmixed_b)
        dpooled = _nt(dmixed_b, w_ref[...])
        sums, acc, k = [], dpooled * inv_count, 1
        for _ in POOL_WINDOWS:
            acc = acc + _shift_up(acc, k)
            sums.append(acc)
            k *= 2
        dp_ref[...] = (_pool_select(grp, sums) - dpooled).astype(BF16)

    vec = pl.BlockSpec((1, POOL_WIDTH), lambda j: (0, 0))
    mat = pl.BlockSpec((POOL_WIDTH, POOL_WIDTH), lambda j: (0, 0))
    pcol = pl.BlockSpec((s, POOL_WIDTH), lambda j: (0, OFF_P // POOL_WIDTH))
    return pl.pallas_call(
        body, name=name, grid=(1,),
        in_specs=[pcol, pl.BlockSpec((s, POOL_WIDTH), lambda j: (0, (SSD_WIDTH + SB_WIDTH) // POOL_WIDTH)), mat, vec, vec, ANY],
        out_specs=[pcol, mat, vec, vec],
        out_shape=[jax.ShapeDtypeStruct(dproj.shape, BF16), jax.ShapeDtypeStruct((POOL_WIDTH, POOL_WIDTH), F32),
                   jax.ShapeDtypeStruct((1, POOL_WIDTH), F32), jax.ShapeDtypeStruct((1, POOL_WIDTH), F32)],
        input_output_aliases={5: 0},
        compiler_params=_params(("arbitrary",)),
    )(proj, dyall, wbd, pb, ps, dproj)


N_PAIRS = SSD_HEADS // 2


def _ssd_common(xc, dtraw, dtb, alog):
    c = CHUNK
    dt = _softplus(dtraw + dtb)
    a = -jnp.exp(alog)
    ltri = (_iota2((c, c), 0) >= _iota2((c, c), 1)).astype(BF16)
    acum = _split_dot_left(ltri, dt * a)
    expand = (_iota2((c, SSD_WIDTH), 1) // HEAD_DIM == _iota2((c, SSD_WIDTH), 0)).astype(BF16)
    expand_wide = (_iota2((c, SSD_HEADS * c), 1) // c == _iota2((c, SSD_HEADS * c), 0)).astype(BF16)
    acum_x = _split_dot(acum, expand, 2)
    dt_x = _split_dot(dt, expand, 2)
    alast_x = acum_x[c - 1:c, :]
    return dict(dt=dt, a=a, acum=acum, acum_x=acum_x, dt_x=dt_x, ea_x=jnp.exp(acum_x),
                dte_x=jnp.exp(alast_x - acum_x), eal_x=jnp.exp(alast_x),
                acol=_split_dot(acum, expand_wide, 2), acum_t=acum.T,
                xs=xc[:, :SSD_WIDTH], causal=_iota2((c, c), 0) >= _iota2((c, c), 1),
                left=_iota2((c, c), 1) < HEAD_DIM)


def _ssd_group(xc, g):
    b = xc[:, SSD_WIDTH + D_STATE * g:SSD_WIDTH + D_STATE * (g + 1)]
    cm = xc[:, SSD_WIDTH + 2 * D_STATE + D_STATE * g:SSD_WIDTH + 2 * D_STATE + D_STATE * (g + 1)]
    return b, cm


def _ssd_decay(q, hh):
    col = q["acol"][:, CHUNK * hh:CHUNK * (hh + 1)]
    row = q["acum_t"][hh:hh + 1, :]
    return jnp.where(q["causal"], jnp.exp(jnp.minimum(col - row, 0.0)), 0.0)


def ssd_forward(proj, xc, dtb, alog, dskip_x, nw, name):
    s = xc.shape[0]
    nc = s // CHUNK

    def body(xc_ref, zdt_ref, dtb_ref, alog_ref, dsk_ref, nw_ref, y_ref, yc_ref, st_ref, state):
        @pl.when(pl.program_id(0) == 0)
        def _():
            state[...] = jnp.zeros_like(state)

        xcv = xc_ref[...]
        q = _ssd_common(xcv, zdt_ref[:, SSD_WIDTH:SSD_WIDTH + LANES], dtb_ref[...], alog_ref[...])
        x = q["xs"] * q["dt_x"]
        xb = x.astype(BF16)
        xd = (x * q["dte_x"]).astype(BF16)
        pieces = []
        for g in range(2):
            bg, cg = _ssd_group(xcv, g)
            bgb, cgb = bg.astype(BF16), cg.astype(BF16)
            cb = _nt(cgb, bgb)
            bgt = bg.T.astype(BF16)
            for pr in (2 * g, 2 * g + 1):
                sl = slice(CHUNK * pr, CHUNK * (pr + 1))
                st = state[pr]
                st_ref[0, pr] = st
                yp = _nn(cgb, st.astype(BF16)) * q["ea_x"][:, sl]
                for k, hh in enumerate((2 * pr, 2 * pr + 1)):
                    w = (cb * _ssd_decay(q, hh)).astype(BF16)
                    mask = q["left"] if k == 0 else jnp.logical_not(q["left"])
                    yp = yp + _nn(w, jnp.where(mask, xb[:, sl], jnp.zeros_like(xb[:, sl])))
                state[pr] = st * q["eal_x"][:, sl] + _nn(bgt, xd[:, sl])
                pieces.append(yp)
        y = jnp.concatenate(pieces, axis=1) + q["xs"] * dsk_ref[...]
        yc_ref[...] = y
        zv = zdt_ref[:, :SSD_WIDTH]
        yg = y * (zv * _sigmoid(zv))
        r = lax.rsqrt(jnp.mean(yg * yg, axis=-1, keepdims=True) + EPS)
        y_ref[...] = (yg * r * nw_ref[...]).astype(BF16)

    vec = lambda n: pl.BlockSpec((1, n), lambda c: (0, 0))
    return pl.pallas_call(
        body, name=name, grid=(nc,),
        in_specs=[pl.BlockSpec((CHUNK, CONV_DIM), lambda c: (c, 0)),
                  pl.BlockSpec((CHUNK, ZDT), lambda c: (c, OFF_Z // ZDT)),
                  vec(LANES), vec(LANES), vec(SSD_WIDTH), vec(SSD_WIDTH)],
        out_specs=[pl.BlockSpec((CHUNK, SSD_WIDTH), lambda c: (c, 0)), pl.BlockSpec((CHUNK, SSD_WIDTH), lambda c: (c, 0)),
                   pl.BlockSpec((1, N_PAIRS, D_STATE, CHUNK), lambda c: (c, 0, 0, 0))],
        out_shape=[jax.ShapeDtypeStruct((s, D_MODEL), BF16), jax.ShapeDtypeStruct((s, SSD_WIDTH), F32),
                   jax.ShapeDtypeStruct((nc, N_PAIRS, D_STATE, CHUNK), F32)],
        scratch_shapes=[pltpu.VMEM((N_PAIRS, D_STATE, CHUNK), F32)],
        compiler_params=_params(("arbitrary",)),
    )(xc, proj, dtb, alog, dskip_x, nw)


def ssd_backward(proj, xc, ycore, dyall, states, dtb, alog, dskip_x, nw, name):
    s = xc.shape[0]
    nc = s // CHUNK
    c = CHUNK

    def body(xc_ref, zdt_ref, yc_ref, dy_ref, st_ref, dtb_ref, alog_ref, dsk_ref, nw_ref,
             dxc_ref, dzdt_ref, dnw_ref, ddsk_ref, ddtb_ref, dalog_ref, dstate):
        @pl.when(pl.program_id(0) == 0)
        def _():
            dstate[...] = jnp.zeros_like(dstate)
            dnw_ref[...] = jnp.zeros_like(dnw_ref)
            ddsk_ref[...] = jnp.zeros_like(ddsk_ref)
            ddtb_ref[...] = jnp.zeros_like(ddtb_ref)
            dalog_ref[...] = jnp.zeros_like(dalog_ref)

        xcv = xc_ref[...]
        dtraw = zdt_ref[:, SSD_WIDTH:SSD_WIDTH + LANES]
        q = _ssd_common(xcv, dtraw, dtb_ref[...], alog_ref[...])
        xs = q["xs"]
        x = xs * q["dt_x"]
        zv, yc, dy, nwv = zdt_ref[:, :SSD_WIDTH], yc_ref[...], dy_ref[...], nw_ref[...]
        sgz = _sigmoid(zv)
        siluz = zv * sgz
        yg = yc * siluz
        r = lax.rsqrt(jnp.mean(yg * yg, axis=-1, keepdims=True) + EPS)
        dnw_ref[...] += jnp.sum(dy * yg * r, axis=0, keepdims=True)
        g1 = dy * nwv
        dyg = r * (g1 - yg * (r * r) * jnp.mean(g1 * yg, axis=-1, keepdims=True))
        dyv = dyg * siluz
        dz = (dyg * yc * (sgz * (1.0 + zv * (1.0 - sgz)))).astype(BF16)
        ddsk_ref[...] += jnp.sum(dyv * xs, axis=0, keepdims=True)
        dye = dyv * q["ea_x"]
        dx_parts, yoff_parts, u_parts, v_parts, e_parts = [], [], [], [], []
        db_parts, dc_parts = [], []
        for g in range(2):
            bg, cg = _ssd_group(xcv, g)
            bgb, cgb = bg.astype(BF16), cg.astype(BF16)
            cb = _nt(cgb, bgb)
            cgt = cg.T.astype(BF16)
            dgsum = jnp.zeros((c, c), F32)
            dbg = jnp.zeros((c, D_STATE), F32)
            dcg = jnp.zeros((c, D_STATE), F32)
            for pr in (2 * g, 2 * g + 1):
                sl = slice(c * pr, c * (pr + 1))
                st = st_ref[0, pr]
                dst = dstate[pr]
                stb, dstb = st.astype(BF16), dst.astype(BF16)
                xp = x[:, sl]
                xpb = xp.astype(BF16)
                dyp = dyv[:, sl]
                xdp = xp * q["dte_x"][:, sl]
                yoff_parts.append(_nn(cgb, stb) * q["ea_x"][:, sl])
                rr = _nn(bgb, dstb)
                dxp = rr * q["dte_x"][:, sl]
                u_parts.append(rr * xdp)
                v_parts.append(dst * st * q["eal_x"][:, sl])
                for k, hh in enumerate((2 * pr, 2 * pr + 1)):
                    decay = _ssd_decay(q, hh)
                    w = cb * decay
                    mask = q["left"] if k == 0 else jnp.logical_not(q["left"])
                    dym = jnp.where(mask, dyp, 0.0).astype(BF16)
                    dw = _nt(dym, xpb)
                    dgsum = dgsum + dw * decay
                    e_parts.append(dw * w)
                    dxp = dxp + _nn(w.T.astype(BF16), dym)
                dyeb = dye[:, sl].astype(BF16)
                dcg = dcg + _nt(dyeb, stb)
                dbg = dbg + _nt(xdp.astype(BF16), dstb)
                dstate[pr] = dst * q["eal_x"][:, sl] + _nn(cgt, dyeb)
                dx_parts.append(dxp)
            dcg = dcg + _nn(dgsum.astype(BF16), bgb)
            dbg = dbg + _nn(dgsum.T.astype(BF16), cgb)
            db_parts.append(dbg)
            dc_parts.append(dcg)
        dx = jnp.concatenate(dx_parts, axis=1)
        yoff = jnp.concatenate(yoff_parts, axis=1)
        u = jnp.concatenate(u_parts, axis=1)
        v = jnp.concatenate(v_parts, axis=1)
        reduce_heads = (_iota2((SSD_WIDTH, c), 0) // HEAD_DIM == _iota2((SSD_WIDTH, c), 1)).astype(BF16)
        to_head = (_iota2((SSD_HEADS * c, c), 0) // c == _iota2((SSD_HEADS * c, c), 1)).astype(BF16)
        da = _split_dot(dyv * yoff - u, reduce_heads, 2)
        da = da + _split_dot(jnp.concatenate(e_parts, axis=1), to_head, 2)
        da = da - _split_dot(jnp.concatenate(e_parts, axis=0), to_head, 2, dot=_tn)
        dalast = jnp.sum(_split_dot(u + v, reduce_heads, 2), axis=0, keepdims=True)
        da = da + jnp.where(_iota2((c, c), 0) == c - 1, dalast, 0.0)
        utri = (_iota2((c, c), 1) >= _iota2((c, c), 0)).astype(BF16)
        dda = _split_dot_left(utri, da)
        ddt = dda * q["a"] + _split_dot(dx * xs, reduce_heads, 2)
        dalog_ref[...] += jnp.sum(dda * q["dt"], axis=0, keepdims=True) * q["a"]
        ddtraw = jnp.where(_iota2((c, c), 1) < SSD_HEADS, ddt * _sigmoid(dtraw + dtb_ref[...]), 0.0)
        ddtb_ref[...] += jnp.sum(ddtraw, axis=0, keepdims=True)
        dzdt_ref[...] = jnp.concatenate([dz, ddtraw.astype(BF16), jnp.zeros((c, ZDT - SSD_WIDTH - LANES), BF16)], axis=1)
        dxs = dx * q["dt_x"] + dyv * dsk_ref[...]
        dxc_ref[...] = jnp.concatenate([dxs] + db_parts + dc_parts, axis=1)

    rev = lambda i: nc - 1 - i
    vec = lambda n: pl.BlockSpec((1, n), lambda i: (0, 0))
    wide = pl.BlockSpec((c, SSD_WIDTH), lambda i: (rev(i), 0))
    zdt = pl.BlockSpec((c, ZDT), lambda i: (rev(i), OFF_Z // ZDT))
    return pl.pallas_call(
        body, name=name, grid=(nc,),
        in_specs=[pl.BlockSpec((c, CONV_DIM), lambda i: (rev(i), 0)), zdt, wide, wide,
                  pl.BlockSpec((1, N_PAIRS, D_STATE, c), lambda i: (rev(i), 0, 0, 0)),
                  vec(LANES), vec(LANES), vec(SSD_WIDTH), vec(SSD_WIDTH)],
        out_specs=[pl.BlockSpec((c, CONV_DIM), lambda i: (rev(i), 0)), zdt,
                   vec(SSD_WIDTH), vec(SSD_WIDTH), vec(LANES), vec(LANES)],
        out_shape=[jax.ShapeDtypeStruct((s, CONV_DIM), F32), jax.ShapeDtypeStruct((s, D_INP), BF16),
                   jax.ShapeDtypeStruct((1, SSD_WIDTH), F32),
                   jax.ShapeDtypeStruct((1, SSD_WIDTH), F32), jax.ShapeDtypeStruct((1, LANES), F32),
                   jax.ShapeDtypeStruct((1, LANES), F32)],
        scratch_shapes=[pltpu.VMEM((N_PAIRS, D_STATE, c), F32)],
        compiler_params=_params(("arbitrary",)),
    )(xc, proj, ycore, dyall, states, dtb, alog, dskip_x, nw)


SB_Q, SB_K = 512, 512
SB_T = 256
SB_SCALE = HEAD_DIM ** -0.5


def _key_suffix(x, tri, terms):
    runs = [x[:, SB_T * k:SB_T * (k + 1)] for k in range(SB_K // SB_T)]
    sums = [_split_dot(r, tri, terms) for r in runs]
    later = None
    for k in range(len(runs) - 1, -1, -1):
        if later is not None:
            sums[k] = sums[k] + later
        total = jnp.sum(runs[k], axis=1, keepdims=True)
        later = total if later is None else later + total
    return jnp.concatenate(sums, axis=1), later


def _sb_weights(qm, kb, diagonal, run_lk, strict_after):
    z = _nt(qm, kb)
    nz = -z
    tail = jnp.log(1.0 + jnp.exp(jnp.minimum(z, nz)))
    ls = jnp.minimum(z, 0.0) - tail
    lk = jnp.minimum(nz, 0.0) - tail
    if diagonal is not None:
        valid = _iota2(z.shape, 1) < _iota2(z.shape, 0) + diagonal
        lk = jnp.where(valid, lk, 0.0)
    after, total = _key_suffix(lk, strict_after, 1)
    w = jnp.exp(ls + after + run_lk)
    if diagonal is not None:
        w = jnp.where(valid, w, 0.0)
    return ls, total, w


def _sb_sweep(i, block, init):
    own = (i * SB_Q) // SB_K
    first = block(own, init, i * SB_Q - own * SB_K)
    return lax.fori_loop(1, own + 1, lambda jj, carry: block(own - jj, carry, None), first)


def sb_forward(proj, y_all, name):
    s = proj.shape[0]
    t, tk = SB_Q, SB_K
    nq = s // t

    def body(q_ref, k_ref, v_ref, _, y_ref, o_ref):
        i = pl.program_id(1)
        left = _iota2((t, LANES), 1) < HEAD_DIM
        left_k = _iota2((tk, LANES), 1) < HEAD_DIM
        qv = q_ref[...] * SB_SCALE
        zero = jnp.zeros_like(qv)
        qms = (jnp.where(left, qv, zero).astype(BF16), jnp.where(left, zero, qv).astype(BF16))
        strict_after = (_iota2((SB_T, SB_T), 0) > _iota2((SB_T, SB_T), 1)).astype(BF16)

        def block(j, carry, diagonal):
            o, runs = carry[0], carry[1:]
            rows = pl.ds(pl.multiple_of(j * tk, tk), tk)
            kb = k_ref[rows, :].astype(BF16)
            vv = v_ref[rows, :]
            new_runs = []
            for k in range(2):
                _, total, w = _sb_weights(qms[k], kb, diagonal, runs[k], strict_after)
                vm = jnp.where(left_k if k == 0 else jnp.logical_not(left_k), vv, 0.0).astype(BF16)
                o = o + _nn(w.astype(BF16), vm)
                new_runs.append(runs[k] + total)
            return (o, *new_runs)

        init = (jnp.zeros((t, LANES), F32), jnp.zeros((t, 1), F32), jnp.zeros((t, 1), F32))
        o = _sb_sweep(i, block, init)[0]
        o_ref[...] = o
        y_ref[...] = o.astype(BF16)

    return pl.pallas_call(
        body, name=name, grid=(2, nq),
        in_specs=[pl.BlockSpec((t, LANES), lambda p, i: (i, 3 * p)),
                  pl.BlockSpec((s, LANES), lambda p, i: (0, 3 * p + 1)),
                  pl.BlockSpec((s, LANES), lambda p, i: (0, 3 * p + 2)), ANY],
        out_specs=[pl.BlockSpec((t, LANES), lambda p, i: (i, SSD_WIDTH // LANES + p)),
                   pl.BlockSpec((t, LANES), lambda p, i: (i, p))],
        out_shape=[jax.ShapeDtypeStruct(y_all.shape, BF16), jax.ShapeDtypeStruct((s, SB_WIDTH), F32)],
        input_output_aliases={3: 0},
        compiler_params=_params(("parallel", "arbitrary")),
    )(proj, proj, proj, y_all)


def sb_backward(proj, o, dyall, dproj, name, after=None):
    s = proj.shape[0]
    t, tk = SB_Q, SB_K
    nq = s // t
    specs, ops = _after(after)

    def body(q_ref, k_ref, v_ref, o_ref, do_ref, *rest):
        dqkv_ref, dk_acc, dv_acc = rest[-3:]
        dk_acc[...] = jnp.zeros_like(dk_acc)
        dv_acc[...] = jnp.zeros_like(dv_acc)
        left = _iota2((t, LANES), 1) < HEAD_DIM
        lane_masks = (left, jnp.logical_not(left))
        left_k = _iota2((tk, LANES), 1) < HEAD_DIM
        key_masks = (left_k, jnp.logical_not(left_k))
        strict_after = (_iota2((SB_T, SB_T), 0) > _iota2((SB_T, SB_T), 1)).astype(BF16)
        from_here = (_iota2((SB_T, SB_T), 0) >= _iota2((SB_T, SB_T), 1)).astype(BF16)

        def query_block(i, _):
            qrows = pl.ds(pl.multiple_of(i * t, t), t)
            qv = q_ref[qrows, :] * SB_SCALE
            dov = do_ref[qrows, :]
            zero = jnp.zeros_like(qv)
            qb = qv.astype(BF16)
            dob = dov.astype(BF16)
            prod = dob.astype(F32) * o_ref[qrows, :]
            qms = [jnp.where(m, qv, zero).astype(BF16) for m in lane_masks]
            doms = [jnp.where(m, dov, zero).astype(BF16) for m in lane_masks]
            deltas = [jnp.sum(jnp.where(m, prod, zero), axis=1, keepdims=True) for m in lane_masks]

            def block(j, carry, diagonal):
                dq = carry[0]
                run_lk, run_e = carry[1:3], carry[3:5]
                rows = pl.ds(pl.multiple_of(j * tk, tk), tk)
                kb = k_ref[rows, :].astype(BF16)
                vb = v_ref[rows, :].astype(BF16)
                dkj = jnp.zeros((tk, LANES), F32)
                dvj = jnp.zeros((tk, LANES), F32)
                new_lk, new_e = [], []
                for k in range(2):
                    ls, total, w = _sb_weights(qms[k], kb, diagonal, run_lk[k], strict_after)
                    wb = w.astype(BF16)
                    e = _nt(doms[k], vb) * wb.astype(F32)
                    e_from_here, e_total = _key_suffix(e, from_here, 2)
                    before = deltas[k] - e_from_here - run_e[k]
                    dz = e - jnp.exp(ls) * (e + before)
                    if diagonal is not None:
                        dz = jnp.where(_iota2(dz.shape, 1) < _iota2(dz.shape, 0) + diagonal, dz, 0.0)
                    dz = dz.astype(BF16)
                    m = lane_masks[k]
                    dvj = dvj + jnp.where(key_masks[k], _tn(wb, dob), 0.0)
                    dkj = dkj + jnp.where(key_masks[k], _tn(dz, qb), 0.0)
                    dq = dq + jnp.where(m, _nn(dz, kb), 0.0)
                    new_lk.append(run_lk[k] + total)
                    new_e.append(run_e[k] + e_total)
                dk_acc[rows, :] += dkj
                dv_acc[rows, :] += dvj
                return (dq, *new_lk, *new_e)

            col = jnp.zeros((t, 1), F32)
            dq = _sb_sweep(i, block, (jnp.zeros((t, LANES), F32), col, col, col, col))[0]
            dqkv_ref[qrows, 0:LANES] = (dq * SB_SCALE).astype(BF16)
            return 0

        lax.fori_loop(0, nq, query_block, 0)
        dqkv_ref[:, LANES:2 * LANES] = dk_acc[...].astype(BF16)
        dqkv_ref[:, 2 * LANES:3 * LANES] = dv_acc[...].astype(BF16)

    col = lambda f: pl.BlockSpec((s, LANES), f)
    return pl.pallas_call(
        body, name=name, grid=(2,),
        in_specs=[col(lambda p: (0, 3 * p)), col(lambda p: (0, 3 * p + 1)), col(lambda p: (0, 3 * p + 2)),
                  col(lambda p: (0, p)), col(lambda p: (0, SSD_WIDTH // LANES + p)), ANY] + specs,
        out_specs=pl.BlockSpec((s, 3 * LANES), lambda p: (0, p)),
        out_shape=jax.ShapeDtypeStruct(dproj.shape, BF16),
        input_output_aliases={5: 0},
        scratch_shapes=[pltpu.VMEM((s, LANES), F32), pltpu.VMEM((s, LANES), F32)],
        compiler_params=_params(("parallel",)),
    )(proj, proj, proj, o, dyall, dproj, *ops)


def adamw(w, g, m, v, name):
    b, r, c = w.shape
    tr = max([t for t in range(8, min(r, 512) + 1, 8) if r % t == 0], default=r)

    def body(w_ref, g_ref, m_ref, v_ref, d_ref, nm_ref, nv_ref):
        gv = g_ref[...]
        nm = ADAM_B1 * m_ref[...] + (1.0 - ADAM_B1) * gv
        nv = ADAM_B2 * v_ref[...] + (1.0 - ADAM_B2) * (gv * gv)
        m_hat = nm / (1.0 - ADAM_B1 ** ADAM_STEP)
        v_hat = nv / (1.0 - ADAM_B2 ** ADAM_STEP)
        d_ref[...] = -ADAM_LR * (m_hat / (jnp.sqrt(v_hat) + ADAM_EPS) + ADAM_WD * w_ref[...])
        nm_ref[...] = nm
        nv_ref[...] = nv

    blk = pl.BlockSpec((1, tr, c), lambda i, j: (i, j, 0))
    return pl.pallas_call(
        body, name=name, grid=(b, r // tr),
        in_specs=[blk] * 4, out_specs=[blk] * 3,
        out_shape=[jax.ShapeDtypeStruct(w.shape, F32)] * 3,
        compiler_params=_params(("parallel", "parallel")),
    )(w, g, m, v)


def _position():
    return lax.axis_index("x"), lax.axis_index("y"), lax.axis_index("c")


def _flipped(pos, flip):
    return tuple((1 - p) if f else p for p, f in zip(pos, flip))


FLIP_C = (0, 0, 1)
CHIP_FLIPS = {1: (0, 1, 0), 2: (1, 0, 0), 3: (1, 1, 0)}
SHARD_ROWS = (SHARD_IN, SHARD_OUT, SHARD_FF, SHARD_FF, SHARD_FF)


def _rows(start, size):
    return pl.ds(pl.multiple_of(start, 16), size)


HBM = pl.BlockSpec(memory_space=pltpu.HBM)
SEM = pl.BlockSpec(memory_space=pltpu.SEMAPHORE)
EFFECT = pltpu.SideEffectType.DATAFLOW_SIDE_EFFECTING


def _in_hbm(a):
    return pltpu.with_memory_space_constraint(a, pltpu.HBM)


def _landing(shape, dtype):
    return _in_hbm(lax.empty(shape, dtype))


def _copies(plan, pos, src_refs, land_refs, send_sems, recv_sems):
    return [pltpu.make_async_remote_copy(src_ref=src, dst_ref=dst, send_sem=send_sems.at[k], recv_sem=recv_sems.at[k],
                                         device_id=_flipped(pos, flip), device_id_type=MESH)
            for k, (src, dst, flip) in enumerate(plan(pos, src_refs, land_refs))]


def exchange_start(name, srcs, lands, n, plan, after=None):
    ns, nl = len(srcs), len(lands)
    specs, ops = _after(after)

    def body(*refs):
        src_refs, land_refs = refs[:ns], refs[ns:ns + nl]
        send_sems, recv_sems, token = refs[ns + nl + len(ops)], refs[ns + nl + len(ops) + 1], refs[-1]
        for cp in _copies(plan, _position(), src_refs, land_refs, send_sems, recv_sems):
            cp.start()
        token[...] = jnp.zeros_like(token)

    thru = [pltpu.HBM(a.shape, a.dtype) for a in list(srcs) + list(lands)]
    out = pl.pallas_call(
        body, name=name,
        out_shape=(pltpu.SemaphoreType.DMA((n,)), pltpu.SemaphoreType.DMA((n,)), *thru, jax.ShapeDtypeStruct((8, LANES), F32)),
        in_specs=[HBM] * (ns + nl) + specs,
        out_specs=(SEM, SEM, *([HBM] * (ns + nl)), pl.BlockSpec(memory_space=pltpu.VMEM)),
        input_output_aliases={k: 2 + k for k in range(ns + nl)},
        compiler_params=pltpu.CompilerParams(has_side_effects=EFFECT),
    )(*[_in_hbm(a) for a in srcs], *lands, *ops)
    return out[0], out[1], list(out[2:2 + ns]), list(out[2 + ns:2 + ns + nl]), out[-1]


def exchange_wait(name, started, after, plan):
    send_sems, recv_sems, srcs, lands, _ = started
    ns, nl = len(srcs), len(lands)
    specs, ops = _after(after)

    def body(*refs):
        src_refs, land_refs = refs[:ns], refs[ns:ns + nl]
        send_sems, recv_sems = refs[ns + nl], refs[ns + nl + 1]
        for cp in _copies(plan, _position(), src_refs, land_refs, send_sems, recv_sems):
            cp.wait_send()
            cp.wait_recv()

    out = pl.pallas_call(
        body, name=name,
        out_shape=tuple(pltpu.HBM(a.shape, a.dtype) for a in list(srcs) + list(lands)),
        in_specs=[HBM] * (ns + nl) + [SEM, SEM] + specs,
        out_specs=tuple([HBM] * (ns + nl)),
        input_output_aliases={k: k for k in range(ns + nl)},
        compiler_params=pltpu.CompilerParams(has_side_effects=EFFECT),
    )(*srcs, *lands, send_sems, recv_sems, *ops)
    return list(out[:ns]), list(out[ns:])


def _gather_ici_plan(pos, srcs, lands):
    chip, c = 2 * pos[0] + pos[1], pos[2]
    copies = []
    for src, dst in zip(srcs, lands):
        r = src.shape[0]
        h = r // 2
        for f in (1, 2, 3):
            copies.append((src.at[_rows(c * h, h)], dst.at[_rows(chip * r + c * h, h)], CHIP_FLIPS[f]))
    return copies


def _gather_d2d_plan(pos, srcs, lands):
    chip, c = 2 * pos[0] + pos[1], pos[2]
    copies = []
    for own, dst in zip(srcs, lands):
        r = own.shape[0]
        h = r // 2
        copies.append((own, dst.at[_rows(chip * r, r)], FLIP_C))
        for f in (1, 2, 3):
            at = _rows(lax.bitwise_xor(chip, f) * r + c * h, h)
            copies.append((dst.at[at], dst.at[at], FLIP_C))
    return copies


def gather_ici_start(shards, after=None):
    lands = [_landing((N_CHIPS * a.shape[0], D_MODEL), BF16) for a in shards]
    return exchange_start("gather_ici_start", shards, lands, 3 * len(shards), _gather_ici_plan, after=after)


def gather_d2d_start(shards, fulls, after=None):
    return exchange_start("gather_d2d_start", shards, fulls, 4 * len(shards), _gather_d2d_plan, after=after)


def _reduce_d2d_plan(pos, srcs, lands):
    c = pos[2]
    return [(src.at[:, _rows((1 - c) * (src.shape[1] // 2), src.shape[1] // 2)], dst, FLIP_C) for src, dst in zip(srcs, lands)]


def _reduce_ici_plan(pos, srcs, lands):
    chip = 2 * pos[0] + pos[1]
    return [(src.at[lax.bitwise_xor(chip, f)], dst.at[f - 1], CHIP_FLIPS[f]) for src, dst in zip(srcs, lands) for f in (1, 2, 3)]


def _reduce_swap_plan(pos, srcs, lands):
    c = pos[2]
    copies = []
    for dst in lands:
        h = dst.shape[0] // 2
        at = _rows(c * h, h)
        copies.append((dst.at[at], dst.at[at], FLIP_C))
    return copies


def reduce_d2d_start(grads):
    lands = [_landing((N_CHIPS, g.shape[1] // 2, D_MODEL), BF16) for g in grads]
    return exchange_start("reduce_d2d_start", grads, lands, len(grads), _reduce_d2d_plan)


def reduce_ici_start(chip_sums):
    lands = [_landing((N_CHIPS - 1,) + p.shape[1:], BF16) for p in chip_sums]
    return exchange_start("reduce_ici_start", chip_sums, lands, 3 * len(chip_sums), _reduce_ici_plan)


def reduce_swap_start(mine):
    return exchange_start("reduce_swap_start", [], mine, len(mine), _reduce_swap_plan)


def _by_shape(fn, *lists):
    groups, out = {}, [None] * len(lists[0])
    for k, a in enumerate(lists[0]):
        groups.setdefault(a.shape, []).append(k)
    for idx in groups.values():
        for k, r in zip(idx, fn(*[[l[k] for k in idx] for l in lists])):
            out[k] = r
    return out


def add_halves(ds, recvs, half, name):
    n = len(ds)
    nch, r, c = ds[0].shape
    h = r // 2

    def body(half_ref, *refs):
        for k in range(n):
            refs[2 * n + k][...] = (refs[k][...].astype(F32) + refs[n + k][...].astype(F32)).astype(BF16)

    mine = pl.BlockSpec((1, h, c), lambda j, hf: (j, hf[0], 0))
    whole = pl.BlockSpec((1, h, c), lambda j, hf: (j, 0, 0))
    return pl.pallas_call(
        body, name=name,
        grid_spec=pltpu.PrefetchScalarGridSpec(
            num_scalar_prefetch=1, grid=(nch,), in_specs=[mine] * n + [whole] * n, out_specs=[whole] * n),
        out_shape=[jax.ShapeDtypeStruct(rv.shape, BF16) for rv in recvs],
        compiler_params=_params(("parallel",)),
    )(half, *ds, *recvs)


def add_chips(ps, recvs, chip, name):
    n = len(ps)
    _, r, c = ps[0].shape

    def body(chip_ref, *refs):
        for k in range(n):
            acc = refs[k][0].astype(F32)
            for f in range(N_CHIPS - 1):
                acc = acc + refs[n + k][f].astype(F32)
            refs[2 * n + k][...] = acc

    return pl.pallas_call(
        body, name=name,
        grid_spec=pltpu.PrefetchScalarGridSpec(
            num_scalar_prefetch=1, grid=(1,),
            in_specs=[pl.BlockSpec((1, r, c), lambda i, ch: (ch[0], 0, 0))] * n +
                     [pl.BlockSpec((N_CHIPS - 1, r, c), lambda i, ch: (0, 0, 0))] * n,
            out_specs=[pl.BlockSpec((r, c), lambda i, ch: (ch[1], 0))] * n),
        out_shape=[jax.ShapeDtypeStruct((2 * r, c), F32)] * n,
        compiler_params=_params(("arbitrary",)),
    )(chip, *ps, *recvs)


def adamw_layers(w, gs, m, v, name):
    b, r, c = w.shape
    tr = max([t for t in range(8, min(r, 512) + 1, 8) if r % t == 0], default=r)

    def body(w_ref, m_ref, v_ref, *rest):
        g_refs, (g_ref, d_ref, nm_ref, nv_ref) = rest[:b], rest[b:]
        layer = pl.program_id(0)
        gv = g_refs[0][...]
        for l in range(1, b):
            gv = jnp.where(layer == l, g_refs[l][...], gv)
        nm = ADAM_B1 * m_ref[0] + (1.0 - ADAM_B1) * gv
        nv = ADAM_B2 * v_ref[0] + (1.0 - ADAM_B2) * (gv * gv)
        m_hat = nm / (1.0 - ADAM_B1 ** ADAM_STEP)
        v_hat = nv / (1.0 - ADAM_B2 ** ADAM_STEP)
        g_ref[0] = gv
        d_ref[0] = -ADAM_LR * (m_hat / (jnp.sqrt(v_hat) + ADAM_EPS) + ADAM_WD * w_ref[0])
        nm_ref[0] = nm
        nv_ref[0] = nv

    nr, tc = r // tr, (c if tr < r else _tile(c, 256))
    steps = nr * (c // tc)
    blk = pl.BlockSpec((1, tr, tc), lambda i, j: (i, j % nr, j // nr))
    g_specs = [pl.BlockSpec((tr, tc), lambda i, j, l=l: (jnp.where(i == l, j % nr, jnp.where(i < l, 0, nr - 1)),
                                                         jnp.where(i == l, j // nr, jnp.where(i < l, 0, c // tc - 1))))
               for l in range(b)]
    return pl.pallas_call(
        body, name=name, grid=(b, steps),
        in_specs=[blk] * 3 + g_specs, out_specs=[blk] * 4,
        out_shape=[jax.ShapeDtypeStruct(w.shape, F32)] * 4,
        compiler_params=_params(("arbitrary", "arbitrary")),
    )(w, m, v, *gs)


def small_allreduce(v, name, after=None):
    r, c = v.shape
    specs, ops = _after(after)

    def body(v_ref, *rest):
        o_ref, buf, send_sems, recv_sems = rest[len(ops):]
        pos = _position()
        me = 4 * pos[0] + 2 * pos[1] + pos[2]
        buf[0] = v_ref[...]
        copies = []
        for f in range(1, 8):
            flip = ((f >> 2) & 1, (f >> 1) & 1, f & 1)
            cp = pltpu.make_async_remote_copy(
                src_ref=v_ref, dst_ref=buf.at[f], send_sem=send_sems.at[f - 1], recv_sem=recv_sems.at[f - 1],
                device_id=_flipped(pos, flip), device_id_type=MESH)
            cp.start()
            copies.append(cp)
        for cp in copies:
            cp.wait()
        acc = buf[me]
        for d in range(1, 8):
            acc = acc + buf[lax.bitwise_xor(me, d)]
        o_ref[...] = acc

    return pl.pallas_call(
        body, name=name,
        in_specs=[pl.BlockSpec(memory_space=pltpu.VMEM)] + specs, out_specs=pl.BlockSpec(memory_space=pltpu.VMEM),
        out_shape=jax.ShapeDtypeStruct((r, c), F32),
        scratch_shapes=[pltpu.VMEM((8, r, c), F32), pltpu.SemaphoreType.DMA((7,)), pltpu.SemaphoreType.DMA((7,))],
    )(v, *ops)


def _all_devices_plan(pos, srcs, lands):
    return [(srcs[0], lands[0].at[f], ((f >> 2) & 1, (f >> 1) & 1, f & 1)) for f in range(1, 8)]


def sum_devices(v, gathered, me, name):
    r, c = v.shape

    def body(me_ref, v_ref, g_ref, o_ref):
        own = v_ref[...]
        acc = None
        for d in range(8):
            slot = lax.bitwise_xor(me_ref[0], d)
            term = jnp.where(slot == 0, own, g_ref[slot])
            acc = term if acc is None else acc + term
        o_ref[...] = acc

    return pl.pallas_call(
        body, name=name,
        grid_spec=pltpu.PrefetchScalarGridSpec(
            num_scalar_prefetch=1, grid=(1,),
            in_specs=[pl.BlockSpec((r, c), lambda i, m: (0, 0)), pl.BlockSpec((8, r, c), lambda i, m: (0, 0, 0))],
            out_specs=pl.BlockSpec((r, c), lambda i, m: (0, 0))),
        out_shape=jax.ShapeDtypeStruct((r, c), F32),
        compiler_params=_params(("arbitrary",)),
    )(me, v, gathered)


_IN_SEGMENTS = ((0, 1544, 128), (128, 1800, 128), (256, 2056, 128), (384, 1672, 128), (512, 1928, 128), (640, 2184, 128),
                (OFF_Z, 0, SSD_WIDTH), (OFF_DT, 1536, SSD_HEADS), (OFF_XBC, 512, CONV_DIM), (OFF_P, 2312, POOL_WIDTH))


def _in_column_map():
    m = np.full((D_INP,), -1, np.int64)
    for at, orig, n in _IN_SEGMENTS:
        cols = np.arange(orig, orig + n)
        m[at:at + n] = (cols // COLS_IN) * SHARD_IN + cols % COLS_IN
    return m


def take_rows(a, idx, name):
    dep, r_in, c = a.shape
    blk = LANES
    n_out, n_in = len(idx) // blk, r_in // blk
    assert len(idx) % blk == 0 and r_in % blk == 0
    sources = [sorted({int(v) // blk for v in idx[blk * i:blk * (i + 1)] if v >= 0}) for i in range(n_out)]
    width = max(len(s) for s in sources)
    table = np.zeros((n_out, width), np.int32)
    for i, s in enumerate(sources):
        spare = [b for b in range(n_in) if b not in s][:width - len(s)]
        table[i] = s + spare

    def body(tbl_ref, idx_ref, *refs):
        in_refs, o_ref = refs[:width], refs[width]
        i = pl.program_id(1)
        src = idx_ref[...]
        acc = jnp.zeros((blk, c), F32)
        for k in range(width):
            pick = (src == tbl_ref[i, k] * blk + _iota2((blk, blk), 1)).astype(BF16)
            acc = acc + _nn(pick, in_refs[k][0])
        o_ref[0] = acc.astype(BF16)

    return pl.pallas_call(
        body, name=name,
        grid_spec=pltpu.PrefetchScalarGridSpec(
            num_scalar_prefetch=1, grid=(dep, n_out),
            in_specs=[pl.BlockSpec((blk, 1), lambda l, i, t: (i, 0))] +
                     [pl.BlockSpec((1, blk, c), lambda l, i, t, k=k: (l, t[i, k], 0)) for k in range(width)],
            out_specs=pl.BlockSpec((1, blk, c), lambda l, i, t: (l, i, 0))),
        out_shape=jax.ShapeDtypeStruct((dep, len(idx), c), BF16),
        compiler_params=_params(("parallel", "parallel")),
    )(jnp.asarray(table), jnp.asarray(np.asarray(idx, np.int32).reshape(-1, 1)), *([a] * width))


def _in_weight_layout(staged):
    return take_rows(staged, _in_column_map(), "w_in_layout")


def _in_gradient_layout(dwt):
    fwd = _in_column_map()
    inv = np.full((N_CHIPS * SHARD_IN,), -1, np.int64)
    inv[fwd[fwd >= 0]] = np.nonzero(fwd >= 0)[0]
    return take_rows(dwt, inv, "dw_in_layout")


SMALL_NAMES = ("norm1_w", "conv_w", "conv_b", "dt_bias", "a_log", "d_skip", "ssd_norm_w", "pool_w", "pool_b",
               "pool_scale", "norm2_w", "final_norm_w")
SMALL_ROWS = 104


def _pack_small(parts):
    flat = jnp.concatenate([p.reshape(-1) for p in parts])
    return jnp.pad(flat, (0, SMALL_ROWS * D_MODEL - flat.shape[0])).reshape(SMALL_ROWS, D_MODEL)


def _unpack_small(flat, shapes):
    flat = flat.reshape(-1)
    out, at = [], 0
    for shp in shapes:
        n = int(np.prod(shp))
        out.append(flat[at:at + n].reshape(shp))
        at += n
    return out


def kernel(x, norm1_w, w_in, conv_w, conv_b, dt_bias, a_log, d_skip, ssd_norm_w, pool_w, pool_b, pool_scale, w_out, norm2_w, w_gate, w_up, w_down, final_norm_w, loss_target, m_norm1_w, m_w_in, m_conv_w, m_conv_b, m_dt_bias, m_a_log, m_d_skip, m_ssd_norm_w, m_pool_w, m_pool_b, m_pool_scale, m_w_out, m_norm2_w, m_w_gate, m_w_up, m_w_down, m_final_norm_w, v_norm1_w, v_w_in, v_conv_w, v_conv_b, v_dt_bias, v_a_log, v_d_skip, v_ssd_norm_w, v_pool_w, v_pool_b, v_pool_scale, v_w_out, v_norm2_w, v_w_gate, v_w_up, v_w_down, v_final_norm_w):
    px, py, pc = _position()
    chip = 2 * px + py
    chip_arr = jnp.reshape(chip, (1,)).astype(jnp.int32)
    half_arr = jnp.reshape(pc, (1,)).astype(jnp.int32)

    def layer_shards(l):
        w_in_t = jnp.pad(jnp.swapaxes(w_in[l], 0, 1).astype(BF16), ((0, SHARD_IN - COLS_IN), (0, 0)))
        return [w_in_t, w_out[l].astype(BF16), jnp.swapaxes(w_gate[l], 0, 1).astype(BF16),
                jnp.swapaxes(w_up[l], 0, 1).astype(BF16), w_down[l].astype(BF16)]

    shards0 = layer_shards(0)
    head = gather_ici_start(shards0[:1])
    over_ici = {}

    def pass_on(l, after):
        own, arrived = exchange_wait("gather_ici_wait", over_ici[l], after, _gather_ici_plan)
        swap = gather_d2d_start(own, arrived)
        tokens = [swap[4]]
        if l + 1 < DEPTH:
            over_ici[l + 1] = gather_ici_start(layer_shards(l + 1), after=swap[4])
            tokens.append(over_ici[l + 1][4])
        return swap, tokens

    def weights_of(swap, after):
        _, (w_in_st, w_out_l, w_gate_t, w_up_t, w_down_l) = exchange_wait("gather_d2d_wait", swap, after, _gather_d2d_plan)
        return _in_weight_layout(w_in_st[None])[0], w_out_l, w_gate_t, w_up_t, w_down_l

    pad_heads = lambda v: jnp.pad(v, ((0, 0), (0, LANES - SSD_HEADS)))[:, None, :]
    dtb, alog = pad_heads(dt_bias), pad_heads(a_log)
    dskip_x = jnp.repeat(d_skip, HEAD_DIM, axis=1)[:, None, :]
    eye = jnp.eye(len(POOL_WINDOWS), dtype=F32)
    wbd = (pool_w[:, :, :, None, :] * eye[None, :, None, :, None]).reshape(DEPTH, POOL_WIDTH, POOL_WIDTH).astype(BF16)
    pool_b2 = pool_b.reshape(DEPTH, 1, POOL_WIDTH)
    cw_cols = lax.dynamic_update_slice(jnp.zeros((DEPTH, CONV_WIDTH, CONV_DIM), F32), conv_w,
                                       (0, 0, chip * (CONV_DIM // N_CHIPS)))
    cw_cols = jnp.where(pc == 0, cw_cols, 0.0)
    cw_rows = (DEPTH * CONV_WIDTH * CONV_DIM) // D_MODEL
    conv_w_f = small_allreduce(jnp.pad(cw_cols.reshape(cw_rows, D_MODEL), ((0, 8), (0, 0))), "gather_conv_w")
    conv_w_f = conv_w_f[:cw_rows].reshape(DEPTH, CONV_WIDTH, CONV_DIM)
    cw8 = jnp.pad(conv_w_f, ((0, 0), (0, 8 - CONV_WIDTH), (0, 0)))

    h = x[0]
    saved, weights = [], []
    own, arrived = exchange_wait("gather_ici_wait", head, head[4], _gather_ici_plan)
    head = gather_d2d_start(own, arrived)
    over_ici[0] = gather_ici_start(shards0[1:], after=head[4])
    w_in_f = _in_weight_layout(exchange_wait("gather_d2d_wait", head, [head[4], over_ici[0][4]], _gather_d2d_plan)[1][0][None])[0]
    for l in range(DEPTH):
        if l > 0:
            w_in_f, w_out_f, w_gate_t, w_up_t, w_down_f = weights[l]
        proj = rms_matmul(h, norm1_w[l][None], w_in_f, "in_proj")
        xc = conv_forward(proj, cw8[l], conv_b[l][None], "conv_fwd")
        y_all, ycore, states = ssd_forward(proj, xc, dtb[l], alog[l], dskip_x[l], ssd_norm_w[l][None], "ssd_fwd")
        y_all, o_sb = sb_forward(proj, y_all, "sb_fwd")
        swap, tokens = pass_on(l + 1 if l else 0, o_sb) if l + 1 < DEPTH else (None, None)
        y_all = pool_forward(proj, wbd[l], pool_b2[l], pool_scale[l][None], y_all, "pool_fwd", after=tokens)
        if l == 0:
            w_out_f, w_gate_t, w_up_t, w_down_f = exchange_wait("gather_d2d_wait", swap, y_all, _gather_d2d_plan)[1]
            weights.append((w_in_f, w_out_f, w_gate_t, w_up_t, w_down_f))
        x1 = matmul_residual(y_all, w_out_f, h, "out_proj")
        x2, g, u = ffn_forward(x1, norm2_w[l][None], w_gate_t, w_up_t, w_down_f, "ffn_fwd")
        if l == 0:
            swap, tokens = pass_on(1, x2)
            weights.append(weights_of(swap, tokens))
        elif swap is not None:
            weights.append(weights_of(swap, x2))
        saved.append((h, proj, xc, ycore, states, o_sb, y_all, x1, g, u))
        h = x2

    loss_part, dx, dxb, d_final = loss_head(h, final_norm_w[None], loss_target[0], "loss_head")
    loss = lax.psum(loss_part[0, 0], ("x", "y", "c"))

    small = {n: [None] * DEPTH for n in SMALL_NAMES if n != "final_norm_w"}
    chip_half = jnp.concatenate([chip_arr, half_arr])
    reduced = {}
    d2d = ici = early = None

    def add_cores(d2d, after):
        mine, theirs = exchange_wait("reduce_d2d_wait", d2d[1], after, _reduce_d2d_plan)
        return d2d[0], reduce_ici_start(_by_shape(lambda ds, ts: add_halves(ds, ts, half_arr, "reduce_add_halves"), mine, theirs))

    def add_all(ici, after):
        sums, theirs = exchange_wait("reduce_ici_wait", ici[1], after, _reduce_ici_plan)
        return ici[0], reduce_swap_start(_by_shape(lambda ps, ts: add_chips(ps, ts, chip_half, "reduce_add_chips"), sums, theirs))

    def finish(swap, after):
        reduced[swap[0]] = exchange_wait("reduce_swap_wait", swap[1], after, _reduce_swap_plan)[1]

    swaps = []
    for l in reversed(range(DEPTH)):
        xin, proj, xc, ycore, states, o_sb, y_all, x1, g, u = saved[l]
        w_in_f, w_out_f, w_gate_t, w_up_t, w_down_f = weights[l]
        dg, du, act = ffn_backward_act(dxb, g, u, w_down_f, "ffn_bwd_act", after=None if d2d is None else d2d[1][4])
        dx1, dx1b, h2, dn2 = rms_backward([dg, du], [w_gate_t, w_up_t], x1, norm2_w[l][None], dx, "ffn_bwd_norm", 512)
        if d2d is not None:
            ici = add_cores(d2d, dx1b)
        dyall = matmul_nt(dx1b, w_out_f, "out_proj_bwd", after=None if ici is None else ici[1][4])
        dw_down = matmul_tn(act, dxb, "dw_down")
        dw_gate = matmul_tn(dg, h2, "dw_gate")
        dw_up = matmul_tn(du, h2, "dw_up")
        dw_out = matmul_tn(y_all, dx1b, "dw_out")
        late = [dw.reshape(N_CHIPS, r, D_MODEL) for dw, r in zip((dw_out, dw_gate, dw_up, dw_down), SHARD_ROWS[1:])]
        if l == 0:
            early = ("0 late", reduce_d2d_start(late))
        dxc, dproj, dsn, ddsk, ddtb, dalog = ssd_backward(proj, xc, ycore, dyall, states, dtb[l], alog[l],
                                                          dskip_x[l], ssd_norm_w[l][None], "ssd_bwd")
        dproj, dcw, dcb = conv_backward(proj, dxc, cw8[l], conv_b[l][None], dproj, "conv_bwd",
                                        after=None if early is None else early[1][4])
        if early is not None:
            early = add_cores(early, dproj)
        dproj = sb_backward(proj, o_sb, dyall, dproj, "sb_bwd", after=None if early is None else early[1][4])
        dproj, dwbd, dpb, dps = pool_backward(proj, dyall, wbd[l], pool_b2[l], pool_scale[l][None], dproj, "pool_bwd")
        if ici is not None:
            swaps.append(add_all(ici, dproj))
            ici = None
        dx, dxb, h1, dn1 = rms_backward([dproj], [w_in_f], xin, norm1_w[l][None], dx1, "in_proj_bwd", 512,
                                        after=swaps[-1][1][4] if swaps else None)
        dw_in = _in_gradient_layout(matmul_tn(dproj, h1, "dw_in")[None])[0].reshape(N_CHIPS, SHARD_IN, D_MODEL)
        d2d = (l, reduce_d2d_start([dw_in] if l == 0 else [dw_in] + late))
        small["norm1_w"][l] = dn1[0]
        small["conv_w"][l] = dcw[:CONV_WIDTH]
        small["conv_b"][l] = dcb[0]
        small["dt_bias"][l] = ddtb[0, :SSD_HEADS]
        small["a_log"][l] = dalog[0, :SSD_HEADS]
        small["d_skip"][l] = ddsk.reshape(SSD_HEADS, HEAD_DIM).sum(axis=1)
        small["ssd_norm_w"][l] = dsn[0]
        small["pool_w"][l] = jnp.stack([dwbd[64 * k:64 * k + 64, 64 * k:64 * k + 64] for k in range(len(POOL_WINDOWS))])
        small["pool_b"][l] = dpb.reshape(len(POOL_WINDOWS), -1)
        small["pool_scale"][l] = dps[0]
        small["norm2_w"][l] = dn2[0]
    grad_x = dx[None]

    ici = add_cores(d2d, d2d[1][4])
    small_parts = [d_final if n == "final_norm_w" else jnp.stack(small[n]) for n in SMALL_NAMES]
    small_start = exchange_start("reduce_small_start", [_pack_small(small_parts)],
                                 [_landing((8, SMALL_ROWS, D_MODEL), F32)], 7, _all_devices_plan, after=ici[1][4])
    swaps.append(add_all(early, small_start[4]))
    swaps.append(add_all(ici, swaps[-1][1][4]))
    for swap in swaps:
        finish(swap, swaps[-1][1][4])
    (small_own,), (small_all,) = exchange_wait("reduce_small_wait", small_start, reduced[0][0], _all_devices_plan)
    small_sum = sum_devices(small_own, small_all, jnp.reshape(4 * px + 2 * py + pc, (1,)).astype(jnp.int32), "reduce_small_sum")
    reduced[0] = reduced[0] + reduced["0 late"]
    g_big = {n: [reduced[l][k] for l in range(DEPTH)] for k, n in enumerate(("w_in", "w_out", "w_gate", "w_up", "w_down"))}
    g_big["w_in"] = [gl[:COLS_IN] for gl in g_big["w_in"]]
    transposed = ("w_in", "w_gate", "w_up")

    g_small = dict(zip(SMALL_NAMES, _unpack_small(small_sum, [p.shape for p in small_parts])))
    g_small["final_norm_w"] = g_small["final_norm_w"].reshape(final_norm_w.shape)
    g_small["conv_w"] = lax.dynamic_slice_in_dim(g_small["conv_w"], chip * (CONV_DIM // N_CHIPS), CONV_DIM // N_CHIPS, axis=2)

    given = dict(norm1_w=(norm1_w, m_norm1_w, v_norm1_w), w_in=(w_in, m_w_in, v_w_in), conv_w=(conv_w, m_conv_w, v_conv_w),
                 conv_b=(conv_b, m_conv_b, v_conv_b), dt_bias=(dt_bias, m_dt_bias, v_dt_bias), a_log=(a_log, m_a_log, v_a_log),
                 d_skip=(d_skip, m_d_skip, v_d_skip), ssd_norm_w=(ssd_norm_w, m_ssd_norm_w, v_ssd_norm_w),
                 pool_w=(pool_w, m_pool_w, v_pool_w), pool_b=(pool_b, m_pool_b, v_pool_b),
                 pool_scale=(pool_scale, m_pool_scale, v_pool_scale), w_out=(w_out, m_w_out, v_w_out),
                 norm2_w=(norm2_w, m_norm2_w, v_norm2_w), w_gate=(w_gate, m_w_gate, v_w_gate), w_up=(w_up, m_w_up, v_w_up),
                 w_down=(w_down, m_w_down, v_w_down), final_norm_w=(final_norm_w, m_final_norm_w, v_final_norm_w))
    order = ("norm1_w", "w_in", "conv_w", "conv_b", "dt_bias", "a_log", "d_skip", "ssd_norm_w", "pool_w", "pool_b",
             "pool_scale", "w_out", "norm2_w", "w_gate", "w_up", "w_down", "final_norm_w")
    grads = dict(g_small)
    results = {}
    for n in ("w_in", "w_out", "w_gate", "w_up", "w_down"):
        w, m, v = given[n]
        if n in transposed:
            out = adamw_layers(jnp.swapaxes(w, 1, 2), g_big[n], jnp.swapaxes(m, 1, 2), jnp.swapaxes(v, 1, 2), "adamw_" + n)
            out = [jnp.swapaxes(o, 1, 2) for o in out]
        else:
            out = adamw_layers(w, g_big[n], m, v, "adamw_" + n)
        grads[n], results[n] = out[0], tuple(out[1:])
    small_shapes = [given[n][0].shape for n in SMALL_NAMES]
    packed = [_pack_small([given[n][k] for n in SMALL_NAMES])[None] for k in range(3)]
    packed_g = _pack_small([grads[n] for n in SMALL_NAMES])[None]
    small_out = adamw(packed[0], packed_g, packed[1], packed[2], "adamw_small")
    small_out = [_unpack_small(o[0], small_shapes) for o in small_out]
    for i, n in enumerate(SMALL_NAMES):
        results[n] = tuple(small_out[k][i] for k in range(3))

    return (loss, grad_x, *[grads[n] for n in order], *[results[n][0] for n in order],
            *[results[n][1] for n in order], *[results[n][2] for n in order])
```

```python
import numpy as np
import jax
import jax.numpy as jnp
from jax import lax
from jax.experimental import pallas as pl
from jax.experimental.pallas import tpu as pltpu

F32 = jnp.float32
BF16 = jnp.bfloat16
MESH = pl.DeviceIdType.MESH
ANY = pl.BlockSpec(memory_space=pl.ANY)

D_MODEL = 1024
DEPTH = 4
EPS = 1e-6
SSD_WIDTH = 512
SSD_HEADS = 8
HEAD_DIM = 64
D_STATE = 128
CHUNK = 128
CONV_WIDTH = 4
CONV_DIM = 1024
SB_WIDTH = 256
POOL_WIDTH = 256
POOL_WINDOWS = (2, 4, 8, 16)
D_FF = 2816
D_IN = 2568
N_CHIPS = 4
OFF_QKV, OFF_Z, OFF_DT, OFF_XBC, OFF_P = 0, 768, 1280, 1536, 2560
D_INP = 2816
ZDT = 768
SHARD_IN, SHARD_OUT, SHARD_FF = 672, 256, 704
COLS_IN = 642
ADAM_LR, ADAM_B1, ADAM_B2, ADAM_EPS, ADAM_WD, ADAM_STEP = 0.001, 0.9, 0.999, 1e-08, 0.01, 10
LANES = 128
VMEM_LIMIT = 56 * 1024 * 1024


def _params(sem=None):
    return pltpu.CompilerParams(dimension_semantics=sem, vmem_limit_bytes=VMEM_LIMIT)


def _tile(n, cap):
    best = None
    for t in range(LANES, min(n, cap) + 1, LANES):
        if n % t == 0:
            best = t
    assert best is not None, (n, cap)
    return best


def _nt(a, b):
    return lax.dot_general(a, b, (((1,), (1,)), ((), ())), preferred_element_type=F32)


def _tn(a, b):
    return lax.dot_general(a, b, (((0,), (0,)), ((), ())), preferred_element_type=F32)


def _nn(a, b):
    return jnp.dot(a, b, preferred_element_type=F32)


def _split_dot(a, b_exact, terms=3, dot=_nn):
    acc = None
    rest = a
    for _ in range(terms):
        hi = rest.astype(BF16)
        part = dot(hi, b_exact)
        acc = part if acc is None else acc + part
        rest = rest - hi.astype(F32)
    return acc


def _split_dot_left(a_exact, b, terms=3):
    acc = None
    rest = b
    for _ in range(terms):
        hi = rest.astype(BF16)
        part = _nn(a_exact, hi)
        acc = part if acc is None else acc + part
        rest = rest - hi.astype(F32)
    return acc


def _sigmoid(x):
    return 1.0 / (1.0 + jnp.exp(-x))


def _softplus(x):
    return jnp.maximum(x, 0.0) + jnp.log(1.0 + jnp.exp(-jnp.abs(x)))


def _iota2(shape, dim):
    return lax.broadcasted_iota(jnp.int32, shape, dim)


def _after(after):
    ops = [] if after is None else list(after) if isinstance(after, (list, tuple)) else [after]
    return [ANY] * len(ops), ops


def rms_matmul(x, nw, wt, name, after=None):
    s, d = x.shape
    n = wt.shape[0]
    tm, tn = _tile(s, 512), _tile(n, 2816)
    specs, ops = _after(after)

    def body(x_ref, nw_ref, w_ref, *rest):
        o_ref, h_ref = rest[len(ops):]

        @pl.when(pl.program_id(1) == 0)
        def _():
            xv = x_ref[...]
            r = lax.rsqrt(jnp.mean(xv * xv, axis=-1, keepdims=True) + EPS)
            h_ref[...] = (xv * r * nw_ref[...]).astype(BF16)
        o_ref[...] = _nt(h_ref[...], w_ref[...])

    return pl.pallas_call(
        body, name=name, grid=(s // tm, n // tn),
        in_specs=[pl.BlockSpec((tm, d), lambda i, j: (i, 0)), pl.BlockSpec((1, d), lambda i, j: (0, 0)),
                  pl.BlockSpec((tn, d), lambda i, j: (j, 0))] + specs,
        out_specs=pl.BlockSpec((tm, tn), lambda i, j: (i, j)),
        out_shape=jax.ShapeDtypeStruct((s, n), F32),
        scratch_shapes=[pltpu.VMEM((tm, d), BF16)],
        compiler_params=_params(("parallel", "arbitrary")),
    )(x, nw, wt, *ops)


def matmul_residual(a, w, res, name):
    s, k = a.shape
    n = w.shape[1]
    tm, tn = _tile(s, 512), _tile(n, 1024)

    def body(a_ref, w_ref, r_ref, o_ref):
        o_ref[...] = r_ref[...] + _nn(a_ref[...], w_ref[...])

    return pl.pallas_call(
        body, name=name, grid=(s // tm, n // tn),
        in_specs=[pl.BlockSpec((tm, k), lambda i, j: (i, 0)), pl.BlockSpec((k, tn), lambda i, j: (0, j)),
                  pl.BlockSpec((tm, tn), lambda i, j: (i, j))],
        out_specs=pl.BlockSpec((tm, tn), lambda i, j: (i, j)),
        out_shape=jax.ShapeDtypeStruct((s, n), F32),
        compiler_params=_params(("parallel", "parallel")),
    )(a, w, res)


def matmul_nt(a, w, name, out_dtype=F32, after=None):
    s, n = a.shape
    k = w.shape[0]
    tm, tk = _tile(s, 512), _tile(k, 1024)
    specs, ops = _after(after)

    def body(a_ref, w_ref, *rest):
        rest[-1][...] = _nt(a_ref[...], w_ref[...]).astype(out_dtype)

    return pl.pallas_call(
        body, name=name, grid=(s // tm, k // tk),
        in_specs=[pl.BlockSpec((tm, n), lambda i, j: (i, 0)), pl.BlockSpec((tk, n), lambda i, j: (j, 0))] + specs,
        out_specs=pl.BlockSpec((tm, tk), lambda i, j: (i, j)),
        out_shape=jax.ShapeDtypeStruct((s, k), out_dtype),
        compiler_params=_params(("parallel", "parallel")),
    )(a, w, *ops)


def matmul_tn(a, b, name, after=None):
    s, m = a.shape
    n = b.shape[1]
    tm, tn = _tile(m, 512), _tile(n, 1024)

    def body(a_ref, b_ref, *rest):
        rest[-1][...] = _tn(a_ref[...], b_ref[...]).astype(BF16)

    specs, ops = _after(after)
    return pl.pallas_call(
        body, name=name, grid=(m // tm, n // tn),
        in_specs=[pl.BlockSpec((s, tm), lambda i, j: (0, i)), pl.BlockSpec((s, tn), lambda i, j: (0, j))] + specs,
        out_specs=pl.BlockSpec((tm, tn), lambda i, j: (i, j)),
        out_shape=jax.ShapeDtypeStruct((m, n), BF16),
        compiler_params=_params(("parallel", "parallel")),
    )(a, b, *ops)


def ffn_forward(x1, nw, wgt, wut, wd, name):
    s, d = x1.shape
    f = wgt.shape[0]
    tm, tf = _tile(s, 1024), _tile(f, 256)

    def body(x_ref, nw_ref, wg_ref, wu_ref, wd_ref, o_ref, g_ref, u_ref, h_ref, acc_ref):
        j = pl.program_id(1)

        @pl.when(j == 0)
        def _():
            xv = x_ref[...]
            r = lax.rsqrt(jnp.mean(xv * xv, axis=-1, keepdims=True) + EPS)
            h_ref[...] = (xv * r * nw_ref[...]).astype(BF16)
            acc_ref[...] = xv

        h = h_ref[...]
        g = _nt(h, wg_ref[...])
        u = _nt(h, wu_ref[...])
        g_ref[...] = g.astype(BF16)
        u_ref[...] = u.astype(BF16)
        a = (g * _sigmoid(g) * u).astype(BF16)
        acc_ref[...] += _nn(a, wd_ref[...])

        @pl.when(j == pl.num_programs(1) - 1)
        def _():
            o_ref[...] = acc_ref[...]

    wblk = pl.BlockSpec((tf, d), lambda i, j: (j, 0))
    return pl.pallas_call(
        body, name=name, grid=(s // tm, f // tf),
        in_specs=[pl.BlockSpec((tm, d), lambda i, j: (i, 0)), pl.BlockSpec((1, d), lambda i, j: (0, 0)), wblk, wblk, wblk],
        out_specs=[pl.BlockSpec((tm, d), lambda i, j: (i, 0)), pl.BlockSpec((tm, tf), lambda i, j: (i, j)),
                   pl.BlockSpec((tm, tf), lambda i, j: (i, j))],
        out_shape=[jax.ShapeDtypeStruct((s, d), F32), jax.ShapeDtypeStruct((s, f), BF16),
                   jax.ShapeDtypeStruct((s, f), BF16)],
        scratch_shapes=[pltpu.VMEM((tm, d), BF16), pltpu.VMEM((tm, d), F32)],
        compiler_params=_params(("parallel", "arbitrary")),
    )(x1, nw, wgt, wut, wd)


def ffn_backward_act(dx2, g, u, wd, name, after=None):
    s, d = dx2.shape
    f = wd.shape[0]
    tm, tf = _tile(s, 256), _tile(f, 2816)
    specs, ops = _after(after)

    def body(dx_ref, g_ref, u_ref, wd_ref, *rest):
        dg_ref, du_ref, a_ref = rest[len(ops):]
        da = _nt(dx_ref[...], wd_ref[...])
        gv = g_ref[...].astype(F32)
        uv = u_ref[...].astype(F32)
        sg = _sigmoid(gv)
        silu = gv * sg
        dg_ref[...] = (da * uv * (sg * (1.0 + gv * (1.0 - sg)))).astype(BF16)
        du_ref[...] = (da * silu).astype(BF16)
        a_ref[...] = (silu * uv).astype(BF16)

    blk = pl.BlockSpec((tm, tf), lambda i, j: (i, j))
    return pl.pallas_call(
        body, name=name, grid=(s // tm, f // tf),
        in_specs=[pl.BlockSpec((tm, d), lambda i, j: (i, 0)), blk, blk, pl.BlockSpec((tf, d), lambda i, j: (j, 0))] + specs,
        out_specs=[blk, blk, blk],
        out_shape=[jax.ShapeDtypeStruct((s, f), BF16)] * 3,
        compiler_params=_params(("parallel", "parallel")),
    )(dx2, g, u, wd, *ops)


def rms_backward(dzs, wts, x, nw, dres, name, tm, after=None):
    s, d = x.shape
    nz = len(dzs)
    specs, ops = _after(after)

    def body(*refs):
        dz_refs, w_refs = refs[:nz], refs[nz:2 * nz]
        x_ref, nw_ref, dres_ref = refs[2 * nz:2 * nz + 3]
        dx_ref, dxb_ref, h_ref, dnw_ref = refs[2 * nz + 3 + len(ops):]
        dh = _nn(dz_refs[0][...], w_refs[0][...])
        for k in range(1, nz):
            dh = dh + _nn(dz_refs[k][...], w_refs[k][...])
        xv = x_ref[...]
        r = lax.rsqrt(jnp.mean(xv * xv, axis=-1, keepdims=True) + EPS)
        xhat = xv * r
        nwv = nw_ref[...]
        h_ref[...] = (xhat * nwv).astype(BF16)

        @pl.when(pl.program_id(0) == 0)
        def _():
            dnw_ref[...] = jnp.zeros_like(dnw_ref)

        dnw_ref[...] += jnp.sum(dh * xhat, axis=0, keepdims=True)
        gdh = dh * nwv
        dx = dres_ref[...] + r * (gdh - xhat * jnp.mean(gdh * xhat, axis=-1, keepdims=True))
        dx_ref[...] = dx
        dxb_ref[...] = dx.astype(BF16)

    row = pl.BlockSpec((tm, d), lambda i: (i, 0))
    in_specs = [pl.BlockSpec((tm, dz.shape[1]), lambda i: (i, 0)) for dz in dzs]
    in_specs += [pl.BlockSpec(w.shape, lambda i: (0, 0), pipeline_mode=pl.Buffered(1)) for w in wts]
    in_specs += [row, pl.BlockSpec((1, d), lambda i: (0, 0)), row] + specs
    return pl.pallas_call(
        body, name=name, grid=(s // tm,),
        in_specs=in_specs,
        out_specs=[row, row, row, pl.BlockSpec((1, d), lambda i: (0, 0))],
        out_shape=[jax.ShapeDtypeStruct((s, d), F32), jax.ShapeDtypeStruct((s, d), BF16),
                   jax.ShapeDtypeStruct((s, d), BF16), jax.ShapeDtypeStruct((1, d), F32)],
        compiler_params=_params(("arbitrary",)),
    )(*dzs, *wts, x, nw, dres, *ops)


def loss_head(x, nw, target, name):
    s, d = x.shape
    tm = _tile(s, 512)

    def body(x_ref, nw_ref, t_ref, loss_ref, dx_ref, dxb_ref, dnw_ref):
        xv = x_ref[...]
        r = lax.rsqrt(jnp.mean(xv * xv, axis=-1, keepdims=True) + EPS)
        xhat = xv * r
        nwv = nw_ref[...]
        err = xhat * nwv - t_ref[...]

        @pl.when(pl.program_id(0) == 0)
        def _():
            dnw_ref[...] = jnp.zeros_like(dnw_ref)
            loss_ref[...] = jnp.zeros_like(loss_ref)

        part = jnp.sum(jnp.sum(err * err, axis=-1, keepdims=True), axis=0, keepdims=True) * (0.5 / d)
        loss_ref[...] += jnp.broadcast_to(part, loss_ref.shape)
        dout = err * (1.0 / d)
        dnw_ref[...] += jnp.sum(dout * xhat, axis=0, keepdims=True)
        gdh = dout * nwv
        dx = r * (gdh - xhat * jnp.mean(gdh * xhat, axis=-1, keepdims=True))
        dx_ref[...] = dx
        dxb_ref[...] = dx.astype(BF16)

    row = pl.BlockSpec((tm, d), lambda i: (i, 0))
    return pl.pallas_call(
        body, name=name, grid=(s // tm,),
        in_specs=[row, pl.BlockSpec((1, d), lambda i: (0, 0)), row],
        out_specs=[pl.BlockSpec((1, LANES), lambda i: (0, 0)), row, row, pl.BlockSpec((1, d), lambda i: (0, 0))],
        out_shape=[jax.ShapeDtypeStruct((1, LANES), F32), jax.ShapeDtypeStruct((s, d), F32),
                   jax.ShapeDtypeStruct((s, d), BF16), jax.ShapeDtypeStruct((1, d), F32)],
        compiler_params=_params(("arbitrary",)),
    )(x, nw, target)


def _shift_down(x, k):
    return jnp.where(_iota2(x.shape, 0) >= k, pltpu.roll(x, k, axis=0), 0.0)


def _shift_up(x, k):
    s = x.shape[0]
    return jnp.where(_iota2(x.shape, 0) < s - k, pltpu.roll(x, s - k, axis=0), 0.0)


CONV_TILE = 256


def conv_forward(proj, cw, cb, name):
    s = proj.shape[0]
    tn = CONV_TILE
    off = OFF_XBC // tn

    def body(u_ref, w_ref, b_ref, o_ref):
        u = u_ref[...]
        pre = b_ref[...] + w_ref[CONV_WIDTH - 1:CONV_WIDTH, :] * u
        for i in range(CONV_WIDTH - 1):
            pre = pre + w_ref[i:i + 1, :] * _shift_down(u, CONV_WIDTH - 1 - i)
        o_ref[...] = pre * _sigmoid(pre)

    return pl.pallas_call(
        body, name=name, grid=(CONV_DIM // tn,),
        in_specs=[pl.BlockSpec((s, tn), lambda j: (0, off + j)), pl.BlockSpec((8, tn), lambda j: (0, j)),
                  pl.BlockSpec((1, tn), lambda j: (0, j))],
        out_specs=pl.BlockSpec((s, tn), lambda j: (0, j)),
        out_shape=jax.ShapeDtypeStruct((s, CONV_DIM), F32),
        compiler_params=_params(("parallel",)),
    )(proj, cw, cb)


def conv_backward(proj, dxc, cw, cb, dproj, name, after=None):
    s = proj.shape[0]
    tn = CONV_TILE
    off = OFF_XBC // tn

    specs, ops = _after(after)

    def body(u_ref, d_ref, w_ref, b_ref, *rest):
        du_ref, dw_ref, db_ref = rest[-3:]
        u = u_ref[...]
        shifted = [_shift_down(u, CONV_WIDTH - 1 - i) for i in range(CONV_WIDTH - 1)] + [u]
        pre = b_ref[...] + w_ref[CONV_WIDTH - 1:CONV_WIDTH, :] * u
        for i in range(CONV_WIDTH - 1):
            pre = pre + w_ref[i:i + 1, :] * shifted[i]
        sg = _sigmoid(pre)
        dpre = d_ref[...] * (sg * (1.0 + pre * (1.0 - sg)))
        du = w_ref[CONV_WIDTH - 1:CONV_WIDTH, :] * dpre
        for i in range(CONV_WIDTH - 1):
            du = du + w_ref[i:i + 1, :] * _shift_up(dpre, CONV_WIDTH - 1 - i)
        du_ref[...] = du.astype(BF16)
        rows = [jnp.sum(dpre * shifted[i], axis=0, keepdims=True) for i in range(CONV_WIDTH)]
        rows.append(jnp.zeros((8 - CONV_WIDTH, tn), F32))
        dw_ref[...] = jnp.concatenate(rows, axis=0)
        db_ref[...] = jnp.sum(dpre, axis=0, keepdims=True)

    return pl.pallas_call(
        body, name=name, grid=(CONV_DIM // tn,),
        in_specs=[pl.BlockSpec((s, tn), lambda j: (0, off + j)), pl.BlockSpec((s, tn), lambda j: (0, j)),
                  pl.BlockSpec((8, tn), lambda j: (0, j)), pl.BlockSpec((1, tn), lambda j: (0, j)), ANY] + specs,
        out_specs=[pl.BlockSpec((s, tn), lambda j: (0, off + j)), pl.BlockSpec((8, tn), lambda j: (0, j)),
                   pl.BlockSpec((1, tn), lambda j: (0, j))],
        out_shape=[jax.ShapeDtypeStruct(dproj.shape, BF16), jax.ShapeDtypeStruct((8, CONV_DIM), F32),
                   jax.ShapeDtypeStruct((1, CONV_DIM), F32)],
        input_output_aliases={4: 0},
        compiler_params=_params(("parallel",)),
    )(proj, dxc, cw, cb, dproj, *ops)


def _pool_lane_window(shape):
    grp = _iota2(shape, 1) // (POOL_WIDTH // len(POOL_WINDOWS))
    win = jnp.full(shape, POOL_WINDOWS[-1], jnp.int32)
    for gi in range(len(POOL_WINDOWS) - 2, -1, -1):
        win = jnp.where(grp == gi, POOL_WINDOWS[gi], win)
    return grp, win


def _pool_select(grp, sums):
    out = sums[-1]
    for gi in range(len(sums) - 2, -1, -1):
        out = jnp.where(grp == gi, sums[gi], out)
    return out


def _pool_pooled(p):
    grp, win = _pool_lane_window(p.shape)
    inv_count = 1.0 / jnp.minimum(_iota2(p.shape, 0) + 1, win).astype(F32)
    sums, acc, k = [], p, 1
    for _ in POOL_WINDOWS:
        acc = acc + _shift_down(acc, k)
        sums.append(acc)
        k *= 2
    return _pool_select(grp, sums) * inv_count - p, grp, inv_count


def pool_forward(proj, wbd, pb, ps, y_all, name, after=None):
    s = proj.shape[0]
    specs, ops = _after(after)

    def body(p_ref, w_ref, b_ref, s_ref, *rest):
        o_ref = rest[-1]
        pooled, _, _ = _pool_pooled(p_ref[...])
        mixed = _nn(pooled.astype(BF16), w_ref[...]) + b_ref[...]
        o_ref[...] = (mixed * s_ref[...]).astype(BF16)

    vec = pl.BlockSpec((1, POOL_WIDTH), lambda j: (0, 0))
    return pl.pallas_call(
        body, name=name, grid=(1,),
        in_specs=[pl.BlockSpec((s, POOL_WIDTH), lambda j: (0, OFF_P // POOL_WIDTH)),
                  pl.BlockSpec((POOL_WIDTH, POOL_WIDTH), lambda j: (0, 0)), vec, vec, ANY] + specs,
        out_specs=pl.BlockSpec((s, POOL_WIDTH), lambda j: (0, (SSD_WIDTH + SB_WIDTH) // POOL_WIDTH)),
        out_shape=jax.ShapeDtypeStruct(y_all.shape, BF16),
        input_output_aliases={4: 0},
        compiler_params=_params(("arbitrary",)),
    )(proj, wbd, pb, ps, y_all, *ops)


def pool_backward(proj, dyall, wbd, pb, ps, dproj, name):
    s = proj.shape[0]

    def body(p_ref, dy_ref, w_ref, b_ref, s_ref, _, dp_ref, dw_ref, db_ref, ds_ref):
        pooled, grp, inv_count = _pool_pooled(p_ref[...])
        pooled_b = pooled.astype(BF16)
        mixed = _nn(pooled_b, w_ref[...]) + b_ref[...]
        dy = dy_ref[...]
        ds_ref[...] = jnp.sum(dy * mixed, axis=0, keepdims=True)
        dmixed = dy * s_ref[...]
        db_ref[...] = jnp.sum(dmixed, axis=0, keepdims=True)
        dmixed_b = dmixed.astype(BF16)
        dw_ref[...] = _tn(pooled_b, dmixed_b)
        dpooled = _nt(dmixed_b, w_ref[...])
        sums, acc, k = [], dpooled * inv_count, 1
        for _ in POOL_WINDOWS:
            acc = acc + _shift_up(acc, k)
            sums.append(acc)
            k *= 2
        dp_ref[...] = (_pool_select(grp, sums) - dpooled).astype(BF16)

    vec = pl.BlockSpec((1, POOL_WIDTH), lambda j: (0, 0))
    mat = pl.BlockSpec((POOL_WIDTH, POOL_WIDTH), lambda j: (0, 0))
    pcol = pl.BlockSpec((s, POOL_WIDTH), lambda j: (0, OFF_P // POOL_WIDTH))
    return pl.pallas_call(
        body, name=name, grid=(1,),
        in_specs=[pcol, pl.BlockSpec((s, POOL_WIDTH), lambda j: (0, (SSD_WIDTH + SB_WIDTH) // POOL_WIDTH)), mat, vec, vec, ANY],
        out_specs=[pcol, mat, vec, vec],
        out_shape=[jax.ShapeDtypeStruct(dproj.shape, BF16), jax.ShapeDtypeStruct((POOL_WIDTH, POOL_WIDTH), F32),
                   jax.ShapeDtypeStruct((1, POOL_WIDTH), F32), jax.ShapeDtypeStruct((1, POOL_WIDTH), F32)],
        input_output_aliases={5: 0},
        compiler_params=_params(("arbitrary",)),
    )(proj, dyall, wbd, pb, ps, dproj)


N_PAIRS = SSD_HEADS // 2


def _ssd_common(xc, dtraw, dtb, alog):
    c = CHUNK
    dt = _softplus(dtraw + dtb)
    a = -jnp.exp(alog)
    ltri = (_iota2((c, c), 0) >= _iota2((c, c), 1)).astype(BF16)
    acum = _split_dot_left(ltri, dt * a)
    expand = (_iota2((c, SSD_WIDTH), 1) // HEAD_DIM == _iota2((c, SSD_WIDTH), 0)).astype(BF16)
    expand_wide = (_iota2((c, SSD_HEADS * c), 1) // c == _iota2((c, SSD_HEADS * c), 0)).astype(BF16)
    acum_x = _split_dot(acum, expand, 2)
    dt_x = _split_dot(dt, expand, 2)
    alast_x = acum_x[c - 1:c, :]
    return dict(dt=dt, a=a, acum=acum, acum_x=acum_x, dt_x=dt_x, ea_x=jnp.exp(acum_x),
                dte_x=jnp.exp(alast_x - acum_x), eal_x=jnp.exp(alast_x),
                acol=_split_dot(acum, expand_wide, 2), acum_t=acum.T,
                xs=xc[:, :SSD_WIDTH], causal=_iota2((c, c), 0) >= _iota2((c, c), 1),
                left=_iota2((c, c), 1) < HEAD_DIM)


def _ssd_group(xc, g):
    b = xc[:, SSD_WIDTH + D_STATE * g:SSD_WIDTH + D_STATE * (g + 1)]
    cm = xc[:, SSD_WIDTH + 2 * D_STATE + D_STATE * g:SSD_WIDTH + 2 * D_STATE + D_STATE * (g + 1)]
    return b, cm


def _ssd_decay(q, hh):
    col = q["acol"][:, CHUNK * hh:CHUNK * (hh + 1)]
    row = q["acum_t"][hh:hh + 1, :]
    return jnp.where(q["causal"], jnp.exp(jnp.minimum(col - row, 0.0)), 0.0)


def ssd_forward(proj, xc, dtb, alog, dskip_x, nw, name):
    s = xc.shape[0]
    nc = s // CHUNK

    def body(xc_ref, zdt_ref, dtb_ref, alog_ref, dsk_ref, nw_ref, y_ref, yc_ref, st_ref, state):
        @pl.when(pl.program_id(0) == 0)
        def _():
            state[...] = jnp.zeros_like(state)

        xcv = xc_ref[...]
        q = _ssd_common(xcv, zdt_ref[:, SSD_WIDTH:SSD_WIDTH + LANES], dtb_ref[...], alog_ref[...])
        x = q["xs"] * q["dt_x"]
        xb = x.astype(BF16)
        xd = (x * q["dte_x"]).astype(BF16)
        pieces = []
        for g in range(2):
            bg, cg = _ssd_group(xcv, g)
            bgb, cgb = bg.astype(BF16), cg.astype(BF16)
            cb = _nt(cgb, bgb)
            bgt = bg.T.astype(BF16)
            for pr in (2 * g, 2 * g + 1):
                sl = slice(CHUNK * pr, CHUNK * (pr + 1))
                st = state[pr]
                st_ref[0, pr] = st
                yp = _nn(cgb, st.astype(BF16)) * q["ea_x"][:, sl]
                for k, hh in enumerate((2 * pr, 2 * pr + 1)):
                    w = (cb * _ssd_decay(q, hh)).astype(BF16)
                    mask = q["left"] if k == 0 else jnp.logical_not(q["left"])
                    yp = yp + _nn(w, jnp.where(mask, xb[:, sl], jnp.zeros_like(xb[:, sl])))
                state[pr] = st * q["eal_x"][:, sl] + _nn(bgt, xd[:, sl])
                pieces.append(yp)
        y = jnp.concatenate(pieces, axis=1) + q["xs"] * dsk_ref[...]
        yc_ref[...] = y
        zv = zdt_ref[:, :SSD_WIDTH]
        yg = y * (zv * _sigmoid(zv))
        r = lax.rsqrt(jnp.mean(yg * yg, axis=-1, keepdims=True) + EPS)
        y_ref[...] = (yg * r * nw_ref[...]).astype(BF16)

    vec = lambda n: pl.BlockSpec((1, n), lambda c: (0, 0))
    return pl.pallas_call(
        body, name=name, grid=(nc,),
        in_specs=[pl.BlockSpec((CHUNK, CONV_DIM), lambda c: (c, 0)),
                  pl.BlockSpec((CHUNK, ZDT), lambda c: (c, OFF_Z // ZDT)),
                  vec(LANES), vec(LANES), vec(SSD_WIDTH), vec(SSD_WIDTH)],
        out_specs=[pl.BlockSpec((CHUNK, SSD_WIDTH), lambda c: (c, 0)), pl.BlockSpec((CHUNK, SSD_WIDTH), lambda c: (c, 0)),
                   pl.BlockSpec((1, N_PAIRS, D_STATE, CHUNK), lambda c: (c, 0, 0, 0))],
        out_shape=[jax.ShapeDtypeStruct((s, D_MODEL), BF16), jax.ShapeDtypeStruct((s, SSD_WIDTH), F32),
                   jax.ShapeDtypeStruct((nc, N_PAIRS, D_STATE, CHUNK), F32)],
        scratch_shapes=[pltpu.VMEM((N_PAIRS, D_STATE, CHUNK), F32)],
        compiler_params=_params(("arbitrary",)),
    )(xc, proj, dtb, alog, dskip_x, nw)


def ssd_backward(proj, xc, ycore, dyall, states, dtb, alog, dskip_x, nw, name):
    s = xc.shape[0]
    nc = s // CHUNK
    c = CHUNK

    def body(xc_ref, zdt_ref, yc_ref, dy_ref, st_ref, dtb_ref, alog_ref, dsk_ref, nw_ref,
             dxc_ref, dzdt_ref, dnw_ref, ddsk_ref, ddtb_ref, dalog_ref, dstate):
        @pl.when(pl.program_id(0) == 0)
        def _():
            dstate[...] = jnp.zeros_like(dstate)
            dnw_ref[...] = jnp.zeros_like(dnw_ref)
            ddsk_ref[...] = jnp.zeros_like(ddsk_ref)
            ddtb_ref[...] = jnp.zeros_like(ddtb_ref)
            dalog_ref[...] = jnp.zeros_like(dalog_ref)

        xcv = xc_ref[...]
        dtraw = zdt_ref[:, SSD_WIDTH:SSD_WIDTH + LANES]
        q = _ssd_common(xcv, dtraw, dtb_ref[...], alog_ref[...])
        xs = q["xs"]
        x = xs * q["dt_x"]
        zv, yc, dy, nwv = zdt_ref[:, :SSD_WIDTH], yc_ref[...], dy_ref[...], nw_ref[...]
        sgz = _sigmoid(zv)
        siluz = zv * sgz
        yg = yc * siluz
        r = lax.rsqrt(jnp.mean(yg * yg, axis=-1, keepdims=True) + EPS)
        dnw_ref[...] += jnp.sum(dy * yg * r, axis=0, keepdims=True)
        g1 = dy * nwv
        dyg = r * (g1 - yg * (r * r) * jnp.mean(g1 * yg, axis=-1, keepdims=True))
        dyv = dyg * siluz
        dz = (dyg * yc * (sgz * (1.0 + zv * (1.0 - sgz)))).astype(BF16)
        ddsk_ref[...] += jnp.sum(dyv * xs, axis=0, keepdims=True)
        dye = dyv * q["ea_x"]
        dx_parts, yoff_parts, u_parts, v_parts, e_parts = [], [], [], [], []
        db_parts, dc_parts = [], []
        for g in range(2):
            bg, cg = _ssd_group(xcv, g)
            bgb, cgb = bg.astype(BF16), cg.astype(BF16)
            cb = _nt(cgb, bgb)
            cgt = cg.T.astype(BF16)
            dgsum = jnp.zeros((c, c), F32)
            dbg = jnp.zeros((c, D_STATE), F32)
            dcg = jnp.zeros((c, D_STATE), F32)
            for pr in (2 * g, 2 * g + 1):
                sl = slice(c * pr, c * (pr + 1))
                st = st_ref[0, pr]
                dst = dstate[pr]
                stb, dstb = st.astype(BF16), dst.astype(BF16)
                xp = x[:, sl]
                xpb = xp.astype(BF16)
                dyp = dyv[:, sl]
                xdp = xp * q["dte_x"][:, sl]
                yoff_parts.append(_nn(cgb, stb) * q["ea_x"][:, sl])
                rr = _nn(bgb, dstb)
                dxp = rr * q["dte_x"][:, sl]
                u_parts.append(rr * xdp)
                v_parts.append(dst * st * q["eal_x"][:, sl])
                for k, hh in enumerate((2 * pr, 2 * pr + 1)):
                    decay = _ssd_decay(q, hh)
                    w = cb * decay
                    mask = q["left"] if k == 0 else jnp.logical_not(q["left"])
                    dym = jnp.where(mask, dyp, 0.0).astype(BF16)
                    dw = _nt(dym, xpb)
                    dgsum = dgsum + dw * decay
                    e_parts.append(dw * w)
                    dxp = dxp + _nn(w.T.astype(BF16), dym)
                dyeb = dye[:, sl].astype(BF16)
                dcg = dcg + _nt(dyeb, stb)
                dbg = dbg + _nt(xdp.astype(BF16), dstb)
                dstate[pr] = dst * q["eal_x"][:, sl] + _nn(cgt, dyeb)
                dx_parts.append(dxp)
            dcg = dcg + _nn(dgsum.astype(BF16), bgb)
            dbg = dbg + _nn(dgsum.T.astype(BF16), cgb)
            db_parts.append(dbg)
            dc_parts.append(dcg)
        dx = jnp.concatenate(dx_parts, axis=1)
        yoff = jnp.concatenate(yoff_parts, axis=1)
        u = jnp.concatenate(u_parts, axis=1)
        v = jnp.concatenate(v_parts, axis=1)
        reduce_heads = (_iota2((SSD_WIDTH, c), 0) // HEAD_DIM == _iota2((SSD_WIDTH, c), 1)).astype(BF16)
        to_head = (_iota2((SSD_HEADS * c, c), 0) // c == _iota2((SSD_HEADS * c, c), 1)).astype(BF16)
        da = _split_dot(dyv * yoff - u, reduce_heads, 2)
        da = da + _split_dot(jnp.concatenate(e_parts, axis=1), to_head, 2)
        da = da - _split_dot(jnp.concatenate(e_parts, axis=0), to_head, 2, dot=_tn)
        dalast = jnp.sum(_split_dot(u + v, reduce_heads, 2), axis=0, keepdims=True)
        da = da + jnp.where(_iota2((c, c), 0) == c - 1, dalast, 0.0)
        utri = (_iota2((c, c), 1) >= _iota2((c, c), 0)).astype(BF16)
        dda = _split_dot_left(utri, da)
        ddt = dda * q["a"] + _split_dot(dx * xs, reduce_heads, 2)
        dalog_ref[...] += jnp.sum(dda * q["dt"], axis=0, keepdims=True) * q["a"]
        ddtraw = jnp.where(_iota2((c, c), 1) < SSD_HEADS, ddt * _sigmoid(dtraw + dtb_ref[...]), 0.0)
        ddtb_ref[...] += jnp.sum(ddtraw, axis=0, keepdims=True)
        dzdt_ref[...] = jnp.concatenate([dz, ddtraw.astype(BF16), jnp.zeros((c, ZDT - SSD_WIDTH - LANES), BF16)], axis=1)
        dxs = dx * q["dt_x"] + dyv * dsk_ref[...]
        dxc_ref[...] = jnp.concatenate([dxs] + db_parts + dc_parts, axis=1)

    rev = lambda i: nc - 1 - i
    vec = lambda n: pl.BlockSpec((1, n), lambda i: (0, 0))
    wide = pl.BlockSpec((c, SSD_WIDTH), lambda i: (rev(i), 0))
    zdt = pl.BlockSpec((c, ZDT), lambda i: (rev(i), OFF_Z // ZDT))
    return pl.pallas_call(
        body, name=name, grid=(nc,),
        in_specs=[pl.BlockSpec((c, CONV_DIM), lambda i: (rev(i), 0)), zdt, wide, wide,
                  pl.BlockSpec((1, N_PAIRS, D_STATE, c), lambda i: (rev(i), 0, 0, 0)),
                  vec(LANES), vec(LANES), vec(SSD_WIDTH), vec(SSD_WIDTH)],
        out_specs=[pl.BlockSpec((c, CONV_DIM), lambda i: (rev(i), 0)), zdt,
                   vec(SSD_WIDTH), vec(SSD_WIDTH), vec(LANES), vec(LANES)],
        out_shape=[jax.ShapeDtypeStruct((s, CONV_DIM), F32), jax.ShapeDtypeStruct((s, D_INP), BF16),
                   jax.ShapeDtypeStruct((1, SSD_WIDTH), F32),
                   jax.ShapeDtypeStruct((1, SSD_WIDTH), F32), jax.ShapeDtypeStruct((1, LANES), F32),
                   jax.ShapeDtypeStruct((1, LANES), F32)],
        scratch_shapes=[pltpu.VMEM((N_PAIRS, D_STATE, c), F32)],
        compiler_params=_params(("arbitrary",)),
    )(xc, proj, ycore, dyall, states, dtb, alog, dskip_x, nw)


SB_Q, SB_K = 512, 512
SB_T = 256
SB_SCALE = HEAD_DIM ** -0.5


def _key_suffix(x, tri, terms):
    runs = [x[:, SB_T * k:SB_T * (k + 1)] for k in range(SB_K // SB_T)]
    sums = [_split_dot(r, tri, terms) for r in runs]
    later = None
    for k in range(len(runs) - 1, -1, -1):
        if later is not None:
            sums[k] = sums[k] + later
        total = jnp.sum(runs[k], axis=1, keepdims=True)
        later = total if later is None else later + total
    return jnp.concatenate(sums, axis=1), later


def _sb_weights(qm, kb, diagonal, run_lk, strict_after):
    z = _nt(qm, kb)
    nz = -z
    tail = jnp.log(1.0 + jnp.exp(jnp.minimum(z, nz)))
    ls = jnp.minimum(z, 0.0) - tail
    lk = jnp.minimum(nz, 0.0) - tail
    if diagonal is not None:
        valid = _iota2(z.shape, 1) < _iota2(z.shape, 0) + diagonal
        lk = jnp.where(valid, lk, 0.0)
    after, total = _key_suffix(lk, strict_after, 1)
    w = jnp.exp(ls + after + run_lk)
    if diagonal is not None:
        w = jnp.where(valid, w, 0.0)
    return ls, total, w


def _sb_sweep(i, block, init):
    own = (i * SB_Q) // SB_K
    first = block(own, init, i * SB_Q - own * SB_K)
    return lax.fori_loop(1, own + 1, lambda jj, carry: block(own - jj, carry, None), first)


def sb_forward(proj, y_all, name, after=None):
    s = proj.shape[0]
    t, tk = SB_Q, SB_K
    nq = s // t
    specs, ops = _after(after)

    def body(q_ref, k_ref, v_ref, *rest):
        y_ref, o_ref = rest[-2:]
        i = pl.program_id(1)
        left = _iota2((t, LANES), 1) < HEAD_DIM
        left_k = _iota2((tk, LANES), 1) < HEAD_DIM
        qv = q_ref[...] * SB_SCALE
        zero = jnp.zeros_like(qv)
        qms = (jnp.where(left, qv, zero).astype(BF16), jnp.where(left, zero, qv).astype(BF16))
        strict_after = (_iota2((SB_T, SB_T), 0) > _iota2((SB_T, SB_T), 1)).astype(BF16)

        def block(j, carry, diagonal):
            o, runs = carry[0], carry[1:]
            rows = pl.ds(pl.multiple_of(j * tk, tk), tk)
            kb = k_ref[rows, :].astype(BF16)
            vv = v_ref[rows, :]
            new_runs = []
            for k in range(2):
                _, total, w = _sb_weights(qms[k], kb, diagonal, runs[k], strict_after)
                vm = jnp.where(left_k if k == 0 else jnp.logical_not(left_k), vv, 0.0).astype(BF16)
                o = o + _nn(w.astype(BF16), vm)
                new_runs.append(runs[k] + total)
            return (o, *new_runs)

        init = (jnp.zeros((t, LANES), F32), jnp.zeros((t, 1), F32), jnp.zeros((t, 1), F32))
        o = _sb_sweep(i, block, init)[0]
        o_ref[...] = o
        y_ref[...] = o.astype(BF16)

    return pl.pallas_call(
        body, name=name, grid=(2, nq),
        in_specs=[pl.BlockSpec((t, LANES), lambda p, i: (i, 3 * p)),
                  pl.BlockSpec((s, LANES), lambda p, i: (0, 3 * p + 1)),
                  pl.BlockSpec((s, LANES), lambda p, i: (0, 3 * p + 2)), ANY] + specs,
        out_specs=[pl.BlockSpec((t, LANES), lambda p, i: (i, SSD_WIDTH // LANES + p)),
                   pl.BlockSpec((t, LANES), lambda p, i: (i, p))],
        out_shape=[jax.ShapeDtypeStruct(y_all.shape, BF16), jax.ShapeDtypeStruct((s, SB_WIDTH), F32)],
        input_output_aliases={3: 0},
        compiler_params=_params(("parallel", "arbitrary")),
    )(proj, proj, proj, y_all, *ops)


def sb_backward(proj, o, dyall, dproj, name, after=None):
    s = proj.shape[0]
    t, tk = SB_Q, SB_K
    nq = s // t
    specs, ops = _after(after)

    def body(q_ref, k_ref, v_ref, o_ref, do_ref, *rest):
        dqkv_ref, dk_acc, dv_acc = rest[-3:]
        dk_acc[...] = jnp.zeros_like(dk_acc)
        dv_acc[...] = jnp.zeros_like(dv_acc)
        left = _iota2((t, LANES), 1) < HEAD_DIM
        lane_masks = (left, jnp.logical_not(left))
        left_k = _iota2((tk, LANES), 1) < HEAD_DIM
        key_masks = (left_k, jnp.logical_not(left_k))
        strict_after = (_iota2((SB_T, SB_T), 0) > _iota2((SB_T, SB_T), 1)).astype(BF16)
        from_here = (_iota2((SB_T, SB_T), 0) >= _iota2((SB_T, SB_T), 1)).astype(BF16)

        def query_block(i, _):
            qrows = pl.ds(pl.multiple_of(i * t, t), t)
            qv = q_ref[qrows, :] * SB_SCALE
            dov = do_ref[qrows, :]
            zero = jnp.zeros_like(qv)
            qb = qv.astype(BF16)
            dob = dov.astype(BF16)
            prod = dob.astype(F32) * o_ref[qrows, :]
            qms = [jnp.where(m, qv, zero).astype(BF16) for m in lane_masks]
            doms = [jnp.where(m, dov, zero).astype(BF16) for m in lane_masks]
            deltas = [jnp.sum(jnp.where(m, prod, zero), axis=1, keepdims=True) for m in lane_masks]

            def block(j, carry, diagonal):
                dq = carry[0]
                run_lk, run_e = carry[1:3], carry[3:5]
                rows = pl.ds(pl.multiple_of(j * tk, tk), tk)
                kb = k_ref[rows, :].astype(BF16)
                vb = v_ref[rows, :].astype(BF16)
                dkj = jnp.zeros((tk, LANES), F32)
                dvj = jnp.zeros((tk, LANES), F32)
                new_lk, new_e = [], []
                for k in range(2):
                    ls, total, w = _sb_weights(qms[k], kb, diagonal, run_lk[k], strict_after)
                    wb = w.astype(BF16)
                    e = _nt(doms[k], vb) * wb.astype(F32)
                    e_from_here, e_total = _key_suffix(e, from_here, 2)
                    before = deltas[k] - e_from_here - run_e[k]
                    dz = e - jnp.exp(ls) * (e + before)
                    if diagonal is not None:
                        dz = jnp.where(_iota2(dz.shape, 1) < _iota2(dz.shape, 0) + diagonal, dz, 0.0)
                    dz = dz.astype(BF16)
                    m = lane_masks[k]
                    dvj = dvj + jnp.where(key_masks[k], _tn(wb, dob), 0.0)
                    dkj = dkj + jnp.where(key_masks[k], _tn(dz, qb), 0.0)
                    dq = dq + jnp.where(m, _nn(dz, kb), 0.0)
                    new_lk.append(run_lk[k] + total)
                    new_e.append(run_e[k] + e_total)
                dk_acc[rows, :] += dkj
                dv_acc[rows, :] += dvj
                return (dq, *new_lk, *new_e)

            col = jnp.zeros((t, 1), F32)
            dq = _sb_sweep(i, block, (jnp.zeros((t, LANES), F32), col, col, col, col))[0]
            dqkv_ref[qrows, 0:LANES] = (dq * SB_SCALE).astype(BF16)
            return 0

        lax.fori_loop(0, nq, query_block, 0)
        dqkv_ref[:, LANES:2 * LANES] = dk_acc[...].astype(BF16)
        dqkv_ref[:, 2 * LANES:3 * LANES] = dv_acc[...].astype(BF16)

    col = lambda f: pl.BlockSpec((s, LANES), f)
    return pl.pallas_call(
        body, name=name, grid=(2,),
        in_specs=[col(lambda p: (0, 3 * p)), col(lambda p: (0, 3 * p + 1)), col(lambda p: (0, 3 * p + 2)),
                  col(lambda p: (0, p)), col(lambda p: (0, SSD_WIDTH // LANES + p)), ANY] + specs,
        out_specs=pl.BlockSpec((s, 3 * LANES), lambda p: (0, p)),
        out_shape=jax.ShapeDtypeStruct(dproj.shape, BF16),
        input_output_aliases={5: 0},
        scratch_shapes=[pltpu.VMEM((s, LANES), F32), pltpu.VMEM((s, LANES), F32)],
        compiler_params=_params(("parallel",)),
    )(proj, proj, proj, o, dyall, dproj, *ops)


def adamw(w, g, m, v, name):
    b, r, c = w.shape
    tr = max([t for t in range(8, min(r, 512) + 1, 8) if r % t == 0], default=r)

    def body(w_ref, g_ref, m_ref, v_ref, d_ref, nm_ref, nv_ref):
        gv = g_ref[...]
        nm = ADAM_B1 * m_ref[...] + (1.0 - ADAM_B1) * gv
        nv = ADAM_B2 * v_ref[...] + (1.0 - ADAM_B2) * (gv * gv)
        m_hat = nm / (1.0 - ADAM_B1 ** ADAM_STEP)
        v_hat = nv / (1.0 - ADAM_B2 ** ADAM_STEP)
        d_ref[...] = -ADAM_LR * (m_hat / (jnp.sqrt(v_hat) + ADAM_EPS) + ADAM_WD * w_ref[...])
        nm_ref[...] = nm
        nv_ref[...] = nv

    blk = pl.BlockSpec((1, tr, c), lambda i, j: (i, j, 0))
    return pl.pallas_call(
        body, name=name, grid=(b, r // tr),
        in_specs=[blk] * 4, out_specs=[blk] * 3,
        out_shape=[jax.ShapeDtypeStruct(w.shape, F32)] * 3,
        compiler_params=_params(("parallel", "parallel")),
    )(w, g, m, v)


def _position():
    return lax.axis_index("x"), lax.axis_index("y"), lax.axis_index("c")


def _flipped(pos, flip):
    return tuple((1 - p) if f else p for p, f in zip(pos, flip))


FLIP_C = (0, 0, 1)
CHIP_FLIPS = {1: (0, 1, 0), 2: (1, 0, 0), 3: (1, 1, 0)}
SHARD_ROWS = (SHARD_IN, SHARD_OUT, SHARD_FF, SHARD_FF, SHARD_FF)


def _rows(start, size):
    return pl.ds(pl.multiple_of(start, 16), size)


HBM = pl.BlockSpec(memory_space=pltpu.HBM)
SEM = pl.BlockSpec(memory_space=pltpu.SEMAPHORE)
EFFECT = pltpu.SideEffectType.DATAFLOW_SIDE_EFFECTING


def _in_hbm(a):
    return pltpu.with_memory_space_constraint(a, pltpu.HBM)


def _landing(shape, dtype):
    return _in_hbm(lax.empty(shape, dtype))


def _copies(plan, pos, src_refs, land_refs, send_sems, recv_sems):
    return [pltpu.make_async_remote_copy(src_ref=src, dst_ref=dst, send_sem=send_sems.at[k], recv_sem=recv_sems.at[k],
                                         device_id=_flipped(pos, flip), device_id_type=MESH)
            for k, (src, dst, flip) in enumerate(plan(pos, src_refs, land_refs))]


def exchange_start(name, srcs, lands, n, plan, after=None):
    ns, nl = len(srcs), len(lands)
    specs, ops = _after(after)

    def body(*refs):
        src_refs, land_refs = refs[:ns], refs[ns:ns + nl]
        send_sems, recv_sems, token = refs[ns + nl + len(ops)], refs[ns + nl + len(ops) + 1], refs[-1]
        for cp in _copies(plan, _position(), src_refs, land_refs, send_sems, recv_sems):
            cp.start()
        token[...] = jnp.zeros_like(token)

    thru = [pltpu.HBM(a.shape, a.dtype) for a in list(srcs) + list(lands)]
    out = pl.pallas_call(
        body, name=name,
        out_shape=(pltpu.SemaphoreType.DMA((n,)), pltpu.SemaphoreType.DMA((n,)), *thru, jax.ShapeDtypeStruct((8, LANES), F32)),
        in_specs=[HBM] * (ns + nl) + specs,
        out_specs=(SEM, SEM, *([HBM] * (ns + nl)), pl.BlockSpec(memory_space=pltpu.VMEM)),
        input_output_aliases={k: 2 + k for k in range(ns + nl)},
        compiler_params=pltpu.CompilerParams(has_side_effects=EFFECT),
    )(*[_in_hbm(a) for a in srcs], *lands, *ops)
    return out[0], out[1], list(out[2:2 + ns]), list(out[2 + ns:2 + ns + nl]), out[-1]


def exchange_wait(name, started, after, plan):
    send_sems, recv_sems, srcs, lands, _ = started
    ns, nl = len(srcs), len(lands)
    specs, ops = _after(after)

    def body(*refs):
        src_refs, land_refs = refs[:ns], refs[ns:ns + nl]
        send_sems, recv_sems = refs[ns + nl], refs[ns + nl + 1]
        for cp in _copies(plan, _position(), src_refs, land_refs, send_sems, recv_sems):
            cp.wait_send()
            cp.wait_recv()

    out = pl.pallas_call(
        body, name=name,
        out_shape=tuple(pltpu.HBM(a.shape, a.dtype) for a in list(srcs) + list(lands)),
        in_specs=[HBM] * (ns + nl) + [SEM, SEM] + specs,
        out_specs=tuple([HBM] * (ns + nl)),
        input_output_aliases={k: k for k in range(ns + nl)},
        compiler_params=pltpu.CompilerParams(has_side_effects=EFFECT),
    )(*srcs, *lands, send_sems, recv_sems, *ops)
    return list(out[:ns]), list(out[ns:])


def _gather_ici_plan(pos, srcs, lands):
    chip, c = 2 * pos[0] + pos[1], pos[2]
    copies = []
    for src, dst in zip(srcs, lands):
        r = src.shape[0]
        h = r // 2
        for f in (1, 2, 3):
            copies.append((src.at[_rows(c * h, h)], dst.at[_rows(chip * r + c * h, h)], CHIP_FLIPS[f]))
    return copies


def _gather_d2d_plan(pos, srcs, lands):
    chip, c = 2 * pos[0] + pos[1], pos[2]
    copies = []
    for own, dst in zip(srcs, lands):
        r = own.shape[0]
        h = r // 2
        copies.append((own, dst.at[_rows(chip * r, r)], FLIP_C))
        for f in (1, 2, 3):
            at = _rows(lax.bitwise_xor(chip, f) * r + c * h, h)
            copies.append((dst.at[at], dst.at[at], FLIP_C))
    return copies


def gather_ici_start(shards, after=None):
    lands = [_landing((N_CHIPS * a.shape[0], D_MODEL), BF16) for a in shards]
    return exchange_start("gather_ici_start", shards, lands, 3 * len(shards), _gather_ici_plan, after=after)


def gather_d2d_start(shards, fulls, after=None):
    return exchange_start("gather_d2d_start", shards, fulls, 4 * len(shards), _gather_d2d_plan, after=after)


def _reduce_d2d_plan(pos, srcs, lands):
    c = pos[2]
    return [(src.at[:, _rows((1 - c) * (src.shape[1] // 2), src.shape[1] // 2)], dst, FLIP_C) for src, dst in zip(srcs, lands)]


def _reduce_ici_plan(pos, srcs, lands):
    chip = 2 * pos[0] + pos[1]
    return [(src.at[lax.bitwise_xor(chip, f)], dst.at[f - 1], CHIP_FLIPS[f]) for src, dst in zip(srcs, lands) for f in (1, 2, 3)]


def _reduce_swap_plan(pos, srcs, lands):
    c = pos[2]
    copies = []
    for dst in lands:
        h = dst.shape[0] // 2
        at = _rows(c * h, h)
        copies.append((dst.at[at], dst.at[at], FLIP_C))
    return copies


def reduce_d2d_start(grads):
    lands = [_landing((N_CHIPS, g.shape[1] // 2, D_MODEL), BF16) for g in grads]
    return exchange_start("reduce_d2d_start", grads, lands, len(grads), _reduce_d2d_plan)


def reduce_ici_start(chip_sums):
    lands = [_landing((N_CHIPS - 1,) + p.shape[1:], BF16) for p in chip_sums]
    return exchange_start("reduce_ici_start", chip_sums, lands, 3 * len(chip_sums), _reduce_ici_plan)


def reduce_swap_start(mine):
    return exchange_start("reduce_swap_start", [], mine, len(mine), _reduce_swap_plan)


def _by_shape(fn, *lists):
    groups, out = {}, [None] * len(lists[0])
    for k, a in enumerate(lists[0]):
        groups.setdefault(a.shape, []).append(k)
    for idx in groups.values():
        for k, r in zip(idx, fn(*[[l[k] for k in idx] for l in lists])):
            out[k] = r
    return out


def add_halves(ds, recvs, half, name):
    n = len(ds)
    nch, r, c = ds[0].shape
    h = r // 2

    def body(half_ref, *refs):
        for k in range(n):
            refs[2 * n + k][...] = (refs[k][...].astype(F32) + refs[n + k][...].astype(F32)).astype(BF16)

    mine = pl.BlockSpec((1, h, c), lambda j, hf: (j, hf[0], 0))
    whole = pl.BlockSpec((1, h, c), lambda j, hf: (j, 0, 0))
    return pl.pallas_call(
        body, name=name,
        grid_spec=pltpu.PrefetchScalarGridSpec(
            num_scalar_prefetch=1, grid=(nch,), in_specs=[mine] * n + [whole] * n, out_specs=[whole] * n),
        out_shape=[jax.ShapeDtypeStruct(rv.shape, BF16) for rv in recvs],
        compiler_params=_params(("parallel",)),
    )(half, *ds, *recvs)


def add_chips(ps, recvs, chip, name):
    n = len(ps)
    _, r, c = ps[0].shape

    def body(chip_ref, *refs):
        for k in range(n):
            acc = refs[k][0].astype(F32)
            for f in range(N_CHIPS - 1):
                acc = acc + refs[n + k][f].astype(F32)
            refs[2 * n + k][...] = acc

    return pl.pallas_call(
        body, name=name,
        grid_spec=pltpu.PrefetchScalarGridSpec(
            num_scalar_prefetch=1, grid=(1,),
            in_specs=[pl.BlockSpec((1, r, c), lambda i, ch: (ch[0], 0, 0))] * n +
                     [pl.BlockSpec((N_CHIPS - 1, r, c), lambda i, ch: (0, 0, 0))] * n,
            out_specs=[pl.BlockSpec((r, c), lambda i, ch: (ch[1], 0))] * n),
        out_shape=[jax.ShapeDtypeStruct((2 * r, c), F32)] * n,
        compiler_params=_params(("arbitrary",)),
    )(chip, *ps, *recvs)


def adamw_layers(w, gs, m, v, name):
    b, r, c = w.shape
    tr = max([t for t in range(8, min(r, 512) + 1, 8) if r % t == 0], default=r)

    def body(w_ref, m_ref, v_ref, *rest):
        g_refs, (g_ref, d_ref, nm_ref, nv_ref) = rest[:b], rest[b:]
        layer = pl.program_id(0)
        gv = g_refs[0][...]
        for l in range(1, b):
            gv = jnp.where(layer == l, g_refs[l][...], gv)
        nm = ADAM_B1 * m_ref[0] + (1.0 - ADAM_B1) * gv
        nv = ADAM_B2 * v_ref[0] + (1.0 - ADAM_B2) * (gv * gv)
        m_hat = nm / (1.0 - ADAM_B1 ** ADAM_STEP)
        v_hat = nv / (1.0 - ADAM_B2 ** ADAM_STEP)
        g_ref[0] = gv
        d_ref[0] = -ADAM_LR * (m_hat / (jnp.sqrt(v_hat) + ADAM_EPS) + ADAM_WD * w_ref[0])
        nm_ref[0] = nm
        nv_ref[0] = nv

    nr, tc = r // tr, (c if tr < r else _tile(c, 256))
    steps = nr * (c // tc)
    blk = pl.BlockSpec((1, tr, tc), lambda i, j: (i, j % nr, j // nr))
    g_specs = [pl.BlockSpec((tr, tc), lambda i, j, l=l: (jnp.where(i == l, j % nr, jnp.where(i < l, 0, nr - 1)),
                                                         jnp.where(i == l, j // nr, jnp.where(i < l, 0, c // tc - 1))))
               for l in range(b)]
    return pl.pallas_call(
        body, name=name, grid=(b, steps),
        in_specs=[blk] * 3 + g_specs, out_specs=[blk] * 4,
        out_shape=[jax.ShapeDtypeStruct(w.shape, F32)] * 4,
        compiler_params=_params(("arbitrary", "arbitrary")),
    )(w, m, v, *gs)


def small_allreduce(v, name, after=None):
    r, c = v.shape
    specs, ops = _after(after)

    def body(v_ref, *rest):
        o_ref, buf, send_sems, recv_sems = rest[len(ops):]
        pos = _position()
        me = 4 * pos[0] + 2 * pos[1] + pos[2]
        buf[0] = v_ref[...]
        copies = []
        for f in range(1, 8):
            flip = ((f >> 2) & 1, (f >> 1) & 1, f & 1)
            cp = pltpu.make_async_remote_copy(
                src_ref=v_ref, dst_ref=buf.at[f], send_sem=send_sems.at[f - 1], recv_sem=recv_sems.at[f - 1],
                device_id=_flipped(pos, flip), device_id_type=MESH)
            cp.start()
            copies.append(cp)
        for cp in copies:
            cp.wait()
        acc = buf[me]
        for d in range(1, 8):
            acc = acc + buf[lax.bitwise_xor(me, d)]
        o_ref[...] = acc

    return pl.pallas_call(
        body, name=name,
        in_specs=[pl.BlockSpec(memory_space=pltpu.VMEM)] + specs, out_specs=pl.BlockSpec(memory_space=pltpu.VMEM),
        out_shape=jax.ShapeDtypeStruct((r, c), F32),
        scratch_shapes=[pltpu.VMEM((8, r, c), F32), pltpu.SemaphoreType.DMA((7,)), pltpu.SemaphoreType.DMA((7,))],
    )(v, *ops)


def _all_devices_plan(pos, srcs, lands):
    return [(srcs[0], lands[0].at[f], ((f >> 2) & 1, (f >> 1) & 1, f & 1)) for f in range(1, 8)]


def sum_devices(v, gathered, me, name):
    r, c = v.shape

    def body(me_ref, v_ref, g_ref, o_ref):
        own = v_ref[...]
        acc = None
        for d in range(8):
            slot = lax.bitwise_xor(me_ref[0], d)
            term = jnp.where(slot == 0, own, g_ref[slot])
            acc = term if acc is None else acc + term
        o_ref[...] = acc

    return pl.pallas_call(
        body, name=name,
        grid_spec=pltpu.PrefetchScalarGridSpec(
            num_scalar_prefetch=1, grid=(1,),
            in_specs=[pl.BlockSpec((r, c), lambda i, m: (0, 0)), pl.BlockSpec((8, r, c), lambda i, m: (0, 0, 0))],
            out_specs=pl.BlockSpec((r, c), lambda i, m: (0, 0))),
        out_shape=jax.ShapeDtypeStruct((r, c), F32),
        compiler_params=_params(("arbitrary",)),
    )(me, v, gathered)


_IN_SEGMENTS = ((0, 1544, 128), (128, 1800, 128), (256, 2056, 128), (384, 1672, 128), (512, 1928, 128), (640, 2184, 128),
                (OFF_Z, 0, SSD_WIDTH), (OFF_DT, 1536, SSD_HEADS), (OFF_XBC, 512, CONV_DIM), (OFF_P, 2312, POOL_WIDTH))


def _in_column_map():
    m = np.full((D_INP,), -1, np.int64)
    for at, orig, n in _IN_SEGMENTS:
        cols = np.arange(orig, orig + n)
        m[at:at + n] = (cols // COLS_IN) * SHARD_IN + cols % COLS_IN
    return m


def take_rows(a, idx, name):
    dep, r_in, c = a.shape
    blk = LANES
    n_out, n_in = len(idx) // blk, r_in // blk
    assert len(idx) % blk == 0 and r_in % blk == 0
    sources = [sorted({int(v) // blk for v in idx[blk * i:blk * (i + 1)] if v >= 0}) for i in range(n_out)]
    width = max(len(s) for s in sources)
    table = np.zeros((n_out, width), np.int32)
    for i, s in enumerate(sources):
        spare = [b for b in range(n_in) if b not in s][:width - len(s)]
        table[i] = s + spare

    def body(tbl_ref, idx_ref, *refs):
        in_refs, o_ref = refs[:width], refs[width]
        i = pl.program_id(1)
        src = idx_ref[...]
        acc = jnp.zeros((blk, c), F32)
        for k in range(width):
            pick = (src == tbl_ref[i, k] * blk + _iota2((blk, blk), 1)).astype(BF16)
            acc = acc + _nn(pick, in_refs[k][0])
        o_ref[0] = acc.astype(BF16)

    return pl.pallas_call(
        body, name=name,
        grid_spec=pltpu.PrefetchScalarGridSpec(
            num_scalar_prefetch=1, grid=(dep, n_out),
            in_specs=[pl.BlockSpec((blk, 1), lambda l, i, t: (i, 0))] +
                     [pl.BlockSpec((1, blk, c), lambda l, i, t, k=k: (l, t[i, k], 0)) for k in range(width)],
            out_specs=pl.BlockSpec((1, blk, c), lambda l, i, t: (l, i, 0))),
        out_shape=jax.ShapeDtypeStruct((dep, len(idx), c), BF16),
        compiler_params=_params(("parallel", "parallel")),
    )(jnp.asarray(table), jnp.asarray(np.asarray(idx, np.int32).reshape(-1, 1)), *([a] * width))


def _in_weight_layout(staged):
    return take_rows(staged, _in_column_map(), "w_in_layout")


def _in_gradient_layout(dwt):
    fwd = _in_column_map()
    inv = np.full((N_CHIPS * SHARD_IN,), -1, np.int64)
    inv[fwd[fwd >= 0]] = np.nonzero(fwd >= 0)[0]
    return take_rows(dwt, inv, "dw_in_layout")


SMALL_NAMES = ("norm1_w", "conv_w", "conv_b", "dt_bias", "a_log", "d_skip", "ssd_norm_w", "pool_w", "pool_b",
               "pool_scale", "norm2_w", "final_norm_w")
SMALL_ROWS = 104


def _pack_small(parts):
    flat = jnp.concatenate([p.reshape(-1) for p in parts])
    return jnp.pad(flat, (0, SMALL_ROWS * D_MODEL - flat.shape[0])).reshape(SMALL_ROWS, D_MODEL)


def _unpack_small(flat, shapes):
    flat = flat.reshape(-1)
    out, at = [], 0
    for shp in shapes:
        n = int(np.prod(shp))
        out.append(flat[at:at + n].reshape(shp))
        at += n
    return out


def kernel(x, norm1_w, w_in, conv_w, conv_b, dt_bias, a_log, d_skip, ssd_norm_w, pool_w, pool_b, pool_scale, w_out, norm2_w, w_gate, w_up, w_down, final_norm_w, loss_target, m_norm1_w, m_w_in, m_conv_w, m_conv_b, m_dt_bias, m_a_log, m_d_skip, m_ssd_norm_w, m_pool_w, m_pool_b, m_pool_scale, m_w_out, m_norm2_w, m_w_gate, m_w_up, m_w_down, m_final_norm_w, v_norm1_w, v_w_in, v_conv_w, v_conv_b, v_dt_bias, v_a_log, v_d_skip, v_ssd_norm_w, v_pool_w, v_pool_b, v_pool_scale, v_w_out, v_norm2_w, v_w_gate, v_w_up, v_w_down, v_final_norm_w):
    px, py, pc = _position()
    chip = 2 * px + py
    chip_arr = jnp.reshape(chip, (1,)).astype(jnp.int32)
    half_arr = jnp.reshape(pc, (1,)).astype(jnp.int32)

    def layer_shards(l):
        w_in_t = jnp.pad(jnp.swapaxes(w_in[l], 0, 1).astype(BF16), ((0, SHARD_IN - COLS_IN), (0, 0)))
        return [w_in_t, w_out[l].astype(BF16), jnp.swapaxes(w_gate[l], 0, 1).astype(BF16),
                jnp.swapaxes(w_up[l], 0, 1).astype(BF16), w_down[l].astype(BF16)]

    shards0 = layer_shards(0)
    head = gather_ici_start(shards0[:1])
    over_ici = {}

    def pass_on(l, after):
        own, arrived = exchange_wait("gather_ici_wait", over_ici[l], after, _gather_ici_plan)
        swap = gather_d2d_start(own, arrived)
        tokens = [swap[4]]
        if l + 1 < DEPTH:
            over_ici[l + 1] = gather_ici_start(layer_shards(l + 1), after=swap[4])
            tokens.append(over_ici[l + 1][4])
        return swap, tokens

    def weights_of(swap, after):
        _, (w_in_st, w_out_l, w_gate_t, w_up_t, w_down_l) = exchange_wait("gather_d2d_wait", swap, after, _gather_d2d_plan)
        return _in_weight_layout(w_in_st[None])[0], w_out_l, w_gate_t, w_up_t, w_down_l

    pad_heads = lambda v: jnp.pad(v, ((0, 0), (0, LANES - SSD_HEADS)))[:, None, :]
    dtb, alog = pad_heads(dt_bias), pad_heads(a_log)
    dskip_x = jnp.repeat(d_skip, HEAD_DIM, axis=1)[:, None, :]
    eye = jnp.eye(len(POOL_WINDOWS), dtype=F32)
    wbd = (pool_w[:, :, :, None, :] * eye[None, :, None, :, None]).reshape(DEPTH, POOL_WIDTH, POOL_WIDTH).astype(BF16)
    pool_b2 = pool_b.reshape(DEPTH, 1, POOL_WIDTH)
    cw_cols = lax.dynamic_update_slice(jnp.zeros((DEPTH, CONV_WIDTH, CONV_DIM), F32), conv_w,
                                       (0, 0, chip * (CONV_DIM // N_CHIPS)))
    cw_cols = jnp.where(pc == 0, cw_cols, 0.0)
    cw_rows = (DEPTH * CONV_WIDTH * CONV_DIM) // D_MODEL
    conv_w_f = small_allreduce(jnp.pad(cw_cols.reshape(cw_rows, D_MODEL), ((0, 8), (0, 0))), "gather_conv_w")
    conv_w_f = conv_w_f[:cw_rows].reshape(DEPTH, CONV_WIDTH, CONV_DIM)
    cw8 = jnp.pad(conv_w_f, ((0, 0), (0, 8 - CONV_WIDTH), (0, 0)))

    h = x[0]
    saved, weights = [], []
    own, arrived = exchange_wait("gather_ici_wait", head, head[4], _gather_ici_plan)
    head = gather_d2d_start(own, arrived)
    over_ici[0] = gather_ici_start(shards0[1:], after=head[4])
    w_in_f = _in_weight_layout(exchange_wait("gather_d2d_wait", head, [head[4], over_ici[0][4]], _gather_d2d_plan)[1][0][None])[0]
    for l in range(DEPTH):
        if l > 0:
            w_in_f, w_out_f, w_gate_t, w_up_t, w_down_f = weights[l]
        proj = rms_matmul(h, norm1_w[l][None], w_in_f, "in_proj")
        xc = conv_forward(proj, cw8[l], conv_b[l][None], "conv_fwd")
        y_all, ycore, states = ssd_forward(proj, xc, dtb[l], alog[l], dskip_x[l], ssd_norm_w[l][None], "ssd_fwd")
        if l == 0:
            swap, tokens = pass_on(0, y_all)
            y_all, o_sb = sb_forward(proj, y_all, "sb_fwd", after=tokens)
            tokens = None
        else:
            y_all, o_sb = sb_forward(proj, y_all, "sb_fwd")
            swap, tokens = pass_on(l + 1, o_sb) if l + 1 < DEPTH else (None, None)
        y_all = pool_forward(proj, wbd[l], pool_b2[l], pool_scale[l][None], y_all, "pool_fwd", after=tokens)
        if l == 0:
            w_out_f, w_gate_t, w_up_t, w_down_f = exchange_wait("gather_d2d_wait", swap, y_all, _gather_d2d_plan)[1]
            weights.append((w_in_f, w_out_f, w_gate_t, w_up_t, w_down_f))
        x1 = matmul_residual(y_all, w_out_f, h, "out_proj")
        x2, g, u = ffn_forward(x1, norm2_w[l][None], w_gate_t, w_up_t, w_down_f, "ffn_fwd")
        if l == 0:
            swap, tokens = pass_on(1, x2)
            weights.append(weights_of(swap, tokens))
        elif swap is not None:
            weights.append(weights_of(swap, x2))
        saved.append((h, proj, xc, ycore, states, o_sb, y_all, x1, g, u))
        h = x2

    loss_part, dx, dxb, d_final = loss_head(h, final_norm_w[None], loss_target[0], "loss_head")
    loss = lax.psum(loss_part[0, 0], ("x", "y", "c"))

    small = {n: [None] * DEPTH for n in SMALL_NAMES if n != "final_norm_w"}
    chip_half = jnp.concatenate([chip_arr, half_arr])
    reduced = {}
    d2d = ici = early = None

    def add_cores(d2d, after):
        mine, theirs = exchange_wait("reduce_d2d_wait", d2d[1], after, _reduce_d2d_plan)
        return d2d[0], reduce_ici_start(_by_shape(lambda ds, ts: add_halves(ds, ts, half_arr, "reduce_add_halves"), mine, theirs))

    def add_all(ici, after):
        sums, theirs = exchange_wait("reduce_ici_wait", ici[1], after, _reduce_ici_plan)
        return ici[0], reduce_swap_start(_by_shape(lambda ps, ts: add_chips(ps, ts, chip_half, "reduce_add_chips"), sums, theirs))

    def finish(swap, after):
        reduced[swap[0]] = exchange_wait("reduce_swap_wait", swap[1], after, _reduce_swap_plan)[1]

    swaps = []
    for l in reversed(range(DEPTH)):
        xin, proj, xc, ycore, states, o_sb, y_all, x1, g, u = saved[l]
        w_in_f, w_out_f, w_gate_t, w_up_t, w_down_f = weights[l]
        dg, du, act = ffn_backward_act(dxb, g, u, w_down_f, "ffn_bwd_act", after=None if d2d is None else d2d[1][4])
        dx1, dx1b, h2, dn2 = rms_backward([dg, du], [w_gate_t, w_up_t], x1, norm2_w[l][None], dx, "ffn_bwd_norm", 512)
        if d2d is not None:
            ici = add_cores(d2d, dx1b)
        dyall = matmul_nt(dx1b, w_out_f, "out_proj_bwd", after=None if ici is None else ici[1][4])
        dw_down = matmul_tn(act, dxb, "dw_down")
        dw_gate = matmul_tn(dg, h2, "dw_gate")
        dw_up = matmul_tn(du, h2, "dw_up")
        dw_out = matmul_tn(y_all, dx1b, "dw_out")
        late = [dw.reshape(N_CHIPS, r, D_MODEL) for dw, r in zip((dw_out, dw_gate, dw_up, dw_down), SHARD_ROWS[1:])]
        if l == 0:
            early = ("0 late", reduce_d2d_start(late))
        dxc, dproj, dsn, ddsk, ddtb, dalog = ssd_backward(proj, xc, ycore, dyall, states, dtb[l], alog[l],
                                                          dskip_x[l], ssd_norm_w[l][None], "ssd_bwd")
        dproj, dcw, dcb = conv_backward(proj, dxc, cw8[l], conv_b[l][None], dproj, "conv_bwd",
                                        after=None if early is None else early[1][4])
        if early is not None:
            early = add_cores(early, dproj)
        dproj = sb_backward(proj, o_sb, dyall, dproj, "sb_bwd", after=None if early is None else early[1][4])
        dproj, dwbd, dpb, dps = pool_backward(proj, dyall, wbd[l], pool_b2[l], pool_scale[l][None], dproj, "pool_bwd")
        if ici is not None:
            swaps.append(add_all(ici, dproj))
            ici = None
        dx, dxb, h1, dn1 = rms_backward([dproj], [w_in_f], xin, norm1_w[l][None], dx1, "in_proj_bwd", 512,
                                        after=swaps[-1][1][4] if swaps else None)
        dw_in = _in_gradient_layout(matmul_tn(dproj, h1, "dw_in")[None])[0].reshape(N_CHIPS, SHARD_IN, D_MODEL)
        d2d = (l, reduce_d2d_start([dw_in] if l == 0 else [dw_in] + late))
        small["norm1_w"][l] = dn1[0]
        small["conv_w"][l] = dcw[:CONV_WIDTH]
        small["conv_b"][l] = dcb[0]
        small["dt_bias"][l] = ddtb[0, :SSD_HEADS]
        small["a_log"][l] = dalog[0, :SSD_HEADS]
        small["d_skip"][l] = ddsk.reshape(SSD_HEADS, HEAD_DIM).sum(axis=1)
        small["ssd_norm_w"][l] = dsn[0]
        small["pool_w"][l] = jnp.stack([dwbd[64 * k:64 * k + 64, 64 * k:64 * k + 64] for k in range(len(POOL_WINDOWS))])
        small["pool_b"][l] = dpb.reshape(len(POOL_WINDOWS), -1)
        small["pool_scale"][l] = dps[0]
        small["norm2_w"][l] = dn2[0]
    grad_x = dx[None]

    ici = add_cores(d2d, d2d[1][4])
    small_parts = [d_final if n == "final_norm_w" else jnp.stack(small[n]) for n in SMALL_NAMES]
    small_start = exchange_start("reduce_small_start", [_pack_small(small_parts)],
                                 [_landing((8, SMALL_ROWS, D_MODEL), F32)], 7, _all_devices_plan, after=ici[1][4])
    swaps.append(add_all(early, small_start[4]))
    swaps.append(add_all(ici, swaps[-1][1][4]))
    for swap in swaps:
        finish(swap, swaps[-1][1][4])
    (small_own,), (small_all,) = exchange_wait("reduce_small_wait", small_start, reduced[0][0], _all_devices_plan)
    small_sum = sum_devices(small_own, small_all, jnp.reshape(4 * px + 2 * py + pc, (1,)).astype(jnp.int32), "reduce_small_sum")
    reduced[0] = reduced[0] + reduced["0 late"]
    g_big = {n: [reduced[l][k] for l in range(DEPTH)] for k, n in enumerate(("w_in", "w_out", "w_gate", "w_up", "w_down"))}
    g_big["w_in"] = [gl[:COLS_IN] for gl in g_big["w_in"]]
    transposed = ("w_in", "w_gate", "w_up")

    g_small = dict(zip(SMALL_NAMES, _unpack_small(small_sum, [p.shape for p in small_parts])))
    g_small["final_norm_w"] = g_small["final_norm_w"].reshape(final_norm_w.shape)
    g_small["conv_w"] = lax.dynamic_slice_in_dim(g_small["conv_w"], chip * (CONV_DIM // N_CHIPS), CONV_DIM // N_CHIPS, axis=2)

    given = dict(norm1_w=(norm1_w, m_norm1_w, v_norm1_w), w_in=(w_in, m_w_in, v_w_in), conv_w=(conv_w, m_conv_w, v_conv_w),
                 conv_b=(conv_b, m_conv_b, v_conv_b), dt_bias=(dt_bias, m_dt_bias, v_dt_bias), a_log=(a_log, m_a_log, v_a_log),
                 d_skip=(d_skip, m_d_skip, v_d_skip), ssd_norm_w=(ssd_norm_w, m_ssd_norm_w, v_ssd_norm_w),
                 pool_w=(pool_w, m_pool_w, v_pool_w), pool_b=(pool_b, m_pool_b, v_pool_b),
                 pool_scale=(pool_scale, m_pool_scale, v_pool_scale), w_out=(w_out, m_w_out, v_w_out),
                 norm2_w=(norm2_w, m_norm2_w, v_norm2_w), w_gate=(w_gate, m_w_gate, v_w_gate), w_up=(w_up, m_w_up, v_w_up),
                 w_down=(w_down, m_w_down, v_w_down), final_norm_w=(final_norm_w, m_final_norm_w, v_final_norm_w))
    order = ("norm1_w", "w_in", "conv_w", "conv_b", "dt_bias", "a_log", "d_skip", "ssd_norm_w", "pool_w", "pool_b",
             "pool_scale", "w_out", "norm2_w", "w_gate", "w_up", "w_down", "final_norm_w")
    grads = dict(g_small)
    results = {}
    for n in ("w_in", "w_out", "w_gate", "w_up", "w_down"):
        w, m, v = given[n]
        if n in transposed:
            out = adamw_layers(jnp.swapaxes(w, 1, 2), g_big[n], jnp.swapaxes(m, 1, 2), jnp.swapaxes(v, 1, 2), "adamw_" + n)
            out = [jnp.swapaxes(o, 1, 2) for o in out]
        else:
            out = adamw_layers(w, g_big[n], m, v, "adamw_" + n)
        grads[n], results[n] = out[0], tuple(out[1:])
    small_shapes = [given[n][0].shape for n in SMALL_NAMES]
    packed = [_pack_small([given[n][k] for n in SMALL_NAMES])[None] for k in range(3)]
    packed_g = _pack_small([grads[n] for n in SMALL_NAMES])[None]
    small_out = adamw(packed[0], packed_g, packed[1], packed[2], "adamw_small")
    small_out = [_unpack_small(o[0], small_shapes) for o in small_out]
    for i, n in enumerate(SMALL_NAMES):
        results[n] = tuple(small_out[k][i] for k in range(3))

    return (loss, grad_x, *[grads[n] for n in order], *[results[n][0] for n in order],
            *[results[n][1] for n in order], *[results[n][2] for n in order])
```

```python
import numpy as np
import jax
import jax.numpy as jnp
from jax import lax
from jax.experimental import pallas as pl
from jax.experimental.pallas import tpu as pltpu

F32 = jnp.float32
BF16 = jnp.bfloat16
MESH = pl.DeviceIdType.MESH
ANY = pl.BlockSpec(memory_space=pl.ANY)

D_MODEL = 1024
DEPTH = 4
EPS = 1e-6
SSD_WIDTH = 512
SSD_HEADS = 8
HEAD_DIM = 64
D_STATE = 128
CHUNK = 128
CONV_WIDTH = 4
CONV_DIM = 1024
SB_WIDTH = 256
POOL_WIDTH = 256
POOL_WINDOWS = (2, 4, 8, 16)
D_FF = 2816
D_IN = 2568
N_CHIPS = 4
OFF_QKV, OFF_Z, OFF_DT, OFF_XBC, OFF_P = 0, 768, 1280, 1536, 2560
D_INP = 2816
ZDT = 768
SHARD_IN, SHARD_OUT, SHARD_FF = 704, 256, 704
COLS_IN = 642
ADAM_LR, ADAM_B1, ADAM_B2, ADAM_EPS, ADAM_WD, ADAM_STEP = 0.001, 0.9, 0.999, 1e-08, 0.01, 10
LANES = 128
VMEM_LIMIT = 56 * 1024 * 1024


def _params(sem=None):
    return pltpu.CompilerParams(dimension_semantics=sem, vmem_limit_bytes=VMEM_LIMIT)


def _tile(n, cap):
    best = None
    for t in range(LANES, min(n, cap) + 1, LANES):
        if n % t == 0:
            best = t
    assert best is not None, (n, cap)
    return best


def _nt(a, b):
    return lax.dot_general(a, b, (((1,), (1,)), ((), ())), preferred_element_type=F32)


def _tn(a, b):
    return lax.dot_general(a, b, (((0,), (0,)), ((), ())), preferred_element_type=F32)


def _nn(a, b):
    return jnp.dot(a, b, preferred_element_type=F32)


def _split_dot(a, b_exact, terms=3, dot=_nn):
    acc = None
    rest = a
    for _ in range(terms):
        hi = rest.astype(BF16)
        part = dot(hi, b_exact)
        acc = part if acc is None else acc + part
        rest = rest - hi.astype(F32)
    return acc


def _split_dot_left(a_exact, b, terms=3):
    acc = None
    rest = b
    for _ in range(terms):
        hi = rest.astype(BF16)
        part = _nn(a_exact, hi)
        acc = part if acc is None else acc + part
        rest = rest - hi.astype(F32)
    return acc


def _sigmoid(x):
    return 1.0 / (1.0 + jnp.exp(-x))


def _softplus(x):
    return jnp.maximum(x, 0.0) + jnp.log(1.0 + jnp.exp(-jnp.abs(x)))


def _iota2(shape, dim):
    return lax.broadcasted_iota(jnp.int32, shape, dim)


def _after(after):
    ops = [] if after is None else list(after) if isinstance(after, (list, tuple)) else [after]
    return [ANY] * len(ops), ops


def rms_matmul(x, nw, wt, name, after=None):
    s, d = x.shape
    n = wt.shape[0]
    tm, tn = _tile(s, 512), _tile(n, 2816)
    specs, ops = _after(after)

    def body(x_ref, nw_ref, w_ref, *rest):
        o_ref, h_ref = rest[len(ops):]

        @pl.when(pl.program_id(1) == 0)
        def _():
            xv = x_ref[...]
            r = lax.rsqrt(jnp.mean(xv * xv, axis=-1, keepdims=True) + EPS)
            h_ref[...] = (xv * r * nw_ref[...]).astype(BF16)
        o_ref[...] = _nt(h_ref[...], w_ref[...])

    return pl.pallas_call(
        body, name=name, grid=(s // tm, n // tn),
        in_specs=[pl.BlockSpec((tm, d), lambda i, j: (i, 0)), pl.BlockSpec((1, d), lambda i, j: (0, 0)),
                  pl.BlockSpec((tn, d), lambda i, j: (j, 0))] + specs,
        out_specs=pl.BlockSpec((tm, tn), lambda i, j: (i, j)),
        out_shape=jax.ShapeDtypeStruct((s, n), F32),
        scratch_shapes=[pltpu.VMEM((tm, d), BF16)],
        compiler_params=_params(("parallel", "arbitrary")),
    )(x, nw, wt, *ops)


def matmul_residual(a, w, res, name):
    s, k = a.shape
    n = w.shape[1]
    tm, tn = _tile(s, 512), _tile(n, 1024)

    def body(a_ref, w_ref, r_ref, o_ref):
        o_ref[...] = r_ref[...] + _nn(a_ref[...], w_ref[...])

    return pl.pallas_call(
        body, name=name, grid=(s // tm, n // tn),
        in_specs=[pl.BlockSpec((tm, k), lambda i, j: (i, 0)), pl.BlockSpec((k, tn), lambda i, j: (0, j)),
                  pl.BlockSpec((tm, tn), lambda i, j: (i, j))],
        out_specs=pl.BlockSpec((tm, tn), lambda i, j: (i, j)),
        out_shape=jax.ShapeDtypeStruct((s, n), F32),
        compiler_params=_params(("parallel", "parallel")),
    )(a, w, res)


def matmul_nt(a, w, name, out_dtype=F32, after=None):
    s, n = a.shape
    k = w.shape[0]
    tm, tk = _tile(s, 512), _tile(k, 1024)
    specs, ops = _after(after)

    def body(a_ref, w_ref, *rest):
        rest[-1][...] = _nt(a_ref[...], w_ref[...]).astype(out_dtype)

    return pl.pallas_call(
        body, name=name, grid=(s // tm, k // tk),
        in_specs=[pl.BlockSpec((tm, n), lambda i, j: (i, 0)), pl.BlockSpec((tk, n), lambda i, j: (j, 0))] + specs,
        out_specs=pl.BlockSpec((tm, tk), lambda i, j: (i, j)),
        out_shape=jax.ShapeDtypeStruct((s, k), out_dtype),
        compiler_params=_params(("parallel", "parallel")),
    )(a, w, *ops)


def matmul_tn(a, b, name, after=None):
    s, m = a.shape
    n = b.shape[1]
    tm, tn = _tile(m, 512), _tile(n, 1024)

    def body(a_ref, b_ref, *rest):
        rest[-1][...] = _tn(a_ref[...], b_ref[...]).astype(BF16)

    specs, ops = _after(after)
    return pl.pallas_call(
        body, name=name, grid=(m // tm, n // tn),
        in_specs=[pl.BlockSpec((s, tm), lambda i, j: (0, i)), pl.BlockSpec((s, tn), lambda i, j: (0, j))] + specs,
        out_specs=pl.BlockSpec((tm, tn), lambda i, j: (i, j)),
        out_shape=jax.ShapeDtypeStruct((m, n), BF16),
        compiler_params=_params(("parallel", "parallel")),
    )(a, b, *ops)


def ffn_forward(x1, nw, wgt, wut, wd, name):
    s, d = x1.shape
    f = wgt.shape[0]
    tm, tf = _tile(s, 1024), _tile(f, 256)

    def body(x_ref, nw_ref, wg_ref, wu_ref, wd_ref, o_ref, g_ref, u_ref, h_ref, acc_ref):
        j = pl.program_id(1)

        @pl.when(j == 0)
        def _():
            xv = x_ref[...]
            r = lax.rsqrt(jnp.mean(xv * xv, axis=-1, keepdims=True) + EPS)
            h_ref[...] = (xv * r * nw_ref[...]).astype(BF16)
            acc_ref[...] = xv

        h = h_ref[...]
        g = _nt(h, wg_ref[...])
        u = _nt(h, wu_ref[...])
        g_ref[...] = g.astype(BF16)
        u_ref[...] = u.astype(BF16)
        a = (g * _sigmoid(g) * u).astype(BF16)
        acc_ref[...] += _nn(a, wd_ref[...])

        @pl.when(j == pl.num_programs(1) - 1)
        def _():
            o_ref[...] = acc_ref[...]

    wblk = pl.BlockSpec((tf, d), lambda i, j: (j, 0))
    return pl.pallas_call(
        body, name=name, grid=(s // tm, f // tf),
        in_specs=[pl.BlockSpec((tm, d), lambda i, j: (i, 0)), pl.BlockSpec((1, d), lambda i, j: (0, 0)), wblk, wblk, wblk],
        out_specs=[pl.BlockSpec((tm, d), lambda i, j: (i, 0)), pl.BlockSpec((tm, tf), lambda i, j: (i, j)),
                   pl.BlockSpec((tm, tf), lambda i, j: (i, j))],
        out_shape=[jax.ShapeDtypeStruct((s, d), F32), jax.ShapeDtypeStruct((s, f), BF16),
                   jax.ShapeDtypeStruct((s, f), BF16)],
        scratch_shapes=[pltpu.VMEM((tm, d), BF16), pltpu.VMEM((tm, d), F32)],
        compiler_params=_params(("parallel", "arbitrary")),
    )(x1, nw, wgt, wut, wd)


def ffn_backward_act(dx2, g, u, wd, name, after=None):
    s, d = dx2.shape
    f = wd.shape[0]
    tm, tf = _tile(s, 256), _tile(f, 2816)
    specs, ops = _after(after)

    def body(dx_ref, g_ref, u_ref, wd_ref, *rest):
        dg_ref, du_ref, a_ref = rest[len(ops):]
        da = _nt(dx_ref[...], wd_ref[...])
        gv = g_ref[...].astype(F32)
        uv = u_ref[...].astype(F32)
        sg = _sigmoid(gv)
        silu = gv * sg
        dg_ref[...] = (da * uv * (sg * (1.0 + gv * (1.0 - sg)))).astype(BF16)
        du_ref[...] = (da * silu).astype(BF16)
        a_ref[...] = (silu * uv).astype(BF16)

    blk = pl.BlockSpec((tm, tf), lambda i, j: (i, j))
    return pl.pallas_call(
        body, name=name, grid=(s // tm, f // tf),
        in_specs=[pl.BlockSpec((tm, d), lambda i, j: (i, 0)), blk, blk, pl.BlockSpec((tf, d), lambda i, j: (j, 0))] + specs,
        out_specs=[blk, blk, blk],
        out_shape=[jax.ShapeDtypeStruct((s, f), BF16)] * 3,
        compiler_params=_params(("parallel", "parallel")),
    )(dx2, g, u, wd, *ops)


def rms_backward(dzs, wts, x, nw, dres, name, tm, after=None):
    s, d = x.shape
    nz = len(dzs)
    specs, ops = _after(after)

    def body(*refs):
        dz_refs, w_refs = refs[:nz], refs[nz:2 * nz]
        x_ref, nw_ref, dres_ref = refs[2 * nz:2 * nz + 3]
        dx_ref, dxb_ref, h_ref, dnw_ref = refs[2 * nz + 3 + len(ops):]
        dh = _nn(dz_refs[0][...], w_refs[0][...])
        for k in range(1, nz):
            dh = dh + _nn(dz_refs[k][...], w_refs[k][...])
        xv = x_ref[...]
        r = lax.rsqrt(jnp.mean(xv * xv, axis=-1, keepdims=True) + EPS)
        xhat = xv * r
        nwv = nw_ref[...]
        h_ref[...] = (xhat * nwv).astype(BF16)

        @pl.when(pl.program_id(0) == 0)
        def _():
            dnw_ref[...] = jnp.zeros_like(dnw_ref)

        dnw_ref[...] += jnp.sum(dh * xhat, axis=0, keepdims=True)
        gdh = dh * nwv
        dx = dres_ref[...] + r * (gdh - xhat * jnp.mean(gdh * xhat, axis=-1, keepdims=True))
        dx_ref[...] = dx
        dxb_ref[...] = dx.astype(BF16)

    row = pl.BlockSpec((tm, d), lambda i: (i, 0))
    in_specs = [pl.BlockSpec((tm, dz.shape[1]), lambda i: (i, 0)) for dz in dzs]
    in_specs += [pl.BlockSpec(w.shape, lambda i: (0, 0), pipeline_mode=pl.Buffered(1)) for w in wts]
    in_specs += [row, pl.BlockSpec((1, d), lambda i: (0, 0)), row] + specs
    return pl.pallas_call(
        body, name=name, grid=(s // tm,),
        in_specs=in_specs,
        out_specs=[row, row, row, pl.BlockSpec((1, d), lambda i: (0, 0))],
        out_shape=[jax.ShapeDtypeStruct((s, d), F32), jax.ShapeDtypeStruct((s, d), BF16),
                   jax.ShapeDtypeStruct((s, d), BF16), jax.ShapeDtypeStruct((1, d), F32)],
        compiler_params=_params(("arbitrary",)),
    )(*dzs, *wts, x, nw, dres, *ops)


def loss_head(x, nw, target, name):
    s, d = x.shape
    tm = _tile(s, 512)

    def body(x_ref, nw_ref, t_ref, loss_ref, dx_ref, dxb_ref, dnw_ref):
        xv = x_ref[...]
        r = lax.rsqrt(jnp.mean(xv * xv, axis=-1, keepdims=True) + EPS)
        xhat = xv * r
        nwv = nw_ref[...]
        err = xhat * nwv - t_ref[...]

        @pl.when(pl.program_id(0) == 0)
        def _():
            dnw_ref[...] = jnp.zeros_like(dnw_ref)
            loss_ref[...] = jnp.zeros_like(loss_ref)

        part = jnp.sum(jnp.sum(err * err, axis=-1, keepdims=True), axis=0, keepdims=True) * (0.5 / d)
        loss_ref[...] += jnp.broadcast_to(part, loss_ref.shape)
        dout = err * (1.0 / d)
        dnw_ref[...] += jnp.sum(dout * xhat, axis=0, keepdims=True)
        gdh = dout * nwv
        dx = r * (gdh - xhat * jnp.mean(gdh * xhat, axis=-1, keepdims=True))
        dx_ref[...] = dx
        dxb_ref[...] = dx.astype(BF16)

    row = pl.BlockSpec((tm, d), lambda i: (i, 0))
    return pl.pallas_call(
        body, name=name, grid=(s // tm,),
        in_specs=[row, pl.BlockSpec((1, d), lambda i: (0, 0)), row],
        out_specs=[pl.BlockSpec((1, LANES), lambda i: (0, 0)), row, row, pl.BlockSpec((1, d), lambda i: (0, 0))],
        out_shape=[jax.ShapeDtypeStruct((1, LANES), F32), jax.ShapeDtypeStruct((s, d), F32),
                   jax.ShapeDtypeStruct((s, d), BF16), jax.ShapeDtypeStruct((1, d), F32)],
        compiler_params=_params(("arbitrary",)),
    )(x, nw, target)


def _shift_down(x, k):
    return jnp.where(_iota2(x.shape, 0) >= k, pltpu.roll(x, k, axis=0), 0.0)


def _shift_up(x, k):
    s = x.shape[0]
    return jnp.where(_iota2(x.shape, 0) < s - k, pltpu.roll(x, s - k, axis=0), 0.0)


CONV_TILE = 256


def conv_forward(proj, cw, cb, name):
    s = proj.shape[0]
    tn = CONV_TILE
    off = OFF_XBC // tn

    def body(u_ref, w_ref, b_ref, o_ref):
        u = u_ref[...]
        pre = b_ref[...] + w_ref[CONV_WIDTH - 1:CONV_WIDTH, :] * u
        for i in range(CONV_WIDTH - 1):
            pre = pre + w_ref[i:i + 1, :] * _shift_down(u, CONV_WIDTH - 1 - i)
        o_ref[...] = pre * _sigmoid(pre)

    return pl.pallas_call(
        body, name=name, grid=(CONV_DIM // tn,),
        in_specs=[pl.BlockSpec((s, tn), lambda j: (0, off + j)), pl.BlockSpec((8, tn), lambda j: (0, j)),
                  pl.BlockSpec((1, tn), lambda j: (0, j))],
        out_specs=pl.BlockSpec((s, tn), lambda j: (0, j)),
        out_shape=jax.ShapeDtypeStruct((s, CONV_DIM), F32),
        compiler_params=_params(("parallel",)),
    )(proj, cw, cb)


def conv_backward(proj, dxc, cw, cb, dproj, name, after=None):
    s = proj.shape[0]
    tn = CONV_TILE
    off = OFF_XBC // tn

    specs, ops = _after(after)

    def body(u_ref, d_ref, w_ref, b_ref, *rest):
        du_ref, dw_ref, db_ref = rest[-3:]
        u = u_ref[...]
        shifted = [_shift_down(u, CONV_WIDTH - 1 - i) for i in range(CONV_WIDTH - 1)] + [u]
        pre = b_ref[...] + w_ref[CONV_WIDTH - 1:CONV_WIDTH, :] * u
        for i in range(CONV_WIDTH - 1):
            pre = pre + w_ref[i:i + 1, :] * shifted[i]
        sg = _sigmoid(pre)
        dpre = d_ref[...] * (sg * (1.0 + pre * (1.0 - sg)))
        du = w_ref[CONV_WIDTH - 1:CONV_WIDTH, :] * dpre
        for i in range(CONV_WIDTH - 1):
            du = du + w_ref[i:i + 1, :] * _shift_up(dpre, CONV_WIDTH - 1 - i)
        du_ref[...] = du.astype(BF16)
        rows = [jnp.sum(dpre * shifted[i], axis=0, keepdims=True) for i in range(CONV_WIDTH)]
        rows.append(jnp.zeros((8 - CONV_WIDTH, tn), F32))
        dw_ref[...] = jnp.concatenate(rows, axis=0)
        db_ref[...] = jnp.sum(dpre, axis=0, keepdims=True)

    return pl.pallas_call(
        body, name=name, grid=(CONV_DIM // tn,),
        in_specs=[pl.BlockSpec((s, tn), lambda j: (0, off + j)), pl.BlockSpec((s, tn), lambda j: (0, j)),
                  pl.BlockSpec((8, tn), lambda j: (0, j)), pl.BlockSpec((1, tn), lambda j: (0, j)), ANY] + specs,
        out_specs=[pl.BlockSpec((s, tn), lambda j: (0, off + j)), pl.BlockSpec((8, tn), lambda j: (0, j)),
                   pl.BlockSpec((1, tn), lambda j: (0, j))],
        out_shape=[jax.ShapeDtypeStruct(dproj.shape, BF16), jax.ShapeDtypeStruct((8, CONV_DIM), F32),
                   jax.ShapeDtypeStruct((1, CONV_DIM), F32)],
        input_output_aliases={4: 0},
        compiler_params=_params(("parallel",)),
    )(proj, dxc, cw, cb, dproj, *ops)


def _pool_lane_window(shape):
    grp = _iota2(shape, 1) // (POOL_WIDTH // len(POOL_WINDOWS))
    win = jnp.full(shape, POOL_WINDOWS[-1], jnp.int32)
    for gi in range(len(POOL_WINDOWS) - 2, -1, -1):
        win = jnp.where(grp == gi, POOL_WINDOWS[gi], win)
    return grp, win


def _pool_select(grp, sums):
    out = sums[-1]
    for gi in range(len(sums) - 2, -1, -1):
        out = jnp.where(grp == gi, sums[gi], out)
    return out


def _pool_pooled(p):
    grp, win = _pool_lane_window(p.shape)
    inv_count = 1.0 / jnp.minimum(_iota2(p.shape, 0) + 1, win).astype(F32)
    sums, acc, k = [], p, 1
    for _ in POOL_WINDOWS:
        acc = acc + _shift_down(acc, k)
        sums.append(acc)
        k *= 2
    return _pool_select(grp, sums) * inv_count - p, grp, inv_count


def pool_forward(proj, wbd, pb, ps, y_all, name, after=None):
    s = proj.shape[0]
    specs, ops = _after(after)

    def body(p_ref, w_ref, b_ref, s_ref, *rest):
        o_ref = rest[-1]
        pooled, _, _ = _pool_pooled(p_ref[...])
        mixed = _nn(pooled.astype(BF16), w_ref[...]) + b_ref[...]
        o_ref[...] = (mixed * s_ref[...]).astype(BF16)

    vec = pl.BlockSpec((1, POOL_WIDTH), lambda j: (0, 0))
    return pl.pallas_call(
        body, name=name, grid=(1,),
        in_specs=[pl.BlockSpec((s, POOL_WIDTH), lambda j: (0, OFF_P // POOL_WIDTH)),
                  pl.BlockSpec((POOL_WIDTH, POOL_WIDTH), lambda j: (0, 0)), vec, vec, ANY] + specs,
        out_specs=pl.BlockSpec((s, POOL_WIDTH), lambda j: (0, (SSD_WIDTH + SB_WIDTH) // POOL_WIDTH)),
        out_shape=jax.ShapeDtypeStruct(y_all.shape, BF16),
        input_output_aliases={4: 0},
        compiler_params=_params(("arbitrary",)),
    )(proj, wbd, pb, ps, y_all, *ops)


def pool_backward(proj, dyall, wbd, pb, ps, dproj, name):
    s = proj.shape[0]

    def body(p_ref, dy_ref, w_ref, b_ref, s_ref, _, dp_ref, dw_ref, db_ref, ds_ref):
        pooled, grp, inv_count = _pool_pooled(p_ref[...])
        pooled_b = pooled.astype(BF16)
        mixed = _nn(pooled_b, w_ref[...]) + b_ref[...]
        dy = dy_ref[...]
        ds_ref[...] = jnp.sum(dy * mixed, axis=0, keepdims=True)
        dmixed = dy * s_ref[...]
        db_ref[...] = jnp.sum(dmixed, axis=0, keepdims=True)
        dmixed_b = dmixed.astype(BF16)
        dw_ref[...] = _tn(pooled_b, dmixed_b)
        dpooled = _nt(dmixed_b, w_ref[...])
        sums, acc, k = [], dpooled * inv_count, 1
        for _ in POOL_WINDOWS:
            acc = acc + _shift_up(acc, k)
            sums.append(acc)
            k *= 2
        dp_ref[...] = (_pool_select(grp, sums) - dpooled).astype(BF16)

    vec = pl.BlockSpec((1, POOL_WIDTH), lambda j: (0, 0))
    mat = pl.BlockSpec((POOL_WIDTH, POOL_WIDTH), lambda j: (0, 0))
    pcol = pl.BlockSpec((s, POOL_WIDTH), lambda j: (0, OFF_P // POOL_WIDTH))
    return pl.pallas_call(
        body, name=name, grid=(1,),
        in_specs=[pcol, pl.BlockSpec((s, POOL_WIDTH), lambda j: (0, (SSD_WIDTH + SB_WIDTH) // POOL_WIDTH)), mat, vec, vec, ANY],
        out_specs=[pcol, mat, vec, vec],
        out_shape=[jax.ShapeDtypeStruct(dproj.shape, BF16), jax.ShapeDtypeStruct((POOL_WIDTH, POOL_WIDTH), F32),
                   jax.ShapeDtypeStruct((1, POOL_WIDTH), F32), jax.ShapeDtypeStruct((1, POOL_WIDTH), F32)],
        input_output_aliases={5: 0},
        compiler_params=_params(("arbitrary",)),
    )(proj, dyall, wbd, pb, ps, dproj)


N_PAIRS = SSD_HEADS // 2


def _ssd_common(xc, dtraw, dtb, alog):
    c = CHUNK
    dt = _softplus(dtraw + dtb)
    a = -jnp.exp(alog)
    ltri = (_iota2((c, c), 0) >= _iota2((c, c), 1)).astype(BF16)
    acum = _split_dot_left(ltri, dt * a)
    expand = (_iota2((c, SSD_WIDTH), 1) // HEAD_DIM == _iota2((c, SSD_WIDTH), 0)).astype(BF16)
    expand_wide = (_iota2((c, SSD_HEADS * c), 1) // c == _iota2((c, SSD_HEADS * c), 0)).astype(BF16)
    acum_x = _split_dot(acum, expand, 2)
    dt_x = _split_dot(dt, expand, 2)
    alast_x = acum_x[c - 1:c, :]
    return dict(dt=dt, a=a, acum=acum, acum_x=acum_x, dt_x=dt_x, ea_x=jnp.exp(acum_x),
                dte_x=jnp.exp(alast_x - acum_x), eal_x=jnp.exp(alast_x),
                acol=_split_dot(acum, expand_wide, 2), acum_t=acum.T,
                xs=xc[:, :SSD_WIDTH], causal=_iota2((c, c), 0) >= _iota2((c, c), 1),
                left=_iota2((c, c), 1) < HEAD_DIM)


def _ssd_group(xc, g):
    b = xc[:, SSD_WIDTH + D_STATE * g:SSD_WIDTH + D_STATE * (g + 1)]
    cm = xc[:, SSD_WIDTH + 2 * D_STATE + D_STATE * g:SSD_WIDTH + 2 * D_STATE + D_STATE * (g + 1)]
    return b, cm


def _ssd_decay(q, hh):
    col = q["acol"][:, CHUNK * hh:CHUNK * (hh + 1)]
    row = q["acum_t"][hh:hh + 1, :]
    return jnp.where(q["causal"], jnp.exp(jnp.minimum(col - row, 0.0)), 0.0)


def ssd_forward(proj, xc, dtb, alog, dskip_x, nw, name):
    s = xc.shape[0]
    nc = s // CHUNK

    def body(xc_ref, zdt_ref, dtb_ref, alog_ref, dsk_ref, nw_ref, y_ref, yc_ref, st_ref, state):
        @pl.when(pl.program_id(0) == 0)
        def _():
            state[...] = jnp.zeros_like(state)

        xcv = xc_ref[...]
        q = _ssd_common(xcv, zdt_ref[:, SSD_WIDTH:SSD_WIDTH + LANES], dtb_ref[...], alog_ref[...])
        x = q["xs"] * q["dt_x"]
        xb = x.astype(BF16)
        xd = (x * q["dte_x"]).astype(BF16)
        pieces = []
        for g in range(2):
            bg, cg = _ssd_group(xcv, g)
            bgb, cgb = bg.astype(BF16), cg.astype(BF16)
            cb = _nt(cgb, bgb)
            bgt = bg.T.astype(BF16)
            for pr in (2 * g, 2 * g + 1):
                sl = slice(CHUNK * pr, CHUNK * (pr + 1))
                st = state[pr]
                st_ref[0, pr] = st
                yp = _nn(cgb, st.astype(BF16)) * q["ea_x"][:, sl]
                for k, hh in enumerate((2 * pr, 2 * pr + 1)):
                    w = (cb * _ssd_decay(q, hh)).astype(BF16)
                    mask = q["left"] if k == 0 else jnp.logical_not(q["left"])
                    yp = yp + _nn(w, jnp.where(mask, xb[:, sl], jnp.zeros_like(xb[:, sl])))
                state[pr] = st * q["eal_x"][:, sl] + _nn(bgt, xd[:, sl])
                pieces.append(yp)
        y = jnp.concatenate(pieces, axis=1) + q["xs"] * dsk_ref[...]
        yc_ref[...] = y
        zv = zdt_ref[:, :SSD_WIDTH]
        yg = y * (zv * _sigmoid(zv))
        r = lax.rsqrt(jnp.mean(yg * yg, axis=-1, keepdims=True) + EPS)
        y_ref[...] = (yg * r * nw_ref[...]).astype(BF16)

    vec = lambda n: pl.BlockSpec((1, n), lambda c: (0, 0))
    return pl.pallas_call(
        body, name=name, grid=(nc,),
        in_specs=[pl.BlockSpec((CHUNK, CONV_DIM), lambda c: (c, 0)),
                  pl.BlockSpec((CHUNK, ZDT), lambda c: (c, OFF_Z // ZDT)),
                  vec(LANES), vec(LANES), vec(SSD_WIDTH), vec(SSD_WIDTH)],
        out_specs=[pl.BlockSpec((CHUNK, SSD_WIDTH), lambda c: (c, 0)), pl.BlockSpec((CHUNK, SSD_WIDTH), lambda c: (c, 0)),
                   pl.BlockSpec((1, N_PAIRS, D_STATE, CHUNK), lambda c: (c, 0, 0, 0))],
        out_shape=[jax.ShapeDtypeStruct((s, D_MODEL), BF16), jax.ShapeDtypeStruct((s, SSD_WIDTH), F32),
                   jax.ShapeDtypeStruct((nc, N_PAIRS, D_STATE, CHUNK), F32)],
        scratch_shapes=[pltpu.VMEM((N_PAIRS, D_STATE, CHUNK), F32)],
        compiler_params=_params(("arbitrary",)),
    )(xc, proj, dtb, alog, dskip_x, nw)


def ssd_backward(proj, xc, ycore, dyall, states, dtb, alog, dskip_x, nw, name):
    s = xc.shape[0]
    nc = s // CHUNK
    c = CHUNK

    def body(xc_ref, zdt_ref, yc_ref, dy_ref, st_ref, dtb_ref, alog_ref, dsk_ref, nw_ref,
             dxc_ref, dzdt_ref, dnw_ref, ddsk_ref, ddtb_ref, dalog_ref, dstate):
        @pl.when(pl.program_id(0) == 0)
        def _():
            dstate[...] = jnp.zeros_like(dstate)
            dnw_ref[...] = jnp.zeros_like(dnw_ref)
            ddsk_ref[...] = jnp.zeros_like(ddsk_ref)
            ddtb_ref[...] = jnp.zeros_like(ddtb_ref)
            dalog_ref[...] = jnp.zeros_like(dalog_ref)

        xcv = xc_ref[...]
        dtraw = zdt_ref[:, SSD_WIDTH:SSD_WIDTH + LANES]
        q = _ssd_common(xcv, dtraw, dtb_ref[...], alog_ref[...])
        xs = q["xs"]
        x = xs * q["dt_x"]
        zv, yc, dy, nwv = zdt_ref[:, :SSD_WIDTH], yc_ref[...], dy_ref[...], nw_ref[...]
        sgz = _sigmoid(zv)
        siluz = zv * sgz
        yg = yc * siluz
        r = lax.rsqrt(jnp.mean(yg * yg, axis=-1, keepdims=True) + EPS)
        dnw_ref[...] += jnp.sum(dy * yg * r, axis=0, keepdims=True)
        g1 = dy * nwv
        dyg = r * (g1 - yg * (r * r) * jnp.mean(g1 * yg, axis=-1, keepdims=True))
        dyv = dyg * siluz
        dz = (dyg * yc * (sgz * (1.0 + zv * (1.0 - sgz)))).astype(BF16)
        ddsk_ref[...] += jnp.sum(dyv * xs, axis=0, keepdims=True)
        dye = dyv * q["ea_x"]
        dx_parts, yoff_parts, u_parts, v_parts, e_parts = [], [], [], [], []
        db_parts, dc_parts = [], []
        for g in range(2):
            bg, cg = _ssd_group(xcv, g)
            bgb, cgb = bg.astype(BF16), cg.astype(BF16)
            cb = _nt(cgb, bgb)
            cgt = cg.T.astype(BF16)
            dgsum = jnp.zeros((c, c), F32)
            dbg = jnp.zeros((c, D_STATE), F32)
            dcg = jnp.zeros((c, D_STATE), F32)
            for pr in (2 * g, 2 * g + 1):
                sl = slice(c * pr, c * (pr + 1))
                st = st_ref[0, pr]
                dst = dstate[pr]
                stb, dstb = st.astype(BF16), dst.astype(BF16)
                xp = x[:, sl]
                xpb = xp.astype(BF16)
                dyp = dyv[:, sl]
                xdp = xp * q["dte_x"][:, sl]
                yoff_parts.append(_nn(cgb, stb) * q["ea_x"][:, sl])
                rr = _nn(bgb, dstb)
                dxp = rr * q["dte_x"][:, sl]
                u_parts.append(rr * xdp)
                v_parts.append(dst * st * q["eal_x"][:, sl])
                for k, hh in enumerate((2 * pr, 2 * pr + 1)):
                    decay = _ssd_decay(q, hh)
                    w = cb * decay
                    mask = q["left"] if k == 0 else jnp.logical_not(q["left"])
                    dym = jnp.where(mask, dyp, 0.0).astype(BF16)
                    dw = _nt(dym, xpb)
                    dgsum = dgsum + dw * decay
                    e_parts.append(dw * w)
                    dxp = dxp + _nn(w.T.astype(BF16), dym)
                dyeb = dye[:, sl].astype(BF16)
                dcg = dcg + _nt(dyeb, stb)
                dbg = dbg + _nt(xdp.astype(BF16), dstb)
                dstate[pr] = dst * q["eal_x"][:, sl] + _nn(cgt, dyeb)
                dx_parts.append(dxp)
            dcg = dcg + _nn(dgsum.astype(BF16), bgb)
            dbg = dbg + _nn(dgsum.T.astype(BF16), cgb)
            db_parts.append(dbg)
            dc_parts.append(dcg)
        dx = jnp.concatenate(dx_parts, axis=1)
        yoff = jnp.concatenate(yoff_parts, axis=1)
        u = jnp.concatenate(u_parts, axis=1)
        v = jnp.concatenate(v_parts, axis=1)
        reduce_heads = (_iota2((SSD_WIDTH, c), 0) // HEAD_DIM == _iota2((SSD_WIDTH, c), 1)).astype(BF16)
        to_head = (_iota2((SSD_HEADS * c, c), 0) // c == _iota2((SSD_HEADS * c, c), 1)).astype(BF16)
        da = _split_dot(dyv * yoff - u, reduce_heads, 2)
        da = da + _split_dot(jnp.concatenate(e_parts, axis=1), to_head, 2)
        da = da - _split_dot(jnp.concatenate(e_parts, axis=0), to_head, 2, dot=_tn)
        dalast = jnp.sum(_split_dot(u + v, reduce_heads, 2), axis=0, keepdims=True)
        da = da + jnp.where(_iota2((c, c), 0) == c - 1, dalast, 0.0)
        utri = (_iota2((c, c), 1) >= _iota2((c, c), 0)).astype(BF16)
        dda = _split_dot_left(utri, da)
        ddt = dda * q["a"] + _split_dot(dx * xs, reduce_heads, 2)
        dalog_ref[...] += jnp.sum(dda * q["dt"], axis=0, keepdims=True) * q["a"]
        ddtraw = jnp.where(_iota2((c, c), 1) < SSD_HEADS, ddt * _sigmoid(dtraw + dtb_ref[...]), 0.0)
        ddtb_ref[...] += jnp.sum(ddtraw, axis=0, keepdims=True)
        dzdt_ref[...] = jnp.concatenate([dz, ddtraw.astype(BF16), jnp.zeros((c, ZDT - SSD_WIDTH - LANES), BF16)], axis=1)
        dxs = dx * q["dt_x"] + dyv * dsk_ref[...]
        dxc_ref[...] = jnp.concatenate([dxs] + db_parts + dc_parts, axis=1)

    rev = lambda i: nc - 1 - i
    vec = lambda n: pl.BlockSpec((1, n), lambda i: (0, 0))
    wide = pl.BlockSpec((c, SSD_WIDTH), lambda i: (rev(i), 0))
    zdt = pl.BlockSpec((c, ZDT), lambda i: (rev(i), OFF_Z // ZDT))
    return pl.pallas_call(
        body, name=name, grid=(nc,),
        in_specs=[pl.BlockSpec((c, CONV_DIM), lambda i: (rev(i), 0)), zdt, wide, wide,
                  pl.BlockSpec((1, N_PAIRS, D_STATE, c), lambda i: (rev(i), 0, 0, 0)),
                  vec(LANES), vec(LANES), vec(SSD_WIDTH), vec(SSD_WIDTH)],
        out_specs=[pl.BlockSpec((c, CONV_DIM), lambda i: (rev(i), 0)), zdt,
                   vec(SSD_WIDTH), vec(SSD_WIDTH), vec(LANES), vec(LANES)],
        out_shape=[jax.ShapeDtypeStruct((s, CONV_DIM), F32), jax.ShapeDtypeStruct((s, D_INP), BF16),
                   jax.ShapeDtypeStruct((1, SSD_WIDTH), F32),
                   jax.ShapeDtypeStruct((1, SSD_WIDTH), F32), jax.ShapeDtypeStruct((1, LANES), F32),
                   jax.ShapeDtypeStruct((1, LANES), F32)],
        scratch_shapes=[pltpu.VMEM((N_PAIRS, D_STATE, c), F32)],
        compiler_params=_params(("arbitrary",)),
    )(xc, proj, ycore, dyall, states, dtb, alog, dskip_x, nw)


SB_Q, SB_K = 512, 512
SB_T = 256
SB_SCALE = HEAD_DIM ** -0.5


def _key_suffix(x, tri, terms):
    runs = [x[:, SB_T * k:SB_T * (k + 1)] for k in range(SB_K // SB_T)]
    sums = [_split_dot(r, tri, terms) for r in runs]
    later = None
    for k in range(len(runs) - 1, -1, -1):
        if later is not None:
            sums[k] = sums[k] + later
        total = jnp.sum(runs[k], axis=1, keepdims=True)
        later = total if later is None else later + total
    return jnp.concatenate(sums, axis=1), later


def _sb_weights(qm, kb, diagonal, run_lk, strict_after):
    z = _nt(qm, kb)
    nz = -z
    tail = jnp.log(1.0 + jnp.exp(jnp.minimum(z, nz)))
    ls = jnp.minimum(z, 0.0) - tail
    lk = jnp.minimum(nz, 0.0) - tail
    if diagonal is not None:
        valid = _iota2(z.shape, 1) < _iota2(z.shape, 0) + diagonal
        lk = jnp.where(valid, lk, 0.0)
    after, total = _key_suffix(lk, strict_after, 1)
    w = jnp.exp(ls + after + run_lk)
    if diagonal is not None:
        w = jnp.where(valid, w, 0.0)
    return ls, total, w


def _sb_sweep(i, block, init):
    own = (i * SB_Q) // SB_K
    first = block(own, init, i * SB_Q - own * SB_K)
    return lax.fori_loop(1, own + 1, lambda jj, carry: block(own - jj, carry, None), first)


def sb_forward(proj, y_all, name, after=None):
    s = proj.shape[0]
    t, tk = SB_Q, SB_K
    nq = s // t
    specs, ops = _after(after)

    def body(q_ref, k_ref, v_ref, *rest):
        y_ref, o_ref = rest[-2:]
        i = pl.program_id(1)
        left = _iota2((t, LANES), 1) < HEAD_DIM
        left_k = _iota2((tk, LANES), 1) < HEAD_DIM
        qv = q_ref[...] * SB_SCALE
        zero = jnp.zeros_like(qv)
        qms = (jnp.where(left, qv, zero).astype(BF16), jnp.where(left, zero, qv).astype(BF16))
        strict_after = (_iota2((SB_T, SB_T), 0) > _iota2((SB_T, SB_T), 1)).astype(BF16)

        def block(j, carry, diagonal):
            o, runs = carry[0], carry[1:]
            rows = pl.ds(pl.multiple_of(j * tk, tk), tk)
            kb = k_ref[rows, :].astype(BF16)
            vv = v_ref[rows, :]
            new_runs = []
            for k in range(2):
                _, total, w = _sb_weights(qms[k], kb, diagonal, runs[k], strict_after)
                vm = jnp.where(left_k if k == 0 else jnp.logical_not(left_k), vv, 0.0).astype(BF16)
                o = o + _nn(w.astype(BF16), vm)
                new_runs.append(runs[k] + total)
            return (o, *new_runs)

        init = (jnp.zeros((t, LANES), F32), jnp.zeros((t, 1), F32), jnp.zeros((t, 1), F32))
        o = _sb_sweep(i, block, init)[0]
        o_ref[...] = o
        y_ref[...] = o.astype(BF16)

    return pl.pallas_call(
        body, name=name, grid=(2, nq),
        in_specs=[pl.BlockSpec((t, LANES), lambda p, i: (i, 3 * p)),
                  pl.BlockSpec((s, LANES), lambda p, i: (0, 3 * p + 1)),
                  pl.BlockSpec((s, LANES), lambda p, i: (0, 3 * p + 2)), ANY] + specs,
        out_specs=[pl.BlockSpec((t, LANES), lambda p, i: (i, SSD_WIDTH // LANES + p)),
                   pl.BlockSpec((t, LANES), lambda p, i: (i, p))],
        out_shape=[jax.ShapeDtypeStruct(y_all.shape, BF16), jax.ShapeDtypeStruct((s, SB_WIDTH), F32)],
        input_output_aliases={3: 0},
        compiler_params=_params(("parallel", "arbitrary")),
    )(proj, proj, proj, y_all, *ops)


def sb_backward(proj, o, dyall, dproj, name, after=None):
    s = proj.shape[0]
    t, tk = SB_Q, SB_K
    nq = s // t
    specs, ops = _after(after)

    def body(q_ref, k_ref, v_ref, o_ref, do_ref, *rest):
        dqkv_ref, dk_acc, dv_acc = rest[-3:]
        dk_acc[...] = jnp.zeros_like(dk_acc)
        dv_acc[...] = jnp.zeros_like(dv_acc)
        left = _iota2((t, LANES), 1) < HEAD_DIM
        lane_masks = (left, jnp.logical_not(left))
        left_k = _iota2((tk, LANES), 1) < HEAD_DIM
        key_masks = (left_k, jnp.logical_not(left_k))
        strict_after = (_iota2((SB_T, SB_T), 0) > _iota2((SB_T, SB_T), 1)).astype(BF16)
        from_here = (_iota2((SB_T, SB_T), 0) >= _iota2((SB_T, SB_T), 1)).astype(BF16)

        def query_block(i, _):
            qrows = pl.ds(pl.multiple_of(i * t, t), t)
            qv = q_ref[qrows, :] * SB_SCALE
            dov = do_ref[qrows, :]
            zero = jnp.zeros_like(qv)
            qb = qv.astype(BF16)
            dob = dov.astype(BF16)
            prod = dob.astype(F32) * o_ref[qrows, :]
            qms = [jnp.where(m, qv, zero).astype(BF16) for m in lane_masks]
            doms = [jnp.where(m, dov, zero).astype(BF16) for m in lane_masks]
            deltas = [jnp.sum(jnp.where(m, prod, zero), axis=1, keepdims=True) for m in lane_masks]

            def block(j, carry, diagonal):
                dq = carry[0]
                run_lk, run_e = carry[1:3], carry[3:5]
                rows = pl.ds(pl.multiple_of(j * tk, tk), tk)
                kb = k_ref[rows, :].astype(BF16)
                vb = v_ref[rows, :].astype(BF16)
                dkj = jnp.zeros((tk, LANES), F32)
                dvj = jnp.zeros((tk, LANES), F32)
                new_lk, new_e = [], []
                for k in range(2):
                    ls, total, w = _sb_weights(qms[k], kb, diagonal, run_lk[k], strict_after)
                    wb = w.astype(BF16)
                    e = _nt(doms[k], vb) * wb.astype(F32)
                    e_from_here, e_total = _key_suffix(e, from_here, 2)
                    before = deltas[k] - e_from_here - run_e[k]
                    dz = e - jnp.exp(ls) * (e + before)
                    if diagonal is not None:
                        dz = jnp.where(_iota2(dz.shape, 1) < _iota2(dz.shape, 0) + diagonal, dz, 0.0)
                    dz = dz.astype(BF16)
                    m = lane_masks[k]
                    dvj = dvj + jnp.where(key_masks[k], _tn(wb, dob), 0.0)
                    dkj = dkj + jnp.where(key_masks[k], _tn(dz, qb), 0.0)
                    dq = dq + jnp.where(m, _nn(dz, kb), 0.0)
                    new_lk.append(run_lk[k] + total)
                    new_e.append(run_e[k] + e_total)
                dk_acc[rows, :] += dkj
                dv_acc[rows, :] += dvj
                return (dq, *new_lk, *new_e)

            col = jnp.zeros((t, 1), F32)
            dq = _sb_sweep(i, block, (jnp.zeros((t, LANES), F32), col, col, col, col))[0]
            dqkv_ref[qrows, 0:LANES] = (dq * SB_SCALE).astype(BF16)
            return 0

        lax.fori_loop(0, nq, query_block, 0)
        dqkv_ref[:, LANES:2 * LANES] = dk_acc[...].astype(BF16)
        dqkv_ref[:, 2 * LANES:3 * LANES] = dv_acc[...].astype(BF16)

    col = lambda f: pl.BlockSpec((s, LANES), f)
    return pl.pallas_call(
        body, name=name, grid=(2,),
        in_specs=[col(lambda p: (0, 3 * p)), col(lambda p: (0, 3 * p + 1)), col(lambda p: (0, 3 * p + 2)),
                  col(lambda p: (0, p)), col(lambda p: (0, SSD_WIDTH // LANES + p)), ANY] + specs,
        out_specs=pl.BlockSpec((s, 3 * LANES), lambda p: (0, p)),
        out_shape=jax.ShapeDtypeStruct(dproj.shape, BF16),
        input_output_aliases={5: 0},
        scratch_shapes=[pltpu.VMEM((s, LANES), F32), pltpu.VMEM((s, LANES), F32)],
        compiler_params=_params(("parallel",)),
    )(proj, proj, proj, o, dyall, dproj, *ops)


def adamw(w, g, m, v, name):
    b, r, c = w.shape
    tr = max([t for t in range(8, min(r, 512) + 1, 8) if r % t == 0], default=r)

    def body(w_ref, g_ref, m_ref, v_ref, d_ref, nm_ref, nv_ref):
        gv = g_ref[...]
        nm = ADAM_B1 * m_ref[...] + (1.0 - ADAM_B1) * gv
        nv = ADAM_B2 * v_ref[...] + (1.0 - ADAM_B2) * (gv * gv)
        m_hat = nm / (1.0 - ADAM_B1 ** ADAM_STEP)
        v_hat = nv / (1.0 - ADAM_B2 ** ADAM_STEP)
        d_ref[...] = -ADAM_LR * (m_hat / (jnp.sqrt(v_hat) + ADAM_EPS) + ADAM_WD * w_ref[...])
        nm_ref[...] = nm
        nv_ref[...] = nv

    blk = pl.BlockSpec((1, tr, c), lambda i, j: (i, j, 0))
    return pl.pallas_call(
        body, name=name, grid=(b, r // tr),
        in_specs=[blk] * 4, out_specs=[blk] * 3,
        out_shape=[jax.ShapeDtypeStruct(w.shape, F32)] * 3,
        compiler_params=_params(("parallel", "parallel")),
    )(w, g, m, v)


def _position():
    return lax.axis_index("x"), lax.axis_index("y"), lax.axis_index("c")


def _flipped(pos, flip):
    return tuple((1 - p) if f else p for p, f in zip(pos, flip))


FLIP_C = (0, 0, 1)
CHIP_FLIPS = {1: (0, 1, 0), 2: (1, 0, 0), 3: (1, 1, 0)}
SHARD_ROWS = (SHARD_IN, SHARD_OUT, SHARD_FF, SHARD_FF, SHARD_FF)


def _rows(start, size):
    return pl.ds(pl.multiple_of(start, 16), size)


HBM = pl.BlockSpec(memory_space=pltpu.HBM)
SEM = pl.BlockSpec(memory_space=pltpu.SEMAPHORE)
EFFECT = pltpu.SideEffectType.DATAFLOW_SIDE_EFFECTING


def _in_hbm(a):
    return pltpu.with_memory_space_constraint(a, pltpu.HBM)


def _landing(shape, dtype):
    return _in_hbm(lax.empty(shape, dtype))


def _copies(plan, pos, src_refs, land_refs, send_sems, recv_sems):
    return [pltpu.make_async_remote_copy(src_ref=src, dst_ref=dst, send_sem=send_sems.at[k], recv_sem=recv_sems.at[k],
                                         device_id=_flipped(pos, flip), device_id_type=MESH)
            for k, (src, dst, flip) in enumerate(plan(pos, src_refs, land_refs))]


def exchange_start(name, srcs, lands, n, plan, after=None):
    ns, nl = len(srcs), len(lands)
    specs, ops = _after(after)

    def body(*refs):
        src_refs, land_refs = refs[:ns], refs[ns:ns + nl]
        send_sems, recv_sems, token = refs[ns + nl + len(ops)], refs[ns + nl + len(ops) + 1], refs[-1]
        for cp in _copies(plan, _position(), src_refs, land_refs, send_sems, recv_sems):
            cp.start()
        token[...] = jnp.zeros_like(token)

    thru = [pltpu.HBM(a.shape, a.dtype) for a in list(srcs) + list(lands)]
    out = pl.pallas_call(
        body, name=name,
        out_shape=(pltpu.SemaphoreType.DMA((n,)), pltpu.SemaphoreType.DMA((n,)), *thru, jax.ShapeDtypeStruct((8, LANES), F32)),
        in_specs=[HBM] * (ns + nl) + specs,
        out_specs=(SEM, SEM, *([HBM] * (ns + nl)), pl.BlockSpec(memory_space=pltpu.VMEM)),
        input_output_aliases={k: 2 + k for k in range(ns + nl)},
        compiler_params=pltpu.CompilerParams(has_side_effects=EFFECT),
    )(*[_in_hbm(a) for a in srcs], *lands, *ops)
    return out[0], out[1], list(out[2:2 + ns]), list(out[2 + ns:2 + ns + nl]), out[-1]


def exchange_wait(name, started, after, plan):
    send_sems, recv_sems, srcs, lands, _ = started
    ns, nl = len(srcs), len(lands)
    specs, ops = _after(after)

    def body(*refs):
        src_refs, land_refs = refs[:ns], refs[ns:ns + nl]
        send_sems, recv_sems = refs[ns + nl], refs[ns + nl + 1]
        for cp in _copies(plan, _position(), src_refs, land_refs, send_sems, recv_sems):
            cp.wait_send()
            cp.wait_recv()

    out = pl.pallas_call(
        body, name=name,
        out_shape=tuple(pltpu.HBM(a.shape, a.dtype) for a in list(srcs) + list(lands)),
        in_specs=[HBM] * (ns + nl) + [SEM, SEM] + specs,
        out_specs=tuple([HBM] * (ns + nl)),
        input_output_aliases={k: k for k in range(ns + nl)},
        compiler_params=pltpu.CompilerParams(has_side_effects=EFFECT),
    )(*srcs, *lands, send_sems, recv_sems, *ops)
    return list(out[:ns]), list(out[ns:])


def _gather_ici_plan(pos, srcs, lands):
    chip, c = 2 * pos[0] + pos[1], pos[2]
    copies = []
    for src, dst in zip(srcs, lands):
        r = src.shape[0]
        h = r // 2
        for f in (1, 2, 3):
            copies.append((src.at[_rows(c * h, h)], dst.at[_rows(chip * r + c * h, h)], CHIP_FLIPS[f]))
    return copies


def _gather_d2d_plan(pos, srcs, lands):
    chip, c = 2 * pos[0] + pos[1], pos[2]
    copies = []
    for own, dst in zip(srcs, lands):
        r = own.shape[0]
        h = r // 2
        copies.append((own, dst.at[_rows(chip * r, r)], FLIP_C))
        for f in (1, 2, 3):
            at = _rows(lax.bitwise_xor(chip, f) * r + c * h, h)
            copies.append((dst.at[at], dst.at[at], FLIP_C))
    return copies


def gather_ici_start(shards, after=None):
    lands = [_landing((N_CHIPS * a.shape[0], D_MODEL), BF16) for a in shards]
    return exchange_start("gather_ici_start", shards, lands, 3 * len(shards), _gather_ici_plan, after=after)


def gather_d2d_start(shards, fulls, after=None):
    return exchange_start("gather_d2d_start", shards, fulls, 4 * len(shards), _gather_d2d_plan, after=after)


def _reduce_d2d_plan(pos, srcs, lands):
    c = pos[2]
    return [(src.at[:, _rows((1 - c) * (src.shape[1] // 2), src.shape[1] // 2)], dst, FLIP_C) for src, dst in zip(srcs, lands)]


def _reduce_ici_plan(pos, srcs, lands):
    chip = 2 * pos[0] + pos[1]
    return [(src.at[lax.bitwise_xor(chip, f)], dst.at[f - 1], CHIP_FLIPS[f]) for src, dst in zip(srcs, lands) for f in (1, 2, 3)]


def _reduce_swap_plan(pos, srcs, lands):
    c = pos[2]
    copies = []
    for dst in lands:
        h = dst.shape[0] // 2
        at = _rows(c * h, h)
        copies.append((dst.at[at], dst.at[at], FLIP_C))
    return copies


def reduce_d2d_start(grads):
    lands = [_landing((N_CHIPS, g.shape[1] // 2, D_MODEL), BF16) for g in grads]
    return exchange_start("reduce_d2d_start", grads, lands, len(grads), _reduce_d2d_plan)


def reduce_ici_start(chip_sums):
    lands = [_landing((N_CHIPS - 1,) + p.shape[1:], BF16) for p in chip_sums]
    return exchange_start("reduce_ici_start", chip_sums, lands, 3 * len(chip_sums), _reduce_ici_plan)


def reduce_swap_start(mine):
    return exchange_start("reduce_swap_start", [], mine, len(mine), _reduce_swap_plan)


def _by_shape(fn, *lists):
    groups, out = {}, [None] * len(lists[0])
    for k, a in enumerate(lists[0]):
        groups.setdefault(a.shape, []).append(k)
    for idx in groups.values():
        for k, r in zip(idx, fn(*[[l[k] for k in idx] for l in lists])):
            out[k] = r
    return out


def add_halves(ds, recvs, half, name):
    n = len(ds)
    nch, r, c = ds[0].shape
    h = r // 2

    def body(half_ref, *refs):
        for k in range(n):
            refs[2 * n + k][...] = (refs[k][...].astype(F32) + refs[n + k][...].astype(F32)).astype(BF16)

    mine = pl.BlockSpec((1, h, c), lambda j, hf: (j, hf[0], 0))
    whole = pl.BlockSpec((1, h, c), lambda j, hf: (j, 0, 0))
    return pl.pallas_call(
        body, name=name,
        grid_spec=pltpu.PrefetchScalarGridSpec(
            num_scalar_prefetch=1, grid=(nch,), in_specs=[mine] * n + [whole] * n, out_specs=[whole] * n),
        out_shape=[jax.ShapeDtypeStruct(rv.shape, BF16) for rv in recvs],
        compiler_params=_params(("parallel",)),
    )(half, *ds, *recvs)


def add_chips(ps, recvs, chip, name):
    n = len(ps)
    _, r, c = ps[0].shape

    def body(chip_ref, *refs):
        for k in range(n):
            acc = refs[k][0].astype(F32)
            for f in range(N_CHIPS - 1):
                acc = acc + refs[n + k][f].astype(F32)
            refs[2 * n + k][...] = acc

    return pl.pallas_call(
        body, name=name,
        grid_spec=pltpu.PrefetchScalarGridSpec(
            num_scalar_prefetch=1, grid=(1,),
            in_specs=[pl.BlockSpec((1, r, c), lambda i, ch: (ch[0], 0, 0))] * n +
                     [pl.BlockSpec((N_CHIPS - 1, r, c), lambda i, ch: (0, 0, 0))] * n,
            out_specs=[pl.BlockSpec((r, c), lambda i, ch: (ch[1], 0))] * n),
        out_shape=[jax.ShapeDtypeStruct((2 * r, c), F32)] * n,
        compiler_params=_params(("arbitrary",)),
    )(chip, *ps, *recvs)


def adamw_layers(w, gs, m, v, name):
    b, r, c = w.shape
    tr = max([t for t in range(8, min(r, 512) + 1, 8) if r % t == 0], default=r)

    def body(w_ref, m_ref, v_ref, *rest):
        g_refs, (g_ref, d_ref, nm_ref, nv_ref) = rest[:b], rest[b:]
        layer = pl.program_id(0)
        gv = g_refs[0][...]
        for l in range(1, b):
            gv = jnp.where(layer == l, g_refs[l][...], gv)
        nm = ADAM_B1 * m_ref[0] + (1.0 - ADAM_B1) * gv
        nv = ADAM_B2 * v_ref[0] + (1.0 - ADAM_B2) * (gv * gv)
        m_hat = nm / (1.0 - ADAM_B1 ** ADAM_STEP)
        v_hat = nv / (1.0 - ADAM_B2 ** ADAM_STEP)
        g_ref[0] = gv
        d_ref[0] = -ADAM_LR * (m_hat / (jnp.sqrt(v_hat) + ADAM_EPS) + ADAM_WD * w_ref[0])
        nm_ref[0] = nm
        nv_ref[0] = nv

    nr, tc = r // tr, (c if tr < r else _tile(c, 256))
    steps = nr * (c // tc)
    blk = pl.BlockSpec((1, tr, tc), lambda i, j: (i, j % nr, j // nr))
    g_specs = [pl.BlockSpec((tr, tc), lambda i, j, l=l: (jnp.where(i == l, j % nr, jnp.where(i < l, 0, nr - 1)),
                                                         jnp.where(i == l, j // nr, jnp.where(i < l, 0, c // tc - 1))))
               for l in range(b)]
    return pl.pallas_call(
        body, name=name, grid=(b, steps),
        in_specs=[blk] * 3 + g_specs, out_specs=[blk] * 4,
        out_shape=[jax.ShapeDtypeStruct(w.shape, F32)] * 4,
        compiler_params=_params(("arbitrary", "arbitrary")),
    )(w, m, v, *gs)


def small_allreduce(v, name, after=None):
    r, c = v.shape
    specs, ops = _after(after)

    def body(v_ref, *rest):
        o_ref, buf, send_sems, recv_sems = rest[len(ops):]
        pos = _position()
        me = 4 * pos[0] + 2 * pos[1] + pos[2]
        buf[0] = v_ref[...]
        copies = []
        for f in range(1, 8):
            flip = ((f >> 2) & 1, (f >> 1) & 1, f & 1)
            cp = pltpu.make_async_remote_copy(
                src_ref=v_ref, dst_ref=buf.at[f], send_sem=send_sems.at[f - 1], recv_sem=recv_sems.at[f - 1],
                device_id=_flipped(pos, flip), device_id_type=MESH)
            cp.start()
            copies.append(cp)
        for cp in copies:
            cp.wait()
        acc = buf[me]
        for d in range(1, 8):
            acc = acc + buf[lax.bitwise_xor(me, d)]
        o_ref[...] = acc

    return pl.pallas_call(
        body, name=name,
        in_specs=[pl.BlockSpec(memory_space=pltpu.VMEM)] + specs, out_specs=pl.BlockSpec(memory_space=pltpu.VMEM),
        out_shape=jax.ShapeDtypeStruct((r, c), F32),
        scratch_shapes=[pltpu.VMEM((8, r, c), F32), pltpu.SemaphoreType.DMA((7,)), pltpu.SemaphoreType.DMA((7,))],
    )(v, *ops)


def _all_devices_plan(pos, srcs, lands):
    return [(srcs[0], lands[0].at[f], ((f >> 2) & 1, (f >> 1) & 1, f & 1)) for f in range(1, 8)]


def sum_devices(v, gathered, me, name):
    r, c = v.shape

    def body(me_ref, v_ref, g_ref, o_ref):
        own = v_ref[...]
        acc = None
        for d in range(8):
            slot = lax.bitwise_xor(me_ref[0], d)
            term = jnp.where(slot == 0, own, g_ref[slot])
            acc = term if acc is None else acc + term
        o_ref[...] = acc

    return pl.pallas_call(
        body, name=name,
        grid_spec=pltpu.PrefetchScalarGridSpec(
            num_scalar_prefetch=1, grid=(1,),
            in_specs=[pl.BlockSpec((r, c), lambda i, m: (0, 0)), pl.BlockSpec((8, r, c), lambda i, m: (0, 0, 0))],
            out_specs=pl.BlockSpec((r, c), lambda i, m: (0, 0))),
        out_shape=jax.ShapeDtypeStruct((r, c), F32),
        compiler_params=_params(("arbitrary",)),
    )(me, v, gathered)


_IN_SEGMENTS = ((0, 1544, 128), (128, 1800, 128), (256, 2056, 128), (384, 1672, 128), (512, 1928, 128), (640, 2184, 128),
                (OFF_Z, 0, SSD_WIDTH), (OFF_DT, 1536, SSD_HEADS), (OFF_XBC, 512, CONV_DIM), (OFF_P, 2312, POOL_WIDTH))


def _in_column_map():
    m = np.full((D_INP,), -1, np.int64)
    for at, orig, n in _IN_SEGMENTS:
        cols = np.arange(orig, orig + n)
        m[at:at + n] = (cols // COLS_IN) * SHARD_IN + cols % COLS_IN
    return m


def take_rows(a, idx, name):
    dep, r_in, c = a.shape
    blk = 2 * LANES if len(idx) % (2 * LANES) == 0 and r_in % (2 * LANES) == 0 else LANES
    n_out, n_in = len(idx) // blk, r_in // blk
    assert len(idx) % blk == 0 and r_in % blk == 0
    sources = [sorted({int(v) // blk for v in idx[blk * i:blk * (i + 1)] if v >= 0}) for i in range(n_out)]
    width = max(len(s) for s in sources)
    table = np.zeros((n_out, width), np.int32)
    for i, s in enumerate(sources):
        spare = [b for b in range(n_in) if b not in s][:width - len(s)]
        table[i] = s + spare

    def body(tbl_ref, idx_ref, *refs):
        in_refs, o_ref = refs[:width], refs[width]
        i = pl.program_id(1)
        src = idx_ref[...]
        acc = jnp.zeros((blk, c), F32)
        for k in range(width):
            pick = (src == tbl_ref[i, k] * blk + _iota2((blk, blk), 1)).astype(BF16)
            acc = acc + _nn(pick, in_refs[k][0])
        o_ref[0] = acc.astype(BF16)

    return pl.pallas_call(
        body, name=name,
        grid_spec=pltpu.PrefetchScalarGridSpec(
            num_scalar_prefetch=1, grid=(dep, n_out),
            in_specs=[pl.BlockSpec((blk, 1), lambda l, i, t: (i, 0))] +
                     [pl.BlockSpec((1, blk, c), lambda l, i, t, k=k: (l, t[i, k], 0)) for k in range(width)],
            out_specs=pl.BlockSpec((1, blk, c), lambda l, i, t: (l, i, 0))),
        out_shape=jax.ShapeDtypeStruct((dep, len(idx), c), BF16),
        compiler_params=_params(("parallel", "parallel")),
    )(jnp.asarray(table), jnp.asarray(np.asarray(idx, np.int32).reshape(-1, 1)), *([a] * width))


def _in_weight_layout(staged):
    return take_rows(staged, _in_column_map(), "w_in_layout")


def _in_gradient_layout(dwt):
    fwd = _in_column_map()
    inv = np.full((N_CHIPS * SHARD_IN,), -1, np.int64)
    inv[fwd[fwd >= 0]] = np.nonzero(fwd >= 0)[0]
    return take_rows(dwt, inv, "dw_in_layout")


SMALL_NAMES = ("norm1_w", "conv_w", "conv_b", "dt_bias", "a_log", "d_skip", "ssd_norm_w", "pool_w", "pool_b",
               "pool_scale", "norm2_w", "final_norm_w")
SMALL_ROWS = 160


def _small_rows(shape):
    return -(-int(np.prod(shape)) // (8 * D_MODEL)) * 8


def _pack_small(parts):
    rows = []
    for p in parts:
        flat = p.reshape(-1)
        rows.append(jnp.pad(flat, (0, _small_rows(p.shape) * D_MODEL - flat.shape[0])).reshape(-1, D_MODEL))
    used = sum(r.shape[0] for r in rows)
    return jnp.concatenate(rows + [jnp.zeros((SMALL_ROWS - used, D_MODEL), F32)], axis=0)


def _unpack_small(packed, shapes):
    out, at = [], 0
    for shp in shapes:
        n, r = int(np.prod(shp)), _small_rows(shp)
        out.append(packed[at:at + r].reshape(-1)[:n].reshape(shp))
        at += r
    return out


def kernel(x, norm1_w, w_in, conv_w, conv_b, dt_bias, a_log, d_skip, ssd_norm_w, pool_w, pool_b, pool_scale, w_out, norm2_w, w_gate, w_up, w_down, final_norm_w, loss_target, m_norm1_w, m_w_in, m_conv_w, m_conv_b, m_dt_bias, m_a_log, m_d_skip, m_ssd_norm_w, m_pool_w, m_pool_b, m_pool_scale, m_w_out, m_norm2_w, m_w_gate, m_w_up, m_w_down, m_final_norm_w, v_norm1_w, v_w_in, v_conv_w, v_conv_b, v_dt_bias, v_a_log, v_d_skip, v_ssd_norm_w, v_pool_w, v_pool_b, v_pool_scale, v_w_out, v_norm2_w, v_w_gate, v_w_up, v_w_down, v_final_norm_w):
    px, py, pc = _position()
    chip = 2 * px + py
    chip_arr = jnp.reshape(chip, (1,)).astype(jnp.int32)
    half_arr = jnp.reshape(pc, (1,)).astype(jnp.int32)

    def layer_shards(l):
        w_in_t = jnp.pad(jnp.swapaxes(w_in[l], 0, 1).astype(BF16), ((0, SHARD_IN - COLS_IN), (0, 0)))
        return [w_in_t, w_out[l].astype(BF16), jnp.swapaxes(w_gate[l], 0, 1).astype(BF16),
                jnp.swapaxes(w_up[l], 0, 1).astype(BF16), w_down[l].astype(BF16)]

    shards0 = layer_shards(0)
    head = gather_ici_start(shards0[:1])
    over_ici = {}

    def pass_on(l, after):
        own, arrived = exchange_wait("gather_ici_wait", over_ici[l], after, _gather_ici_plan)
        swap = gather_d2d_start(own, arrived)
        tokens = [swap[4]]
        if l + 1 < DEPTH:
            over_ici[l + 1] = gather_ici_start(layer_shards(l + 1), after=swap[4])
            tokens.append(over_ici[l + 1][4])
        return swap, tokens

    def weights_of(swap, after):
        _, (w_in_st, w_out_l, w_gate_t, w_up_t, w_down_l) = exchange_wait("gather_d2d_wait", swap, after, _gather_d2d_plan)
        return _in_weight_layout(w_in_st[None])[0], w_out_l, w_gate_t, w_up_t, w_down_l

    pad_heads = lambda v: jnp.pad(v, ((0, 0), (0, LANES - SSD_HEADS)))[:, None, :]
    dtb, alog = pad_heads(dt_bias), pad_heads(a_log)
    dskip_x = jnp.repeat(d_skip, HEAD_DIM, axis=1)[:, None, :]
    eye = jnp.eye(len(POOL_WINDOWS), dtype=F32)
    wbd = (pool_w[:, :, :, None, :] * eye[None, :, None, :, None]).reshape(DEPTH, POOL_WIDTH, POOL_WIDTH).astype(BF16)
    pool_b2 = pool_b.reshape(DEPTH, 1, POOL_WIDTH)
    cw_cols = lax.dynamic_update_slice(jnp.zeros((DEPTH, CONV_WIDTH, CONV_DIM), F32), conv_w,
                                       (0, 0, chip * (CONV_DIM // N_CHIPS)))
    cw_cols = jnp.where(pc == 0, cw_cols, 0.0)
    cw_rows = (DEPTH * CONV_WIDTH * CONV_DIM) // D_MODEL
    conv_w_f = small_allreduce(jnp.pad(cw_cols.reshape(cw_rows, D_MODEL), ((0, 8), (0, 0))), "gather_conv_w")
    conv_w_f = conv_w_f[:cw_rows].reshape(DEPTH, CONV_WIDTH, CONV_DIM)
    cw8 = jnp.pad(conv_w_f, ((0, 0), (0, 8 - CONV_WIDTH), (0, 0)))

    h = x[0]
    saved, weights = [], []
    own, arrived = exchange_wait("gather_ici_wait", head, head[4], _gather_ici_plan)
    head = gather_d2d_start(own, arrived)
    over_ici[0] = gather_ici_start(shards0[1:], after=head[4])
    w_in_f = _in_weight_layout(exchange_wait("gather_d2d_wait", head, [head[4], over_ici[0][4]], _gather_d2d_plan)[1][0][None])[0]
    for l in range(DEPTH):
        if l > 0:
            w_in_f, w_out_f, w_gate_t, w_up_t, w_down_f = weights[l]
        proj = rms_matmul(h, norm1_w[l][None], w_in_f, "in_proj")
        xc = conv_forward(proj, cw8[l], conv_b[l][None], "conv_fwd")
        y_all, ycore, states = ssd_forward(proj, xc, dtb[l], alog[l], dskip_x[l], ssd_norm_w[l][None], "ssd_fwd")
        if l == 0:
            swap, tokens = pass_on(0, y_all)
            y_all, o_sb = sb_forward(proj, y_all, "sb_fwd", after=tokens)
            tokens = None
        else:
            y_all, o_sb = sb_forward(proj, y_all, "sb_fwd")
            swap, tokens = pass_on(l + 1, o_sb) if l + 1 < DEPTH else (None, None)
        y_all = pool_forward(proj, wbd[l], pool_b2[l], pool_scale[l][None], y_all, "pool_fwd", after=tokens)
        if l == 0:
            w_out_f, w_gate_t, w_up_t, w_down_f = exchange_wait("gather_d2d_wait", swap, y_all, _gather_d2d_plan)[1]
            weights.append((w_in_f, w_out_f, w_gate_t, w_up_t, w_down_f))
        x1 = matmul_residual(y_all, w_out_f, h, "out_proj")
        x2, g, u = ffn_forward(x1, norm2_w[l][None], w_gate_t, w_up_t, w_down_f, "ffn_fwd")
        if l == 0:
            swap, tokens = pass_on(1, x2)
            weights.append(weights_of(swap, tokens))
        elif swap is not None:
            weights.append(weights_of(swap, x2))
        saved.append((h, proj, xc, ycore, states, o_sb, y_all, x1, g, u))
        h = x2

    loss_part, dx, dxb, d_final = loss_head(h, final_norm_w[None], loss_target[0], "loss_head")
    loss = lax.psum(loss_part[0, 0], ("x", "y", "c"))

    small = {n: [None] * DEPTH for n in SMALL_NAMES if n != "final_norm_w"}
    chip_half = jnp.concatenate([chip_arr, half_arr])
    reduced = {}
    d2d = ici = early = None

    def add_cores(d2d, after):
        mine, theirs = exchange_wait("reduce_d2d_wait", d2d[1], after, _reduce_d2d_plan)
        return d2d[0], reduce_ici_start(_by_shape(lambda ds, ts: add_halves(ds, ts, half_arr, "reduce_add_halves"), mine, theirs))

    def add_all(ici, after):
        sums, theirs = exchange_wait("reduce_ici_wait", ici[1], after, _reduce_ici_plan)
        return ici[0], reduce_swap_start(_by_shape(lambda ps, ts: add_chips(ps, ts, chip_half, "reduce_add_chips"), sums, theirs))

    def finish(swap, after):
        reduced[swap[0]] = exchange_wait("reduce_swap_wait", swap[1], after, _reduce_swap_plan)[1]

    swaps = []
    for l in reversed(range(DEPTH)):
        xin, proj, xc, ycore, states, o_sb, y_all, x1, g, u = saved[l]
        w_in_f, w_out_f, w_gate_t, w_up_t, w_down_f = weights[l]
        dg, du, act = ffn_backward_act(dxb, g, u, w_down_f, "ffn_bwd_act", after=None if d2d is None else d2d[1][4])
        dx1, dx1b, h2, dn2 = rms_backward([dg, du], [w_gate_t, w_up_t], x1, norm2_w[l][None], dx, "ffn_bwd_norm", 512)
        if d2d is not None:
            ici = add_cores(d2d, dx1b)
        dyall = matmul_nt(dx1b, w_out_f, "out_proj_bwd", after=None if ici is None else ici[1][4])
        dw_down = matmul_tn(act, dxb, "dw_down")
        dw_gate = matmul_tn(dg, h2, "dw_gate")
        dw_up = matmul_tn(du, h2, "dw_up")
        dw_out = matmul_tn(y_all, dx1b, "dw_out")
        late = [dw.reshape(N_CHIPS, r, D_MODEL) for dw, r in zip((dw_out, dw_gate, dw_up, dw_down), SHARD_ROWS[1:])]
        if l == 0:
            early = ("0 late", reduce_d2d_start(late))
        dxc, dproj, dsn, ddsk, ddtb, dalog = ssd_backward(proj, xc, ycore, dyall, states, dtb[l], alog[l],
                                                          dskip_x[l], ssd_norm_w[l][None], "ssd_bwd")
        dproj, dcw, dcb = conv_backward(proj, dxc, cw8[l], conv_b[l][None], dproj, "conv_bwd",
                                        after=None if early is None else early[1][4])
        if early is not None:
            early = add_cores(early, dproj)
        dproj = sb_backward(proj, o_sb, dyall, dproj, "sb_bwd", after=None if early is None else early[1][4])
        dproj, dwbd, dpb, dps = pool_backward(proj, dyall, wbd[l], pool_b2[l], pool_scale[l][None], dproj, "pool_bwd")
        if ici is not None:
            swaps.append(add_all(ici, dproj))
            ici = None
        dx, dxb, h1, dn1 = rms_backward([dproj], [w_in_f], xin, norm1_w[l][None], dx1, "in_proj_bwd", 512,
                                        after=swaps[-1][1][4] if swaps else None)
        dw_in = _in_gradient_layout(matmul_tn(dproj, h1, "dw_in")[None])[0].reshape(N_CHIPS, SHARD_IN, D_MODEL)
        d2d = (l, reduce_d2d_start([dw_in] if l == 0 else [dw_in] + late))
        small["norm1_w"][l] = dn1[0]
        small["conv_w"][l] = dcw[:CONV_WIDTH]
        small["conv_b"][l] = dcb[0]
        small["dt_bias"][l] = ddtb[0, :SSD_HEADS]
        small["a_log"][l] = dalog[0, :SSD_HEADS]
        small["d_skip"][l] = ddsk.reshape(SSD_HEADS, HEAD_DIM).sum(axis=1)
        small["ssd_norm_w"][l] = dsn[0]
        small["pool_w"][l] = jnp.stack([dwbd[64 * k:64 * k + 64, 64 * k:64 * k + 64] for k in range(len(POOL_WINDOWS))])
        small["pool_b"][l] = dpb.reshape(len(POOL_WINDOWS), -1)
        small["pool_scale"][l] = dps[0]
        small["norm2_w"][l] = dn2[0]
    grad_x = dx[None]

    ici = add_cores(d2d, d2d[1][4])
    small_parts = [d_final if n == "final_norm_w" else jnp.stack(small[n]) for n in SMALL_NAMES]
    small_start = exchange_start("reduce_small_start", [_pack_small(small_parts)],
                                 [_landing((8, SMALL_ROWS, D_MODEL), F32)], 7, _all_devices_plan, after=ici[1][4])
    swaps.append(add_all(early, small_start[4]))
    swaps.append(add_all(ici, swaps[-1][1][4]))
    for swap in swaps:
        finish(swap, swaps[-1][1][4])
    (small_own,), (small_all,) = exchange_wait("reduce_small_wait", small_start, reduced[0][0], _all_devices_plan)
    small_sum = sum_devices(small_own, small_all, jnp.reshape(4 * px + 2 * py + pc, (1,)).astype(jnp.int32), "reduce_small_sum")
    reduced[0] = reduced[0] + reduced["0 late"]
    g_big = {n: [reduced[l][k] for l in range(DEPTH)] for k, n in enumerate(("w_in", "w_out", "w_gate", "w_up", "w_down"))}
    g_big["w_in"] = [gl[:COLS_IN] for gl in g_big["w_in"]]
    transposed = ("w_in", "w_gate", "w_up")

    g_small = dict(zip(SMALL_NAMES, _unpack_small(small_sum, [p.shape for p in small_parts])))
    g_small["final_norm_w"] = g_small["final_norm_w"].reshape(final_norm_w.shape)
    g_small["conv_w"] = lax.dynamic_slice_in_dim(g_small["conv_w"], chip * (CONV_DIM // N_CHIPS), CONV_DIM // N_CHIPS, axis=2)

    given = dict(norm1_w=(norm1_w, m_norm1_w, v_norm1_w), w_in=(w_in, m_w_in, v_w_in), conv_w=(conv_w, m_conv_w, v_conv_w),
                 conv_b=(conv_b, m_conv_b, v_conv_b), dt_bias=(dt_bias, m_dt_bias, v_dt_bias), a_log=(a_log, m_a_log, v_a_log),
                 d_skip=(d_skip, m_d_skip, v_d_skip), ssd_norm_w=(ssd_norm_w, m_ssd_norm_w, v_ssd_norm_w),
                 pool_w=(pool_w, m_pool_w, v_pool_w), pool_b=(pool_b, m_pool_b, v_pool_b),
                 pool_scale=(pool_scale, m_pool_scale, v_pool_scale), w_out=(w_out, m_w_out, v_w_out),
                 norm2_w=(norm2_w, m_norm2_w, v_norm2_w), w_gate=(w_gate, m_w_gate, v_w_gate), w_up=(w_up, m_w_up, v_w_up),
                 w_down=(w_down, m_w_down, v_w_down), final_norm_w=(final_norm_w, m_final_norm_w, v_final_norm_w))
    order = ("norm1_w", "w_in", "conv_w", "conv_b", "dt_bias", "a_log", "d_skip", "ssd_norm_w", "pool_w", "pool_b",
             "pool_scale", "w_out", "norm2_w", "w_gate", "w_up", "w_down", "final_norm_w")
    grads = dict(g_small)
    results = {}
    for n in ("w_in", "w_out", "w_gate", "w_up", "w_down"):
        w, m, v = given[n]
        if n in transposed:
            out = adamw_layers(jnp.swapaxes(w, 1, 2), g_big[n], jnp.swapaxes(m, 1, 2), jnp.swapaxes(v, 1, 2), "adamw_" + n)
            out = [jnp.swapaxes(o, 1, 2) for o in out]
        else:
            out = adamw_layers(w, g_big[n], m, v, "adamw_" + n)
        grads[n], results[n] = out[0], tuple(out[1:])
    small_shapes = [given[n][0].shape for n in SMALL_NAMES]
    packed = [_pack_small([given[n][k] for n in SMALL_NAMES])[None] for k in range(3)]
    packed_g = _pack_small([grads[n] for n in SMALL_NAMES])[None]
    small_out = adamw(packed[0], packed_g, packed[1], packed[2], "adamw_small")
    small_out = [_unpack_small(o[0], small_shapes) for o in small_out]
    for i, n in enumerate(SMALL_NAMES):
        results[n] = tuple(small_out[k][i] for k in range(3))

    return (loss, grad_x, *[grads[n] for n in order], *[results[n][0] for n in order],
            *[results[n][1] for n in order], *[results[n][2] for n in order])
```

```python
import numpy as np
import jax
import jax.numpy as jnp
from jax import lax
from jax.experimental import pallas as pl
from jax.experimental.pallas import tpu as pltpu

F32 = jnp.float32
BF16 = jnp.bfloat16
MESH = pl.DeviceIdType.MESH
ANY = pl.BlockSpec(memory_space=pl.ANY)

D_MODEL = 1024
DEPTH = 4
EPS = 1e-6
SSD_WIDTH = 512
SSD_HEADS = 8
HEAD_DIM = 64
D_STATE = 128
CHUNK = 128
CONV_WIDTH = 4
CONV_DIM = 1024
SB_WIDTH = 256
POOL_WIDTH = 256
POOL_WINDOWS = (2, 4, 8, 16)
D_FF = 2816
D_IN = 2568
N_CHIPS = 4
OFF_QKV, OFF_Z, OFF_DT, OFF_XBC, OFF_P = 0, 768, 1280, 1536, 2560
D_INP = 2816
ZDT = 768
SHARD_IN, SHARD_OUT, SHARD_FF = 704, 256, 704
COLS_IN = 642
ADAM_LR, ADAM_B1, ADAM_B2, ADAM_EPS, ADAM_WD, ADAM_STEP = 0.001, 0.9, 0.999, 1e-08, 0.01, 10
LANES = 128
VMEM_LIMIT = 56 * 1024 * 1024


def _params(sem=None):
    return pltpu.CompilerParams(dimension_semantics=sem, vmem_limit_bytes=VMEM_LIMIT)


def _tile(n, cap):
    best = None
    for t in range(LANES, min(n, cap) + 1, LANES):
        if n % t == 0:
            best = t
    assert best is not None, (n, cap)
    return best


def _nt(a, b):
    return lax.dot_general(a, b, (((1,), (1,)), ((), ())), preferred_element_type=F32)


def _tn(a, b):
    return lax.dot_general(a, b, (((0,), (0,)), ((), ())), preferred_element_type=F32)


def _nn(a, b):
    return jnp.dot(a, b, preferred_element_type=F32)


def _split_dot(a, b_exact, terms=3, dot=_nn):
    acc = None
    rest = a
    for _ in range(terms):
        hi = rest.astype(BF16)
        part = dot(hi, b_exact)
        acc = part if acc is None else acc + part
        rest = rest - hi.astype(F32)
    return acc


def _split_dot_left(a_exact, b, terms=3):
    acc = None
    rest = b
    for _ in range(terms):
        hi = rest.astype(BF16)
        part = _nn(a_exact, hi)
        acc = part if acc is None else acc + part
        rest = rest - hi.astype(F32)
    return acc


def _sigmoid(x):
    return 1.0 / (1.0 + jnp.exp(-x))


def _softplus(x):
    return jnp.maximum(x, 0.0) + jnp.log(1.0 + jnp.exp(-jnp.abs(x)))


def _iota2(shape, dim):
    return lax.broadcasted_iota(jnp.int32, shape, dim)


def _after(after):
    ops = [] if after is None else list(after) if isinstance(after, (list, tuple)) else [after]
    return [ANY] * len(ops), ops


def rms_matmul(x, nw, wt, name, after=None):
    s, d = x.shape
    n = wt.shape[0]
    tm, tn = _tile(s, 512), _tile(n, 2816)
    specs, ops = _after(after)

    def body(x_ref, nw_ref, w_ref, *rest):
        o_ref, h_ref = rest[len(ops):]

        @pl.when(pl.program_id(1) == 0)
        def _():
            xv = x_ref[...]
            r = lax.rsqrt(jnp.mean(xv * xv, axis=-1, keepdims=True) + EPS)
            h_ref[...] = (xv * r * nw_ref[...]).astype(BF16)
        o_ref[...] = _nt(h_ref[...], w_ref[...])

    return pl.pallas_call(
        body, name=name, grid=(s // tm, n // tn),
        in_specs=[pl.BlockSpec((tm, d), lambda i, j: (i, 0)), pl.BlockSpec((1, d), lambda i, j: (0, 0)),
                  pl.BlockSpec((tn, d), lambda i, j: (j, 0))] + specs,
        out_specs=pl.BlockSpec((tm, tn), lambda i, j: (i, j)),
        out_shape=jax.ShapeDtypeStruct((s, n), F32),
        scratch_shapes=[pltpu.VMEM((tm, d), BF16)],
        compiler_params=_params(("parallel", "arbitrary")),
    )(x, nw, wt, *ops)


def matmul_residual(a, w, res, name):
    s, k = a.shape
    n = w.shape[1]
    tm, tn = _tile(s, 512), _tile(n, 1024)

    def body(a_ref, w_ref, r_ref, o_ref):
        o_ref[...] = r_ref[...] + _nn(a_ref[...], w_ref[...])

    return pl.pallas_call(
        body, name=name, grid=(s // tm, n // tn),
        in_specs=[pl.BlockSpec((tm, k), lambda i, j: (i, 0)), pl.BlockSpec((k, tn), lambda i, j: (0, j)),
                  pl.BlockSpec((tm, tn), lambda i, j: (i, j))],
        out_specs=pl.BlockSpec((tm, tn), lambda i, j: (i, j)),
        out_shape=jax.ShapeDtypeStruct((s, n), F32),
        compiler_params=_params(("parallel", "parallel")),
    )(a, w, res)


def matmul_nt(a, w, name, out_dtype=F32, after=None):
    s, n = a.shape
    k = w.shape[0]
    tm, tk = _tile(s, 512), _tile(k, 1024)
    specs, ops = _after(after)

    def body(a_ref, w_ref, *rest):
        rest[-1][...] = _nt(a_ref[...], w_ref[...]).astype(out_dtype)

    return pl.pallas_call(
        body, name=name, grid=(s // tm, k // tk),
        in_specs=[pl.BlockSpec((tm, n), lambda i, j: (i, 0)), pl.BlockSpec((tk, n), lambda i, j: (j, 0))] + specs,
        out_specs=pl.BlockSpec((tm, tk), lambda i, j: (i, j)),
        out_shape=jax.ShapeDtypeStruct((s, k), out_dtype),
        compiler_params=_params(("parallel", "parallel")),
    )(a, w, *ops)


def matmul_tn(a, b, name, after=None):
    s, m = a.shape
    n = b.shape[1]
    tm, tn = _tile(m, 512), _tile(n, 1024)

    def body(a_ref, b_ref, *rest):
        rest[-1][...] = _tn(a_ref[...], b_ref[...]).astype(BF16)

    specs, ops = _after(after)
    return pl.pallas_call(
        body, name=name, grid=(m // tm, n // tn),
        in_specs=[pl.BlockSpec((s, tm), lambda i, j: (0, i)), pl.BlockSpec((s, tn), lambda i, j: (0, j))] + specs,
        out_specs=pl.BlockSpec((tm, tn), lambda i, j: (i, j)),
        out_shape=jax.ShapeDtypeStruct((m, n), BF16),
        compiler_params=_params(("parallel", "parallel")),
    )(a, b, *ops)


def ffn_forward(x1, nw, wgt, wut, wd, name):
    s, d = x1.shape
    f = wgt.shape[0]
    tm, tf = _tile(s, 1024), _tile(f, 256)

    def body(x_ref, nw_ref, wg_ref, wu_ref, wd_ref, o_ref, g_ref, u_ref, h_ref, acc_ref):
        j = pl.program_id(1)

        @pl.when(j == 0)
        def _():
            xv = x_ref[...]
            r = lax.rsqrt(jnp.mean(xv * xv, axis=-1, keepdims=True) + EPS)
            h_ref[...] = (xv * r * nw_ref[...]).astype(BF16)
            acc_ref[...] = xv

        h = h_ref[...]
        g = _nt(h, wg_ref[...])
        u = _nt(h, wu_ref[...])
        g_ref[...] = g.astype(BF16)
        u_ref[...] = u.astype(BF16)
        a = (g * _sigmoid(g) * u).astype(BF16)
        acc_ref[...] += _nn(a, wd_ref[...])

        @pl.when(j == pl.num_programs(1) - 1)
        def _():
            o_ref[...] = acc_ref[...]

    wblk = pl.BlockSpec((tf, d), lambda i, j: (j, 0))
    return pl.pallas_call(
        body, name=name, grid=(s // tm, f // tf),
        in_specs=[pl.BlockSpec((tm, d), lambda i, j: (i, 0)), pl.BlockSpec((1, d), lambda i, j: (0, 0)), wblk, wblk, wblk],
        out_specs=[pl.BlockSpec((tm, d), lambda i, j: (i, 0)), pl.BlockSpec((tm, tf), lambda i, j: (i, j)),
                   pl.BlockSpec((tm, tf), lambda i, j: (i, j))],
        out_shape=[jax.ShapeDtypeStruct((s, d), F32), jax.ShapeDtypeStruct((s, f), BF16),
                   jax.ShapeDtypeStruct((s, f), BF16)],
        scratch_shapes=[pltpu.VMEM((tm, d), BF16), pltpu.VMEM((tm, d), F32)],
        compiler_params=_params(("parallel", "arbitrary")),
    )(x1, nw, wgt, wut, wd)


def ffn_backward_act(dx2, g, u, wd, name, after=None):
    s, d = dx2.shape
    f = wd.shape[0]
    tm, tf = _tile(s, 256), _tile(f, 2816)
    specs, ops = _after(after)

    def body(dx_ref, g_ref, u_ref, wd_ref, *rest):
        dg_ref, du_ref, a_ref = rest[len(ops):]
        da = _nt(dx_ref[...], wd_ref[...])
        gv = g_ref[...].astype(F32)
        uv = u_ref[...].astype(F32)
        sg = _sigmoid(gv)
        silu = gv * sg
        dg_ref[...] = (da * uv * (sg * (1.0 + gv * (1.0 - sg)))).astype(BF16)
        du_ref[...] = (da * silu).astype(BF16)
        a_ref[...] = (silu * uv).astype(BF16)

    blk = pl.BlockSpec((tm, tf), lambda i, j: (i, j))
    return pl.pallas_call(
        body, name=name, grid=(s // tm, f // tf),
        in_specs=[pl.BlockSpec((tm, d), lambda i, j: (i, 0)), blk, blk, pl.BlockSpec((tf, d), lambda i, j: (j, 0))] + specs,
        out_specs=[blk, blk, blk],
        out_shape=[jax.ShapeDtypeStruct((s, f), BF16)] * 3,
        compiler_params=_params(("parallel", "parallel")),
    )(dx2, g, u, wd, *ops)


def rms_backward(dzs, wts, x, nw, dres, name, tm, after=None):
    s, d = x.shape
    nz = len(dzs)
    specs, ops = _after(after)

    def body(*refs):
        dz_refs, w_refs = refs[:nz], refs[nz:2 * nz]
        x_ref, nw_ref, dres_ref = refs[2 * nz:2 * nz + 3]
        dx_ref, dxb_ref, h_ref, dnw_ref = refs[2 * nz + 3 + len(ops):]
        dh = _nn(dz_refs[0][...], w_refs[0][...])
        for k in range(1, nz):
            dh = dh + _nn(dz_refs[k][...], w_refs[k][...])
        xv = x_ref[...]
        r = lax.rsqrt(jnp.mean(xv * xv, axis=-1, keepdims=True) + EPS)
        xhat = xv * r
        nwv = nw_ref[...]
        h_ref[...] = (xhat * nwv).astype(BF16)

        @pl.when(pl.program_id(0) == 0)
        def _():
            dnw_ref[...] = jnp.zeros_like(dnw_ref)

        dnw_ref[...] += jnp.sum(dh * xhat, axis=0, keepdims=True)
        gdh = dh * nwv
        dx = dres_ref[...] + r * (gdh - xhat * jnp.mean(gdh * xhat, axis=-1, keepdims=True))
        dx_ref[...] = dx
        dxb_ref[...] = dx.astype(BF16)

    row = pl.BlockSpec((tm, d), lambda i: (i, 0))
    in_specs = [pl.BlockSpec((tm, dz.shape[1]), lambda i: (i, 0)) for dz in dzs]
    in_specs += [pl.BlockSpec(w.shape, lambda i: (0, 0), pipeline_mode=pl.Buffered(1)) for w in wts]
    in_specs += [row, pl.BlockSpec((1, d), lambda i: (0, 0)), row] + specs
    return pl.pallas_call(
        body, name=name, grid=(s // tm,),
        in_specs=in_specs,
        out_specs=[row, row, row, pl.BlockSpec((1, d), lambda i: (0, 0))],
        out_shape=[jax.ShapeDtypeStruct((s, d), F32), jax.ShapeDtypeStruct((s, d), BF16),
                   jax.ShapeDtypeStruct((s, d), BF16), jax.ShapeDtypeStruct((1, d), F32)],
        compiler_params=_params(("arbitrary",)),
    )(*dzs, *wts, x, nw, dres, *ops)


def loss_head(x, nw, target, name):
    s, d = x.shape
    tm = _tile(s, 512)

    def body(x_ref, nw_ref, t_ref, loss_ref, dx_ref, dxb_ref, dnw_ref):
        xv = x_ref[...]
        r = lax.rsqrt(jnp.mean(xv * xv, axis=-1, keepdims=True) + EPS)
        xhat = xv * r
        nwv = nw_ref[...]
        err = xhat * nwv - t_ref[...]

        @pl.when(pl.program_id(0) == 0)
        def _():
            dnw_ref[...] = jnp.zeros_like(dnw_ref)
            loss_ref[...] = jnp.zeros_like(loss_ref)

        part = jnp.sum(jnp.sum(err * err, axis=-1, keepdims=True), axis=0, keepdims=True) * (0.5 / d)
        loss_ref[...] += jnp.broadcast_to(part, loss_ref.shape)
        dout = err * (1.0 / d)
        dnw_ref[...] += jnp.sum(dout * xhat, axis=0, keepdims=True)
        gdh = dout * nwv
        dx = r * (gdh - xhat * jnp.mean(gdh * xhat, axis=-1, keepdims=True))
        dx_ref[...] = dx
        dxb_ref[...] = dx.astype(BF16)

    row = pl.BlockSpec((tm, d), lambda i: (i, 0))
    return pl.pallas_call(
        body, name=name, grid=(s // tm,),
        in_specs=[row, pl.BlockSpec((1, d), lambda i: (0, 0)), row],
        out_specs=[pl.BlockSpec((1, LANES), lambda i: (0, 0)), row, row, pl.BlockSpec((1, d), lambda i: (0, 0))],
        out_shape=[jax.ShapeDtypeStruct((1, LANES), F32), jax.ShapeDtypeStruct((s, d), F32),
                   jax.ShapeDtypeStruct((s, d), BF16), jax.ShapeDtypeStruct((1, d), F32)],
        compiler_params=_params(("arbitrary",)),
    )(x, nw, target)


def _shift_down(x, k):
    return jnp.where(_iota2(x.shape, 0) >= k, pltpu.roll(x, k, axis=0), 0.0)


def _shift_up(x, k):
    s = x.shape[0]
    return jnp.where(_iota2(x.shape, 0) < s - k, pltpu.roll(x, s - k, axis=0), 0.0)


CONV_TILE = 256


def conv_forward(proj, cw, cb, name):
    s = proj.shape[0]
    tn = CONV_TILE
    off = OFF_XBC // tn

    def body(u_ref, w_ref, b_ref, o_ref):
        u = u_ref[...]
        pre = b_ref[...] + w_ref[CONV_WIDTH - 1:CONV_WIDTH, :] * u
        for i in range(CONV_WIDTH - 1):
            pre = pre + w_ref[i:i + 1, :] * _shift_down(u, CONV_WIDTH - 1 - i)
        o_ref[...] = pre * _sigmoid(pre)

    return pl.pallas_call(
        body, name=name, grid=(CONV_DIM // tn,),
        in_specs=[pl.BlockSpec((s, tn), lambda j: (0, off + j)), pl.BlockSpec((8, tn), lambda j: (0, j)),
                  pl.BlockSpec((1, tn), lambda j: (0, j))],
        out_specs=pl.BlockSpec((s, tn), lambda j: (0, j)),
        out_shape=jax.ShapeDtypeStruct((s, CONV_DIM), F32),
        compiler_params=_params(("parallel",)),
    )(proj, cw, cb)


def conv_backward(proj, dxc, cw, cb, dproj, name, after=None):
    s = proj.shape[0]
    tn = CONV_TILE
    off = OFF_XBC // tn

    specs, ops = _after(after)

    def body(u_ref, d_ref, w_ref, b_ref, *rest):
        du_ref, dw_ref, db_ref = rest[-3:]
        u = u_ref[...]
        shifted = [_shift_down(u, CONV_WIDTH - 1 - i) for i in range(CONV_WIDTH - 1)] + [u]
        pre = b_ref[...] + w_ref[CONV_WIDTH - 1:CONV_WIDTH, :] * u
        for i in range(CONV_WIDTH - 1):
            pre = pre + w_ref[i:i + 1, :] * shifted[i]
        sg = _sigmoid(pre)
        dpre = d_ref[...] * (sg * (1.0 + pre * (1.0 - sg)))
        du = w_ref[CONV_WIDTH - 1:CONV_WIDTH, :] * dpre
        for i in range(CONV_WIDTH - 1):
            du = du + w_ref[i:i + 1, :] * _shift_up(dpre, CONV_WIDTH - 1 - i)
        du_ref[...] = du.astype(BF16)
        rows = [jnp.sum(dpre * shifted[i], axis=0, keepdims=True) for i in range(CONV_WIDTH)]
        rows.append(jnp.zeros((8 - CONV_WIDTH, tn), F32))
        dw_ref[...] = jnp.concatenate(rows, axis=0)
        db_ref[...] = jnp.sum(dpre, axis=0, keepdims=True)

    return pl.pallas_call(
        body, name=name, grid=(CONV_DIM // tn,),
        in_specs=[pl.BlockSpec((s, tn), lambda j: (0, off + j)), pl.BlockSpec((s, tn), lambda j: (0, j)),
                  pl.BlockSpec((8, tn), lambda j: (0, j)), pl.BlockSpec((1, tn), lambda j: (0, j)), ANY] + specs,
        out_specs=[pl.BlockSpec((s, tn), lambda j: (0, off + j)), pl.BlockSpec((8, tn), lambda j: (0, j)),
                   pl.BlockSpec((1, tn), lambda j: (0, j))],
        out_shape=[jax.ShapeDtypeStruct(dproj.shape, BF16), jax.ShapeDtypeStruct((8, CONV_DIM), F32),
                   jax.ShapeDtypeStruct((1, CONV_DIM), F32)],
        input_output_aliases={4: 0},
        compiler_params=_params(("parallel",)),
    )(proj, dxc, cw, cb, dproj, *ops)


def _pool_lane_window(shape):
    grp = _iota2(shape, 1) // (POOL_WIDTH // len(POOL_WINDOWS))
    win = jnp.full(shape, POOL_WINDOWS[-1], jnp.int32)
    for gi in range(len(POOL_WINDOWS) - 2, -1, -1):
        win = jnp.where(grp == gi, POOL_WINDOWS[gi], win)
    return grp, win


def _pool_select(grp, sums):
    out = sums[-1]
    for gi in range(len(sums) - 2, -1, -1):
        out = jnp.where(grp == gi, sums[gi], out)
    return out


def _pool_pooled(p):
    grp, win = _pool_lane_window(p.shape)
    inv_count = 1.0 / jnp.minimum(_iota2(p.shape, 0) + 1, win).astype(F32)
    sums, acc, k = [], p, 1
    for _ in POOL_WINDOWS:
        acc = acc + _shift_down(acc, k)
        sums.append(acc)
        k *= 2
    return _pool_select(grp, sums) * inv_count - p, grp, inv_count


def pool_forward(proj, wbd, pb, ps, y_all, name, after=None):
    s = proj.shape[0]
    specs, ops = _after(after)

    def body(p_ref, w_ref, b_ref, s_ref, *rest):
        o_ref = rest[-1]
        pooled, _, _ = _pool_pooled(p_ref[...])
        mixed = _nn(pooled.astype(BF16), w_ref[...]) + b_ref[...]
        o_ref[...] = (mixed * s_ref[...]).astype(BF16)

    vec = pl.BlockSpec((1, POOL_WIDTH), lambda j: (0, 0))
    return pl.pallas_call(
        body, name=name, grid=(1,),
        in_specs=[pl.BlockSpec((s, POOL_WIDTH), lambda j: (0, OFF_P // POOL_WIDTH)),
                  pl.BlockSpec((POOL_WIDTH, POOL_WIDTH), lambda j: (0, 0)), vec, vec, ANY] + specs,
        out_specs=pl.BlockSpec((s, POOL_WIDTH), lambda j: (0, (SSD_WIDTH + SB_WIDTH) // POOL_WIDTH)),
        out_shape=jax.ShapeDtypeStruct(y_all.shape, BF16),
        input_output_aliases={4: 0},
        compiler_params=_params(("arbitrary",)),
    )(proj, wbd, pb, ps, y_all, *ops)


def pool_backward(proj, dyall, wbd, pb, ps, dproj, name):
    s = proj.shape[0]

    def body(p_ref, dy_ref, w_ref, b_ref, s_ref, _, dp_ref, dw_ref, db_ref, ds_ref):
        pooled, grp, inv_count = _pool_pooled(p_ref[...])
        pooled_b = pooled.astype(BF16)
        mixed = _nn(pooled_b, w_ref[...]) + b_ref[...]
        dy = dy_ref[...]
        ds_ref[...] = jnp.sum(dy * mixed, axis=0, keepdims=True)
        dmixed = dy * s_ref[...]
        db_ref[...] = jnp.sum(dmixed, axis=0, keepdims=True)
        dmixed_b = dmixed.astype(BF16)
        dw_ref[...] = _tn(pooled_b, dmixed_b)
        dpooled = _nt(dmixed_b, w_ref[...])
        sums, acc, k = [], dpooled * inv_count, 1
        for _ in POOL_WINDOWS:
            acc = acc + _shift_up(acc, k)
            sums.append(acc)
            k *= 2
        dp_ref[...] = (_pool_select(grp, sums) - dpooled).astype(BF16)

    vec = pl.BlockSpec((1, POOL_WIDTH), lambda j: (0, 0))
    mat = pl.BlockSpec((POOL_WIDTH, POOL_WIDTH), lambda j: (0, 0))
    pcol = pl.BlockSpec((s, POOL_WIDTH), lambda j: (0, OFF_P // POOL_WIDTH))
    return pl.pallas_call(
        body, name=name, grid=(1,),
        in_specs=[pcol, pl.BlockSpec((s, POOL_WIDTH), lambda j: (0, (SSD_WIDTH + SB_WIDTH) // POOL_WIDTH)), mat, vec, vec, ANY],
        out_specs=[pcol, mat, vec, vec],
        out_shape=[jax.ShapeDtypeStruct(dproj.shape, BF16), jax.ShapeDtypeStruct((POOL_WIDTH, POOL_WIDTH), F32),
                   jax.ShapeDtypeStruct((1, POOL_WIDTH), F32), jax.ShapeDtypeStruct((1, POOL_WIDTH), F32)],
        input_output_aliases={5: 0},
        compiler_params=_params(("arbitrary",)),
    )(proj, dyall, wbd, pb, ps, dproj)


N_PAIRS = SSD_HEADS // 2


def _ssd_common(xc, dtraw, dtb, alog):
    c = CHUNK
    dt = _softplus(dtraw + dtb)
    a = -jnp.exp(alog)
    ltri = (_iota2((c, c), 0) >= _iota2((c, c), 1)).astype(BF16)
    acum = _split_dot_left(ltri, dt * a)
    expand = (_iota2((c, SSD_WIDTH), 1) // HEAD_DIM == _iota2((c, SSD_WIDTH), 0)).astype(BF16)
    expand_wide = (_iota2((c, SSD_HEADS * c), 1) // c == _iota2((c, SSD_HEADS * c), 0)).astype(BF16)
    acum_x = _split_dot(acum, expand, 2)
    dt_x = _split_dot(dt, expand, 2)
    alast_x = acum_x[c - 1:c, :]
    return dict(dt=dt, a=a, acum=acum, acum_x=acum_x, dt_x=dt_x, ea_x=jnp.exp(acum_x),
                dte_x=jnp.exp(alast_x - acum_x), eal_x=jnp.exp(alast_x),
                acol=_split_dot(acum, expand_wide, 2), acum_t=acum.T,
                xs=xc[:, :SSD_WIDTH], causal=_iota2((c, c), 0) >= _iota2((c, c), 1),
                left=_iota2((c, c), 1) < HEAD_DIM)


def _ssd_group(xc, g):
    b = xc[:, SSD_WIDTH + D_STATE * g:SSD_WIDTH + D_STATE * (g + 1)]
    cm = xc[:, SSD_WIDTH + 2 * D_STATE + D_STATE * g:SSD_WIDTH + 2 * D_STATE + D_STATE * (g + 1)]
    return b, cm


def _ssd_decay(q, hh):
    col = q["acol"][:, CHUNK * hh:CHUNK * (hh + 1)]
    row = q["acum_t"][hh:hh + 1, :]
    return jnp.where(q["causal"], jnp.exp(jnp.minimum(col - row, 0.0)), 0.0)


def ssd_forward(proj, xc, dtb, alog, dskip_x, nw, name):
    s = xc.shape[0]
    nc = s // CHUNK

    def body(xc_ref, zdt_ref, dtb_ref, alog_ref, dsk_ref, nw_ref, y_ref, yc_ref, st_ref, state):
        @pl.when(pl.program_id(0) == 0)
        def _():
            state[...] = jnp.zeros_like(state)

        xcv = xc_ref[...]
        q = _ssd_common(xcv, zdt_ref[:, SSD_WIDTH:SSD_WIDTH + LANES], dtb_ref[...], alog_ref[...])
        x = q["xs"] * q["dt_x"]
        xb = x.astype(BF16)
        xd = (x * q["dte_x"]).astype(BF16)
        pieces = []
        for g in range(2):
            bg, cg = _ssd_group(xcv, g)
            bgb, cgb = bg.astype(BF16), cg.astype(BF16)
            cb = _nt(cgb, bgb)
            bgt = bg.T.astype(BF16)
            for pr in (2 * g, 2 * g + 1):
                sl = slice(CHUNK * pr, CHUNK * (pr + 1))
                st = state[pr]
                st_ref[0, pr] = st
                yp = _nn(cgb, st.astype(BF16)) * q["ea_x"][:, sl]
                for k, hh in enumerate((2 * pr, 2 * pr + 1)):
                    w = (cb * _ssd_decay(q, hh)).astype(BF16)
                    mask = q["left"] if k == 0 else jnp.logical_not(q["left"])
                    yp = yp + _nn(w, jnp.where(mask, xb[:, sl], jnp.zeros_like(xb[:, sl])))
                state[pr] = st * q["eal_x"][:, sl] + _nn(bgt, xd[:, sl])
                pieces.append(yp)
        y = jnp.concatenate(pieces, axis=1) + q["xs"] * dsk_ref[...]
        yc_ref[...] = y
        zv = zdt_ref[:, :SSD_WIDTH]
        yg = y * (zv * _sigmoid(zv))
        r = lax.rsqrt(jnp.mean(yg * yg, axis=-1, keepdims=True) + EPS)
        y_ref[...] = (yg * r * nw_ref[...]).astype(BF16)

    vec = lambda n: pl.BlockSpec((1, n), lambda c: (0, 0))
    return pl.pallas_call(
        body, name=name, grid=(nc,),
        in_specs=[pl.BlockSpec((CHUNK, CONV_DIM), lambda c: (c, 0)),
                  pl.BlockSpec((CHUNK, ZDT), lambda c: (c, OFF_Z // ZDT)),
                  vec(LANES), vec(LANES), vec(SSD_WIDTH), vec(SSD_WIDTH)],
        out_specs=[pl.BlockSpec((CHUNK, SSD_WIDTH), lambda c: (c, 0)), pl.BlockSpec((CHUNK, SSD_WIDTH), lambda c: (c, 0)),
                   pl.BlockSpec((1, N_PAIRS, D_STATE, CHUNK), lambda c: (c, 0, 0, 0))],
        out_shape=[jax.ShapeDtypeStruct((s, D_MODEL), BF16), jax.ShapeDtypeStruct((s, SSD_WIDTH), F32),
                   jax.ShapeDtypeStruct((nc, N_PAIRS, D_STATE, CHUNK), F32)],
        scratch_shapes=[pltpu.VMEM((N_PAIRS, D_STATE, CHUNK), F32)],
        compiler_params=_params(("arbitrary",)),
    )(xc, proj, dtb, alog, dskip_x, nw)


def ssd_backward(proj, xc, ycore, dyall, states, dtb, alog, dskip_x, nw, name):
    s = xc.shape[0]
    nc = s // CHUNK
    c = CHUNK

    def body(xc_ref, zdt_ref, yc_ref, dy_ref, st_ref, dtb_ref, alog_ref, dsk_ref, nw_ref,
             dxc_ref, dzdt_ref, dnw_ref, ddsk_ref, ddtb_ref, dalog_ref, dstate):
        @pl.when(pl.program_id(0) == 0)
        def _():
            dstate[...] = jnp.zeros_like(dstate)
            dnw_ref[...] = jnp.zeros_like(dnw_ref)
            ddsk_ref[...] = jnp.zeros_like(ddsk_ref)
            ddtb_ref[...] = jnp.zeros_like(ddtb_ref)
            dalog_ref[...] = jnp.zeros_like(dalog_ref)

        xcv = xc_ref[...]
        dtraw = zdt_ref[:, SSD_WIDTH:SSD_WIDTH + LANES]
        q = _ssd_common(xcv, dtraw, dtb_ref[...], alog_ref[...])
        xs = q["xs"]
        x = xs * q["dt_x"]
        zv, yc, dy, nwv = zdt_ref[:, :SSD_WIDTH], yc_ref[...], dy_ref[...], nw_ref[...]
        sgz = _sigmoid(zv)
        siluz = zv * sgz
        yg = yc * siluz
        r = lax.rsqrt(jnp.mean(yg * yg, axis=-1, keepdims=True) + EPS)
        dnw_ref[...] += jnp.sum(dy * yg * r, axis=0, keepdims=True)
        g1 = dy * nwv
        dyg = r * (g1 - yg * (r * r) * jnp.mean(g1 * yg, axis=-1, keepdims=True))
        dyv = dyg * siluz
        dz = (dyg * yc * (sgz * (1.0 + zv * (1.0 - sgz)))).astype(BF16)
        ddsk_ref[...] += jnp.sum(dyv * xs, axis=0, keepdims=True)
        dye = dyv * q["ea_x"]
        dx_parts, yoff_parts, u_parts, v_parts, e_parts = [], [], [], [], []
        db_parts, dc_parts = [], []
        for g in range(2):
            bg, cg = _ssd_group(xcv, g)
            bgb, cgb = bg.astype(BF16), cg.astype(BF16)
            cb = _nt(cgb, bgb)
            cgt = cg.T.astype(BF16)
            dgsum = jnp.zeros((c, c), F32)
            dbg = jnp.zeros((c, D_STATE), F32)
            dcg = jnp.zeros((c, D_STATE), F32)
            for pr in (2 * g, 2 * g + 1):
                sl = slice(c * pr, c * (pr + 1))
                st = st_ref[0, pr]
                dst = dstate[pr]
                stb, dstb = st.astype(BF16), dst.astype(BF16)
                xp = x[:, sl]
                xpb = xp.astype(BF16)
                dyp = dyv[:, sl]
                xdp = xp * q["dte_x"][:, sl]
                yoff_parts.append(_nn(cgb, stb) * q["ea_x"][:, sl])
                rr = _nn(bgb, dstb)
                dxp = rr * q["dte_x"][:, sl]
                u_parts.append(rr * xdp)
                v_parts.append(dst * st * q["eal_x"][:, sl])
                for k, hh in enumerate((2 * pr, 2 * pr + 1)):
                    decay = _ssd_decay(q, hh)
                    w = cb * decay
                    mask = q["left"] if k == 0 else jnp.logical_not(q["left"])
                    dym = jnp.where(mask, dyp, 0.0).astype(BF16)
                    dw = _nt(dym, xpb)
                    dgsum = dgsum + dw * decay
                    e_parts.append(dw * w)
                    dxp = dxp + _nn(w.T.astype(BF16), dym)
                dyeb = dye[:, sl].astype(BF16)
                dcg = dcg + _nt(dyeb, stb)
                dbg = dbg + _nt(xdp.astype(BF16), dstb)
                dstate[pr] = dst * q["eal_x"][:, sl] + _nn(cgt, dyeb)
                dx_parts.append(dxp)
            dcg = dcg + _nn(dgsum.astype(BF16), bgb)
            dbg = dbg + _nn(dgsum.T.astype(BF16), cgb)
            db_parts.append(dbg)
            dc_parts.append(dcg)
        dx = jnp.concatenate(dx_parts, axis=1)
        yoff = jnp.concatenate(yoff_parts, axis=1)
        u = jnp.concatenate(u_parts, axis=1)
        v = jnp.concatenate(v_parts, axis=1)
        reduce_heads = (_iota2((SSD_WIDTH, c), 0) // HEAD_DIM == _iota2((SSD_WIDTH, c), 1)).astype(BF16)
        to_head = (_iota2((SSD_HEADS * c, c), 0) // c == _iota2((SSD_HEADS * c, c), 1)).astype(BF16)
        da = _split_dot(dyv * yoff - u, reduce_heads, 2)
        da = da + _split_dot(jnp.concatenate(e_parts, axis=1), to_head, 2)
        da = da - _split_dot(jnp.concatenate(e_parts, axis=0), to_head, 2, dot=_tn)
        dalast = jnp.sum(_split_dot(u + v, reduce_heads, 2), axis=0, keepdims=True)
        da = da + jnp.where(_iota2((c, c), 0) == c - 1, dalast, 0.0)
        utri = (_iota2((c, c), 1) >= _iota2((c, c), 0)).astype(BF16)
        dda = _split_dot_left(utri, da)
        ddt = dda * q["a"] + _split_dot(dx * xs, reduce_heads, 2)
        dalog_ref[...] += jnp.sum(dda * q["dt"], axis=0, keepdims=True) * q["a"]
        ddtraw = jnp.where(_iota2((c, c), 1) < SSD_HEADS, ddt * _sigmoid(dtraw + dtb_ref[...]), 0.0)
        ddtb_ref[...] += jnp.sum(ddtraw, axis=0, keepdims=True)
        dzdt_ref[...] = jnp.concatenate([dz, ddtraw.astype(BF16), jnp.zeros((c, ZDT - SSD_WIDTH - LANES), BF16)], axis=1)
        dxs = dx * q["dt_x"] + dyv * dsk_ref[...]
        dxc_ref[...] = jnp.concatenate([dxs] + db_parts + dc_parts, axis=1)

    rev = lambda i: nc - 1 - i
    vec = lambda n: pl.BlockSpec((1, n), lambda i: (0, 0))
    wide = pl.BlockSpec((c, SSD_WIDTH), lambda i: (rev(i), 0))
    zdt = pl.BlockSpec((c, ZDT), lambda i: (rev(i), OFF_Z // ZDT))
    return pl.pallas_call(
        body, name=name, grid=(nc,),
        in_specs=[pl.BlockSpec((c, CONV_DIM), lambda i: (rev(i), 0)), zdt, wide, wide,
                  pl.BlockSpec((1, N_PAIRS, D_STATE, c), lambda i: (rev(i), 0, 0, 0)),
                  vec(LANES), vec(LANES), vec(SSD_WIDTH), vec(SSD_WIDTH)],
        out_specs=[pl.BlockSpec((c, CONV_DIM), lambda i: (rev(i), 0)), zdt,
                   vec(SSD_WIDTH), vec(SSD_WIDTH), vec(LANES), vec(LANES)],
        out_shape=[jax.ShapeDtypeStruct((s, CONV_DIM), F32), jax.ShapeDtypeStruct((s, D_INP), BF16),
                   jax.ShapeDtypeStruct((1, SSD_WIDTH), F32),
                   jax.ShapeDtypeStruct((1, SSD_WIDTH), F32), jax.ShapeDtypeStruct((1, LANES), F32),
                   jax.ShapeDtypeStruct((1, LANES), F32)],
        scratch_shapes=[pltpu.VMEM((N_PAIRS, D_STATE, c), F32)],
        compiler_params=_params(("arbitrary",)),
    )(xc, proj, ycore, dyall, states, dtb, alog, dskip_x, nw)


SB_Q, SB_K = 512, 512
SB_T = 256
SB_SCALE = HEAD_DIM ** -0.5


def _key_suffix(x, tri, terms):
    runs = [x[:, SB_T * k:SB_T * (k + 1)] for k in range(SB_K // SB_T)]
    sums = [_split_dot(r, tri, terms) for r in runs]
    later = None
    for k in range(len(runs) - 1, -1, -1):
        if later is not None:
            sums[k] = sums[k] + later
        total = jnp.sum(runs[k], axis=1, keepdims=True)
        later = total if later is None else later + total
    return jnp.concatenate(sums, axis=1), later


def _sb_weights(qm, kb, diagonal, run_lk, strict_after):
    z = _nt(qm, kb)
    nz = -z
    tail = jnp.log(1.0 + jnp.exp(jnp.minimum(z, nz)))
    ls = jnp.minimum(z, 0.0) - tail
    lk = jnp.minimum(nz, 0.0) - tail
    if diagonal is not None:
        valid = _iota2(z.shape, 1) < _iota2(z.shape, 0) + diagonal
        lk = jnp.where(valid, lk, 0.0)
    after, total = _key_suffix(lk, strict_after, 1)
    w = jnp.exp(ls + after + run_lk)
    if diagonal is not None:
        w = jnp.where(valid, w, 0.0)
    return ls, total, w


def _sb_sweep(i, block, init):
    own = (i * SB_Q) // SB_K
    first = block(own, init, i * SB_Q - own * SB_K)
    return lax.fori_loop(1, own + 1, lambda jj, carry: block(own - jj, carry, None), first)


def sb_forward(proj, y_all, name, after=None):
    s = proj.shape[0]
    t, tk = SB_Q, SB_K
    nq = s // t
    specs, ops = _after(after)

    def body(q_ref, k_ref, v_ref, *rest):
        y_ref, o_ref = rest[-2:]
        i = pl.program_id(1)
        left = _iota2((t, LANES), 1) < HEAD_DIM
        left_k = _iota2((tk, LANES), 1) < HEAD_DIM
        qv = q_ref[...] * SB_SCALE
        zero = jnp.zeros_like(qv)
        qms = (jnp.where(left, qv, zero).astype(BF16), jnp.where(left, zero, qv).astype(BF16))
        strict_after = (_iota2((SB_T, SB_T), 0) > _iota2((SB_T, SB_T), 1)).astype(BF16)

        def block(j, carry, diagonal):
            o, runs = carry[0], carry[1:]
            rows = pl.ds(pl.multiple_of(j * tk, tk), tk)
            kb = k_ref[rows, :].astype(BF16)
            vv = v_ref[rows, :]
            new_runs = []
            for k in range(2):
                _, total, w = _sb_weights(qms[k], kb, diagonal, runs[k], strict_after)
                vm = jnp.where(left_k if k == 0 else jnp.logical_not(left_k), vv, 0.0).astype(BF16)
                o = o + _nn(w.astype(BF16), vm)
                new_runs.append(runs[k] + total)
            return (o, *new_runs)

        init = (jnp.zeros((t, LANES), F32), jnp.zeros((t, 1), F32), jnp.zeros((t, 1), F32))
        o = _sb_sweep(i, block, init)[0]
        o_ref[...] = o
        y_ref[...] = o.astype(BF16)

    return pl.pallas_call(
        body, name=name, grid=(2, nq),
        in_specs=[pl.BlockSpec((t, LANES), lambda p, i: (i, 3 * p)),
                  pl.BlockSpec((s, LANES), lambda p, i: (0, 3 * p + 1)),
                  pl.BlockSpec((s, LANES), lambda p, i: (0, 3 * p + 2)), ANY] + specs,
        out_specs=[pl.BlockSpec((t, LANES), lambda p, i: (i, SSD_WIDTH // LANES + p)),
                   pl.BlockSpec((t, LANES), lambda p, i: (i, p))],
        out_shape=[jax.ShapeDtypeStruct(y_all.shape, BF16), jax.ShapeDtypeStruct((s, SB_WIDTH), F32)],
        input_output_aliases={3: 0},
        compiler_params=_params(("parallel", "arbitrary")),
    )(proj, proj, proj, y_all, *ops)


def sb_backward(proj, o, dyall, dproj, name, after=None):
    s = proj.shape[0]
    t, tk = SB_Q, SB_K
    nq = s // t
    specs, ops = _after(after)

    def body(q_ref, k_ref, v_ref, o_ref, do_ref, *rest):
        dqkv_ref, dk_acc, dv_acc = rest[-3:]
        dk_acc[...] = jnp.zeros_like(dk_acc)
        dv_acc[...] = jnp.zeros_like(dv_acc)
        left = _iota2((t, LANES), 1) < HEAD_DIM
        lane_masks = (left, jnp.logical_not(left))
        left_k = _iota2((tk, LANES), 1) < HEAD_DIM
        key_masks = (left_k, jnp.logical_not(left_k))
        strict_after = (_iota2((SB_T, SB_T), 0) > _iota2((SB_T, SB_T), 1)).astype(BF16)
        from_here = (_iota2((SB_T, SB_T), 0) >= _iota2((SB_T, SB_T), 1)).astype(BF16)

        def query_block(i, _):
            qrows = pl.ds(pl.multiple_of(i * t, t), t)
            qv = q_ref[qrows, :] * SB_SCALE
            dov = do_ref[qrows, :]
            zero = jnp.zeros_like(qv)
            qb = qv.astype(BF16)
            dob = dov.astype(BF16)
            prod = dob.astype(F32) * o_ref[qrows, :]
            qms = [jnp.where(m, qv, zero).astype(BF16) for m in lane_masks]
            doms = [jnp.where(m, dov, zero).astype(BF16) for m in lane_masks]
            deltas = [jnp.sum(jnp.where(m, prod, zero), axis=1, keepdims=True) for m in lane_masks]

            def block(j, carry, diagonal):
                dq = carry[0]
                run_lk, run_e = carry[1:3], carry[3:5]
                rows = pl.ds(pl.multiple_of(j * tk, tk), tk)
                kb = k_ref[rows, :].astype(BF16)
                vb = v_ref[rows, :].astype(BF16)
                dkj = jnp.zeros((tk, LANES), F32)
                dvj = jnp.zeros((tk, LANES), F32)
                new_lk, new_e = [], []
                for k in range(2):
                    ls, total, w = _sb_weights(qms[k], kb, diagonal, run_lk[k], strict_after)
                    wb = w.astype(BF16)
                    e = _nt(doms[k], vb) * wb.astype(F32)
                    e_from_here, e_total = _key_suffix(e, from_here, 2)
                    before = deltas[k] - e_from_here - run_e[k]
                    dz = e - jnp.exp(ls) * (e + before)
                    if diagonal is not None:
                        dz = jnp.where(_iota2(dz.shape, 1) < _iota2(dz.shape, 0) + diagonal, dz, 0.0)
                    dz = dz.astype(BF16)
                    m = lane_masks[k]
                    dvj = dvj + jnp.where(key_masks[k], _tn(wb, dob), 0.0)
                    dkj = dkj + jnp.where(key_masks[k], _tn(dz, qb), 0.0)
                    dq = dq + jnp.where(m, _nn(dz, kb), 0.0)
                    new_lk.append(run_lk[k] + total)
                    new_e.append(run_e[k] + e_total)
                dk_acc[rows, :] += dkj
                dv_acc[rows, :] += dvj
                return (dq, *new_lk, *new_e)

            col = jnp.zeros((t, 1), F32)
            dq = _sb_sweep(i, block, (jnp.zeros((t, LANES), F32), col, col, col, col))[0]
            dqkv_ref[qrows, 0:LANES] = (dq * SB_SCALE).astype(BF16)
            return 0

        lax.fori_loop(0, nq, query_block, 0)
        dqkv_ref[:, LANES:2 * LANES] = dk_acc[...].astype(BF16)
        dqkv_ref[:, 2 * LANES:3 * LANES] = dv_acc[...].astype(BF16)

    col = lambda f: pl.BlockSpec((s, LANES), f)
    return pl.pallas_call(
        body, name=name, grid=(2,),
        in_specs=[col(lambda p: (0, 3 * p)), col(lambda p: (0, 3 * p + 1)), col(lambda p: (0, 3 * p + 2)),
                  col(lambda p: (0, p)), col(lambda p: (0, SSD_WIDTH // LANES + p)), ANY] + specs,
        out_specs=pl.BlockSpec((s, 3 * LANES), lambda p: (0, p)),
        out_shape=jax.ShapeDtypeStruct(dproj.shape, BF16),
        input_output_aliases={5: 0},
        scratch_shapes=[pltpu.VMEM((s, LANES), F32), pltpu.VMEM((s, LANES), F32)],
        compiler_params=_params(("parallel",)),
    )(proj, proj, proj, o, dyall, dproj, *ops)


def adamw(w, g, m, v, name):
    b, r, c = w.shape
    tr = max([t for t in range(8, min(r, 512) + 1, 8) if r % t == 0], default=r)

    def body(w_ref, g_ref, m_ref, v_ref, d_ref, nm_ref, nv_ref):
        gv = g_ref[...]
        nm = ADAM_B1 * m_ref[...] + (1.0 - ADAM_B1) * gv
        nv = ADAM_B2 * v_ref[...] + (1.0 - ADAM_B2) * (gv * gv)
        m_hat = nm / (1.0 - ADAM_B1 ** ADAM_STEP)
        v_hat = nv / (1.0 - ADAM_B2 ** ADAM_STEP)
        d_ref[...] = -ADAM_LR * (m_hat / (jnp.sqrt(v_hat) + ADAM_EPS) + ADAM_WD * w_ref[...])
        nm_ref[...] = nm
        nv_ref[...] = nv

    blk = pl.BlockSpec((1, tr, c), lambda i, j: (i, j, 0))
    return pl.pallas_call(
        body, name=name, grid=(b, r // tr),
        in_specs=[blk] * 4, out_specs=[blk] * 3,
        out_shape=[jax.ShapeDtypeStruct(w.shape, F32)] * 3,
        compiler_params=_params(("parallel", "parallel")),
    )(w, g, m, v)


def _position():
    return lax.axis_index("x"), lax.axis_index("y"), lax.axis_index("c")


def _flipped(pos, flip):
    return tuple((1 - p) if f else p for p, f in zip(pos, flip))


FLIP_C = (0, 0, 1)
CHIP_FLIPS = {1: (0, 1, 0), 2: (1, 0, 0), 3: (1, 1, 0)}
SHARD_ROWS = (SHARD_IN, SHARD_OUT, SHARD_FF, SHARD_FF, SHARD_FF)


def _rows(start, size):
    return pl.ds(pl.multiple_of(start, 16), size)


HBM = pl.BlockSpec(memory_space=pltpu.HBM)
SEM = pl.BlockSpec(memory_space=pltpu.SEMAPHORE)
EFFECT = pltpu.SideEffectType.DATAFLOW_SIDE_EFFECTING


def _in_hbm(a):
    return pltpu.with_memory_space_constraint(a, pltpu.HBM)


def _landing(shape, dtype):
    return _in_hbm(lax.empty(shape, dtype))


def _copies(plan, pos, src_refs, land_refs, send_sems, recv_sems):
    return [pltpu.make_async_remote_copy(src_ref=src, dst_ref=dst, send_sem=send_sems.at[k], recv_sem=recv_sems.at[k],
                                         device_id=_flipped(pos, flip), device_id_type=MESH)
            for k, (src, dst, flip) in enumerate(plan(pos, src_refs, land_refs))]


def exchange_start(name, srcs, lands, n, plan, after=None):
    ns, nl = len(srcs), len(lands)
    specs, ops = _after(after)

    def body(*refs):
        src_refs, land_refs = refs[:ns], refs[ns:ns + nl]
        send_sems, recv_sems, token = refs[ns + nl + len(ops)], refs[ns + nl + len(ops) + 1], refs[-1]
        for cp in _copies(plan, _position(), src_refs, land_refs, send_sems, recv_sems):
            cp.start()
        token[...] = jnp.zeros_like(token)

    thru = [pltpu.HBM(a.shape, a.dtype) for a in list(srcs) + list(lands)]
    out = pl.pallas_call(
        body, name=name,
        out_shape=(pltpu.SemaphoreType.DMA((n,)), pltpu.SemaphoreType.DMA((n,)), *thru, jax.ShapeDtypeStruct((8, LANES), F32)),
        in_specs=[HBM] * (ns + nl) + specs,
        out_specs=(SEM, SEM, *([HBM] * (ns + nl)), pl.BlockSpec(memory_space=pltpu.VMEM)),
        input_output_aliases={k: 2 + k for k in range(ns + nl)},
        compiler_params=pltpu.CompilerParams(has_side_effects=EFFECT),
    )(*[_in_hbm(a) for a in srcs], *lands, *ops)
    return out[0], out[1], list(out[2:2 + ns]), list(out[2 + ns:2 + ns + nl]), out[-1]


def exchange_wait(name, started, after, plan):
    send_sems, recv_sems, srcs, lands, _ = started
    ns, nl = len(srcs), len(lands)
    specs, ops = _after(after)

    def body(*refs):
        src_refs, land_refs = refs[:ns], refs[ns:ns + nl]
        send_sems, recv_sems = refs[ns + nl], refs[ns + nl + 1]
        for cp in _copies(plan, _position(), src_refs, land_refs, send_sems, recv_sems):
            cp.wait_send()
            cp.wait_recv()

    out = pl.pallas_call(
        body, name=name,
        out_shape=tuple(pltpu.HBM(a.shape, a.dtype) for a in list(srcs) + list(lands)),
        in_specs=[HBM] * (ns + nl) + [SEM, SEM] + specs,
        out_specs=tuple([HBM] * (ns + nl)),
        input_output_aliases={k: k for k in range(ns + nl)},
        compiler_params=pltpu.CompilerParams(has_side_effects=EFFECT),
    )(*srcs, *lands, send_sems, recv_sems, *ops)
    return list(out[:ns]), list(out[ns:])


def _gather_ici_plan(pos, srcs, lands):
    chip, c = 2 * pos[0] + pos[1], pos[2]
    copies = []
    for src, dst in zip(srcs, lands):
        r = src.shape[0]
        h = r // 2
        for f in (1, 2, 3):
            copies.append((src.at[_rows(c * h, h)], dst.at[_rows(chip * r + c * h, h)], CHIP_FLIPS[f]))
    return copies


def _gather_d2d_plan(pos, srcs, lands):
    chip, c = 2 * pos[0] + pos[1], pos[2]
    copies = []
    for own, dst in zip(srcs, lands):
        r = own.shape[0]
        h = r // 2
        copies.append((own, dst.at[_rows(chip * r, r)], FLIP_C))
        for f in (1, 2, 3):
            at = _rows(lax.bitwise_xor(chip, f) * r + c * h, h)
            copies.append((dst.at[at], dst.at[at], FLIP_C))
    return copies


def gather_ici_start(shards, after=None):
    lands = [_landing((N_CHIPS * a.shape[0], D_MODEL), BF16) for a in shards]
    return exchange_start("gather_ici_start", shards, lands, 3 * len(shards), _gather_ici_plan, after=after)


def gather_d2d_start(shards, fulls, after=None):
    return exchange_start("gather_d2d_start", shards, fulls, 4 * len(shards), _gather_d2d_plan, after=after)


def _reduce_d2d_plan(pos, srcs, lands):
    c = pos[2]
    return [(src.at[:, _rows((1 - c) * (src.shape[1] // 2), src.shape[1] // 2)], dst, FLIP_C) for src, dst in zip(srcs, lands)]


def _reduce_ici_plan(pos, srcs, lands):
    chip = 2 * pos[0] + pos[1]
    return [(src.at[lax.bitwise_xor(chip, f)], dst.at[f - 1], CHIP_FLIPS[f]) for src, dst in zip(srcs, lands) for f in (1, 2, 3)]


def _reduce_swap_plan(pos, srcs, lands):
    c = pos[2]
    copies = []
    for dst in lands:
        h = dst.shape[0] // 2
        at = _rows(c * h, h)
        copies.append((dst.at[at], dst.at[at], FLIP_C))
    return copies


def reduce_d2d_start(grads):
    lands = [_landing((N_CHIPS, g.shape[1] // 2, D_MODEL), BF16) for g in grads]
    return exchange_start("reduce_d2d_start", grads, lands, len(grads), _reduce_d2d_plan)


def reduce_ici_start(chip_sums):
    lands = [_landing((N_CHIPS - 1,) + p.shape[1:], BF16) for p in chip_sums]
    return exchange_start("reduce_ici_start", chip_sums, lands, 3 * len(chip_sums), _reduce_ici_plan)


def reduce_swap_start(mine):
    return exchange_start("reduce_swap_start", [], mine, len(mine), _reduce_swap_plan)


def _by_shape(fn, *lists):
    groups, out = {}, [None] * len(lists[0])
    for k, a in enumerate(lists[0]):
        groups.setdefault(a.shape, []).append(k)
    for idx in groups.values():
        for k, r in zip(idx, fn(*[[l[k] for k in idx] for l in lists])):
            out[k] = r
    return out


def add_halves(ds, recvs, half, name):
    n = len(ds)
    nch, r, c = ds[0].shape
    h = r // 2

    def body(half_ref, *refs):
        for k in range(n):
            refs[2 * n + k][...] = (refs[k][...].astype(F32) + refs[n + k][...].astype(F32)).astype(BF16)

    mine = pl.BlockSpec((1, h, c), lambda j, hf: (j, hf[0], 0))
    whole = pl.BlockSpec((1, h, c), lambda j, hf: (j, 0, 0))
    return pl.pallas_call(
        body, name=name,
        grid_spec=pltpu.PrefetchScalarGridSpec(
            num_scalar_prefetch=1, grid=(nch,), in_specs=[mine] * n + [whole] * n, out_specs=[whole] * n),
        out_shape=[jax.ShapeDtypeStruct(rv.shape, BF16) for rv in recvs],
        compiler_params=_params(("parallel",)),
    )(half, *ds, *recvs)


def add_chips(ps, recvs, chip, name):
    n = len(ps)
    _, r, c = ps[0].shape

    def body(chip_ref, *refs):
        for k in range(n):
            acc = refs[k][0].astype(F32)
            for f in range(N_CHIPS - 1):
                acc = acc + refs[n + k][f].astype(F32)
            refs[2 * n + k][...] = acc

    return pl.pallas_call(
        body, name=name,
        grid_spec=pltpu.PrefetchScalarGridSpec(
            num_scalar_prefetch=1, grid=(1,),
            in_specs=[pl.BlockSpec((1, r, c), lambda i, ch: (ch[0], 0, 0))] * n +
                     [pl.BlockSpec((N_CHIPS - 1, r, c), lambda i, ch: (0, 0, 0))] * n,
            out_specs=[pl.BlockSpec((r, c), lambda i, ch: (ch[1], 0))] * n),
        out_shape=[jax.ShapeDtypeStruct((2 * r, c), F32)] * n,
        compiler_params=_params(("arbitrary",)),
    )(chip, *ps, *recvs)


def adamw_layers(w, gs, m, v, name):
    b, r, c = w.shape
    tr = max([t for t in range(8, min(r, 512) + 1, 8) if r % t == 0], default=r)

    def body(w_ref, m_ref, v_ref, *rest):
        g_refs, (g_ref, d_ref, nm_ref, nv_ref) = rest[:b], rest[b:]
        layer = pl.program_id(0)
        gv = g_refs[0][...]
        for l in range(1, b):
            gv = jnp.where(layer == l, g_refs[l][...], gv)
        nm = ADAM_B1 * m_ref[0] + (1.0 - ADAM_B1) * gv
        nv = ADAM_B2 * v_ref[0] + (1.0 - ADAM_B2) * (gv * gv)
        m_hat = nm / (1.0 - ADAM_B1 ** ADAM_STEP)
        v_hat = nv / (1.0 - ADAM_B2 ** ADAM_STEP)
        g_ref[0] = gv
        d_ref[0] = -ADAM_LR * (m_hat / (jnp.sqrt(v_hat) + ADAM_EPS) + ADAM_WD * w_ref[0])
        nm_ref[0] = nm
        nv_ref[0] = nv

    nr, tc = r // tr, (c if tr < r else _tile(c, 256))
    steps = nr * (c // tc)
    blk = pl.BlockSpec((1, tr, tc), lambda i, j: (i, j % nr, j // nr))
    g_specs = [pl.BlockSpec((tr, tc), lambda i, j, l=l: (jnp.where(i == l, j % nr, jnp.where(i < l, 0, nr - 1)),
                                                         jnp.where(i == l, j // nr, jnp.where(i < l, 0, c // tc - 1))))
               for l in range(b)]
    return pl.pallas_call(
        body, name=name, grid=(b, steps),
        in_specs=[blk] * 3 + g_specs, out_specs=[blk] * 4,
        out_shape=[jax.ShapeDtypeStruct(w.shape, F32)] * 4,
        compiler_params=_params(("arbitrary", "arbitrary")),
    )(w, m, v, *gs)


def small_allreduce(v, name, after=None):
    r, c = v.shape
    specs, ops = _after(after)

    def body(v_ref, *rest):
        o_ref, buf, send_sems, recv_sems = rest[len(ops):]
        pos = _position()
        me = 4 * pos[0] + 2 * pos[1] + pos[2]
        buf[0] = v_ref[...]
        copies = []
        for f in range(1, 8):
            flip = ((f >> 2) & 1, (f >> 1) & 1, f & 1)
            cp = pltpu.make_async_remote_copy(
                src_ref=v_ref, dst_ref=buf.at[f], send_sem=send_sems.at[f - 1], recv_sem=recv_sems.at[f - 1],
                device_id=_flipped(pos, flip), device_id_type=MESH)
            cp.start()
            copies.append(cp)
        for cp in copies:
            cp.wait()
        acc = buf[me]
        for d in range(1, 8):
            acc = acc + buf[lax.bitwise_xor(me, d)]
        o_ref[...] = acc

    return pl.pallas_call(
        body, name=name,
        in_specs=[pl.BlockSpec(memory_space=pltpu.VMEM)] + specs, out_specs=pl.BlockSpec(memory_space=pltpu.VMEM),
        out_shape=jax.ShapeDtypeStruct((r, c), F32),
        scratch_shapes=[pltpu.VMEM((8, r, c), F32), pltpu.SemaphoreType.DMA((7,)), pltpu.SemaphoreType.DMA((7,))],
    )(v, *ops)


def _all_devices_plan(pos, srcs, lands):
    return [(srcs[0], lands[0].at[f], ((f >> 2) & 1, (f >> 1) & 1, f & 1)) for f in range(1, 8)]


def sum_devices(v, gathered, me, name):
    r, c = v.shape

    def body(me_ref, v_ref, g_ref, o_ref):
        own = v_ref[...]
        acc = None
        for d in range(8):
            slot = lax.bitwise_xor(me_ref[0], d)
            term = jnp.where(slot == 0, own, g_ref[slot])
            acc = term if acc is None else acc + term
        o_ref[...] = acc

    return pl.pallas_call(
        body, name=name,
        grid_spec=pltpu.PrefetchScalarGridSpec(
            num_scalar_prefetch=1, grid=(1,),
            in_specs=[pl.BlockSpec((r, c), lambda i, m: (0, 0)), pl.BlockSpec((8, r, c), lambda i, m: (0, 0, 0))],
            out_specs=pl.BlockSpec((r, c), lambda i, m: (0, 0))),
        out_shape=jax.ShapeDtypeStruct((r, c), F32),
        compiler_params=_params(("arbitrary",)),
    )(me, v, gathered)


_IN_SEGMENTS = ((0, 1544, 128), (128, 1800, 128), (256, 2056, 128), (384, 1672, 128), (512, 1928, 128), (640, 2184, 128),
                (OFF_Z, 0, SSD_WIDTH), (OFF_DT, 1536, SSD_HEADS), (OFF_XBC, 512, CONV_DIM), (OFF_P, 2312, POOL_WIDTH))


def _in_column_map():
    m = np.full((D_INP,), -1, np.int64)
    for at, orig, n in _IN_SEGMENTS:
        cols = np.arange(orig, orig + n)
        m[at:at + n] = (cols // COLS_IN) * SHARD_IN + cols % COLS_IN
    return m


def take_rows(a, idx, name):
    dep, r_in, c = a.shape
    blk = 2 * LANES if len(idx) % (2 * LANES) == 0 and r_in % (2 * LANES) == 0 else LANES
    n_out, n_in = len(idx) // blk, r_in // blk
    assert len(idx) % blk == 0 and r_in % blk == 0
    sources = [sorted({int(v) // blk for v in idx[blk * i:blk * (i + 1)] if v >= 0}) for i in range(n_out)]
    width = max(len(s) for s in sources)
    table = np.zeros((n_out, width), np.int32)
    for i, s in enumerate(sources):
        spare = [b for b in range(n_in) if b not in s][:width - len(s)]
        table[i] = s + spare

    def body(tbl_ref, idx_ref, *refs):
        in_refs, o_ref = refs[:width], refs[width]
        i = pl.program_id(1)
        src = idx_ref[...]
        acc = jnp.zeros((blk, c), F32)
        for k in range(width):
            pick = (src == tbl_ref[i, k] * blk + _iota2((blk, blk), 1)).astype(BF16)
            acc = acc + _nn(pick, in_refs[k][0])
        o_ref[0] = acc.astype(BF16)

    return pl.pallas_call(
        body, name=name,
        grid_spec=pltpu.PrefetchScalarGridSpec(
            num_scalar_prefetch=1, grid=(dep, n_out),
            in_specs=[pl.BlockSpec((blk, 1), lambda l, i, t: (i, 0))] +
                     [pl.BlockSpec((1, blk, c), lambda l, i, t, k=k: (l, t[i, k], 0)) for k in range(width)],
            out_specs=pl.BlockSpec((1, blk, c), lambda l, i, t: (l, i, 0))),
        out_shape=jax.ShapeDtypeStruct((dep, len(idx), c), BF16),
        compiler_params=_params(("parallel", "parallel")),
    )(jnp.asarray(table), jnp.asarray(np.asarray(idx, np.int32).reshape(-1, 1)), *([a] * width))


def _in_weight_layout(staged):
    return take_rows(staged, _in_column_map(), "w_in_layout")


def _in_gradient_layout(dwt):
    fwd = _in_column_map()
    inv = np.full((N_CHIPS * SHARD_IN,), -1, np.int64)
    inv[fwd[fwd >= 0]] = np.nonzero(fwd >= 0)[0]
    return take_rows(dwt, inv, "dw_in_layout")


SMALL_NAMES = ("norm1_w", "conv_w", "conv_b", "dt_bias", "a_log", "d_skip", "ssd_norm_w", "pool_w", "pool_b",
               "pool_scale", "norm2_w", "final_norm_w")
SMALL_ROWS = 160


def _small_rows(shape):
    return -(-int(np.prod(shape)) // (8 * D_MODEL)) * 8


def _pack_small(parts):
    rows = []
    for p in parts:
        flat = p.reshape(-1)
        rows.append(jnp.pad(flat, (0, _small_rows(p.shape) * D_MODEL - flat.shape[0])).reshape(-1, D_MODEL))
    used = sum(r.shape[0] for r in rows)
    return jnp.concatenate(rows + [jnp.zeros((SMALL_ROWS - used, D_MODEL), F32)], axis=0)


def _unpack_small(packed, shapes):
    out, at = [], 0
    for shp in shapes:
        n, r = int(np.prod(shp)), _small_rows(shp)
        out.append(packed[at:at + r].reshape(-1)[:n].reshape(shp))
        at += r
    return out


def kernel(x, norm1_w, w_in, conv_w, conv_b, dt_bias, a_log, d_skip, ssd_norm_w, pool_w, pool_b, pool_scale, w_out, norm2_w, w_gate, w_up, w_down, final_norm_w, loss_target, m_norm1_w, m_w_in, m_conv_w, m_conv_b, m_dt_bias, m_a_log, m_d_skip, m_ssd_norm_w, m_pool_w, m_pool_b, m_pool_scale, m_w_out, m_norm2_w, m_w_gate, m_w_up, m_w_down, m_final_norm_w, v_norm1_w, v_w_in, v_conv_w, v_conv_b, v_dt_bias, v_a_log, v_d_skip, v_ssd_norm_w, v_pool_w, v_pool_b, v_pool_scale, v_w_out, v_norm2_w, v_w_gate, v_w_up, v_w_down, v_final_norm_w):
    px, py, pc = _position()
    chip = 2 * px + py
    chip_arr = jnp.reshape(chip, (1,)).astype(jnp.int32)
    half_arr = jnp.reshape(pc, (1,)).astype(jnp.int32)

    def layer_shards(l):
        w_in_t = jnp.pad(jnp.swapaxes(w_in[l], 0, 1).astype(BF16), ((0, SHARD_IN - COLS_IN), (0, 0)))
        return [w_in_t, w_out[l].astype(BF16), jnp.swapaxes(w_gate[l], 0, 1).astype(BF16),
                jnp.swapaxes(w_up[l], 0, 1).astype(BF16), w_down[l].astype(BF16)]

    shards0 = layer_shards(0)
    head = gather_ici_start(shards0[:1])
    over_ici = {}

    def pass_on(l, after):
        own, arrived = exchange_wait("gather_ici_wait", over_ici[l], after, _gather_ici_plan)
        swap = gather_d2d_start(own, arrived)
        tokens = [swap[4]]
        if l + 1 < DEPTH:
            over_ici[l + 1] = gather_ici_start(layer_shards(l + 1), after=swap[4])
            tokens.append(over_ici[l + 1][4])
        return swap, tokens

    def weights_of(swap, after):
        _, (w_in_st, w_out_l, w_gate_t, w_up_t, w_down_l) = exchange_wait("gather_d2d_wait", swap, after, _gather_d2d_plan)
        return _in_weight_layout(w_in_st[None])[0], w_out_l, w_gate_t, w_up_t, w_down_l

    pad_heads = lambda v: jnp.pad(v, ((0, 0), (0, LANES - SSD_HEADS)))[:, None, :]
    dtb, alog = pad_heads(dt_bias), pad_heads(a_log)
    dskip_x = jnp.repeat(d_skip, HEAD_DIM, axis=1)[:, None, :]
    eye = jnp.eye(len(POOL_WINDOWS), dtype=F32)
    wbd = (pool_w[:, :, :, None, :] * eye[None, :, None, :, None]).reshape(DEPTH, POOL_WIDTH, POOL_WIDTH).astype(BF16)
    pool_b2 = pool_b.reshape(DEPTH, 1, POOL_WIDTH)
    cw_cols = lax.dynamic_update_slice(jnp.zeros((DEPTH, CONV_WIDTH, CONV_DIM), F32), conv_w,
                                       (0, 0, chip * (CONV_DIM // N_CHIPS)))
    cw_cols = jnp.where(pc == 0, cw_cols, 0.0)
    cw_rows = (DEPTH * CONV_WIDTH * CONV_DIM) // D_MODEL
    conv_w_f = small_allreduce(jnp.pad(cw_cols.reshape(cw_rows, D_MODEL), ((0, 8), (0, 0))), "gather_conv_w", after=head[4])
    conv_w_f = conv_w_f[:cw_rows].reshape(DEPTH, CONV_WIDTH, CONV_DIM)
    cw8 = jnp.pad(conv_w_f, ((0, 0), (0, 8 - CONV_WIDTH), (0, 0)))

    h = x[0]
    saved, weights = [], []
    own, arrived = exchange_wait("gather_ici_wait", head, [head[4], conv_w_f] + shards0[1:], _gather_ici_plan)
    head = gather_d2d_start(own, arrived)
    over_ici[0] = gather_ici_start(shards0[1:], after=head[4])
    w_in_f = _in_weight_layout(exchange_wait("gather_d2d_wait", head, [head[4], over_ici[0][4]], _gather_d2d_plan)[1][0][None])[0]
    for l in range(DEPTH):
        if l > 0:
            w_in_f, w_out_f, w_gate_t, w_up_t, w_down_f = weights[l]
        proj = rms_matmul(h, norm1_w[l][None], w_in_f, "in_proj")
        xc = conv_forward(proj, cw8[l], conv_b[l][None], "conv_fwd")
        y_all, ycore, states = ssd_forward(proj, xc, dtb[l], alog[l], dskip_x[l], ssd_norm_w[l][None], "ssd_fwd")
        if l == 0:
            swap, tokens = pass_on(0, y_all)
            y_all, o_sb = sb_forward(proj, y_all, "sb_fwd", after=tokens)
            tokens = None
        else:
            y_all, o_sb = sb_forward(proj, y_all, "sb_fwd")
            swap, tokens = pass_on(l + 1, o_sb) if l + 1 < DEPTH else (None, None)
        y_all = pool_forward(proj, wbd[l], pool_b2[l], pool_scale[l][None], y_all, "pool_fwd", after=tokens)
        if l == 0:
            w_out_f, w_gate_t, w_up_t, w_down_f = exchange_wait("gather_d2d_wait", swap, y_all, _gather_d2d_plan)[1]
            weights.append((w_in_f, w_out_f, w_gate_t, w_up_t, w_down_f))
        x1 = matmul_residual(y_all, w_out_f, h, "out_proj")
        x2, g, u = ffn_forward(x1, norm2_w[l][None], w_gate_t, w_up_t, w_down_f, "ffn_fwd")
        if l == 0:
            swap, tokens = pass_on(1, x2)
            weights.append(weights_of(swap, tokens))
        elif swap is not None:
            weights.append(weights_of(swap, x2))
        saved.append((h, proj, xc, ycore, states, o_sb, y_all, x1, g, u))
        h = x2

    loss_part, dx, dxb, d_final = loss_head(h, final_norm_w[None], loss_target[0], "loss_head")
    loss = lax.psum(loss_part[0, 0], ("x", "y", "c"))

    small = {n: [None] * DEPTH for n in SMALL_NAMES if n != "final_norm_w"}
    chip_half = jnp.concatenate([chip_arr, half_arr])
    reduced = {}
    d2d = ici = early = None

    def add_cores(d2d, after):
        mine, theirs = exchange_wait("reduce_d2d_wait", d2d[1], after, _reduce_d2d_plan)
        return d2d[0], reduce_ici_start(_by_shape(lambda ds, ts: add_halves(ds, ts, half_arr, "reduce_add_halves"), mine, theirs))

    def add_all(ici, after):
        sums, theirs = exchange_wait("reduce_ici_wait", ici[1], after, _reduce_ici_plan)
        return ici[0], reduce_swap_start(_by_shape(lambda ps, ts: add_chips(ps, ts, chip_half, "reduce_add_chips"), sums, theirs))

    def finish(swap, after):
        reduced[swap[0]] = exchange_wait("reduce_swap_wait", swap[1], after, _reduce_swap_plan)[1]

    swaps = []
    for l in reversed(range(DEPTH)):
        xin, proj, xc, ycore, states, o_sb, y_all, x1, g, u = saved[l]
        w_in_f, w_out_f, w_gate_t, w_up_t, w_down_f = weights[l]
        dg, du, act = ffn_backward_act(dxb, g, u, w_down_f, "ffn_bwd_act", after=None if d2d is None else d2d[1][4])
        dx1, dx1b, h2, dn2 = rms_backward([dg, du], [w_gate_t, w_up_t], x1, norm2_w[l][None], dx, "ffn_bwd_norm", 512)
        if d2d is not None:
            ici = add_cores(d2d, dx1b)
        dyall = matmul_nt(dx1b, w_out_f, "out_proj_bwd", after=None if ici is None else ici[1][4])
        dw_down = matmul_tn(act, dxb, "dw_down")
        dw_gate = matmul_tn(dg, h2, "dw_gate")
        dw_up = matmul_tn(du, h2, "dw_up")
        dw_out = matmul_tn(y_all, dx1b, "dw_out")
        late = [dw.reshape(N_CHIPS, r, D_MODEL) for dw, r in zip((dw_out, dw_gate, dw_up, dw_down), SHARD_ROWS[1:])]
        if l == 0:
            early = ("0 late", reduce_d2d_start(late))
        dxc, dproj, dsn, ddsk, ddtb, dalog = ssd_backward(proj, xc, ycore, dyall, states, dtb[l], alog[l],
                                                          dskip_x[l], ssd_norm_w[l][None], "ssd_bwd")
        dproj, dcw, dcb = conv_backward(proj, dxc, cw8[l], conv_b[l][None], dproj, "conv_bwd",
                                        after=None if early is None else early[1][4])
        if early is not None:
            early = add_cores(early, dproj)
        dproj = sb_backward(proj, o_sb, dyall, dproj, "sb_bwd", after=None if early is None else early[1][4])
        dproj, dwbd, dpb, dps = pool_backward(proj, dyall, wbd[l], pool_b2[l], pool_scale[l][None], dproj, "pool_bwd")
        if ici is not None:
            swaps.append(add_all(ici, dproj))
            ici = None
        dx, dxb, h1, dn1 = rms_backward([dproj], [w_in_f], xin, norm1_w[l][None], dx1, "in_proj_bwd", 512,
                                        after=swaps[-1][1][4] if swaps else None)
        dw_in = _in_gradient_layout(matmul_tn(dproj, h1, "dw_in")[None])[0].reshape(N_CHIPS, SHARD_IN, D_MODEL)
        d2d = (l, reduce_d2d_start([dw_in] if l == 0 else [dw_in] + late))
        small["norm1_w"][l] = dn1[0]
        small["conv_w"][l] = dcw[:CONV_WIDTH]
        small["conv_b"][l] = dcb[0]
        small["dt_bias"][l] = ddtb[0, :SSD_HEADS]
        small["a_log"][l] = dalog[0, :SSD_HEADS]
        small["d_skip"][l] = ddsk.reshape(SSD_HEADS, HEAD_DIM).sum(axis=1)
        small["ssd_norm_w"][l] = dsn[0]
        small["pool_w"][l] = jnp.stack([dwbd[64 * k:64 * k + 64, 64 * k:64 * k + 64] for k in range(len(POOL_WINDOWS))])
        small["pool_b"][l] = dpb.reshape(len(POOL_WINDOWS), -1)
        small["pool_scale"][l] = dps[0]
        small["norm2_w"][l] = dn2[0]
    grad_x = dx[None]

    ici = add_cores(d2d, d2d[1][4])
    small_parts = [d_final if n == "final_norm_w" else jnp.stack(small[n]) for n in SMALL_NAMES]
    small_start = exchange_start("reduce_small_start", [_pack_small(small_parts)],
                                 [_landing((8, SMALL_ROWS, D_MODEL), F32)], 7, _all_devices_plan, after=ici[1][4])
    swaps.append(add_all(early, small_start[4]))
    swaps.append(add_all(ici, swaps[-1][1][4]))
    for swap in swaps:
        finish(swap, swaps[-1][1][4])
    (small_own,), (small_all,) = exchange_wait("reduce_small_wait", small_start, reduced[0][0], _all_devices_plan)
    small_sum = sum_devices(small_own, small_all, jnp.reshape(4 * px + 2 * py + pc, (1,)).astype(jnp.int32), "reduce_small_sum")
    reduced[0] = reduced[0] + reduced["0 late"]
    g_big = {n: [reduced[l][k] for l in range(DEPTH)] for k, n in enumerate(("w_in", "w_out", "w_gate", "w_up", "w_down"))}
    g_big["w_in"] = [gl[:COLS_IN] for gl in g_big["w_in"]]
    transposed = ("w_in", "w_gate", "w_up")

    g_small = dict(zip(SMALL_NAMES, _unpack_small(small_sum, [p.shape for p in small_parts])))
    g_small["final_norm_w"] = g_small["final_norm_w"].reshape(final_norm_w.shape)
    g_small["conv_w"] = lax.dynamic_slice_in_dim(g_small["conv_w"], chip * (CONV_DIM // N_CHIPS), CONV_DIM // N_CHIPS, axis=2)

    given = dict(norm1_w=(norm1_w, m_norm1_w, v_norm1_w), w_in=(w_in, m_w_in, v_w_in), conv_w=(conv_w, m_conv_w, v_conv_w),
                 conv_b=(conv_b, m_conv_b, v_conv_b), dt_bias=(dt_bias, m_dt_bias, v_dt_bias), a_log=(a_log, m_a_log, v_a_log),
                 d_skip=(d_skip, m_d_skip, v_d_skip), ssd_norm_w=(ssd_norm_w, m_ssd_norm_w, v_ssd_norm_w),
                 pool_w=(pool_w, m_pool_w, v_pool_w), pool_b=(pool_b, m_pool_b, v_pool_b),
                 pool_scale=(pool_scale, m_pool_scale, v_pool_scale), w_out=(w_out, m_w_out, v_w_out),
                 norm2_w=(norm2_w, m_norm2_w, v_norm2_w), w_gate=(w_gate, m_w_gate, v_w_gate), w_up=(w_up, m_w_up, v_w_up),
                 w_down=(w_down, m_w_down, v_w_down), final_norm_w=(final_norm_w, m_final_norm_w, v_final_norm_w))
    order = ("norm1_w", "w_in", "conv_w", "conv_b", "dt_bias", "a_log", "d_skip", "ssd_norm_w", "pool_w", "pool_b",
             "pool_scale", "w_out", "norm2_w", "w_gate", "w_up", "w_down", "final_norm_w")
    grads = dict(g_small)
    results = {}
    for n in ("w_in", "w_out", "w_gate", "w_up", "w_down"):
        w, m, v = given[n]
        if n in transposed:
            out = adamw_layers(jnp.swapaxes(w, 1, 2), g_big[n], jnp.swapaxes(m, 1, 2), jnp.swapaxes(v, 1, 2), "adamw_" + n)
            out = [jnp.swapaxes(o, 1, 2) for o in out]
        else:
            out = adamw_layers(w, g_big[n], m, v, "adamw_" + n)
        grads[n], results[n] = out[0], tuple(out[1:])
    small_shapes = [given[n][0].shape for n in SMALL_NAMES]
    packed = [_pack_small([given[n][k] for n in SMALL_NAMES])[None] for k in range(3)]
    packed_g = _pack_small([grads[n] for n in SMALL_NAMES])[None]
    small_out = adamw(packed[0], packed_g, packed[1], packed[2], "adamw_small")
    small_out = [_unpack_small(o[0], small_shapes) for o in small_out]
    for i, n in enumerate(SMALL_NAMES):
        results[n] = tuple(small_out[k][i] for k in range(3))

    return (loss, grad_x, *[grads[n] for n in order], *[results[n][0] for n in order],
            *[results[n][1] for n in order], *[results[n][2] for n in order])
```

```python
import numpy as np
import jax
import jax.numpy as jnp
from jax import lax
from jax.experimental import pallas as pl
from jax.experimental.pallas import tpu as pltpu

F32 = jnp.float32
BF16 = jnp.bfloat16
MESH = pl.DeviceIdType.MESH
ANY = pl.BlockSpec(memory_space=pl.ANY)

D_MODEL = 1024
DEPTH = 4
EPS = 1e-6
SSD_WIDTH = 512
SSD_HEADS = 8
HEAD_DIM = 64
D_STATE = 128
CHUNK = 128
CONV_WIDTH = 4
CONV_DIM = 1024
SB_WIDTH = 256
POOL_WIDTH = 256
POOL_WINDOWS = (2, 4, 8, 16)
D_FF = 2816
D_IN = 2568
N_CHIPS = 4
OFF_QKV, OFF_Z, OFF_DT, OFF_XBC, OFF_P = 0, 768, 1280, 1536, 2560
D_INP = 2816
ZDT = 768
SHARD_IN, SHARD_OUT, SHARD_FF = 704, 256, 704
COLS_IN = 642
ADAM_LR, ADAM_B1, ADAM_B2, ADAM_EPS, ADAM_WD, ADAM_STEP = 0.001, 0.9, 0.999, 1e-08, 0.01, 10
LANES = 128
VMEM_LIMIT = 56 * 1024 * 1024


def _params(sem=None):
    return pltpu.CompilerParams(dimension_semantics=sem, vmem_limit_bytes=VMEM_LIMIT)


def _tile(n, cap):
    best = None
    for t in range(LANES, min(n, cap) + 1, LANES):
        if n % t == 0:
            best = t
    assert best is not None, (n, cap)
    return best


def _nt(a, b):
    return lax.dot_general(a, b, (((1,), (1,)), ((), ())), preferred_element_type=F32)


def _tn(a, b):
    return lax.dot_general(a, b, (((0,), (0,)), ((), ())), preferred_element_type=F32)


def _nn(a, b):
    return jnp.dot(a, b, preferred_element_type=F32)


def _split_dot(a, b_exact, terms=3, dot=_nn):
    acc = None
    rest = a
    for _ in range(terms):
        hi = rest.astype(BF16)
        part = dot(hi, b_exact)
        acc = part if acc is None else acc + part
        rest = rest - hi.astype(F32)
    return acc


def _split_dot_left(a_exact, b, terms=3):
    acc = None
    rest = b
    for _ in range(terms):
        hi = rest.astype(BF16)
        part = _nn(a_exact, hi)
        acc = part if acc is None else acc + part
        rest = rest - hi.astype(F32)
    return acc


def _sigmoid(x):
    return 1.0 / (1.0 + jnp.exp(-x))


def _softplus(x):
    return jnp.maximum(x, 0.0) + jnp.log(1.0 + jnp.exp(-jnp.abs(x)))


def _iota2(shape, dim):
    return lax.broadcasted_iota(jnp.int32, shape, dim)


def _after(after):
    ops = [] if after is None else list(after) if isinstance(after, (list, tuple)) else [after]
    return [ANY] * len(ops), ops


def rms_matmul(x, nw, wt, name, after=None):
    s, d = x.shape
    n = wt.shape[0]
    tm, tn = _tile(s, 512), _tile(n, 2816)
    specs, ops = _after(after)

    def body(x_ref, nw_ref, w_ref, *rest):
        o_ref, h_ref = rest[len(ops):]

        @pl.when(pl.program_id(1) == 0)
        def _():
            xv = x_ref[...]
            r = lax.rsqrt(jnp.mean(xv * xv, axis=-1, keepdims=True) + EPS)
            h_ref[...] = (xv * r * nw_ref[...]).astype(BF16)
        o_ref[...] = _nt(h_ref[...], w_ref[...])

    return pl.pallas_call(
        body, name=name, grid=(s // tm, n // tn),
        in_specs=[pl.BlockSpec((tm, d), lambda i, j: (i, 0)), pl.BlockSpec((1, d), lambda i, j: (0, 0)),
                  pl.BlockSpec((tn, d), lambda i, j: (j, 0))] + specs,
        out_specs=pl.BlockSpec((tm, tn), lambda i, j: (i, j)),
        out_shape=jax.ShapeDtypeStruct((s, n), F32),
        scratch_shapes=[pltpu.VMEM((tm, d), BF16)],
        compiler_params=_params(("parallel", "arbitrary")),
    )(x, nw, wt, *ops)


def matmul_residual(a, w, res, name):
    s, k = a.shape
    n = w.shape[1]
    tm, tn = _tile(s, 512), _tile(n, 1024)

    def body(a_ref, w_ref, r_ref, o_ref):
        o_ref[...] = r_ref[...] + _nn(a_ref[...], w_ref[...])

    return pl.pallas_call(
        body, name=name, grid=(s // tm, n // tn),
        in_specs=[pl.BlockSpec((tm, k), lambda i, j: (i, 0)), pl.BlockSpec((k, tn), lambda i, j: (0, j)),
                  pl.BlockSpec((tm, tn), lambda i, j: (i, j))],
        out_specs=pl.BlockSpec((tm, tn), lambda i, j: (i, j)),
        out_shape=jax.ShapeDtypeStruct((s, n), F32),
        compiler_params=_params(("parallel", "parallel")),
    )(a, w, res)


def matmul_nt(a, w, name, out_dtype=F32, after=None):
    s, n = a.shape
    k = w.shape[0]
    tm, tk = _tile(s, 512), _tile(k, 1024)
    specs, ops = _after(after)

    def body(a_ref, w_ref, *rest):
        rest[-1][...] = _nt(a_ref[...], w_ref[...]).astype(out_dtype)

    return pl.pallas_call(
        body, name=name, grid=(s // tm, k // tk),
        in_specs=[pl.BlockSpec((tm, n), lambda i, j: (i, 0)), pl.BlockSpec((tk, n), lambda i, j: (j, 0))] + specs,
        out_specs=pl.BlockSpec((tm, tk), lambda i, j: (i, j)),
        out_shape=jax.ShapeDtypeStruct((s, k), out_dtype),
        compiler_params=_params(("parallel", "parallel")),
    )(a, w, *ops)


def matmul_tn(a, b, name, after=None):
    s, m = a.shape
    n = b.shape[1]
    tm, tn = _tile(m, 512), _tile(n, 1024)

    def body(a_ref, b_ref, *rest):
        rest[-1][...] = _tn(a_ref[...], b_ref[...]).astype(BF16)

    specs, ops = _after(after)
    return pl.pallas_call(
        body, name=name, grid=(m // tm, n // tn),
        in_specs=[pl.BlockSpec((s, tm), lambda i, j: (0, i)), pl.BlockSpec((s, tn), lambda i, j: (0, j))] + specs,
        out_specs=pl.BlockSpec((tm, tn), lambda i, j: (i, j)),
        out_shape=jax.ShapeDtypeStruct((m, n), BF16),
        compiler_params=_params(("parallel", "parallel")),
    )(a, b, *ops)


def ffn_forward(x1, nw, wgt, wut, wd, name):
    s, d = x1.shape
    f = wgt.shape[0]
    tm, tf = _tile(s, 1024), _tile(f, 256)

    def body(x_ref, nw_ref, wg_ref, wu_ref, wd_ref, o_ref, g_ref, u_ref, h_ref, acc_ref):
        j = pl.program_id(1)

        @pl.when(j == 0)
        def _():
            xv = x_ref[...]
            r = lax.rsqrt(jnp.mean(xv * xv, axis=-1, keepdims=True) + EPS)
            h_ref[...] = (xv * r * nw_ref[...]).astype(BF16)
            acc_ref[...] = xv

        h = h_ref[...]
        g = _nt(h, wg_ref[...])
        u = _nt(h, wu_ref[...])
        g_ref[...] = g.astype(BF16)
        u_ref[...] = u.astype(BF16)
        a = (g * _sigmoid(g) * u).astype(BF16)
        acc_ref[...] += _nn(a, wd_ref[...])

        @pl.when(j == pl.num_programs(1) - 1)
        def _():
            o_ref[...] = acc_ref[...]

    wblk = pl.BlockSpec((tf, d), lambda i, j: (j, 0))
    return pl.pallas_call(
        body, name=name, grid=(s // tm, f // tf),
        in_specs=[pl.BlockSpec((tm, d), lambda i, j: (i, 0)), pl.BlockSpec((1, d), lambda i, j: (0, 0)), wblk, wblk, wblk],
        out_specs=[pl.BlockSpec((tm, d), lambda i, j: (i, 0)), pl.BlockSpec((tm, tf), lambda i, j: (i, j)),
                   pl.BlockSpec((tm, tf), lambda i, j: (i, j))],
        out_shape=[jax.ShapeDtypeStruct((s, d), F32), jax.ShapeDtypeStruct((s, f), BF16),
                   jax.ShapeDtypeStruct((s, f), BF16)],
        scratch_shapes=[pltpu.VMEM((tm, d), BF16), pltpu.VMEM((tm, d), F32)],
        compiler_params=_params(("parallel", "arbitrary")),
    )(x1, nw, wgt, wut, wd)


def ffn_backward_act(dx2, g, u, wd, name, after=None):
    s, d = dx2.shape
    f = wd.shape[0]
    tm, tf = _tile(s, 256), _tile(f, 2816)
    specs, ops = _after(after)

    def body(dx_ref, g_ref, u_ref, wd_ref, *rest):
        dg_ref, du_ref, a_ref = rest[len(ops):]
        da = _nt(dx_ref[...], wd_ref[...])
        gv = g_ref[...].astype(F32)
        uv = u_ref[...].astype(F32)
        sg = _sigmoid(gv)
        silu = gv * sg
        dg_ref[...] = (da * uv * (sg * (1.0 + gv * (1.0 - sg)))).astype(BF16)
        du_ref[...] = (da * silu).astype(BF16)
        a_ref[...] = (silu * uv).astype(BF16)

    blk = pl.BlockSpec((tm, tf), lambda i, j: (i, j))
    return pl.pallas_call(
        body, name=name, grid=(s // tm, f // tf),
        in_specs=[pl.BlockSpec((tm, d), lambda i, j: (i, 0)), blk, blk, pl.BlockSpec((tf, d), lambda i, j: (j, 0))] + specs,
        out_specs=[blk, blk, blk],
        out_shape=[jax.ShapeDtypeStruct((s, f), BF16)] * 3,
        compiler_params=_params(("parallel", "parallel")),
    )(dx2, g, u, wd, *ops)


def rms_backward(dzs, wts, x, nw, dres, name, tm, after=None):
    s, d = x.shape
    nz = len(dzs)
    specs, ops = _after(after)

    def body(*refs):
        dz_refs, w_refs = refs[:nz], refs[nz:2 * nz]
        x_ref, nw_ref, dres_ref = refs[2 * nz:2 * nz + 3]
        dx_ref, dxb_ref, h_ref, dnw_ref = refs[2 * nz + 3 + len(ops):]
        dh = _nn(dz_refs[0][...], w_refs[0][...])
        for k in range(1, nz):
            dh = dh + _nn(dz_refs[k][...], w_refs[k][...])
        xv = x_ref[...]
        r = lax.rsqrt(jnp.mean(xv * xv, axis=-1, keepdims=True) + EPS)
        xhat = xv * r
        nwv = nw_ref[...]
        h_ref[...] = (xhat * nwv).astype(BF16)

        @pl.when(pl.program_id(0) == 0)
        def _():
            dnw_ref[...] = jnp.zeros_like(dnw_ref)

        dnw_ref[...] += jnp.sum(dh * xhat, axis=0, keepdims=True)
        gdh = dh * nwv
        dx = dres_ref[...] + r * (gdh - xhat * jnp.mean(gdh * xhat, axis=-1, keepdims=True))
        dx_ref[...] = dx
        dxb_ref[...] = dx.astype(BF16)

    row = pl.BlockSpec((tm, d), lambda i: (i, 0))
    in_specs = [pl.BlockSpec((tm, dz.shape[1]), lambda i: (i, 0)) for dz in dzs]
    in_specs += [pl.BlockSpec(w.shape, lambda i: (0, 0), pipeline_mode=pl.Buffered(1)) for w in wts]
    in_specs += [row, pl.BlockSpec((1, d), lambda i: (0, 0)), row] + specs
    return pl.pallas_call(
        body, name=name, grid=(s // tm,),
        in_specs=in_specs,
        out_specs=[row, row, row, pl.BlockSpec((1, d), lambda i: (0, 0))],
        out_shape=[jax.ShapeDtypeStruct((s, d), F32), jax.ShapeDtypeStruct((s, d), BF16),
                   jax.ShapeDtypeStruct((s, d), BF16), jax.ShapeDtypeStruct((1, d), F32)],
        compiler_params=_params(("arbitrary",)),
    )(*dzs, *wts, x, nw, dres, *ops)


def loss_head(x, nw, target, name):
    s, d = x.shape
    tm = _tile(s, 512)

    def body(x_ref, nw_ref, t_ref, loss_ref, dx_ref, dxb_ref, dnw_ref):
        xv = x_ref[...]
        r = lax.rsqrt(jnp.mean(xv * xv, axis=-1, keepdims=True) + EPS)
        xhat = xv * r
        nwv = nw_ref[...]
        err = xhat * nwv - t_ref[...]

        @pl.when(pl.program_id(0) == 0)
        def _():
            dnw_ref[...] = jnp.zeros_like(dnw_ref)
            loss_ref[...] = jnp.zeros_like(loss_ref)

        part = jnp.sum(jnp.sum(err * err, axis=-1, keepdims=True), axis=0, keepdims=True) * (0.5 / d)
        loss_ref[...] += jnp.broadcast_to(part, loss_ref.shape)
        dout = err * (1.0 / d)
        dnw_ref[...] += jnp.sum(dout * xhat, axis=0, keepdims=True)
        gdh = dout * nwv
        dx = r * (gdh - xhat * jnp.mean(gdh * xhat, axis=-1, keepdims=True))
        dx_ref[...] = dx
        dxb_ref[...] = dx.astype(BF16)

    row = pl.BlockSpec((tm, d), lambda i: (i, 0))
    return pl.pallas_call(
        body, name=name, grid=(s // tm,),
        in_specs=[row, pl.BlockSpec((1, d), lambda i: (0, 0)), row],
        out_specs=[pl.BlockSpec((1, LANES), lambda i: (0, 0)), row, row, pl.BlockSpec((1, d), lambda i: (0, 0))],
        out_shape=[jax.ShapeDtypeStruct((1, LANES), F32), jax.ShapeDtypeStruct((s, d), F32),
                   jax.ShapeDtypeStruct((s, d), BF16), jax.ShapeDtypeStruct((1, d), F32)],
        compiler_params=_params(("arbitrary",)),
    )(x, nw, target)


def _shift_down(x, k):
    return jnp.where(_iota2(x.shape, 0) >= k, pltpu.roll(x, k, axis=0), 0.0)


def _shift_up(x, k):
    s = x.shape[0]
    return jnp.where(_iota2(x.shape, 0) < s - k, pltpu.roll(x, s - k, axis=0), 0.0)


CONV_TILE = 256


def conv_forward(proj, cw, cb, name):
    s = proj.shape[0]
    tn = CONV_TILE
    off = OFF_XBC // tn

    def body(u_ref, w_ref, b_ref, o_ref):
        u = u_ref[...]
        pre = b_ref[...] + w_ref[CONV_WIDTH - 1:CONV_WIDTH, :] * u
        for i in range(CONV_WIDTH - 1):
            pre = pre + w_ref[i:i + 1, :] * _shift_down(u, CONV_WIDTH - 1 - i)
        o_ref[...] = pre * _sigmoid(pre)

    return pl.pallas_call(
        body, name=name, grid=(CONV_DIM // tn,),
        in_specs=[pl.BlockSpec((s, tn), lambda j: (0, off + j)), pl.BlockSpec((8, tn), lambda j: (0, j)),
                  pl.BlockSpec((1, tn), lambda j: (0, j))],
        out_specs=pl.BlockSpec((s, tn), lambda j: (0, j)),
        out_shape=jax.ShapeDtypeStruct((s, CONV_DIM), F32),
        compiler_params=_params(("parallel",)),
    )(proj, cw, cb)


def conv_backward(proj, dxc, cw, cb, dproj, name, after=None):
    s = proj.shape[0]
    tn = CONV_TILE
    off = OFF_XBC // tn

    specs, ops = _after(after)

    def body(u_ref, d_ref, w_ref, b_ref, *rest):
        du_ref, dw_ref, db_ref = rest[-3:]
        u = u_ref[...]
        shifted = [_shift_down(u, CONV_WIDTH - 1 - i) for i in range(CONV_WIDTH - 1)] + [u]
        pre = b_ref[...] + w_ref[CONV_WIDTH - 1:CONV_WIDTH, :] * u
        for i in range(CONV_WIDTH - 1):
            pre = pre + w_ref[i:i + 1, :] * shifted[i]
        sg = _sigmoid(pre)
        dpre = d_ref[...] * (sg * (1.0 + pre * (1.0 - sg)))
        du = w_ref[CONV_WIDTH - 1:CONV_WIDTH, :] * dpre
        for i in range(CONV_WIDTH - 1):
            du = du + w_ref[i:i + 1, :] * _shift_up(dpre, CONV_WIDTH - 1 - i)
        du_ref[...] = du.astype(BF16)
        rows = [jnp.sum(dpre * shifted[i], axis=0, keepdims=True) for i in range(CONV_WIDTH)]
        rows.append(jnp.zeros((8 - CONV_WIDTH, tn), F32))
        dw_ref[...] = jnp.concatenate(rows, axis=0)
        db_ref[...] = jnp.sum(dpre, axis=0, keepdims=True)

    return pl.pallas_call(
        body, name=name, grid=(CONV_DIM // tn,),
        in_specs=[pl.BlockSpec((s, tn), lambda j: (0, off + j)), pl.BlockSpec((s, tn), lambda j: (0, j)),
                  pl.BlockSpec((8, tn), lambda j: (0, j)), pl.BlockSpec((1, tn), lambda j: (0, j)), ANY] + specs,
        out_specs=[pl.BlockSpec((s, tn), lambda j: (0, off + j)), pl.BlockSpec((8, tn), lambda j: (0, j)),
                   pl.BlockSpec((1, tn), lambda j: (0, j))],
        out_shape=[jax.ShapeDtypeStruct(dproj.shape, BF16), jax.ShapeDtypeStruct((8, CONV_DIM), F32),
                   jax.ShapeDtypeStruct((1, CONV_DIM), F32)],
        input_output_aliases={4: 0},
        compiler_params=_params(("parallel",)),
    )(proj, dxc, cw, cb, dproj, *ops)


def _pool_lane_window(shape):
    grp = _iota2(shape, 1) // (POOL_WIDTH // len(POOL_WINDOWS))
    win = jnp.full(shape, POOL_WINDOWS[-1], jnp.int32)
    for gi in range(len(POOL_WINDOWS) - 2, -1, -1):
        win = jnp.where(grp == gi, POOL_WINDOWS[gi], win)
    return grp, win


def _pool_select(grp, sums):
    out = sums[-1]
    for gi in range(len(sums) - 2, -1, -1):
        out = jnp.where(grp == gi, sums[gi], out)
    return out


def _pool_pooled(p):
    grp, win = _pool_lane_window(p.shape)
    inv_count = 1.0 / jnp.minimum(_iota2(p.shape, 0) + 1, win).astype(F32)
    sums, acc, k = [], p, 1
    for _ in POOL_WINDOWS:
        acc = acc + _shift_down(acc, k)
        sums.append(acc)
        k *= 2
    return _pool_select(grp, sums) * inv_count - p, grp, inv_count


def pool_forward(proj, wbd, pb, ps, y_all, name, after=None):
    s = proj.shape[0]
    specs, ops = _after(after)

    def body(p_ref, w_ref, b_ref, s_ref, *rest):
        o_ref = rest[-1]
        pooled, _, _ = _pool_pooled(p_ref[...])
        mixed = _nn(pooled.astype(BF16), w_ref[...]) + b_ref[...]
        o_ref[...] = (mixed * s_ref[...]).astype(BF16)

    vec = pl.BlockSpec((1, POOL_WIDTH), lambda j: (0, 0))
    return pl.pallas_call(
        body, name=name, grid=(1,),
        in_specs=[pl.BlockSpec((s, POOL_WIDTH), lambda j: (0, OFF_P // POOL_WIDTH)),
                  pl.BlockSpec((POOL_WIDTH, POOL_WIDTH), lambda j: (0, 0)), vec, vec, ANY] + specs,
        out_specs=pl.BlockSpec((s, POOL_WIDTH), lambda j: (0, (SSD_WIDTH + SB_WIDTH) // POOL_WIDTH)),
        out_shape=jax.ShapeDtypeStruct(y_all.shape, BF16),
        input_output_aliases={4: 0},
        compiler_params=_params(("arbitrary",)),
    )(proj, wbd, pb, ps, y_all, *ops)


def pool_backward(proj, dyall, wbd, pb, ps, dproj, name):
    s = proj.shape[0]

    def body(p_ref, dy_ref, w_ref, b_ref, s_ref, _, dp_ref, dw_ref, db_ref, ds_ref):
        pooled, grp, inv_count = _pool_pooled(p_ref[...])
        pooled_b = pooled.astype(BF16)
        mixed = _nn(pooled_b, w_ref[...]) + b_ref[...]
        dy = dy_ref[...]
        ds_ref[...] = jnp.sum(dy * mixed, axis=0, keepdims=True)
        dmixed = dy * s_ref[...]
        db_ref[...] = jnp.sum(dmixed, axis=0, keepdims=True)
        dmixed_b = dmixed.astype(BF16)
        dw_ref[...] = _tn(pooled_b, dmixed_b)
        dpooled = _nt(dmixed_b, w_ref[...])
        sums, acc, k = [], dpooled * inv_count, 1
        for _ in POOL_WINDOWS:
            acc = acc + _shift_up(acc, k)
            sums.append(acc)
            k *= 2
        dp_ref[...] = (_pool_select(grp, sums) - dpooled).astype(BF16)

    vec = pl.BlockSpec((1, POOL_WIDTH), lambda j: (0, 0))
    mat = pl.BlockSpec((POOL_WIDTH, POOL_WIDTH), lambda j: (0, 0))
    pcol = pl.BlockSpec((s, POOL_WIDTH), lambda j: (0, OFF_P // POOL_WIDTH))
    return pl.pallas_call(
        body, name=name, grid=(1,),
        in_specs=[pcol, pl.BlockSpec((s, POOL_WIDTH), lambda j: (0, (SSD_WIDTH + SB_WIDTH) // POOL_WIDTH)), mat, vec, vec, ANY],
        out_specs=[pcol, mat, vec, vec],
        out_shape=[jax.ShapeDtypeStruct(dproj.shape, BF16), jax.ShapeDtypeStruct((POOL_WIDTH, POOL_WIDTH), F32),
                   jax.ShapeDtypeStruct((1, POOL_WIDTH), F32), jax.ShapeDtypeStruct((1, POOL_WIDTH), F32)],
        input_output_aliases={5: 0},
        compiler_params=_params(("arbitrary",)),
    )(proj, dyall, wbd, pb, ps, dproj)


N_PAIRS = SSD_HEADS // 2


def _ssd_common(xc, dtraw, dtb, alog):
    c = CHUNK
    dt = _softplus(dtraw + dtb)
    a = -jnp.exp(alog)
    ltri = (_iota2((c, c), 0) >= _iota2((c, c), 1)).astype(BF16)
    acum = _split_dot_left(ltri, dt * a)
    expand = (_iota2((c, SSD_WIDTH), 1) // HEAD_DIM == _iota2((c, SSD_WIDTH), 0)).astype(BF16)
    expand_wide = (_iota2((c, SSD_HEADS * c), 1) // c == _iota2((c, SSD_HEADS * c), 0)).astype(BF16)
    acum_x = _split_dot(acum, expand, 2)
    dt_x = _split_dot(dt, expand, 2)
    alast_x = acum_x[c - 1:c, :]
    return dict(dt=dt, a=a, acum=acum, acum_x=acum_x, dt_x=dt_x, ea_x=jnp.exp(acum_x),
                dte_x=jnp.exp(alast_x - acum_x), eal_x=jnp.exp(alast_x),
                acol=_split_dot(acum, expand_wide, 2), acum_t=acum.T,
                xs=xc[:, :SSD_WIDTH], causal=_iota2((c, c), 0) >= _iota2((c, c), 1),
                left=_iota2((c, c), 1) < HEAD_DIM)


def _ssd_group(xc, g):
    b = xc[:, SSD_WIDTH + D_STATE * g:SSD_WIDTH + D_STATE * (g + 1)]
    cm = xc[:, SSD_WIDTH + 2 * D_STATE + D_STATE * g:SSD_WIDTH + 2 * D_STATE + D_STATE * (g + 1)]
    return b, cm


def _ssd_decay(q, hh):
    col = q["acol"][:, CHUNK * hh:CHUNK * (hh + 1)]
    row = q["acum_t"][hh:hh + 1, :]
    return jnp.where(q["causal"], jnp.exp(jnp.minimum(col - row, 0.0)), 0.0)


def ssd_forward(proj, xc, dtb, alog, dskip_x, nw, name):
    s = xc.shape[0]
    nc = s // CHUNK

    def body(xc_ref, zdt_ref, dtb_ref, alog_ref, dsk_ref, nw_ref, y_ref, yc_ref, st_ref, state):
        @pl.when(pl.program_id(0) == 0)
        def _():
            state[...] = jnp.zeros_like(state)

        xcv = xc_ref[...]
        q = _ssd_common(xcv, zdt_ref[:, SSD_WIDTH:SSD_WIDTH + LANES], dtb_ref[...], alog_ref[...])
        x = q["xs"] * q["dt_x"]
        xb = x.astype(BF16)
        xd = (x * q["dte_x"]).astype(BF16)
        pieces = []
        for g in range(2):
            bg, cg = _ssd_group(xcv, g)
            bgb, cgb = bg.astype(BF16), cg.astype(BF16)
            cb = _nt(cgb, bgb)
            bgt = bg.T.astype(BF16)
            for pr in (2 * g, 2 * g + 1):
                sl = slice(CHUNK * pr, CHUNK * (pr + 1))
                st = state[pr]
                st_ref[0, pr] = st
                yp = _nn(cgb, st.astype(BF16)) * q["ea_x"][:, sl]
                for k, hh in enumerate((2 * pr, 2 * pr + 1)):
                    w = (cb * _ssd_decay(q, hh)).astype(BF16)
                    mask = q["left"] if k == 0 else jnp.logical_not(q["left"])
                    yp = yp + _nn(w, jnp.where(mask, xb[:, sl], jnp.zeros_like(xb[:, sl])))
                state[pr] = st * q["eal_x"][:, sl] + _nn(bgt, xd[:, sl])
                pieces.append(yp)
        y = jnp.concatenate(pieces, axis=1) + q["xs"] * dsk_ref[...]
        yc_ref[...] = y
        zv = zdt_ref[:, :SSD_WIDTH]
        yg = y * (zv * _sigmoid(zv))
        r = lax.rsqrt(jnp.mean(yg * yg, axis=-1, keepdims=True) + EPS)
        y_ref[...] = (yg * r * nw_ref[...]).astype(BF16)

    vec = lambda n: pl.BlockSpec((1, n), lambda c: (0, 0))
    return pl.pallas_call(
        body, name=name, grid=(nc,),
        in_specs=[pl.BlockSpec((CHUNK, CONV_DIM), lambda c: (c, 0)),
                  pl.BlockSpec((CHUNK, ZDT), lambda c: (c, OFF_Z // ZDT)),
                  vec(LANES), vec(LANES), vec(SSD_WIDTH), vec(SSD_WIDTH)],
        out_specs=[pl.BlockSpec((CHUNK, SSD_WIDTH), lambda c: (c, 0)), pl.BlockSpec((CHUNK, SSD_WIDTH), lambda c: (c, 0)),
                   pl.BlockSpec((1, N_PAIRS, D_STATE, CHUNK), lambda c: (c, 0, 0, 0))],
        out_shape=[jax.ShapeDtypeStruct((s, D_MODEL), BF16), jax.ShapeDtypeStruct((s, SSD_WIDTH), F32),
                   jax.ShapeDtypeStruct((nc, N_PAIRS, D_STATE, CHUNK), F32)],
        scratch_shapes=[pltpu.VMEM((N_PAIRS, D_STATE, CHUNK), F32)],
        compiler_params=_params(("arbitrary",)),
    )(xc, proj, dtb, alog, dskip_x, nw)


def ssd_backward(proj, xc, ycore, dyall, states, dtb, alog, dskip_x, nw, name):
    s = xc.shape[0]
    nc = s // CHUNK
    c = CHUNK

    def body(xc_ref, zdt_ref, yc_ref, dy_ref, st_ref, dtb_ref, alog_ref, dsk_ref, nw_ref,
             dxc_ref, dzdt_ref, dnw_ref, ddsk_ref, ddtb_ref, dalog_ref, dstate):
        @pl.when(pl.program_id(0) == 0)
        def _():
            dstate[...] = jnp.zeros_like(dstate)
            dnw_ref[...] = jnp.zeros_like(dnw_ref)
            ddsk_ref[...] = jnp.zeros_like(ddsk_ref)
            ddtb_ref[...] = jnp.zeros_like(ddtb_ref)
            dalog_ref[...] = jnp.zeros_like(dalog_ref)

        xcv = xc_ref[...]
        dtraw = zdt_ref[:, SSD_WIDTH:SSD_WIDTH + LANES]
        q = _ssd_common(xcv, dtraw, dtb_ref[...], alog_ref[...])
        xs = q["xs"]
        x = xs * q["dt_x"]
        zv, yc, dy, nwv = zdt_ref[:, :SSD_WIDTH], yc_ref[...], dy_ref[...], nw_ref[...]
        sgz = _sigmoid(zv)
        siluz = zv * sgz
        yg = yc * siluz
        r = lax.rsqrt(jnp.mean(yg * yg, axis=-1, keepdims=True) + EPS)
        dnw_ref[...] += jnp.sum(dy * yg * r, axis=0, keepdims=True)
        g1 = dy * nwv
        dyg = r * (g1 - yg * (r * r) * jnp.mean(g1 * yg, axis=-1, keepdims=True))
        dyv = dyg * siluz
        dz = (dyg * yc * (sgz * (1.0 + zv * (1.0 - sgz)))).astype(BF16)
        ddsk_ref[...] += jnp.sum(dyv * xs, axis=0, keepdims=True)
        dye = dyv * q["ea_x"]
        dx_parts, yoff_parts, u_parts, v_parts, e_parts = [], [], [], [], []
        db_parts, dc_parts = [], []
        for g in range(2):
            bg, cg = _ssd_group(xcv, g)
            bgb, cgb = bg.astype(BF16), cg.astype(BF16)
            cb = _nt(cgb, bgb)
            cgt = cg.T.astype(BF16)
            dgsum = jnp.zeros((c, c), F32)
            dbg = jnp.zeros((c, D_STATE), F32)
            dcg = jnp.zeros((c, D_STATE), F32)
            for pr in (2 * g, 2 * g + 1):
                sl = slice(c * pr, c * (pr + 1))
                st = st_ref[0, pr]
                dst = dstate[pr]
                stb, dstb = st.astype(BF16), dst.astype(BF16)
                xp = x[:, sl]
                xpb = xp.astype(BF16)
                dyp = dyv[:, sl]
                xdp = xp * q["dte_x"][:, sl]
                yoff_parts.append(_nn(cgb, stb) * q["ea_x"][:, sl])
                rr = _nn(bgb, dstb)
                dxp = rr * q["dte_x"][:, sl]
                u_parts.append(rr * xdp)
                v_parts.append(dst * st * q["eal_x"][:, sl])
                for k, hh in enumerate((2 * pr, 2 * pr + 1)):
                    decay = _ssd_decay(q, hh)
                    w = cb * decay
                    mask = q["left"] if k == 0 else jnp.logical_not(q["left"])
                    dym = jnp.where(mask, dyp, 0.0).astype(BF16)
                    dw = _nt(dym, xpb)
                    dgsum = dgsum + dw * decay
                    e_parts.append(dw * w)
                    dxp = dxp + _nn(w.T.astype(BF16), dym)
                dyeb = dye[:, sl].astype(BF16)
                dcg = dcg + _nt(dyeb, stb)
                dbg = dbg + _nt(xdp.astype(BF16), dstb)
                dstate[pr] = dst * q["eal_x"][:, sl] + _nn(cgt, dyeb)
                dx_parts.append(dxp)
            dcg = dcg + _nn(dgsum.astype(BF16), bgb)
            dbg = dbg + _nn(dgsum.T.astype(BF16), cgb)
            db_parts.append(dbg)
            dc_parts.append(dcg)
        dx = jnp.concatenate(dx_parts, axis=1)
        yoff = jnp.concatenate(yoff_parts, axis=1)
        u = jnp.concatenate(u_parts, axis=1)
        v = jnp.concatenate(v_parts, axis=1)
        reduce_heads = (_iota2((SSD_WIDTH, c), 0) // HEAD_DIM == _iota2((SSD_WIDTH, c), 1)).astype(BF16)
        to_head = (_iota2((SSD_HEADS * c, c), 0) // c == _iota2((SSD_HEADS * c, c), 1)).astype(BF16)
        da = _split_dot(dyv * yoff - u, reduce_heads, 2)
        da = da + _split_dot(jnp.concatenate(e_parts, axis=1), to_head, 2)
        da = da - _split_dot(jnp.concatenate(e_parts, axis=0), to_head, 2, dot=_tn)
        dalast = jnp.sum(_split_dot(u + v, reduce_heads, 2), axis=0, keepdims=True)
        da = da + jnp.where(_iota2((c, c), 0) == c - 1, dalast, 0.0)
        utri = (_iota2((c, c), 1) >= _iota2((c, c), 0)).astype(BF16)
        dda = _split_dot_left(utri, da)
        ddt = dda * q["a"] + _split_dot(dx * xs, reduce_heads, 2)
        dalog_ref[...] += jnp.sum(dda * q["dt"], axis=0, keepdims=True) * q["a"]
        ddtraw = jnp.where(_iota2((c, c), 1) < SSD_HEADS, ddt * _sigmoid(dtraw + dtb_ref[...]), 0.0)
        ddtb_ref[...] += jnp.sum(ddtraw, axis=0, keepdims=True)
        dzdt_ref[...] = jnp.concatenate([dz, ddtraw.astype(BF16), jnp.zeros((c, ZDT - SSD_WIDTH - LANES), BF16)], axis=1)
        dxs = dx * q["dt_x"] + dyv * dsk_ref[...]
        dxc_ref[...] = jnp.concatenate([dxs] + db_parts + dc_parts, axis=1)

    rev = lambda i: nc - 1 - i
    vec = lambda n: pl.BlockSpec((1, n), lambda i: (0, 0))
    wide = pl.BlockSpec((c, SSD_WIDTH), lambda i: (rev(i), 0))
    zdt = pl.BlockSpec((c, ZDT), lambda i: (rev(i), OFF_Z // ZDT))
    return pl.pallas_call(
        body, name=name, grid=(nc,),
        in_specs=[pl.BlockSpec((c, CONV_DIM), lambda i: (rev(i), 0)), zdt, wide, wide,
                  pl.BlockSpec((1, N_PAIRS, D_STATE, c), lambda i: (rev(i), 0, 0, 0)),
                  vec(LANES), vec(LANES), vec(SSD_WIDTH), vec(SSD_WIDTH)],
        out_specs=[pl.BlockSpec((c, CONV_DIM), lambda i: (rev(i), 0)), zdt,
                   vec(SSD_WIDTH), vec(SSD_WIDTH), vec(LANES), vec(LANES)],
        out_shape=[jax.ShapeDtypeStruct((s, CONV_DIM), F32), jax.ShapeDtypeStruct((s, D_INP), BF16),
                   jax.ShapeDtypeStruct((1, SSD_WIDTH), F32),
                   jax.ShapeDtypeStruct((1, SSD_WIDTH), F32), jax.ShapeDtypeStruct((1, LANES), F32),
                   jax.ShapeDtypeStruct((1, LANES), F32)],
        scratch_shapes=[pltpu.VMEM((N_PAIRS, D_STATE, c), F32)],
        compiler_params=_params(("arbitrary",)),
    )(xc, proj, ycore, dyall, states, dtb, alog, dskip_x, nw)


SB_Q, SB_K = 512, 512
SB_T = 256
SB_SCALE = HEAD_DIM ** -0.5


def _key_suffix(x, tri, terms):
    runs = [x[:, SB_T * k:SB_T * (k + 1)] for k in range(SB_K // SB_T)]
    sums = [_split_dot(r, tri, terms) for r in runs]
    later = None
    for k in range(len(runs) - 1, -1, -1):
        if later is not None:
            sums[k] = sums[k] + later
        total = jnp.sum(runs[k], axis=1, keepdims=True)
        later = total if later is None else later + total
    return jnp.concatenate(sums, axis=1), later


def _sb_weights(qm, kb, diagonal, run_lk, strict_after):
    z = _nt(qm, kb)
    nz = -z
    tail = jnp.log(1.0 + jnp.exp(jnp.minimum(z, nz)))
    ls = jnp.minimum(z, 0.0) - tail
    lk = jnp.minimum(nz, 0.0) - tail
    if diagonal is not None:
        valid = _iota2(z.shape, 1) < _iota2(z.shape, 0) + diagonal
        lk = jnp.where(valid, lk, 0.0)
    after, total = _key_suffix(lk, strict_after, 1)
    w = jnp.exp(ls + after + run_lk)
    if diagonal is not None:
        w = jnp.where(valid, w, 0.0)
    return ls, total, w


def _sb_sweep(i, block, init):
    own = (i * SB_Q) // SB_K
    first = block(own, init, i * SB_Q - own * SB_K)
    return lax.fori_loop(1, own + 1, lambda jj, carry: block(own - jj, carry, None), first)


def sb_forward(proj, y_all, name, after=None):
    s = proj.shape[0]
    t, tk = SB_Q, SB_K
    nq = s // t
    specs, ops = _after(after)

    def body(q_ref, k_ref, v_ref, *rest):
        y_ref, o_ref = rest[-2:]
        i = pl.program_id(1)
        left = _iota2((t, LANES), 1) < HEAD_DIM
        left_k = _iota2((tk, LANES), 1) < HEAD_DIM
        qv = q_ref[...] * SB_SCALE
        zero = jnp.zeros_like(qv)
        qms = (jnp.where(left, qv, zero).astype(BF16), jnp.where(left, zero, qv).astype(BF16))
        strict_after = (_iota2((SB_T, SB_T), 0) > _iota2((SB_T, SB_T), 1)).astype(BF16)

        def block(j, carry, diagonal):
            o, runs = carry[0], carry[1:]
            rows = pl.ds(pl.multiple_of(j * tk, tk), tk)
            kb = k_ref[rows, :].astype(BF16)
            vv = v_ref[rows, :]
            new_runs = []
            for k in range(2):
                _, total, w = _sb_weights(qms[k], kb, diagonal, runs[k], strict_after)
                vm = jnp.where(left_k if k == 0 else jnp.logical_not(left_k), vv, 0.0).astype(BF16)
                o = o + _nn(w.astype(BF16), vm)
                new_runs.append(runs[k] + total)
            return (o, *new_runs)

        init = (jnp.zeros((t, LANES), F32), jnp.zeros((t, 1), F32), jnp.zeros((t, 1), F32))
        o = _sb_sweep(i, block, init)[0]
        o_ref[...] = o
        y_ref[...] = o.astype(BF16)

    return pl.pallas_call(
        body, name=name, grid=(2, nq),
        in_specs=[pl.BlockSpec((t, LANES), lambda p, i: (i, 3 * p)),
                  pl.BlockSpec((s, LANES), lambda p, i: (0, 3 * p + 1)),
                  pl.BlockSpec((s, LANES), lambda p, i: (0, 3 * p + 2)), ANY] + specs,
        out_specs=[pl.BlockSpec((t, LANES), lambda p, i: (i, SSD_WIDTH // LANES + p)),
                   pl.BlockSpec((t, LANES), lambda p, i: (i, p))],
        out_shape=[jax.ShapeDtypeStruct(y_all.shape, BF16), jax.ShapeDtypeStruct((s, SB_WIDTH), F32)],
        input_output_aliases={3: 0},
        compiler_params=_params(("parallel", "arbitrary")),
    )(proj, proj, proj, y_all, *ops)


def sb_backward(proj, o, dyall, dproj, name, after=None):
    s = proj.shape[0]
    t, tk = SB_Q, SB_K
    nq = s // t
    specs, ops = _after(after)

    def body(q_ref, k_ref, v_ref, o_ref, do_ref, *rest):
        dqkv_ref, dk_acc, dv_acc = rest[-3:]
        dk_acc[...] = jnp.zeros_like(dk_acc)
        dv_acc[...] = jnp.zeros_like(dv_acc)
        left = _iota2((t, LANES), 1) < HEAD_DIM
        lane_masks = (left, jnp.logical_not(left))
        left_k = _iota2((tk, LANES), 1) < HEAD_DIM
        key_masks = (left_k, jnp.logical_not(left_k))
        strict_after = (_iota2((SB_T, SB_T), 0) > _iota2((SB_T, SB_T), 1)).astype(BF16)
        from_here = (_iota2((SB_T, SB_T), 0) >= _iota2((SB_T, SB_T), 1)).astype(BF16)

        def query_block(i, _):
            qrows = pl.ds(pl.multiple_of(i * t, t), t)
            qv = q_ref[qrows, :] * SB_SCALE
            dov = do_ref[qrows, :]
            zero = jnp.zeros_like(qv)
            qb = qv.astype(BF16)
            dob = dov.astype(BF16)
            prod = dob.astype(F32) * o_ref[qrows, :]
            qms = [jnp.where(m, qv, zero).astype(BF16) for m in lane_masks]
            doms = [jnp.where(m, dov, zero).astype(BF16) for m in lane_masks]
            deltas = [jnp.sum(jnp.where(m, prod, zero), axis=1, keepdims=True) for m in lane_masks]

            def block(j, carry, diagonal):
                dq = carry[0]
                run_lk, run_e = carry[1:3], carry[3:5]
                rows = pl.ds(pl.multiple_of(j * tk, tk), tk)
                kb = k_ref[rows, :].astype(BF16)
                vb = v_ref[rows, :].astype(BF16)
                dkj = jnp.zeros((tk, LANES), F32)
                dvj = jnp.zeros((tk, LANES), F32)
                new_lk, new_e = [], []
                for k in range(2):
                    ls, total, w = _sb_weights(qms[k], kb, diagonal, run_lk[k], strict_after)
                    wb = w.astype(BF16)
                    e = _nt(doms[k], vb) * wb.astype(F32)
                    e_from_here, e_total = _key_suffix(e, from_here, 2)
                    before = deltas[k] - e_from_here - run_e[k]
                    dz = e - jnp.exp(ls) * (e + before)
                    if diagonal is not None:
                        dz = jnp.where(_iota2(dz.shape, 1) < _iota2(dz.shape, 0) + diagonal, dz, 0.0)
                    dz = dz.astype(BF16)
                    m = lane_masks[k]
                    dvj = dvj + jnp.where(key_masks[k], _tn(wb, dob), 0.0)
                    dkj = dkj + jnp.where(key_masks[k], _tn(dz, qb), 0.0)
                    dq = dq + jnp.where(m, _nn(dz, kb), 0.0)
                    new_lk.append(run_lk[k] + total)
                    new_e.append(run_e[k] + e_total)
                dk_acc[rows, :] += dkj
                dv_acc[rows, :] += dvj
                return (dq, *new_lk, *new_e)

            col = jnp.zeros((t, 1), F32)
            dq = _sb_sweep(i, block, (jnp.zeros((t, LANES), F32), col, col, col, col))[0]
            dqkv_ref[qrows, 0:LANES] = (dq * SB_SCALE).astype(BF16)
            return 0

        lax.fori_loop(0, nq, query_block, 0)
        dqkv_ref[:, LANES:2 * LANES] = dk_acc[...].astype(BF16)
        dqkv_ref[:, 2 * LANES:3 * LANES] = dv_acc[...].astype(BF16)

    col = lambda f: pl.BlockSpec((s, LANES), f)
    return pl.pallas_call(
        body, name=name, grid=(2,),
        in_specs=[col(lambda p: (0, 3 * p)), col(lambda p: (0, 3 * p + 1)), col(lambda p: (0, 3 * p + 2)),
                  col(lambda p: (0, p)), col(lambda p: (0, SSD_WIDTH // LANES + p)), ANY] + specs,
        out_specs=pl.BlockSpec((s, 3 * LANES), lambda p: (0, p)),
        out_shape=jax.ShapeDtypeStruct(dproj.shape, BF16),
        input_output_aliases={5: 0},
        scratch_shapes=[pltpu.VMEM((s, LANES), F32), pltpu.VMEM((s, LANES), F32)],
        compiler_params=_params(("parallel",)),
    )(proj, proj, proj, o, dyall, dproj, *ops)


def adamw(w, g, m, v, name):
    b, r, c = w.shape
    tr = max([t for t in range(8, min(r, 512) + 1, 8) if r % t == 0], default=r)

    def body(w_ref, g_ref, m_ref, v_ref, d_ref, nm_ref, nv_ref):
        gv = g_ref[...]
        nm = ADAM_B1 * m_ref[...] + (1.0 - ADAM_B1) * gv
        nv = ADAM_B2 * v_ref[...] + (1.0 - ADAM_B2) * (gv * gv)
        m_hat = nm / (1.0 - ADAM_B1 ** ADAM_STEP)
        v_hat = nv / (1.0 - ADAM_B2 ** ADAM_STEP)
        d_ref[...] = -ADAM_LR * (m_hat / (jnp.sqrt(v_hat) + ADAM_EPS) + ADAM_WD * w_ref[...])
        nm_ref[...] = nm
        nv_ref[...] = nv

    blk = pl.BlockSpec((1, tr, c), lambda i, j: (i, j, 0))
    return pl.pallas_call(
        body, name=name, grid=(b, r // tr),
        in_specs=[blk] * 4, out_specs=[blk] * 3,
        out_shape=[jax.ShapeDtypeStruct(w.shape, F32)] * 3,
        compiler_params=_params(("parallel", "parallel")),
    )(w, g, m, v)


def _position():
    return lax.axis_index("x"), lax.axis_index("y"), lax.axis_index("c")


def _flipped(pos, flip):
    return tuple((1 - p) if f else p for p, f in zip(pos, flip))


FLIP_C = (0, 0, 1)
CHIP_FLIPS = {1: (0, 1, 0), 2: (1, 0, 0), 3: (1, 1, 0)}
SHARD_ROWS = (SHARD_IN, SHARD_OUT, SHARD_FF, SHARD_FF, SHARD_FF)


def _rows(start, size):
    return pl.ds(pl.multiple_of(start, 16), size)


HBM = pl.BlockSpec(memory_space=pltpu.HBM)
SEM = pl.BlockSpec(memory_space=pltpu.SEMAPHORE)
EFFECT = pltpu.SideEffectType.DATAFLOW_SIDE_EFFECTING


def _in_hbm(a):
    return pltpu.with_memory_space_constraint(a, pltpu.HBM)


def _landing(shape, dtype):
    return _in_hbm(lax.empty(shape, dtype))


def _copies(plan, pos, src_refs, land_refs, send_sems, recv_sems):
    return [pltpu.make_async_remote_copy(src_ref=src, dst_ref=dst, send_sem=send_sems.at[k], recv_sem=recv_sems.at[k],
                                         device_id=_flipped(pos, flip), device_id_type=MESH)
            for k, (src, dst, flip) in enumerate(plan(pos, src_refs, land_refs))]


def exchange_start(name, srcs, lands, n, plan, after=None):
    ns, nl = len(srcs), len(lands)
    specs, ops = _after(after)

    def body(*refs):
        src_refs, land_refs = refs[:ns], refs[ns:ns + nl]
        send_sems, recv_sems, token = refs[ns + nl + len(ops)], refs[ns + nl + len(ops) + 1], refs[-1]
        for cp in _copies(plan, _position(), src_refs, land_refs, send_sems, recv_sems):
            cp.start()
        token[...] = jnp.zeros_like(token)

    thru = [pltpu.HBM(a.shape, a.dtype) for a in list(srcs) + list(lands)]
    out = pl.pallas_call(
        body, name=name,
        out_shape=(pltpu.SemaphoreType.DMA((n,)), pltpu.SemaphoreType.DMA((n,)), *thru, jax.ShapeDtypeStruct((8, LANES), F32)),
        in_specs=[HBM] * (ns + nl) + specs,
        out_specs=(SEM, SEM, *([HBM] * (ns + nl)), pl.BlockSpec(memory_space=pltpu.VMEM)),
        input_output_aliases={k: 2 + k for k in range(ns + nl)},
        compiler_params=pltpu.CompilerParams(has_side_effects=EFFECT),
    )(*[_in_hbm(a) for a in srcs], *lands, *ops)
    return out[0], out[1], list(out[2:2 + ns]), list(out[2 + ns:2 + ns + nl]), out[-1]


def exchange_wait(name, started, after, plan):
    send_sems, recv_sems, srcs, lands, _ = started
    ns, nl = len(srcs), len(lands)
    specs, ops = _after(after)

    def body(*refs):
        src_refs, land_refs = refs[:ns], refs[ns:ns + nl]
        send_sems, recv_sems = refs[ns + nl], refs[ns + nl + 1]
        for cp in _copies(plan, _position(), src_refs, land_refs, send_sems, recv_sems):
            cp.wait_send()
            cp.wait_recv()

    out = pl.pallas_call(
        body, name=name,
        out_shape=tuple(pltpu.HBM(a.shape, a.dtype) for a in list(srcs) + list(lands)),
        in_specs=[HBM] * (ns + nl) + [SEM, SEM] + specs,
        out_specs=tuple([HBM] * (ns + nl)),
        input_output_aliases={k: k for k in range(ns + nl)},
        compiler_params=pltpu.CompilerParams(has_side_effects=EFFECT),
    )(*srcs, *lands, send_sems, recv_sems, *ops)
    return list(out[:ns]), list(out[ns:])


def _gather_ici_plan(pos, srcs, lands):
    chip, c = 2 * pos[0] + pos[1], pos[2]
    copies = []
    for src, dst in zip(srcs, lands):
        r = src.shape[0]
        h = r // 2
        for f in (1, 2, 3):
            copies.append((src.at[_rows(c * h, h)], dst.at[_rows(chip * r + c * h, h)], CHIP_FLIPS[f]))
    return copies


def _gather_d2d_plan(pos, srcs, lands):
    chip, c = 2 * pos[0] + pos[1], pos[2]
    copies = []
    for own, dst in zip(srcs, lands):
        r = own.shape[0]
        h = r // 2
        copies.append((own, dst.at[_rows(chip * r, r)], FLIP_C))
        for f in (1, 2, 3):
            at = _rows(lax.bitwise_xor(chip, f) * r + c * h, h)
            copies.append((dst.at[at], dst.at[at], FLIP_C))
    return copies


def gather_ici_start(shards, after=None):
    lands = [_landing((N_CHIPS * a.shape[0], D_MODEL), BF16) for a in shards]
    return exchange_start("gather_ici_start", shards, lands, 3 * len(shards), _gather_ici_plan, after=after)


def gather_d2d_start(shards, fulls, after=None):
    return exchange_start("gather_d2d_start", shards, fulls, 4 * len(shards), _gather_d2d_plan, after=after)


def _reduce_d2d_plan(pos, srcs, lands):
    c = pos[2]
    return [(src.at[:, _rows((1 - c) * (src.shape[1] // 2), src.shape[1] // 2)], dst, FLIP_C) for src, dst in zip(srcs, lands)]


def _reduce_ici_plan(pos, srcs, lands):
    chip = 2 * pos[0] + pos[1]
    return [(src.at[lax.bitwise_xor(chip, f)], dst.at[f - 1], CHIP_FLIPS[f]) for src, dst in zip(srcs, lands) for f in (1, 2, 3)]


def _reduce_swap_plan(pos, srcs, lands):
    c = pos[2]
    copies = []
    for dst in lands:
        h = dst.shape[0] // 2
        at = _rows(c * h, h)
        copies.append((dst.at[at], dst.at[at], FLIP_C))
    return copies


def reduce_d2d_start(grads):
    lands = [_landing((N_CHIPS, g.shape[1] // 2, D_MODEL), BF16) for g in grads]
    return exchange_start("reduce_d2d_start", grads, lands, len(grads), _reduce_d2d_plan)


def reduce_ici_start(chip_sums):
    lands = [_landing((N_CHIPS - 1,) + p.shape[1:], BF16) for p in chip_sums]
    return exchange_start("reduce_ici_start", chip_sums, lands, 3 * len(chip_sums), _reduce_ici_plan)


def reduce_swap_start(mine):
    return exchange_start("reduce_swap_start", [], mine, len(mine), _reduce_swap_plan)


def _by_shape(fn, *lists):
    groups, out = {}, [None] * len(lists[0])
    for k, a in enumerate(lists[0]):
        groups.setdefault(a.shape, []).append(k)
    for idx in groups.values():
        for k, r in zip(idx, fn(*[[l[k] for k in idx] for l in lists])):
            out[k] = r
    return out


def add_halves(ds, recvs, half, name):
    n = len(ds)
    nch, r, c = ds[0].shape
    h = r // 2

    def body(half_ref, *refs):
        for k in range(n):
            refs[2 * n + k][...] = (refs[k][...].astype(F32) + refs[n + k][...].astype(F32)).astype(BF16)

    mine = pl.BlockSpec((1, h, c), lambda j, hf: (j, hf[0], 0))
    whole = pl.BlockSpec((1, h, c), lambda j, hf: (j, 0, 0))
    return pl.pallas_call(
        body, name=name,
        grid_spec=pltpu.PrefetchScalarGridSpec(
            num_scalar_prefetch=1, grid=(nch,), in_specs=[mine] * n + [whole] * n, out_specs=[whole] * n),
        out_shape=[jax.ShapeDtypeStruct(rv.shape, BF16) for rv in recvs],
        compiler_params=_params(("parallel",)),
    )(half, *ds, *recvs)


def add_chips(ps, recvs, chip, name):
    n = len(ps)
    _, r, c = ps[0].shape

    def body(chip_ref, *refs):
        for k in range(n):
            acc = refs[k][0].astype(F32)
            for f in range(N_CHIPS - 1):
                acc = acc + refs[n + k][f].astype(F32)
            refs[2 * n + k][...] = acc

    return pl.pallas_call(
        body, name=name,
        grid_spec=pltpu.PrefetchScalarGridSpec(
            num_scalar_prefetch=1, grid=(1,),
            in_specs=[pl.BlockSpec((1, r, c), lambda i, ch: (ch[0], 0, 0))] * n +
                     [pl.BlockSpec((N_CHIPS - 1, r, c), lambda i, ch: (0, 0, 0))] * n,
            out_specs=[pl.BlockSpec((r, c), lambda i, ch: (ch[1], 0))] * n),
        out_shape=[jax.ShapeDtypeStruct((2 * r, c), F32)] * n,
        compiler_params=_params(("arbitrary",)),
    )(chip, *ps, *recvs)


def adamw_layers(w, gs, m, v, name):
    b, r, c = w.shape
    tr = max([t for t in range(8, min(r, 512) + 1, 8) if r % t == 0], default=r)

    def body(w_ref, m_ref, v_ref, *rest):
        g_refs, (g_ref, d_ref, nm_ref, nv_ref) = rest[:b], rest[b:]
        layer = pl.program_id(0)
        gv = g_refs[0][...]
        for l in range(1, b):
            gv = jnp.where(layer == l, g_refs[l][...], gv)
        nm = ADAM_B1 * m_ref[0] + (1.0 - ADAM_B1) * gv
        nv = ADAM_B2 * v_ref[0] + (1.0 - ADAM_B2) * (gv * gv)
        m_hat = nm / (1.0 - ADAM_B1 ** ADAM_STEP)
        v_hat = nv / (1.0 - ADAM_B2 ** ADAM_STEP)
        g_ref[0] = gv
        d_ref[0] = -ADAM_LR * (m_hat / (jnp.sqrt(v_hat) + ADAM_EPS) + ADAM_WD * w_ref[0])
        nm_ref[0] = nm
        nv_ref[0] = nv

    nr, tc = r // tr, (c if tr < r else _tile(c, 256))
    steps = nr * (c // tc)
    blk = pl.BlockSpec((1, tr, tc), lambda i, j: (i, j % nr, j // nr))
    g_specs = [pl.BlockSpec((tr, tc), lambda i, j, l=l: (jnp.where(i == l, j % nr, jnp.where(i < l, 0, nr - 1)),
                                                         jnp.where(i == l, j // nr, jnp.where(i < l, 0, c // tc - 1))))
               for l in range(b)]
    return pl.pallas_call(
        body, name=name, grid=(b, steps),
        in_specs=[blk] * 3 + g_specs, out_specs=[blk] * 4,
        out_shape=[jax.ShapeDtypeStruct(w.shape, F32)] * 4,
        compiler_params=_params(("arbitrary", "arbitrary")),
    )(w, m, v, *gs)


def small_allreduce(v, name, after=None):
    r, c = v.shape
    specs, ops = _after(after)

    def body(v_ref, *rest):
        o_ref, buf, send_sems, recv_sems = rest[len(ops):]
        pos = _position()
        me = 4 * pos[0] + 2 * pos[1] + pos[2]
        buf[0] = v_ref[...]
        copies = []
        for f in range(1, 8):
            flip = ((f >> 2) & 1, (f >> 1) & 1, f & 1)
            cp = pltpu.make_async_remote_copy(
                src_ref=v_ref, dst_ref=buf.at[f], send_sem=send_sems.at[f - 1], recv_sem=recv_sems.at[f - 1],
                device_id=_flipped(pos, flip), device_id_type=MESH)
            cp.start()
            copies.append(cp)
        for cp in copies:
            cp.wait()
        acc = buf[me]
        for d in range(1, 8):
            acc = acc + buf[lax.bitwise_xor(me, d)]
        o_ref[...] = acc

    return pl.pallas_call(
        body, name=name,
        in_specs=[pl.BlockSpec(memory_space=pltpu.VMEM)] + specs, out_specs=pl.BlockSpec(memory_space=pltpu.VMEM),
        out_shape=jax.ShapeDtypeStruct((r, c), F32),
        scratch_shapes=[pltpu.VMEM((8, r, c), F32), pltpu.SemaphoreType.DMA((7,)), pltpu.SemaphoreType.DMA((7,))],
    )(v, *ops)


def _all_devices_plan(pos, srcs, lands):
    return [(srcs[0], lands[0].at[f], ((f >> 2) & 1, (f >> 1) & 1, f & 1)) for f in range(1, 8)]


def sum_devices(v, gathered, me, name):
    r, c = v.shape

    def body(me_ref, v_ref, g_ref, o_ref):
        own = v_ref[...]
        acc = None
        for d in range(8):
            slot = lax.bitwise_xor(me_ref[0], d)
            term = jnp.where(slot == 0, own, g_ref[slot])
            acc = term if acc is None else acc + term
        o_ref[...] = acc

    return pl.pallas_call(
        body, name=name,
        grid_spec=pltpu.PrefetchScalarGridSpec(
            num_scalar_prefetch=1, grid=(1,),
            in_specs=[pl.BlockSpec((r, c), lambda i, m: (0, 0)), pl.BlockSpec((8, r, c), lambda i, m: (0, 0, 0))],
            out_specs=pl.BlockSpec((r, c), lambda i, m: (0, 0))),
        out_shape=jax.ShapeDtypeStruct((r, c), F32),
        compiler_params=_params(("arbitrary",)),
    )(me, v, gathered)


_IN_SEGMENTS = ((0, 1544, 128), (128, 1800, 128), (256, 2056, 128), (384, 1672, 128), (512, 1928, 128), (640, 2184, 128),
                (OFF_Z, 0, SSD_WIDTH), (OFF_DT, 1536, SSD_HEADS), (OFF_XBC, 512, CONV_DIM), (OFF_P, 2312, POOL_WIDTH))


def _in_column_map():
    m = np.full((D_INP,), -1, np.int64)
    for at, orig, n in _IN_SEGMENTS:
        cols = np.arange(orig, orig + n)
        m[at:at + n] = (cols // COLS_IN) * SHARD_IN + cols % COLS_IN
    return m


def take_rows(a, idx, name):
    dep, r_in, c = a.shape
    blk = 2 * LANES if len(idx) % (2 * LANES) == 0 and r_in % (2 * LANES) == 0 else LANES
    n_out, n_in = len(idx) // blk, r_in // blk
    assert len(idx) % blk == 0 and r_in % blk == 0
    sources = [sorted({int(v) // blk for v in idx[blk * i:blk * (i + 1)] if v >= 0}) for i in range(n_out)]
    width = max(len(s) for s in sources)
    table = np.zeros((n_out, width), np.int32)
    for i, s in enumerate(sources):
        spare = [b for b in range(n_in) if b not in s][:width - len(s)]
        table[i] = s + spare

    def body(tbl_ref, idx_ref, *refs):
        in_refs, o_ref = refs[:width], refs[width]
        i = pl.program_id(1)
        src = idx_ref[...]
        acc = jnp.zeros((blk, c), F32)
        for k in range(width):
            pick = (src == tbl_ref[i, k] * blk + _iota2((blk, blk), 1)).astype(BF16)
            acc = acc + _nn(pick, in_refs[k][0])
        o_ref[0] = acc.astype(BF16)

    return pl.pallas_call(
        body, name=name,
        grid_spec=pltpu.PrefetchScalarGridSpec(
            num_scalar_prefetch=1, grid=(dep, n_out),
            in_specs=[pl.BlockSpec((blk, 1), lambda l, i, t: (i, 0))] +
                     [pl.BlockSpec((1, blk, c), lambda l, i, t, k=k: (l, t[i, k], 0)) for k in range(width)],
            out_specs=pl.BlockSpec((1, blk, c), lambda l, i, t: (l, i, 0))),
        out_shape=jax.ShapeDtypeStruct((dep, len(idx), c), BF16),
        compiler_params=_params(("parallel", "parallel")),
    )(jnp.asarray(table), jnp.asarray(np.asarray(idx, np.int32).reshape(-1, 1)), *([a] * width))


def _in_weight_layout(staged):
    return take_rows(staged, _in_column_map(), "w_in_layout")


def _in_gradient_layout(dwt):
    fwd = _in_column_map()
    inv = np.full((N_CHIPS * SHARD_IN,), -1, np.int64)
    inv[fwd[fwd >= 0]] = np.nonzero(fwd >= 0)[0]
    return take_rows(dwt, inv, "dw_in_layout")


SMALL_NAMES = ("norm1_w", "conv_w", "conv_b", "dt_bias", "a_log", "d_skip", "ssd_norm_w", "pool_w", "pool_b",
               "pool_scale", "norm2_w", "final_norm_w")
SMALL_ROWS = 160


def _small_rows(shape):
    return -(-int(np.prod(shape)) // (8 * D_MODEL)) * 8


def _pack_small(parts):
    rows = []
    for p in parts:
        flat = p.reshape(-1)
        rows.append(jnp.pad(flat, (0, _small_rows(p.shape) * D_MODEL - flat.shape[0])).reshape(-1, D_MODEL))
    used = sum(r.shape[0] for r in rows)
    return jnp.concatenate(rows + [jnp.zeros((SMALL_ROWS - used, D_MODEL), F32)], axis=0)


def _unpack_small(packed, shapes):
    out, at = [], 0
    for shp in shapes:
        n, r = int(np.prod(shp)), _small_rows(shp)
        out.append(packed[at:at + r].reshape(-1)[:n].reshape(shp))
        at += r
    return out


def kernel(x, norm1_w, w_in, conv_w, conv_b, dt_bias, a_log, d_skip, ssd_norm_w, pool_w, pool_b, pool_scale, w_out, norm2_w, w_gate, w_up, w_down, final_norm_w, loss_target, m_norm1_w, m_w_in, m_conv_w, m_conv_b, m_dt_bias, m_a_log, m_d_skip, m_ssd_norm_w, m_pool_w, m_pool_b, m_pool_scale, m_w_out, m_norm2_w, m_w_gate, m_w_up, m_w_down, m_final_norm_w, v_norm1_w, v_w_in, v_conv_w, v_conv_b, v_dt_bias, v_a_log, v_d_skip, v_ssd_norm_w, v_pool_w, v_pool_b, v_pool_scale, v_w_out, v_norm2_w, v_w_gate, v_w_up, v_w_down, v_final_norm_w):
    px, py, pc = _position()
    chip = 2 * px + py
    chip_arr = jnp.reshape(chip, (1,)).astype(jnp.int32)
    half_arr = jnp.reshape(pc, (1,)).astype(jnp.int32)

    def layer_shards(l):
        w_in_t = jnp.pad(jnp.swapaxes(w_in[l], 0, 1).astype(BF16), ((0, SHARD_IN - COLS_IN), (0, 0)))
        return [w_in_t, w_out[l].astype(BF16), jnp.swapaxes(w_gate[l], 0, 1).astype(BF16),
                jnp.swapaxes(w_up[l], 0, 1).astype(BF16), w_down[l].astype(BF16)]

    shards0 = layer_shards(0)
    head = gather_ici_start(shards0[:1])
    over_ici = {}

    def pass_on(l, after):
        own, arrived = exchange_wait("gather_ici_wait", over_ici[l], after, _gather_ici_plan)
        swap = gather_d2d_start(own, arrived)
        tokens = [swap[4]]
        if l + 1 < DEPTH:
            over_ici[l + 1] = gather_ici_start(layer_shards(l + 1), after=swap[4])
            tokens.append(over_ici[l + 1][4])
        return swap, tokens

    def weights_of(swap, after):
        _, (w_in_st, w_out_l, w_gate_t, w_up_t, w_down_l) = exchange_wait("gather_d2d_wait", swap, after, _gather_d2d_plan)
        return _in_weight_layout(w_in_st[None])[0], w_out_l, w_gate_t, w_up_t, w_down_l

    pad_heads = lambda v: jnp.pad(v, ((0, 0), (0, LANES - SSD_HEADS)))[:, None, :]
    dtb, alog = pad_heads(dt_bias), pad_heads(a_log)
    dskip_x = jnp.repeat(d_skip, HEAD_DIM, axis=1)[:, None, :]
    eye = jnp.eye(len(POOL_WINDOWS), dtype=F32)
    wbd = (pool_w[:, :, :, None, :] * eye[None, :, None, :, None]).reshape(DEPTH, POOL_WIDTH, POOL_WIDTH).astype(BF16)
    pool_b2 = pool_b.reshape(DEPTH, 1, POOL_WIDTH)
    cw_cols = lax.dynamic_update_slice(jnp.zeros((DEPTH, CONV_WIDTH, CONV_DIM), F32), conv_w,
                                       (0, 0, chip * (CONV_DIM // N_CHIPS)))
    cw_cols = jnp.where(pc == 0, cw_cols, 0.0)
    cw_rows = (DEPTH * CONV_WIDTH * CONV_DIM) // D_MODEL
    cw_cols = jnp.pad(cw_cols.reshape(cw_rows, D_MODEL), ((0, 8), (0, 0)))

    h = x[0]
    saved, weights = [], []
    own, arrived = exchange_wait("gather_ici_wait", head, [head[4]] + shards0[1:], _gather_ici_plan)
    head = gather_d2d_start(own, arrived)
    over_ici[0] = gather_ici_start(shards0[1:], after=head[4])
    conv_w_f = small_allreduce(cw_cols, "gather_conv_w", after=over_ici[0][4])[:cw_rows].reshape(DEPTH, CONV_WIDTH, CONV_DIM)
    cw8 = jnp.pad(conv_w_f, ((0, 0), (0, 8 - CONV_WIDTH), (0, 0)))
    w_in_f = _in_weight_layout(exchange_wait("gather_d2d_wait", head, [head[4], conv_w_f], _gather_d2d_plan)[1][0][None])[0]
    for l in range(DEPTH):
        if l > 0:
            w_in_f, w_out_f, w_gate_t, w_up_t, w_down_f = weights[l]
        proj = rms_matmul(h, norm1_w[l][None], w_in_f, "in_proj")
        xc = conv_forward(proj, cw8[l], conv_b[l][None], "conv_fwd")
        y_all, ycore, states = ssd_forward(proj, xc, dtb[l], alog[l], dskip_x[l], ssd_norm_w[l][None], "ssd_fwd")
        if l == 0:
            swap, tokens = pass_on(0, y_all)
            y_all, o_sb = sb_forward(proj, y_all, "sb_fwd", after=tokens)
            tokens = None
        else:
            y_all, o_sb = sb_forward(proj, y_all, "sb_fwd")
            swap, tokens = pass_on(l + 1, o_sb) if l + 1 < DEPTH else (None, None)
        y_all = pool_forward(proj, wbd[l], pool_b2[l], pool_scale[l][None], y_all, "pool_fwd", after=tokens)
        if l == 0:
            w_out_f, w_gate_t, w_up_t, w_down_f = exchange_wait("gather_d2d_wait", swap, y_all, _gather_d2d_plan)[1]
            weights.append((w_in_f, w_out_f, w_gate_t, w_up_t, w_down_f))
        x1 = matmul_residual(y_all, w_out_f, h, "out_proj")
        x2, g, u = ffn_forward(x1, norm2_w[l][None], w_gate_t, w_up_t, w_down_f, "ffn_fwd")
        if l == 0:
            swap, tokens = pass_on(1, x2)
            weights.append(weights_of(swap, tokens))
        elif swap is not None:
            weights.append(weights_of(swap, x2))
        saved.append((h, proj, xc, ycore, states, o_sb, y_all, x1, g, u))
        h = x2

    loss_part, dx, dxb, d_final = loss_head(h, final_norm_w[None], loss_target[0], "loss_head")
    loss = lax.psum(loss_part[0, 0], ("x", "y", "c"))

    small = {n: [None] * DEPTH for n in SMALL_NAMES if n != "final_norm_w"}
    chip_half = jnp.concatenate([chip_arr, half_arr])
    reduced = {}
    d2d = ici = early = None

    def add_cores(d2d, after):
        mine, theirs = exchange_wait("reduce_d2d_wait", d2d[1], after, _reduce_d2d_plan)
        return d2d[0], reduce_ici_start(_by_shape(lambda ds, ts: add_halves(ds, ts, half_arr, "reduce_add_halves"), mine, theirs))

    def add_all(ici, after):
        sums, theirs = exchange_wait("reduce_ici_wait", ici[1], after, _reduce_ici_plan)
        return ici[0], reduce_swap_start(_by_shape(lambda ps, ts: add_chips(ps, ts, chip_half, "reduce_add_chips"), sums, theirs))

    def finish(swap, after):
        reduced[swap[0]] = exchange_wait("reduce_swap_wait", swap[1], after, _reduce_swap_plan)[1]

    swaps = []
    for l in reversed(range(DEPTH)):
        xin, proj, xc, ycore, states, o_sb, y_all, x1, g, u = saved[l]
        w_in_f, w_out_f, w_gate_t, w_up_t, w_down_f = weights[l]
        dg, du, act = ffn_backward_act(dxb, g, u, w_down_f, "ffn_bwd_act", after=None if d2d is None else d2d[1][4])
        dx1, dx1b, h2, dn2 = rms_backward([dg, du], [w_gate_t, w_up_t], x1, norm2_w[l][None], dx, "ffn_bwd_norm", 512)
        if d2d is not None:
            ici = add_cores(d2d, dx1b)
        dyall = matmul_nt(dx1b, w_out_f, "out_proj_bwd", after=None if ici is None else ici[1][4])
        dw_down = matmul_tn(act, dxb, "dw_down")
        dw_gate = matmul_tn(dg, h2, "dw_gate")
        dw_up = matmul_tn(du, h2, "dw_up")
        dw_out = matmul_tn(y_all, dx1b, "dw_out")
        late = [dw.reshape(N_CHIPS, r, D_MODEL) for dw, r in zip((dw_out, dw_gate, dw_up, dw_down), SHARD_ROWS[1:])]
        if l == 0:
            early = ("0 late", reduce_d2d_start(late))
        dxc, dproj, dsn, ddsk, ddtb, dalog = ssd_backward(proj, xc, ycore, dyall, states, dtb[l], alog[l],
                                                          dskip_x[l], ssd_norm_w[l][None], "ssd_bwd")
        dproj, dcw, dcb = conv_backward(proj, dxc, cw8[l], conv_b[l][None], dproj, "conv_bwd",
                                        after=None if early is None else early[1][4])
        if early is not None:
            early = add_cores(early, dproj)
        dproj = sb_backward(proj, o_sb, dyall, dproj, "sb_bwd", after=None if early is None else early[1][4])
        dproj, dwbd, dpb, dps = pool_backward(proj, dyall, wbd[l], pool_b2[l], pool_scale[l][None], dproj, "pool_bwd")
        if ici is not None:
            swaps.append(add_all(ici, dproj))
            ici = None
        dx, dxb, h1, dn1 = rms_backward([dproj], [w_in_f], xin, norm1_w[l][None], dx1, "in_proj_bwd", 512,
                                        after=swaps[-1][1][4] if swaps else None)
        dw_in = _in_gradient_layout(matmul_tn(dproj, h1, "dw_in")[None])[0].reshape(N_CHIPS, SHARD_IN, D_MODEL)
        d2d = (l, reduce_d2d_start([dw_in] if l == 0 else [dw_in] + late))
        small["norm1_w"][l] = dn1[0]
        small["conv_w"][l] = dcw[:CONV_WIDTH]
        small["conv_b"][l] = dcb[0]
        small["dt_bias"][l] = ddtb[0, :SSD_HEADS]
        small["a_log"][l] = dalog[0, :SSD_HEADS]
        small["d_skip"][l] = ddsk.reshape(SSD_HEADS, HEAD_DIM).sum(axis=1)
        small["ssd_norm_w"][l] = dsn[0]
        small["pool_w"][l] = jnp.stack([dwbd[64 * k:64 * k + 64, 64 * k:64 * k + 64] for k in range(len(POOL_WINDOWS))])
        small["pool_b"][l] = dpb.reshape(len(POOL_WINDOWS), -1)
        small["pool_scale"][l] = dps[0]
        small["norm2_w"][l] = dn2[0]
    grad_x = dx[None]

    ici = add_cores(d2d, d2d[1][4])
    small_parts = [d_final if n == "final_norm_w" else jnp.stack(small[n]) for n in SMALL_NAMES]
    small_start = exchange_start("reduce_small_start", [_pack_small(small_parts)],
                                 [_landing((8, SMALL_ROWS, D_MODEL), F32)], 7, _all_devices_plan, after=ici[1][4])
    swaps.append(add_all(early, small_start[4]))
    swaps.append(add_all(ici, swaps[-1][1][4]))
    for swap in swaps:
        finish(swap, swaps[-1][1][4])
    (small_own,), (small_all,) = exchange_wait("reduce_small_wait", small_start, reduced[0][0], _all_devices_plan)
    small_sum = sum_devices(small_own, small_all, jnp.reshape(4 * px + 2 * py + pc, (1,)).astype(jnp.int32), "reduce_small_sum")
    reduced[0] = reduced[0] + reduced["0 late"]
    g_big = {n: [reduced[l][k] for l in range(DEPTH)] for k, n in enumerate(("w_in", "w_out", "w_gate", "w_up", "w_down"))}
    g_big["w_in"] = [gl[:COLS_IN] for gl in g_big["w_in"]]
    transposed = ("w_in", "w_gate", "w_up")

    g_small = dict(zip(SMALL_NAMES, _unpack_small(small_sum, [p.shape for p in small_parts])))
    g_small["final_norm_w"] = g_small["final_norm_w"].reshape(final_norm_w.shape)
    g_small["conv_w"] = lax.dynamic_slice_in_dim(g_small["conv_w"], chip * (CONV_DIM // N_CHIPS), CONV_DIM // N_CHIPS, axis=2)

    given = dict(norm1_w=(norm1_w, m_norm1_w, v_norm1_w), w_in=(w_in, m_w_in, v_w_in), conv_w=(conv_w, m_conv_w, v_conv_w),
                 conv_b=(conv_b, m_conv_b, v_conv_b), dt_bias=(dt_bias, m_dt_bias, v_dt_bias), a_log=(a_log, m_a_log, v_a_log),
                 d_skip=(d_skip, m_d_skip, v_d_skip), ssd_norm_w=(ssd_norm_w, m_ssd_norm_w, v_ssd_norm_w),
                 pool_w=(pool_w, m_pool_w, v_pool_w), pool_b=(pool_b, m_pool_b, v_pool_b),
                 pool_scale=(pool_scale, m_pool_scale, v_pool_scale), w_out=(w_out, m_w_out, v_w_out),
                 norm2_w=(norm2_w, m_norm2_w, v_norm2_w), w_gate=(w_gate, m_w_gate, v_w_gate), w_up=(w_up, m_w_up, v_w_up),
                 w_down=(w_down, m_w_down, v_w_down), final_norm_w=(final_norm_w, m_final_norm_w, v_final_norm_w))
    order = ("norm1_w", "w_in", "conv_w", "conv_b", "dt_bias", "a_log", "d_skip", "ssd_norm_w", "pool_w", "pool_b",
             "pool_scale", "w_out", "norm2_w", "w_gate", "w_up", "w_down", "final_norm_w")
    grads = dict(g_small)
    results = {}
    for n in ("w_in", "w_out", "w_gate", "w_up", "w_down"):
        w, m, v = given[n]
        if n in transposed:
            out = adamw_layers(jnp.swapaxes(w, 1, 2), g_big[n], jnp.swapaxes(m, 1, 2), jnp.swapaxes(v, 1, 2), "adamw_" + n)
            out = [jnp.swapaxes(o, 1, 2) for o in out]
        else:
            out = adamw_layers(w, g_big[n], m, v, "adamw_" + n)
        grads[n], results[n] = out[0], tuple(out[1:])
    small_shapes = [given[n][0].shape for n in SMALL_NAMES]
    packed = [_pack_small([given[n][k] for n in SMALL_NAMES])[None] for k in range(3)]
    packed_g = _pack_small([grads[n] for n in SMALL_NAMES])[None]
    small_out = adamw(packed[0], packed_g, packed[1], packed[2], "adamw_small")
    small_out = [_unpack_small(o[0], small_shapes) for o in small_out]
    for i, n in enumerate(SMALL_NAMES):
        results[n] = tuple(small_out[k][i] for k in range(3))

    return (loss, grad_x, *[grads[n] for n in order], *[results[n][0] for n in order],
            *[results[n][1] for n in order], *[results[n][2] for n in order])
```

```python
import numpy as np
import jax
import jax.numpy as jnp
from jax import lax
from jax.experimental import pallas as pl
from jax.experimental.pallas import tpu as pltpu

F32 = jnp.float32
BF16 = jnp.bfloat16
MESH = pl.DeviceIdType.MESH
ANY = pl.BlockSpec(memory_space=pl.ANY)

D_MODEL = 1024
DEPTH = 4
EPS = 1e-6
SSD_WIDTH = 512
SSD_HEADS = 8
HEAD_DIM = 64
D_STATE = 128
CHUNK = 128
CONV_WIDTH = 4
CONV_DIM = 1024
SB_WIDTH = 256
POOL_WIDTH = 256
POOL_WINDOWS = (2, 4, 8, 16)
D_FF = 2816
D_IN = 2568
N_CHIPS = 4
OFF_QKV, OFF_Z, OFF_DT, OFF_XBC, OFF_P = 0, 768, 1280, 1536, 2560
D_INP = 2816
ZDT = 768
SHARD_IN, SHARD_OUT, SHARD_FF = 704, 256, 704
COLS_IN = 642
ADAM_LR, ADAM_B1, ADAM_B2, ADAM_EPS, ADAM_WD, ADAM_STEP = 0.001, 0.9, 0.999, 1e-08, 0.01, 10
LANES = 128
VMEM_LIMIT = 56 * 1024 * 1024


def _params(sem=None):
    return pltpu.CompilerParams(dimension_semantics=sem, vmem_limit_bytes=VMEM_LIMIT)


def _tile(n, cap):
    best = None
    for t in range(LANES, min(n, cap) + 1, LANES):
        if n % t == 0:
            best = t
    assert best is not None, (n, cap)
    return best


def _nt(a, b):
    return lax.dot_general(a, b, (((1,), (1,)), ((), ())), preferred_element_type=F32)


def _tn(a, b):
    return lax.dot_general(a, b, (((0,), (0,)), ((), ())), preferred_element_type=F32)


def _nn(a, b):
    return jnp.dot(a, b, preferred_element_type=F32)


def _split_dot(a, b_exact, terms=3, dot=_nn):
    acc = None
    rest = a
    for _ in range(terms):
        hi = rest.astype(BF16)
        part = dot(hi, b_exact)
        acc = part if acc is None else acc + part
        rest = rest - hi.astype(F32)
    return acc


def _split_dot_left(a_exact, b, terms=3):
    acc = None
    rest = b
    for _ in range(terms):
        hi = rest.astype(BF16)
        part = _nn(a_exact, hi)
        acc = part if acc is None else acc + part
        rest = rest - hi.astype(F32)
    return acc


def _sigmoid(x):
    return 1.0 / (1.0 + jnp.exp(-x))


def _softplus(x):
    return jnp.maximum(x, 0.0) + jnp.log(1.0 + jnp.exp(-jnp.abs(x)))


def _iota2(shape, dim):
    return lax.broadcasted_iota(jnp.int32, shape, dim)


def _after(after):
    ops = [] if after is None else list(after) if isinstance(after, (list, tuple)) else [after]
    return [ANY] * len(ops), ops


def rms_matmul(x, nw, wt, name, after=None):
    s, d = x.shape
    n = wt.shape[0]
    tm, tn = _tile(s, 512), _tile(n, 2816)
    specs, ops = _after(after)

    def body(x_ref, nw_ref, w_ref, *rest):
        o_ref, h_ref = rest[len(ops):]

        @pl.when(pl.program_id(1) == 0)
        def _():
            xv = x_ref[...]
            r = lax.rsqrt(jnp.mean(xv * xv, axis=-1, keepdims=True) + EPS)
            h_ref[...] = (xv * r * nw_ref[...]).astype(BF16)
        o_ref[...] = _nt(h_ref[...], w_ref[...])

    return pl.pallas_call(
        body, name=name, grid=(s // tm, n // tn),
        in_specs=[pl.BlockSpec((tm, d), lambda i, j: (i, 0)), pl.BlockSpec((1, d), lambda i, j: (0, 0)),
                  pl.BlockSpec((tn, d), lambda i, j: (j, 0))] + specs,
        out_specs=pl.BlockSpec((tm, tn), lambda i, j: (i, j)),
        out_shape=jax.ShapeDtypeStruct((s, n), F32),
        scratch_shapes=[pltpu.VMEM((tm, d), BF16)],
        compiler_params=_params(("parallel", "arbitrary")),
    )(x, nw, wt, *ops)


def matmul_residual(a, w, res, name):
    s, k = a.shape
    n = w.shape[1]
    tm, tn = _tile(s, 512), _tile(n, 1024)

    def body(a_ref, w_ref, r_ref, o_ref):
        o_ref[...] = r_ref[...] + _nn(a_ref[...], w_ref[...])

    return pl.pallas_call(
        body, name=name, grid=(s // tm, n // tn),
        in_specs=[pl.BlockSpec((tm, k), lambda i, j: (i, 0)), pl.BlockSpec((k, tn), lambda i, j: (0, j)),
                  pl.BlockSpec((tm, tn), lambda i, j: (i, j))],
        out_specs=pl.BlockSpec((tm, tn), lambda i, j: (i, j)),
        out_shape=jax.ShapeDtypeStruct((s, n), F32),
        compiler_params=_params(("parallel", "parallel")),
    )(a, w, res)


def matmul_nt(a, w, name, out_dtype=F32, after=None):
    s, n = a.shape
    k = w.shape[0]
    tm, tk = _tile(s, 512), _tile(k, 1024)
    specs, ops = _after(after)

    def body(a_ref, w_ref, *rest):
        rest[-1][...] = _nt(a_ref[...], w_ref[...]).astype(out_dtype)

    return pl.pallas_call(
        body, name=name, grid=(s // tm, k // tk),
        in_specs=[pl.BlockSpec((tm, n), lambda i, j: (i, 0)), pl.BlockSpec((tk, n), lambda i, j: (j, 0))] + specs,
        out_specs=pl.BlockSpec((tm, tk), lambda i, j: (i, j)),
        out_shape=jax.ShapeDtypeStruct((s, k), out_dtype),
        compiler_params=_params(("parallel", "parallel")),
    )(a, w, *ops)


def matmul_tn(a, b, name, after=None):
    s, m = a.shape
    n = b.shape[1]
    tm, tn = _tile(m, 512), _tile(n, 1024)

    def body(a_ref, b_ref, *rest):
        rest[-1][...] = _tn(a_ref[...], b_ref[...]).astype(BF16)

    specs, ops = _after(after)
    return pl.pallas_call(
        body, name=name, grid=(m // tm, n // tn),
        in_specs=[pl.BlockSpec((s, tm), lambda i, j: (0, i)), pl.BlockSpec((s, tn), lambda i, j: (0, j))] + specs,
        out_specs=pl.BlockSpec((tm, tn), lambda i, j: (i, j)),
        out_shape=jax.ShapeDtypeStruct((m, n), BF16),
        compiler_params=_params(("parallel", "parallel")),
    )(a, b, *ops)


def ffn_forward(x1, nw, wgt, wut, wd, name):
    s, d = x1.shape
    f = wgt.shape[0]
    tm, tf = _tile(s, 1024), _tile(f, 256)

    def body(x_ref, nw_ref, wg_ref, wu_ref, wd_ref, o_ref, g_ref, u_ref, h_ref, acc_ref):
        j = pl.program_id(1)

        @pl.when(j == 0)
        def _():
            xv = x_ref[...]
            r = lax.rsqrt(jnp.mean(xv * xv, axis=-1, keepdims=True) + EPS)
            h_ref[...] = (xv * r * nw_ref[...]).astype(BF16)
            acc_ref[...] = xv

        h = h_ref[...]
        g = _nt(h, wg_ref[...])
        u = _nt(h, wu_ref[...])
        g_ref[...] = g.astype(BF16)
        u_ref[...] = u.astype(BF16)
        a = (g * _sigmoid(g) * u).astype(BF16)
        acc_ref[...] += _nn(a, wd_ref[...])

        @pl.when(j == pl.num_programs(1) - 1)
        def _():
            o_ref[...] = acc_ref[...]

    wblk = pl.BlockSpec((tf, d), lambda i, j: (j, 0))
    return pl.pallas_call(
        body, name=name, grid=(s // tm, f // tf),
        in_specs=[pl.BlockSpec((tm, d), lambda i, j: (i, 0)), pl.BlockSpec((1, d), lambda i, j: (0, 0)), wblk, wblk, wblk],
        out_specs=[pl.BlockSpec((tm, d), lambda i, j: (i, 0)), pl.BlockSpec((tm, tf), lambda i, j: (i, j)),
                   pl.BlockSpec((tm, tf), lambda i, j: (i, j))],
        out_shape=[jax.ShapeDtypeStruct((s, d), F32), jax.ShapeDtypeStruct((s, f), BF16),
                   jax.ShapeDtypeStruct((s, f), BF16)],
        scratch_shapes=[pltpu.VMEM((tm, d), BF16), pltpu.VMEM((tm, d), F32)],
        compiler_params=_params(("parallel", "arbitrary")),
    )(x1, nw, wgt, wut, wd)


def ffn_backward_act(dx2, g, u, wd, name, after=None):
    s, d = dx2.shape
    f = wd.shape[0]
    tm, tf = _tile(s, 256), _tile(f, 2816)
    specs, ops = _after(after)

    def body(dx_ref, g_ref, u_ref, wd_ref, *rest):
        dg_ref, du_ref, a_ref = rest[len(ops):]
        da = _nt(dx_ref[...], wd_ref[...])
        gv = g_ref[...].astype(F32)
        uv = u_ref[...].astype(F32)
        sg = _sigmoid(gv)
        silu = gv * sg
        dg_ref[...] = (da * uv * (sg * (1.0 + gv * (1.0 - sg)))).astype(BF16)
        du_ref[...] = (da * silu).astype(BF16)
        a_ref[...] = (silu * uv).astype(BF16)

    blk = pl.BlockSpec((tm, tf), lambda i, j: (i, j))
    return pl.pallas_call(
        body, name=name, grid=(s // tm, f // tf),
        in_specs=[pl.BlockSpec((tm, d), lambda i, j: (i, 0)), blk, blk, pl.BlockSpec((tf, d), lambda i, j: (j, 0))] + specs,
        out_specs=[blk, blk, blk],
        out_shape=[jax.ShapeDtypeStruct((s, f), BF16)] * 3,
        compiler_params=_params(("parallel", "parallel")),
    )(dx2, g, u, wd, *ops)


def rms_backward(dzs, wts, x, nw, dres, name, tm, after=None):
    s, d = x.shape
    nz = len(dzs)
    specs, ops = _after(after)

    def body(*refs):
        dz_refs, w_refs = refs[:nz], refs[nz:2 * nz]
        x_ref, nw_ref, dres_ref = refs[2 * nz:2 * nz + 3]
        dx_ref, dxb_ref, h_ref, dnw_ref = refs[2 * nz + 3 + len(ops):]
        dh = _nn(dz_refs[0][...], w_refs[0][...])
        for k in range(1, nz):
            dh = dh + _nn(dz_refs[k][...], w_refs[k][...])
        xv = x_ref[...]
        r = lax.rsqrt(jnp.mean(xv * xv, axis=-1, keepdims=True) + EPS)
        xhat = xv * r
        nwv = nw_ref[...]
        h_ref[...] = (xhat * nwv).astype(BF16)

        @pl.when(pl.program_id(0) == 0)
        def _():
            dnw_ref[...] = jnp.zeros_like(dnw_ref)

        dnw_ref[...] += jnp.sum(dh * xhat, axis=0, keepdims=True)
        gdh = dh * nwv
        dx = dres_ref[...] + r * (gdh - xhat * jnp.mean(gdh * xhat, axis=-1, keepdims=True))
        dx_ref[...] = dx
        dxb_ref[...] = dx.astype(BF16)

    row = pl.BlockSpec((tm, d), lambda i: (i, 0))
    in_specs = [pl.BlockSpec((tm, dz.shape[1]), lambda i: (i, 0)) for dz in dzs]
    in_specs += [pl.BlockSpec(w.shape, lambda i: (0, 0), pipeline_mode=pl.Buffered(1)) for w in wts]
    in_specs += [row, pl.BlockSpec((1, d), lambda i: (0, 0)), row] + specs
    return pl.pallas_call(
        body, name=name, grid=(s // tm,),
        in_specs=in_specs,
        out_specs=[row, row, row, pl.BlockSpec((1, d), lambda i: (0, 0))],
        out_shape=[jax.ShapeDtypeStruct((s, d), F32), jax.ShapeDtypeStruct((s, d), BF16),
                   jax.ShapeDtypeStruct((s, d), BF16), jax.ShapeDtypeStruct((1, d), F32)],
        compiler_params=_params(("arbitrary",)),
    )(*dzs, *wts, x, nw, dres, *ops)


def loss_head(x, nw, target, name):
    s, d = x.shape
    tm = _tile(s, 512)

    def body(x_ref, nw_ref, t_ref, loss_ref, dx_ref, dxb_ref, dnw_ref):
        xv = x_ref[...]
        r = lax.rsqrt(jnp.mean(xv * xv, axis=-1, keepdims=True) + EPS)
        xhat = xv * r
        nwv = nw_ref[...]
        err = xhat * nwv - t_ref[...]

        @pl.when(pl.program_id(0) == 0)
        def _():
            dnw_ref[...] = jnp.zeros_like(dnw_ref)
            loss_ref[...] = jnp.zeros_like(loss_ref)

        part = jnp.sum(jnp.sum(err * err, axis=-1, keepdims=True), axis=0, keepdims=True) * (0.5 / d)
        loss_ref[...] += jnp.broadcast_to(part, loss_ref.shape)
        dout = err * (1.0 / d)
        dnw_ref[...] += jnp.sum(dout * xhat, axis=0, keepdims=True)
        gdh = dout * nwv
        dx = r * (gdh - xhat * jnp.mean(gdh * xhat, axis=-1, keepdims=True))
        dx_ref[...] = dx
        dxb_ref[...] = dx.astype(BF16)

    row = pl.BlockSpec((tm, d), lambda i: (i, 0))
    return pl.pallas_call(
        body, name=name, grid=(s // tm,),
        in_specs=[row, pl.BlockSpec((1, d), lambda i: (0, 0)), row],
        out_specs=[pl.BlockSpec((1, LANES), lambda i: (0, 0)), row, row, pl.BlockSpec((1, d), lambda i: (0, 0))],
        out_shape=[jax.ShapeDtypeStruct((1, LANES), F32), jax.ShapeDtypeStruct((s, d), F32),
                   jax.ShapeDtypeStruct((s, d), BF16), jax.ShapeDtypeStruct((1, d), F32)],
        compiler_params=_params(("arbitrary",)),
    )(x, nw, target)


def _shift_down(x, k):
    return jnp.where(_iota2(x.shape, 0) >= k, pltpu.roll(x, k, axis=0), 0.0)


def _shift_up(x, k):
    s = x.shape[0]
    return jnp.where(_iota2(x.shape, 0) < s - k, pltpu.roll(x, s - k, axis=0), 0.0)


CONV_TILE = 256


def conv_forward(proj, cw, cb, name):
    s = proj.shape[0]
    tn = CONV_TILE
    off = OFF_XBC // tn

    def body(u_ref, w_ref, b_ref, o_ref):
        u = u_ref[...]
        pre = b_ref[...] + w_ref[CONV_WIDTH - 1:CONV_WIDTH, :] * u
        for i in range(CONV_WIDTH - 1):
            pre = pre + w_ref[i:i + 1, :] * _shift_down(u, CONV_WIDTH - 1 - i)
        o_ref[...] = pre * _sigmoid(pre)

    return pl.pallas_call(
        body, name=name, grid=(CONV_DIM // tn,),
        in_specs=[pl.BlockSpec((s, tn), lambda j: (0, off + j)), pl.BlockSpec((8, tn), lambda j: (0, j)),
                  pl.BlockSpec((1, tn), lambda j: (0, j))],
        out_specs=pl.BlockSpec((s, tn), lambda j: (0, j)),
        out_shape=jax.ShapeDtypeStruct((s, CONV_DIM), F32),
        compiler_params=_params(("parallel",)),
    )(proj, cw, cb)


def conv_backward(proj, dxc, cw, cb, dproj, name, after=None):
    s = proj.shape[0]
    tn = CONV_TILE
    off = OFF_XBC // tn

    specs, ops = _after(after)

    def body(u_ref, d_ref, w_ref, b_ref, *rest):
        du_ref, dw_ref, db_ref = rest[-3:]
        u = u_ref[...]
        shifted = [_shift_down(u, CONV_WIDTH - 1 - i) for i in range(CONV_WIDTH - 1)] + [u]
        pre = b_ref[...] + w_ref[CONV_WIDTH - 1:CONV_WIDTH, :] * u
        for i in range(CONV_WIDTH - 1):
            pre = pre + w_ref[i:i + 1, :] * shifted[i]
        sg = _sigmoid(pre)
        dpre = d_ref[...] * (sg * (1.0 + pre * (1.0 - sg)))
        du = w_ref[CONV_WIDTH - 1:CONV_WIDTH, :] * dpre
        for i in range(CONV_WIDTH - 1):
            du = du + w_ref[i:i + 1, :] * _shift_up(dpre, CONV_WIDTH - 1 - i)
        du_ref[...] = du.astype(BF16)
        rows = [jnp.sum(dpre * shifted[i], axis=0, keepdims=True) for i in range(CONV_WIDTH)]
        rows.append(jnp.zeros((8 - CONV_WIDTH, tn), F32))
        dw_ref[...] = jnp.concatenate(rows, axis=0)
        db_ref[...] = jnp.sum(dpre, axis=0, keepdims=True)

    return pl.pallas_call(
        body, name=name, grid=(CONV_DIM // tn,),
        in_specs=[pl.BlockSpec((s, tn), lambda j: (0, off + j)), pl.BlockSpec((s, tn), lambda j: (0, j)),
                  pl.BlockSpec((8, tn), lambda j: (0, j)), pl.BlockSpec((1, tn), lambda j: (0, j)), ANY] + specs,
        out_specs=[pl.BlockSpec((s, tn), lambda j: (0, off + j)), pl.BlockSpec((8, tn), lambda j: (0, j)),
                   pl.BlockSpec((1, tn), lambda j: (0, j))],
        out_shape=[jax.ShapeDtypeStruct(dproj.shape, BF16), jax.ShapeDtypeStruct((8, CONV_DIM), F32),
                   jax.ShapeDtypeStruct((1, CONV_DIM), F32)],
        input_output_aliases={4: 0},
        compiler_params=_params(("parallel",)),
    )(proj, dxc, cw, cb, dproj, *ops)


def _pool_lane_window(shape):
    grp = _iota2(shape, 1) // (POOL_WIDTH // len(POOL_WINDOWS))
    win = jnp.full(shape, POOL_WINDOWS[-1], jnp.int32)
    for gi in range(len(POOL_WINDOWS) - 2, -1, -1):
        win = jnp.where(grp == gi, POOL_WINDOWS[gi], win)
    return grp, win


def _pool_select(grp, sums):
    out = sums[-1]
    for gi in range(len(sums) - 2, -1, -1):
        out = jnp.where(grp == gi, sums[gi], out)
    return out


def _pool_pooled(p):
    grp, win = _pool_lane_window(p.shape)
    inv_count = 1.0 / jnp.minimum(_iota2(p.shape, 0) + 1, win).astype(F32)
    sums, acc, k = [], p, 1
    for _ in POOL_WINDOWS:
        acc = acc + _shift_down(acc, k)
        sums.append(acc)
        k *= 2
    return _pool_select(grp, sums) * inv_count - p, grp, inv_count


def pool_forward(proj, wbd, pb, ps, y_all, name, after=None):
    s = proj.shape[0]
    specs, ops = _after(after)

    def body(p_ref, w_ref, b_ref, s_ref, *rest):
        o_ref = rest[-1]
        pooled, _, _ = _pool_pooled(p_ref[...])
        mixed = _nn(pooled.astype(BF16), w_ref[...]) + b_ref[...]
        o_ref[...] = (mixed * s_ref[...]).astype(BF16)

    vec = pl.BlockSpec((1, POOL_WIDTH), lambda j: (0, 0))
    return pl.pallas_call(
        body, name=name, grid=(1,),
        in_specs=[pl.BlockSpec((s, POOL_WIDTH), lambda j: (0, OFF_P // POOL_WIDTH)),
                  pl.BlockSpec((POOL_WIDTH, POOL_WIDTH), lambda j: (0, 0)), vec, vec, ANY] + specs,
        out_specs=pl.BlockSpec((s, POOL_WIDTH), lambda j: (0, (SSD_WIDTH + SB_WIDTH) // POOL_WIDTH)),
        out_shape=jax.ShapeDtypeStruct(y_all.shape, BF16),
        input_output_aliases={4: 0},
        compiler_params=_params(("arbitrary",)),
    )(proj, wbd, pb, ps, y_all, *ops)


def pool_backward(proj, dyall, wbd, pb, ps, dproj, name):
    s = proj.shape[0]

    def body(p_ref, dy_ref, w_ref, b_ref, s_ref, _, dp_ref, dw_ref, db_ref, ds_ref):
        pooled, grp, inv_count = _pool_pooled(p_ref[...])
        pooled_b = pooled.astype(BF16)
        mixed = _nn(pooled_b, w_ref[...]) + b_ref[...]
        dy = dy_ref[...]
        ds_ref[...] = jnp.sum(dy * mixed, axis=0, keepdims=True)
        dmixed = dy * s_ref[...]
        db_ref[...] = jnp.sum(dmixed, axis=0, keepdims=True)
        dmixed_b = dmixed.astype(BF16)
        dw_ref[...] = _tn(pooled_b, dmixed_b)
        dpooled = _nt(dmixed_b, w_ref[...])
        sums, acc, k = [], dpooled * inv_count, 1
        for _ in POOL_WINDOWS:
            acc = acc + _shift_up(acc, k)
            sums.append(acc)
            k *= 2
        dp_ref[...] = (_pool_select(grp, sums) - dpooled).astype(BF16)

    vec = pl.BlockSpec((1, POOL_WIDTH), lambda j: (0, 0))
    mat = pl.BlockSpec((POOL_WIDTH, POOL_WIDTH), lambda j: (0, 0))
    pcol = pl.BlockSpec((s, POOL_WIDTH), lambda j: (0, OFF_P // POOL_WIDTH))
    return pl.pallas_call(
        body, name=name, grid=(1,),
        in_specs=[pcol, pl.BlockSpec((s, POOL_WIDTH), lambda j: (0, (SSD_WIDTH + SB_WIDTH) // POOL_WIDTH)), mat, vec, vec, ANY],
        out_specs=[pcol, mat, vec, vec],
        out_shape=[jax.ShapeDtypeStruct(dproj.shape, BF16), jax.ShapeDtypeStruct((POOL_WIDTH, POOL_WIDTH), F32),
                   jax.ShapeDtypeStruct((1, POOL_WIDTH), F32), jax.ShapeDtypeStruct((1, POOL_WIDTH), F32)],
        input_output_aliases={5: 0},
        compiler_params=_params(("arbitrary",)),
    )(proj, dyall, wbd, pb, ps, dproj)


N_PAIRS = SSD_HEADS // 2


def _ssd_common(xc, dtraw, dtb, alog):
    c = CHUNK
    dt = _softplus(dtraw + dtb)
    a = -jnp.exp(alog)
    ltri = (_iota2((c, c), 0) >= _iota2((c, c), 1)).astype(BF16)
    acum = _split_dot_left(ltri, dt * a)
    expand = (_iota2((c, SSD_WIDTH), 1) // HEAD_DIM == _iota2((c, SSD_WIDTH), 0)).astype(BF16)
    expand_wide = (_iota2((c, SSD_HEADS * c), 1) // c == _iota2((c, SSD_HEADS * c), 0)).astype(BF16)
    acum_x = _split_dot(acum, expand, 2)
    dt_x = _split_dot(dt, expand, 2)
    alast_x = acum_x[c - 1:c, :]
    return dict(dt=dt, a=a, acum=acum, acum_x=acum_x, dt_x=dt_x, ea_x=jnp.exp(acum_x),
                dte_x=jnp.exp(alast_x - acum_x), eal_x=jnp.exp(alast_x),
                acol=_split_dot(acum, expand_wide, 2), acum_t=acum.T,
                xs=xc[:, :SSD_WIDTH], causal=_iota2((c, c), 0) >= _iota2((c, c), 1),
                left=_iota2((c, c), 1) < HEAD_DIM)


def _ssd_group(xc, g):
    b = xc[:, SSD_WIDTH + D_STATE * g:SSD_WIDTH + D_STATE * (g + 1)]
    cm = xc[:, SSD_WIDTH + 2 * D_STATE + D_STATE * g:SSD_WIDTH + 2 * D_STATE + D_STATE * (g + 1)]
    return b, cm


def _ssd_decay(q, hh):
    col = q["acol"][:, CHUNK * hh:CHUNK * (hh + 1)]
    row = q["acum_t"][hh:hh + 1, :]
    return jnp.where(q["causal"], jnp.exp(jnp.minimum(col - row, 0.0)), 0.0)


def ssd_forward(proj, xc, dtb, alog, dskip_x, nw, name):
    s = xc.shape[0]
    nc = s // CHUNK

    def body(xc_ref, zdt_ref, dtb_ref, alog_ref, dsk_ref, nw_ref, y_ref, yc_ref, st_ref, state):
        @pl.when(pl.program_id(0) == 0)
        def _():
            state[...] = jnp.zeros_like(state)

        xcv = xc_ref[...]
        q = _ssd_common(xcv, zdt_ref[:, SSD_WIDTH:SSD_WIDTH + LANES], dtb_ref[...], alog_ref[...])
        x = q["xs"] * q["dt_x"]
        xb = x.astype(BF16)
        xd = (x * q["dte_x"]).astype(BF16)
        pieces = []
        for g in range(2):
            bg, cg = _ssd_group(xcv, g)
            bgb, cgb = bg.astype(BF16), cg.astype(BF16)
            cb = _nt(cgb, bgb)
            bgt = bg.T.astype(BF16)
            for pr in (2 * g, 2 * g + 1):
                sl = slice(CHUNK * pr, CHUNK * (pr + 1))
                st = state[pr]
                st_ref[0, pr] = st
                yp = _nn(cgb, st.astype(BF16)) * q["ea_x"][:, sl]
                for k, hh in enumerate((2 * pr, 2 * pr + 1)):
                    w = (cb * _ssd_decay(q, hh)).astype(BF16)
                    mask = q["left"] if k == 0 else jnp.logical_not(q["left"])
                    yp = yp + _nn(w, jnp.where(mask, xb[:, sl], jnp.zeros_like(xb[:, sl])))
                state[pr] = st * q["eal_x"][:, sl] + _nn(bgt, xd[:, sl])
                pieces.append(yp)
        y = jnp.concatenate(pieces, axis=1) + q["xs"] * dsk_ref[...]
        yc_ref[...] = y
        zv = zdt_ref[:, :SSD_WIDTH]
        yg = y * (zv * _sigmoid(zv))
        r = lax.rsqrt(jnp.mean(yg * yg, axis=-1, keepdims=True) + EPS)
        y_ref[...] = (yg * r * nw_ref[...]).astype(BF16)

    vec = lambda n: pl.BlockSpec((1, n), lambda c: (0, 0))
    return pl.pallas_call(
        body, name=name, grid=(nc,),
        in_specs=[pl.BlockSpec((CHUNK, CONV_DIM), lambda c: (c, 0)),
                  pl.BlockSpec((CHUNK, ZDT), lambda c: (c, OFF_Z // ZDT)),
                  vec(LANES), vec(LANES), vec(SSD_WIDTH), vec(SSD_WIDTH)],
        out_specs=[pl.BlockSpec((CHUNK, SSD_WIDTH), lambda c: (c, 0)), pl.BlockSpec((CHUNK, SSD_WIDTH), lambda c: (c, 0)),
                   pl.BlockSpec((1, N_PAIRS, D_STATE, CHUNK), lambda c: (c, 0, 0, 0))],
        out_shape=[jax.ShapeDtypeStruct((s, D_MODEL), BF16), jax.ShapeDtypeStruct((s, SSD_WIDTH), F32),
                   jax.ShapeDtypeStruct((nc, N_PAIRS, D_STATE, CHUNK), F32)],
        scratch_shapes=[pltpu.VMEM((N_PAIRS, D_STATE, CHUNK), F32)],
        compiler_params=_params(("arbitrary",)),
    )(xc, proj, dtb, alog, dskip_x, nw)


def ssd_backward(proj, xc, ycore, dyall, states, dtb, alog, dskip_x, nw, name):
    s = xc.shape[0]
    nc = s // CHUNK
    c = CHUNK

    def body(xc_ref, zdt_ref, yc_ref, dy_ref, st_ref, dtb_ref, alog_ref, dsk_ref, nw_ref,
             dxc_ref, dzdt_ref, dnw_ref, ddsk_ref, ddtb_ref, dalog_ref, dstate):
        @pl.when(pl.program_id(0) == 0)
        def _():
            dstate[...] = jnp.zeros_like(dstate)
            dnw_ref[...] = jnp.zeros_like(dnw_ref)
            ddsk_ref[...] = jnp.zeros_like(ddsk_ref)
            ddtb_ref[...] = jnp.zeros_like(ddtb_ref)
            dalog_ref[...] = jnp.zeros_like(dalog_ref)

        xcv = xc_ref[...]
        dtraw = zdt_ref[:, SSD_WIDTH:SSD_WIDTH + LANES]
        q = _ssd_common(xcv, dtraw, dtb_ref[...], alog_ref[...])
        xs = q["xs"]
        x = xs * q["dt_x"]
        zv, yc, dy, nwv = zdt_ref[:, :SSD_WIDTH], yc_ref[...], dy_ref[...], nw_ref[...]
        sgz = _sigmoid(zv)
        siluz = zv * sgz
        yg = yc * siluz
        r = lax.rsqrt(jnp.mean(yg * yg, axis=-1, keepdims=True) + EPS)
        dnw_ref[...] += jnp.sum(dy * yg * r, axis=0, keepdims=True)
        g1 = dy * nwv
        dyg = r * (g1 - yg * (r * r) * jnp.mean(g1 * yg, axis=-1, keepdims=True))
        dyv = dyg * siluz
        dz = (dyg * yc * (sgz * (1.0 + zv * (1.0 - sgz)))).astype(BF16)
        ddsk_ref[...] += jnp.sum(dyv * xs, axis=0, keepdims=True)
        dye = dyv * q["ea_x"]
        dx_parts, yoff_parts, u_parts, v_parts, e_parts = [], [], [], [], []
        db_parts, dc_parts = [], []
        for g in range(2):
            bg, cg = _ssd_group(xcv, g)
            bgb, cgb = bg.astype(BF16), cg.astype(BF16)
            cb = _nt(cgb, bgb)
            cgt = cg.T.astype(BF16)
            dgsum = jnp.zeros((c, c), F32)
            dbg = jnp.zeros((c, D_STATE), F32)
            dcg = jnp.zeros((c, D_STATE), F32)
            for pr in (2 * g, 2 * g + 1):
                sl = slice(c * pr, c * (pr + 1))
                st = st_ref[0, pr]
                dst = dstate[pr]
                stb, dstb = st.astype(BF16), dst.astype(BF16)
                xp = x[:, sl]
                xpb = xp.astype(BF16)
                dyp = dyv[:, sl]
                xdp = xp * q["dte_x"][:, sl]
                yoff_parts.append(_nn(cgb, stb) * q["ea_x"][:, sl])
                rr = _nn(bgb, dstb)
                dxp = rr * q["dte_x"][:, sl]
                u_parts.append(rr * xdp)
                v_parts.append(dst * st * q["eal_x"][:, sl])
                for k, hh in enumerate((2 * pr, 2 * pr + 1)):
                    decay = _ssd_decay(q, hh)
                    w = cb * decay
                    mask = q["left"] if k == 0 else jnp.logical_not(q["left"])
                    dym = jnp.where(mask, dyp, 0.0).astype(BF16)
                    dw = _nt(dym, xpb)
                    dgsum = dgsum + dw * decay
                    e_parts.append(dw * w)
                    dxp = dxp + _nn(w.T.astype(BF16), dym)
                dyeb = dye[:, sl].astype(BF16)
                dcg = dcg + _nt(dyeb, stb)
                dbg = dbg + _nt(xdp.astype(BF16), dstb)
                dstate[pr] = dst * q["eal_x"][:, sl] + _nn(cgt, dyeb)
                dx_parts.append(dxp)
            dcg = dcg + _nn(dgsum.astype(BF16), bgb)
            dbg = dbg + _nn(dgsum.T.astype(BF16), cgb)
            db_parts.append(dbg)
            dc_parts.append(dcg)
        dx = jnp.concatenate(dx_parts, axis=1)
        yoff = jnp.concatenate(yoff_parts, axis=1)
        u = jnp.concatenate(u_parts, axis=1)
        v = jnp.concatenate(v_parts, axis=1)
        reduce_heads = (_iota2((SSD_WIDTH, c), 0) // HEAD_DIM == _iota2((SSD_WIDTH, c), 1)).astype(BF16)
        to_head = (_iota2((SSD_HEADS * c, c), 0) // c == _iota2((SSD_HEADS * c, c), 1)).astype(BF16)
        da = _split_dot(dyv * yoff - u, reduce_heads, 2)
        da = da + _split_dot(jnp.concatenate(e_parts, axis=1), to_head, 2)
        da = da - _split_dot(jnp.concatenate(e_parts, axis=0), to_head, 2, dot=_tn)
        dalast = jnp.sum(_split_dot(u + v, reduce_heads, 2), axis=0, keepdims=True)
        da = da + jnp.where(_iota2((c, c), 0) == c - 1, dalast, 0.0)
        utri = (_iota2((c, c), 1) >= _iota2((c, c), 0)).astype(BF16)
        dda = _split_dot_left(utri, da)
        ddt = dda * q["a"] + _split_dot(dx * xs, reduce_heads, 2)
        dalog_ref[...] += jnp.sum(dda * q["dt"], axis=0, keepdims=True) * q["a"]
        ddtraw = jnp.where(_iota2((c, c), 1) < SSD_HEADS, ddt * _sigmoid(dtraw + dtb_ref[...]), 0.0)
        ddtb_ref[...] += jnp.sum(ddtraw, axis=0, keepdims=True)
        dzdt_ref[...] = jnp.concatenate([dz, ddtraw.astype(BF16), jnp.zeros((c, ZDT - SSD_WIDTH - LANES), BF16)], axis=1)
        dxs = dx * q["dt_x"] + dyv * dsk_ref[...]
        dxc_ref[...] = jnp.concatenate([dxs] + db_parts + dc_parts, axis=1)

    rev = lambda i: nc - 1 - i
    vec = lambda n: pl.BlockSpec((1, n), lambda i: (0, 0))
    wide = pl.BlockSpec((c, SSD_WIDTH), lambda i: (rev(i), 0))
    zdt = pl.BlockSpec((c, ZDT), lambda i: (rev(i), OFF_Z // ZDT))
    return pl.pallas_call(
        body, name=name, grid=(nc,),
        in_specs=[pl.BlockSpec((c, CONV_DIM), lambda i: (rev(i), 0)), zdt, wide, wide,
                  pl.BlockSpec((1, N_PAIRS, D_STATE, c), lambda i: (rev(i), 0, 0, 0)),
                  vec(LANES), vec(LANES), vec(SSD_WIDTH), vec(SSD_WIDTH)],
        out_specs=[pl.BlockSpec((c, CONV_DIM), lambda i: (rev(i), 0)), zdt,
                   vec(SSD_WIDTH), vec(SSD_WIDTH), vec(LANES), vec(LANES)],
        out_shape=[jax.ShapeDtypeStruct((s, CONV_DIM), F32), jax.ShapeDtypeStruct((s, D_INP), BF16),
                   jax.ShapeDtypeStruct((1, SSD_WIDTH), F32),
                   jax.ShapeDtypeStruct((1, SSD_WIDTH), F32), jax.ShapeDtypeStruct((1, LANES), F32),
                   jax.ShapeDtypeStruct((1, LANES), F32)],
        scratch_shapes=[pltpu.VMEM((N_PAIRS, D_STATE, c), F32)],
        compiler_params=_params(("arbitrary",)),
    )(xc, proj, ycore, dyall, states, dtb, alog, dskip_x, nw)


SB_Q, SB_K = 512, 512
SB_T = 256
SB_SCALE = HEAD_DIM ** -0.5


def _key_suffix(x, tri, terms):
    runs = [x[:, SB_T * k:SB_T * (k + 1)] for k in range(SB_K // SB_T)]
    sums = [_split_dot(r, tri, terms) for r in runs]
    later = None
    for k in range(len(runs) - 1, -1, -1):
        if later is not None:
            sums[k] = sums[k] + later
        total = jnp.sum(runs[k], axis=1, keepdims=True)
        later = total if later is None else later + total
    return jnp.concatenate(sums, axis=1), later


def _sb_weights(qm, kb, diagonal, run_lk, strict_after):
    z = _nt(qm, kb)
    nz = -z
    tail = jnp.log(1.0 + jnp.exp(jnp.minimum(z, nz)))
    ls = jnp.minimum(z, 0.0) - tail
    lk = jnp.minimum(nz, 0.0) - tail
    if diagonal is not None:
        valid = _iota2(z.shape, 1) < _iota2(z.shape, 0) + diagonal
        lk = jnp.where(valid, lk, 0.0)
    after, total = _key_suffix(lk, strict_after, 1)
    w = jnp.exp(ls + after + run_lk)
    if diagonal is not None:
        w = jnp.where(valid, w, 0.0)
    return ls, total, w


def _sb_sweep(i, block, init):
    own = (i * SB_Q) // SB_K
    first = block(own, init, i * SB_Q - own * SB_K)
    return lax.fori_loop(1, own + 1, lambda jj, carry: block(own - jj, carry, None), first)


def sb_forward(proj, y_all, name, after=None):
    s = proj.shape[0]
    t, tk = SB_Q, SB_K
    nq = s // t
    specs, ops = _after(after)

    def body(q_ref, k_ref, v_ref, *rest):
        y_ref, o_ref = rest[-2:]
        i = pl.program_id(1)
        left = _iota2((t, LANES), 1) < HEAD_DIM
        left_k = _iota2((tk, LANES), 1) < HEAD_DIM
        qv = q_ref[...] * SB_SCALE
        zero = jnp.zeros_like(qv)
        qms = (jnp.where(left, qv, zero).astype(BF16), jnp.where(left, zero, qv).astype(BF16))
        strict_after = (_iota2((SB_T, SB_T), 0) > _iota2((SB_T, SB_T), 1)).astype(BF16)

        def block(j, carry, diagonal):
            o, runs = carry[0], carry[1:]
            rows = pl.ds(pl.multiple_of(j * tk, tk), tk)
            kb = k_ref[rows, :].astype(BF16)
            vv = v_ref[rows, :]
            new_runs = []
            for k in range(2):
                _, total, w = _sb_weights(qms[k], kb, diagonal, runs[k], strict_after)
                vm = jnp.where(left_k if k == 0 else jnp.logical_not(left_k), vv, 0.0).astype(BF16)
                o = o + _nn(w.astype(BF16), vm)
                new_runs.append(runs[k] + total)
            return (o, *new_runs)

        init = (jnp.zeros((t, LANES), F32), jnp.zeros((t, 1), F32), jnp.zeros((t, 1), F32))
        o = _sb_sweep(i, block, init)[0]
        o_ref[...] = o
        y_ref[...] = o.astype(BF16)

    return pl.pallas_call(
        body, name=name, grid=(2, nq),
        in_specs=[pl.BlockSpec((t, LANES), lambda p, i: (i, 3 * p)),
                  pl.BlockSpec((s, LANES), lambda p, i: (0, 3 * p + 1)),
                  pl.BlockSpec((s, LANES), lambda p, i: (0, 3 * p + 2)), ANY] + specs,
        out_specs=[pl.BlockSpec((t, LANES), lambda p, i: (i, SSD_WIDTH // LANES + p)),
                   pl.BlockSpec((t, LANES), lambda p, i: (i, p))],
        out_shape=[jax.ShapeDtypeStruct(y_all.shape, BF16), jax.ShapeDtypeStruct((s, SB_WIDTH), F32)],
        input_output_aliases={3: 0},
        compiler_params=_params(("parallel", "arbitrary")),
    )(proj, proj, proj, y_all, *ops)


def sb_backward(proj, o, dyall, dproj, name, after=None):
    s = proj.shape[0]
    t, tk = SB_Q, SB_K
    nq = s // t
    specs, ops = _after(after)

    def body(q_ref, k_ref, v_ref, o_ref, do_ref, *rest):
        dqkv_ref, dk_acc, dv_acc = rest[-3:]
        dk_acc[...] = jnp.zeros_like(dk_acc)
        dv_acc[...] = jnp.zeros_like(dv_acc)
        left = _iota2((t, LANES), 1) < HEAD_DIM
        lane_masks = (left, jnp.logical_not(left))
        left_k = _iota2((tk, LANES), 1) < HEAD_DIM
        key_masks = (left_k, jnp.logical_not(left_k))
        strict_after = (_iota2((SB_T, SB_T), 0) > _iota2((SB_T, SB_T), 1)).astype(BF16)
        from_here = (_iota2((SB_T, SB_T), 0) >= _iota2((SB_T, SB_T), 1)).astype(BF16)

        def query_block(i, _):
            qrows = pl.ds(pl.multiple_of(i * t, t), t)
            qv = q_ref[qrows, :] * SB_SCALE
            dov = do_ref[qrows, :]
            zero = jnp.zeros_like(qv)
            qb = qv.astype(BF16)
            dob = dov.astype(BF16)
            prod = dob.astype(F32) * o_ref[qrows, :]
            qms = [jnp.where(m, qv, zero).astype(BF16) for m in lane_masks]
            doms = [jnp.where(m, dov, zero).astype(BF16) for m in lane_masks]
            deltas = [jnp.sum(jnp.where(m, prod, zero), axis=1, keepdims=True) for m in lane_masks]

            def block(j, carry, diagonal):
                dq = carry[0]
                run_lk, run_e = carry[1:3], carry[3:5]
                rows = pl.ds(pl.multiple_of(j * tk, tk), tk)
                kb = k_ref[rows, :].astype(BF16)
                vb = v_ref[rows, :].astype(BF16)
                dkj = jnp.zeros((tk, LANES), F32)
                dvj = jnp.zeros((tk, LANES), F32)
                new_lk, new_e = [], []
                for k in range(2):
                    ls, total, w = _sb_weights(qms[k], kb, diagonal, run_lk[k], strict_after)
                    wb = w.astype(BF16)
                    e = _nt(doms[k], vb) * wb.astype(F32)
                    e_from_here, e_total = _key_suffix(e, from_here, 2)
                    before = deltas[k] - e_from_here - run_e[k]
                    dz = e - jnp.exp(ls) * (e + before)
                    if diagonal is not None:
                        dz = jnp.where(_iota2(dz.shape, 1) < _iota2(dz.shape, 0) + diagonal, dz, 0.0)
                    dz = dz.astype(BF16)
                    m = lane_masks[k]
                    dvj = dvj + jnp.where(key_masks[k], _tn(wb, dob), 0.0)
                    dkj = dkj + jnp.where(key_masks[k], _tn(dz, qb), 0.0)
                    dq = dq + jnp.where(m, _nn(dz, kb), 0.0)
                    new_lk.append(run_lk[k] + total)
                    new_e.append(run_e[k] + e_total)
                dk_acc[rows, :] += dkj
                dv_acc[rows, :] += dvj
                return (dq, *new_lk, *new_e)

            col = jnp.zeros((t, 1), F32)
            dq = _sb_sweep(i, block, (jnp.zeros((t, LANES), F32), col, col, col, col))[0]
            dqkv_ref[qrows, 0:LANES] = (dq * SB_SCALE).astype(BF16)
            return 0

        lax.fori_loop(0, nq, query_block, 0)
        dqkv_ref[:, LANES:2 * LANES] = dk_acc[...].astype(BF16)
        dqkv_ref[:, 2 * LANES:3 * LANES] = dv_acc[...].astype(BF16)

    col = lambda f: pl.BlockSpec((s, LANES), f)
    return pl.pallas_call(
        body, name=name, grid=(2,),
        in_specs=[col(lambda p: (0, 3 * p)), col(lambda p: (0, 3 * p + 1)), col(lambda p: (0, 3 * p + 2)),
                  col(lambda p: (0, p)), col(lambda p: (0, SSD_WIDTH // LANES + p)), ANY] + specs,
        out_specs=pl.BlockSpec((s, 3 * LANES), lambda p: (0, p)),
        out_shape=jax.ShapeDtypeStruct(dproj.shape, BF16),
        input_output_aliases={5: 0},
        scratch_shapes=[pltpu.VMEM((s, LANES), F32), pltpu.VMEM((s, LANES), F32)],
        compiler_params=_params(("parallel",)),
    )(proj, proj, proj, o, dyall, dproj, *ops)


def adamw(w, g, m, v, name):
    b, r, c = w.shape
    tr = max([t for t in range(8, min(r, 512) + 1, 8) if r % t == 0], default=r)

    def body(w_ref, g_ref, m_ref, v_ref, d_ref, nm_ref, nv_ref):
        gv = g_ref[...]
        nm = ADAM_B1 * m_ref[...] + (1.0 - ADAM_B1) * gv
        nv = ADAM_B2 * v_ref[...] + (1.0 - ADAM_B2) * (gv * gv)
        m_hat = nm / (1.0 - ADAM_B1 ** ADAM_STEP)
        v_hat = nv / (1.0 - ADAM_B2 ** ADAM_STEP)
        d_ref[...] = -ADAM_LR * (m_hat / (jnp.sqrt(v_hat) + ADAM_EPS) + ADAM_WD * w_ref[...])
        nm_ref[...] = nm
        nv_ref[...] = nv

    blk = pl.BlockSpec((1, tr, c), lambda i, j: (i, j, 0))
    return pl.pallas_call(
        body, name=name, grid=(b, r // tr),
        in_specs=[blk] * 4, out_specs=[blk] * 3,
        out_shape=[jax.ShapeDtypeStruct(w.shape, F32)] * 3,
        compiler_params=_params(("parallel", "parallel")),
    )(w, g, m, v)


def _position():
    return lax.axis_index("x"), lax.axis_index("y"), lax.axis_index("c")


def _flipped(pos, flip):
    return tuple((1 - p) if f else p for p, f in zip(pos, flip))


FLIP_C = (0, 0, 1)
CHIP_FLIPS = {1: (0, 1, 0), 2: (1, 0, 0), 3: (1, 1, 0)}
SHARD_ROWS = (SHARD_IN, SHARD_OUT, SHARD_FF, SHARD_FF, SHARD_FF)


def _rows(start, size):
    return pl.ds(pl.multiple_of(start, 16), size)


HBM = pl.BlockSpec(memory_space=pltpu.HBM)
SEM = pl.BlockSpec(memory_space=pltpu.SEMAPHORE)
EFFECT = pltpu.SideEffectType.DATAFLOW_SIDE_EFFECTING


def _in_hbm(a):
    return pltpu.with_memory_space_constraint(a, pltpu.HBM)


def _landing(shape, dtype):
    return _in_hbm(lax.empty(shape, dtype))


def _copies(plan, pos, src_refs, land_refs, send_sems, recv_sems):
    return [pltpu.make_async_remote_copy(src_ref=src, dst_ref=dst, send_sem=send_sems.at[k], recv_sem=recv_sems.at[k],
                                         device_id=_flipped(pos, flip), device_id_type=MESH)
            for k, (src, dst, flip) in enumerate(plan(pos, src_refs, land_refs))]


def exchange_start(name, srcs, lands, n, plan, after=None):
    ns, nl = len(srcs), len(lands)
    specs, ops = _after(after)

    def body(*refs):
        src_refs, land_refs = refs[:ns], refs[ns:ns + nl]
        send_sems, recv_sems, token = refs[ns + nl + len(ops)], refs[ns + nl + len(ops) + 1], refs[-1]
        for cp in _copies(plan, _position(), src_refs, land_refs, send_sems, recv_sems):
            cp.start()
        token[...] = jnp.zeros_like(token)

    thru = [pltpu.HBM(a.shape, a.dtype) for a in list(srcs) + list(lands)]
    out = pl.pallas_call(
        body, name=name,
        out_shape=(pltpu.SemaphoreType.DMA((n,)), pltpu.SemaphoreType.DMA((n,)), *thru, jax.ShapeDtypeStruct((8, LANES), F32)),
        in_specs=[HBM] * (ns + nl) + specs,
        out_specs=(SEM, SEM, *([HBM] * (ns + nl)), pl.BlockSpec(memory_space=pltpu.VMEM)),
        input_output_aliases={k: 2 + k for k in range(ns + nl)},
        compiler_params=pltpu.CompilerParams(has_side_effects=EFFECT),
    )(*[_in_hbm(a) for a in srcs], *lands, *ops)
    return out[0], out[1], list(out[2:2 + ns]), list(out[2 + ns:2 + ns + nl]), out[-1]


def exchange_wait(name, started, after, plan):
    send_sems, recv_sems, srcs, lands, _ = started
    ns, nl = len(srcs), len(lands)
    specs, ops = _after(after)

    def body(*refs):
        src_refs, land_refs = refs[:ns], refs[ns:ns + nl]
        send_sems, recv_sems = refs[ns + nl], refs[ns + nl + 1]
        for cp in _copies(plan, _position(), src_refs, land_refs, send_sems, recv_sems):
            cp.wait_send()
            cp.wait_recv()

    out = pl.pallas_call(
        body, name=name,
        out_shape=tuple(pltpu.HBM(a.shape, a.dtype) for a in list(srcs) + list(lands)),
        in_specs=[HBM] * (ns + nl) + [SEM, SEM] + specs,
        out_specs=tuple([HBM] * (ns + nl)),
        input_output_aliases={k: k for k in range(ns + nl)},
        compiler_params=pltpu.CompilerParams(has_side_effects=EFFECT),
    )(*srcs, *lands, send_sems, recv_sems, *ops)
    return list(out[:ns]), list(out[ns:])


def _gather_ici_plan(pos, srcs, lands):
    chip, c = 2 * pos[0] + pos[1], pos[2]
    copies = []
    for src, dst in zip(srcs, lands):
        r = src.shape[0]
        h = r // 2
        for f in (1, 2, 3):
            copies.append((src.at[_rows(c * h, h)], dst.at[_rows(chip * r + c * h, h)], CHIP_FLIPS[f]))
    return copies


def _gather_d2d_plan(pos, srcs, lands):
    chip, c = 2 * pos[0] + pos[1], pos[2]
    copies = []
    for own, dst in zip(srcs, lands):
        r = own.shape[0]
        h = r // 2
        copies.append((own, dst.at[_rows(chip * r, r)], FLIP_C))
        for f in (1, 2, 3):
            at = _rows(lax.bitwise_xor(chip, f) * r + c * h, h)
            copies.append((dst.at[at], dst.at[at], FLIP_C))
    return copies


def gather_ici_start(shards, after=None):
    lands = [_landing((N_CHIPS * a.shape[0], D_MODEL), BF16) for a in shards]
    return exchange_start("gather_ici_start", shards, lands, 3 * len(shards), _gather_ici_plan, after=after)


def gather_d2d_start(shards, fulls, after=None):
    return exchange_start("gather_d2d_start", shards, fulls, 4 * len(shards), _gather_d2d_plan, after=after)


def _reduce_d2d_plan(pos, srcs, lands):
    c = pos[2]
    return [(src.at[:, _rows((1 - c) * (src.shape[1] // 2), src.shape[1] // 2)], dst, FLIP_C) for src, dst in zip(srcs, lands)]


def _reduce_ici_plan(pos, srcs, lands):
    chip = 2 * pos[0] + pos[1]
    return [(src.at[lax.bitwise_xor(chip, f)], dst.at[f - 1], CHIP_FLIPS[f]) for src, dst in zip(srcs, lands) for f in (1, 2, 3)]


def _reduce_swap_plan(pos, srcs, lands):
    c = pos[2]
    copies = []
    for dst in lands:
        h = dst.shape[0] // 2
        at = _rows(c * h, h)
        copies.append((dst.at[at], dst.at[at], FLIP_C))
    return copies


def reduce_d2d_start(grads):
    lands = [_landing((N_CHIPS, g.shape[1] // 2, D_MODEL), BF16) for g in grads]
    return exchange_start("reduce_d2d_start", grads, lands, len(grads), _reduce_d2d_plan)


def reduce_ici_start(chip_sums):
    lands = [_landing((N_CHIPS - 1,) + p.shape[1:], BF16) for p in chip_sums]
    return exchange_start("reduce_ici_start", chip_sums, lands, 3 * len(chip_sums), _reduce_ici_plan)


def reduce_swap_start(mine):
    return exchange_start("reduce_swap_start", [], mine, len(mine), _reduce_swap_plan)


def _by_shape(fn, *lists):
    groups, out = {}, [None] * len(lists[0])
    for k, a in enumerate(lists[0]):
        groups.setdefault(a.shape, []).append(k)
    for idx in groups.values():
        for k, r in zip(idx, fn(*[[l[k] for k in idx] for l in lists])):
            out[k] = r
    return out


def add_halves(ds, recvs, half, name):
    n = len(ds)
    nch, r, c = ds[0].shape
    h = r // 2

    def body(half_ref, *refs):
        for k in range(n):
            refs[2 * n + k][...] = (refs[k][...].astype(F32) + refs[n + k][...].astype(F32)).astype(BF16)

    mine = pl.BlockSpec((1, h, c), lambda j, hf: (j, hf[0], 0))
    whole = pl.BlockSpec((1, h, c), lambda j, hf: (j, 0, 0))
    return pl.pallas_call(
        body, name=name,
        grid_spec=pltpu.PrefetchScalarGridSpec(
            num_scalar_prefetch=1, grid=(nch,), in_specs=[mine] * n + [whole] * n, out_specs=[whole] * n),
        out_shape=[jax.ShapeDtypeStruct(rv.shape, BF16) for rv in recvs],
        compiler_params=_params(("parallel",)),
    )(half, *ds, *recvs)


def add_chips(ps, recvs, chip, name):
    n = len(ps)
    _, r, c = ps[0].shape

    def body(chip_ref, *refs):
        for k in range(n):
            acc = refs[k][0].astype(F32)
            for f in range(N_CHIPS - 1):
                acc = acc + refs[n + k][f].astype(F32)
            refs[2 * n + k][...] = acc

    return pl.pallas_call(
        body, name=name,
        grid_spec=pltpu.PrefetchScalarGridSpec(
            num_scalar_prefetch=1, grid=(1,),
            in_specs=[pl.BlockSpec((1, r, c), lambda i, ch: (ch[0], 0, 0))] * n +
                     [pl.BlockSpec((N_CHIPS - 1, r, c), lambda i, ch: (0, 0, 0))] * n,
            out_specs=[pl.BlockSpec((r, c), lambda i, ch: (ch[1], 0))] * n),
        out_shape=[jax.ShapeDtypeStruct((2 * r, c), F32)] * n,
        compiler_params=_params(("arbitrary",)),
    )(chip, *ps, *recvs)


def adamw_layers(w, gs, m, v, name):
    b, r, c = w.shape
    tr = max([t for t in range(8, min(r, 512) + 1, 8) if r % t == 0], default=r)

    def body(w_ref, m_ref, v_ref, *rest):
        g_refs, (g_ref, d_ref, nm_ref, nv_ref) = rest[:b], rest[b:]
        layer = pl.program_id(0)
        gv = g_refs[0][...]
        for l in range(1, b):
            gv = jnp.where(layer == l, g_refs[l][...], gv)
        nm = ADAM_B1 * m_ref[0] + (1.0 - ADAM_B1) * gv
        nv = ADAM_B2 * v_ref[0] + (1.0 - ADAM_B2) * (gv * gv)
        m_hat = nm / (1.0 - ADAM_B1 ** ADAM_STEP)
        v_hat = nv / (1.0 - ADAM_B2 ** ADAM_STEP)
        g_ref[0] = gv
        d_ref[0] = -ADAM_LR * (m_hat / (jnp.sqrt(v_hat) + ADAM_EPS) + ADAM_WD * w_ref[0])
        nm_ref[0] = nm
        nv_ref[0] = nv

    nr, tc = r // tr, (c if tr < r else _tile(c, 256))
    steps = nr * (c // tc)
    blk = pl.BlockSpec((1, tr, tc), lambda i, j: (i, j % nr, j // nr))
    g_specs = [pl.BlockSpec((tr, tc), lambda i, j, l=l: (jnp.where(i == l, j % nr, jnp.where(i < l, 0, nr - 1)),
                                                         jnp.where(i == l, j // nr, jnp.where(i < l, 0, c // tc - 1))))
               for l in range(b)]
    return pl.pallas_call(
        body, name=name, grid=(b, steps),
        in_specs=[blk] * 3 + g_specs, out_specs=[blk] * 4,
        out_shape=[jax.ShapeDtypeStruct(w.shape, F32)] * 4,
        compiler_params=_params(("arbitrary", "arbitrary")),
    )(w, m, v, *gs)


def _all_devices_plan(pos, srcs, lands):
    return [(srcs[0], lands[0].at[f], ((f >> 2) & 1, (f >> 1) & 1, f & 1)) for f in range(1, 8)]


def sum_devices(v, gathered, me, name):
    r, c = v.shape

    def body(me_ref, v_ref, g_ref, o_ref):
        own = v_ref[...]
        acc = None
        for d in range(8):
            slot = lax.bitwise_xor(me_ref[0], d)
            term = jnp.where(slot == 0, own, g_ref[slot])
            acc = term if acc is None else acc + term
        o_ref[...] = acc

    return pl.pallas_call(
        body, name=name,
        grid_spec=pltpu.PrefetchScalarGridSpec(
            num_scalar_prefetch=1, grid=(1,),
            in_specs=[pl.BlockSpec((r, c), lambda i, m: (0, 0)), pl.BlockSpec((8, r, c), lambda i, m: (0, 0, 0))],
            out_specs=pl.BlockSpec((r, c), lambda i, m: (0, 0))),
        out_shape=jax.ShapeDtypeStruct((r, c), F32),
        compiler_params=_params(("arbitrary",)),
    )(me, v, gathered)


_IN_SEGMENTS = ((0, 1544, 128), (128, 1800, 128), (256, 2056, 128), (384, 1672, 128), (512, 1928, 128), (640, 2184, 128),
                (OFF_Z, 0, SSD_WIDTH), (OFF_DT, 1536, SSD_HEADS), (OFF_XBC, 512, CONV_DIM), (OFF_P, 2312, POOL_WIDTH))


def _in_column_map():
    m = np.full((D_INP,), -1, np.int64)
    for at, orig, n in _IN_SEGMENTS:
        cols = np.arange(orig, orig + n)
        m[at:at + n] = (cols // COLS_IN) * SHARD_IN + cols % COLS_IN
    return m


def take_rows(a, idx, name):
    dep, r_in, c = a.shape
    blk = 2 * LANES if len(idx) % (2 * LANES) == 0 and r_in % (2 * LANES) == 0 else LANES
    n_out, n_in = len(idx) // blk, r_in // blk
    assert len(idx) % blk == 0 and r_in % blk == 0
    sources = [sorted({int(v) // blk for v in idx[blk * i:blk * (i + 1)] if v >= 0}) for i in range(n_out)]
    width = max(len(s) for s in sources)
    table = np.zeros((n_out, width), np.int32)
    for i, s in enumerate(sources):
        spare = [b for b in range(n_in) if b not in s][:width - len(s)]
        table[i] = s + spare

    def body(tbl_ref, idx_ref, *refs):
        in_refs, o_ref = refs[:width], refs[width]
        i = pl.program_id(1)
        src = idx_ref[...]
        acc = jnp.zeros((blk, c), F32)
        for k in range(width):
            pick = (src == tbl_ref[i, k] * blk + _iota2((blk, blk), 1)).astype(BF16)
            acc = acc + _nn(pick, in_refs[k][0])
        o_ref[0] = acc.astype(BF16)

    return pl.pallas_call(
        body, name=name,
        grid_spec=pltpu.PrefetchScalarGridSpec(
            num_scalar_prefetch=1, grid=(dep, n_out),
            in_specs=[pl.BlockSpec((blk, 1), lambda l, i, t: (i, 0))] +
                     [pl.BlockSpec((1, blk, c), lambda l, i, t, k=k: (l, t[i, k], 0)) for k in range(width)],
            out_specs=pl.BlockSpec((1, blk, c), lambda l, i, t: (l, i, 0))),
        out_shape=jax.ShapeDtypeStruct((dep, len(idx), c), BF16),
        compiler_params=_params(("parallel", "parallel")),
    )(jnp.asarray(table), jnp.asarray(np.asarray(idx, np.int32).reshape(-1, 1)), *([a] * width))


def _in_weight_layout(staged):
    return take_rows(staged, _in_column_map(), "w_in_layout")


def _in_gradient_layout(dwt):
    fwd = _in_column_map()
    inv = np.full((N_CHIPS * SHARD_IN,), -1, np.int64)
    inv[fwd[fwd >= 0]] = np.nonzero(fwd >= 0)[0]
    return take_rows(dwt, inv, "dw_in_layout")


SMALL_NAMES = ("norm1_w", "conv_w", "conv_b", "dt_bias", "a_log", "d_skip", "ssd_norm_w", "pool_w", "pool_b",
               "pool_scale", "norm2_w", "final_norm_w")
SMALL_ROWS = 160


def _small_rows(shape):
    return -(-int(np.prod(shape)) // (8 * D_MODEL)) * 8


def _pack_small(parts):
    rows = []
    for p in parts:
        flat = p.reshape(-1)
        rows.append(jnp.pad(flat, (0, _small_rows(p.shape) * D_MODEL - flat.shape[0])).reshape(-1, D_MODEL))
    used = sum(r.shape[0] for r in rows)
    return jnp.concatenate(rows + [jnp.zeros((SMALL_ROWS - used, D_MODEL), F32)], axis=0)


def _unpack_small(packed, shapes):
    out, at = [], 0
    for shp in shapes:
        n, r = int(np.prod(shp)), _small_rows(shp)
        out.append(packed[at:at + r].reshape(-1)[:n].reshape(shp))
        at += r
    return out


def kernel(x, norm1_w, w_in, conv_w, conv_b, dt_bias, a_log, d_skip, ssd_norm_w, pool_w, pool_b, pool_scale, w_out, norm2_w, w_gate, w_up, w_down, final_norm_w, loss_target, m_norm1_w, m_w_in, m_conv_w, m_conv_b, m_dt_bias, m_a_log, m_d_skip, m_ssd_norm_w, m_pool_w, m_pool_b, m_pool_scale, m_w_out, m_norm2_w, m_w_gate, m_w_up, m_w_down, m_final_norm_w, v_norm1_w, v_w_in, v_conv_w, v_conv_b, v_dt_bias, v_a_log, v_d_skip, v_ssd_norm_w, v_pool_w, v_pool_b, v_pool_scale, v_w_out, v_norm2_w, v_w_gate, v_w_up, v_w_down, v_final_norm_w):
    px, py, pc = _position()
    chip = 2 * px + py
    chip_arr = jnp.reshape(chip, (1,)).astype(jnp.int32)
    half_arr = jnp.reshape(pc, (1,)).astype(jnp.int32)

    def layer_shards(l):
        w_in_t = jnp.pad(jnp.swapaxes(w_in[l], 0, 1).astype(BF16), ((0, SHARD_IN - COLS_IN), (0, 0)))
        return [w_in_t, w_out[l].astype(BF16), jnp.swapaxes(w_gate[l], 0, 1).astype(BF16),
                jnp.swapaxes(w_up[l], 0, 1).astype(BF16), w_down[l].astype(BF16)]

    shards0 = layer_shards(0)
    head = gather_ici_start(shards0[:1])
    over_ici = {}

    def pass_on(l, after):
        own, arrived = exchange_wait("gather_ici_wait", over_ici[l], after, _gather_ici_plan)
        swap = gather_d2d_start(own, arrived)
        tokens = [swap[4]]
        if l + 1 < DEPTH:
            over_ici[l + 1] = gather_ici_start(layer_shards(l + 1), after=swap[4])
            tokens.append(over_ici[l + 1][4])
        return swap, tokens

    def weights_of(swap, after):
        _, (w_in_st, w_out_l, w_gate_t, w_up_t, w_down_l) = exchange_wait("gather_d2d_wait", swap, after, _gather_d2d_plan)
        return _in_weight_layout(w_in_st[None])[0], w_out_l, w_gate_t, w_up_t, w_down_l

    pad_heads = lambda v: jnp.pad(v, ((0, 0), (0, LANES - SSD_HEADS)))[:, None, :]
    dtb, alog = pad_heads(dt_bias), pad_heads(a_log)
    dskip_x = jnp.repeat(d_skip, HEAD_DIM, axis=1)[:, None, :]
    eye = jnp.eye(len(POOL_WINDOWS), dtype=F32)
    wbd = (pool_w[:, :, :, None, :] * eye[None, :, None, :, None]).reshape(DEPTH, POOL_WIDTH, POOL_WIDTH).astype(BF16)
    pool_b2 = pool_b.reshape(DEPTH, 1, POOL_WIDTH)
    cw_cols = lax.dynamic_update_slice(jnp.zeros((DEPTH, CONV_WIDTH, CONV_DIM), F32), conv_w,
                                       (0, 0, chip * (CONV_DIM // N_CHIPS)))
    cw_cols = jnp.where(pc == 0, cw_cols, 0.0)
    cw_rows = (DEPTH * CONV_WIDTH * CONV_DIM) // D_MODEL
    me_arr = jnp.reshape(4 * px + 2 * py + pc, (1,)).astype(jnp.int32)
    cw_cols = jnp.pad(cw_cols.reshape(cw_rows, D_MODEL), ((0, 8), (0, 0)))
    cw_start = exchange_start("gather_conv_w_start", [cw_cols], [_landing((8,) + cw_cols.shape, F32)], 7,
                              _all_devices_plan, after=head[4])

    h = x[0]
    saved, weights = [], []
    own, arrived = exchange_wait("gather_ici_wait", head, [head[4], cw_start[4]] + shards0[1:], _gather_ici_plan)
    head = gather_d2d_start(own, arrived)
    over_ici[0] = gather_ici_start(shards0[1:], after=head[4])
    w_in_f = _in_weight_layout(exchange_wait("gather_d2d_wait", head, [head[4], over_ici[0][4]], _gather_d2d_plan)[1][0][None])[0]
    for l in range(DEPTH):
        if l > 0:
            w_in_f, w_out_f, w_gate_t, w_up_t, w_down_f = weights[l]
        proj = rms_matmul(h, norm1_w[l][None], w_in_f, "in_proj")
        if l == 0:
            (cw_own,), (cw_all,) = exchange_wait("gather_conv_w_wait", cw_start, proj, _all_devices_plan)
            conv_w_f = sum_devices(cw_own, cw_all, me_arr, "gather_conv_w_sum")[:cw_rows].reshape(DEPTH, CONV_WIDTH, CONV_DIM)
            cw8 = jnp.pad(conv_w_f, ((0, 0), (0, 8 - CONV_WIDTH), (0, 0)))
        xc = conv_forward(proj, cw8[l], conv_b[l][None], "conv_fwd")
        y_all, ycore, states = ssd_forward(proj, xc, dtb[l], alog[l], dskip_x[l], ssd_norm_w[l][None], "ssd_fwd")
        if l == 0:
            swap, tokens = pass_on(0, y_all)
            y_all, o_sb = sb_forward(proj, y_all, "sb_fwd", after=tokens)
            tokens = None
        else:
            y_all, o_sb = sb_forward(proj, y_all, "sb_fwd")
            swap, tokens = pass_on(l + 1, o_sb) if l + 1 < DEPTH else (None, None)
        y_all = pool_forward(proj, wbd[l], pool_b2[l], pool_scale[l][None], y_all, "pool_fwd", after=tokens)
        if l == 0:
            w_out_f, w_gate_t, w_up_t, w_down_f = exchange_wait("gather_d2d_wait", swap, y_all, _gather_d2d_plan)[1]
            weights.append((w_in_f, w_out_f, w_gate_t, w_up_t, w_down_f))
        x1 = matmul_residual(y_all, w_out_f, h, "out_proj")
        x2, g, u = ffn_forward(x1, norm2_w[l][None], w_gate_t, w_up_t, w_down_f, "ffn_fwd")
        if l == 0:
            swap, tokens = pass_on(1, x2)
            weights.append(weights_of(swap, tokens))
        elif swap is not None:
            weights.append(weights_of(swap, x2))
        saved.append((h, proj, xc, ycore, states, o_sb, y_all, x1, g, u))
        h = x2

    loss_part, dx, dxb, d_final = loss_head(h, final_norm_w[None], loss_target[0], "loss_head")
    loss = lax.psum(loss_part[0, 0], ("x", "y", "c"))

    small = {n: [None] * DEPTH for n in SMALL_NAMES if n != "final_norm_w"}
    chip_half = jnp.concatenate([chip_arr, half_arr])
    reduced = {}
    d2d = ici = early = None

    def add_cores(d2d, after):
        mine, theirs = exchange_wait("reduce_d2d_wait", d2d[1], after, _reduce_d2d_plan)
        return d2d[0], reduce_ici_start(_by_shape(lambda ds, ts: add_halves(ds, ts, half_arr, "reduce_add_halves"), mine, theirs))

    def add_all(ici, after):
        sums, theirs = exchange_wait("reduce_ici_wait", ici[1], after, _reduce_ici_plan)
        return ici[0], reduce_swap_start(_by_shape(lambda ps, ts: add_chips(ps, ts, chip_half, "reduce_add_chips"), sums, theirs))

    def finish(swap, after):
        reduced[swap[0]] = exchange_wait("reduce_swap_wait", swap[1], after, _reduce_swap_plan)[1]

    swaps = []
    for l in reversed(range(DEPTH)):
        xin, proj, xc, ycore, states, o_sb, y_all, x1, g, u = saved[l]
        w_in_f, w_out_f, w_gate_t, w_up_t, w_down_f = weights[l]
        dg, du, act = ffn_backward_act(dxb, g, u, w_down_f, "ffn_bwd_act", after=None if d2d is None else d2d[1][4])
        dx1, dx1b, h2, dn2 = rms_backward([dg, du], [w_gate_t, w_up_t], x1, norm2_w[l][None], dx, "ffn_bwd_norm", 512)
        if d2d is not None:
            ici = add_cores(d2d, dx1b)
        dyall = matmul_nt(dx1b, w_out_f, "out_proj_bwd", after=None if ici is None else ici[1][4])
        dw_down = matmul_tn(act, dxb, "dw_down")
        dw_gate = matmul_tn(dg, h2, "dw_gate")
        dw_up = matmul_tn(du, h2, "dw_up")
        dw_out = matmul_tn(y_all, dx1b, "dw_out")
        late = [dw.reshape(N_CHIPS, r, D_MODEL) for dw, r in zip((dw_out, dw_gate, dw_up, dw_down), SHARD_ROWS[1:])]
        if l == 0:
            early = ("0 late", reduce_d2d_start(late))
        dxc, dproj, dsn, ddsk, ddtb, dalog = ssd_backward(proj, xc, ycore, dyall, states, dtb[l], alog[l],
                                                          dskip_x[l], ssd_norm_w[l][None], "ssd_bwd")
        dproj, dcw, dcb = conv_backward(proj, dxc, cw8[l], conv_b[l][None], dproj, "conv_bwd",
                                        after=None if early is None else early[1][4])
        if early is not None:
            early = add_cores(early, dproj)
        dproj = sb_backward(proj, o_sb, dyall, dproj, "sb_bwd", after=None if early is None else early[1][4])
        dproj, dwbd, dpb, dps = pool_backward(proj, dyall, wbd[l], pool_b2[l], pool_scale[l][None], dproj, "pool_bwd")
        if ici is not None:
            swaps.append(add_all(ici, dproj))
            ici = None
        dx, dxb, h1, dn1 = rms_backward([dproj], [w_in_f], xin, norm1_w[l][None], dx1, "in_proj_bwd", 512,
                                        after=swaps[-1][1][4] if swaps else None)
        dw_in = _in_gradient_layout(matmul_tn(dproj, h1, "dw_in")[None])[0].reshape(N_CHIPS, SHARD_IN, D_MODEL)
        d2d = (l, reduce_d2d_start([dw_in] if l == 0 else [dw_in] + late))
        small["norm1_w"][l] = dn1[0]
        small["conv_w"][l] = dcw[:CONV_WIDTH]
        small["conv_b"][l] = dcb[0]
        small["dt_bias"][l] = ddtb[0, :SSD_HEADS]
        small["a_log"][l] = dalog[0, :SSD_HEADS]
        small["d_skip"][l] = ddsk.reshape(SSD_HEADS, HEAD_DIM).sum(axis=1)
        small["ssd_norm_w"][l] = dsn[0]
        small["pool_w"][l] = jnp.stack([dwbd[64 * k:64 * k + 64, 64 * k:64 * k + 64] for k in range(len(POOL_WINDOWS))])
        small["pool_b"][l] = dpb.reshape(len(POOL_WINDOWS), -1)
        small["pool_scale"][l] = dps[0]
        small["norm2_w"][l] = dn2[0]
    grad_x = dx[None]

    ici = add_cores(d2d, d2d[1][4])
    small_parts = [d_final if n == "final_norm_w" else jnp.stack(small[n]) for n in SMALL_NAMES]
    small_start = exchange_start("reduce_small_start", [_pack_small(small_parts)],
                                 [_landing((8, SMALL_ROWS, D_MODEL), F32)], 7, _all_devices_plan, after=ici[1][4])
    swaps.append(add_all(early, small_start[4]))
    swaps.append(add_all(ici, swaps[-1][1][4]))
    for swap in swaps:
        finish(swap, swaps[-1][1][4])
    (small_own,), (small_all,) = exchange_wait("reduce_small_wait", small_start, reduced[0][0], _all_devices_plan)
    small_sum = sum_devices(small_own, small_all, me_arr, "reduce_small_sum")
    reduced[0] = reduced[0] + reduced["0 late"]
    g_big = {n: [reduced[l][k] for l in range(DEPTH)] for k, n in enumerate(("w_in", "w_out", "w_gate", "w_up", "w_down"))}
    g_big["w_in"] = [gl[:COLS_IN] for gl in g_big["w_in"]]
    transposed = ("w_in", "w_gate", "w_up")

    g_small = dict(zip(SMALL_NAMES, _unpack_small(small_sum, [p.shape for p in small_parts])))
    g_small["final_norm_w"] = g_small["final_norm_w"].reshape(final_norm_w.shape)
    g_small["conv_w"] = lax.dynamic_slice_in_dim(g_small["conv_w"], chip * (CONV_DIM // N_CHIPS), CONV_DIM // N_CHIPS, axis=2)

    given = dict(norm1_w=(norm1_w, m_norm1_w, v_norm1_w), w_in=(w_in, m_w_in, v_w_in), conv_w=(conv_w, m_conv_w, v_conv_w),
                 conv_b=(conv_b, m_conv_b, v_conv_b), dt_bias=(dt_bias, m_dt_bias, v_dt_bias), a_log=(a_log, m_a_log, v_a_log),
                 d_skip=(d_skip, m_d_skip, v_d_skip), ssd_norm_w=(ssd_norm_w, m_ssd_norm_w, v_ssd_norm_w),
                 pool_w=(pool_w, m_pool_w, v_pool_w), pool_b=(pool_b, m_pool_b, v_pool_b),
                 pool_scale=(pool_scale, m_pool_scale, v_pool_scale), w_out=(w_out, m_w_out, v_w_out),
                 norm2_w=(norm2_w, m_norm2_w, v_norm2_w), w_gate=(w_gate, m_w_gate, v_w_gate), w_up=(w_up, m_w_up, v_w_up),
                 w_down=(w_down, m_w_down, v_w_down), final_norm_w=(final_norm_w, m_final_norm_w, v_final_norm_w))
    order = ("norm1_w", "w_in", "conv_w", "conv_b", "dt_bias", "a_log", "d_skip", "ssd_norm_w", "pool_w", "pool_b",
             "pool_scale", "w_out", "norm2_w", "w_gate", "w_up", "w_down", "final_norm_w")
    grads = dict(g_small)
    results = {}
    for n in ("w_in", "w_out", "w_gate", "w_up", "w_down"):
        w, m, v = given[n]
        if n in transposed:
            out = adamw_layers(jnp.swapaxes(w, 1, 2), g_big[n], jnp.swapaxes(m, 1, 2), jnp.swapaxes(v, 1, 2), "adamw_" + n)
            out = [jnp.swapaxes(o, 1, 2) for o in out]
        else:
            out = adamw_layers(w, g_big[n], m, v, "adamw_" + n)
        grads[n], results[n] = out[0], tuple(out[1:])
    small_shapes = [given[n][0].shape for n in SMALL_NAMES]
    packed = [_pack_small([given[n][k] for n in SMALL_NAMES])[None] for k in range(3)]
    packed_g = _pack_small([grads[n] for n in SMALL_NAMES])[None]
    small_out = adamw(packed[0], packed_g, packed[1], packed[2], "adamw_small")
    small_out = [_unpack_small(o[0], small_shapes) for o in small_out]
    for i, n in enumerate(SMALL_NAMES):
        results[n] = tuple(small_out[k][i] for k in range(3))

    return (loss, grad_x, *[grads[n] for n in order], *[results[n][0] for n in order],
            *[results[n][1] for n in order], *[results[n][2] for n in order])
```

```python
import numpy as np
import jax
import jax.numpy as jnp
from jax import lax
from jax.experimental import pallas as pl
from jax.experimental.pallas import tpu as pltpu

F32 = jnp.float32
BF16 = jnp.bfloat16
MESH = pl.DeviceIdType.MESH
ANY = pl.BlockSpec(memory_space=pl.ANY)

D_MODEL = 1024
DEPTH = 4
EPS = 1e-6
SSD_WIDTH = 512
SSD_HEADS = 8
HEAD_DIM = 64
D_STATE = 128
CHUNK = 128
CONV_WIDTH = 4
CONV_DIM = 1024
SB_WIDTH = 256
POOL_WIDTH = 256
POOL_WINDOWS = (2, 4, 8, 16)
D_FF = 2816
D_IN = 2568
N_CHIPS = 4
OFF_QKV, OFF_Z, OFF_DT, OFF_XBC, OFF_P = 0, 768, 1280, 1536, 2560
D_INP = 2816
ZDT = 768
SHARD_IN, SHARD_OUT, SHARD_FF = 704, 256, 704
COLS_IN = 642
ADAM_LR, ADAM_B1, ADAM_B2, ADAM_EPS, ADAM_WD, ADAM_STEP = 0.001, 0.9, 0.999, 1e-08, 0.01, 10
LANES = 128
VMEM_LIMIT = 56 * 1024 * 1024


def _params(sem=None):
    return pltpu.CompilerParams(dimension_semantics=sem, vmem_limit_bytes=VMEM_LIMIT)


def _tile(n, cap):
    best = None
    for t in range(LANES, min(n, cap) + 1, LANES):
        if n % t == 0:
            best = t
    assert best is not None, (n, cap)
    return best


def _nt(a, b):
    return lax.dot_general(a, b, (((1,), (1,)), ((), ())), preferred_element_type=F32)


def _tn(a, b):
    return lax.dot_general(a, b, (((0,), (0,)), ((), ())), preferred_element_type=F32)


def _nn(a, b):
    return jnp.dot(a, b, preferred_element_type=F32)


def _split_dot(a, b_exact, terms=3, dot=_nn):
    acc = None
    rest = a
    for _ in range(terms):
        hi = rest.astype(BF16)
        part = dot(hi, b_exact)
        acc = part if acc is None else acc + part
        rest = rest - hi.astype(F32)
    return acc


def _split_dot_left(a_exact, b, terms=3):
    acc = None
    rest = b
    for _ in range(terms):
        hi = rest.astype(BF16)
        part = _nn(a_exact, hi)
        acc = part if acc is None else acc + part
        rest = rest - hi.astype(F32)
    return acc


def _sigmoid(x):
    return 1.0 / (1.0 + jnp.exp(-x))


def _softplus(x):
    return jnp.maximum(x, 0.0) + jnp.log(1.0 + jnp.exp(-jnp.abs(x)))


def _iota2(shape, dim):
    return lax.broadcasted_iota(jnp.int32, shape, dim)


def _after(after):
    ops = [] if after is None else list(after) if isinstance(after, (list, tuple)) else [after]
    return [ANY] * len(ops), ops


def rms_matmul(x, nw, wt, name, after=None):
    s, d = x.shape
    n = wt.shape[0]
    tm, tn = _tile(s, 512), _tile(n, 2816)
    specs, ops = _after(after)

    def body(x_ref, nw_ref, w_ref, *rest):
        o_ref, h_ref = rest[len(ops):]

        @pl.when(pl.program_id(1) == 0)
        def _():
            xv = x_ref[...]
            r = lax.rsqrt(jnp.mean(xv * xv, axis=-1, keepdims=True) + EPS)
            h_ref[...] = (xv * r * nw_ref[...]).astype(BF16)
        o_ref[...] = _nt(h_ref[...], w_ref[...])

    return pl.pallas_call(
        body, name=name, grid=(s // tm, n // tn),
        in_specs=[pl.BlockSpec((tm, d), lambda i, j: (i, 0)), pl.BlockSpec((1, d), lambda i, j: (0, 0)),
                  pl.BlockSpec((tn, d), lambda i, j: (j, 0))] + specs,
        out_specs=pl.BlockSpec((tm, tn), lambda i, j: (i, j)),
        out_shape=jax.ShapeDtypeStruct((s, n), F32),
        scratch_shapes=[pltpu.VMEM((tm, d), BF16)],
        compiler_params=_params(("parallel", "arbitrary")),
    )(x, nw, wt, *ops)


def matmul_residual(a, w, res, name):
    s, k = a.shape
    n = w.shape[1]
    tm, tn = _tile(s, 512), _tile(n, 1024)

    def body(a_ref, w_ref, r_ref, o_ref):
        o_ref[...] = r_ref[...] + _nn(a_ref[...], w_ref[...])

    return pl.pallas_call(
        body, name=name, grid=(s // tm, n // tn),
        in_specs=[pl.BlockSpec((tm, k), lambda i, j: (i, 0)), pl.BlockSpec((k, tn), lambda i, j: (0, j)),
                  pl.BlockSpec((tm, tn), lambda i, j: (i, j))],
        out_specs=pl.BlockSpec((tm, tn), lambda i, j: (i, j)),
        out_shape=jax.ShapeDtypeStruct((s, n), F32),
        compiler_params=_params(("parallel", "parallel")),
    )(a, w, res)


def matmul_nt(a, w, name, out_dtype=F32, after=None):
    s, n = a.shape
    k = w.shape[0]
    tm, tk = _tile(s, 512), _tile(k, 1024)
    specs, ops = _after(after)

    def body(a_ref, w_ref, *rest):
        rest[-1][...] = _nt(a_ref[...], w_ref[...]).astype(out_dtype)

    return pl.pallas_call(
        body, name=name, grid=(s // tm, k // tk),
        in_specs=[pl.BlockSpec((tm, n), lambda i, j: (i, 0)), pl.BlockSpec((tk, n), lambda i, j: (j, 0))] + specs,
        out_specs=pl.BlockSpec((tm, tk), lambda i, j: (i, j)),
        out_shape=jax.ShapeDtypeStruct((s, k), out_dtype),
        compiler_params=_params(("parallel", "parallel")),
    )(a, w, *ops)


def matmul_tn(a, b, name, after=None):
    s, m = a.shape
    n = b.shape[1]
    tm, tn = _tile(m, 512), _tile(n, 1024)

    def body(a_ref, b_ref, *rest):
        rest[-1][...] = _tn(a_ref[...], b_ref[...]).astype(BF16)

    specs, ops = _after(after)
    return pl.pallas_call(
        body, name=name, grid=(m // tm, n // tn),
        in_specs=[pl.BlockSpec((s, tm), lambda i, j: (0, i)), pl.BlockSpec((s, tn), lambda i, j: (0, j))] + specs,
        out_specs=pl.BlockSpec((tm, tn), lambda i, j: (i, j)),
        out_shape=jax.ShapeDtypeStruct((m, n), BF16),
        compiler_params=_params(("parallel", "parallel")),
    )(a, b, *ops)


def ffn_forward(x1, nw, wgt, wut, wd, name):
    s, d = x1.shape
    f = wgt.shape[0]
    tm, tf = _tile(s, 2048), _tile(f, 256)

    def body(x_ref, nw_ref, wg_ref, wu_ref, wd_ref, o_ref, g_ref, u_ref, h_ref):
        j = pl.program_id(1)

        @pl.when(j == 0)
        def _():
            xv = x_ref[...]
            r = lax.rsqrt(jnp.mean(xv * xv, axis=-1, keepdims=True) + EPS)
            h_ref[...] = (xv * r * nw_ref[...]).astype(BF16)
            o_ref[...] = xv

        h = h_ref[...]
        g = _nt(h, wg_ref[...])
        u = _nt(h, wu_ref[...])
        g_ref[...] = g.astype(BF16)
        u_ref[...] = u.astype(BF16)
        a = (g * _sigmoid(g) * u).astype(BF16)
        o_ref[...] += _nn(a, wd_ref[...])

    wblk = pl.BlockSpec((tf, d), lambda i, j: (j, 0))
    return pl.pallas_call(
        body, name=name, grid=(s // tm, f // tf),
        in_specs=[pl.BlockSpec((tm, d), lambda i, j: (i, 0), pipeline_mode=pl.Buffered(1)),
                  pl.BlockSpec((1, d), lambda i, j: (0, 0)), wblk, wblk, wblk],
        out_specs=[pl.BlockSpec((tm, d), lambda i, j: (i, 0)), pl.BlockSpec((tm, tf), lambda i, j: (i, j)),
                   pl.BlockSpec((tm, tf), lambda i, j: (i, j))],
        out_shape=[jax.ShapeDtypeStruct((s, d), F32), jax.ShapeDtypeStruct((s, f), BF16),
                   jax.ShapeDtypeStruct((s, f), BF16)],
        scratch_shapes=[pltpu.VMEM((tm, d), BF16)],
        compiler_params=_params(("parallel", "arbitrary")),
    )(x1, nw, wgt, wut, wd)


def ffn_backward_act(dx2, g, u, wd, name, after=None):
    s, d = dx2.shape
    f = wd.shape[0]
    tm, tf = _tile(s, 256), _tile(f, 2816)
    specs, ops = _after(after)

    def body(dx_ref, g_ref, u_ref, wd_ref, *rest):
        dg_ref, du_ref, a_ref = rest[len(ops):]
        da = _nt(dx_ref[...], wd_ref[...])
        gv = g_ref[...].astype(F32)
        uv = u_ref[...].astype(F32)
        sg = _sigmoid(gv)
        silu = gv * sg
        dg_ref[...] = (da * uv * (sg * (1.0 + gv * (1.0 - sg)))).astype(BF16)
        du_ref[...] = (da * silu).astype(BF16)
        a_ref[...] = (silu * uv).astype(BF16)

    blk = pl.BlockSpec((tm, tf), lambda i, j: (i, j))
    return pl.pallas_call(
        body, name=name, grid=(s // tm, f // tf),
        in_specs=[pl.BlockSpec((tm, d), lambda i, j: (i, 0)), blk, blk, pl.BlockSpec((tf, d), lambda i, j: (j, 0))] + specs,
        out_specs=[blk, blk, blk],
        out_shape=[jax.ShapeDtypeStruct((s, f), BF16)] * 3,
        compiler_params=_params(("parallel", "parallel")),
    )(dx2, g, u, wd, *ops)


def rms_backward(dzs, wts, x, nw, dres, name, tm, after=None):
    s, d = x.shape
    nz = len(dzs)
    specs, ops = _after(after)

    def body(*refs):
        dz_refs, w_refs = refs[:nz], refs[nz:2 * nz]
        x_ref, nw_ref, dres_ref = refs[2 * nz:2 * nz + 3]
        dx_ref, dxb_ref, h_ref, dnw_ref = refs[2 * nz + 3 + len(ops):]
        dh = _nn(dz_refs[0][...], w_refs[0][...])
        for k in range(1, nz):
            dh = dh + _nn(dz_refs[k][...], w_refs[k][...])
        xv = x_ref[...]
        r = lax.rsqrt(jnp.mean(xv * xv, axis=-1, keepdims=True) + EPS)
        xhat = xv * r
        nwv = nw_ref[...]
        h_ref[...] = (xhat * nwv).astype(BF16)

        @pl.when(pl.program_id(0) == 0)
        def _():
            dnw_ref[...] = jnp.zeros_like(dnw_ref)

        dnw_ref[...] += jnp.sum(dh * xhat, axis=0, keepdims=True)
        gdh = dh * nwv
        dx = dres_ref[...] + r * (gdh - xhat * jnp.mean(gdh * xhat, axis=-1, keepdims=True))
        dx_ref[...] = dx
        dxb_ref[...] = dx.astype(BF16)

    row = pl.BlockSpec((tm, d), lambda i: (i, 0))
    in_specs = [pl.BlockSpec((tm, dz.shape[1]), lambda i: (i, 0)) for dz in dzs]
    in_specs += [pl.BlockSpec(w.shape, lambda i: (0, 0), pipeline_mode=pl.Buffered(1)) for w in wts]
    in_specs += [row, pl.BlockSpec((1, d), lambda i: (0, 0)), row] + specs
    return pl.pallas_call(
        body, name=name, grid=(s // tm,),
        in_specs=in_specs,
        out_specs=[row, row, row, pl.BlockSpec((1, d), lambda i: (0, 0))],
        out_shape=[jax.ShapeDtypeStruct((s, d), F32), jax.ShapeDtypeStruct((s, d), BF16),
                   jax.ShapeDtypeStruct((s, d), BF16), jax.ShapeDtypeStruct((1, d), F32)],
        compiler_params=_params(("arbitrary",)),
    )(*dzs, *wts, x, nw, dres, *ops)


def loss_head(x, nw, target, name):
    s, d = x.shape
    tm = _tile(s, 512)

    def body(x_ref, nw_ref, t_ref, loss_ref, dx_ref, dxb_ref, dnw_ref):
        xv = x_ref[...]
        r = lax.rsqrt(jnp.mean(xv * xv, axis=-1, keepdims=True) + EPS)
        xhat = xv * r
        nwv = nw_ref[...]
        err = xhat * nwv - t_ref[...]

        @pl.when(pl.program_id(0) == 0)
        def _():
            dnw_ref[...] = jnp.zeros_like(dnw_ref)
            loss_ref[...] = jnp.zeros_like(loss_ref)

        part = jnp.sum(jnp.sum(err * err, axis=-1, keepdims=True), axis=0, keepdims=True) * (0.5 / d)
        loss_ref[...] += jnp.broadcast_to(part, loss_ref.shape)
        dout = err * (1.0 / d)
        dnw_ref[...] += jnp.sum(dout * xhat, axis=0, keepdims=True)
        gdh = dout * nwv
        dx = r * (gdh - xhat * jnp.mean(gdh * xhat, axis=-1, keepdims=True))
        dx_ref[...] = dx
        dxb_ref[...] = dx.astype(BF16)

    row = pl.BlockSpec((tm, d), lambda i: (i, 0))
    return pl.pallas_call(
        body, name=name, grid=(s // tm,),
        in_specs=[row, pl.BlockSpec((1, d), lambda i: (0, 0)), row],
        out_specs=[pl.BlockSpec((1, LANES), lambda i: (0, 0)), row, row, pl.BlockSpec((1, d), lambda i: (0, 0))],
        out_shape=[jax.ShapeDtypeStruct((1, LANES), F32), jax.ShapeDtypeStruct((s, d), F32),
                   jax.ShapeDtypeStruct((s, d), BF16), jax.ShapeDtypeStruct((1, d), F32)],
        compiler_params=_params(("arbitrary",)),
    )(x, nw, target)


def _shift_down(x, k):
    return jnp.where(_iota2(x.shape, 0) >= k, pltpu.roll(x, k, axis=0), 0.0)


def _shift_up(x, k):
    s = x.shape[0]
    return jnp.where(_iota2(x.shape, 0) < s - k, pltpu.roll(x, s - k, axis=0), 0.0)


CONV_TILE = 256


def conv_forward(proj, cw, cb, name):
    s = proj.shape[0]
    tn = CONV_TILE
    off = OFF_XBC // tn

    def body(u_ref, w_ref, b_ref, o_ref):
        u = u_ref[...]
        pre = b_ref[...] + w_ref[CONV_WIDTH - 1:CONV_WIDTH, :] * u
        for i in range(CONV_WIDTH - 1):
            pre = pre + w_ref[i:i + 1, :] * _shift_down(u, CONV_WIDTH - 1 - i)
        o_ref[...] = pre * _sigmoid(pre)

    return pl.pallas_call(
        body, name=name, grid=(CONV_DIM // tn,),
        in_specs=[pl.BlockSpec((s, tn), lambda j: (0, off + j)), pl.BlockSpec((8, tn), lambda j: (0, j)),
                  pl.BlockSpec((1, tn), lambda j: (0, j))],
        out_specs=pl.BlockSpec((s, tn), lambda j: (0, j)),
        out_shape=jax.ShapeDtypeStruct((s, CONV_DIM), F32),
        compiler_params=_params(("parallel",)),
    )(proj, cw, cb)


def conv_backward(proj, dxc, cw, cb, dproj, name, after=None):
    s = proj.shape[0]
    tn = CONV_TILE
    off = OFF_XBC // tn

    specs, ops = _after(after)

    def body(u_ref, d_ref, w_ref, b_ref, *rest):
        du_ref, dw_ref, db_ref = rest[-3:]
        u = u_ref[...]
        shifted = [_shift_down(u, CONV_WIDTH - 1 - i) for i in range(CONV_WIDTH - 1)] + [u]
        pre = b_ref[...] + w_ref[CONV_WIDTH - 1:CONV_WIDTH, :] * u
        for i in range(CONV_WIDTH - 1):
            pre = pre + w_ref[i:i + 1, :] * shifted[i]
        sg = _sigmoid(pre)
        dpre = d_ref[...] * (sg * (1.0 + pre * (1.0 - sg)))
        du = w_ref[CONV_WIDTH - 1:CONV_WIDTH, :] * dpre
        for i in range(CONV_WIDTH - 1):
            du = du + w_ref[i:i + 1, :] * _shift_up(dpre, CONV_WIDTH - 1 - i)
        du_ref[...] = du.astype(BF16)
        rows = [jnp.sum(dpre * shifted[i], axis=0, keepdims=True) for i in range(CONV_WIDTH)]
        rows.append(jnp.zeros((8 - CONV_WIDTH, tn), F32))
        dw_ref[...] = jnp.concatenate(rows, axis=0)
        db_ref[...] = jnp.sum(dpre, axis=0, keepdims=True)

    return pl.pallas_call(
        body, name=name, grid=(CONV_DIM // tn,),
        in_specs=[pl.BlockSpec((s, tn), lambda j: (0, off + j)), pl.BlockSpec((s, tn), lambda j: (0, j)),
                  pl.BlockSpec((8, tn), lambda j: (0, j)), pl.BlockSpec((1, tn), lambda j: (0, j)), ANY] + specs,
        out_specs=[pl.BlockSpec((s, tn), lambda j: (0, off + j)), pl.BlockSpec((8, tn), lambda j: (0, j)),
                   pl.BlockSpec((1, tn), lambda j: (0, j))],
        out_shape=[jax.ShapeDtypeStruct(dproj.shape, BF16), jax.ShapeDtypeStruct((8, CONV_DIM), F32),
                   jax.ShapeDtypeStruct((1, CONV_DIM), F32)],
        input_output_aliases={4: 0},
        compiler_params=_params(("parallel",)),
    )(proj, dxc, cw, cb, dproj, *ops)


def _pool_lane_window(shape):
    grp = _iota2(shape, 1) // (POOL_WIDTH // len(POOL_WINDOWS))
    win = jnp.full(shape, POOL_WINDOWS[-1], jnp.int32)
    for gi in range(len(POOL_WINDOWS) - 2, -1, -1):
        win = jnp.where(grp == gi, POOL_WINDOWS[gi], win)
    return grp, win


def _pool_select(grp, sums):
    out = sums[-1]
    for gi in range(len(sums) - 2, -1, -1):
        out = jnp.where(grp == gi, sums[gi], out)
    return out


def _pool_pooled(p):
    grp, win = _pool_lane_window(p.shape)
    inv_count = 1.0 / jnp.minimum(_iota2(p.shape, 0) + 1, win).astype(F32)
    sums, acc, k = [], p, 1
    for _ in POOL_WINDOWS:
        acc = acc + _shift_down(acc, k)
        sums.append(acc)
        k *= 2
    return _pool_select(grp, sums) * inv_count - p, grp, inv_count


def pool_forward(proj, wbd, pb, ps, y_all, name, after=None):
    s = proj.shape[0]
    specs, ops = _after(after)

    def body(p_ref, w_ref, b_ref, s_ref, *rest):
        o_ref = rest[-1]
        pooled, _, _ = _pool_pooled(p_ref[...])
        mixed = _nn(pooled.astype(BF16), w_ref[...]) + b_ref[...]
        o_ref[...] = (mixed * s_ref[...]).astype(BF16)

    vec = pl.BlockSpec((1, POOL_WIDTH), lambda j: (0, 0))
    return pl.pallas_call(
        body, name=name, grid=(1,),
        in_specs=[pl.BlockSpec((s, POOL_WIDTH), lambda j: (0, OFF_P // POOL_WIDTH)),
                  pl.BlockSpec((POOL_WIDTH, POOL_WIDTH), lambda j: (0, 0)), vec, vec, ANY] + specs,
        out_specs=pl.BlockSpec((s, POOL_WIDTH), lambda j: (0, (SSD_WIDTH + SB_WIDTH) // POOL_WIDTH)),
        out_shape=jax.ShapeDtypeStruct(y_all.shape, BF16),
        input_output_aliases={4: 0},
        compiler_params=_params(("arbitrary",)),
    )(proj, wbd, pb, ps, y_all, *ops)


def pool_backward(proj, dyall, wbd, pb, ps, dproj, name):
    s = proj.shape[0]

    def body(p_ref, dy_ref, w_ref, b_ref, s_ref, _, dp_ref, dw_ref, db_ref, ds_ref):
        pooled, grp, inv_count = _pool_pooled(p_ref[...])
        pooled_b = pooled.astype(BF16)
        mixed = _nn(pooled_b, w_ref[...]) + b_ref[...]
        dy = dy_ref[...]
        ds_ref[...] = jnp.sum(dy * mixed, axis=0, keepdims=True)
        dmixed = dy * s_ref[...]
        db_ref[...] = jnp.sum(dmixed, axis=0, keepdims=True)
        dmixed_b = dmixed.astype(BF16)
        dw_ref[...] = _tn(pooled_b, dmixed_b)
        dpooled = _nt(dmixed_b, w_ref[...])
        sums, acc, k = [], dpooled * inv_count, 1
        for _ in POOL_WINDOWS:
            acc = acc + _shift_up(acc, k)
            sums.append(acc)
            k *= 2
        dp_ref[...] = (_pool_select(grp, sums) - dpooled).astype(BF16)

    vec = pl.BlockSpec((1, POOL_WIDTH), lambda j: (0, 0))
    mat = pl.BlockSpec((POOL_WIDTH, POOL_WIDTH), lambda j: (0, 0))
    pcol = pl.BlockSpec((s, POOL_WIDTH), lambda j: (0, OFF_P // POOL_WIDTH))
    return pl.pallas_call(
        body, name=name, grid=(1,),
        in_specs=[pcol, pl.BlockSpec((s, POOL_WIDTH), lambda j: (0, (SSD_WIDTH + SB_WIDTH) // POOL_WIDTH)), mat, vec, vec, ANY],
        out_specs=[pcol, mat, vec, vec],
        out_shape=[jax.ShapeDtypeStruct(dproj.shape, BF16), jax.ShapeDtypeStruct((POOL_WIDTH, POOL_WIDTH), F32),
                   jax.ShapeDtypeStruct((1, POOL_WIDTH), F32), jax.ShapeDtypeStruct((1, POOL_WIDTH), F32)],
        input_output_aliases={5: 0},
        compiler_params=_params(("arbitrary",)),
    )(proj, dyall, wbd, pb, ps, dproj)


N_PAIRS = SSD_HEADS // 2


def _ssd_common(xc, dtraw, dtb, alog):
    c = CHUNK
    dt = _softplus(dtraw + dtb)
    a = -jnp.exp(alog)
    ltri = (_iota2((c, c), 0) >= _iota2((c, c), 1)).astype(BF16)
    acum = _split_dot_left(ltri, dt * a)
    expand = (_iota2((c, SSD_WIDTH), 1) // HEAD_DIM == _iota2((c, SSD_WIDTH), 0)).astype(BF16)
    expand_wide = (_iota2((c, SSD_HEADS * c), 1) // c == _iota2((c, SSD_HEADS * c), 0)).astype(BF16)
    acum_x = _split_dot(acum, expand, 2)
    dt_x = _split_dot(dt, expand, 2)
    alast_x = acum_x[c - 1:c, :]
    return dict(dt=dt, a=a, acum=acum, acum_x=acum_x, dt_x=dt_x, ea_x=jnp.exp(acum_x),
                dte_x=jnp.exp(alast_x - acum_x), eal_x=jnp.exp(alast_x),
                acol=_split_dot(acum, expand_wide, 2), acum_t=acum.T,
                xs=xc[:, :SSD_WIDTH], causal=_iota2((c, c), 0) >= _iota2((c, c), 1),
                left=_iota2((c, c), 1) < HEAD_DIM)


def _ssd_group(xc, g):
    b = xc[:, SSD_WIDTH + D_STATE * g:SSD_WIDTH + D_STATE * (g + 1)]
    cm = xc[:, SSD_WIDTH + 2 * D_STATE + D_STATE * g:SSD_WIDTH + 2 * D_STATE + D_STATE * (g + 1)]
    return b, cm


def _ssd_decay(q, hh):
    col = q["acol"][:, CHUNK * hh:CHUNK * (hh + 1)]
    row = q["acum_t"][hh:hh + 1, :]
    return jnp.where(q["causal"], jnp.exp(jnp.minimum(col - row, 0.0)), 0.0)


def ssd_forward(proj, xc, dtb, alog, dskip_x, nw, name):
    s = xc.shape[0]
    nc = s // CHUNK

    def body(xc_ref, zdt_ref, dtb_ref, alog_ref, dsk_ref, nw_ref, y_ref, yc_ref, st_ref, state):
        @pl.when(pl.program_id(0) == 0)
        def _():
            state[...] = jnp.zeros_like(state)

        xcv = xc_ref[...]
        q = _ssd_common(xcv, zdt_ref[:, SSD_WIDTH:SSD_WIDTH + LANES], dtb_ref[...], alog_ref[...])
        x = q["xs"] * q["dt_x"]
        xb = x.astype(BF16)
        xd = (x * q["dte_x"]).astype(BF16)
        pieces = []
        for g in range(2):
            bg, cg = _ssd_group(xcv, g)
            bgb, cgb = bg.astype(BF16), cg.astype(BF16)
            cb = _nt(cgb, bgb)
            bgt = bg.T.astype(BF16)
            for pr in (2 * g, 2 * g + 1):
                sl = slice(CHUNK * pr, CHUNK * (pr + 1))
                st = state[pr]
                st_ref[0, pr] = st
                yp = _nn(cgb, st.astype(BF16)) * q["ea_x"][:, sl]
                for k, hh in enumerate((2 * pr, 2 * pr + 1)):
                    w = (cb * _ssd_decay(q, hh)).astype(BF16)
                    mask = q["left"] if k == 0 else jnp.logical_not(q["left"])
                    yp = yp + _nn(w, jnp.where(mask, xb[:, sl], jnp.zeros_like(xb[:, sl])))
                state[pr] = st * q["eal_x"][:, sl] + _nn(bgt, xd[:, sl])
                pieces.append(yp)
        y = jnp.concatenate(pieces, axis=1) + q["xs"] * dsk_ref[...]
        yc_ref[...] = y
        zv = zdt_ref[:, :SSD_WIDTH]
        yg = y * (zv * _sigmoid(zv))
        r = lax.rsqrt(jnp.mean(yg * yg, axis=-1, keepdims=True) + EPS)
        y_ref[...] = (yg * r * nw_ref[...]).astype(BF16)

    vec = lambda n: pl.BlockSpec((1, n), lambda c: (0, 0))
    return pl.pallas_call(
        body, name=name, grid=(nc,),
        in_specs=[pl.BlockSpec((CHUNK, CONV_DIM), lambda c: (c, 0)),
                  pl.BlockSpec((CHUNK, ZDT), lambda c: (c, OFF_Z // ZDT)),
                  vec(LANES), vec(LANES), vec(SSD_WIDTH), vec(SSD_WIDTH)],
        out_specs=[pl.BlockSpec((CHUNK, SSD_WIDTH), lambda c: (c, 0)), pl.BlockSpec((CHUNK, SSD_WIDTH), lambda c: (c, 0)),
                   pl.BlockSpec((1, N_PAIRS, D_STATE, CHUNK), lambda c: (c, 0, 0, 0))],
        out_shape=[jax.ShapeDtypeStruct((s, D_MODEL), BF16), jax.ShapeDtypeStruct((s, SSD_WIDTH), F32),
                   jax.ShapeDtypeStruct((nc, N_PAIRS, D_STATE, CHUNK), F32)],
        scratch_shapes=[pltpu.VMEM((N_PAIRS, D_STATE, CHUNK), F32)],
        compiler_params=_params(("arbitrary",)),
    )(xc, proj, dtb, alog, dskip_x, nw)


def ssd_backward(proj, xc, ycore, dyall, states, dtb, alog, dskip_x, nw, name):
    s = xc.shape[0]
    nc = s // CHUNK
    c = CHUNK

    def body(xc_ref, zdt_ref, yc_ref, dy_ref, st_ref, dtb_ref, alog_ref, dsk_ref, nw_ref,
             dxc_ref, dzdt_ref, dnw_ref, ddsk_ref, ddtb_ref, dalog_ref, dstate):
        @pl.when(pl.program_id(0) == 0)
        def _():
            dstate[...] = jnp.zeros_like(dstate)
            dnw_ref[...] = jnp.zeros_like(dnw_ref)
            ddsk_ref[...] = jnp.zeros_like(ddsk_ref)
            ddtb_ref[...] = jnp.zeros_like(ddtb_ref)
            dalog_ref[...] = jnp.zeros_like(dalog_ref)

        xcv = xc_ref[...]
        dtraw = zdt_ref[:, SSD_WIDTH:SSD_WIDTH + LANES]
        q = _ssd_common(xcv, dtraw, dtb_ref[...], alog_ref[...])
        xs = q["xs"]
        x = xs * q["dt_x"]
        zv, yc, dy, nwv = zdt_ref[:, :SSD_WIDTH], yc_ref[...], dy_ref[...], nw_ref[...]
        sgz = _sigmoid(zv)
        siluz = zv * sgz
        yg = yc * siluz
        r = lax.rsqrt(jnp.mean(yg * yg, axis=-1, keepdims=True) + EPS)
        dnw_ref[...] += jnp.sum(dy * yg * r, axis=0, keepdims=True)
        g1 = dy * nwv
        dyg = r * (g1 - yg * (r * r) * jnp.mean(g1 * yg, axis=-1, keepdims=True))
        dyv = dyg * siluz
        dz = (dyg * yc * (sgz * (1.0 + zv * (1.0 - sgz)))).astype(BF16)
        ddsk_ref[...] += jnp.sum(dyv * xs, axis=0, keepdims=True)
        dye = dyv * q["ea_x"]
        dx_parts, yoff_parts, u_parts, v_parts, e_parts = [], [], [], [], []
        db_parts, dc_parts = [], []
        for g in range(2):
            bg, cg = _ssd_group(xcv, g)
            bgb, cgb = bg.astype(BF16), cg.astype(BF16)
            cb = _nt(cgb, bgb)
            cgt = cg.T.astype(BF16)
            dgsum = jnp.zeros((c, c), F32)
            dbg = jnp.zeros((c, D_STATE), F32)
            dcg = jnp.zeros((c, D_STATE), F32)
            for pr in (2 * g, 2 * g + 1):
                sl = slice(c * pr, c * (pr + 1))
                st = st_ref[0, pr]
                dst = dstate[pr]
                stb, dstb = st.astype(BF16), dst.astype(BF16)
                xp = x[:, sl]
                xpb = xp.astype(BF16)
                dyp = dyv[:, sl]
                xdp = xp * q["dte_x"][:, sl]
                yoff_parts.append(_nn(cgb, stb) * q["ea_x"][:, sl])
                rr = _nn(bgb, dstb)
                dxp = rr * q["dte_x"][:, sl]
                u_parts.append(rr * xdp)
                v_parts.append(dst * st * q["eal_x"][:, sl])
                for k, hh in enumerate((2 * pr, 2 * pr + 1)):
                    decay = _ssd_decay(q, hh)
                    w = cb * decay
                    mask = q["left"] if k == 0 else jnp.logical_not(q["left"])
                    dym = jnp.where(mask, dyp, 0.0).astype(BF16)
                    dw = _nt(dym, xpb)
                    dgsum = dgsum + dw * decay
                    e_parts.append(dw * w)
                    dxp = dxp + _nn(w.T.astype(BF16), dym)
                dyeb = dye[:, sl].astype(BF16)
                dcg = dcg + _nt(dyeb, stb)
                dbg = dbg + _nt(xdp.astype(BF16), dstb)
                dstate[pr] = dst * q["eal_x"][:, sl] + _nn(cgt, dyeb)
                dx_parts.append(dxp)
            dcg = dcg + _nn(dgsum.astype(BF16), bgb)
            dbg = dbg + _nn(dgsum.T.astype(BF16), cgb)
            db_parts.append(dbg)
            dc_parts.append(dcg)
        dx = jnp.concatenate(dx_parts, axis=1)
        yoff = jnp.concatenate(yoff_parts, axis=1)
        u = jnp.concatenate(u_parts, axis=1)
        v = jnp.concatenate(v_parts, axis=1)
        reduce_heads = (_iota2((SSD_WIDTH, c), 0) // HEAD_DIM == _iota2((SSD_WIDTH, c), 1)).astype(BF16)
        to_head = (_iota2((SSD_HEADS * c, c), 0) // c == _iota2((SSD_HEADS * c, c), 1)).astype(BF16)
        da = _split_dot(dyv * yoff - u, reduce_heads, 2)
        da = da + _split_dot(jnp.concatenate(e_parts, axis=1), to_head, 2)
        da = da - _split_dot(jnp.concatenate(e_parts, axis=0), to_head, 2, dot=_tn)
        dalast = jnp.sum(_split_dot(u + v, reduce_heads, 2), axis=0, keepdims=True)
        da = da + jnp.where(_iota2((c, c), 0) == c - 1, dalast, 0.0)
        utri = (_iota2((c, c), 1) >= _iota2((c, c), 0)).astype(BF16)
        dda = _split_dot_left(utri, da)
        ddt = dda * q["a"] + _split_dot(dx * xs, reduce_heads, 2)
        dalog_ref[...] += jnp.sum(dda * q["dt"], axis=0, keepdims=True) * q["a"]
        ddtraw = jnp.where(_iota2((c, c), 1) < SSD_HEADS, ddt * _sigmoid(dtraw + dtb_ref[...]), 0.0)
        ddtb_ref[...] += jnp.sum(ddtraw, axis=0, keepdims=True)
        dzdt_ref[...] = jnp.concatenate([dz, ddtraw.astype(BF16), jnp.zeros((c, ZDT - SSD_WIDTH - LANES), BF16)], axis=1)
        dxs = dx * q["dt_x"] + dyv * dsk_ref[...]
        dxc_ref[...] = jnp.concatenate([dxs] + db_parts + dc_parts, axis=1)

    rev = lambda i: nc - 1 - i
    vec = lambda n: pl.BlockSpec((1, n), lambda i: (0, 0))
    wide = pl.BlockSpec((c, SSD_WIDTH), lambda i: (rev(i), 0))
    zdt = pl.BlockSpec((c, ZDT), lambda i: (rev(i), OFF_Z // ZDT))
    return pl.pallas_call(
        body, name=name, grid=(nc,),
        in_specs=[pl.BlockSpec((c, CONV_DIM), lambda i: (rev(i), 0)), zdt, wide, wide,
                  pl.BlockSpec((1, N_PAIRS, D_STATE, c), lambda i: (rev(i), 0, 0, 0)),
                  vec(LANES), vec(LANES), vec(SSD_WIDTH), vec(SSD_WIDTH)],
        out_specs=[pl.BlockSpec((c, CONV_DIM), lambda i: (rev(i), 0)), zdt,
                   vec(SSD_WIDTH), vec(SSD_WIDTH), vec(LANES), vec(LANES)],
        out_shape=[jax.ShapeDtypeStruct((s, CONV_DIM), F32), jax.ShapeDtypeStruct((s, D_INP), BF16),
                   jax.ShapeDtypeStruct((1, SSD_WIDTH), F32),
                   jax.ShapeDtypeStruct((1, SSD_WIDTH), F32), jax.ShapeDtypeStruct((1, LANES), F32),
                   jax.ShapeDtypeStruct((1, LANES), F32)],
        scratch_shapes=[pltpu.VMEM((N_PAIRS, D_STATE, c), F32)],
        compiler_params=_params(("arbitrary",)),
    )(xc, proj, ycore, dyall, states, dtb, alog, dskip_x, nw)


SB_Q, SB_K = 512, 512
SB_T = 256
SB_SCALE = HEAD_DIM ** -0.5


def _key_suffix(x, tri, terms):
    runs = [x[:, SB_T * k:SB_T * (k + 1)] for k in range(SB_K // SB_T)]
    sums = [_split_dot(r, tri, terms) for r in runs]
    later = None
    for k in range(len(runs) - 1, -1, -1):
        if later is not None:
            sums[k] = sums[k] + later
        total = jnp.sum(runs[k], axis=1, keepdims=True)
        later = total if later is None else later + total
    return jnp.concatenate(sums, axis=1), later


def _sb_weights(qm, kb, diagonal, run_lk, strict_after):
    z = _nt(qm, kb)
    nz = -z
    tail = jnp.log(1.0 + jnp.exp(jnp.minimum(z, nz)))
    ls = jnp.minimum(z, 0.0) - tail
    lk = jnp.minimum(nz, 0.0) - tail
    if diagonal is not None:
        valid = _iota2(z.shape, 1) < _iota2(z.shape, 0) + diagonal
        lk = jnp.where(valid, lk, 0.0)
    after, total = _key_suffix(lk, strict_after, 1)
    w = jnp.exp(ls + after + run_lk)
    if diagonal is not None:
        w = jnp.where(valid, w, 0.0)
    return ls, total, w


def _sb_sweep(i, block, init):
    own = (i * SB_Q) // SB_K
    first = block(own, init, i * SB_Q - own * SB_K)
    return lax.fori_loop(1, own + 1, lambda jj, carry: block(own - jj, carry, None), first)


def sb_forward(proj, y_all, name, after=None):
    s = proj.shape[0]
    t, tk = SB_Q, SB_K
    nq = s // t
    specs, ops = _after(after)

    def body(q_ref, k_ref, v_ref, *rest):
        y_ref, o_ref = rest[-2:]
        i = pl.program_id(1)
        left = _iota2((t, LANES), 1) < HEAD_DIM
        left_k = _iota2((tk, LANES), 1) < HEAD_DIM
        qv = q_ref[...] * SB_SCALE
        zero = jnp.zeros_like(qv)
        qms = (jnp.where(left, qv, zero).astype(BF16), jnp.where(left, zero, qv).astype(BF16))
        strict_after = (_iota2((SB_T, SB_T), 0) > _iota2((SB_T, SB_T), 1)).astype(BF16)

        def block(j, carry, diagonal):
            o, runs = carry[0], carry[1:]
            rows = pl.ds(pl.multiple_of(j * tk, tk), tk)
            kb = k_ref[rows, :].astype(BF16)
            vv = v_ref[rows, :]
            new_runs = []
            for k in range(2):
                _, total, w = _sb_weights(qms[k], kb, diagonal, runs[k], strict_after)
                vm = jnp.where(left_k if k == 0 else jnp.logical_not(left_k), vv, 0.0).astype(BF16)
                o = o + _nn(w.astype(BF16), vm)
                new_runs.append(runs[k] + total)
            return (o, *new_runs)

        init = (jnp.zeros((t, LANES), F32), jnp.zeros((t, 1), F32), jnp.zeros((t, 1), F32))
        o = _sb_sweep(i, block, init)[0]
        o_ref[...] = o
        y_ref[...] = o.astype(BF16)

    return pl.pallas_call(
        body, name=name, grid=(2, nq),
        in_specs=[pl.BlockSpec((t, LANES), lambda p, i: (i, 3 * p)),
                  pl.BlockSpec((s, LANES), lambda p, i: (0, 3 * p + 1)),
                  pl.BlockSpec((s, LANES), lambda p, i: (0, 3 * p + 2)), ANY] + specs,
        out_specs=[pl.BlockSpec((t, LANES), lambda p, i: (i, SSD_WIDTH // LANES + p)),
                   pl.BlockSpec((t, LANES), lambda p, i: (i, p))],
        out_shape=[jax.ShapeDtypeStruct(y_all.shape, BF16), jax.ShapeDtypeStruct((s, SB_WIDTH), F32)],
        input_output_aliases={3: 0},
        compiler_params=_params(("parallel", "arbitrary")),
    )(proj, proj, proj, y_all, *ops)


def sb_backward(proj, o, dyall, dproj, name, after=None):
    s = proj.shape[0]
    t, tk = SB_Q, SB_K
    nq = s // t
    specs, ops = _after(after)

    def body(q_ref, k_ref, v_ref, o_ref, do_ref, *rest):
        dqkv_ref, dk_acc, dv_acc = rest[-3:]
        dk_acc[...] = jnp.zeros_like(dk_acc)
        dv_acc[...] = jnp.zeros_like(dv_acc)
        left = _iota2((t, LANES), 1) < HEAD_DIM
        lane_masks = (left, jnp.logical_not(left))
        left_k = _iota2((tk, LANES), 1) < HEAD_DIM
        key_masks = (left_k, jnp.logical_not(left_k))
        strict_after = (_iota2((SB_T, SB_T), 0) > _iota2((SB_T, SB_T), 1)).astype(BF16)
        from_here = (_iota2((SB_T, SB_T), 0) >= _iota2((SB_T, SB_T), 1)).astype(BF16)

        def query_block(i, _):
            qrows = pl.ds(pl.multiple_of(i * t, t), t)
            qv = q_ref[qrows, :] * SB_SCALE
            dov = do_ref[qrows, :]
            zero = jnp.zeros_like(qv)
            qb = qv.astype(BF16)
            dob = dov.astype(BF16)
            prod = dob.astype(F32) * o_ref[qrows, :]
            qms = [jnp.where(m, qv, zero).astype(BF16) for m in lane_masks]
            doms = [jnp.where(m, dov, zero).astype(BF16) for m in lane_masks]
            deltas = [jnp.sum(jnp.where(m, prod, zero), axis=1, keepdims=True) for m in lane_masks]

            def block(j, carry, diagonal):
                dq = carry[0]
                run_lk, run_e = carry[1:3], carry[3:5]
                rows = pl.ds(pl.multiple_of(j * tk, tk), tk)
                kb = k_ref[rows, :].astype(BF16)
                vb = v_ref[rows, :].astype(BF16)
                dkj = jnp.zeros((tk, LANES), F32)
                dvj = jnp.zeros((tk, LANES), F32)
                new_lk, new_e = [], []
                for k in range(2):
                    ls, total, w = _sb_weights(qms[k], kb, diagonal, run_lk[k], strict_after)
                    wb = w.astype(BF16)
                    e = _nt(doms[k], vb) * wb.astype(F32)
                    e_from_here, e_total = _key_suffix(e, from_here, 2)
                    before = deltas[k] - e_from_here - run_e[k]
                    dz = e - jnp.exp(ls) * (e + before)
                    if diagonal is not None:
                        dz = jnp.where(_iota2(dz.shape, 1) < _iota2(dz.shape, 0) + diagonal, dz, 0.0)
                    dz = dz.astype(BF16)
                    m = lane_masks[k]
                    dvj = dvj + jnp.where(key_masks[k], _tn(wb, dob), 0.0)
                    dkj = dkj + jnp.where(key_masks[k], _tn(dz, qb), 0.0)
                    dq = dq + jnp.where(m, _nn(dz, kb), 0.0)
                    new_lk.append(run_lk[k] + total)
                    new_e.append(run_e[k] + e_total)
                dk_acc[rows, :] += dkj
                dv_acc[rows, :] += dvj
                return (dq, *new_lk, *new_e)

            col = jnp.zeros((t, 1), F32)
            dq = _sb_sweep(i, block, (jnp.zeros((t, LANES), F32), col, col, col, col))[0]
            dqkv_ref[qrows, 0:LANES] = (dq * SB_SCALE).astype(BF16)
            return 0

        lax.fori_loop(0, nq, query_block, 0)
        dqkv_ref[:, LANES:2 * LANES] = dk_acc[...].astype(BF16)
        dqkv_ref[:, 2 * LANES:3 * LANES] = dv_acc[...].astype(BF16)

    col = lambda f: pl.BlockSpec((s, LANES), f)
    return pl.pallas_call(
        body, name=name, grid=(2,),
        in_specs=[col(lambda p: (0, 3 * p)), col(lambda p: (0, 3 * p + 1)), col(lambda p: (0, 3 * p + 2)),
                  col(lambda p: (0, p)), col(lambda p: (0, SSD_WIDTH // LANES + p)), ANY] + specs,
        out_specs=pl.BlockSpec((s, 3 * LANES), lambda p: (0, p)),
        out_shape=jax.ShapeDtypeStruct(dproj.shape, BF16),
        input_output_aliases={5: 0},
        scratch_shapes=[pltpu.VMEM((s, LANES), F32), pltpu.VMEM((s, LANES), F32)],
        compiler_params=_params(("parallel",)),
    )(proj, proj, proj, o, dyall, dproj, *ops)


def adamw(w, g, m, v, name):
    b, r, c = w.shape
    tr = max([t for t in range(8, min(r, 512) + 1, 8) if r % t == 0], default=r)

    def body(w_ref, g_ref, m_ref, v_ref, d_ref, nm_ref, nv_ref):
        gv = g_ref[...]
        nm = ADAM_B1 * m_ref[...] + (1.0 - ADAM_B1) * gv
        nv = ADAM_B2 * v_ref[...] + (1.0 - ADAM_B2) * (gv * gv)
        m_hat = nm / (1.0 - ADAM_B1 ** ADAM_STEP)
        v_hat = nv / (1.0 - ADAM_B2 ** ADAM_STEP)
        d_ref[...] = -ADAM_LR * (m_hat / (jnp.sqrt(v_hat) + ADAM_EPS) + ADAM_WD * w_ref[...])
        nm_ref[...] = nm
        nv_ref[...] = nv

    blk = pl.BlockSpec((1, tr, c), lambda i, j: (i, j, 0))
    return pl.pallas_call(
        body, name=name, grid=(b, r // tr),
        in_specs=[blk] * 4, out_specs=[blk] * 3,
        out_shape=[jax.ShapeDtypeStruct(w.shape, F32)] * 3,
        compiler_params=_params(("parallel", "parallel")),
    )(w, g, m, v)


def _position():
    return lax.axis_index("x"), lax.axis_index("y"), lax.axis_index("c")


def _flipped(pos, flip):
    return tuple((1 - p) if f else p for p, f in zip(pos, flip))


FLIP_C = (0, 0, 1)
CHIP_FLIPS = {1: (0, 1, 0), 2: (1, 0, 0), 3: (1, 1, 0)}
SHARD_ROWS = (SHARD_IN, SHARD_OUT, SHARD_FF, SHARD_FF, SHARD_FF)


def _rows(start, size):
    return pl.ds(pl.multiple_of(start, 16), size)


HBM = pl.BlockSpec(memory_space=pltpu.HBM)
SEM = pl.BlockSpec(memory_space=pltpu.SEMAPHORE)
EFFECT = pltpu.SideEffectType.DATAFLOW_SIDE_EFFECTING


def _in_hbm(a):
    return pltpu.with_memory_space_constraint(a, pltpu.HBM)


def _landing(shape, dtype):
    return _in_hbm(lax.empty(shape, dtype))


def _copies(plan, pos, src_refs, land_refs, send_sems, recv_sems):
    return [pltpu.make_async_remote_copy(src_ref=src, dst_ref=dst, send_sem=send_sems.at[k], recv_sem=recv_sems.at[k],
                                         device_id=_flipped(pos, flip), device_id_type=MESH)
            for k, (src, dst, flip) in enumerate(plan(pos, src_refs, land_refs))]


def exchange_start(name, srcs, lands, n, plan, after=None):
    ns, nl = len(srcs), len(lands)
    specs, ops = _after(after)

    def body(*refs):
        src_refs, land_refs = refs[:ns], refs[ns:ns + nl]
        send_sems, recv_sems, token = refs[ns + nl + len(ops)], refs[ns + nl + len(ops) + 1], refs[-1]
        for cp in _copies(plan, _position(), src_refs, land_refs, send_sems, recv_sems):
            cp.start()
        token[...] = jnp.zeros_like(token)

    thru = [pltpu.HBM(a.shape, a.dtype) for a in list(srcs) + list(lands)]
    out = pl.pallas_call(
        body, name=name,
        out_shape=(pltpu.SemaphoreType.DMA((n,)), pltpu.SemaphoreType.DMA((n,)), *thru, jax.ShapeDtypeStruct((8, LANES), F32)),
        in_specs=[HBM] * (ns + nl) + specs,
        out_specs=(SEM, SEM, *([HBM] * (ns + nl)), pl.BlockSpec(memory_space=pltpu.VMEM)),
        input_output_aliases={k: 2 + k for k in range(ns + nl)},
        compiler_params=pltpu.CompilerParams(has_side_effects=EFFECT),
    )(*[_in_hbm(a) for a in srcs], *lands, *ops)
    return out[0], out[1], list(out[2:2 + ns]), list(out[2 + ns:2 + ns + nl]), out[-1]


def exchange_wait(name, started, after, plan):
    send_sems, recv_sems, srcs, lands, _ = started
    ns, nl = len(srcs), len(lands)
    specs, ops = _after(after)

    def body(*refs):
        src_refs, land_refs = refs[:ns], refs[ns:ns + nl]
        send_sems, recv_sems = refs[ns + nl], refs[ns + nl + 1]
        for cp in _copies(plan, _position(), src_refs, land_refs, send_sems, recv_sems):
            cp.wait_send()
            cp.wait_recv()

    out = pl.pallas_call(
        body, name=name,
        out_shape=tuple(pltpu.HBM(a.shape, a.dtype) for a in list(srcs) + list(lands)),
        in_specs=[HBM] * (ns + nl) + [SEM, SEM] + specs,
        out_specs=tuple([HBM] * (ns + nl)),
        input_output_aliases={k: k for k in range(ns + nl)},
        compiler_params=pltpu.CompilerParams(has_side_effects=EFFECT),
    )(*srcs, *lands, send_sems, recv_sems, *ops)
    return list(out[:ns]), list(out[ns:])


def _gather_ici_plan(pos, srcs, lands):
    chip, c = 2 * pos[0] + pos[1], pos[2]
    copies = []
    for src, dst in zip(srcs, lands):
        r = src.shape[0]
        h = r // 2
        for f in (1, 2, 3):
            copies.append((src.at[_rows(c * h, h)], dst.at[_rows(chip * r + c * h, h)], CHIP_FLIPS[f]))
    return copies


def _gather_d2d_plan(pos, srcs, lands):
    chip, c = 2 * pos[0] + pos[1], pos[2]
    copies = []
    for own, dst in zip(srcs, lands):
        r = own.shape[0]
        h = r // 2
        copies.append((own, dst.at[_rows(chip * r, r)], FLIP_C))
        for f in (1, 2, 3):
            at = _rows(lax.bitwise_xor(chip, f) * r + c * h, h)
            copies.append((dst.at[at], dst.at[at], FLIP_C))
    return copies


def gather_ici_start(shards, after=None):
    lands = [_landing((N_CHIPS * a.shape[0], D_MODEL), BF16) for a in shards]
    return exchange_start("gather_ici_start", shards, lands, 3 * len(shards), _gather_ici_plan, after=after)


def gather_d2d_start(shards, fulls, after=None):
    return exchange_start("gather_d2d_start", shards, fulls, 4 * len(shards), _gather_d2d_plan, after=after)


def _reduce_d2d_plan(pos, srcs, lands):
    c = pos[2]
    return [(src.at[:, _rows((1 - c) * (src.shape[1] // 2), src.shape[1] // 2)], dst, FLIP_C) for src, dst in zip(srcs, lands)]


def _reduce_ici_plan(pos, srcs, lands):
    chip = 2 * pos[0] + pos[1]
    return [(src.at[lax.bitwise_xor(chip, f)], dst.at[f - 1], CHIP_FLIPS[f]) for src, dst in zip(srcs, lands) for f in (1, 2, 3)]


def _reduce_swap_plan(pos, srcs, lands):
    c = pos[2]
    copies = []
    for dst in lands:
        h = dst.shape[0] // 2
        at = _rows(c * h, h)
        copies.append((dst.at[at], dst.at[at], FLIP_C))
    return copies


def reduce_d2d_start(grads):
    lands = [_landing((N_CHIPS, g.shape[1] // 2, D_MODEL), BF16) for g in grads]
    return exchange_start("reduce_d2d_start", grads, lands, len(grads), _reduce_d2d_plan)


def reduce_ici_start(chip_sums):
    lands = [_landing((N_CHIPS - 1,) + p.shape[1:], BF16) for p in chip_sums]
    return exchange_start("reduce_ici_start", chip_sums, lands, 3 * len(chip_sums), _reduce_ici_plan)


def reduce_swap_start(mine):
    return exchange_start("reduce_swap_start", [], mine, len(mine), _reduce_swap_plan)


def _by_shape(fn, *lists):
    groups, out = {}, [None] * len(lists[0])
    for k, a in enumerate(lists[0]):
        groups.setdefault(a.shape, []).append(k)
    for idx in groups.values():
        for k, r in zip(idx, fn(*[[l[k] for k in idx] for l in lists])):
            out[k] = r
    return out


def add_halves(ds, recvs, half, name):
    n = len(ds)
    nch, r, c = ds[0].shape
    h = r // 2

    def body(half_ref, *refs):
        for k in range(n):
            refs[2 * n + k][...] = (refs[k][...].astype(F32) + refs[n + k][...].astype(F32)).astype(BF16)

    mine = pl.BlockSpec((1, h, c), lambda j, hf: (j, hf[0], 0))
    whole = pl.BlockSpec((1, h, c), lambda j, hf: (j, 0, 0))
    return pl.pallas_call(
        body, name=name,
        grid_spec=pltpu.PrefetchScalarGridSpec(
            num_scalar_prefetch=1, grid=(nch,), in_specs=[mine] * n + [whole] * n, out_specs=[whole] * n),
        out_shape=[jax.ShapeDtypeStruct(rv.shape, BF16) for rv in recvs],
        compiler_params=_params(("parallel",)),
    )(half, *ds, *recvs)


def add_chips(ps, recvs, chip, name):
    n = len(ps)
    _, r, c = ps[0].shape

    def body(chip_ref, *refs):
        for k in range(n):
            acc = refs[k][0].astype(F32)
            for f in range(N_CHIPS - 1):
                acc = acc + refs[n + k][f].astype(F32)
            refs[2 * n + k][...] = acc

    return pl.pallas_call(
        body, name=name,
        grid_spec=pltpu.PrefetchScalarGridSpec(
            num_scalar_prefetch=1, grid=(1,),
            in_specs=[pl.BlockSpec((1, r, c), lambda i, ch: (ch[0], 0, 0))] * n +
                     [pl.BlockSpec((N_CHIPS - 1, r, c), lambda i, ch: (0, 0, 0))] * n,
            out_specs=[pl.BlockSpec((r, c), lambda i, ch: (ch[1], 0))] * n),
        out_shape=[jax.ShapeDtypeStruct((2 * r, c), F32)] * n,
        compiler_params=_params(("arbitrary",)),
    )(chip, *ps, *recvs)


def adamw_layers(w, gs, m, v, name):
    b, r, c = w.shape
    tr = max([t for t in range(8, min(r, 512) + 1, 8) if r % t == 0], default=r)

    def body(w_ref, m_ref, v_ref, *rest):
        g_refs, (g_ref, d_ref, nm_ref, nv_ref) = rest[:b], rest[b:]
        layer = pl.program_id(0)
        gv = g_refs[0][...]
        for l in range(1, b):
            gv = jnp.where(layer == l, g_refs[l][...], gv)
        nm = ADAM_B1 * m_ref[0] + (1.0 - ADAM_B1) * gv
        nv = ADAM_B2 * v_ref[0] + (1.0 - ADAM_B2) * (gv * gv)
        m_hat = nm / (1.0 - ADAM_B1 ** ADAM_STEP)
        v_hat = nv / (1.0 - ADAM_B2 ** ADAM_STEP)
        g_ref[0] = gv
        d_ref[0] = -ADAM_LR * (m_hat / (jnp.sqrt(v_hat) + ADAM_EPS) + ADAM_WD * w_ref[0])
        nm_ref[0] = nm
        nv_ref[0] = nv

    nr, tc = r // tr, (c if tr < r else _tile(c, 256))
    steps = nr * (c // tc)
    blk = pl.BlockSpec((1, tr, tc), lambda i, j: (i, j % nr, j // nr))
    g_specs = [pl.BlockSpec((tr, tc), lambda i, j, l=l: (jnp.where(i == l, j % nr, jnp.where(i < l, 0, nr - 1)),
                                                         jnp.where(i == l, j // nr, jnp.where(i < l, 0, c // tc - 1))))
               for l in range(b)]
    return pl.pallas_call(
        body, name=name, grid=(b, steps),
        in_specs=[blk] * 3 + g_specs, out_specs=[blk] * 4,
        out_shape=[jax.ShapeDtypeStruct(w.shape, F32)] * 4,
        compiler_params=_params(("arbitrary", "arbitrary")),
    )(w, m, v, *gs)


def _all_devices_plan(pos, srcs, lands):
    return [(srcs[0], lands[0].at[f], ((f >> 2) & 1, (f >> 1) & 1, f & 1)) for f in range(1, 8)]


def sum_devices(v, gathered, me, name):
    r, c = v.shape

    def body(me_ref, v_ref, g_ref, o_ref):
        own = v_ref[...]
        acc = None
        for d in range(8):
            slot = lax.bitwise_xor(me_ref[0], d)
            term = jnp.where(slot == 0, own, g_ref[slot])
            acc = term if acc is None else acc + term
        o_ref[...] = acc

    return pl.pallas_call(
        body, name=name,
        grid_spec=pltpu.PrefetchScalarGridSpec(
            num_scalar_prefetch=1, grid=(1,),
            in_specs=[pl.BlockSpec((r, c), lambda i, m: (0, 0)), pl.BlockSpec((8, r, c), lambda i, m: (0, 0, 0))],
            out_specs=pl.BlockSpec((r, c), lambda i, m: (0, 0))),
        out_shape=jax.ShapeDtypeStruct((r, c), F32),
        compiler_params=_params(("arbitrary",)),
    )(me, v, gathered)


_IN_SEGMENTS = ((0, 1544, 128), (128, 1800, 128), (256, 2056, 128), (384, 1672, 128), (512, 1928, 128), (640, 2184, 128),
                (OFF_Z, 0, SSD_WIDTH), (OFF_DT, 1536, SSD_HEADS), (OFF_XBC, 512, CONV_DIM), (OFF_P, 2312, POOL_WIDTH))


def _in_column_map():
    m = np.full((D_INP,), -1, np.int64)
    for at, orig, n in _IN_SEGMENTS:
        cols = np.arange(orig, orig + n)
        m[at:at + n] = (cols // COLS_IN) * SHARD_IN + cols % COLS_IN
    return m


def take_rows(a, idx, name):
    dep, r_in, c = a.shape
    blk = 2 * LANES if len(idx) % (2 * LANES) == 0 and r_in % (2 * LANES) == 0 else LANES
    n_out, n_in = len(idx) // blk, r_in // blk
    assert len(idx) % blk == 0 and r_in % blk == 0
    sources = [sorted({int(v) // blk for v in idx[blk * i:blk * (i + 1)] if v >= 0}) for i in range(n_out)]
    width = max(len(s) for s in sources)
    table = np.zeros((n_out, width), np.int32)
    for i, s in enumerate(sources):
        spare = [b for b in range(n_in) if b not in s][:width - len(s)]
        table[i] = s + spare

    def body(tbl_ref, idx_ref, *refs):
        in_refs, o_ref = refs[:width], refs[width]
        i = pl.program_id(1)
        src = idx_ref[...]
        acc = jnp.zeros((blk, c), F32)
        for k in range(width):
            pick = (src == tbl_ref[i, k] * blk + _iota2((blk, blk), 1)).astype(BF16)
            acc = acc + _nn(pick, in_refs[k][0])
        o_ref[0] = acc.astype(BF16)

    return pl.pallas_call(
        body, name=name,
        grid_spec=pltpu.PrefetchScalarGridSpec(
            num_scalar_prefetch=1, grid=(dep, n_out),
            in_specs=[pl.BlockSpec((blk, 1), lambda l, i, t: (i, 0))] +
                     [pl.BlockSpec((1, blk, c), lambda l, i, t, k=k: (l, t[i, k], 0)) for k in range(width)],
            out_specs=pl.BlockSpec((1, blk, c), lambda l, i, t: (l, i, 0))),
        out_shape=jax.ShapeDtypeStruct((dep, len(idx), c), BF16),
        compiler_params=_params(("parallel", "parallel")),
    )(jnp.asarray(table), jnp.asarray(np.asarray(idx, np.int32).reshape(-1, 1)), *([a] * width))


def _in_weight_layout(staged):
    return take_rows(staged, _in_column_map(), "w_in_layout")


def _in_gradient_layout(dwt):
    fwd = _in_column_map()
    inv = np.full((N_CHIPS * SHARD_IN,), -1, np.int64)
    inv[fwd[fwd >= 0]] = np.nonzero(fwd >= 0)[0]
    return take_rows(dwt, inv, "dw_in_layout")


SMALL_NAMES = ("norm1_w", "conv_w", "conv_b", "dt_bias", "a_log", "d_skip", "ssd_norm_w", "pool_w", "pool_b",
               "pool_scale", "norm2_w", "final_norm_w")
SMALL_ROWS = 160


def _small_rows(shape):
    return -(-int(np.prod(shape)) // (8 * D_MODEL)) * 8


def _pack_small(parts):
    rows = []
    for p in parts:
        flat = p.reshape(-1)
        rows.append(jnp.pad(flat, (0, _small_rows(p.shape) * D_MODEL - flat.shape[0])).reshape(-1, D_MODEL))
    used = sum(r.shape[0] for r in rows)
    return jnp.concatenate(rows + [jnp.zeros((SMALL_ROWS - used, D_MODEL), F32)], axis=0)


def _unpack_small(packed, shapes):
    out, at = [], 0
    for shp in shapes:
        n, r = int(np.prod(shp)), _small_rows(shp)
        out.append(packed[at:at + r].reshape(-1)[:n].reshape(shp))
        at += r
    return out


def kernel(x, norm1_w, w_in, conv_w, conv_b, dt_bias, a_log, d_skip, ssd_norm_w, pool_w, pool_b, pool_scale, w_out, norm2_w, w_gate, w_up, w_down, final_norm_w, loss_target, m_norm1_w, m_w_in, m_conv_w, m_conv_b, m_dt_bias, m_a_log, m_d_skip, m_ssd_norm_w, m_pool_w, m_pool_b, m_pool_scale, m_w_out, m_norm2_w, m_w_gate, m_w_up, m_w_down, m_final_norm_w, v_norm1_w, v_w_in, v_conv_w, v_conv_b, v_dt_bias, v_a_log, v_d_skip, v_ssd_norm_w, v_pool_w, v_pool_b, v_pool_scale, v_w_out, v_norm2_w, v_w_gate, v_w_up, v_w_down, v_final_norm_w):
    px, py, pc = _position()
    chip = 2 * px + py
    chip_arr = jnp.reshape(chip, (1,)).astype(jnp.int32)
    half_arr = jnp.reshape(pc, (1,)).astype(jnp.int32)

    def layer_shards(l):
        w_in_t = jnp.pad(jnp.swapaxes(w_in[l], 0, 1).astype(BF16), ((0, SHARD_IN - COLS_IN), (0, 0)))
        return [w_in_t, w_out[l].astype(BF16), jnp.swapaxes(w_gate[l], 0, 1).astype(BF16),
                jnp.swapaxes(w_up[l], 0, 1).astype(BF16), w_down[l].astype(BF16)]

    shards0 = layer_shards(0)
    head = gather_ici_start(shards0[:1])
    over_ici = {}

    def pass_on(l, after):
        own, arrived = exchange_wait("gather_ici_wait", over_ici[l], after, _gather_ici_plan)
        swap = gather_d2d_start(own, arrived)
        tokens = [swap[4]]
        if l + 1 < DEPTH:
            over_ici[l + 1] = gather_ici_start(layer_shards(l + 1), after=swap[4])
            tokens.append(over_ici[l + 1][4])
        return swap, tokens

    def weights_of(swap, after):
        _, (w_in_st, w_out_l, w_gate_t, w_up_t, w_down_l) = exchange_wait("gather_d2d_wait", swap, after, _gather_d2d_plan)
        return _in_weight_layout(w_in_st[None])[0], w_out_l, w_gate_t, w_up_t, w_down_l

    pad_heads = lambda v: jnp.pad(v, ((0, 0), (0, LANES - SSD_HEADS)))[:, None, :]
    dtb, alog = pad_heads(dt_bias), pad_heads(a_log)
    dskip_x = jnp.repeat(d_skip, HEAD_DIM, axis=1)[:, None, :]
    eye = jnp.eye(len(POOL_WINDOWS), dtype=F32)
    wbd = (pool_w[:, :, :, None, :] * eye[None, :, None, :, None]).reshape(DEPTH, POOL_WIDTH, POOL_WIDTH).astype(BF16)
    pool_b2 = pool_b.reshape(DEPTH, 1, POOL_WIDTH)
    cw_cols = lax.dynamic_update_slice(jnp.zeros((DEPTH, CONV_WIDTH, CONV_DIM), F32), conv_w,
                                       (0, 0, chip * (CONV_DIM // N_CHIPS)))
    cw_cols = jnp.where(pc == 0, cw_cols, 0.0)
    cw_rows = (DEPTH * CONV_WIDTH * CONV_DIM) // D_MODEL
    me_arr = jnp.reshape(4 * px + 2 * py + pc, (1,)).astype(jnp.int32)
    cw_cols = jnp.pad(cw_cols.reshape(cw_rows, D_MODEL), ((0, 8), (0, 0)))
    cw_start = exchange_start("gather_conv_w_start", [cw_cols], [_landing((8,) + cw_cols.shape, F32)], 7,
                              _all_devices_plan, after=head[4])

    h = x[0]
    saved, weights = [], []
    own, arrived = exchange_wait("gather_ici_wait", head, [head[4], cw_start[4]] + shards0[1:], _gather_ici_plan)
    head = gather_d2d_start(own, arrived)
    over_ici[0] = gather_ici_start(shards0[1:], after=head[4])
    w_in_f = _in_weight_layout(exchange_wait("gather_d2d_wait", head, [head[4], over_ici[0][4]], _gather_d2d_plan)[1][0][None])[0]
    for l in range(DEPTH):
        if l > 0:
            w_in_f, w_out_f, w_gate_t, w_up_t, w_down_f = weights[l]
        proj = rms_matmul(h, norm1_w[l][None], w_in_f, "in_proj")
        if l == 0:
            (cw_own,), (cw_all,) = exchange_wait("gather_conv_w_wait", cw_start, proj, _all_devices_plan)
            conv_w_f = sum_devices(cw_own, cw_all, me_arr, "gather_conv_w_sum")[:cw_rows].reshape(DEPTH, CONV_WIDTH, CONV_DIM)
            cw8 = jnp.pad(conv_w_f, ((0, 0), (0, 8 - CONV_WIDTH), (0, 0)))
        xc = conv_forward(proj, cw8[l], conv_b[l][None], "conv_fwd")
        y_all, ycore, states = ssd_forward(proj, xc, dtb[l], alog[l], dskip_x[l], ssd_norm_w[l][None], "ssd_fwd")
        if l == 0:
            swap, tokens = pass_on(0, y_all)
            y_all, o_sb = sb_forward(proj, y_all, "sb_fwd", after=tokens)
            tokens = None
        else:
            y_all, o_sb = sb_forward(proj, y_all, "sb_fwd")
            swap, tokens = pass_on(l + 1, o_sb) if l + 1 < DEPTH else (None, None)
        y_all = pool_forward(proj, wbd[l], pool_b2[l], pool_scale[l][None], y_all, "pool_fwd", after=tokens)
        if l == 0:
            w_out_f, w_gate_t, w_up_t, w_down_f = exchange_wait("gather_d2d_wait", swap, y_all, _gather_d2d_plan)[1]
            weights.append((w_in_f, w_out_f, w_gate_t, w_up_t, w_down_f))
        x1 = matmul_residual(y_all, w_out_f, h, "out_proj")
        x2, g, u = ffn_forward(x1, norm2_w[l][None], w_gate_t, w_up_t, w_down_f, "ffn_fwd")
        if l == 0:
            swap, tokens = pass_on(1, x2)
            weights.append(weights_of(swap, tokens))
        elif swap is not None:
            weights.append(weights_of(swap, x2))
        saved.append((h, proj, xc, ycore, states, o_sb, y_all, x1, g, u))
        h = x2

    loss_part, dx, dxb, d_final = loss_head(h, final_norm_w[None], loss_target[0], "loss_head")
    loss = lax.psum(loss_part[0, 0], ("x", "y", "c"))

    small = {n: [None] * DEPTH for n in SMALL_NAMES if n != "final_norm_w"}
    chip_half = jnp.concatenate([chip_arr, half_arr])
    reduced = {}
    d2d = ici = early = None

    def add_cores(d2d, after):
        mine, theirs = exchange_wait("reduce_d2d_wait", d2d[1], after, _reduce_d2d_plan)
        return d2d[0], reduce_ici_start(_by_shape(lambda ds, ts: add_halves(ds, ts, half_arr, "reduce_add_halves"), mine, theirs))

    def add_all(ici, after):
        sums, theirs = exchange_wait("reduce_ici_wait", ici[1], after, _reduce_ici_plan)
        return ici[0], reduce_swap_start(_by_shape(lambda ps, ts: add_chips(ps, ts, chip_half, "reduce_add_chips"), sums, theirs))

    def finish(swap, after):
        reduced[swap[0]] = exchange_wait("reduce_swap_wait", swap[1], after, _reduce_swap_plan)[1]

    swaps = []
    for l in reversed(range(DEPTH)):
        xin, proj, xc, ycore, states, o_sb, y_all, x1, g, u = saved[l]
        w_in_f, w_out_f, w_gate_t, w_up_t, w_down_f = weights[l]
        dg, du, act = ffn_backward_act(dxb, g, u, w_down_f, "ffn_bwd_act", after=None if d2d is None else d2d[1][4])
        dx1, dx1b, h2, dn2 = rms_backward([dg, du], [w_gate_t, w_up_t], x1, norm2_w[l][None], dx, "ffn_bwd_norm", 512)
        if d2d is not None:
            ici = add_cores(d2d, dx1b)
        dyall = matmul_nt(dx1b, w_out_f, "out_proj_bwd", after=None if ici is None else ici[1][4])
        dw_down = matmul_tn(act, dxb, "dw_down")
        dw_gate = matmul_tn(dg, h2, "dw_gate")
        dw_up = matmul_tn(du, h2, "dw_up")
        dw_out = matmul_tn(y_all, dx1b, "dw_out")
        late = [dw.reshape(N_CHIPS, r, D_MODEL) for dw, r in zip((dw_out, dw_gate, dw_up, dw_down), SHARD_ROWS[1:])]
        if l == 0:
            early = ("0 late", reduce_d2d_start(late))
        dxc, dproj, dsn, ddsk, ddtb, dalog = ssd_backward(proj, xc, ycore, dyall, states, dtb[l], alog[l],
                                                          dskip_x[l], ssd_norm_w[l][None], "ssd_bwd")
        dproj, dcw, dcb = conv_backward(proj, dxc, cw8[l], conv_b[l][None], dproj, "conv_bwd",
                                        after=None if early is None else early[1][4])
        if early is not None:
            early = add_cores(early, dproj)
        dproj = sb_backward(proj, o_sb, dyall, dproj, "sb_bwd", after=None if early is None else early[1][4])
        dproj, dwbd, dpb, dps = pool_backward(proj, dyall, wbd[l], pool_b2[l], pool_scale[l][None], dproj, "pool_bwd")
        if ici is not None:
            swaps.append(add_all(ici, dproj))
            ici = None
        dx, dxb, h1, dn1 = rms_backward([dproj], [w_in_f], xin, norm1_w[l][None], dx1, "in_proj_bwd", 512,
                                        after=swaps[-1][1][4] if swaps else None)
        dw_in = _in_gradient_layout(matmul_tn(dproj, h1, "dw_in")[None])[0].reshape(N_CHIPS, SHARD_IN, D_MODEL)
        d2d = (l, reduce_d2d_start([dw_in] if l == 0 else [dw_in] + late))
        small["norm1_w"][l] = dn1[0]
        small["conv_w"][l] = dcw[:CONV_WIDTH]
        small["conv_b"][l] = dcb[0]
        small["dt_bias"][l] = ddtb[0, :SSD_HEADS]
        small["a_log"][l] = dalog[0, :SSD_HEADS]
        small["d_skip"][l] = ddsk.reshape(SSD_HEADS, HEAD_DIM).sum(axis=1)
        small["ssd_norm_w"][l] = dsn[0]
        small["pool_w"][l] = jnp.stack([dwbd[64 * k:64 * k + 64, 64 * k:64 * k + 64] for k in range(len(POOL_WINDOWS))])
        small["pool_b"][l] = dpb.reshape(len(POOL_WINDOWS), -1)
        small["pool_scale"][l] = dps[0]
        small["norm2_w"][l] = dn2[0]
    grad_x = dx[None]

    ici = add_cores(d2d, d2d[1][4])
    small_parts = [d_final if n == "final_norm_w" else jnp.stack(small[n]) for n in SMALL_NAMES]
    small_start = exchange_start("reduce_small_start", [_pack_small(small_parts)],
                                 [_landing((8, SMALL_ROWS, D_MODEL), F32)], 7, _all_devices_plan, after=ici[1][4])
    swaps.append(add_all(early, small_start[4]))
    swaps.append(add_all(ici, swaps[-1][1][4]))
    for swap in swaps:
        finish(swap, swaps[-1][1][4])
    (small_own,), (small_all,) = exchange_wait("reduce_small_wait", small_start, reduced[0][0], _all_devices_plan)
    small_sum = sum_devices(small_own, small_all, me_arr, "reduce_small_sum")
    reduced[0] = reduced[0] + reduced["0 late"]
    g_big = {n: [reduced[l][k] for l in range(DEPTH)] for k, n in enumerate(("w_in", "w_out", "w_gate", "w_up", "w_down"))}
    g_big["w_in"] = [gl[:COLS_IN] for gl in g_big["w_in"]]
    transposed = ("w_in", "w_gate", "w_up")

    g_small = dict(zip(SMALL_NAMES, _unpack_small(small_sum, [p.shape for p in small_parts])))
    g_small["final_norm_w"] = g_small["final_norm_w"].reshape(final_norm_w.shape)
    g_small["conv_w"] = lax.dynamic_slice_in_dim(g_small["conv_w"], chip * (CONV_DIM // N_CHIPS), CONV_DIM // N_CHIPS, axis=2)

    given = dict(norm1_w=(norm1_w, m_norm1_w, v_norm1_w), w_in=(w_in, m_w_in, v_w_in), conv_w=(conv_w, m_conv_w, v_conv_w),
                 conv_b=(conv_b, m_conv_b, v_conv_b), dt_bias=(dt_bias, m_dt_bias, v_dt_bias), a_log=(a_log, m_a_log, v_a_log),
                 d_skip=(d_skip, m_d_skip, v_d_skip), ssd_norm_w=(ssd_norm_w, m_ssd_norm_w, v_ssd_norm_w),
                 pool_w=(pool_w, m_pool_w, v_pool_w), pool_b=(pool_b, m_pool_b, v_pool_b),
                 pool_scale=(pool_scale, m_pool_scale, v_pool_scale), w_out=(w_out, m_w_out, v_w_out),
                 norm2_w=(norm2_w, m_norm2_w, v_norm2_w), w_gate=(w_gate, m_w_gate, v_w_gate), w_up=(w_up, m_w_up, v_w_up),
                 w_down=(w_down, m_w_down, v_w_down), final_norm_w=(final_norm_w, m_final_norm_w, v_final_norm_w))
    order = ("norm1_w", "w_in", "conv_w", "conv_b", "dt_bias", "a_log", "d_skip", "ssd_norm_w", "pool_w", "pool_b",
             "pool_scale", "w_out", "norm2_w", "w_gate", "w_up", "w_down", "final_norm_w")
    grads = dict(g_small)
    results = {}
    for n in ("w_in", "w_out", "w_gate", "w_up", "w_down"):
        w, m, v = given[n]
        if n in transposed:
            out = adamw_layers(jnp.swapaxes(w, 1, 2), g_big[n], jnp.swapaxes(m, 1, 2), jnp.swapaxes(v, 1, 2), "adamw_" + n)
            out = [jnp.swapaxes(o, 1, 2) for o in out]
        else:
            out = adamw_layers(w, g_big[n], m, v, "adamw_" + n)
        grads[n], results[n] = out[0], tuple(out[1:])
    small_shapes = [given[n][0].shape for n in SMALL_NAMES]
    packed = [_pack_small([given[n][k] for n in SMALL_NAMES])[None] for k in range(3)]
    packed_g = _pack_small([grads[n] for n in SMALL_NAMES])[None]
    small_out = adamw(packed[0], packed_g, packed[1], packed[2], "adamw_small")
    small_out = [_unpack_small(o[0], small_shapes) for o in small_out]
    for i, n in enumerate(SMALL_NAMES):
        results[n] = tuple(small_out[k][i] for k in range(3))

    return (loss, grad_x, *[grads[n] for n in order], *[results[n][0] for n in order],
            *[results[n][1] for n in order], *[results[n][2] for n in order])
```

```python
import numpy as np
import jax
import jax.numpy as jnp
from jax import lax
from jax.experimental import pallas as pl
from jax.experimental.pallas import tpu as pltpu

F32 = jnp.float32
BF16 = jnp.bfloat16
MESH = pl.DeviceIdType.MESH
ANY = pl.BlockSpec(memory_space=pl.ANY)

D_MODEL = 1024
DEPTH = 4
EPS = 1e-6
SSD_WIDTH = 512
SSD_HEADS = 8
HEAD_DIM = 64
D_STATE = 128
CHUNK = 128
CONV_WIDTH = 4
CONV_DIM = 1024
SB_WIDTH = 256
POOL_WIDTH = 256
POOL_WINDOWS = (2, 4, 8, 16)
D_FF = 2816
D_IN = 2568
N_CHIPS = 4
OFF_QKV, OFF_Z, OFF_DT, OFF_XBC, OFF_P = 0, 768, 1280, 1536, 2560
D_INP = 2816
ZDT = 768
SHARD_IN, SHARD_OUT, SHARD_FF = 704, 256, 704
COLS_IN = 642
ADAM_LR, ADAM_B1, ADAM_B2, ADAM_EPS, ADAM_WD, ADAM_STEP = 0.001, 0.9, 0.999, 1e-08, 0.01, 10
LANES = 128
VMEM_LIMIT = 56 * 1024 * 1024


def _params(sem=None):
    return pltpu.CompilerParams(dimension_semantics=sem, vmem_limit_bytes=VMEM_LIMIT)


def _tile(n, cap):
    best = None
    for t in range(LANES, min(n, cap) + 1, LANES):
        if n % t == 0:
            best = t
    assert best is not None, (n, cap)
    return best


def _nt(a, b):
    return lax.dot_general(a, b, (((1,), (1,)), ((), ())), preferred_element_type=F32)


def _tn(a, b):
    return lax.dot_general(a, b, (((0,), (0,)), ((), ())), preferred_element_type=F32)


def _nn(a, b):
    return jnp.dot(a, b, preferred_element_type=F32)


def _split_dot(a, b_exact, terms=3, dot=_nn):
    acc = None
    rest = a
    for _ in range(terms):
        hi = rest.astype(BF16)
        part = dot(hi, b_exact)
        acc = part if acc is None else acc + part
        rest = rest - hi.astype(F32)
    return acc


def _split_dot_left(a_exact, b, terms=3):
    acc = None
    rest = b
    for _ in range(terms):
        hi = rest.astype(BF16)
        part = _nn(a_exact, hi)
        acc = part if acc is None else acc + part
        rest = rest - hi.astype(F32)
    return acc


def _sigmoid(x):
    return 1.0 / (1.0 + jnp.exp(-x))


def _softplus(x):
    return jnp.maximum(x, 0.0) + jnp.log(1.0 + jnp.exp(-jnp.abs(x)))


def _iota2(shape, dim):
    return lax.broadcasted_iota(jnp.int32, shape, dim)


def _after(after):
    ops = [] if after is None else list(after) if isinstance(after, (list, tuple)) else [after]
    return [ANY] * len(ops), ops


def rms_matmul(x, nw, wt, name, after=None):
    s, d = x.shape
    n = wt.shape[0]
    tm, tn = _tile(s, 512), _tile(n, 2816)
    specs, ops = _after(after)

    def body(x_ref, nw_ref, w_ref, *rest):
        o_ref, h_ref = rest[len(ops):]

        @pl.when(pl.program_id(1) == 0)
        def _():
            xv = x_ref[...]
            r = lax.rsqrt(jnp.mean(xv * xv, axis=-1, keepdims=True) + EPS)
            h_ref[...] = (xv * r * nw_ref[...]).astype(BF16)
        o_ref[...] = _nt(h_ref[...], w_ref[...])

    return pl.pallas_call(
        body, name=name, grid=(s // tm, n // tn),
        in_specs=[pl.BlockSpec((tm, d), lambda i, j: (i, 0)), pl.BlockSpec((1, d), lambda i, j: (0, 0)),
                  pl.BlockSpec((tn, d), lambda i, j: (j, 0))] + specs,
        out_specs=pl.BlockSpec((tm, tn), lambda i, j: (i, j)),
        out_shape=jax.ShapeDtypeStruct((s, n), F32),
        scratch_shapes=[pltpu.VMEM((tm, d), BF16)],
        compiler_params=_params(("parallel", "arbitrary")),
    )(x, nw, wt, *ops)


def matmul_residual(a, w, res, name):
    s, k = a.shape
    n = w.shape[1]
    tm, tn = _tile(s, 512), _tile(n, 1024)

    def body(a_ref, w_ref, r_ref, o_ref):
        o_ref[...] = r_ref[...] + _nn(a_ref[...], w_ref[...])

    return pl.pallas_call(
        body, name=name, grid=(s // tm, n // tn),
        in_specs=[pl.BlockSpec((tm, k), lambda i, j: (i, 0)), pl.BlockSpec((k, tn), lambda i, j: (0, j)),
                  pl.BlockSpec((tm, tn), lambda i, j: (i, j))],
        out_specs=pl.BlockSpec((tm, tn), lambda i, j: (i, j)),
        out_shape=jax.ShapeDtypeStruct((s, n), F32),
        compiler_params=_params(("parallel", "parallel")),
    )(a, w, res)


def matmul_nt(a, w, name, out_dtype=F32, after=None):
    s, n = a.shape
    k = w.shape[0]
    tm, tk = _tile(s, 512), _tile(k, 1024)
    specs, ops = _after(after)

    def body(a_ref, w_ref, *rest):
        rest[-1][...] = _nt(a_ref[...], w_ref[...]).astype(out_dtype)

    return pl.pallas_call(
        body, name=name, grid=(s // tm, k // tk),
        in_specs=[pl.BlockSpec((tm, n), lambda i, j: (i, 0)), pl.BlockSpec((tk, n), lambda i, j: (j, 0))] + specs,
        out_specs=pl.BlockSpec((tm, tk), lambda i, j: (i, j)),
        out_shape=jax.ShapeDtypeStruct((s, k), out_dtype),
        compiler_params=_params(("parallel", "parallel")),
    )(a, w, *ops)


def matmul_tn(a, b, name, after=None):
    s, m = a.shape
    n = b.shape[1]
    tm, tn = _tile(m, 512), _tile(n, 1024)

    def body(a_ref, b_ref, *rest):
        rest[-1][...] = _tn(a_ref[...], b_ref[...]).astype(BF16)

    specs, ops = _after(after)
    return pl.pallas_call(
        body, name=name, grid=(m // tm, n // tn),
        in_specs=[pl.BlockSpec((s, tm), lambda i, j: (0, i)), pl.BlockSpec((s, tn), lambda i, j: (0, j))] + specs,
        out_specs=pl.BlockSpec((tm, tn), lambda i, j: (i, j)),
        out_shape=jax.ShapeDtypeStruct((m, n), BF16),
        compiler_params=_params(("parallel", "parallel")),
    )(a, b, *ops)


def ffn_forward(x1, nw, wgt, wut, wd, name):
    s, d = x1.shape
    f = wgt.shape[0]
    tm, tf = _tile(s, 2048), _tile(f, 256)

    def body(x_ref, nw_ref, wg_ref, wu_ref, wd_ref, o_ref, g_ref, u_ref, h_ref):
        j = pl.program_id(1)

        @pl.when(j == 0)
        def _():
            xv = x_ref[...]
            r = lax.rsqrt(jnp.mean(xv * xv, axis=-1, keepdims=True) + EPS)
            h_ref[...] = (xv * r * nw_ref[...]).astype(BF16)
            o_ref[...] = xv

        h = h_ref[...]
        g = _nt(h, wg_ref[...])
        u = _nt(h, wu_ref[...])
        g_ref[...] = g.astype(BF16)
        u_ref[...] = u.astype(BF16)
        a = (g * _sigmoid(g) * u).astype(BF16)
        o_ref[...] += _nn(a, wd_ref[...])

    wblk = pl.BlockSpec((tf, d), lambda i, j: (j, 0))
    return pl.pallas_call(
        body, name=name, grid=(s // tm, f // tf),
        in_specs=[pl.BlockSpec((tm, d), lambda i, j: (i, 0), pipeline_mode=pl.Buffered(1)),
                  pl.BlockSpec((1, d), lambda i, j: (0, 0)), wblk, wblk, wblk],
        out_specs=[pl.BlockSpec((tm, d), lambda i, j: (i, 0)), pl.BlockSpec((tm, tf), lambda i, j: (i, j)),
                   pl.BlockSpec((tm, tf), lambda i, j: (i, j))],
        out_shape=[jax.ShapeDtypeStruct((s, d), F32), jax.ShapeDtypeStruct((s, f), BF16),
                   jax.ShapeDtypeStruct((s, f), BF16)],
        scratch_shapes=[pltpu.VMEM((tm, d), BF16)],
        compiler_params=_params(("parallel", "arbitrary")),
    )(x1, nw, wgt, wut, wd)


def ffn_backward_act(dx2, g, u, wd, name, after=None):
    s, d = dx2.shape
    f = wd.shape[0]
    tm, tf = _tile(s, 256), _tile(f, 2816)
    specs, ops = _after(after)

    def body(dx_ref, g_ref, u_ref, wd_ref, *rest):
        dg_ref, du_ref, a_ref = rest[len(ops):]
        da = _nt(dx_ref[...], wd_ref[...])
        gv = g_ref[...].astype(F32)
        uv = u_ref[...].astype(F32)
        sg = _sigmoid(gv)
        silu = gv * sg
        dg_ref[...] = (da * uv * (sg * (1.0 + gv * (1.0 - sg)))).astype(BF16)
        du_ref[...] = (da * silu).astype(BF16)
        a_ref[...] = (silu * uv).astype(BF16)

    blk = pl.BlockSpec((tm, tf), lambda i, j: (i, j))
    return pl.pallas_call(
        body, name=name, grid=(s // tm, f // tf),
        in_specs=[pl.BlockSpec((tm, d), lambda i, j: (i, 0)), blk, blk, pl.BlockSpec((tf, d), lambda i, j: (j, 0))] + specs,
        out_specs=[blk, blk, blk],
        out_shape=[jax.ShapeDtypeStruct((s, f), BF16)] * 3,
        compiler_params=_params(("parallel", "parallel")),
    )(dx2, g, u, wd, *ops)


def rms_backward(dzs, wts, x, nw, dres, name, tm, after=None):
    s, d = x.shape
    nz = len(dzs)
    specs, ops = _after(after)

    def body(*refs):
        dz_refs, w_refs = refs[:nz], refs[nz:2 * nz]
        x_ref, nw_ref, dres_ref = refs[2 * nz:2 * nz + 3]
        dx_ref, dxb_ref, h_ref, dnw_ref = refs[2 * nz + 3 + len(ops):]
        dh = _nn(dz_refs[0][...], w_refs[0][...])
        for k in range(1, nz):
            dh = dh + _nn(dz_refs[k][...], w_refs[k][...])
        xv = x_ref[...]
        r = lax.rsqrt(jnp.mean(xv * xv, axis=-1, keepdims=True) + EPS)
        xhat = xv * r
        nwv = nw_ref[...]
        h_ref[...] = (xhat * nwv).astype(BF16)

        @pl.when(pl.program_id(0) == 0)
        def _():
            dnw_ref[...] = jnp.zeros_like(dnw_ref)

        dnw_ref[...] += jnp.sum(dh * xhat, axis=0, keepdims=True)
        gdh = dh * nwv
        dx = dres_ref[...] + r * (gdh - xhat * jnp.mean(gdh * xhat, axis=-1, keepdims=True))
        dx_ref[...] = dx
        dxb_ref[...] = dx.astype(BF16)

    row = pl.BlockSpec((tm, d), lambda i: (i, 0))
    in_specs = [pl.BlockSpec((tm, dz.shape[1]), lambda i: (i, 0)) for dz in dzs]
    in_specs += [pl.BlockSpec(w.shape, lambda i: (0, 0), pipeline_mode=pl.Buffered(1)) for w in wts]
    in_specs += [row, pl.BlockSpec((1, d), lambda i: (0, 0)), row] + specs
    return pl.pallas_call(
        body, name=name, grid=(s // tm,),
        in_specs=in_specs,
        out_specs=[row, row, row, pl.BlockSpec((1, d), lambda i: (0, 0))],
        out_shape=[jax.ShapeDtypeStruct((s, d), F32), jax.ShapeDtypeStruct((s, d), BF16),
                   jax.ShapeDtypeStruct((s, d), BF16), jax.ShapeDtypeStruct((1, d), F32)],
        compiler_params=_params(("arbitrary",)),
    )(*dzs, *wts, x, nw, dres, *ops)


def loss_head(x, nw, target, name):
    s, d = x.shape
    tm = _tile(s, 512)

    def body(x_ref, nw_ref, t_ref, loss_ref, dx_ref, dxb_ref, dnw_ref):
        xv = x_ref[...]
        r = lax.rsqrt(jnp.mean(xv * xv, axis=-1, keepdims=True) + EPS)
        xhat = xv * r
        nwv = nw_ref[...]
        err = xhat * nwv - t_ref[...]

        @pl.when(pl.program_id(0) == 0)
        def _():
            dnw_ref[...] = jnp.zeros_like(dnw_ref)
            loss_ref[...] = jnp.zeros_like(loss_ref)

        part = jnp.sum(jnp.sum(err * err, axis=-1, keepdims=True), axis=0, keepdims=True) * (0.5 / d)
        loss_ref[...] += jnp.broadcast_to(part, loss_ref.shape)
        dout = err * (1.0 / d)
        dnw_ref[...] += jnp.sum(dout * xhat, axis=0, keepdims=True)
        gdh = dout * nwv
        dx = r * (gdh - xhat * jnp.mean(gdh * xhat, axis=-1, keepdims=True))
        dx_ref[...] = dx
        dxb_ref[...] = dx.astype(BF16)

    row = pl.BlockSpec((tm, d), lambda i: (i, 0))
    return pl.pallas_call(
        body, name=name, grid=(s // tm,),
        in_specs=[row, pl.BlockSpec((1, d), lambda i: (0, 0)), row],
        out_specs=[pl.BlockSpec((1, LANES), lambda i: (0, 0)), row, row, pl.BlockSpec((1, d), lambda i: (0, 0))],
        out_shape=[jax.ShapeDtypeStruct((1, LANES), F32), jax.ShapeDtypeStruct((s, d), F32),
                   jax.ShapeDtypeStruct((s, d), BF16), jax.ShapeDtypeStruct((1, d), F32)],
        compiler_params=_params(("arbitrary",)),
    )(x, nw, target)


def _shift_down(x, k):
    return jnp.where(_iota2(x.shape, 0) >= k, pltpu.roll(x, k, axis=0), 0.0)


def _shift_up(x, k):
    s = x.shape[0]
    return jnp.where(_iota2(x.shape, 0) < s - k, pltpu.roll(x, s - k, axis=0), 0.0)


CONV_TILE = 256


def conv_forward(proj, cw, cb, name):
    s = proj.shape[0]
    tn = CONV_TILE
    off = OFF_XBC // tn

    def body(u_ref, w_ref, b_ref, o_ref):
        u = u_ref[...]
        pre = b_ref[...] + w_ref[CONV_WIDTH - 1:CONV_WIDTH, :] * u
        for i in range(CONV_WIDTH - 1):
            pre = pre + w_ref[i:i + 1, :] * _shift_down(u, CONV_WIDTH - 1 - i)
        o_ref[...] = pre * _sigmoid(pre)

    return pl.pallas_call(
        body, name=name, grid=(CONV_DIM // tn,),
        in_specs=[pl.BlockSpec((s, tn), lambda j: (0, off + j)), pl.BlockSpec((8, tn), lambda j: (0, j)),
                  pl.BlockSpec((1, tn), lambda j: (0, j))],
        out_specs=pl.BlockSpec((s, tn), lambda j: (0, j)),
        out_shape=jax.ShapeDtypeStruct((s, CONV_DIM), F32),
        compiler_params=_params(("parallel",)),
    )(proj, cw, cb)


def conv_backward(proj, dxc, cw, cb, dproj, name, after=None):
    s = proj.shape[0]
    tn = CONV_TILE
    off = OFF_XBC // tn

    specs, ops = _after(after)

    def body(u_ref, d_ref, w_ref, b_ref, *rest):
        du_ref, dw_ref, db_ref = rest[-3:]
        u = u_ref[...]
        shifted = [_shift_down(u, CONV_WIDTH - 1 - i) for i in range(CONV_WIDTH - 1)] + [u]
        pre = b_ref[...] + w_ref[CONV_WIDTH - 1:CONV_WIDTH, :] * u
        for i in range(CONV_WIDTH - 1):
            pre = pre + w_ref[i:i + 1, :] * shifted[i]
        sg = _sigmoid(pre)
        dpre = d_ref[...] * (sg * (1.0 + pre * (1.0 - sg)))
        du = w_ref[CONV_WIDTH - 1:CONV_WIDTH, :] * dpre
        for i in range(CONV_WIDTH - 1):
            du = du + w_ref[i:i + 1, :] * _shift_up(dpre, CONV_WIDTH - 1 - i)
        du_ref[...] = du.astype(BF16)
        rows = [jnp.sum(dpre * shifted[i], axis=0, keepdims=True) for i in range(CONV_WIDTH)]
        rows.append(jnp.zeros((8 - CONV_WIDTH, tn), F32))
        dw_ref[...] = jnp.concatenate(rows, axis=0)
        db_ref[...] = jnp.sum(dpre, axis=0, keepdims=True)

    return pl.pallas_call(
        body, name=name, grid=(CONV_DIM // tn,),
        in_specs=[pl.BlockSpec((s, tn), lambda j: (0, off + j)), pl.BlockSpec((s, tn), lambda j: (0, j)),
                  pl.BlockSpec((8, tn), lambda j: (0, j)), pl.BlockSpec((1, tn), lambda j: (0, j)), ANY] + specs,
        out_specs=[pl.BlockSpec((s, tn), lambda j: (0, off + j)), pl.BlockSpec((8, tn), lambda j: (0, j)),
                   pl.BlockSpec((1, tn), lambda j: (0, j))],
        out_shape=[jax.ShapeDtypeStruct(dproj.shape, BF16), jax.ShapeDtypeStruct((8, CONV_DIM), F32),
                   jax.ShapeDtypeStruct((1, CONV_DIM), F32)],
        input_output_aliases={4: 0},
        compiler_params=_params(("parallel",)),
    )(proj, dxc, cw, cb, dproj, *ops)


def _pool_lane_window(shape):
    grp = _iota2(shape, 1) // (POOL_WIDTH // len(POOL_WINDOWS))
    win = jnp.full(shape, POOL_WINDOWS[-1], jnp.int32)
    for gi in range(len(POOL_WINDOWS) - 2, -1, -1):
        win = jnp.where(grp == gi, POOL_WINDOWS[gi], win)
    return grp, win


def _pool_select(grp, sums):
    out = sums[-1]
    for gi in range(len(sums) - 2, -1, -1):
        out = jnp.where(grp == gi, sums[gi], out)
    return out


def _pool_pooled(p):
    grp, win = _pool_lane_window(p.shape)
    inv_count = 1.0 / jnp.minimum(_iota2(p.shape, 0) + 1, win).astype(F32)
    sums, acc, k = [], p, 1
    for _ in POOL_WINDOWS:
        acc = acc + _shift_down(acc, k)
        sums.append(acc)
        k *= 2
    return _pool_select(grp, sums) * inv_count - p, grp, inv_count


def pool_forward(proj, wbd, pb, ps, y_all, name, after=None):
    s = proj.shape[0]
    specs, ops = _after(after)

    def body(p_ref, w_ref, b_ref, s_ref, *rest):
        o_ref = rest[-1]
        pooled, _, _ = _pool_pooled(p_ref[...])
        mixed = _nn(pooled.astype(BF16), w_ref[...]) + b_ref[...]
        o_ref[...] = (mixed * s_ref[...]).astype(BF16)

    vec = pl.BlockSpec((1, POOL_WIDTH), lambda j: (0, 0))
    return pl.pallas_call(
        body, name=name, grid=(1,),
        in_specs=[pl.BlockSpec((s, POOL_WIDTH), lambda j: (0, OFF_P // POOL_WIDTH)),
                  pl.BlockSpec((POOL_WIDTH, POOL_WIDTH), lambda j: (0, 0)), vec, vec, ANY] + specs,
        out_specs=pl.BlockSpec((s, POOL_WIDTH), lambda j: (0, (SSD_WIDTH + SB_WIDTH) // POOL_WIDTH)),
        out_shape=jax.ShapeDtypeStruct(y_all.shape, BF16),
        input_output_aliases={4: 0},
        compiler_params=_params(("arbitrary",)),
    )(proj, wbd, pb, ps, y_all, *ops)


def pool_backward(proj, dyall, wbd, pb, ps, dproj, name):
    s = proj.shape[0]

    def body(p_ref, dy_ref, w_ref, b_ref, s_ref, _, dp_ref, dw_ref, db_ref, ds_ref):
        pooled, grp, inv_count = _pool_pooled(p_ref[...])
        pooled_b = pooled.astype(BF16)
        mixed = _nn(pooled_b, w_ref[...]) + b_ref[...]
        dy = dy_ref[...]
        ds_ref[...] = jnp.sum(dy * mixed, axis=0, keepdims=True)
        dmixed = dy * s_ref[...]
        db_ref[...] = jnp.sum(dmixed, axis=0, keepdims=True)
        dmixed_b = dmixed.astype(BF16)
        dw_ref[...] = _tn(pooled_b, dmixed_b)
        dpooled = _nt(dmixed_b, w_ref[...])
        sums, acc, k = [], dpooled * inv_count, 1
        for _ in POOL_WINDOWS:
            acc = acc + _shift_up(acc, k)
            sums.append(acc)
            k *= 2
        dp_ref[...] = (_pool_select(grp, sums) - dpooled).astype(BF16)

    vec = pl.BlockSpec((1, POOL_WIDTH), lambda j: (0, 0))
    mat = pl.BlockSpec((POOL_WIDTH, POOL_WIDTH), lambda j: (0, 0))
    pcol = pl.BlockSpec((s, POOL_WIDTH), lambda j: (0, OFF_P // POOL_WIDTH))
    return pl.pallas_call(
        body, name=name, grid=(1,),
        in_specs=[pcol, pl.BlockSpec((s, POOL_WIDTH), lambda j: (0, (SSD_WIDTH + SB_WIDTH) // POOL_WIDTH)), mat, vec, vec, ANY],
        out_specs=[pcol, mat, vec, vec],
        out_shape=[jax.ShapeDtypeStruct(dproj.shape, BF16), jax.ShapeDtypeStruct((POOL_WIDTH, POOL_WIDTH), F32),
                   jax.ShapeDtypeStruct((1, POOL_WIDTH), F32), jax.ShapeDtypeStruct((1, POOL_WIDTH), F32)],
        input_output_aliases={5: 0},
        compiler_params=_params(("arbitrary",)),
    )(proj, dyall, wbd, pb, ps, dproj)


N_PAIRS = SSD_HEADS // 2


def _ssd_common(xc, dtraw, dtb, alog):
    c = CHUNK
    dt = _softplus(dtraw + dtb)
    a = -jnp.exp(alog)
    ltri = (_iota2((c, c), 0) >= _iota2((c, c), 1)).astype(BF16)
    acum = _split_dot_left(ltri, dt * a)
    expand = (_iota2((c, SSD_WIDTH), 1) // HEAD_DIM == _iota2((c, SSD_WIDTH), 0)).astype(BF16)
    expand_wide = (_iota2((c, SSD_HEADS * c), 1) // c == _iota2((c, SSD_HEADS * c), 0)).astype(BF16)
    acum_x = _split_dot(acum, expand, 2)
    dt_x = _split_dot(dt, expand, 2)
    alast_x = acum_x[c - 1:c, :]
    return dict(dt=dt, a=a, acum=acum, acum_x=acum_x, dt_x=dt_x, ea_x=jnp.exp(acum_x),
                dte_x=jnp.exp(alast_x - acum_x), eal_x=jnp.exp(alast_x),
                acol=_split_dot(acum, expand_wide, 2), acum_t=acum.T,
                xs=xc[:, :SSD_WIDTH], causal=_iota2((c, c), 0) >= _iota2((c, c), 1),
                left=_iota2((c, c), 1) < HEAD_DIM)


def _ssd_group(xc, g):
    b = xc[:, SSD_WIDTH + D_STATE * g:SSD_WIDTH + D_STATE * (g + 1)]
    cm = xc[:, SSD_WIDTH + 2 * D_STATE + D_STATE * g:SSD_WIDTH + 2 * D_STATE + D_STATE * (g + 1)]
    return b, cm


def _ssd_decay(q, hh):
    col = q["acol"][:, CHUNK * hh:CHUNK * (hh + 1)]
    row = q["acum_t"][hh:hh + 1, :]
    return jnp.where(q["causal"], jnp.exp(jnp.minimum(col - row, 0.0)), 0.0)


def ssd_forward(proj, xc, dtb, alog, dskip_x, nw, name):
    s = xc.shape[0]
    nc = s // CHUNK

    def body(xc_ref, zdt_ref, dtb_ref, alog_ref, dsk_ref, nw_ref, y_ref, yc_ref, st_ref, state):
        @pl.when(pl.program_id(0) == 0)
        def _():
            state[...] = jnp.zeros_like(state)

        xcv = xc_ref[...]
        q = _ssd_common(xcv, zdt_ref[:, SSD_WIDTH:SSD_WIDTH + LANES], dtb_ref[...], alog_ref[...])
        x = q["xs"] * q["dt_x"]
        xb = x.astype(BF16)
        xd = (x * q["dte_x"]).astype(BF16)
        pieces = []
        for g in range(2):
            bg, cg = _ssd_group(xcv, g)
            bgb, cgb = bg.astype(BF16), cg.astype(BF16)
            cb = _nt(cgb, bgb)
            bgt = bg.T.astype(BF16)
            for pr in (2 * g, 2 * g + 1):
                sl = slice(CHUNK * pr, CHUNK * (pr + 1))
                st = state[pr]
                st_ref[0, pr] = st
                yp = _nn(cgb, st.astype(BF16)) * q["ea_x"][:, sl]
                for k, hh in enumerate((2 * pr, 2 * pr + 1)):
                    w = (cb * _ssd_decay(q, hh)).astype(BF16)
                    mask = q["left"] if k == 0 else jnp.logical_not(q["left"])
                    yp = yp + _nn(w, jnp.where(mask, xb[:, sl], jnp.zeros_like(xb[:, sl])))
                state[pr] = st * q["eal_x"][:, sl] + _nn(bgt, xd[:, sl])
                pieces.append(yp)
        y = jnp.concatenate(pieces, axis=1) + q["xs"] * dsk_ref[...]
        yc_ref[...] = y
        zv = zdt_ref[:, :SSD_WIDTH]
        yg = y * (zv * _sigmoid(zv))
        r = lax.rsqrt(jnp.mean(yg * yg, axis=-1, keepdims=True) + EPS)
        y_ref[...] = (yg * r * nw_ref[...]).astype(BF16)

    vec = lambda n: pl.BlockSpec((1, n), lambda c: (0, 0))
    return pl.pallas_call(
        body, name=name, grid=(nc,),
        in_specs=[pl.BlockSpec((CHUNK, CONV_DIM), lambda c: (c, 0)),
                  pl.BlockSpec((CHUNK, ZDT), lambda c: (c, OFF_Z // ZDT)),
                  vec(LANES), vec(LANES), vec(SSD_WIDTH), vec(SSD_WIDTH)],
        out_specs=[pl.BlockSpec((CHUNK, SSD_WIDTH), lambda c: (c, 0)), pl.BlockSpec((CHUNK, SSD_WIDTH), lambda c: (c, 0)),
                   pl.BlockSpec((1, N_PAIRS, D_STATE, CHUNK), lambda c: (c, 0, 0, 0))],
        out_shape=[jax.ShapeDtypeStruct((s, D_MODEL), BF16), jax.ShapeDtypeStruct((s, SSD_WIDTH), F32),
                   jax.ShapeDtypeStruct((nc, N_PAIRS, D_STATE, CHUNK), F32)],
        scratch_shapes=[pltpu.VMEM((N_PAIRS, D_STATE, CHUNK), F32)],
        compiler_params=_params(("arbitrary",)),
    )(xc, proj, dtb, alog, dskip_x, nw)


def ssd_backward(proj, xc, ycore, dyall, states, dtb, alog, dskip_x, nw, name):
    s = xc.shape[0]
    nc = s // CHUNK
    c = CHUNK

    def body(xc_ref, zdt_ref, yc_ref, dy_ref, st_ref, dtb_ref, alog_ref, dsk_ref, nw_ref,
             dxc_ref, dzdt_ref, dnw_ref, ddsk_ref, ddtb_ref, dalog_ref, dstate):
        @pl.when(pl.program_id(0) == 0)
        def _():
            dstate[...] = jnp.zeros_like(dstate)
            dnw_ref[...] = jnp.zeros_like(dnw_ref)
            ddsk_ref[...] = jnp.zeros_like(ddsk_ref)
            ddtb_ref[...] = jnp.zeros_like(ddtb_ref)
            dalog_ref[...] = jnp.zeros_like(dalog_ref)

        xcv = xc_ref[...]
        dtraw = zdt_ref[:, SSD_WIDTH:SSD_WIDTH + LANES]
        q = _ssd_common(xcv, dtraw, dtb_ref[...], alog_ref[...])
        xs = q["xs"]
        x = xs * q["dt_x"]
        zv, yc, dy, nwv = zdt_ref[:, :SSD_WIDTH], yc_ref[...], dy_ref[...], nw_ref[...]
        sgz = _sigmoid(zv)
        siluz = zv * sgz
        yg = yc * siluz
        r = lax.rsqrt(jnp.mean(yg * yg, axis=-1, keepdims=True) + EPS)
        dnw_ref[...] += jnp.sum(dy * yg * r, axis=0, keepdims=True)
        g1 = dy * nwv
        dyg = r * (g1 - yg * (r * r) * jnp.mean(g1 * yg, axis=-1, keepdims=True))
        dyv = dyg * siluz
        dz = (dyg * yc * (sgz * (1.0 + zv * (1.0 - sgz)))).astype(BF16)
        ddsk_ref[...] += jnp.sum(dyv * xs, axis=0, keepdims=True)
        dye = dyv * q["ea_x"]
        dx_parts, yoff_parts, u_parts, v_parts, e_parts = [], [], [], [], []
        db_parts, dc_parts = [], []
        for g in range(2):
            bg, cg = _ssd_group(xcv, g)
            bgb, cgb = bg.astype(BF16), cg.astype(BF16)
            cb = _nt(cgb, bgb)
            cgt = cg.T.astype(BF16)
            dgsum = jnp.zeros((c, c), F32)
            dbg = jnp.zeros((c, D_STATE), F32)
            dcg = jnp.zeros((c, D_STATE), F32)
            for pr in (2 * g, 2 * g + 1):
                sl = slice(c * pr, c * (pr + 1))
                st = st_ref[0, pr]
                dst = dstate[pr]
                stb, dstb = st.astype(BF16), dst.astype(BF16)
                xp = x[:, sl]
                xpb = xp.astype(BF16)
                dyp = dyv[:, sl]
                xdp = xp * q["dte_x"][:, sl]
                yoff_parts.append(_nn(cgb, stb) * q["ea_x"][:, sl])
                rr = _nn(bgb, dstb)
                dxp = rr * q["dte_x"][:, sl]
                u_parts.append(rr * xdp)
                v_parts.append(dst * st * q["eal_x"][:, sl])
                for k, hh in enumerate((2 * pr, 2 * pr + 1)):
                    decay = _ssd_decay(q, hh)
                    w = cb * decay
                    mask = q["left"] if k == 0 else jnp.logical_not(q["left"])
                    dym = jnp.where(mask, dyp, 0.0).astype(BF16)
                    dw = _nt(dym, xpb)
                    dgsum = dgsum + dw * decay
                    e_parts.append(dw * w)
                    dxp = dxp + _nn(w.T.astype(BF16), dym)
                dyeb = dye[:, sl].astype(BF16)
                dcg = dcg + _nt(dyeb, stb)
                dbg = dbg + _nt(xdp.astype(BF16), dstb)
                dstate[pr] = dst * q["eal_x"][:, sl] + _nn(cgt, dyeb)
                dx_parts.append(dxp)
            dcg = dcg + _nn(dgsum.astype(BF16), bgb)
            dbg = dbg + _nn(dgsum.T.astype(BF16), cgb)
            db_parts.append(dbg)
            dc_parts.append(dcg)
        dx = jnp.concatenate(dx_parts, axis=1)
        yoff = jnp.concatenate(yoff_parts, axis=1)
        u = jnp.concatenate(u_parts, axis=1)
        v = jnp.concatenate(v_parts, axis=1)
        reduce_heads = (_iota2((SSD_WIDTH, c), 0) // HEAD_DIM == _iota2((SSD_WIDTH, c), 1)).astype(BF16)
        to_head = (_iota2((SSD_HEADS * c, c), 0) // c == _iota2((SSD_HEADS * c, c), 1)).astype(BF16)
        da = _split_dot(dyv * yoff - u, reduce_heads, 2)
        da = da + _split_dot(jnp.concatenate(e_parts, axis=1), to_head, 2)
        da = da - _split_dot(jnp.concatenate(e_parts, axis=0), to_head, 2, dot=_tn)
        dalast = jnp.sum(_split_dot(u + v, reduce_heads, 2), axis=0, keepdims=True)
        da = da + jnp.where(_iota2((c, c), 0) == c - 1, dalast, 0.0)
        utri = (_iota2((c, c), 1) >= _iota2((c, c), 0)).astype(BF16)
        dda = _split_dot_left(utri, da)
        ddt = dda * q["a"] + _split_dot(dx * xs, reduce_heads, 2)
        dalog_ref[...] += jnp.sum(dda * q["dt"], axis=0, keepdims=True) * q["a"]
        ddtraw = jnp.where(_iota2((c, c), 1) < SSD_HEADS, ddt * _sigmoid(dtraw + dtb_ref[...]), 0.0)
        ddtb_ref[...] += jnp.sum(ddtraw, axis=0, keepdims=True)
        dzdt_ref[...] = jnp.concatenate([dz, ddtraw.astype(BF16), jnp.zeros((c, ZDT - SSD_WIDTH - LANES), BF16)], axis=1)
        dxs = dx * q["dt_x"] + dyv * dsk_ref[...]
        dxc_ref[...] = jnp.concatenate([dxs] + db_parts + dc_parts, axis=1)

    rev = lambda i: nc - 1 - i
    vec = lambda n: pl.BlockSpec((1, n), lambda i: (0, 0))
    wide = pl.BlockSpec((c, SSD_WIDTH), lambda i: (rev(i), 0))
    zdt = pl.BlockSpec((c, ZDT), lambda i: (rev(i), OFF_Z // ZDT))
    return pl.pallas_call(
        body, name=name, grid=(nc,),
        in_specs=[pl.BlockSpec((c, CONV_DIM), lambda i: (rev(i), 0)), zdt, wide, wide,
                  pl.BlockSpec((1, N_PAIRS, D_STATE, c), lambda i: (rev(i), 0, 0, 0)),
                  vec(LANES), vec(LANES), vec(SSD_WIDTH), vec(SSD_WIDTH)],
        out_specs=[pl.BlockSpec((c, CONV_DIM), lambda i: (rev(i), 0)), zdt,
                   vec(SSD_WIDTH), vec(SSD_WIDTH), vec(LANES), vec(LANES)],
        out_shape=[jax.ShapeDtypeStruct((s, CONV_DIM), F32), jax.ShapeDtypeStruct((s, D_INP), BF16),
                   jax.ShapeDtypeStruct((1, SSD_WIDTH), F32),
                   jax.ShapeDtypeStruct((1, SSD_WIDTH), F32), jax.ShapeDtypeStruct((1, LANES), F32),
                   jax.ShapeDtypeStruct((1, LANES), F32)],
        scratch_shapes=[pltpu.VMEM((N_PAIRS, D_STATE, c), F32)],
        compiler_params=_params(("arbitrary",)),
    )(xc, proj, ycore, dyall, states, dtb, alog, dskip_x, nw)


SB_Q, SB_K = 512, 512
SB_T = 256
SB_SCALE = HEAD_DIM ** -0.5


def _sb_weights(qm, kb, diagonal, run_lk, strict_after):
    z = _nt(qm, kb)
    nz = -z
    tail = jnp.log(1.0 + jnp.exp(jnp.minimum(z, nz)))
    ls = jnp.minimum(z, 0.0) - tail
    lk = jnp.minimum(nz, 0.0) - tail
    if diagonal is not None:
        valid = _iota2(z.shape, 1) < _iota2(z.shape, 0) + diagonal
        lk = jnp.where(valid, lk, 0.0)
    w = jnp.exp(ls + _nn(lk.astype(BF16), strict_after) + run_lk)
    if diagonal is not None:
        w = jnp.where(valid, w, 0.0)
    return ls, jnp.sum(lk, axis=1, keepdims=True), w


SB_RUNS = tuple(reversed(range(SB_K // SB_T)))


def _sb_sweep(i, block, init):
    own = (i * SB_Q) // SB_K
    first = block(own, init, i * SB_Q - own * SB_K)
    return lax.fori_loop(1, own + 1, lambda jj, carry: block(own - jj, carry, None), first)


def sb_forward(proj, y_all, name, after=None):
    s = proj.shape[0]
    t, tk = SB_Q, SB_K
    nq = s // t
    specs, ops = _after(after)

    def body(q_ref, k_ref, v_ref, *rest):
        y_ref, o_ref = rest[-2:]
        i = pl.program_id(1)
        left = _iota2((t, LANES), 1) < HEAD_DIM
        left_k = _iota2((tk, LANES), 1) < HEAD_DIM
        qv = q_ref[...] * SB_SCALE
        zero = jnp.zeros_like(qv)
        qms = (jnp.where(left, qv, zero).astype(BF16), jnp.where(left, zero, qv).astype(BF16))
        strict_after = (_iota2((SB_T, SB_T), 0) > _iota2((SB_T, SB_T), 1)).astype(BF16)

        def block(j, carry, diagonal):
            o, runs = carry[0], carry[1:]
            rows = pl.ds(pl.multiple_of(j * tk, tk), tk)
            kb = k_ref[rows, :].astype(BF16)
            vv = v_ref[rows, :]
            new_runs = []
            for k in range(2):
                vm = jnp.where(left_k if k == 0 else jnp.logical_not(left_k), vv, 0.0).astype(BF16)
                run = runs[k]
                for r in SB_RUNS:
                    keys = slice(SB_T * r, SB_T * (r + 1))
                    _, total, w = _sb_weights(qms[k], kb[keys], None if diagonal is None else diagonal - SB_T * r,
                                              run, strict_after)
                    o = o + _nn(w.astype(BF16), vm[keys])
                    run = run + total
                new_runs.append(run)
            return (o, *new_runs)

        init = (jnp.zeros((t, LANES), F32), jnp.zeros((t, 1), F32), jnp.zeros((t, 1), F32))
        o = _sb_sweep(i, block, init)[0]
        o_ref[...] = o
        y_ref[...] = o.astype(BF16)

    return pl.pallas_call(
        body, name=name, grid=(2, nq),
        in_specs=[pl.BlockSpec((t, LANES), lambda p, i: (i, 3 * p)),
                  pl.BlockSpec((s, LANES), lambda p, i: (0, 3 * p + 1)),
                  pl.BlockSpec((s, LANES), lambda p, i: (0, 3 * p + 2)), ANY] + specs,
        out_specs=[pl.BlockSpec((t, LANES), lambda p, i: (i, SSD_WIDTH // LANES + p)),
                   pl.BlockSpec((t, LANES), lambda p, i: (i, p))],
        out_shape=[jax.ShapeDtypeStruct(y_all.shape, BF16), jax.ShapeDtypeStruct((s, SB_WIDTH), F32)],
        input_output_aliases={3: 0},
        compiler_params=_params(("parallel", "arbitrary")),
    )(proj, proj, proj, y_all, *ops)


def sb_backward(proj, o, dyall, dproj, name, after=None):
    s = proj.shape[0]
    t, tk = SB_Q, SB_K
    nq = s // t
    specs, ops = _after(after)

    def body(q_ref, k_ref, v_ref, o_ref, do_ref, *rest):
        dqkv_ref, dk_acc, dv_acc = rest[-3:]
        dk_acc[...] = jnp.zeros_like(dk_acc)
        dv_acc[...] = jnp.zeros_like(dv_acc)
        left = _iota2((t, LANES), 1) < HEAD_DIM
        lane_masks = (left, jnp.logical_not(left))
        left_k = _iota2((SB_T, LANES), 1) < HEAD_DIM
        key_masks = (left_k, jnp.logical_not(left_k))
        strict_after = (_iota2((SB_T, SB_T), 0) > _iota2((SB_T, SB_T), 1)).astype(BF16)
        from_here = (_iota2((SB_T, SB_T), 0) >= _iota2((SB_T, SB_T), 1)).astype(BF16)

        def query_block(i, _):
            qrows = pl.ds(pl.multiple_of(i * t, t), t)
            qv = q_ref[qrows, :] * SB_SCALE
            dov = do_ref[qrows, :]
            zero = jnp.zeros_like(qv)
            qb = qv.astype(BF16)
            dob = dov.astype(BF16)
            prod = dob.astype(F32) * o_ref[qrows, :]
            qms = [jnp.where(m, qv, zero).astype(BF16) for m in lane_masks]
            doms = [jnp.where(m, dov, zero).astype(BF16) for m in lane_masks]
            deltas = [jnp.sum(jnp.where(m, prod, zero), axis=1, keepdims=True) for m in lane_masks]

            def block(j, carry, diagonal):
                dq = carry[0]
                run_lk, run_e = carry[1:3], carry[3:5]
                rows = pl.ds(pl.multiple_of(j * tk, tk), tk)
                kb = k_ref[rows, :].astype(BF16)
                vb = v_ref[rows, :].astype(BF16)
                dkj = [jnp.zeros((SB_T, LANES), F32) for _ in SB_RUNS]
                dvj = [jnp.zeros((SB_T, LANES), F32) for _ in SB_RUNS]
                new_lk, new_e = [], []
                for k in range(2):
                    lk_sum, e_sum = run_lk[k], run_e[k]
                    for r in SB_RUNS:
                        keys = slice(SB_T * r, SB_T * (r + 1))
                        off = None if diagonal is None else diagonal - SB_T * r
                        ls, total, w = _sb_weights(qms[k], kb[keys], off, lk_sum, strict_after)
                        wb = w.astype(BF16)
                        e = _nt(doms[k], vb[keys]) * wb.astype(F32)
                        before = deltas[k] - _split_dot(e, from_here, 2) - e_sum
                        dz = e - jnp.exp(ls) * (e + before)
                        if off is not None:
                            dz = jnp.where(_iota2(dz.shape, 1) < _iota2(dz.shape, 0) + off, dz, 0.0)
                        dz = dz.astype(BF16)
                        dvj[r] = dvj[r] + jnp.where(key_masks[k], _tn(wb, dob), 0.0)
                        dkj[r] = dkj[r] + jnp.where(key_masks[k], _tn(dz, qb), 0.0)
                        dq = dq + jnp.where(lane_masks[k], _nn(dz, kb[keys]), 0.0)
                        lk_sum = lk_sum + total
                        e_sum = e_sum + jnp.sum(e, axis=1, keepdims=True)
                    new_lk.append(lk_sum)
                    new_e.append(e_sum)
                dk_acc[rows, :] += jnp.concatenate(dkj, axis=0)
                dv_acc[rows, :] += jnp.concatenate(dvj, axis=0)
                return (dq, *new_lk, *new_e)

            col = jnp.zeros((t, 1), F32)
            dq = _sb_sweep(i, block, (jnp.zeros((t, LANES), F32), col, col, col, col))[0]
            dqkv_ref[qrows, 0:LANES] = (dq * SB_SCALE).astype(BF16)
            return 0

        lax.fori_loop(0, nq, query_block, 0)
        dqkv_ref[:, LANES:2 * LANES] = dk_acc[...].astype(BF16)
        dqkv_ref[:, 2 * LANES:3 * LANES] = dv_acc[...].astype(BF16)

    col = lambda f: pl.BlockSpec((s, LANES), f)
    return pl.pallas_call(
        body, name=name, grid=(2,),
        in_specs=[col(lambda p: (0, 3 * p)), col(lambda p: (0, 3 * p + 1)), col(lambda p: (0, 3 * p + 2)),
                  col(lambda p: (0, p)), col(lambda p: (0, SSD_WIDTH // LANES + p)), ANY] + specs,
        out_specs=pl.BlockSpec((s, 3 * LANES), lambda p: (0, p)),
        out_shape=jax.ShapeDtypeStruct(dproj.shape, BF16),
        input_output_aliases={5: 0},
        scratch_shapes=[pltpu.VMEM((s, LANES), F32), pltpu.VMEM((s, LANES), F32)],
        compiler_params=_params(("parallel",)),
    )(proj, proj, proj, o, dyall, dproj, *ops)


def adamw(w, g, m, v, name):
    b, r, c = w.shape
    tr = max([t for t in range(8, min(r, 512) + 1, 8) if r % t == 0], default=r)

    def body(w_ref, g_ref, m_ref, v_ref, d_ref, nm_ref, nv_ref):
        gv = g_ref[...]
        nm = ADAM_B1 * m_ref[...] + (1.0 - ADAM_B1) * gv
        nv = ADAM_B2 * v_ref[...] + (1.0 - ADAM_B2) * (gv * gv)
        m_hat = nm / (1.0 - ADAM_B1 ** ADAM_STEP)
        v_hat = nv / (1.0 - ADAM_B2 ** ADAM_STEP)
        d_ref[...] = -ADAM_LR * (m_hat / (jnp.sqrt(v_hat) + ADAM_EPS) + ADAM_WD * w_ref[...])
        nm_ref[...] = nm
        nv_ref[...] = nv

    blk = pl.BlockSpec((1, tr, c), lambda i, j: (i, j, 0))
    return pl.pallas_call(
        body, name=name, grid=(b, r // tr),
        in_specs=[blk] * 4, out_specs=[blk] * 3,
        out_shape=[jax.ShapeDtypeStruct(w.shape, F32)] * 3,
        compiler_params=_params(("parallel", "parallel")),
    )(w, g, m, v)


def _position():
    return lax.axis_index("x"), lax.axis_index("y"), lax.axis_index("c")


def _flipped(pos, flip):
    return tuple((1 - p) if f else p for p, f in zip(pos, flip))


FLIP_C = (0, 0, 1)
CHIP_FLIPS = {1: (0, 1, 0), 2: (1, 0, 0), 3: (1, 1, 0)}
SHARD_ROWS = (SHARD_IN, SHARD_OUT, SHARD_FF, SHARD_FF, SHARD_FF)


def _rows(start, size):
    return pl.ds(pl.multiple_of(start, 16), size)


HBM = pl.BlockSpec(memory_space=pltpu.HBM)
SEM = pl.BlockSpec(memory_space=pltpu.SEMAPHORE)
EFFECT = pltpu.SideEffectType.DATAFLOW_SIDE_EFFECTING


def _in_hbm(a):
    return pltpu.with_memory_space_constraint(a, pltpu.HBM)


def _landing(shape, dtype):
    return _in_hbm(lax.empty(shape, dtype))


def _copies(plan, pos, src_refs, land_refs, send_sems, recv_sems):
    return [pltpu.make_async_remote_copy(src_ref=src, dst_ref=dst, send_sem=send_sems.at[k], recv_sem=recv_sems.at[k],
                                         device_id=_flipped(pos, flip), device_id_type=MESH)
            for k, (src, dst, flip) in enumerate(plan(pos, src_refs, land_refs))]


def exchange_start(name, srcs, lands, n, plan, after=None):
    ns, nl = len(srcs), len(lands)
    specs, ops = _after(after)

    def body(*refs):
        src_refs, land_refs = refs[:ns], refs[ns:ns + nl]
        send_sems, recv_sems, token = refs[ns + nl + len(ops)], refs[ns + nl + len(ops) + 1], refs[-1]
        for cp in _copies(plan, _position(), src_refs, land_refs, send_sems, recv_sems):
            cp.start()
        token[...] = jnp.zeros_like(token)

    thru = [pltpu.HBM(a.shape, a.dtype) for a in list(srcs) + list(lands)]
    out = pl.pallas_call(
        body, name=name,
        out_shape=(pltpu.SemaphoreType.DMA((n,)), pltpu.SemaphoreType.DMA((n,)), *thru, jax.ShapeDtypeStruct((8, LANES), F32)),
        in_specs=[HBM] * (ns + nl) + specs,
        out_specs=(SEM, SEM, *([HBM] * (ns + nl)), pl.BlockSpec(memory_space=pltpu.VMEM)),
        input_output_aliases={k: 2 + k for k in range(ns + nl)},
        compiler_params=pltpu.CompilerParams(has_side_effects=EFFECT),
    )(*[_in_hbm(a) for a in srcs], *lands, *ops)
    return out[0], out[1], list(out[2:2 + ns]), list(out[2 + ns:2 + ns + nl]), out[-1]


def exchange_wait(name, started, after, plan):
    send_sems, recv_sems, srcs, lands, _ = started
    ns, nl = len(srcs), len(lands)
    specs, ops = _after(after)

    def body(*refs):
        src_refs, land_refs = refs[:ns], refs[ns:ns + nl]
        send_sems, recv_sems = refs[ns + nl], refs[ns + nl + 1]
        for cp in _copies(plan, _position(), src_refs, land_refs, send_sems, recv_sems):
            cp.wait_send()
            cp.wait_recv()

    out = pl.pallas_call(
        body, name=name,
        out_shape=tuple(pltpu.HBM(a.shape, a.dtype) for a in list(srcs) + list(lands)),
        in_specs=[HBM] * (ns + nl) + [SEM, SEM] + specs,
        out_specs=tuple([HBM] * (ns + nl)),
        input_output_aliases={k: k for k in range(ns + nl)},
        compiler_params=pltpu.CompilerParams(has_side_effects=EFFECT),
    )(*srcs, *lands, send_sems, recv_sems, *ops)
    return list(out[:ns]), list(out[ns:])


def _gather_ici_plan(pos, srcs, lands):
    chip, c = 2 * pos[0] + pos[1], pos[2]
    copies = []
    for src, dst in zip(srcs, lands):
        r = src.shape[0]
        h = r // 2
        for f in (1, 2, 3):
            copies.append((src.at[_rows(c * h, h)], dst.at[_rows(chip * r + c * h, h)], CHIP_FLIPS[f]))
    return copies


def _gather_d2d_plan(pos, srcs, lands):
    chip, c = 2 * pos[0] + pos[1], pos[2]
    copies = []
    for own, dst in zip(srcs, lands):
        r = own.shape[0]
        h = r // 2
        copies.append((own, dst.at[_rows(chip * r, r)], FLIP_C))
        for f in (1, 2, 3):
            at = _rows(lax.bitwise_xor(chip, f) * r + c * h, h)
            copies.append((dst.at[at], dst.at[at], FLIP_C))
    return copies


def gather_ici_start(shards, after=None):
    lands = [_landing((N_CHIPS * a.shape[0], D_MODEL), BF16) for a in shards]
    return exchange_start("gather_ici_start", shards, lands, 3 * len(shards), _gather_ici_plan, after=after)


def gather_d2d_start(shards, fulls, after=None):
    return exchange_start("gather_d2d_start", shards, fulls, 4 * len(shards), _gather_d2d_plan, after=after)


def _reduce_d2d_plan(pos, srcs, lands):
    c = pos[2]
    return [(src.at[:, _rows((1 - c) * (src.shape[1] // 2), src.shape[1] // 2)], dst, FLIP_C) for src, dst in zip(srcs, lands)]


def _reduce_ici_plan(pos, srcs, lands):
    chip = 2 * pos[0] + pos[1]
    return [(src.at[lax.bitwise_xor(chip, f)], dst.at[f - 1], CHIP_FLIPS[f]) for src, dst in zip(srcs, lands) for f in (1, 2, 3)]


def _reduce_swap_plan(pos, srcs, lands):
    c = pos[2]
    copies = []
    for dst in lands:
        h = dst.shape[0] // 2
        at = _rows(c * h, h)
        copies.append((dst.at[at], dst.at[at], FLIP_C))
    return copies


def reduce_d2d_start(grads):
    lands = [_landing((N_CHIPS, g.shape[1] // 2, D_MODEL), BF16) for g in grads]
    return exchange_start("reduce_d2d_start", grads, lands, len(grads), _reduce_d2d_plan)


def reduce_ici_start(chip_sums):
    lands = [_landing((N_CHIPS - 1,) + p.shape[1:], BF16) for p in chip_sums]
    return exchange_start("reduce_ici_start", chip_sums, lands, 3 * len(chip_sums), _reduce_ici_plan)


def reduce_swap_start(mine):
    return exchange_start("reduce_swap_start", [], mine, len(mine), _reduce_swap_plan)


def _by_shape(fn, *lists):
    groups, out = {}, [None] * len(lists[0])
    for k, a in enumerate(lists[0]):
        groups.setdefault(a.shape, []).append(k)
    for idx in groups.values():
        for k, r in zip(idx, fn(*[[l[k] for k in idx] for l in lists])):
            out[k] = r
    return out


def add_halves(ds, recvs, half, name):
    n = len(ds)
    nch, r, c = ds[0].shape
    h = r // 2

    def body(half_ref, *refs):
        for k in range(n):
            refs[2 * n + k][...] = (refs[k][...].astype(F32) + refs[n + k][...].astype(F32)).astype(BF16)

    mine = pl.BlockSpec((1, h, c), lambda j, hf: (j, hf[0], 0))
    whole = pl.BlockSpec((1, h, c), lambda j, hf: (j, 0, 0))
    return pl.pallas_call(
        body, name=name,
        grid_spec=pltpu.PrefetchScalarGridSpec(
            num_scalar_prefetch=1, grid=(nch,), in_specs=[mine] * n + [whole] * n, out_specs=[whole] * n),
        out_shape=[jax.ShapeDtypeStruct(rv.shape, BF16) for rv in recvs],
        compiler_params=_params(("parallel",)),
    )(half, *ds, *recvs)


def add_chips(ps, recvs, chip, name):
    n = len(ps)
    _, r, c = ps[0].shape

    def body(chip_ref, *refs):
        for k in range(n):
            acc = refs[k][0].astype(F32)
            for f in range(N_CHIPS - 1):
                acc = acc + refs[n + k][f].astype(F32)
            refs[2 * n + k][...] = acc

    return pl.pallas_call(
        body, name=name,
        grid_spec=pltpu.PrefetchScalarGridSpec(
            num_scalar_prefetch=1, grid=(1,),
            in_specs=[pl.BlockSpec((1, r, c), lambda i, ch: (ch[0], 0, 0))] * n +
                     [pl.BlockSpec((N_CHIPS - 1, r, c), lambda i, ch: (0, 0, 0))] * n,
            out_specs=[pl.BlockSpec((r, c), lambda i, ch: (ch[1], 0))] * n),
        out_shape=[jax.ShapeDtypeStruct((2 * r, c), F32)] * n,
        compiler_params=_params(("arbitrary",)),
    )(chip, *ps, *recvs)


def adamw_layers(w, gs, m, v, name):
    b, r, c = w.shape
    tr = max([t for t in range(8, min(r, 512) + 1, 8) if r % t == 0], default=r)

    def body(w_ref, m_ref, v_ref, *rest):
        g_refs, (g_ref, d_ref, nm_ref, nv_ref) = rest[:b], rest[b:]
        layer = pl.program_id(0)
        gv = g_refs[0][...]
        for l in range(1, b):
            gv = jnp.where(layer == l, g_refs[l][...], gv)
        nm = ADAM_B1 * m_ref[0] + (1.0 - ADAM_B1) * gv
        nv = ADAM_B2 * v_ref[0] + (1.0 - ADAM_B2) * (gv * gv)
        m_hat = nm / (1.0 - ADAM_B1 ** ADAM_STEP)
        v_hat = nv / (1.0 - ADAM_B2 ** ADAM_STEP)
        g_ref[0] = gv
        d_ref[0] = -ADAM_LR * (m_hat / (jnp.sqrt(v_hat) + ADAM_EPS) + ADAM_WD * w_ref[0])
        nm_ref[0] = nm
        nv_ref[0] = nv

    nr, tc = r // tr, (c if tr < r else _tile(c, 256))
    steps = nr * (c // tc)
    blk = pl.BlockSpec((1, tr, tc), lambda i, j: (i, j % nr, j // nr))
    g_specs = [pl.BlockSpec((tr, tc), lambda i, j, l=l: (jnp.where(i == l, j % nr, jnp.where(i < l, 0, nr - 1)),
                                                         jnp.where(i == l, j // nr, jnp.where(i < l, 0, c // tc - 1))))
               for l in range(b)]
    return pl.pallas_call(
        body, name=name, grid=(b, steps),
        in_specs=[blk] * 3 + g_specs, out_specs=[blk] * 4,
        out_shape=[jax.ShapeDtypeStruct(w.shape, F32)] * 4,
        compiler_params=_params(("arbitrary", "arbitrary")),
    )(w, m, v, *gs)


def _all_devices_plan(pos, srcs, lands):
    return [(srcs[0], lands[0].at[f], ((f >> 2) & 1, (f >> 1) & 1, f & 1)) for f in range(1, 8)]


def sum_devices(v, gathered, me, name):
    r, c = v.shape

    def body(me_ref, v_ref, g_ref, o_ref):
        own = v_ref[...]
        acc = None
        for d in range(8):
            slot = lax.bitwise_xor(me_ref[0], d)
            term = jnp.where(slot == 0, own, g_ref[slot])
            acc = term if acc is None else acc + term
        o_ref[...] = acc

    return pl.pallas_call(
        body, name=name,
        grid_spec=pltpu.PrefetchScalarGridSpec(
            num_scalar_prefetch=1, grid=(1,),
            in_specs=[pl.BlockSpec((r, c), lambda i, m: (0, 0)), pl.BlockSpec((8, r, c), lambda i, m: (0, 0, 0))],
            out_specs=pl.BlockSpec((r, c), lambda i, m: (0, 0))),
        out_shape=jax.ShapeDtypeStruct((r, c), F32),
        compiler_params=_params(("arbitrary",)),
    )(me, v, gathered)


_IN_SEGMENTS = ((0, 1544, 128), (128, 1800, 128), (256, 2056, 128), (384, 1672, 128), (512, 1928, 128), (640, 2184, 128),
                (OFF_Z, 0, SSD_WIDTH), (OFF_DT, 1536, SSD_HEADS), (OFF_XBC, 512, CONV_DIM), (OFF_P, 2312, POOL_WIDTH))


def _in_column_map():
    m = np.full((D_INP,), -1, np.int64)
    for at, orig, n in _IN_SEGMENTS:
        cols = np.arange(orig, orig + n)
        m[at:at + n] = (cols // COLS_IN) * SHARD_IN + cols % COLS_IN
    return m


def take_rows(a, idx, name):
    dep, r_in, c = a.shape
    blk = 2 * LANES if len(idx) % (2 * LANES) == 0 and r_in % (2 * LANES) == 0 else LANES
    n_out, n_in = len(idx) // blk, r_in // blk
    assert len(idx) % blk == 0 and r_in % blk == 0
    sources = [sorted({int(v) // blk for v in idx[blk * i:blk * (i + 1)] if v >= 0}) for i in range(n_out)]
    width = max(len(s) for s in sources)
    table = np.zeros((n_out, width), np.int32)
    for i, s in enumerate(sources):
        spare = [b for b in range(n_in) if b not in s][:width - len(s)]
        table[i] = s + spare

    def body(tbl_ref, idx_ref, *refs):
        in_refs, o_ref = refs[:width], refs[width]
        i = pl.program_id(1)
        src = idx_ref[...]
        acc = jnp.zeros((blk, c), F32)
        for k in range(width):
            pick = (src == tbl_ref[i, k] * blk + _iota2((blk, blk), 1)).astype(BF16)
            acc = acc + _nn(pick, in_refs[k][0])
        o_ref[0] = acc.astype(BF16)

    return pl.pallas_call(
        body, name=name,
        grid_spec=pltpu.PrefetchScalarGridSpec(
            num_scalar_prefetch=1, grid=(dep, n_out),
            in_specs=[pl.BlockSpec((blk, 1), lambda l, i, t: (i, 0))] +
                     [pl.BlockSpec((1, blk, c), lambda l, i, t, k=k: (l, t[i, k], 0)) for k in range(width)],
            out_specs=pl.BlockSpec((1, blk, c), lambda l, i, t: (l, i, 0))),
        out_shape=jax.ShapeDtypeStruct((dep, len(idx), c), BF16),
        compiler_params=_params(("parallel", "parallel")),
    )(jnp.asarray(table), jnp.asarray(np.asarray(idx, np.int32).reshape(-1, 1)), *([a] * width))


def _in_weight_layout(staged):
    return take_rows(staged, _in_column_map(), "w_in_layout")


def _in_gradient_layout(dwt):
    fwd = _in_column_map()
    inv = np.full((N_CHIPS * SHARD_IN,), -1, np.int64)
    inv[fwd[fwd >= 0]] = np.nonzero(fwd >= 0)[0]
    return take_rows(dwt, inv, "dw_in_layout")


SMALL_NAMES = ("norm1_w", "conv_w", "conv_b", "dt_bias", "a_log", "d_skip", "ssd_norm_w", "pool_w", "pool_b",
               "pool_scale", "norm2_w", "final_norm_w")
SMALL_ROWS = 160


def _small_rows(shape):
    return -(-int(np.prod(shape)) // (8 * D_MODEL)) * 8


def _pack_small(parts):
    rows = []
    for p in parts:
        flat = p.reshape(-1)
        rows.append(jnp.pad(flat, (0, _small_rows(p.shape) * D_MODEL - flat.shape[0])).reshape(-1, D_MODEL))
    used = sum(r.shape[0] for r in rows)
    return jnp.concatenate(rows + [jnp.zeros((SMALL_ROWS - used, D_MODEL), F32)], axis=0)


def _unpack_small(packed, shapes):
    out, at = [], 0
    for shp in shapes:
        n, r = int(np.prod(shp)), _small_rows(shp)
        out.append(packed[at:at + r].reshape(-1)[:n].reshape(shp))
        at += r
    return out


def kernel(x, norm1_w, w_in, conv_w, conv_b, dt_bias, a_log, d_skip, ssd_norm_w, pool_w, pool_b, pool_scale, w_out, norm2_w, w_gate, w_up, w_down, final_norm_w, loss_target, m_norm1_w, m_w_in, m_conv_w, m_conv_b, m_dt_bias, m_a_log, m_d_skip, m_ssd_norm_w, m_pool_w, m_pool_b, m_pool_scale, m_w_out, m_norm2_w, m_w_gate, m_w_up, m_w_down, m_final_norm_w, v_norm1_w, v_w_in, v_conv_w, v_conv_b, v_dt_bias, v_a_log, v_d_skip, v_ssd_norm_w, v_pool_w, v_pool_b, v_pool_scale, v_w_out, v_norm2_w, v_w_gate, v_w_up, v_w_down, v_final_norm_w):
    px, py, pc = _position()
    chip = 2 * px + py
    chip_arr = jnp.reshape(chip, (1,)).astype(jnp.int32)
    half_arr = jnp.reshape(pc, (1,)).astype(jnp.int32)

    def layer_shards(l):
        w_in_t = jnp.pad(jnp.swapaxes(w_in[l], 0, 1).astype(BF16), ((0, SHARD_IN - COLS_IN), (0, 0)))
        return [w_in_t, w_out[l].astype(BF16), jnp.swapaxes(w_gate[l], 0, 1).astype(BF16),
                jnp.swapaxes(w_up[l], 0, 1).astype(BF16), w_down[l].astype(BF16)]

    shards0 = layer_shards(0)
    head = gather_ici_start(shards0[:1])
    over_ici = {}

    def pass_on(l, after):
        own, arrived = exchange_wait("gather_ici_wait", over_ici[l], after, _gather_ici_plan)
        swap = gather_d2d_start(own, arrived)
        tokens = [swap[4]]
        if l + 1 < DEPTH:
            over_ici[l + 1] = gather_ici_start(layer_shards(l + 1), after=swap[4])
            tokens.append(over_ici[l + 1][4])
        return swap, tokens

    def weights_of(swap, after):
        _, (w_in_st, w_out_l, w_gate_t, w_up_t, w_down_l) = exchange_wait("gather_d2d_wait", swap, after, _gather_d2d_plan)
        return _in_weight_layout(w_in_st[None])[0], w_out_l, w_gate_t, w_up_t, w_down_l

    pad_heads = lambda v: jnp.pad(v, ((0, 0), (0, LANES - SSD_HEADS)))[:, None, :]
    dtb, alog = pad_heads(dt_bias), pad_heads(a_log)
    dskip_x = jnp.repeat(d_skip, HEAD_DIM, axis=1)[:, None, :]
    eye = jnp.eye(len(POOL_WINDOWS), dtype=F32)
    wbd = (pool_w[:, :, :, None, :] * eye[None, :, None, :, None]).reshape(DEPTH, POOL_WIDTH, POOL_WIDTH).astype(BF16)
    pool_b2 = pool_b.reshape(DEPTH, 1, POOL_WIDTH)
    cw_cols = lax.dynamic_update_slice(jnp.zeros((DEPTH, CONV_WIDTH, CONV_DIM), F32), conv_w,
                                       (0, 0, chip * (CONV_DIM // N_CHIPS)))
    cw_cols = jnp.where(pc == 0, cw_cols, 0.0)
    cw_rows = (DEPTH * CONV_WIDTH * CONV_DIM) // D_MODEL
    me_arr = jnp.reshape(4 * px + 2 * py + pc, (1,)).astype(jnp.int32)
    cw_cols = jnp.pad(cw_cols.reshape(cw_rows, D_MODEL), ((0, 8), (0, 0)))
    cw_start = exchange_start("gather_conv_w_start", [cw_cols], [_landing((8,) + cw_cols.shape, F32)], 7,
                              _all_devices_plan, after=head[4])

    h = x[0]
    saved, weights = [], []
    own, arrived = exchange_wait("gather_ici_wait", head, [head[4], cw_start[4]] + shards0[1:], _gather_ici_plan)
    head = gather_d2d_start(own, arrived)
    over_ici[0] = gather_ici_start(shards0[1:], after=head[4])
    w_in_f = _in_weight_layout(exchange_wait("gather_d2d_wait", head, [head[4], over_ici[0][4]], _gather_d2d_plan)[1][0][None])[0]
    for l in range(DEPTH):
        if l > 0:
            w_in_f, w_out_f, w_gate_t, w_up_t, w_down_f = weights[l]
        proj = rms_matmul(h, norm1_w[l][None], w_in_f, "in_proj")
        if l == 0:
            (cw_own,), (cw_all,) = exchange_wait("gather_conv_w_wait", cw_start, proj, _all_devices_plan)
            conv_w_f = sum_devices(cw_own, cw_all, me_arr, "gather_conv_w_sum")[:cw_rows].reshape(DEPTH, CONV_WIDTH, CONV_DIM)
            cw8 = jnp.pad(conv_w_f, ((0, 0), (0, 8 - CONV_WIDTH), (0, 0)))
        xc = conv_forward(proj, cw8[l], conv_b[l][None], "conv_fwd")
        y_all, ycore, states = ssd_forward(proj, xc, dtb[l], alog[l], dskip_x[l], ssd_norm_w[l][None], "ssd_fwd")
        if l == 0:
            swap, tokens = pass_on(0, y_all)
            y_all, o_sb = sb_forward(proj, y_all, "sb_fwd", after=tokens)
            tokens = None
        else:
            y_all, o_sb = sb_forward(proj, y_all, "sb_fwd")
            swap, tokens = pass_on(l + 1, o_sb) if l + 1 < DEPTH else (None, None)
        y_all = pool_forward(proj, wbd[l], pool_b2[l], pool_scale[l][None], y_all, "pool_fwd", after=tokens)
        if l == 0:
            w_out_f, w_gate_t, w_up_t, w_down_f = exchange_wait("gather_d2d_wait", swap, y_all, _gather_d2d_plan)[1]
            weights.append((w_in_f, w_out_f, w_gate_t, w_up_t, w_down_f))
        x1 = matmul_residual(y_all, w_out_f, h, "out_proj")
        x2, g, u = ffn_forward(x1, norm2_w[l][None], w_gate_t, w_up_t, w_down_f, "ffn_fwd")
        if l == 0:
            swap, tokens = pass_on(1, x2)
            weights.append(weights_of(swap, tokens))
        elif swap is not None:
            weights.append(weights_of(swap, x2))
        saved.append((h, proj, xc, ycore, states, o_sb, y_all, x1, g, u))
        h = x2

    loss_part, dx, dxb, d_final = loss_head(h, final_norm_w[None], loss_target[0], "loss_head")
    loss = lax.psum(loss_part[0, 0], ("x", "y", "c"))

    small = {n: [None] * DEPTH for n in SMALL_NAMES if n != "final_norm_w"}
    chip_half = jnp.concatenate([chip_arr, half_arr])
    reduced = {}
    d2d = ici = early = None

    def add_cores(d2d, after):
        mine, theirs = exchange_wait("reduce_d2d_wait", d2d[1], after, _reduce_d2d_plan)
        return d2d[0], reduce_ici_start(_by_shape(lambda ds, ts: add_halves(ds, ts, half_arr, "reduce_add_halves"), mine, theirs))

    def add_all(ici, after):
        sums, theirs = exchange_wait("reduce_ici_wait", ici[1], after, _reduce_ici_plan)
        return ici[0], reduce_swap_start(_by_shape(lambda ps, ts: add_chips(ps, ts, chip_half, "reduce_add_chips"), sums, theirs))

    def finish(swap, after):
        reduced[swap[0]] = exchange_wait("reduce_swap_wait", swap[1], after, _reduce_swap_plan)[1]

    swaps = []
    for l in reversed(range(DEPTH)):
        xin, proj, xc, ycore, states, o_sb, y_all, x1, g, u = saved[l]
        w_in_f, w_out_f, w_gate_t, w_up_t, w_down_f = weights[l]
        dg, du, act = ffn_backward_act(dxb, g, u, w_down_f, "ffn_bwd_act", after=None if d2d is None else d2d[1][4])
        dx1, dx1b, h2, dn2 = rms_backward([dg, du], [w_gate_t, w_up_t], x1, norm2_w[l][None], dx, "ffn_bwd_norm", 512)
        if d2d is not None:
            ici = add_cores(d2d, dx1b)
        dyall = matmul_nt(dx1b, w_out_f, "out_proj_bwd", after=None if ici is None else ici[1][4])
        dw_down = matmul_tn(act, dxb, "dw_down")
        dw_gate = matmul_tn(dg, h2, "dw_gate")
        dw_up = matmul_tn(du, h2, "dw_up")
        dw_out = matmul_tn(y_all, dx1b, "dw_out")
        late = [dw.reshape(N_CHIPS, r, D_MODEL) for dw, r in zip((dw_out, dw_gate, dw_up, dw_down), SHARD_ROWS[1:])]
        if l == 0:
            early = ("0 late", reduce_d2d_start(late))
        dxc, dproj, dsn, ddsk, ddtb, dalog = ssd_backward(proj, xc, ycore, dyall, states, dtb[l], alog[l],
                                                          dskip_x[l], ssd_norm_w[l][None], "ssd_bwd")
        dproj, dcw, dcb = conv_backward(proj, dxc, cw8[l], conv_b[l][None], dproj, "conv_bwd",
                                        after=None if early is None else early[1][4])
        if early is not None:
            early = add_cores(early, dproj)
        dproj = sb_backward(proj, o_sb, dyall, dproj, "sb_bwd", after=None if early is None else early[1][4])
        dproj, dwbd, dpb, dps = pool_backward(proj, dyall, wbd[l], pool_b2[l], pool_scale[l][None], dproj, "pool_bwd")
        if ici is not None:
            swaps.append(add_all(ici, dproj))
            ici = None
        dx, dxb, h1, dn1 = rms_backward([dproj], [w_in_f], xin, norm1_w[l][None], dx1, "in_proj_bwd", 512,
                                        after=swaps[-1][1][4] if swaps else None)
        dw_in = _in_gradient_layout(matmul_tn(dproj, h1, "dw_in")[None])[0].reshape(N_CHIPS, SHARD_IN, D_MODEL)
        d2d = (l, reduce_d2d_start([dw_in] if l == 0 else [dw_in] + late))
        small["norm1_w"][l] = dn1[0]
        small["conv_w"][l] = dcw[:CONV_WIDTH]
        small["conv_b"][l] = dcb[0]
        small["dt_bias"][l] = ddtb[0, :SSD_HEADS]
        small["a_log"][l] = dalog[0, :SSD_HEADS]
        small["d_skip"][l] = ddsk.reshape(SSD_HEADS, HEAD_DIM).sum(axis=1)
        small["ssd_norm_w"][l] = dsn[0]
        small["pool_w"][l] = jnp.stack([dwbd[64 * k:64 * k + 64, 64 * k:64 * k + 64] for k in range(len(POOL_WINDOWS))])
        small["pool_b"][l] = dpb.reshape(len(POOL_WINDOWS), -1)
        small["pool_scale"][l] = dps[0]
        small["norm2_w"][l] = dn2[0]
    grad_x = dx[None]

    ici = add_cores(d2d, d2d[1][4])
    small_parts = [d_final if n == "final_norm_w" else jnp.stack(small[n]) for n in SMALL_NAMES]
    small_start = exchange_start("reduce_small_start", [_pack_small(small_parts)],
                                 [_landing((8, SMALL_ROWS, D_MODEL), F32)], 7, _all_devices_plan, after=ici[1][4])
    swaps.append(add_all(early, small_start[4]))
    swaps.append(add_all(ici, swaps[-1][1][4]))
    for swap in swaps:
        finish(swap, swaps[-1][1][4])
    (small_own,), (small_all,) = exchange_wait("reduce_small_wait", small_start, reduced[0][0], _all_devices_plan)
    small_sum = sum_devices(small_own, small_all, me_arr, "reduce_small_sum")
    reduced[0] = reduced[0] + reduced["0 late"]
    g_big = {n: [reduced[l][k] for l in range(DEPTH)] for k, n in enumerate(("w_in", "w_out", "w_gate", "w_up", "w_down"))}
    g_big["w_in"] = [gl[:COLS_IN] for gl in g_big["w_in"]]
    transposed = ("w_in", "w_gate", "w_up")

    g_small = dict(zip(SMALL_NAMES, _unpack_small(small_sum, [p.shape for p in small_parts])))
    g_small["final_norm_w"] = g_small["final_norm_w"].reshape(final_norm_w.shape)
    g_small["conv_w"] = lax.dynamic_slice_in_dim(g_small["conv_w"], chip * (CONV_DIM // N_CHIPS), CONV_DIM // N_CHIPS, axis=2)

    given = dict(norm1_w=(norm1_w, m_norm1_w, v_norm1_w), w_in=(w_in, m_w_in, v_w_in), conv_w=(conv_w, m_conv_w, v_conv_w),
                 conv_b=(conv_b, m_conv_b, v_conv_b), dt_bias=(dt_bias, m_dt_bias, v_dt_bias), a_log=(a_log, m_a_log, v_a_log),
                 d_skip=(d_skip, m_d_skip, v_d_skip), ssd_norm_w=(ssd_norm_w, m_ssd_norm_w, v_ssd_norm_w),
                 pool_w=(pool_w, m_pool_w, v_pool_w), pool_b=(pool_b, m_pool_b, v_pool_b),
                 pool_scale=(pool_scale, m_pool_scale, v_pool_scale), w_out=(w_out, m_w_out, v_w_out),
                 norm2_w=(norm2_w, m_norm2_w, v_norm2_w), w_gate=(w_gate, m_w_gate, v_w_gate), w_up=(w_up, m_w_up, v_w_up),
                 w_down=(w_down, m_w_down, v_w_down), final_norm_w=(final_norm_w, m_final_norm_w, v_final_norm_w))
    order = ("norm1_w", "w_in", "conv_w", "conv_b", "dt_bias", "a_log", "d_skip", "ssd_norm_w", "pool_w", "pool_b",
             "pool_scale", "w_out", "norm2_w", "w_gate", "w_up", "w_down", "final_norm_w")
    grads = dict(g_small)
    results = {}
    for n in ("w_in", "w_out", "w_gate", "w_up", "w_down"):
        w, m, v = given[n]
        if n in transposed:
            out = adamw_layers(jnp.swapaxes(w, 1, 2), g_big[n], jnp.swapaxes(m, 1, 2), jnp.swapaxes(v, 1, 2), "adamw_" + n)
            out = [jnp.swapaxes(o, 1, 2) for o in out]
        else:
            out = adamw_layers(w, g_big[n], m, v, "adamw_" + n)
        grads[n], results[n] = out[0], tuple(out[1:])
    small_shapes = [given[n][0].shape for n in SMALL_NAMES]
    packed = [_pack_small([given[n][k] for n in SMALL_NAMES])[None] for k in range(3)]
    packed_g = _pack_small([grads[n] for n in SMALL_NAMES])[None]
    small_out = adamw(packed[0], packed_g, packed[1], packed[2], "adamw_small")
    small_out = [_unpack_small(o[0], small_shapes) for o in small_out]
    for i, n in enumerate(SMALL_NAMES):
        results[n] = tuple(small_out[k][i] for k in range(3))

    return (loss, grad_x, *[grads[n] for n in order], *[results[n][0] for n in order],
            *[results[n][1] for n in order], *[results[n][2] for n in order])
```

```python
import numpy as np
import jax
import jax.numpy as jnp
from jax import lax
from jax.experimental import pallas as pl
from jax.experimental.pallas import tpu as pltpu

F32 = jnp.float32
BF16 = jnp.bfloat16
MESH = pl.DeviceIdType.MESH
ANY = pl.BlockSpec(memory_space=pl.ANY)

D_MODEL = 1024
DEPTH = 4
EPS = 1e-6
SSD_WIDTH = 512
SSD_HEADS = 8
HEAD_DIM = 64
D_STATE = 128
CHUNK = 128
CONV_WIDTH = 4
CONV_DIM = 1024
SB_WIDTH = 256
POOL_WIDTH = 256
POOL_WINDOWS = (2, 4, 8, 16)
D_FF = 2816
D_IN = 2568
N_CHIPS = 4
OFF_QKV, OFF_Z, OFF_DT, OFF_XBC, OFF_P = 0, 768, 1280, 1536, 2560
D_INP = 2816
ZDT = 768
SHARD_IN, SHARD_OUT, SHARD_FF = 704, 256, 704
COLS_IN = 642
ADAM_LR, ADAM_B1, ADAM_B2, ADAM_EPS, ADAM_WD, ADAM_STEP = 0.001, 0.9, 0.999, 1e-08, 0.01, 10
LANES = 128
VMEM_LIMIT = 56 * 1024 * 1024


def _params(sem=None):
    return pltpu.CompilerParams(dimension_semantics=sem, vmem_limit_bytes=VMEM_LIMIT)


def _tile(n, cap):
    best = None
    for t in range(LANES, min(n, cap) + 1, LANES):
        if n % t == 0:
            best = t
    assert best is not None, (n, cap)
    return best


def _nt(a, b):
    return lax.dot_general(a, b, (((1,), (1,)), ((), ())), preferred_element_type=F32)


def _tn(a, b):
    return lax.dot_general(a, b, (((0,), (0,)), ((), ())), preferred_element_type=F32)


def _nn(a, b):
    return jnp.dot(a, b, preferred_element_type=F32)


def _split_dot(a, b_exact, terms=3, dot=_nn):
    acc = None
    rest = a
    for _ in range(terms):
        hi = rest.astype(BF16)
        part = dot(hi, b_exact)
        acc = part if acc is None else acc + part
        rest = rest - hi.astype(F32)
    return acc


def _split_dot_left(a_exact, b, terms=3):
    acc = None
    rest = b
    for _ in range(terms):
        hi = rest.astype(BF16)
        part = _nn(a_exact, hi)
        acc = part if acc is None else acc + part
        rest = rest - hi.astype(F32)
    return acc


def _sigmoid(x):
    return 1.0 / (1.0 + jnp.exp(-x))


def _softplus(x):
    return jnp.maximum(x, 0.0) + jnp.log(1.0 + jnp.exp(-jnp.abs(x)))


def _iota2(shape, dim):
    return lax.broadcasted_iota(jnp.int32, shape, dim)


def _after(after):
    ops = [] if after is None else list(after) if isinstance(after, (list, tuple)) else [after]
    return [ANY] * len(ops), ops


def rms_matmul(x, nw, wt, name, after=None):
    s, d = x.shape
    n = wt.shape[0]
    tm, tn = _tile(s, 512), _tile(n, 2816)
    specs, ops = _after(after)

    def body(x_ref, nw_ref, w_ref, *rest):
        o_ref, h_ref = rest[len(ops):]

        @pl.when(pl.program_id(1) == 0)
        def _():
            xv = x_ref[...]
            r = lax.rsqrt(jnp.mean(xv * xv, axis=-1, keepdims=True) + EPS)
            h_ref[...] = (xv * r * nw_ref[...]).astype(BF16)
        o_ref[...] = _nt(h_ref[...], w_ref[...])

    return pl.pallas_call(
        body, name=name, grid=(s // tm, n // tn),
        in_specs=[pl.BlockSpec((tm, d), lambda i, j: (i, 0)), pl.BlockSpec((1, d), lambda i, j: (0, 0)),
                  pl.BlockSpec((tn, d), lambda i, j: (j, 0))] + specs,
        out_specs=pl.BlockSpec((tm, tn), lambda i, j: (i, j)),
        out_shape=jax.ShapeDtypeStruct((s, n), F32),
        scratch_shapes=[pltpu.VMEM((tm, d), BF16)],
        compiler_params=_params(("parallel", "arbitrary")),
    )(x, nw, wt, *ops)


def matmul_residual(a, w, res, name):
    s, k = a.shape
    n = w.shape[1]
    tm, tn = _tile(s, 512), _tile(n, 1024)

    def body(a_ref, w_ref, r_ref, o_ref):
        o_ref[...] = r_ref[...] + _nn(a_ref[...], w_ref[...])

    return pl.pallas_call(
        body, name=name, grid=(s // tm, n // tn),
        in_specs=[pl.BlockSpec((tm, k), lambda i, j: (i, 0)), pl.BlockSpec((k, tn), lambda i, j: (0, j)),
                  pl.BlockSpec((tm, tn), lambda i, j: (i, j))],
        out_specs=pl.BlockSpec((tm, tn), lambda i, j: (i, j)),
        out_shape=jax.ShapeDtypeStruct((s, n), F32),
        compiler_params=_params(("parallel", "parallel")),
    )(a, w, res)


def matmul_nt(a, w, name, out_dtype=F32, after=None):
    s, n = a.shape
    k = w.shape[0]
    tm, tk = _tile(s, 512), _tile(k, 1024)
    specs, ops = _after(after)

    def body(a_ref, w_ref, *rest):
        rest[-1][...] = _nt(a_ref[...], w_ref[...]).astype(out_dtype)

    return pl.pallas_call(
        body, name=name, grid=(s // tm, k // tk),
        in_specs=[pl.BlockSpec((tm, n), lambda i, j: (i, 0)), pl.BlockSpec((tk, n), lambda i, j: (j, 0))] + specs,
        out_specs=pl.BlockSpec((tm, tk), lambda i, j: (i, j)),
        out_shape=jax.ShapeDtypeStruct((s, k), out_dtype),
        compiler_params=_params(("parallel", "parallel")),
    )(a, w, *ops)


def matmul_tn(a, b, name, after=None):
    s, m = a.shape
    n = b.shape[1]
    tm, tn = _tile(m, 512), _tile(n, 1024)

    def body(a_ref, b_ref, *rest):
        rest[-1][...] = _tn(a_ref[...], b_ref[...]).astype(BF16)

    specs, ops = _after(after)
    return pl.pallas_call(
        body, name=name, grid=(m // tm, n // tn),
        in_specs=[pl.BlockSpec((s, tm), lambda i, j: (0, i)), pl.BlockSpec((s, tn), lambda i, j: (0, j))] + specs,
        out_specs=pl.BlockSpec((tm, tn), lambda i, j: (i, j)),
        out_shape=jax.ShapeDtypeStruct((m, n), BF16),
        compiler_params=_params(("parallel", "parallel")),
    )(a, b, *ops)


def ffn_forward(x1, nw, wgt, wut, wd, name):
    s, d = x1.shape
    f = wgt.shape[0]
    tm, tf = _tile(s, 2048), _tile(f, 256)

    def body(x_ref, nw_ref, wg_ref, wu_ref, wd_ref, o_ref, g_ref, u_ref, h_ref):
        j = pl.program_id(1)

        @pl.when(j == 0)
        def _():
            xv = x_ref[...]
            r = lax.rsqrt(jnp.mean(xv * xv, axis=-1, keepdims=True) + EPS)
            h_ref[...] = (xv * r * nw_ref[...]).astype(BF16)
            o_ref[...] = xv

        h = h_ref[...]
        g = _nt(h, wg_ref[...])
        u = _nt(h, wu_ref[...])
        g_ref[...] = g.astype(BF16)
        u_ref[...] = u.astype(BF16)
        a = (g * _sigmoid(g) * u).astype(BF16)
        o_ref[...] += _nn(a, wd_ref[...])

    wblk = pl.BlockSpec((tf, d), lambda i, j: (j, 0))
    return pl.pallas_call(
        body, name=name, grid=(s // tm, f // tf),
        in_specs=[pl.BlockSpec((tm, d), lambda i, j: (i, 0), pipeline_mode=pl.Buffered(1)),
                  pl.BlockSpec((1, d), lambda i, j: (0, 0)), wblk, wblk, wblk],
        out_specs=[pl.BlockSpec((tm, d), lambda i, j: (i, 0)), pl.BlockSpec((tm, tf), lambda i, j: (i, j)),
                   pl.BlockSpec((tm, tf), lambda i, j: (i, j))],
        out_shape=[jax.ShapeDtypeStruct((s, d), F32), jax.ShapeDtypeStruct((s, f), BF16),
                   jax.ShapeDtypeStruct((s, f), BF16)],
        scratch_shapes=[pltpu.VMEM((tm, d), BF16)],
        compiler_params=_params(("parallel", "arbitrary")),
    )(x1, nw, wgt, wut, wd)


def ffn_backward_act(dx2, g, u, wd, name, after=None):
    s, d = dx2.shape
    f = wd.shape[0]
    tm, tf = _tile(s, 256), _tile(f, 2816)
    specs, ops = _after(after)

    def body(dx_ref, g_ref, u_ref, wd_ref, *rest):
        dg_ref, du_ref, a_ref = rest[len(ops):]
        da = _nt(dx_ref[...], wd_ref[...])
        gv = g_ref[...].astype(F32)
        uv = u_ref[...].astype(F32)
        sg = _sigmoid(gv)
        silu = gv * sg
        dg_ref[...] = (da * uv * (sg * (1.0 + gv * (1.0 - sg)))).astype(BF16)
        du_ref[...] = (da * silu).astype(BF16)
        a_ref[...] = (silu * uv).astype(BF16)

    blk = pl.BlockSpec((tm, tf), lambda i, j: (i, j))
    return pl.pallas_call(
        body, name=name, grid=(s // tm, f // tf),
        in_specs=[pl.BlockSpec((tm, d), lambda i, j: (i, 0)), blk, blk, pl.BlockSpec((tf, d), lambda i, j: (j, 0))] + specs,
        out_specs=[blk, blk, blk],
        out_shape=[jax.ShapeDtypeStruct((s, f), BF16)] * 3,
        compiler_params=_params(("parallel", "parallel")),
    )(dx2, g, u, wd, *ops)


def rms_backward(dzs, wts, x, nw, dres, name, tm, after=None):
    s, d = x.shape
    nz = len(dzs)
    specs, ops = _after(after)

    def body(*refs):
        dz_refs, w_refs = refs[:nz], refs[nz:2 * nz]
        x_ref, nw_ref, dres_ref = refs[2 * nz:2 * nz + 3]
        dx_ref, dxb_ref, h_ref, dnw_ref = refs[2 * nz + 3 + len(ops):]
        dh = _nn(dz_refs[0][...], w_refs[0][...])
        for k in range(1, nz):
            dh = dh + _nn(dz_refs[k][...], w_refs[k][...])
        xv = x_ref[...]
        r = lax.rsqrt(jnp.mean(xv * xv, axis=-1, keepdims=True) + EPS)
        xhat = xv * r
        nwv = nw_ref[...]
        h_ref[...] = (xhat * nwv).astype(BF16)

        @pl.when(pl.program_id(0) == 0)
        def _():
            dnw_ref[...] = jnp.zeros_like(dnw_ref)

        dnw_ref[...] += jnp.sum(dh * xhat, axis=0, keepdims=True)
        gdh = dh * nwv
        dx = dres_ref[...] + r * (gdh - xhat * jnp.mean(gdh * xhat, axis=-1, keepdims=True))
        dx_ref[...] = dx
        dxb_ref[...] = dx.astype(BF16)

    row = pl.BlockSpec((tm, d), lambda i: (i, 0))
    in_specs = [pl.BlockSpec((tm, dz.shape[1]), lambda i: (i, 0)) for dz in dzs]
    in_specs += [pl.BlockSpec(w.shape, lambda i: (0, 0), pipeline_mode=pl.Buffered(1)) for w in wts]
    in_specs += [row, pl.BlockSpec((1, d), lambda i: (0, 0)), row] + specs
    return pl.pallas_call(
        body, name=name, grid=(s // tm,),
        in_specs=in_specs,
        out_specs=[row, row, row, pl.BlockSpec((1, d), lambda i: (0, 0))],
        out_shape=[jax.ShapeDtypeStruct((s, d), F32), jax.ShapeDtypeStruct((s, d), BF16),
                   jax.ShapeDtypeStruct((s, d), BF16), jax.ShapeDtypeStruct((1, d), F32)],
        compiler_params=_params(("arbitrary",)),
    )(*dzs, *wts, x, nw, dres, *ops)


def loss_head(x, nw, target, name):
    s, d = x.shape
    tm = _tile(s, 512)

    def body(x_ref, nw_ref, t_ref, loss_ref, dx_ref, dxb_ref, dnw_ref):
        xv = x_ref[...]
        r = lax.rsqrt(jnp.mean(xv * xv, axis=-1, keepdims=True) + EPS)
        xhat = xv * r
        nwv = nw_ref[...]
        err = xhat * nwv - t_ref[...]

        @pl.when(pl.program_id(0) == 0)
        def _():
            dnw_ref[...] = jnp.zeros_like(dnw_ref)
            loss_ref[...] = jnp.zeros_like(loss_ref)

        part = jnp.sum(jnp.sum(err * err, axis=-1, keepdims=True), axis=0, keepdims=True) * (0.5 / d)
        loss_ref[...] += jnp.broadcast_to(part, loss_ref.shape)
        dout = err * (1.0 / d)
        dnw_ref[...] += jnp.sum(dout * xhat, axis=0, keepdims=True)
        gdh = dout * nwv
        dx = r * (gdh - xhat * jnp.mean(gdh * xhat, axis=-1, keepdims=True))
        dx_ref[...] = dx
        dxb_ref[...] = dx.astype(BF16)

    row = pl.BlockSpec((tm, d), lambda i: (i, 0))
    return pl.pallas_call(
        body, name=name, grid=(s // tm,),
        in_specs=[row, pl.BlockSpec((1, d), lambda i: (0, 0)), row],
        out_specs=[pl.BlockSpec((1, LANES), lambda i: (0, 0)), row, row, pl.BlockSpec((1, d), lambda i: (0, 0))],
        out_shape=[jax.ShapeDtypeStruct((1, LANES), F32), jax.ShapeDtypeStruct((s, d), F32),
                   jax.ShapeDtypeStruct((s, d), BF16), jax.ShapeDtypeStruct((1, d), F32)],
        compiler_params=_params(("arbitrary",)),
    )(x, nw, target)


def _shift_down(x, k):
    return jnp.where(_iota2(x.shape, 0) >= k, pltpu.roll(x, k, axis=0), 0.0)


def _shift_up(x, k):
    s = x.shape[0]
    return jnp.where(_iota2(x.shape, 0) < s - k, pltpu.roll(x, s - k, axis=0), 0.0)


CONV_TILE = 256


def conv_forward(proj, cw, cb, name):
    s = proj.shape[0]
    tn = CONV_TILE
    off = OFF_XBC // tn

    def body(u_ref, w_ref, b_ref, o_ref):
        u = u_ref[...]
        pre = b_ref[...] + w_ref[CONV_WIDTH - 1:CONV_WIDTH, :] * u
        for i in range(CONV_WIDTH - 1):
            pre = pre + w_ref[i:i + 1, :] * _shift_down(u, CONV_WIDTH - 1 - i)
        o_ref[...] = pre * _sigmoid(pre)

    return pl.pallas_call(
        body, name=name, grid=(CONV_DIM // tn,),
        in_specs=[pl.BlockSpec((s, tn), lambda j: (0, off + j)), pl.BlockSpec((8, tn), lambda j: (0, j)),
                  pl.BlockSpec((1, tn), lambda j: (0, j))],
        out_specs=pl.BlockSpec((s, tn), lambda j: (0, j)),
        out_shape=jax.ShapeDtypeStruct((s, CONV_DIM), F32),
        compiler_params=_params(("parallel",)),
    )(proj, cw, cb)


def conv_backward(proj, dxc, cw, cb, dproj, name, after=None):
    s = proj.shape[0]
    tn = CONV_TILE
    off = OFF_XBC // tn

    specs, ops = _after(after)

    def body(u_ref, d_ref, w_ref, b_ref, *rest):
        du_ref, dw_ref, db_ref = rest[-3:]
        u = u_ref[...]
        shifted = [_shift_down(u, CONV_WIDTH - 1 - i) for i in range(CONV_WIDTH - 1)] + [u]
        pre = b_ref[...] + w_ref[CONV_WIDTH - 1:CONV_WIDTH, :] * u
        for i in range(CONV_WIDTH - 1):
            pre = pre + w_ref[i:i + 1, :] * shifted[i]
        sg = _sigmoid(pre)
        dpre = d_ref[...] * (sg * (1.0 + pre * (1.0 - sg)))
        du = w_ref[CONV_WIDTH - 1:CONV_WIDTH, :] * dpre
        for i in range(CONV_WIDTH - 1):
            du = du + w_ref[i:i + 1, :] * _shift_up(dpre, CONV_WIDTH - 1 - i)
        du_ref[...] = du.astype(BF16)
        rows = [jnp.sum(dpre * shifted[i], axis=0, keepdims=True) for i in range(CONV_WIDTH)]
        rows.append(jnp.zeros((8 - CONV_WIDTH, tn), F32))
        dw_ref[...] = jnp.concatenate(rows, axis=0)
        db_ref[...] = jnp.sum(dpre, axis=0, keepdims=True)

    return pl.pallas_call(
        body, name=name, grid=(CONV_DIM // tn,),
        in_specs=[pl.BlockSpec((s, tn), lambda j: (0, off + j)), pl.BlockSpec((s, tn), lambda j: (0, j)),
                  pl.BlockSpec((8, tn), lambda j: (0, j)), pl.BlockSpec((1, tn), lambda j: (0, j)), ANY] + specs,
        out_specs=[pl.BlockSpec((s, tn), lambda j: (0, off + j)), pl.BlockSpec((8, tn), lambda j: (0, j)),
                   pl.BlockSpec((1, tn), lambda j: (0, j))],
        out_shape=[jax.ShapeDtypeStruct(dproj.shape, BF16), jax.ShapeDtypeStruct((8, CONV_DIM), F32),
                   jax.ShapeDtypeStruct((1, CONV_DIM), F32)],
        input_output_aliases={4: 0},
        compiler_params=_params(("parallel",)),
    )(proj, dxc, cw, cb, dproj, *ops)


def _pool_lane_window(shape):
    grp = _iota2(shape, 1) // (POOL_WIDTH // len(POOL_WINDOWS))
    win = jnp.full(shape, POOL_WINDOWS[-1], jnp.int32)
    for gi in range(len(POOL_WINDOWS) - 2, -1, -1):
        win = jnp.where(grp == gi, POOL_WINDOWS[gi], win)
    return grp, win


def _pool_select(grp, sums):
    out = sums[-1]
    for gi in range(len(sums) - 2, -1, -1):
        out = jnp.where(grp == gi, sums[gi], out)
    return out


def _pool_pooled(p):
    grp, win = _pool_lane_window(p.shape)
    inv_count = 1.0 / jnp.minimum(_iota2(p.shape, 0) + 1, win).astype(F32)
    sums, acc, k = [], p, 1
    for _ in POOL_WINDOWS:
        acc = acc + _shift_down(acc, k)
        sums.append(acc)
        k *= 2
    return _pool_select(grp, sums) * inv_count - p, grp, inv_count


def pool_forward(proj, wbd, pb, ps, y_all, name, after=None):
    s = proj.shape[0]
    specs, ops = _after(after)

    def body(p_ref, w_ref, b_ref, s_ref, *rest):
        o_ref = rest[-1]
        pooled, _, _ = _pool_pooled(p_ref[...])
        mixed = _nn(pooled.astype(BF16), w_ref[...]) + b_ref[...]
        o_ref[...] = (mixed * s_ref[...]).astype(BF16)

    vec = pl.BlockSpec((1, POOL_WIDTH), lambda j: (0, 0))
    return pl.pallas_call(
        body, name=name, grid=(1,),
        in_specs=[pl.BlockSpec((s, POOL_WIDTH), lambda j: (0, OFF_P // POOL_WIDTH)),
                  pl.BlockSpec((POOL_WIDTH, POOL_WIDTH), lambda j: (0, 0)), vec, vec, ANY] + specs,
        out_specs=pl.BlockSpec((s, POOL_WIDTH), lambda j: (0, (SSD_WIDTH + SB_WIDTH) // POOL_WIDTH)),
        out_shape=jax.ShapeDtypeStruct(y_all.shape, BF16),
        input_output_aliases={4: 0},
        compiler_params=_params(("arbitrary",)),
    )(proj, wbd, pb, ps, y_all, *ops)


def pool_backward(proj, dyall, wbd, pb, ps, dproj, name):
    s = proj.shape[0]

    def body(p_ref, dy_ref, w_ref, b_ref, s_ref, _, dp_ref, dw_ref, db_ref, ds_ref):
        pooled, grp, inv_count = _pool_pooled(p_ref[...])
        pooled_b = pooled.astype(BF16)
        mixed = _nn(pooled_b, w_ref[...]) + b_ref[...]
        dy = dy_ref[...]
        ds_ref[...] = jnp.sum(dy * mixed, axis=0, keepdims=True)
        dmixed = dy * s_ref[...]
        db_ref[...] = jnp.sum(dmixed, axis=0, keepdims=True)
        dmixed_b = dmixed.astype(BF16)
        dw_ref[...] = _tn(pooled_b, dmixed_b)
        dpooled = _nt(dmixed_b, w_ref[...])
        sums, acc, k = [], dpooled * inv_count, 1
        for _ in POOL_WINDOWS:
            acc = acc + _shift_up(acc, k)
            sums.append(acc)
            k *= 2
        dp_ref[...] = (_pool_select(grp, sums) - dpooled).astype(BF16)

    vec = pl.BlockSpec((1, POOL_WIDTH), lambda j: (0, 0))
    mat = pl.BlockSpec((POOL_WIDTH, POOL_WIDTH), lambda j: (0, 0))
    pcol = pl.BlockSpec((s, POOL_WIDTH), lambda j: (0, OFF_P // POOL_WIDTH))
    return pl.pallas_call(
        body, name=name, grid=(1,),
        in_specs=[pcol, pl.BlockSpec((s, POOL_WIDTH), lambda j: (0, (SSD_WIDTH + SB_WIDTH) // POOL_WIDTH)), mat, vec, vec, ANY],
        out_specs=[pcol, mat, vec, vec],
        out_shape=[jax.ShapeDtypeStruct(dproj.shape, BF16), jax.ShapeDtypeStruct((POOL_WIDTH, POOL_WIDTH), F32),
                   jax.ShapeDtypeStruct((1, POOL_WIDTH), F32), jax.ShapeDtypeStruct((1, POOL_WIDTH), F32)],
        input_output_aliases={5: 0},
        compiler_params=_params(("arbitrary",)),
    )(proj, dyall, wbd, pb, ps, dproj)


N_PAIRS = SSD_HEADS // 2


def _ssd_common(xc, dtraw, dtb, alog):
    c = CHUNK
    dt = _softplus(dtraw + dtb)
    a = -jnp.exp(alog)
    ltri = (_iota2((c, c), 0) >= _iota2((c, c), 1)).astype(BF16)
    acum = _split_dot_left(ltri, dt * a)
    expand = (_iota2((c, SSD_WIDTH), 1) // HEAD_DIM == _iota2((c, SSD_WIDTH), 0)).astype(BF16)
    expand_wide = (_iota2((c, SSD_HEADS * c), 1) // c == _iota2((c, SSD_HEADS * c), 0)).astype(BF16)
    acum_x = _split_dot(acum, expand, 2)
    dt_x = _split_dot(dt, expand, 2)
    alast_x = acum_x[c - 1:c, :]
    return dict(dt=dt, a=a, acum=acum, acum_x=acum_x, dt_x=dt_x, ea_x=jnp.exp(acum_x),
                dte_x=jnp.exp(alast_x - acum_x), eal_x=jnp.exp(alast_x),
                acol=_split_dot(acum, expand_wide, 2), acum_t=acum.T,
                xs=xc[:, :SSD_WIDTH], causal=_iota2((c, c), 0) >= _iota2((c, c), 1),
                left=_iota2((c, c), 1) < HEAD_DIM)


def _ssd_group(xc, g):
    b = xc[:, SSD_WIDTH + D_STATE * g:SSD_WIDTH + D_STATE * (g + 1)]
    cm = xc[:, SSD_WIDTH + 2 * D_STATE + D_STATE * g:SSD_WIDTH + 2 * D_STATE + D_STATE * (g + 1)]
    return b, cm


def _ssd_decay(q, hh):
    col = q["acol"][:, CHUNK * hh:CHUNK * (hh + 1)]
    row = q["acum_t"][hh:hh + 1, :]
    return jnp.where(q["causal"], jnp.exp(jnp.minimum(col - row, 0.0)), 0.0)


def ssd_forward(proj, xc, dtb, alog, dskip_x, nw, name):
    s = xc.shape[0]
    nc = s // CHUNK

    def body(xc_ref, zdt_ref, dtb_ref, alog_ref, dsk_ref, nw_ref, y_ref, yc_ref, st_ref, state):
        @pl.when(pl.program_id(0) == 0)
        def _():
            state[...] = jnp.zeros_like(state)

        xcv = xc_ref[...]
        q = _ssd_common(xcv, zdt_ref[:, SSD_WIDTH:SSD_WIDTH + LANES], dtb_ref[...], alog_ref[...])
        x = q["xs"] * q["dt_x"]
        xb = x.astype(BF16)
        xd = (x * q["dte_x"]).astype(BF16)
        pieces = []
        for g in range(2):
            bg, cg = _ssd_group(xcv, g)
            bgb, cgb = bg.astype(BF16), cg.astype(BF16)
            cb = _nt(cgb, bgb)
            bgt = bg.T.astype(BF16)
            for pr in (2 * g, 2 * g + 1):
                sl = slice(CHUNK * pr, CHUNK * (pr + 1))
                st = state[pr]
                st_ref[0, pr] = st
                yp = _nn(cgb, st.astype(BF16)) * q["ea_x"][:, sl]
                for k, hh in enumerate((2 * pr, 2 * pr + 1)):
                    w = (cb * _ssd_decay(q, hh)).astype(BF16)
                    mask = q["left"] if k == 0 else jnp.logical_not(q["left"])
                    yp = yp + _nn(w, jnp.where(mask, xb[:, sl], jnp.zeros_like(xb[:, sl])))
                state[pr] = st * q["eal_x"][:, sl] + _nn(bgt, xd[:, sl])
                pieces.append(yp)
        y = jnp.concatenate(pieces, axis=1) + q["xs"] * dsk_ref[...]
        yc_ref[...] = y
        zv = zdt_ref[:, :SSD_WIDTH]
        yg = y * (zv * _sigmoid(zv))
        r = lax.rsqrt(jnp.mean(yg * yg, axis=-1, keepdims=True) + EPS)
        y_ref[...] = (yg * r * nw_ref[...]).astype(BF16)

    vec = lambda n: pl.BlockSpec((1, n), lambda c: (0, 0))
    return pl.pallas_call(
        body, name=name, grid=(nc,),
        in_specs=[pl.BlockSpec((CHUNK, CONV_DIM), lambda c: (c, 0)),
                  pl.BlockSpec((CHUNK, ZDT), lambda c: (c, OFF_Z // ZDT)),
                  vec(LANES), vec(LANES), vec(SSD_WIDTH), vec(SSD_WIDTH)],
        out_specs=[pl.BlockSpec((CHUNK, SSD_WIDTH), lambda c: (c, 0)), pl.BlockSpec((CHUNK, SSD_WIDTH), lambda c: (c, 0)),
                   pl.BlockSpec((1, N_PAIRS, D_STATE, CHUNK), lambda c: (c, 0, 0, 0))],
        out_shape=[jax.ShapeDtypeStruct((s, D_MODEL), BF16), jax.ShapeDtypeStruct((s, SSD_WIDTH), F32),
                   jax.ShapeDtypeStruct((nc, N_PAIRS, D_STATE, CHUNK), F32)],
        scratch_shapes=[pltpu.VMEM((N_PAIRS, D_STATE, CHUNK), F32)],
        compiler_params=_params(("arbitrary",)),
    )(xc, proj, dtb, alog, dskip_x, nw)


def ssd_backward(proj, xc, ycore, dyall, states, dtb, alog, dskip_x, nw, name):
    s = xc.shape[0]
    nc = s // CHUNK
    c = CHUNK

    def body(xc_ref, zdt_ref, yc_ref, dy_ref, st_ref, dtb_ref, alog_ref, dsk_ref, nw_ref,
             dxc_ref, dzdt_ref, dnw_ref, ddsk_ref, ddtb_ref, dalog_ref, dstate):
        @pl.when(pl.program_id(0) == 0)
        def _():
            dstate[...] = jnp.zeros_like(dstate)
            dnw_ref[...] = jnp.zeros_like(dnw_ref)
            ddsk_ref[...] = jnp.zeros_like(ddsk_ref)
            ddtb_ref[...] = jnp.zeros_like(ddtb_ref)
            dalog_ref[...] = jnp.zeros_like(dalog_ref)

        xcv = xc_ref[...]
        dtraw = zdt_ref[:, SSD_WIDTH:SSD_WIDTH + LANES]
        q = _ssd_common(xcv, dtraw, dtb_ref[...], alog_ref[...])
        xs = q["xs"]
        x = xs * q["dt_x"]
        zv, yc, dy, nwv = zdt_ref[:, :SSD_WIDTH], yc_ref[...], dy_ref[...], nw_ref[...]
        sgz = _sigmoid(zv)
        siluz = zv * sgz
        yg = yc * siluz
        r = lax.rsqrt(jnp.mean(yg * yg, axis=-1, keepdims=True) + EPS)
        dnw_ref[...] += jnp.sum(dy * yg * r, axis=0, keepdims=True)
        g1 = dy * nwv
        dyg = r * (g1 - yg * (r * r) * jnp.mean(g1 * yg, axis=-1, keepdims=True))
        dyv = dyg * siluz
        dz = (dyg * yc * (sgz * (1.0 + zv * (1.0 - sgz)))).astype(BF16)
        ddsk_ref[...] += jnp.sum(dyv * xs, axis=0, keepdims=True)
        dye = dyv * q["ea_x"]
        dx_parts, yoff_parts, u_parts, v_parts, e_parts = [], [], [], [], []
        db_parts, dc_parts = [], []
        for g in range(2):
            bg, cg = _ssd_group(xcv, g)
            bgb, cgb = bg.astype(BF16), cg.astype(BF16)
            cb = _nt(cgb, bgb)
            cgt = cg.T.astype(BF16)
            dgsum = jnp.zeros((c, c), F32)
            dbg = jnp.zeros((c, D_STATE), F32)
            dcg = jnp.zeros((c, D_STATE), F32)
            for pr in (2 * g, 2 * g + 1):
                sl = slice(c * pr, c * (pr + 1))
                st = st_ref[0, pr]
                dst = dstate[pr]
                stb, dstb = st.astype(BF16), dst.astype(BF16)
                xp = x[:, sl]
                xpb = xp.astype(BF16)
                dyp = dyv[:, sl]
                xdp = xp * q["dte_x"][:, sl]
                yoff_parts.append(_nn(cgb, stb) * q["ea_x"][:, sl])
                rr = _nn(bgb, dstb)
                dxp = rr * q["dte_x"][:, sl]
                u_parts.append(rr * xdp)
                v_parts.append(dst * st * q["eal_x"][:, sl])
                for k, hh in enumerate((2 * pr, 2 * pr + 1)):
                    decay = _ssd_decay(q, hh)
                    w = cb * decay
                    mask = q["left"] if k == 0 else jnp.logical_not(q["left"])
                    dym = jnp.where(mask, dyp, 0.0).astype(BF16)
                    dw = _nt(dym, xpb)
                    dgsum = dgsum + dw * decay
                    e_parts.append(dw * w)
                    dxp = dxp + _nn(w.T.astype(BF16), dym)
                dyeb = dye[:, sl].astype(BF16)
                dcg = dcg + _nt(dyeb, stb)
                dbg = dbg + _nt(xdp.astype(BF16), dstb)
                dstate[pr] = dst * q["eal_x"][:, sl] + _nn(cgt, dyeb)
                dx_parts.append(dxp)
            dcg = dcg + _nn(dgsum.astype(BF16), bgb)
            dbg = dbg + _nn(dgsum.T.astype(BF16), cgb)
            db_parts.append(dbg)
            dc_parts.append(dcg)
        dx = jnp.concatenate(dx_parts, axis=1)
        yoff = jnp.concatenate(yoff_parts, axis=1)
        u = jnp.concatenate(u_parts, axis=1)
        v = jnp.concatenate(v_parts, axis=1)
        reduce_heads = (_iota2((SSD_WIDTH, c), 0) // HEAD_DIM == _iota2((SSD_WIDTH, c), 1)).astype(BF16)
        to_head = (_iota2((SSD_HEADS * c, c), 0) // c == _iota2((SSD_HEADS * c, c), 1)).astype(BF16)
        da = _split_dot(dyv * yoff - u, reduce_heads, 2)
        da = da + _split_dot(jnp.concatenate(e_parts, axis=1), to_head, 2)
        da = da - _split_dot(jnp.concatenate(e_parts, axis=0), to_head, 2, dot=_tn)
        dalast = jnp.sum(_split_dot(u + v, reduce_heads, 2), axis=0, keepdims=True)
        da = da + jnp.where(_iota2((c, c), 0) == c - 1, dalast, 0.0)
        utri = (_iota2((c, c), 1) >= _iota2((c, c), 0)).astype(BF16)
        dda = _split_dot_left(utri, da)
        ddt = dda * q["a"] + _split_dot(dx * xs, reduce_heads, 2)
        dalog_ref[...] += jnp.sum(dda * q["dt"], axis=0, keepdims=True) * q["a"]
        ddtraw = jnp.where(_iota2((c, c), 1) < SSD_HEADS, ddt * _sigmoid(dtraw + dtb_ref[...]), 0.0)
        ddtb_ref[...] += jnp.sum(ddtraw, axis=0, keepdims=True)
        dzdt_ref[...] = jnp.concatenate([dz, ddtraw.astype(BF16), jnp.zeros((c, ZDT - SSD_WIDTH - LANES), BF16)], axis=1)
        dxs = dx * q["dt_x"] + dyv * dsk_ref[...]
        dxc_ref[...] = jnp.concatenate([dxs] + db_parts + dc_parts, axis=1)

    rev = lambda i: nc - 1 - i
    vec = lambda n: pl.BlockSpec((1, n), lambda i: (0, 0))
    wide = pl.BlockSpec((c, SSD_WIDTH), lambda i: (rev(i), 0))
    zdt = pl.BlockSpec((c, ZDT), lambda i: (rev(i), OFF_Z // ZDT))
    return pl.pallas_call(
        body, name=name, grid=(nc,),
        in_specs=[pl.BlockSpec((c, CONV_DIM), lambda i: (rev(i), 0)), zdt, wide, wide,
                  pl.BlockSpec((1, N_PAIRS, D_STATE, c), lambda i: (rev(i), 0, 0, 0)),
                  vec(LANES), vec(LANES), vec(SSD_WIDTH), vec(SSD_WIDTH)],
        out_specs=[pl.BlockSpec((c, CONV_DIM), lambda i: (rev(i), 0)), zdt,
                   vec(SSD_WIDTH), vec(SSD_WIDTH), vec(LANES), vec(LANES)],
        out_shape=[jax.ShapeDtypeStruct((s, CONV_DIM), F32), jax.ShapeDtypeStruct((s, D_INP), BF16),
                   jax.ShapeDtypeStruct((1, SSD_WIDTH), F32),
                   jax.ShapeDtypeStruct((1, SSD_WIDTH), F32), jax.ShapeDtypeStruct((1, LANES), F32),
                   jax.ShapeDtypeStruct((1, LANES), F32)],
        scratch_shapes=[pltpu.VMEM((N_PAIRS, D_STATE, c), F32)],
        compiler_params=_params(("arbitrary",)),
    )(xc, proj, ycore, dyall, states, dtb, alog, dskip_x, nw)


SB_Q, SB_K = 512, 512
SB_T = 256
SB_SCALE = HEAD_DIM ** -0.5


def _sb_weights(qm, kb, diagonal, run_lk, strict_after):
    z = _nt(qm, kb)
    nz = -z
    tail = jnp.log(1.0 + jnp.exp(jnp.minimum(z, nz)))
    ls = jnp.minimum(z, 0.0) - tail
    lk = jnp.minimum(nz, 0.0) - tail
    if diagonal is not None:
        valid = _iota2(z.shape, 1) < _iota2(z.shape, 0) + diagonal
        lk = jnp.where(valid, lk, 0.0)
    w = jnp.exp(ls + _nn(lk.astype(BF16), strict_after) + run_lk)
    if diagonal is not None:
        w = jnp.where(valid, w, 0.0)
    return ls, jnp.sum(lk, axis=1, keepdims=True), w


SB_RUNS = tuple(reversed(range(SB_K // SB_T)))


def _key_suffix(x, tri, terms):
    runs = [x[:, SB_T * k:SB_T * (k + 1)] for k in range(SB_K // SB_T)]
    sums = [_split_dot(r, tri, terms) for r in runs]
    later = None
    for k in range(len(runs) - 1, -1, -1):
        if later is not None:
            sums[k] = sums[k] + later
        total = jnp.sum(runs[k], axis=1, keepdims=True)
        later = total if later is None else later + total
    return jnp.concatenate(sums, axis=1), later


def _sb_block_weights(qm, kb, diagonal, run_lk, strict_after):
    z = _nt(qm, kb)
    nz = -z
    tail = jnp.log(1.0 + jnp.exp(jnp.minimum(z, nz)))
    ls = jnp.minimum(z, 0.0) - tail
    lk = jnp.minimum(nz, 0.0) - tail
    if diagonal is not None:
        valid = _iota2(z.shape, 1) < _iota2(z.shape, 0) + diagonal
        lk = jnp.where(valid, lk, 0.0)
    after, total = _key_suffix(lk, strict_after, 1)
    w = jnp.exp(ls + after + run_lk)
    if diagonal is not None:
        w = jnp.where(valid, w, 0.0)
    return ls, total, w


def _sb_sweep(i, block, init):
    own = (i * SB_Q) // SB_K
    first = block(own, init, i * SB_Q - own * SB_K)
    return lax.fori_loop(1, own + 1, lambda jj, carry: block(own - jj, carry, None), first)


def sb_forward(proj, y_all, name, after=None):
    s = proj.shape[0]
    t, tk = SB_Q, SB_K
    nq = s // t
    specs, ops = _after(after)

    def body(q_ref, k_ref, v_ref, *rest):
        y_ref, o_ref = rest[-2:]
        i = pl.program_id(1)
        left = _iota2((t, LANES), 1) < HEAD_DIM
        left_k = _iota2((tk, LANES), 1) < HEAD_DIM
        qv = q_ref[...] * SB_SCALE
        zero = jnp.zeros_like(qv)
        qms = (jnp.where(left, qv, zero).astype(BF16), jnp.where(left, zero, qv).astype(BF16))
        strict_after = (_iota2((SB_T, SB_T), 0) > _iota2((SB_T, SB_T), 1)).astype(BF16)

        def block(j, carry, diagonal):
            o, runs = carry[0], carry[1:]
            rows = pl.ds(pl.multiple_of(j * tk, tk), tk)
            kb = k_ref[rows, :].astype(BF16)
            vv = v_ref[rows, :]
            new_runs = []
            for k in range(2):
                vm = jnp.where(left_k if k == 0 else jnp.logical_not(left_k), vv, 0.0).astype(BF16)
                run = runs[k]
                for r in SB_RUNS:
                    keys = slice(SB_T * r, SB_T * (r + 1))
                    _, total, w = _sb_weights(qms[k], kb[keys], None if diagonal is None else diagonal - SB_T * r,
                                              run, strict_after)
                    o = o + _nn(w.astype(BF16), vm[keys])
                    run = run + total
                new_runs.append(run)
            return (o, *new_runs)

        init = (jnp.zeros((t, LANES), F32), jnp.zeros((t, 1), F32), jnp.zeros((t, 1), F32))
        o = _sb_sweep(i, block, init)[0]
        o_ref[...] = o
        y_ref[...] = o.astype(BF16)

    return pl.pallas_call(
        body, name=name, grid=(2, nq),
        in_specs=[pl.BlockSpec((t, LANES), lambda p, i: (i, 3 * p)),
                  pl.BlockSpec((s, LANES), lambda p, i: (0, 3 * p + 1)),
                  pl.BlockSpec((s, LANES), lambda p, i: (0, 3 * p + 2)), ANY] + specs,
        out_specs=[pl.BlockSpec((t, LANES), lambda p, i: (i, SSD_WIDTH // LANES + p)),
                   pl.BlockSpec((t, LANES), lambda p, i: (i, p))],
        out_shape=[jax.ShapeDtypeStruct(y_all.shape, BF16), jax.ShapeDtypeStruct((s, SB_WIDTH), F32)],
        input_output_aliases={3: 0},
        compiler_params=_params(("parallel", "arbitrary")),
    )(proj, proj, proj, y_all, *ops)


def sb_backward(proj, o, dyall, dproj, name, after=None):
    s = proj.shape[0]
    t, tk = SB_Q, SB_K
    nq = s // t
    specs, ops = _after(after)

    def body(q_ref, k_ref, v_ref, o_ref, do_ref, *rest):
        dqkv_ref, dk_acc, dv_acc = rest[-3:]
        dk_acc[...] = jnp.zeros_like(dk_acc)
        dv_acc[...] = jnp.zeros_like(dv_acc)
        left = _iota2((t, LANES), 1) < HEAD_DIM
        lane_masks = (left, jnp.logical_not(left))
        left_k = _iota2((tk, LANES), 1) < HEAD_DIM
        key_masks = (left_k, jnp.logical_not(left_k))
        strict_after = (_iota2((SB_T, SB_T), 0) > _iota2((SB_T, SB_T), 1)).astype(BF16)
        from_here = (_iota2((SB_T, SB_T), 0) >= _iota2((SB_T, SB_T), 1)).astype(BF16)

        def query_block(i, _):
            qrows = pl.ds(pl.multiple_of(i * t, t), t)
            qv = q_ref[qrows, :] * SB_SCALE
            dov = do_ref[qrows, :]
            zero = jnp.zeros_like(qv)
            qb = qv.astype(BF16)
            dob = dov.astype(BF16)
            prod = dob.astype(F32) * o_ref[qrows, :]
            qms = [jnp.where(m, qv, zero).astype(BF16) for m in lane_masks]
            doms = [jnp.where(m, dov, zero).astype(BF16) for m in lane_masks]
            deltas = [jnp.sum(jnp.where(m, prod, zero), axis=1, keepdims=True) for m in lane_masks]

            def block(j, carry, diagonal):
                dq = carry[0]
                run_lk, run_e = carry[1:3], carry[3:5]
                rows = pl.ds(pl.multiple_of(j * tk, tk), tk)
                kb = k_ref[rows, :].astype(BF16)
                vb = v_ref[rows, :].astype(BF16)
                dkj = jnp.zeros((tk, LANES), F32)
                dvj = jnp.zeros((tk, LANES), F32)
                new_lk, new_e = [], []
                for k in range(2):
                    ls, total, w = _sb_block_weights(qms[k], kb, diagonal, run_lk[k], strict_after)
                    wb = w.astype(BF16)
                    e = _nt(doms[k], vb) * wb.astype(F32)
                    e_from_here, e_total = _key_suffix(e, from_here, 2)
                    before = deltas[k] - e_from_here - run_e[k]
                    dz = e - jnp.exp(ls) * (e + before)
                    if diagonal is not None:
                        dz = jnp.where(_iota2(dz.shape, 1) < _iota2(dz.shape, 0) + diagonal, dz, 0.0)
                    dz = dz.astype(BF16)
                    m = lane_masks[k]
                    dvj = dvj + jnp.where(key_masks[k], _tn(wb, dob), 0.0)
                    dkj = dkj + jnp.where(key_masks[k], _tn(dz, qb), 0.0)
                    dq = dq + jnp.where(m, _nn(dz, kb), 0.0)
                    new_lk.append(run_lk[k] + total)
                    new_e.append(run_e[k] + e_total)
                dk_acc[rows, :] += dkj
                dv_acc[rows, :] += dvj
                return (dq, *new_lk, *new_e)

            col = jnp.zeros((t, 1), F32)
            dq = _sb_sweep(i, block, (jnp.zeros((t, LANES), F32), col, col, col, col))[0]
            dqkv_ref[qrows, 0:LANES] = (dq * SB_SCALE).astype(BF16)
            return 0

        lax.fori_loop(0, nq, query_block, 0)
        dqkv_ref[:, LANES:2 * LANES] = dk_acc[...].astype(BF16)
        dqkv_ref[:, 2 * LANES:3 * LANES] = dv_acc[...].astype(BF16)

    col = lambda f: pl.BlockSpec((s, LANES), f)
    return pl.pallas_call(
        body, name=name, grid=(2,),
        in_specs=[col(lambda p: (0, 3 * p)), col(lambda p: (0, 3 * p + 1)), col(lambda p: (0, 3 * p + 2)),
                  col(lambda p: (0, p)), col(lambda p: (0, SSD_WIDTH // LANES + p)), ANY] + specs,
        out_specs=pl.BlockSpec((s, 3 * LANES), lambda p: (0, p)),
        out_shape=jax.ShapeDtypeStruct(dproj.shape, BF16),
        input_output_aliases={5: 0},
        scratch_shapes=[pltpu.VMEM((s, LANES), F32), pltpu.VMEM((s, LANES), F32)],
        compiler_params=_params(("parallel",)),
    )(proj, proj, proj, o, dyall, dproj, *ops)


def adamw(w, g, m, v, name):
    b, r, c = w.shape
    tr = max([t for t in range(8, min(r, 512) + 1, 8) if r % t == 0], default=r)

    def body(w_ref, g_ref, m_ref, v_ref, d_ref, nm_ref, nv_ref):
        gv = g_ref[...]
        nm = ADAM_B1 * m_ref[...] + (1.0 - ADAM_B1) * gv
        nv = ADAM_B2 * v_ref[...] + (1.0 - ADAM_B2) * (gv * gv)
        m_hat = nm / (1.0 - ADAM_B1 ** ADAM_STEP)
        v_hat = nv / (1.0 - ADAM_B2 ** ADAM_STEP)
        d_ref[...] = -ADAM_LR * (m_hat / (jnp.sqrt(v_hat) + ADAM_EPS) + ADAM_WD * w_ref[...])
        nm_ref[...] = nm
        nv_ref[...] = nv

    blk = pl.BlockSpec((1, tr, c), lambda i, j: (i, j, 0))
    return pl.pallas_call(
        body, name=name, grid=(b, r // tr),
        in_specs=[blk] * 4, out_specs=[blk] * 3,
        out_shape=[jax.ShapeDtypeStruct(w.shape, F32)] * 3,
        compiler_params=_params(("parallel", "parallel")),
    )(w, g, m, v)


def _position():
    return lax.axis_index("x"), lax.axis_index("y"), lax.axis_index("c")


def _flipped(pos, flip):
    return tuple((1 - p) if f else p for p, f in zip(pos, flip))


FLIP_C = (0, 0, 1)
CHIP_FLIPS = {1: (0, 1, 0), 2: (1, 0, 0), 3: (1, 1, 0)}
SHARD_ROWS = (SHARD_IN, SHARD_OUT, SHARD_FF, SHARD_FF, SHARD_FF)


def _rows(start, size):
    return pl.ds(pl.multiple_of(start, 16), size)


HBM = pl.BlockSpec(memory_space=pltpu.HBM)
SEM = pl.BlockSpec(memory_space=pltpu.SEMAPHORE)
EFFECT = pltpu.SideEffectType.DATAFLOW_SIDE_EFFECTING


def _in_hbm(a):
    return pltpu.with_memory_space_constraint(a, pltpu.HBM)


def _landing(shape, dtype):
    return _in_hbm(lax.empty(shape, dtype))


def _copies(plan, pos, src_refs, land_refs, send_sems, recv_sems):
    return [pltpu.make_async_remote_copy(src_ref=src, dst_ref=dst, send_sem=send_sems.at[k], recv_sem=recv_sems.at[k],
                                         device_id=_flipped(pos, flip), device_id_type=MESH)
            for k, (src, dst, flip) in enumerate(plan(pos, src_refs, land_refs))]


def exchange_start(name, srcs, lands, n, plan, after=None):
    ns, nl = len(srcs), len(lands)
    specs, ops = _after(after)

    def body(*refs):
        src_refs, land_refs = refs[:ns], refs[ns:ns + nl]
        send_sems, recv_sems, token = refs[ns + nl + len(ops)], refs[ns + nl + len(ops) + 1], refs[-1]
        for cp in _copies(plan, _position(), src_refs, land_refs, send_sems, recv_sems):
            cp.start()
        token[...] = jnp.zeros_like(token)

    thru = [pltpu.HBM(a.shape, a.dtype) for a in list(srcs) + list(lands)]
    out = pl.pallas_call(
        body, name=name,
        out_shape=(pltpu.SemaphoreType.DMA((n,)), pltpu.SemaphoreType.DMA((n,)), *thru, jax.ShapeDtypeStruct((8, LANES), F32)),
        in_specs=[HBM] * (ns + nl) + specs,
        out_specs=(SEM, SEM, *([HBM] * (ns + nl)), pl.BlockSpec(memory_space=pltpu.VMEM)),
        input_output_aliases={k: 2 + k for k in range(ns + nl)},
        compiler_params=pltpu.CompilerParams(has_side_effects=EFFECT),
    )(*[_in_hbm(a) for a in srcs], *lands, *ops)
    return out[0], out[1], list(out[2:2 + ns]), list(out[2 + ns:2 + ns + nl]), out[-1]


def exchange_wait(name, started, after, plan):
    send_sems, recv_sems, srcs, lands, _ = started
    ns, nl = len(srcs), len(lands)
    specs, ops = _after(after)

    def body(*refs):
        src_refs, land_refs = refs[:ns], refs[ns:ns + nl]
        send_sems, recv_sems = refs[ns + nl], refs[ns + nl + 1]
        for cp in _copies(plan, _position(), src_refs, land_refs, send_sems, recv_sems):
            cp.wait_send()
            cp.wait_recv()

    out = pl.pallas_call(
        body, name=name,
        out_shape=tuple(pltpu.HBM(a.shape, a.dtype) for a in list(srcs) + list(lands)),
        in_specs=[HBM] * (ns + nl) + [SEM, SEM] + specs,
        out_specs=tuple([HBM] * (ns + nl)),
        input_output_aliases={k: k for k in range(ns + nl)},
        compiler_params=pltpu.CompilerParams(has_side_effects=EFFECT),
    )(*srcs, *lands, send_sems, recv_sems, *ops)
    return list(out[:ns]), list(out[ns:])


def _gather_ici_plan(pos, srcs, lands):
    chip, c = 2 * pos[0] + pos[1], pos[2]
    copies = []
    for src, dst in zip(srcs, lands):
        r = src.shape[0]
        h = r // 2
        for f in (1, 2, 3):
            copies.append((src.at[_rows(c * h, h)], dst.at[_rows(chip * r + c * h, h)], CHIP_FLIPS[f]))
    return copies


def _gather_d2d_plan(pos, srcs, lands):
    chip, c = 2 * pos[0] + pos[1], pos[2]
    copies = []
    for own, dst in zip(srcs, lands):
        r = own.shape[0]
        h = r // 2
        copies.append((own, dst.at[_rows(chip * r, r)], FLIP_C))
        for f in (1, 2, 3):
            at = _rows(lax.bitwise_xor(chip, f) * r + c * h, h)
            copies.append((dst.at[at], dst.at[at], FLIP_C))
    return copies


def gather_ici_start(shards, after=None):
    lands = [_landing((N_CHIPS * a.shape[0], D_MODEL), BF16) for a in shards]
    return exchange_start("gather_ici_start", shards, lands, 3 * len(shards), _gather_ici_plan, after=after)


def gather_d2d_start(shards, fulls, after=None):
    return exchange_start("gather_d2d_start", shards, fulls, 4 * len(shards), _gather_d2d_plan, after=after)


def _reduce_d2d_plan(pos, srcs, lands):
    c = pos[2]
    return [(src.at[:, _rows((1 - c) * (src.shape[1] // 2), src.shape[1] // 2)], dst, FLIP_C) for src, dst in zip(srcs, lands)]


def _reduce_ici_plan(pos, srcs, lands):
    chip = 2 * pos[0] + pos[1]
    return [(src.at[lax.bitwise_xor(chip, f)], dst.at[f - 1], CHIP_FLIPS[f]) for src, dst in zip(srcs, lands) for f in (1, 2, 3)]


def _reduce_swap_plan(pos, srcs, lands):
    c = pos[2]
    copies = []
    for dst in lands:
        h = dst.shape[0] // 2
        at = _rows(c * h, h)
        copies.append((dst.at[at], dst.at[at], FLIP_C))
    return copies


def reduce_d2d_start(grads):
    lands = [_landing((N_CHIPS, g.shape[1] // 2, D_MODEL), BF16) for g in grads]
    return exchange_start("reduce_d2d_start", grads, lands, len(grads), _reduce_d2d_plan)


def reduce_ici_start(chip_sums):
    lands = [_landing((N_CHIPS - 1,) + p.shape[1:], BF16) for p in chip_sums]
    return exchange_start("reduce_ici_start", chip_sums, lands, 3 * len(chip_sums), _reduce_ici_plan)


def reduce_swap_start(mine):
    return exchange_start("reduce_swap_start", [], mine, len(mine), _reduce_swap_plan)


def _by_shape(fn, *lists):
    groups, out = {}, [None] * len(lists[0])
    for k, a in enumerate(lists[0]):
        groups.setdefault(a.shape, []).append(k)
    for idx in groups.values():
        for k, r in zip(idx, fn(*[[l[k] for k in idx] for l in lists])):
            out[k] = r
    return out


def add_halves(ds, recvs, half, name):
    n = len(ds)
    nch, r, c = ds[0].shape
    h = r // 2

    def body(half_ref, *refs):
        for k in range(n):
            refs[2 * n + k][...] = (refs[k][...].astype(F32) + refs[n + k][...].astype(F32)).astype(BF16)

    mine = pl.BlockSpec((1, h, c), lambda j, hf: (j, hf[0], 0))
    whole = pl.BlockSpec((1, h, c), lambda j, hf: (j, 0, 0))
    return pl.pallas_call(
        body, name=name,
        grid_spec=pltpu.PrefetchScalarGridSpec(
            num_scalar_prefetch=1, grid=(nch,), in_specs=[mine] * n + [whole] * n, out_specs=[whole] * n),
        out_shape=[jax.ShapeDtypeStruct(rv.shape, BF16) for rv in recvs],
        compiler_params=_params(("parallel",)),
    )(half, *ds, *recvs)


def add_chips(ps, recvs, chip, name):
    n = len(ps)
    _, r, c = ps[0].shape

    def body(chip_ref, *refs):
        for k in range(n):
            acc = refs[k][0].astype(F32)
            for f in range(N_CHIPS - 1):
                acc = acc + refs[n + k][f].astype(F32)
            refs[2 * n + k][...] = acc

    return pl.pallas_call(
        body, name=name,
        grid_spec=pltpu.PrefetchScalarGridSpec(
            num_scalar_prefetch=1, grid=(1,),
            in_specs=[pl.BlockSpec((1, r, c), lambda i, ch: (ch[0], 0, 0))] * n +
                     [pl.BlockSpec((N_CHIPS - 1, r, c), lambda i, ch: (0, 0, 0))] * n,
            out_specs=[pl.BlockSpec((r, c), lambda i, ch: (ch[1], 0))] * n),
        out_shape=[jax.ShapeDtypeStruct((2 * r, c), F32)] * n,
        compiler_params=_params(("arbitrary",)),
    )(chip, *ps, *recvs)


def adamw_layers(w, gs, m, v, name):
    b, r, c = w.shape
    tr = max([t for t in range(8, min(r, 512) + 1, 8) if r % t == 0], default=r)

    def body(w_ref, m_ref, v_ref, *rest):
        g_refs, (g_ref, d_ref, nm_ref, nv_ref) = rest[:b], rest[b:]
        layer = pl.program_id(0)
        gv = g_refs[0][...]
        for l in range(1, b):
            gv = jnp.where(layer == l, g_refs[l][...], gv)
        nm = ADAM_B1 * m_ref[0] + (1.0 - ADAM_B1) * gv
        nv = ADAM_B2 * v_ref[0] + (1.0 - ADAM_B2) * (gv * gv)
        m_hat = nm / (1.0 - ADAM_B1 ** ADAM_STEP)
        v_hat = nv / (1.0 - ADAM_B2 ** ADAM_STEP)
        g_ref[0] = gv
        d_ref[0] = -ADAM_LR * (m_hat / (jnp.sqrt(v_hat) + ADAM_EPS) + ADAM_WD * w_ref[0])
        nm_ref[0] = nm
        nv_ref[0] = nv

    nr, tc = r // tr, (c if tr < r else _tile(c, 256))
    steps = nr * (c // tc)
    blk = pl.BlockSpec((1, tr, tc), lambda i, j: (i, j % nr, j // nr))
    g_specs = [pl.BlockSpec((tr, tc), lambda i, j, l=l: (jnp.where(i == l, j % nr, jnp.where(i < l, 0, nr - 1)),
                                                         jnp.where(i == l, j // nr, jnp.where(i < l, 0, c // tc - 1))))
               for l in range(b)]
    return pl.pallas_call(
        body, name=name, grid=(b, steps),
        in_specs=[blk] * 3 + g_specs, out_specs=[blk] * 4,
        out_shape=[jax.ShapeDtypeStruct(w.shape, F32)] * 4,
        compiler_params=_params(("arbitrary", "arbitrary")),
    )(w, m, v, *gs)


def _all_devices_plan(pos, srcs, lands):
    return [(srcs[0], lands[0].at[f], ((f >> 2) & 1, (f >> 1) & 1, f & 1)) for f in range(1, 8)]


def sum_devices(v, gathered, me, name):
    r, c = v.shape

    def body(me_ref, v_ref, g_ref, o_ref):
        own = v_ref[...]
        acc = None
        for d in range(8):
            slot = lax.bitwise_xor(me_ref[0], d)
            term = jnp.where(slot == 0, own, g_ref[slot])
            acc = term if acc is None else acc + term
        o_ref[...] = acc

    return pl.pallas_call(
        body, name=name,
        grid_spec=pltpu.PrefetchScalarGridSpec(
            num_scalar_prefetch=1, grid=(1,),
            in_specs=[pl.BlockSpec((r, c), lambda i, m: (0, 0)), pl.BlockSpec((8, r, c), lambda i, m: (0, 0, 0))],
            out_specs=pl.BlockSpec((r, c), lambda i, m: (0, 0))),
        out_shape=jax.ShapeDtypeStruct((r, c), F32),
        compiler_params=_params(("arbitrary",)),
    )(me, v, gathered)


_IN_SEGMENTS = ((0, 1544, 128), (128, 1800, 128), (256, 2056, 128), (384, 1672, 128), (512, 1928, 128), (640, 2184, 128),
                (OFF_Z, 0, SSD_WIDTH), (OFF_DT, 1536, SSD_HEADS), (OFF_XBC, 512, CONV_DIM), (OFF_P, 2312, POOL_WIDTH))


def _in_column_map():
    m = np.full((D_INP,), -1, np.int64)
    for at, orig, n in _IN_SEGMENTS:
        cols = np.arange(orig, orig + n)
        m[at:at + n] = (cols // COLS_IN) * SHARD_IN + cols % COLS_IN
    return m


def take_rows(a, idx, name):
    dep, r_in, c = a.shape
    blk = 2 * LANES if len(idx) % (2 * LANES) == 0 and r_in % (2 * LANES) == 0 else LANES
    n_out, n_in = len(idx) // blk, r_in // blk
    assert len(idx) % blk == 0 and r_in % blk == 0
    sources = [sorted({int(v) // blk for v in idx[blk * i:blk * (i + 1)] if v >= 0}) for i in range(n_out)]
    width = max(len(s) for s in sources)
    table = np.zeros((n_out, width), np.int32)
    for i, s in enumerate(sources):
        spare = [b for b in range(n_in) if b not in s][:width - len(s)]
        table[i] = s + spare

    def body(tbl_ref, idx_ref, *refs):
        in_refs, o_ref = refs[:width], refs[width]
        i = pl.program_id(1)
        src = idx_ref[...]
        acc = jnp.zeros((blk, c), F32)
        for k in range(width):
            pick = (src == tbl_ref[i, k] * blk + _iota2((blk, blk), 1)).astype(BF16)
            acc = acc + _nn(pick, in_refs[k][0])
        o_ref[0] = acc.astype(BF16)

    return pl.pallas_call(
        body, name=name,
        grid_spec=pltpu.PrefetchScalarGridSpec(
            num_scalar_prefetch=1, grid=(dep, n_out),
            in_specs=[pl.BlockSpec((blk, 1), lambda l, i, t: (i, 0))] +
                     [pl.BlockSpec((1, blk, c), lambda l, i, t, k=k: (l, t[i, k], 0)) for k in range(width)],
            out_specs=pl.BlockSpec((1, blk, c), lambda l, i, t: (l, i, 0))),
        out_shape=jax.ShapeDtypeStruct((dep, len(idx), c), BF16),
        compiler_params=_params(("parallel", "parallel")),
    )(jnp.asarray(table), jnp.asarray(np.asarray(idx, np.int32).reshape(-1, 1)), *([a] * width))


def _in_weight_layout(staged):
    return take_rows(staged, _in_column_map(), "w_in_layout")


def _in_gradient_layout(dwt):
    fwd = _in_column_map()
    inv = np.full((N_CHIPS * SHARD_IN,), -1, np.int64)
    inv[fwd[fwd >= 0]] = np.nonzero(fwd >= 0)[0]
    return take_rows(dwt, inv, "dw_in_layout")


SMALL_NAMES = ("norm1_w", "conv_w", "conv_b", "dt_bias", "a_log", "d_skip", "ssd_norm_w", "pool_w", "pool_b",
               "pool_scale", "norm2_w", "final_norm_w")
SMALL_ROWS = 160


def _small_rows(shape):
    return -(-int(np.prod(shape)) // (8 * D_MODEL)) * 8


def _pack_small(parts):
    rows = []
    for p in parts:
        flat = p.reshape(-1)
        rows.append(jnp.pad(flat, (0, _small_rows(p.shape) * D_MODEL - flat.shape[0])).reshape(-1, D_MODEL))
    used = sum(r.shape[0] for r in rows)
    return jnp.concatenate(rows + [jnp.zeros((SMALL_ROWS - used, D_MODEL), F32)], axis=0)


def _unpack_small(packed, shapes):
    out, at = [], 0
    for shp in shapes:
        n, r = int(np.prod(shp)), _small_rows(shp)
        out.append(packed[at:at + r].reshape(-1)[:n].reshape(shp))
        at += r
    return out


def kernel(x, norm1_w, w_in, conv_w, conv_b, dt_bias, a_log, d_skip, ssd_norm_w, pool_w, pool_b, pool_scale, w_out, norm2_w, w_gate, w_up, w_down, final_norm_w, loss_target, m_norm1_w, m_w_in, m_conv_w, m_conv_b, m_dt_bias, m_a_log, m_d_skip, m_ssd_norm_w, m_pool_w, m_pool_b, m_pool_scale, m_w_out, m_norm2_w, m_w_gate, m_w_up, m_w_down, m_final_norm_w, v_norm1_w, v_w_in, v_conv_w, v_conv_b, v_dt_bias, v_a_log, v_d_skip, v_ssd_norm_w, v_pool_w, v_pool_b, v_pool_scale, v_w_out, v_norm2_w, v_w_gate, v_w_up, v_w_down, v_final_norm_w):
    px, py, pc = _position()
    chip = 2 * px + py
    chip_arr = jnp.reshape(chip, (1,)).astype(jnp.int32)
    half_arr = jnp.reshape(pc, (1,)).astype(jnp.int32)

    def layer_shards(l):
        w_in_t = jnp.pad(jnp.swapaxes(w_in[l], 0, 1).astype(BF16), ((0, SHARD_IN - COLS_IN), (0, 0)))
        return [w_in_t, w_out[l].astype(BF16), jnp.swapaxes(w_gate[l], 0, 1).astype(BF16),
                jnp.swapaxes(w_up[l], 0, 1).astype(BF16), w_down[l].astype(BF16)]

    shards0 = layer_shards(0)
    head = gather_ici_start(shards0[:1])
    over_ici = {}

    def pass_on(l, after):
        own, arrived = exchange_wait("gather_ici_wait", over_ici[l], after, _gather_ici_plan)
        swap = gather_d2d_start(own, arrived)
        tokens = [swap[4]]
        if l + 1 < DEPTH:
            over_ici[l + 1] = gather_ici_start(layer_shards(l + 1), after=swap[4])
            tokens.append(over_ici[l + 1][4])
        return swap, tokens

    def weights_of(swap, after):
        _, (w_in_st, w_out_l, w_gate_t, w_up_t, w_down_l) = exchange_wait("gather_d2d_wait", swap, after, _gather_d2d_plan)
        return _in_weight_layout(w_in_st[None])[0], w_out_l, w_gate_t, w_up_t, w_down_l

    pad_heads = lambda v: jnp.pad(v, ((0, 0), (0, LANES - SSD_HEADS)))[:, None, :]
    dtb, alog = pad_heads(dt_bias), pad_heads(a_log)
    dskip_x = jnp.repeat(d_skip, HEAD_DIM, axis=1)[:, None, :]
    eye = jnp.eye(len(POOL_WINDOWS), dtype=F32)
    wbd = (pool_w[:, :, :, None, :] * eye[None, :, None, :, None]).reshape(DEPTH, POOL_WIDTH, POOL_WIDTH).astype(BF16)
    pool_b2 = pool_b.reshape(DEPTH, 1, POOL_WIDTH)
    cw_cols = lax.dynamic_update_slice(jnp.zeros((DEPTH, CONV_WIDTH, CONV_DIM), F32), conv_w,
                                       (0, 0, chip * (CONV_DIM // N_CHIPS)))
    cw_cols = jnp.where(pc == 0, cw_cols, 0.0)
    cw_rows = (DEPTH * CONV_WIDTH * CONV_DIM) // D_MODEL
    me_arr = jnp.reshape(4 * px + 2 * py + pc, (1,)).astype(jnp.int32)
    cw_cols = jnp.pad(cw_cols.reshape(cw_rows, D_MODEL), ((0, 8), (0, 0)))
    cw_start = exchange_start("gather_conv_w_start", [cw_cols], [_landing((8,) + cw_cols.shape, F32)], 7,
                              _all_devices_plan, after=head[4])

    h = x[0]
    saved, weights = [], []
    own, arrived = exchange_wait("gather_ici_wait", head, [head[4], cw_start[4]] + shards0[1:], _gather_ici_plan)
    head = gather_d2d_start(own, arrived)
    over_ici[0] = gather_ici_start(shards0[1:], after=head[4])
    w_in_f = _in_weight_layout(exchange_wait("gather_d2d_wait", head, [head[4], over_ici[0][4]], _gather_d2d_plan)[1][0][None])[0]
    for l in range(DEPTH):
        if l > 0:
            w_in_f, w_out_f, w_gate_t, w_up_t, w_down_f = weights[l]
        proj = rms_matmul(h, norm1_w[l][None], w_in_f, "in_proj")
        if l == 0:
            (cw_own,), (cw_all,) = exchange_wait("gather_conv_w_wait", cw_start, proj, _all_devices_plan)
            conv_w_f = sum_devices(cw_own, cw_all, me_arr, "gather_conv_w_sum")[:cw_rows].reshape(DEPTH, CONV_WIDTH, CONV_DIM)
            cw8 = jnp.pad(conv_w_f, ((0, 0), (0, 8 - CONV_WIDTH), (0, 0)))
        xc = conv_forward(proj, cw8[l], conv_b[l][None], "conv_fwd")
        y_all, ycore, states = ssd_forward(proj, xc, dtb[l], alog[l], dskip_x[l], ssd_norm_w[l][None], "ssd_fwd")
        if l == 0:
            swap, tokens = pass_on(0, y_all)
            y_all, o_sb = sb_forward(proj, y_all, "sb_fwd", after=tokens)
            tokens = None
        else:
            y_all, o_sb = sb_forward(proj, y_all, "sb_fwd")
            swap, tokens = pass_on(l + 1, o_sb) if l + 1 < DEPTH else (None, None)
        y_all = pool_forward(proj, wbd[l], pool_b2[l], pool_scale[l][None], y_all, "pool_fwd", after=tokens)
        if l == 0:
            w_out_f, w_gate_t, w_up_t, w_down_f = exchange_wait("gather_d2d_wait", swap, y_all, _gather_d2d_plan)[1]
            weights.append((w_in_f, w_out_f, w_gate_t, w_up_t, w_down_f))
        x1 = matmul_residual(y_all, w_out_f, h, "out_proj")
        x2, g, u = ffn_forward(x1, norm2_w[l][None], w_gate_t, w_up_t, w_down_f, "ffn_fwd")
        if l == 0:
            swap, tokens = pass_on(1, x2)
            weights.append(weights_of(swap, tokens))
        elif swap is not None:
            weights.append(weights_of(swap, x2))
        saved.append((h, proj, xc, ycore, states, o_sb, y_all, x1, g, u))
        h = x2

    loss_part, dx, dxb, d_final = loss_head(h, final_norm_w[None], loss_target[0], "loss_head")
    loss = lax.psum(loss_part[0, 0], ("x", "y", "c"))

    small = {n: [None] * DEPTH for n in SMALL_NAMES if n != "final_norm_w"}
    chip_half = jnp.concatenate([chip_arr, half_arr])
    reduced = {}
    d2d = ici = early = None

    def add_cores(d2d, after):
        mine, theirs = exchange_wait("reduce_d2d_wait", d2d[1], after, _reduce_d2d_plan)
        return d2d[0], reduce_ici_start(_by_shape(lambda ds, ts: add_halves(ds, ts, half_arr, "reduce_add_halves"), mine, theirs))

    def add_all(ici, after):
        sums, theirs = exchange_wait("reduce_ici_wait", ici[1], after, _reduce_ici_plan)
        return ici[0], reduce_swap_start(_by_shape(lambda ps, ts: add_chips(ps, ts, chip_half, "reduce_add_chips"), sums, theirs))

    def finish(swap, after):
        reduced[swap[0]] = exchange_wait("reduce_swap_wait", swap[1], after, _reduce_swap_plan)[1]

    swaps = []
    for l in reversed(range(DEPTH)):
        xin, proj, xc, ycore, states, o_sb, y_all, x1, g, u = saved[l]
        w_in_f, w_out_f, w_gate_t, w_up_t, w_down_f = weights[l]
        dg, du, act = ffn_backward_act(dxb, g, u, w_down_f, "ffn_bwd_act", after=None if d2d is None else d2d[1][4])
        dx1, dx1b, h2, dn2 = rms_backward([dg, du], [w_gate_t, w_up_t], x1, norm2_w[l][None], dx, "ffn_bwd_norm", 512)
        if d2d is not None:
            ici = add_cores(d2d, dx1b)
        dyall = matmul_nt(dx1b, w_out_f, "out_proj_bwd", after=None if ici is None else ici[1][4])
        dw_down = matmul_tn(act, dxb, "dw_down")
        dw_gate = matmul_tn(dg, h2, "dw_gate")
        dw_up = matmul_tn(du, h2, "dw_up")
        dw_out = matmul_tn(y_all, dx1b, "dw_out")
        late = [dw.reshape(N_CHIPS, r, D_MODEL) for dw, r in zip((dw_out, dw_gate, dw_up, dw_down), SHARD_ROWS[1:])]
        if l == 0:
            early = ("0 late", reduce_d2d_start(late))
        dxc, dproj, dsn, ddsk, ddtb, dalog = ssd_backward(proj, xc, ycore, dyall, states, dtb[l], alog[l],
                                                          dskip_x[l], ssd_norm_w[l][None], "ssd_bwd")
        dproj, dcw, dcb = conv_backward(proj, dxc, cw8[l], conv_b[l][None], dproj, "conv_bwd",
                                        after=None if early is None else early[1][4])
        if early is not None:
            early = add_cores(early, dproj)
        dproj = sb_backward(proj, o_sb, dyall, dproj, "sb_bwd", after=None if early is None else early[1][4])
        dproj, dwbd, dpb, dps = pool_backward(proj, dyall, wbd[l], pool_b2[l], pool_scale[l][None], dproj, "pool_bwd")
        if ici is not None:
            swaps.append(add_all(ici, dproj))
            ici = None
        dx, dxb, h1, dn1 = rms_backward([dproj], [w_in_f], xin, norm1_w[l][None], dx1, "in_proj_bwd", 512,
                                        after=swaps[-1][1][4] if swaps else None)
        dw_in = _in_gradient_layout(matmul_tn(dproj, h1, "dw_in")[None])[0].reshape(N_CHIPS, SHARD_IN, D_MODEL)
        d2d = (l, reduce_d2d_start([dw_in] if l == 0 else [dw_in] + late))
        small["norm1_w"][l] = dn1[0]
        small["conv_w"][l] = dcw[:CONV_WIDTH]
        small["conv_b"][l] = dcb[0]
        small["dt_bias"][l] = ddtb[0, :SSD_HEADS]
        small["a_log"][l] = dalog[0, :SSD_HEADS]
        small["d_skip"][l] = ddsk.reshape(SSD_HEADS, HEAD_DIM).sum(axis=1)
        small["ssd_norm_w"][l] = dsn[0]
        small["pool_w"][l] = jnp.stack([dwbd[64 * k:64 * k + 64, 64 * k:64 * k + 64] for k in range(len(POOL_WINDOWS))])
        small["pool_b"][l] = dpb.reshape(len(POOL_WINDOWS), -1)
        small["pool_scale"][l] = dps[0]
        small["norm2_w"][l] = dn2[0]
    grad_x = dx[None]

    ici = add_cores(d2d, d2d[1][4])
    small_parts = [d_final if n == "final_norm_w" else jnp.stack(small[n]) for n in SMALL_NAMES]
    small_start = exchange_start("reduce_small_start", [_pack_small(small_parts)],
                                 [_landing((8, SMALL_ROWS, D_MODEL), F32)], 7, _all_devices_plan, after=ici[1][4])
    swaps.append(add_all(early, small_start[4]))
    swaps.append(add_all(ici, swaps[-1][1][4]))
    for swap in swaps:
        finish(swap, swaps[-1][1][4])
    (small_own,), (small_all,) = exchange_wait("reduce_small_wait", small_start, reduced[0][0], _all_devices_plan)
    small_sum = sum_devices(small_own, small_all, me_arr, "reduce_small_sum")
    reduced[0] = reduced[0] + reduced["0 late"]
    g_big = {n: [reduced[l][k] for l in range(DEPTH)] for k, n in enumerate(("w_in", "w_out", "w_gate", "w_up", "w_down"))}
    g_big["w_in"] = [gl[:COLS_IN] for gl in g_big["w_in"]]
    transposed = ("w_in", "w_gate", "w_up")

    g_small = dict(zip(SMALL_NAMES, _unpack_small(small_sum, [p.shape for p in small_parts])))
    g_small["final_norm_w"] = g_small["final_norm_w"].reshape(final_norm_w.shape)
    g_small["conv_w"] = lax.dynamic_slice_in_dim(g_small["conv_w"], chip * (CONV_DIM // N_CHIPS), CONV_DIM // N_CHIPS, axis=2)

    given = dict(norm1_w=(norm1_w, m_norm1_w, v_norm1_w), w_in=(w_in, m_w_in, v_w_in), conv_w=(conv_w, m_conv_w, v_conv_w),
                 conv_b=(conv_b, m_conv_b, v_conv_b), dt_bias=(dt_bias, m_dt_bias, v_dt_bias), a_log=(a_log, m_a_log, v_a_log),
                 d_skip=(d_skip, m_d_skip, v_d_skip), ssd_norm_w=(ssd_norm_w, m_ssd_norm_w, v_ssd_norm_w),
                 pool_w=(pool_w, m_pool_w, v_pool_w), pool_b=(pool_b, m_pool_b, v_pool_b),
                 pool_scale=(pool_scale, m_pool_scale, v_pool_scale), w_out=(w_out, m_w_out, v_w_out),
                 norm2_w=(norm2_w, m_norm2_w, v_norm2_w), w_gate=(w_gate, m_w_gate, v_w_gate), w_up=(w_up, m_w_up, v_w_up),
                 w_down=(w_down, m_w_down, v_w_down), final_norm_w=(final_norm_w, m_final_norm_w, v_final_norm_w))
    order = ("norm1_w", "w_in", "conv_w", "conv_b", "dt_bias", "a_log", "d_skip", "ssd_norm_w", "pool_w", "pool_b",
             "pool_scale", "w_out", "norm2_w", "w_gate", "w_up", "w_down", "final_norm_w")
    grads = dict(g_small)
    results = {}
    for n in ("w_in", "w_out", "w_gate", "w_up", "w_down"):
        w, m, v = given[n]
        if n in transposed:
            out = adamw_layers(jnp.swapaxes(w, 1, 2), g_big[n], jnp.swapaxes(m, 1, 2), jnp.swapaxes(v, 1, 2), "adamw_" + n)
            out = [jnp.swapaxes(o, 1, 2) for o in out]
        else:
            out = adamw_layers(w, g_big[n], m, v, "adamw_" + n)
        grads[n], results[n] = out[0], tuple(out[1:])
    small_shapes = [given[n][0].shape for n in SMALL_NAMES]
    packed = [_pack_small([given[n][k] for n in SMALL_NAMES])[None] for k in range(3)]
    packed_g = _pack_small([grads[n] for n in SMALL_NAMES])[None]
    small_out = adamw(packed[0], packed_g, packed[1], packed[2], "adamw_small")
    small_out = [_unpack_small(o[0], small_shapes) for o in small_out]
    for i, n in enumerate(SMALL_NAMES):
        results[n] = tuple(small_out[k][i] for k in range(3))

    return (loss, grad_x, *[grads[n] for n in order], *[results[n][0] for n in order],
            *[results[n][1] for n in order], *[results[n][2] for n in order])
```
